```python
import jax, jax.numpy as jnp
from jax import lax
import numpy as np

D_MODEL = 1024
BATCH = 8
SEQ = 8192
DEPTH = 1

D_MIX = D_MODEL
D_A = D_MIX // 2
D_B = D_MIX - D_A
N_HEADS_A = 8
HEAD_DIM_A = D_A // N_HEADS_A
N_GROUPS_B = 8
GROUP_DIM_B = D_B // N_GROUPS_B
CHUNK = 128
CONV_WIDTH = 31
D_IN = 2 * D_A + 2 * D_B
D_FF = -(-8 * D_MODEL // (3 * 256)) * 256
N_MOD = 6
EPS = 1e-6

kernel_name = "hybrid_sgu_conformer_conv_adaln_block"


def rmsnorm(x, g):
    xf = x.astype(jnp.float32)
    y = xf * lax.rsqrt(jnp.mean(xf * xf, axis=-1, keepdims=True) + EPS)
    return (y * g.astype(jnp.float32)).astype(x.dtype)


def layernorm(x, g, b):
    xf = x.astype(jnp.float32)
    mu = jnp.mean(xf, axis=-1, keepdims=True)
    xc = xf - mu
    var = jnp.mean(xc * xc, axis=-1, keepdims=True)
    y = xc * lax.rsqrt(var + EPS) * g.astype(jnp.float32) + b.astype(jnp.float32)
    return y.astype(x.dtype)


def modulate(h, shift, scale):
    return h * (1 + scale[:, None, :]) + shift[:, None, :]


def spatial_gating_mixer(u, v, ln_g, ln_b, w_s, b_s):
    B, S, _ = v.shape
    v = layernorm(v, ln_g, ln_b)
    v = v.reshape(B, S // CHUNK, CHUNK, N_HEADS_A, HEAD_DIM_A)
    causal = jnp.tril(jnp.ones((CHUNK, CHUNK), dtype=bool))
    w = jnp.where(causal[None], w_s, jnp.zeros_like(w_s)).astype(v.dtype)
    mixed = jnp.einsum('hts,bcshd->bcthd', w, v) + b_s.T.astype(v.dtype)[None, None, :, :, None]
    return u * mixed.reshape(B, S, D_A)


def conformer_conv_mixer(val, gate, conv_w, conv_b, gn_g, gn_b):
    B, S, _ = val.shape
    y = val * jax.nn.sigmoid(gate)
    y = lax.conv_general_dilated(
        y, conv_w[:, None, :].astype(y.dtype), window_strides=(1,),
        padding=[(CONV_WIDTH - 1, 0)],
        dimension_numbers=('NWC', 'WIO', 'NWC'),
        feature_group_count=D_B) + conv_b.astype(y.dtype)
    y = y.reshape(B, S, N_GROUPS_B, GROUP_DIM_B)
    y = layernorm(y, gn_g.reshape(N_GROUPS_B, GROUP_DIM_B), gn_b.reshape(N_GROUPS_B, GROUP_DIM_B))
    return jax.nn.silu(y.reshape(B, S, D_B))


def _fwd_setup_inputs(seed: int = 0) -> dict:
    key = jax.random.key(seed)
    ks = jax.random.split(key, 24)
    f32 = jnp.float32
    nrm = lambda k, shape, s: jax.random.normal(k, shape, f32) * s
    L = DEPTH
    return {
        "x": jax.random.normal(ks[0], (BATCH, SEQ, D_MODEL), f32),
        "c": jax.random.normal(ks[1], (BATCH, D_MODEL), f32),
        "ada_w": nrm(ks[2], (L, D_MODEL, N_MOD * D_MODEL), D_MODEL ** -0.5),
        "ada_b": nrm(ks[3], (L, N_MOD * D_MODEL), 0.02),
        "norm1_g": 1.0 + nrm(ks[4], (L, D_MODEL), 0.05),
        "w_in": nrm(ks[5], (L, D_MODEL, D_IN), D_MODEL ** -0.5),
        "b_in": nrm(ks[6], (L, D_IN), 0.02),
        "a_ln_g": 1.0 + nrm(ks[7], (L, D_A), 0.05),
        "a_ln_b": nrm(ks[8], (L, D_A), 0.02),
        "a_spatial_w": nrm(ks[9], (L, N_HEADS_A, CHUNK, CHUNK), CHUNK ** -0.5),
        "a_spatial_b": 1.0 + nrm(ks[10], (L, N_HEADS_A, CHUNK), 0.1),
        "b_conv_w": nrm(ks[11], (L, CONV_WIDTH, D_B), CONV_WIDTH ** -0.5),
        "b_conv_b": nrm(ks[12], (L, D_B), 0.02),
        "b_gn_g": 1.0 + nrm(ks[13], (L, D_B), 0.05),
        "b_gn_b": nrm(ks[14], (L, D_B), 0.02),
        "out_norm_a_g": 1.0 + nrm(ks[15], (L, D_A), 0.05),
        "out_norm_b_g": 1.0 + nrm(ks[16], (L, D_B), 0.05),
        "w_out": nrm(ks[17], (L, D_MIX, D_MODEL), D_MIX ** -0.5),
        "norm2_g": 1.0 + nrm(ks[18], (L, D_MODEL), 0.05),
        "w_ffn_in": nrm(ks[19], (L, D_MODEL, 2 * D_FF), D_MODEL ** -0.5),
        "w_ffn_out": nrm(ks[20], (L, D_FF, D_MODEL), D_FF ** -0.5),
        "ada_f_w": nrm(ks[21], (D_MODEL, 2 * D_MODEL), D_MODEL ** -0.5),
        "ada_f_b": nrm(ks[22], (2 * D_MODEL,), 0.02),
        "norm_f_g": 1.0 + nrm(ks[23], (D_MODEL,), 0.05),
    }


def _fwd_reference(x, c, ada_w, ada_b, norm1_g, w_in, b_in, a_ln_g, a_ln_b, a_spatial_w,
              a_spatial_b, b_conv_w, b_conv_b, b_gn_g, b_gn_b, out_norm_a_g, out_norm_b_g,
              w_out, norm2_g, w_ffn_in, w_ffn_out, ada_f_w, ada_f_b, norm_f_g):
    c_act = jax.nn.silu(c)
    for i in range(DEPTH):
        cond = c_act @ ada_w[i] + ada_b[i]
        shift1, scale1, gate1, shift2, scale2, gate2 = jnp.split(cond, N_MOD, axis=-1)

        h = modulate(rmsnorm(x, norm1_g[i]), shift1, scale1)
        z = h @ w_in[i] + b_in[i]
        u, v, val, gate = jnp.split(z, [D_A, 2 * D_A, 2 * D_A + D_B], axis=-1)
        y_a = spatial_gating_mixer(jax.nn.gelu(u, approximate=False),
                                   jax.nn.gelu(v, approximate=False),
                                   a_ln_g[i], a_ln_b[i], a_spatial_w[i], a_spatial_b[i])
        y_b = conformer_conv_mixer(val, gate, b_conv_w[i], b_conv_b[i], b_gn_g[i], b_gn_b[i])
        y = jnp.concatenate([rmsnorm(y_a, out_norm_a_g[i]), rmsnorm(y_b, out_norm_b_g[i])], axis=-1)
        x = x + gate1[:, None, :] * (y @ w_out[i])

        h2 = modulate(rmsnorm(x, norm2_g[i]), shift2, scale2)
        g_ff, up_ff = jnp.split(h2 @ w_ffn_in[i], 2, axis=-1)
        x = x + gate2[:, None, :] * ((jax.nn.silu(g_ff) * up_ff) @ w_ffn_out[i])

    shift_f, scale_f = jnp.split(c_act @ ada_f_w + ada_f_b, 2, axis=-1)
    return modulate(rmsnorm(x, norm_f_g), shift_f, scale_f)


import jax as _jax
import jax.numpy as _jnp

TWIN_FORMAT = 'train_step'
FWD_PARAMS = ['x', 'c', 'ada_w', 'ada_b', 'norm1_g', 'w_in', 'b_in', 'a_ln_g', 'a_ln_b', 'a_spatial_w', 'a_spatial_b', 'b_conv_w', 'b_conv_b', 'b_gn_g', 'b_gn_b', 'out_norm_a_g', 'out_norm_b_g', 'w_out', 'norm2_g', 'w_ffn_in', 'w_ffn_out', 'ada_f_w', 'ada_f_b', 'norm_f_g']
TWIN_WEIGHTS = ['ada_w', 'ada_b', 'norm1_g', 'w_in', 'b_in', 'a_ln_g', 'a_ln_b', 'a_spatial_w', 'a_spatial_b', 'b_conv_w', 'b_conv_b', 'b_gn_g', 'b_gn_b', 'out_norm_a_g', 'out_norm_b_g', 'w_out', 'norm2_g', 'w_ffn_in', 'w_ffn_out', 'ada_f_w', 'ada_f_b', 'norm_f_g']
TWIN_DIFF_INPUT = 'x'
TWIN_INPUTS = ['x', 'c', 'ada_w', 'ada_b', 'norm1_g', 'w_in', 'b_in', 'a_ln_g', 'a_ln_b', 'a_spatial_w', 'a_spatial_b', 'b_conv_w', 'b_conv_b', 'b_gn_g', 'b_gn_b', 'out_norm_a_g', 'out_norm_b_g', 'w_out', 'norm2_g', 'w_ffn_in', 'w_ffn_out', 'ada_f_w', 'ada_f_b', 'norm_f_g', 'loss_target', 'm_ada_w', 'm_ada_b', 'm_norm1_g', 'm_w_in', 'm_b_in', 'm_a_ln_g', 'm_a_ln_b', 'm_a_spatial_w', 'm_a_spatial_b', 'm_b_conv_w', 'm_b_conv_b', 'm_b_gn_g', 'm_b_gn_b', 'm_out_norm_a_g', 'm_out_norm_b_g', 'm_w_out', 'm_norm2_g', 'm_w_ffn_in', 'm_w_ffn_out', 'm_ada_f_w', 'm_ada_f_b', 'm_norm_f_g', 'v_ada_w', 'v_ada_b', 'v_norm1_g', 'v_w_in', 'v_b_in', 'v_a_ln_g', 'v_a_ln_b', 'v_a_spatial_w', 'v_a_spatial_b', 'v_b_conv_w', 'v_b_conv_b', 'v_b_gn_g', 'v_b_gn_b', 'v_out_norm_a_g', 'v_out_norm_b_g', 'v_w_out', 'v_norm2_g', 'v_w_ffn_in', 'v_w_ffn_out', 'v_ada_f_w', 'v_ada_f_b', 'v_norm_f_g']
TWIN_OUTPUTS = ['loss', 'grad_x', 'grad_ada_w', 'grad_ada_b', 'grad_norm1_g', 'grad_w_in', 'grad_b_in', 'grad_a_ln_g', 'grad_a_ln_b', 'grad_a_spatial_w', 'grad_a_spatial_b', 'grad_b_conv_w', 'grad_b_conv_b', 'grad_b_gn_g', 'grad_b_gn_b', 'grad_out_norm_a_g', 'grad_out_norm_b_g', 'grad_w_out', 'grad_norm2_g', 'grad_w_ffn_in', 'grad_w_ffn_out', 'grad_ada_f_w', 'grad_ada_f_b', 'grad_norm_f_g', 'delta_ada_w', 'delta_ada_b', 'delta_norm1_g', 'delta_w_in', 'delta_b_in', 'delta_a_ln_g', 'delta_a_ln_b', 'delta_a_spatial_w', 'delta_a_spatial_b', 'delta_b_conv_w', 'delta_b_conv_b', 'delta_b_gn_g', 'delta_b_gn_b', 'delta_out_norm_a_g', 'delta_out_norm_b_g', 'delta_w_out', 'delta_norm2_g', 'delta_w_ffn_in', 'delta_w_ffn_out', 'delta_ada_f_w', 'delta_ada_f_b', 'delta_norm_f_g', 'new_m_ada_w', 'new_m_ada_b', 'new_m_norm1_g', 'new_m_w_in', 'new_m_b_in', 'new_m_a_ln_g', 'new_m_a_ln_b', 'new_m_a_spatial_w', 'new_m_a_spatial_b', 'new_m_b_conv_w', 'new_m_b_conv_b', 'new_m_b_gn_g', 'new_m_b_gn_b', 'new_m_out_norm_a_g', 'new_m_out_norm_b_g', 'new_m_w_out', 'new_m_norm2_g', 'new_m_w_ffn_in', 'new_m_w_ffn_out', 'new_m_ada_f_w', 'new_m_ada_f_b', 'new_m_norm_f_g', 'new_v_ada_w', 'new_v_ada_b', 'new_v_norm1_g', 'new_v_w_in', 'new_v_b_in', 'new_v_a_ln_g', 'new_v_a_ln_b', 'new_v_a_spatial_w', 'new_v_a_spatial_b', 'new_v_b_conv_w', 'new_v_b_conv_b', 'new_v_b_gn_g', 'new_v_b_gn_b', 'new_v_out_norm_a_g', 'new_v_out_norm_b_g', 'new_v_w_out', 'new_v_norm2_g', 'new_v_w_ffn_in', 'new_v_w_ffn_out', 'new_v_ada_f_w', 'new_v_ada_f_b', 'new_v_norm_f_g']
TWIN_LEAF_KINDS = {'loss': 'loss', 'grad_x': 'grad_x', 'grad_ada_w': 'grad_w', 'grad_ada_b': 'grad_w', 'grad_norm1_g': 'grad_w', 'grad_w_in': 'grad_w', 'grad_b_in': 'grad_w', 'grad_a_ln_g': 'grad_w', 'grad_a_ln_b': 'grad_w', 'grad_a_spatial_w': 'grad_w', 'grad_a_spatial_b': 'grad_w', 'grad_b_conv_w': 'grad_w', 'grad_b_conv_b': 'grad_w', 'grad_b_gn_g': 'grad_w', 'grad_b_gn_b': 'grad_w', 'grad_out_norm_a_g': 'grad_w', 'grad_out_norm_b_g': 'grad_w', 'grad_w_out': 'grad_w', 'grad_norm2_g': 'grad_w', 'grad_w_ffn_in': 'grad_w', 'grad_w_ffn_out': 'grad_w', 'grad_ada_f_w': 'grad_w', 'grad_ada_f_b': 'grad_w', 'grad_norm_f_g': 'grad_w', 'delta_ada_w': 'delta_w', 'delta_ada_b': 'delta_w', 'delta_norm1_g': 'delta_w', 'delta_w_in': 'delta_w', 'delta_b_in': 'delta_w', 'delta_a_ln_g': 'delta_w', 'delta_a_ln_b': 'delta_w', 'delta_a_spatial_w': 'delta_w', 'delta_a_spatial_b': 'delta_w', 'delta_b_conv_w': 'delta_w', 'delta_b_conv_b': 'delta_w', 'delta_b_gn_g': 'delta_w', 'delta_b_gn_b': 'delta_w', 'delta_out_norm_a_g': 'delta_w', 'delta_out_norm_b_g': 'delta_w', 'delta_w_out': 'delta_w', 'delta_norm2_g': 'delta_w', 'delta_w_ffn_in': 'delta_w', 'delta_w_ffn_out': 'delta_w', 'delta_ada_f_w': 'delta_w', 'delta_ada_f_b': 'delta_w', 'delta_norm_f_g': 'delta_w', 'new_m_ada_w': 'new_m', 'new_m_ada_b': 'new_m', 'new_m_norm1_g': 'new_m', 'new_m_w_in': 'new_m', 'new_m_b_in': 'new_m', 'new_m_a_ln_g': 'new_m', 'new_m_a_ln_b': 'new_m', 'new_m_a_spatial_w': 'new_m', 'new_m_a_spatial_b': 'new_m', 'new_m_b_conv_w': 'new_m', 'new_m_b_conv_b': 'new_m', 'new_m_b_gn_g': 'new_m', 'new_m_b_gn_b': 'new_m', 'new_m_out_norm_a_g': 'new_m', 'new_m_out_norm_b_g': 'new_m', 'new_m_w_out': 'new_m', 'new_m_norm2_g': 'new_m', 'new_m_w_ffn_in': 'new_m', 'new_m_w_ffn_out': 'new_m', 'new_m_ada_f_w': 'new_m', 'new_m_ada_f_b': 'new_m', 'new_m_norm_f_g': 'new_m', 'new_v_ada_w': 'new_v', 'new_v_ada_b': 'new_v', 'new_v_norm1_g': 'new_v', 'new_v_w_in': 'new_v', 'new_v_b_in': 'new_v', 'new_v_a_ln_g': 'new_v', 'new_v_a_ln_b': 'new_v', 'new_v_a_spatial_w': 'new_v', 'new_v_a_spatial_b': 'new_v', 'new_v_b_conv_w': 'new_v', 'new_v_b_conv_b': 'new_v', 'new_v_b_gn_g': 'new_v', 'new_v_b_gn_b': 'new_v', 'new_v_out_norm_a_g': 'new_v', 'new_v_out_norm_b_g': 'new_v', 'new_v_w_out': 'new_v', 'new_v_norm2_g': 'new_v', 'new_v_w_ffn_in': 'new_v', 'new_v_w_ffn_out': 'new_v', 'new_v_ada_f_w': 'new_v', 'new_v_ada_f_b': 'new_v', 'new_v_norm_f_g': 'new_v'}


def _forward(args):
    return _fwd_reference(*[args[k] for k in FWD_PARAMS])


def _output_shape():
    def fwd():
        inp = _fwd_setup_inputs(0)
        return _fwd_reference(*[inp[k] for k in FWD_PARAMS])
    out = _jax.eval_shape(fwd)
    return out.shape, out.dtype

N_MICROBATCH = 1
ADAM_LR = 0.001
ADAM_B1 = 0.9
ADAM_B2 = 0.999
ADAM_EPS = 1e-08
ADAM_WD = 0.01
ADAM_STEP = 10
PER_EXAMPLE_BATCH_AXIS = {'x': 0, 'c': 0, 'loss_target': 0}
SHARED_INPUTS = []
_WEIGHT_DTYPES = {'ada_w': _jnp.float32, 'ada_b': _jnp.float32, 'norm1_g': _jnp.float32, 'w_in': _jnp.float32, 'b_in': _jnp.float32, 'a_ln_g': _jnp.float32, 'a_ln_b': _jnp.float32, 'a_spatial_w': _jnp.float32, 'a_spatial_b': _jnp.float32, 'b_conv_w': _jnp.float32, 'b_conv_b': _jnp.float32, 'b_gn_g': _jnp.float32, 'b_gn_b': _jnp.float32, 'out_norm_a_g': _jnp.float32, 'out_norm_b_g': _jnp.float32, 'w_out': _jnp.float32, 'norm2_g': _jnp.float32, 'w_ffn_in': _jnp.float32, 'w_ffn_out': _jnp.float32, 'ada_f_w': _jnp.float32, 'ada_f_b': _jnp.float32, 'norm_f_g': _jnp.float32}
MOMENT_SCALE = {'ada_w': 4.708198e+00, 'ada_b': 7.994465e+00, 'norm1_g': 3.995139e-01, 'w_in': 2.076172e+00, 'b_in': 3.871196e+00, 'a_ln_g': 2.536992e-01, 'a_ln_b': 2.946937e-01, 'a_spatial_w': 1.189426e-01, 'a_spatial_b': 2.305237e-01, 'b_conv_w': 3.317661e+00, 'b_conv_b': 1.029357e+01, 'b_gn_g': 6.529660e+00, 'b_gn_b': 8.323816e+00, 'out_norm_a_g': 4.425065e+00, 'out_norm_b_g': 4.691731e+00, 'w_out': 4.737536e+00, 'norm2_g': 1.367815e+00, 'w_ffn_in': 1.320838e+00, 'w_ffn_out': 2.542047e+00, 'ada_f_w': 1.700257e+01, 'ada_f_b': 5.063838e+01, 'norm_f_g': 9.802150e+01}


def _to_microbatches(a, axis):
    t = _jnp.moveaxis(a, axis, 0)
    t = t.reshape((N_MICROBATCH, t.shape[0] // N_MICROBATCH) + t.shape[1:])
    return _jnp.moveaxis(t, 1, axis + 1)


def setup_inputs(seed: int = 0) -> dict:
    inp = _fwd_setup_inputs(seed)
    key = _jax.random.fold_in(_jax.random.key(seed), 7919)
    shape, _ = _output_shape()
    out = dict(inp)
    out["loss_target"] = _jax.random.normal(_jax.random.fold_in(key, 0), shape, _jnp.float32)
    for i, name in enumerate(TWIN_WEIGHTS):
        w = inp[name].astype(_jnp.float32)
        if MOMENT_SCALE is None:
            s = _jnp.sqrt(_jnp.mean(_jnp.square(w)) + 1e-30)
        else:
            s = MOMENT_SCALE[name]
        km, kv = _jax.random.split(_jax.random.fold_in(key, i + 1))
        out[name] = w
        out["m_" + name] = s * _jax.random.normal(km, w.shape, _jnp.float32)
        out["v_" + name] = (s * s) * _jax.random.uniform(kv, w.shape, _jnp.float32, 0.5, 1.5)
    if N_MICROBATCH > 1:
        for name, axis in PER_EXAMPLE_BATCH_AXIS.items():
            out[name] = _to_microbatches(out[name], axis)
    return {'x': out['x'], 'c': out['c'], 'ada_w': out['ada_w'], 'ada_b': out['ada_b'], 'norm1_g': out['norm1_g'], 'w_in': out['w_in'], 'b_in': out['b_in'], 'a_ln_g': out['a_ln_g'], 'a_ln_b': out['a_ln_b'], 'a_spatial_w': out['a_spatial_w'], 'a_spatial_b': out['a_spatial_b'], 'b_conv_w': out['b_conv_w'], 'b_conv_b': out['b_conv_b'], 'b_gn_g': out['b_gn_g'], 'b_gn_b': out['b_gn_b'], 'out_norm_a_g': out['out_norm_a_g'], 'out_norm_b_g': out['out_norm_b_g'], 'w_out': out['w_out'], 'norm2_g': out['norm2_g'], 'w_ffn_in': out['w_ffn_in'], 'w_ffn_out': out['w_ffn_out'], 'ada_f_w': out['ada_f_w'], 'ada_f_b': out['ada_f_b'], 'norm_f_g': out['norm_f_g'], 'loss_target': out['loss_target'], 'm_ada_w': out['m_ada_w'], 'm_ada_b': out['m_ada_b'], 'm_norm1_g': out['m_norm1_g'], 'm_w_in': out['m_w_in'], 'm_b_in': out['m_b_in'], 'm_a_ln_g': out['m_a_ln_g'], 'm_a_ln_b': out['m_a_ln_b'], 'm_a_spatial_w': out['m_a_spatial_w'], 'm_a_spatial_b': out['m_a_spatial_b'], 'm_b_conv_w': out['m_b_conv_w'], 'm_b_conv_b': out['m_b_conv_b'], 'm_b_gn_g': out['m_b_gn_g'], 'm_b_gn_b': out['m_b_gn_b'], 'm_out_norm_a_g': out['m_out_norm_a_g'], 'm_out_norm_b_g': out['m_out_norm_b_g'], 'm_w_out': out['m_w_out'], 'm_norm2_g': out['m_norm2_g'], 'm_w_ffn_in': out['m_w_ffn_in'], 'm_w_ffn_out': out['m_w_ffn_out'], 'm_ada_f_w': out['m_ada_f_w'], 'm_ada_f_b': out['m_ada_f_b'], 'm_norm_f_g': out['m_norm_f_g'], 'v_ada_w': out['v_ada_w'], 'v_ada_b': out['v_ada_b'], 'v_norm1_g': out['v_norm1_g'], 'v_w_in': out['v_w_in'], 'v_b_in': out['v_b_in'], 'v_a_ln_g': out['v_a_ln_g'], 'v_a_ln_b': out['v_a_ln_b'], 'v_a_spatial_w': out['v_a_spatial_w'], 'v_a_spatial_b': out['v_a_spatial_b'], 'v_b_conv_w': out['v_b_conv_w'], 'v_b_conv_b': out['v_b_conv_b'], 'v_b_gn_g': out['v_b_gn_g'], 'v_b_gn_b': out['v_b_gn_b'], 'v_out_norm_a_g': out['v_out_norm_a_g'], 'v_out_norm_b_g': out['v_out_norm_b_g'], 'v_w_out': out['v_w_out'], 'v_norm2_g': out['v_norm2_g'], 'v_w_ffn_in': out['v_w_ffn_in'], 'v_w_ffn_out': out['v_w_ffn_out'], 'v_ada_f_w': out['v_ada_f_w'], 'v_ada_f_b': out['v_ada_f_b'], 'v_norm_f_g': out['v_norm_f_g']}


def _loss(weights, diff, rest, loss_target):
    with _jax.named_scope("forward"):
        args = {**rest, TWIN_DIFF_INPUT: diff, **{k: w.astype(_WEIGHT_DTYPES[k]) for k, w in weights.items()}}
        y = _forward(args)
    with _jax.named_scope("loss_head"):
        err = _jnp.square(y.astype(_jnp.float32) - loss_target)
        return 0.5 * _jnp.sum(_jnp.mean(err, axis=-1)) if err.ndim else 0.5 * err


def _adamw(w, g, m, v):
    m = ADAM_B1 * m + (1.0 - ADAM_B1) * g
    v = ADAM_B2 * v + (1.0 - ADAM_B2) * _jnp.square(g)
    m_hat = m / (1.0 - ADAM_B1 ** ADAM_STEP)
    v_hat = v / (1.0 - ADAM_B2 ** ADAM_STEP)
    delta = -ADAM_LR * (m_hat / (_jnp.sqrt(v_hat) + ADAM_EPS) + ADAM_WD * w)
    return delta, m, v


def reference(x, c, ada_w, ada_b, norm1_g, w_in, b_in, a_ln_g, a_ln_b, a_spatial_w, a_spatial_b, b_conv_w, b_conv_b, b_gn_g, b_gn_b, out_norm_a_g, out_norm_b_g, w_out, norm2_g, w_ffn_in, w_ffn_out, ada_f_w, ada_f_b, norm_f_g, loss_target, m_ada_w, m_ada_b, m_norm1_g, m_w_in, m_b_in, m_a_ln_g, m_a_ln_b, m_a_spatial_w, m_a_spatial_b, m_b_conv_w, m_b_conv_b, m_b_gn_g, m_b_gn_b, m_out_norm_a_g, m_out_norm_b_g, m_w_out, m_norm2_g, m_w_ffn_in, m_w_ffn_out, m_ada_f_w, m_ada_f_b, m_norm_f_g, v_ada_w, v_ada_b, v_norm1_g, v_w_in, v_b_in, v_a_ln_g, v_a_ln_b, v_a_spatial_w, v_a_spatial_b, v_b_conv_w, v_b_conv_b, v_b_gn_g, v_b_gn_b, v_out_norm_a_g, v_out_norm_b_g, v_w_out, v_norm2_g, v_w_ffn_in, v_w_ffn_out, v_ada_f_w, v_ada_f_b, v_norm_f_g):
    given = dict(x=x, c=c, ada_w=ada_w, ada_b=ada_b, norm1_g=norm1_g, w_in=w_in, b_in=b_in, a_ln_g=a_ln_g, a_ln_b=a_ln_b, a_spatial_w=a_spatial_w, a_spatial_b=a_spatial_b, b_conv_w=b_conv_w, b_conv_b=b_conv_b, b_gn_g=b_gn_g, b_gn_b=b_gn_b, out_norm_a_g=out_norm_a_g, out_norm_b_g=out_norm_b_g, w_out=w_out, norm2_g=norm2_g, w_ffn_in=w_ffn_in, w_ffn_out=w_ffn_out, ada_f_w=ada_f_w, ada_f_b=ada_f_b, norm_f_g=norm_f_g, loss_target=loss_target, m_ada_w=m_ada_w, m_ada_b=m_ada_b, m_norm1_g=m_norm1_g, m_w_in=m_w_in, m_b_in=m_b_in, m_a_ln_g=m_a_ln_g, m_a_ln_b=m_a_ln_b, m_a_spatial_w=m_a_spatial_w, m_a_spatial_b=m_a_spatial_b, m_b_conv_w=m_b_conv_w, m_b_conv_b=m_b_conv_b, m_b_gn_g=m_b_gn_g, m_b_gn_b=m_b_gn_b, m_out_norm_a_g=m_out_norm_a_g, m_out_norm_b_g=m_out_norm_b_g, m_w_out=m_w_out, m_norm2_g=m_norm2_g, m_w_ffn_in=m_w_ffn_in, m_w_ffn_out=m_w_ffn_out, m_ada_f_w=m_ada_f_w, m_ada_f_b=m_ada_f_b, m_norm_f_g=m_norm_f_g, v_ada_w=v_ada_w, v_ada_b=v_ada_b, v_norm1_g=v_norm1_g, v_w_in=v_w_in, v_b_in=v_b_in, v_a_ln_g=v_a_ln_g, v_a_ln_b=v_a_ln_b, v_a_spatial_w=v_a_spatial_w, v_a_spatial_b=v_a_spatial_b, v_b_conv_w=v_b_conv_w, v_b_conv_b=v_b_conv_b, v_b_gn_g=v_b_gn_g, v_b_gn_b=v_b_gn_b, v_out_norm_a_g=v_out_norm_a_g, v_out_norm_b_g=v_out_norm_b_g, v_w_out=v_w_out, v_norm2_g=v_norm2_g, v_w_ffn_in=v_w_ffn_in, v_w_ffn_out=v_w_ffn_out, v_ada_f_w=v_ada_f_w, v_ada_f_b=v_ada_f_b, v_norm_f_g=v_norm_f_g)
    weights = {n: given[n] for n in TWIN_WEIGHTS}
    shared = {n: given[n] for n in SHARED_INPUTS}
    per_example = {n: given[n] for n in ['x', 'c']}
    grad_fn = _jax.value_and_grad(_loss, argnums=(0, 1))

    def one_microbatch(ex, loss_target):
        ex = dict(ex)
        diff = ex.pop(TWIN_DIFF_INPUT)
        return grad_fn(weights, diff, {**shared, **ex}, loss_target)

    if N_MICROBATCH == 1:
        loss, (grad_w, grad_x) = one_microbatch(per_example, given["loss_target"])
    else:
        def body(carry, xs):
            loss_sum, grad_sum = carry
            l_k, (gw_k, gx_k) = one_microbatch(xs[0], xs[1])
            with _jax.named_scope("update"):
                return (loss_sum + l_k, _jax.tree.map(_jnp.add, grad_sum, gw_k)), gx_k

        init = (_jnp.zeros((), _jnp.float32), _jax.tree.map(_jnp.zeros_like, weights))
        (loss, grad_w), grad_x = _jax.lax.scan(body, init, (per_example, given["loss_target"]))
    with _jax.named_scope("update"):
        delta_w, new_m, new_v = {}, {}, {}
        for n in TWIN_WEIGHTS:
            delta_w[n], new_m[n], new_v[n] = _adamw(weights[n], grad_w[n], given["m_" + n], given["v_" + n])
    return (loss, grad_x, *[grad_w[n] for n in TWIN_WEIGHTS], *[delta_w[n] for n in TWIN_WEIGHTS],
            *[new_m[n] for n in TWIN_WEIGHTS], *[new_v[n] for n in TWIN_WEIGHTS])
```

```python
import functools
import math

import jax
import jax.numpy as jnp
from jax import lax
from jax.experimental import pallas as pl
from jax.experimental.pallas import tpu as pltpu

F32 = jnp.float32
MM_DTYPE = jnp.bfloat16
SDS = jax.ShapeDtypeStruct
MESH = pl.DeviceIdType.MESH

D = 1024
DA = 512
DB = 512
NQ = 4
PW_IN = 512
DFF = 2816
PW_FF = 1408
CHUNK = 128
N_HEADS = 8
CONV_W = 31
HALO = 32
EPS = 1e-6
LANES = 128
HALF = 64

ROW_TILE = 256
GRAD_ROW_TILE = 512
VMEM_LIMIT = 60 * 1024 * 1024

ADAM_LR = 0.001
ADAM_B1 = 0.9
ADAM_B2 = 0.999
ADAM_EPS = 1e-08
ADAM_WD = 0.01
ADAM_STEP = 10

PK_DMOD = 0
PK_G1 = 8
PK_G2 = 9
PK_GF = 10
PK_BIN = 11
PK_V512 = 13
PK_SB = 17
PK_CW = 18
PK_SW = 34
PK_ROWS = 168


def _dot(a, b):
    return jnp.dot(a, b, preferred_element_type=F32)


def _dot_nt(a, b):
    return lax.dot_general(a, b, (((1,), (1,)), ((), ())), preferred_element_type=F32)


def _dot_tn(a, b):
    return lax.dot_general(a, b, (((0,), (0,)), ((), ())), preferred_element_type=F32)


def _rowsum(x):
    return jnp.sum(x, axis=-1, keepdims=True)


def _colsum(x):
    return jnp.sum(x, axis=0, keepdims=True)


def _group_sum(x):
    rows, width = x.shape
    lo_mask = lax.broadcasted_iota(jnp.int32, (rows, LANES), 1) < HALF
    outs = []
    for jb in range(width // LANES):
        xb = x[:, jb * LANES:(jb + 1) * LANES]
        lo = _rowsum(jnp.where(lo_mask, xb, 0.0))
        hi = _rowsum(jnp.where(lo_mask, 0.0, xb))
        outs.append(jnp.where(lo_mask, lo, hi))
    return jnp.concatenate(outs, axis=-1)


def _sigmoid(x):
    return 1.0 / (1.0 + jnp.exp(-x))


def _gelu_parts(u):
    cdf = 0.5 * (1.0 + lax.erf(u * (1.0 / math.sqrt(2.0))))
    pdf = jnp.exp(-0.5 * u * u) * (1.0 / math.sqrt(2.0 * math.pi))
    return u * cdf, cdf + u * pdf


def _whole_vmem():
    return pl.BlockSpec(memory_space=pltpu.VMEM)


def _params(*semantics):
    return pltpu.CompilerParams(dimension_semantics=semantics, vmem_limit_bytes=VMEM_LIMIT)


def _mod_rows(mod_ref, modb_ref, first, count):
    m = mod_ref[...] + modb_ref[...]
    return [m[:, (first + k) * D:(first + k + 1) * D] for k in range(count)]


def _mixer_recompute(z_parts, lng, lnb, wm_ref, bst_ref, mix_ref):
    u, v, val, gate = z_parts
    rows = u.shape[0]
    gu, dgu = _gelu_parts(u)
    gv, dgv = _gelu_parts(v)
    mu = _rowsum(gv) * (1.0 / DA)
    vc = gv - mu
    rs = lax.rsqrt(_rowsum(vc * vc) * (1.0 / DA) + EPS)
    vhat = vc * rs
    vl = vhat * lng + lnb
    vlb = vl.astype(MM_DTYPE)
    lo_mask = lax.broadcasted_iota(jnp.int32, (CHUNK, LANES), 1) < HALF
    for ck in range(rows // CHUNK):
        for jb in range(DA // LANES):
            blk = vlb[ck * CHUNK:(ck + 1) * CHUNK, jb * LANES:(jb + 1) * LANES]
            a = _dot(wm_ref[2 * jb], blk)
            b = _dot(wm_ref[2 * jb + 1], blk)
            mix_ref[ck * CHUNK:(ck + 1) * CHUNK, jb * LANES:(jb + 1) * LANES] = (
                jnp.where(lo_mask, a, b) + bst_ref[:, jb * LANES:(jb + 1) * LANES])
    mixed = mix_ref[...]
    sg = _sigmoid(gate)
    yb0 = val * sg
    return dict(gu=gu, dgu=dgu, dgv=dgv, rs=rs, vhat=vhat, vlb=vlb, mixed=mixed, sg=sg, yb0=yb0)


def _conv_branch_tail(yb1, gng, gnb):
    gm = _group_sum(yb1) * (1.0 / HALF)
    gc = yb1 - gm
    grs = lax.rsqrt(_group_sum(gc * gc) * (1.0 / HALF) + EPS)
    ghat = gc * grs
    yb2 = ghat * gng + gnb
    s2 = _sigmoid(yb2)
    return dict(grs=grs, ghat=ghat, yb2=yb2, s2=s2, y_b=yb2 * s2)


def _mixer_fwd(x, mod, mods, norm1_g, w_in4, b_in, ln_g, ln_b, wm, bst, conv_w, conv_b, gn_g, gn_b, ga, gb, w_out):
    s = x.shape[0]
    ts = min(ROW_TILE, s)
    nt = s // ts

    def body(x_ref, mod_ref, modb_ref, g1_ref, w4_ref, bin_ref, lng_ref, lnb_ref, wm_ref, bst_ref, cw_ref, cb_ref,
             gng_ref, gnb_ref, ga_ref, gb_ref, wout_ref, z_ref, x1_ref, yb1_ref, y_ref, e_ref, mix_ref):
        @pl.when(pl.program_id(0) == 0)
        def _():
            e_ref[0:HALO, :] = jnp.zeros((HALO, DB), F32)

        shift1, scale1, gate1 = _mod_rows(mod_ref, modb_ref, 0, 3)
        x_t = x_ref[...]
        r1 = lax.rsqrt(_rowsum(x_t * x_t) * (1.0 / D) + EPS)
        h = (x_t * r1 * g1_ref[...]) * (1.0 + scale1) + shift1
        hb = h.astype(MM_DTYPE)
        z_parts = []
        for q in range(NQ):
            zq = _dot(hb, w4_ref[q]) + bin_ref[:, q * PW_IN:(q + 1) * PW_IN]
            z_ref[:, q * PW_IN:(q + 1) * PW_IN] = zq
            z_parts.append(zq)
        r = _mixer_recompute(z_parts, lng_ref[...], lnb_ref[...], wm_ref, bst_ref, mix_ref)
        y_a = r["gu"] * r["mixed"]
        e_ref[HALO:HALO + ts, :] = r["yb0"]
        acc = jnp.zeros((ts, DB), F32) + cb_ref[...]
        for k in range(CONV_W):
            acc = acc + e_ref[pl.ds(HALO - (CONV_W - 1) + k, ts), :] * cw_ref[k:k + 1, :]
        e_ref[0:HALO, :] = e_ref[ts:ts + HALO, :]
        yb1_ref[...] = acc
        t = _conv_branch_tail(acc, gng_ref[...], gnb_ref[...])
        ra = lax.rsqrt(_rowsum(y_a * y_a) * (1.0 / DA) + EPS)
        rb = lax.rsqrt(_rowsum(t["y_b"] * t["y_b"]) * (1.0 / DB) + EPS)
        yan = (y_a * ra * ga_ref[...]).astype(MM_DTYPE)
        ybn = (t["y_b"] * rb * gb_ref[...]).astype(MM_DTYPE)
        y_ref[:, 0:DA] = yan
        y_ref[:, DA:D] = ybn
        o1 = _dot(yan, wout_ref[0:DA, :]) + _dot(ybn, wout_ref[DA:D, :])
        x1_ref[...] = x_t + gate1 * o1

    row = lambda w: pl.BlockSpec((ts, w), lambda i: (i, 0))
    full = lambda a: pl.BlockSpec(a.shape, lambda i: (0,) * a.ndim)
    return pl.pallas_call(
        body, name="mixer_fwd", grid=(nt,),
        in_specs=[row(D), full(mod), full(mods), full(norm1_g), _whole_vmem(), full(b_in), full(ln_g), full(ln_b),
                  _whole_vmem(), full(bst), full(conv_w), full(conv_b), full(gn_g), full(gn_b), full(ga), full(gb),
                  _whole_vmem()],
        out_specs=[row(4 * PW_IN), row(D), row(DB), row(D)],
        out_shape=[SDS((s, 4 * PW_IN), F32), SDS((s, D), F32), SDS((s, DB), F32), SDS((s, D), MM_DTYPE)],
        scratch_shapes=[pltpu.VMEM((ts + HALO, DB), F32), pltpu.VMEM((ts, DA), F32)],
        compiler_params=_params("arbitrary"),
    )(x, mod, mods, norm1_g, w_in4, b_in, ln_g, ln_b, wm, bst, conv_w, conv_b, gn_g, gn_b, ga, gb, w_out)


def _ffn_fwd(x1, target, mod, mods, norm2_g, norm_f_g, w_ffn_in4, w_ffn_out):
    s = x1.shape[0]
    ts = min(ROW_TILE, s)
    nt = s // ts

    def body(x1_ref, tgt_ref, mod_ref, modb_ref, g2_ref, gf_ref, wf_ref, wo_ref,
             g_ref, up_ref, h2_ref, dx2_ref, acc_ref):
        i = pl.program_id(0)

        @pl.when(i == 0)
        def _():
            acc_ref[...] = jnp.zeros(acc_ref.shape, F32)

        shift2, scale2, gate2, shift_f, scale_f = _mod_rows(mod_ref, modb_ref, 3, 5)
        x1_t = x1_ref[...]
        r2 = lax.rsqrt(_rowsum(x1_t * x1_t) * (1.0 / D) + EPS)
        h2 = (x1_t * r2 * g2_ref[...]) * (1.0 + scale2) + shift2
        h2b = h2.astype(MM_DTYPE)
        h2_ref[...] = h2b
        o2 = jnp.zeros((ts, D), F32)
        for p in range(2):
            g = _dot(h2b, wf_ref[p])
            up = _dot(h2b, wf_ref[2 + p])
            g_ref[:, p * PW_FF:(p + 1) * PW_FF] = g.astype(MM_DTYPE)
            up_ref[:, p * PW_FF:(p + 1) * PW_FF] = up.astype(MM_DTYPE)
            a = (g * _sigmoid(g) * up).astype(MM_DTYPE)
            o2 = o2 + _dot(a, wo_ref[p * PW_FF:(p + 1) * PW_FF, :])
        x2 = x1_t + gate2 * o2
        rf = lax.rsqrt(_rowsum(x2 * x2) * (1.0 / D) + EPS)
        gf = gf_ref[...]
        nf = x2 * rf * gf
        err = nf * (1.0 + scale_f) + shift_f - tgt_ref[...]
        d_out = err * (1.0 / D)
        d_nf = d_out * (1.0 + scale_f)
        t = d_nf * gf
        dx2_ref[...] = rf * t - x2 * (rf * rf * rf) * (_rowsum(t * x2) * (1.0 / D))
        acc_ref[0:1, :] += _colsum(d_out)
        acc_ref[1:2, :] += _colsum(d_out * nf)
        acc_ref[2:3, :] += _colsum(d_nf * x2 * rf)
        acc_ref[3:4, :] += _colsum(err * err)

        @pl.when(i == nt - 1)
        def _():
            acc_ref[4:5, :] = jnp.zeros((1, D), F32) + _rowsum(acc_ref[3:4, :]) * (0.5 / D)

    row = lambda w: pl.BlockSpec((ts, w), lambda i: (i, 0))
    full = lambda a: pl.BlockSpec(a.shape, lambda i: (0,) * a.ndim)
    return pl.pallas_call(
        body, name="ffn_fwd", grid=(nt,),
        in_specs=[row(D), row(D), full(mod), full(mods), full(norm2_g), full(norm_f_g), _whole_vmem(), _whole_vmem()],
        out_specs=[row(DFF), row(DFF), row(D), row(D), pl.BlockSpec((8, D), lambda i: (0, 0))],
        out_shape=[SDS((s, DFF), MM_DTYPE), SDS((s, DFF), MM_DTYPE), SDS((s, D), MM_DTYPE), SDS((s, D), F32),
                   SDS((8, D), F32)],
        compiler_params=_params("arbitrary"),
    )(x1, target, mod, mods, norm2_g, norm_f_g, w_ffn_in4, w_ffn_out)


def _ffn_bwd(dx2, x1, g, up, mod, mods, norm2_g, w_ffn_in4, w_ffn_out):
    s = x1.shape[0]
    ts = min(ROW_TILE, s)
    nt = s // ts

    def body(dx2_ref, x1_ref, g_ref, up_ref, mod_ref, modb_ref, g2_ref, wf_ref, wo_ref,
             dff_ref, a_ref, dx1_ref, acc_ref):
        @pl.when(pl.program_id(0) == 0)
        def _():
            acc_ref[...] = jnp.zeros(acc_ref.shape, F32)

        shift2, scale2, gate2 = _mod_rows(mod_ref, modb_ref, 3, 3)
        dx2_t = dx2_ref[...]
        do2 = (dx2_t * gate2).astype(MM_DTYPE)
        dh2 = jnp.zeros((ts, D), F32)
        for p in range(2):
            da = _dot_nt(do2, wo_ref[p * PW_FF:(p + 1) * PW_FF, :])
            gp = g_ref[:, p * PW_FF:(p + 1) * PW_FF].astype(F32)
            upp = up_ref[:, p * PW_FF:(p + 1) * PW_FF].astype(F32)
            sg = _sigmoid(gp)
            silu = gp * sg
            a_ref[:, p * PW_FF:(p + 1) * PW_FF] = (silu * upp).astype(MM_DTYPE)
            dg = (da * upp * (sg * (1.0 + gp * (1.0 - sg)))).astype(MM_DTYPE)
            dup = (da * silu).astype(MM_DTYPE)
            dff_ref[:, p * PW_FF:(p + 1) * PW_FF] = dg
            dff_ref[:, DFF + p * PW_FF:DFF + (p + 1) * PW_FF] = dup
            dh2 = dh2 + _dot_nt(dg, wf_ref[p]) + _dot_nt(dup, wf_ref[2 + p])
        x1_t = x1_ref[...]
        r2 = lax.rsqrt(_rowsum(x1_t * x1_t) * (1.0 / D) + EPS)
        g2 = g2_ref[...]
        xr = x1_t * r2
        dn2 = dh2 * (1.0 + scale2)
        t = dn2 * g2
        dx1_ref[...] = dx2_t + r2 * t - x1_t * (r2 * r2 * r2) * (_rowsum(t * x1_t) * (1.0 / D))
        acc_ref[0:1, :] += _colsum(dh2)
        acc_ref[1:2, :] += _colsum(dh2 * (xr * g2))
        acc_ref[2:3, :] += _colsum(dn2 * xr)

    row = lambda w: pl.BlockSpec((ts, w), lambda i: (i, 0))
    full = lambda a: pl.BlockSpec(a.shape, lambda i: (0,) * a.ndim)
    return pl.pallas_call(
        body, name="ffn_bwd", grid=(nt,),
        in_specs=[row(D), row(D), row(DFF), row(DFF), full(mod), full(mods), full(norm2_g), _whole_vmem(), _whole_vmem()],
        out_specs=[row(2 * DFF), row(DFF), row(D), pl.BlockSpec((8, D), lambda i: (0, 0))],
        out_shape=[SDS((s, 2 * DFF), MM_DTYPE), SDS((s, DFF), MM_DTYPE), SDS((s, D), F32), SDS((8, D), F32)],
        compiler_params=_params("arbitrary"),
    )(dx2, x1, g, up, mod, mods, norm2_g, w_ffn_in4, w_ffn_out)


def _mixer_bwd(dx1, x, z, yb1, mod, mods, norm1_g, w_in4, ln_g, ln_b, wm, wmt, bst, conv_w, gn_g, gn_b, ga, gb, w_out):
    s = x.shape[0]
    ts = min(ROW_TILE, s)
    nt = s // ts

    def body(dx1_ref, x_ref, z_ref, yb1_ref, mod_ref, modb_ref, g1_ref, w4_ref, lng_ref, lnb_ref, wm_ref, wmt_ref,
             bst_ref, cw_ref, gng_ref, gnb_ref, ga_ref, gb_ref, wout_ref,
             gx_ref, dz_ref, h_ref, a1_ref, a2_ref, a5_ref, acw_ref, asw_ref, asb_ref,
             e_ref, mix_ref, dvl_ref):
        i = pl.program_id(0)

        @pl.when(i == 0)
        def _():
            e_ref[ts:ts + HALO, :] = jnp.zeros((HALO, DB), F32)
            for r in (a1_ref, a2_ref, a5_ref, acw_ref, asw_ref, asb_ref):
                r[...] = jnp.zeros(r.shape, F32)

        shift1, scale1, gate1 = _mod_rows(mod_ref, modb_ref, 0, 3)
        dx1_t = dx1_ref[...]
        do1 = (dx1_t * gate1).astype(MM_DTYPE)
        d_yan = _dot_nt(do1, wout_ref[0:DA, :])
        d_ybn = _dot_nt(do1, wout_ref[DA:D, :])

        z_parts = [z_ref[:, q * PW_IN:(q + 1) * PW_IN] for q in range(NQ)]
        u, v, val, gate = z_parts
        lng = lng_ref[...]
        r = _mixer_recompute(z_parts, lng, lnb_ref[...], wm_ref, bst_ref, mix_ref)
        gng = gng_ref[...]
        t = _conv_branch_tail(yb1_ref[...], gng, gnb_ref[...])
        y_a = r["gu"] * r["mixed"]
        y_b = t["y_b"]
        ga_v, gb_v = ga_ref[...], gb_ref[...]
        ra = lax.rsqrt(_rowsum(y_a * y_a) * (1.0 / DA) + EPS)
        rb = lax.rsqrt(_rowsum(y_b * y_b) * (1.0 / DB) + EPS)

        a5_ref[0:1, :] += _colsum(d_yan * y_a * ra)
        a5_ref[1:2, :] += _colsum(d_ybn * y_b * rb)
        ta = d_yan * ga_v
        d_ya = ra * ta - y_a * (ra * ra * ra) * (_rowsum(ta * y_a) * (1.0 / DA))
        tb = d_ybn * gb_v
        d_yb = rb * tb - y_b * (rb * rb * rb) * (_rowsum(tb * y_b) * (1.0 / DB))

        d_u = d_ya * r["mixed"] * r["dgu"]
        d_mixed = d_ya * r["gu"]
        dmb = d_mixed.astype(MM_DTYPE)
        lo_mask = lax.broadcasted_iota(jnp.int32, (CHUNK, LANES), 1) < HALF
        zero_blk = jnp.zeros((CHUNK, LANES), MM_DTYPE)
        sb_acc = jnp.zeros((CHUNK, DA), F32)
        for ck in range(ts // CHUNK):
            rows = slice(ck * CHUNK, (ck + 1) * CHUNK)
            sb_acc = sb_acc + d_mixed[rows, :]
            for jb in range(DA // LANES):
                cols = slice(jb * LANES, (jb + 1) * LANES)
                dm_blk = dmb[rows, cols]
                vl_blk = r["vlb"][rows, cols]
                da_ = _dot(wmt_ref[2 * jb], dm_blk)
                db_ = _dot(wmt_ref[2 * jb + 1], dm_blk)
                dvl_ref[rows, cols] = jnp.where(lo_mask, da_, db_)
                asw_ref[2 * jb] += _dot_nt(jnp.where(lo_mask, dm_blk, zero_blk), vl_blk)
                asw_ref[2 * jb + 1] += _dot_nt(jnp.where(lo_mask, zero_blk, dm_blk), vl_blk)
        asb_ref[...] += sb_acc
        d_vl = dvl_ref[...]
        a5_ref[2:3, :] += _colsum(d_vl * r["vhat"])
        a5_ref[3:4, :] += _colsum(d_vl)
        dvh = d_vl * lng
        d_gv = r["rs"] * (dvh - _rowsum(dvh) * (1.0 / DA) - r["vhat"] * (_rowsum(dvh * r["vhat"]) * (1.0 / DA)))
        d_v = d_gv * r["dgv"]

        yb2, s2 = t["yb2"], t["s2"]
        d_yb2 = d_yb * (s2 * (1.0 + yb2 * (1.0 - s2)))
        a5_ref[4:5, :] += _colsum(d_yb2 * t["ghat"])
        a5_ref[5:6, :] += _colsum(d_yb2)
        dgh = d_yb2 * gng
        d_yb1 = t["grs"] * (dgh - _group_sum(dgh) * (1.0 / HALF) - t["ghat"] * (_group_sum(dgh * t["ghat"]) * (1.0 / HALF)))
        a5_ref[6:7, :] += _colsum(d_yb1)
        e_ref[0:ts, :] = d_yb1
        yb0 = r["yb0"]
        d_yb0 = jnp.zeros((ts, DB), F32)
        for j in range(CONV_W):
            k = CONV_W - 1 - j
            win = e_ref[pl.ds(j, ts), :]
            d_yb0 = d_yb0 + win * cw_ref[k:k + 1, :]
            acw_ref[k:k + 1, :] += _colsum(yb0 * win)
        e_ref[ts:ts + HALO, :] = e_ref[0:HALO, :]
        sg = r["sg"]
        d_val = d_yb0 * sg
        d_gate = d_yb0 * val * sg * (1.0 - sg)

        dh = jnp.zeros((ts, D), F32)
        for q, dzq in enumerate((d_u, d_v, d_val, d_gate)):
            a2_ref[0:1, q * PW_IN:(q + 1) * PW_IN] += _colsum(dzq)
            dzb = dzq.astype(MM_DTYPE)
            dz_ref[:, q * PW_IN:(q + 1) * PW_IN] = dzb
            dh = dh + _dot_nt(dzb, w4_ref[q])
        x_t = x_ref[...]
        r1 = lax.rsqrt(_rowsum(x_t * x_t) * (1.0 / D) + EPS)
        g1 = g1_ref[...]
        xr = x_t * r1
        n1 = xr * g1
        h_ref[...] = (n1 * (1.0 + scale1) + shift1).astype(MM_DTYPE)
        dn1 = dh * (1.0 + scale1)
        t1 = dn1 * g1
        gx_ref[...] = dx1_t + r1 * t1 - x_t * (r1 * r1 * r1) * (_rowsum(t1 * x_t) * (1.0 / D))
        a1_ref[0:1, :] += _colsum(dh)
        a1_ref[1:2, :] += _colsum(dh * n1)
        a1_ref[2:3, :] += _colsum(dn1 * xr)

        @pl.when(i == nt - 1)
        def _():
            asb_ref[...] = _group_sum(asb_ref[...])

    row = lambda w: pl.BlockSpec((ts, w), lambda i: (nt - 1 - i, 0))
    full = lambda a: pl.BlockSpec(a.shape, lambda i: (0,) * a.ndim)
    keep = lambda shape: pl.BlockSpec(shape, lambda i: (0,) * len(shape))
    return pl.pallas_call(
        body, name="mixer_bwd", grid=(nt,),
        in_specs=[row(D), row(D), row(4 * PW_IN), row(DB), full(mod), full(mods), full(norm1_g), _whole_vmem(),
                  full(ln_g), full(ln_b), _whole_vmem(), _whole_vmem(), full(bst), full(conv_w), full(gn_g), full(gn_b),
                  full(ga), full(gb), _whole_vmem()],
        out_specs=[row(D), row(4 * PW_IN), row(D), keep((8, D)), keep((8, 4 * PW_IN)), keep((8, DA)),
                   keep((HALO, DB)), keep((N_HEADS, CHUNK, CHUNK)), keep((CHUNK, DA))],
        out_shape=[SDS((s, D), F32), SDS((s, 4 * PW_IN), MM_DTYPE), SDS((s, D), MM_DTYPE), SDS((8, D), F32),
                   SDS((8, 4 * PW_IN), F32), SDS((8, DA), F32), SDS((HALO, DB), F32),
                   SDS((N_HEADS, CHUNK, CHUNK), F32), SDS((CHUNK, DA), F32)],
        scratch_shapes=[pltpu.VMEM((ts + HALO, DB), F32), pltpu.VMEM((ts, DA), F32), pltpu.VMEM((ts, DA), F32)],
        compiler_params=_params("arbitrary"),
    )(dx1, x, z, yb1, mod, mods, norm1_g, w_in4, ln_g, ln_b, wm, wmt, bst, conv_w, gn_g, gn_b, ga, gb, w_out)


def _grad_matmul(name, a, b, ka_tile, nb_tile, pieces):
    s, ka = a.shape
    nb = b.shape[1]
    ts = min(GRAD_ROW_TILE, s)
    nt = s // ts
    nja, njb = ka // ka_tile, nb // nb_tile

    def body(a_ref, b_ref, o_ref):
        @pl.when(pl.program_id(2) == 0)
        def _():
            o_ref[...] = jnp.zeros(o_ref.shape, F32)

        prod = _dot_tn(a_ref[...].astype(MM_DTYPE), b_ref[...].astype(MM_DTYPE))
        o_ref[...] += prod.reshape(o_ref.shape)

    if pieces:
        assert nja == 1
        out_shape = SDS((njb, ka, nb_tile), F32)
        out_spec = pl.BlockSpec((1, ka, nb_tile), lambda ja, jb, i: (jb, 0, 0))
    else:
        out_shape = SDS((ka, nb), F32)
        out_spec = pl.BlockSpec((ka_tile, nb_tile), lambda ja, jb, i: (ja, jb))
    return pl.pallas_call(
        body, name=name, grid=(nja, njb, nt),
        in_specs=[pl.BlockSpec((ts, ka_tile), lambda ja, jb, i: (i, ja)),
                  pl.BlockSpec((ts, nb_tile), lambda ja, jb, i: (i, jb))],
        out_specs=out_spec, out_shape=out_shape,
        compiler_params=_params("parallel", "parallel", "arbitrary"),
    )(a, b)


def _gate_finalize(name, gmat, w, gate):
    k, n = gmat.shape
    kt = 256 if k % 256 == 0 else 352
    nk = k // kt

    def body(g_ref, w_ref, gate_ref, o_ref, dg_ref):
        @pl.when(pl.program_id(0) == 0)
        def _():
            dg_ref[...] = jnp.zeros(dg_ref.shape, F32)

        gm = g_ref[...]
        o_ref[...] = gm * gate_ref[...]
        dg_ref[0:1, :] += _colsum(gm * w_ref[...].astype(F32))

    return pl.pallas_call(
        body, name=name, grid=(nk,),
        in_specs=[pl.BlockSpec((kt, n), lambda i: (i, 0)), pl.BlockSpec((kt, n), lambda i: (i, 0)),
                  pl.BlockSpec((1, n), lambda i: (0, 0))],
        out_specs=[pl.BlockSpec((kt, n), lambda i: (i, 0)), pl.BlockSpec((8, n), lambda i: (0, 0))],
        out_shape=[SDS((k, n), F32), SDS((8, n), F32)],
        compiler_params=_params("arbitrary"),
    )(gmat, w, gate)


def _cond_partial(c_all, w_cat):
    n = w_cat.shape[1]
    nt_cols = 512

    def body(c_ref, w_ref, o_ref):
        c_t = c_ref[...]
        ca = (c_t * _sigmoid(c_t)).astype(MM_DTYPE)
        o_ref[...] = _dot(ca, w_ref[...].astype(MM_DTYPE))

    return pl.pallas_call(
        body, name="cond_partial", grid=(n // nt_cols,),
        in_specs=[pl.BlockSpec((8, D), lambda j: (0, 0)), pl.BlockSpec((D, nt_cols), lambda j: (0, j))],
        out_specs=pl.BlockSpec((8, nt_cols), lambda j: (0, j)), out_shape=SDS((8, n), F32),
        compiler_params=_params("parallel"),
    )(c_all, w_cat)


def _cond_grad(c_all, dmod_cols):
    n = dmod_cols.shape[1]
    nt_cols = 512

    def body(c_ref, d_ref, o_ref):
        c_t = c_ref[...]
        ca = jnp.concatenate([c_t * _sigmoid(c_t), jnp.zeros((8, D), F32)], axis=0).astype(MM_DTYPE)
        dm = jnp.concatenate([d_ref[...], jnp.zeros((8, nt_cols), F32)], axis=0).astype(MM_DTYPE)
        o_ref[...] = _dot_tn(ca, dm)

    return pl.pallas_call(
        body, name="cond_grad", grid=(n // nt_cols,),
        in_specs=[pl.BlockSpec((8, D), lambda j: (0, 0)), pl.BlockSpec((8, nt_cols), lambda j: (0, j))],
        out_specs=pl.BlockSpec((D, nt_cols), lambda j: (0, j)), out_shape=SDS((D, n), F32),
        compiler_params=_params("parallel"),
    )(c_all, dmod_cols)


def _row_tile(rows, cap=256):
    if rows <= cap:
        return rows
    for t in range(cap, 7, -8):
        if rows % t == 0:
            return t
    return rows


def _ordered_sum(name, parts, out_dtype=F32):
    n, rows, cols = parts.shape
    rt = _row_tile(rows)

    def body(p_ref, o_ref):
        acc = p_ref[0].astype(F32)
        for k in range(1, n):
            acc = acc + p_ref[k].astype(F32)
        o_ref[...] = acc.astype(out_dtype)

    return pl.pallas_call(
        body, name=name, grid=(rows // rt,),
        in_specs=[pl.BlockSpec((n, rt, cols), lambda i: (0, i, 0))],
        out_specs=pl.BlockSpec((rt, cols), lambda i: (i, 0)), out_shape=SDS((rows, cols), out_dtype),
        compiler_params=_params("parallel"),
    )(parts)


def _add2(name, a, b):
    rows, cols = a.shape
    rt = _row_tile(rows)

    def body(a_ref, b_ref, o_ref):
        o_ref[...] = a_ref[...] + b_ref[...]

    spec = pl.BlockSpec((rt, cols), lambda i: (i, 0))
    return pl.pallas_call(body, name=name, grid=(rows // rt,), in_specs=[spec, spec], out_specs=spec,
                          out_shape=SDS((rows, cols), F32), compiler_params=_params("parallel"))(a, b)


def _adamw(name, w, g, m, v):
    rows, cols = w.shape
    rt = _row_tile(rows)
    c1 = 1.0 - ADAM_B1 ** ADAM_STEP
    c2 = 1.0 - ADAM_B2 ** ADAM_STEP

    def body(w_ref, g_ref, m_ref, v_ref, d_ref, nm_ref, nv_ref):
        g_t = g_ref[...]
        m_new = ADAM_B1 * m_ref[...] + (1.0 - ADAM_B1) * g_t
        v_new = ADAM_B2 * v_ref[...] + (1.0 - ADAM_B2) * (g_t * g_t)
        nm_ref[...] = m_new
        nv_ref[...] = v_new
        d_ref[...] = -ADAM_LR * ((m_new / c1) / (jnp.sqrt(v_new / c2) + ADAM_EPS) + ADAM_WD * w_ref[...])

    spec = pl.BlockSpec((rt, cols), lambda i: (i, 0))
    out = SDS((rows, cols), F32)
    return pl.pallas_call(body, name=name, grid=(rows // rt,), in_specs=[spec] * 4, out_specs=[spec] * 3,
                          out_shape=[out, out, out], compiler_params=_params("parallel"))(w, g, m, v)


def _place():
    return lax.axis_index("x"), lax.axis_index("y"), lax.axis_index("c")


def _other_chips(x, y):
    return [(1 - x, y), (x, 1 - y), (1 - x, 1 - y)]


def _all_gather8(name, blk):
    m, n = blk.shape

    def body(x_ref, out_ref, send_sems, recv_sems, local_sem):
        x, y, c = _place()
        me, sibling = (x, y, c), (x, y, 1 - c)
        chips = _other_chips(x, y)

        def slot(px, py, pc):
            return out_ref.at[4 * px + 2 * py + pc]

        def copy(k, block, to, src=None):
            return pltpu.make_async_remote_copy(
                src_ref=slot(*block) if src is None else src, dst_ref=slot(*block),
                send_sem=send_sems.at[k], recv_sem=recv_sems.at[k], device_id=to, device_id_type=MESH)

        mine = pltpu.make_async_copy(x_ref, slot(*me), local_sem)
        mine.start()
        first = [copy(0, me, sibling, src=x_ref)]
        first += [copy(1 + j, me, (*chip, c), src=x_ref) for j, chip in enumerate(chips)]
        for cp in first:
            cp.start()
        passed = [copy(4 + j, (*chip, c), sibling) for j, chip in enumerate(chips)]
        for j, chip in enumerate(chips):
            copy(1 + j, (*chip, c), me).wait_recv()
            passed[j].start()
        copy(0, sibling, me).wait_recv()
        for j, chip in enumerate(chips):
            copy(4 + j, (*chip, 1 - c), me).wait_recv()
        for cp in first + passed:
            cp.wait_send()
        mine.wait()

    return pl.pallas_call(
        body, name=name, out_shape=SDS((8, m, n), blk.dtype),
        in_specs=[_whole_vmem()], out_specs=_whole_vmem(),
        scratch_shapes=[pltpu.SemaphoreType.DMA((7,)), pltpu.SemaphoreType.DMA((7,)), pltpu.SemaphoreType.DMA],
        compiler_params=pltpu.CompilerParams(vmem_limit_bytes=VMEM_LIMIT),
    )(blk)


def _any():
    return pl.BlockSpec(memory_space=pl.ANY)


def _gather_weights(shards):
    n = len(shards)

    def body(*refs):
        ins, outs = refs[:n], refs[n:2 * n]
        send_sems, recv_sems, local_sems = refs[2 * n:]
        x, y, c = _place()
        sibling = (x, y, 1 - c)
        chips = _other_chips(x, y)

        def half(ref, q, hc, rows):
            return ref.at[q, pl.ds(hc * (rows // 2), rows // 2)]

        def copy(w, k, q, hc, to, src=None):
            rows = ins[w].shape[0]
            dst = half(outs[w], q, hc, rows)
            return pltpu.make_async_remote_copy(
                src_ref=dst if src is None else src, dst_ref=dst,
                send_sem=send_sems.at[w, k], recv_sem=recv_sems.at[w, k], device_id=to, device_id_type=MESH)

        myq = 2 * x + y
        local = [pltpu.make_async_copy(ins[w], outs[w].at[myq], local_sems.at[w]) for w in range(n)]
        for cp in local:
            cp.start()
        sent = []
        for w in range(n):
            rows = ins[w].shape[0]
            src = ins[w].at[pl.ds(c * (rows // 2), rows // 2)]
            for j, chip in enumerate(chips):
                cp = copy(w, j, myq, c, (*chip, c), src=src)
                cp.start()
                sent.append(cp)
        for w in range(n):
            for j, (qx, qy) in enumerate(chips):
                copy(w, j, 2 * qx + qy, c, sibling).wait_recv()
                cp = copy(w, 3 + j, 2 * qx + qy, c, sibling)
                cp.start()
                sent.append(cp)
        for w in range(n):
            for j, (qx, qy) in enumerate(chips):
                copy(w, 3 + j, 2 * qx + qy, 1 - c, sibling).wait_recv()
        for cp in sent:
            cp.wait_send()
        for cp in local:
            cp.wait()

    return pl.pallas_call(
        body, name="gather_weights",
        out_shape=[SDS((NQ,) + s.shape, s.dtype) for s in shards],
        in_specs=[_any()] * n, out_specs=[_any()] * n,
        scratch_shapes=[pltpu.SemaphoreType.DMA((n, 6)), pltpu.SemaphoreType.DMA((n, 6)), pltpu.SemaphoreType.DMA((n,))],
    )(*shards)


def _swap_halves(name, parts):
    n = len(parts)

    def body(*refs):
        ins, outs = refs[:n], refs[n:2 * n]
        send_sems, recv_sems = refs[2 * n:]
        x, y, c = _place()
        sibling = (x, y, 1 - c)
        cps = []
        for w in range(n):
            rows = ins[w].shape[1]
            src = ins[w].at[:, pl.ds((1 - c) * (rows // 2), rows // 2)]
            cp = pltpu.make_async_remote_copy(src_ref=src, dst_ref=outs[w], send_sem=send_sems.at[w],
                                              recv_sem=recv_sems.at[w], device_id=sibling, device_id_type=MESH)
            cp.start()
            cps.append(cp)
        for cp in cps:
            cp.wait()

    return pl.pallas_call(
        body, name=name,
        out_shape=[SDS((p.shape[0], p.shape[1] // 2, p.shape[2]), p.dtype) for p in parts],
        in_specs=[_any()] * n, out_specs=[_any()] * n,
        scratch_shapes=[pltpu.SemaphoreType.DMA((n,)), pltpu.SemaphoreType.DMA((n,))],
    )(*parts)


def _scatter_to_chips(parts):
    n = len(parts)

    def body(*refs):
        ins, outs = refs[:n], refs[n:2 * n]
        send_sems, recv_sems, local_sems = refs[2 * n:]
        x, y, c = _place()
        chips = _other_chips(x, y)
        myq = 2 * x + y
        local = [pltpu.make_async_copy(ins[w].at[myq], outs[w].at[myq], local_sems.at[w]) for w in range(n)]
        for cp in local:
            cp.start()
        cps = []
        for w in range(n):
            for j, (qx, qy) in enumerate(chips):
                cp = pltpu.make_async_remote_copy(
                    src_ref=ins[w].at[2 * qx + qy], dst_ref=outs[w].at[myq],
                    send_sem=send_sems.at[w, j], recv_sem=recv_sems.at[w, j],
                    device_id=(qx, qy, c), device_id_type=MESH)
                cp.start()
                cps.append(cp)
        for cp in cps:
            cp.wait()
        for cp in local:
            cp.wait()

    return pl.pallas_call(
        body, name="scatter_to_chips",
        out_shape=[SDS(p.shape, p.dtype) for p in parts],
        in_specs=[_any()] * n, out_specs=[_any()] * n,
        scratch_shapes=[pltpu.SemaphoreType.DMA((n, 3)), pltpu.SemaphoreType.DMA((n, 3)), pltpu.SemaphoreType.DMA((n,))],
    )(*parts)


def _join_halves(halves):
    n = len(halves)

    def body(*refs):
        ins, outs = refs[:n], refs[n:2 * n]
        send_sems, recv_sems, local_sems = refs[2 * n:]
        x, y, c = _place()
        sibling = (x, y, 1 - c)
        cps, local = [], []
        for w in range(n):
            h = ins[w].shape[0]
            dst = outs[w].at[pl.ds(c * h, h)]
            lc = pltpu.make_async_copy(ins[w], dst, local_sems.at[w])
            lc.start()
            local.append(lc)
            cp = pltpu.make_async_remote_copy(src_ref=ins[w], dst_ref=dst, send_sem=send_sems.at[w],
                                              recv_sem=recv_sems.at[w], device_id=sibling, device_id_type=MESH)
            cp.start()
            cps.append(cp)
        for cp in cps:
            cp.wait()
        for lc in local:
            lc.wait()

    return pl.pallas_call(
        body, name="join_halves",
        out_shape=[SDS((2 * h.shape[0], h.shape[1]), h.dtype) for h in halves],
        in_specs=[_any()] * n, out_specs=[_any()] * n,
        scratch_shapes=[pltpu.SemaphoreType.DMA((n,)), pltpu.SemaphoreType.DMA((n,)), pltpu.SemaphoreType.DMA((n,))],
    )(*halves)


def _reduce_scatter(parts):
    c = lax.axis_index("c")
    from_sibling = _swap_halves("swap_halves", parts)
    chip_sums = []
    for w, (p, other) in enumerate(zip(parts, from_sibling)):
        nq, rows, cols = p.shape
        h = rows // 2
        mine = lax.dynamic_slice_in_dim(p, c * h, h, axis=1)
        chip_sums.append(_add2(f"chip_sum_{w}", mine.reshape(nq * h, cols), other.reshape(nq * h, cols)).reshape(nq, h, cols))
    arrived = _scatter_to_chips(chip_sums)
    halves = [_ordered_sum(f"chip_order_sum_{w}", a) for w, a in enumerate(arrived)]
    return _join_halves(halves)


def _pad_rows(a, rows):
    return jnp.concatenate([a, jnp.zeros((rows - a.shape[0],) + a.shape[1:], a.dtype)], axis=0)


def _pack_small(dmod, g1, g2, gf, b_in, ln_g, ln_b, conv_b, gn_g, gn_b, ga, gb, sb, cw32, sw):
    v512 = jnp.concatenate([ln_g, ln_b, conv_b, gn_g, gn_b, ga, gb, jnp.zeros((1, DA), F32)], axis=1).reshape(4, D)
    rows = [dmod.reshape(8, D), g1, g2, gf, b_in.reshape(2, D), v512, sb.reshape(1, D), cw32.reshape(16, D),
            sw.reshape(CHUNK, D)]
    packed = jnp.concatenate(rows, axis=0)
    return _pad_rows(packed, PK_ROWS)


def _unpack_small(p):
    v512 = p[PK_V512:PK_V512 + 4].reshape(1, 8 * DA)
    pieces = [v512[:, k * DA:(k + 1) * DA] for k in range(7)]
    return dict(
        dmod=p[PK_DMOD:PK_DMOD + 8].reshape(1, 8 * D), norm1_g=p[PK_G1:PK_G1 + 1], norm2_g=p[PK_G2:PK_G2 + 1],
        norm_f_g=p[PK_GF:PK_GF + 1], b_in=p[PK_BIN:PK_BIN + 2].reshape(1, 2 * D),
        a_ln_g=pieces[0], a_ln_b=pieces[1], b_conv_b=pieces[2], b_gn_g=pieces[3], b_gn_b=pieces[4],
        out_norm_a_g=pieces[5], out_norm_b_g=pieces[6],
        a_spatial_b=p[PK_SB:PK_SB + 1].reshape(N_HEADS, CHUNK),
        b_conv_w=p[PK_CW:PK_CW + 16].reshape(HALO, DB),
        a_spatial_w=p[PK_SW:PK_SW + CHUNK].reshape(N_HEADS, CHUNK, CHUNK))


def kernel(x, c, ada_w, ada_b, norm1_g, w_in, b_in, a_ln_g, a_ln_b, a_spatial_w, a_spatial_b, b_conv_w, b_conv_b, b_gn_g, b_gn_b, out_norm_a_g, out_norm_b_g, w_out, norm2_g, w_ffn_in, w_ffn_out, ada_f_w, ada_f_b, norm_f_g, loss_target, m_ada_w, m_ada_b, m_norm1_g, m_w_in, m_b_in, m_a_ln_g, m_a_ln_b, m_a_spatial_w, m_a_spatial_b, m_b_conv_w, m_b_conv_b, m_b_gn_g, m_b_gn_b, m_out_norm_a_g, m_out_norm_b_g, m_w_out, m_norm2_g, m_w_ffn_in, m_w_ffn_out, m_ada_f_w, m_ada_f_b, m_norm_f_g, v_ada_w, v_ada_b, v_norm1_g, v_w_in, v_b_in, v_a_ln_g, v_a_ln_b, v_a_spatial_w, v_a_spatial_b, v_b_conv_w, v_b_conv_b, v_b_gn_g, v_b_gn_b, v_out_norm_a_g, v_out_norm_b_g, v_w_out, v_norm2_g, v_w_ffn_in, v_w_ffn_out, v_ada_f_w, v_ada_f_b, v_norm_f_g):
    mx, my, mc = _place()
    me = 4 * mx + 2 * my + mc
    myq = 2 * mx + my
    xs = x[0]
    target = loss_target[0]
    s = xs.shape[0]
    ada_w_cat = jnp.concatenate([ada_w[0], ada_f_w], axis=1)
    n_ada = ada_w.shape[2]

    cw_shard = _pad_rows(b_conv_w[0], HALO)
    first = _all_gather8("gather_c", jnp.concatenate([c.reshape(8, LANES), cw_shard], axis=0))
    c_all = first[:, 0:8, :].reshape(8, D)
    conv_w = jnp.concatenate([first[4 * (q // 2) + 2 * (q % 2), 8:8 + HALO, :] for q in range(NQ)], axis=1)
    cond_part = _cond_partial(c_all, ada_w_cat)
    cond_all = _all_gather8("gather_cond", cond_part)
    cond_q = [cond_all[4 * (q // 2) + 2 * (q % 2)] for q in range(NQ)]
    mod_all = jnp.concatenate([cq[:, :n_ada] for cq in cond_q] + [cq[:, n_ada:] for cq in cond_q], axis=1)
    mod = lax.dynamic_slice_in_dim(mod_all, me, 1, axis=0)
    mods = jnp.concatenate([ada_b, ada_f_b.reshape(1, 2 * D)], axis=1)

    w_in4, w_out4, w_ffn_in4, w_ffn_out4 = _gather_weights(
        [w_in[0].astype(MM_DTYPE), w_out[0].astype(MM_DTYPE), w_ffn_in[0].astype(MM_DTYPE), w_ffn_out[0].astype(MM_DTYPE)])
    w_out_f = w_out4.reshape(D, D)
    w_ffn_out_f = w_ffn_out4.reshape(DFF, D)

    causal = jnp.tril(jnp.ones((CHUNK, CHUNK), dtype=bool))
    wm_f = jnp.where(causal[None], a_spatial_w[0], 0.0)
    wm = wm_f.astype(MM_DTYPE)
    wmt = jnp.swapaxes(wm_f, 1, 2).astype(MM_DTYPE)
    bst = jnp.repeat(a_spatial_b[0].T, HALF, axis=1)

    z, x1, yb1, y = _mixer_fwd(xs, mod, mods, norm1_g, w_in4, b_in, a_ln_g, a_ln_b, wm, bst, conv_w, b_conv_b,
                               b_gn_g, b_gn_b, out_norm_a_g, out_norm_b_g, w_out_f)
    g, up, h2, dx2, acc_f = _ffn_fwd(x1, target, mod, mods, norm2_g, norm_f_g, w_ffn_in4, w_ffn_out_f)
    loss = lax.psum(acc_f[4, 0], ("x", "y", "c"))

    dff, a_act, dx1, acc_2 = _ffn_bwd(dx2, x1, g, up, mod, mods, norm2_g, w_ffn_in4, w_ffn_out_f)
    gw_ffn_in4 = _grad_matmul("grad_w_ffn_in", h2, dff, D, PW_FF, True)
    gm2 = _grad_matmul("grad_w_ffn_out", a_act, dx2, PW_FF, D, False)
    modv = mod + mods
    gw_ffn_out, dgate2 = _gate_finalize("gate2_finalize", gm2, w_ffn_out_f, modv[:, 5 * D:6 * D])
    gx, dz, h, acc_1, acc_bin, acc_5, acc_cw, acc_sw, acc_sb = _mixer_bwd(
        dx1, xs, z, yb1, mod, mods, norm1_g, w_in4, a_ln_g, a_ln_b, wm, wmt, bst, conv_w, b_gn_g, b_gn_b,
        out_norm_a_g, out_norm_b_g, w_out_f)
    gw_in4 = _grad_matmul("grad_w_in", h, dz, D, PW_IN, True)
    gm1 = _grad_matmul("grad_w_out", y, dx1, D, D, False)
    gw_out, dgate1 = _gate_finalize("gate1_finalize", gm1, w_out_f, modv[:, 2 * D:3 * D])

    g_w_in, g_w_out, g_w_ffn_in, g_w_ffn_out = _reduce_scatter(
        [gw_in4, gw_out.reshape(NQ, D // NQ, D), gw_ffn_in4, gw_ffn_out.reshape(NQ, DFF // NQ, D)])

    dmod = jnp.concatenate([acc_1[0:1], acc_1[1:2], dgate1[0:1], acc_2[0:1], acc_2[1:2], dgate2[0:1],
                            acc_f[0:1], acc_f[1:2]], axis=1)
    sw_grad = jnp.where(causal[None], acc_sw, 0.0)
    sb_grad = acc_sb[:, ::HALF].T
    packed = _pack_small(dmod, acc_1[2:3], acc_2[2:3], acc_f[2:3], acc_bin[0:1], acc_5[2:3], acc_5[3:4], acc_5[6:7],
                         acc_5[4:5], acc_5[5:6], acc_5[0:1], acc_5[1:2], sb_grad, acc_cw, sw_grad)
    gathered = _all_gather8("gather_small_grads", packed)
    small = _unpack_small(_ordered_sum("small_grad_sum", gathered))
    dmod_all = gathered[:, PK_DMOD:PK_DMOD + 8, :].reshape(8, 8 * D)
    dmod_cols = jnp.concatenate([lax.dynamic_slice_in_dim(dmod_all, myq * n_ada, n_ada, axis=1),
                                 lax.dynamic_slice_in_dim(dmod_all, 6 * D + myq * PW_IN, PW_IN, axis=1)], axis=1)
    g_ada_cat = _cond_grad(c_all, dmod_cols)

    grads = dict(
        ada_w=g_ada_cat[:, :n_ada], ada_b=small["dmod"][:, :6 * D], norm1_g=small["norm1_g"], w_in=g_w_in,
        b_in=small["b_in"], a_ln_g=small["a_ln_g"], a_ln_b=small["a_ln_b"], a_spatial_w=small["a_spatial_w"],
        a_spatial_b=small["a_spatial_b"],
        b_conv_w=lax.dynamic_slice_in_dim(small["b_conv_w"], myq * LANES, LANES, axis=1)[:CONV_W],
        b_conv_b=small["b_conv_b"], b_gn_g=small["b_gn_g"], b_gn_b=small["b_gn_b"],
        out_norm_a_g=small["out_norm_a_g"], out_norm_b_g=small["out_norm_b_g"], w_out=g_w_out,
        norm2_g=small["norm2_g"], w_ffn_in=g_w_ffn_in, w_ffn_out=g_w_ffn_out, ada_f_w=g_ada_cat[:, n_ada:],
        ada_f_b=small["dmod"][:, 6 * D:], norm_f_g=small["norm_f_g"])

    weights = dict(ada_w=ada_w, ada_b=ada_b, norm1_g=norm1_g, w_in=w_in, b_in=b_in, a_ln_g=a_ln_g, a_ln_b=a_ln_b,
                   a_spatial_w=a_spatial_w, a_spatial_b=a_spatial_b, b_conv_w=b_conv_w, b_conv_b=b_conv_b, b_gn_g=b_gn_g,
                   b_gn_b=b_gn_b, out_norm_a_g=out_norm_a_g, out_norm_b_g=out_norm_b_g, w_out=w_out, norm2_g=norm2_g,
                   w_ffn_in=w_ffn_in, w_ffn_out=w_ffn_out, ada_f_w=ada_f_w, ada_f_b=ada_f_b, norm_f_g=norm_f_g)
    m_in = dict(ada_w=m_ada_w, ada_b=m_ada_b, norm1_g=m_norm1_g, w_in=m_w_in, b_in=m_b_in, a_ln_g=m_a_ln_g, a_ln_b=m_a_ln_b,
                a_spatial_w=m_a_spatial_w, a_spatial_b=m_a_spatial_b, b_conv_w=m_b_conv_w, b_conv_b=m_b_conv_b,
                b_gn_g=m_b_gn_g, b_gn_b=m_b_gn_b, out_norm_a_g=m_out_norm_a_g, out_norm_b_g=m_out_norm_b_g, w_out=m_w_out,
                norm2_g=m_norm2_g, w_ffn_in=m_w_ffn_in, w_ffn_out=m_w_ffn_out, ada_f_w=m_ada_f_w, ada_f_b=m_ada_f_b,
                norm_f_g=m_norm_f_g)
    v_in = dict(ada_w=v_ada_w, ada_b=v_ada_b, norm1_g=v_norm1_g, w_in=v_w_in, b_in=v_b_in, a_ln_g=v_a_ln_g, a_ln_b=v_a_ln_b,
                a_spatial_w=v_a_spatial_w, a_spatial_b=v_a_spatial_b, b_conv_w=v_b_conv_w, b_conv_b=v_b_conv_b,
                b_gn_g=v_b_gn_g, b_gn_b=v_b_gn_b, out_norm_a_g=v_out_norm_a_g, out_norm_b_g=v_out_norm_b_g, w_out=v_w_out,
                norm2_g=v_norm2_g, w_ffn_in=v_w_ffn_in, w_ffn_out=v_w_ffn_out, ada_f_w=v_ada_f_w, ada_f_b=v_ada_f_b,
                norm_f_g=v_norm_f_g)
    names = list(weights)
    big = ("ada_w", "w_in", "w_out", "w_ffn_in", "w_ffn_out", "ada_f_w")

    def flat2(a):
        return a.reshape(-1, a.shape[-1])

    delta, new_m, new_v = {}, {}, {}
    for nm in big:
        shape = weights[nm].shape
        grads[nm] = grads[nm].reshape(shape)
        d_, m_, v_ = _adamw("adamw_" + nm, flat2(weights[nm]), flat2(grads[nm]), flat2(m_in[nm]), flat2(v_in[nm]))
        delta[nm], new_m[nm], new_v[nm] = d_.reshape(shape), m_.reshape(shape), v_.reshape(shape)

    small_names = [nm for nm in names if nm not in big]
    sizes = [math.prod(weights[nm].shape) for nm in small_names]
    total = sum(sizes)
    rows = -(-total // (8 * D)) * 8

    def pack(tree):
        flat = jnp.concatenate([tree[nm].reshape(-1) for nm in small_names] + [jnp.ones((rows * D - total,), F32)])
        return flat.reshape(rows, D)

    for nm in small_names:
        grads[nm] = grads[nm].reshape(weights[nm].shape)
    d_s, m_s, v_s = _adamw("adamw_small", pack(weights), pack(grads), pack(m_in), pack(v_in))
    off = 0
    for nm, size in zip(small_names, sizes):
        shape = weights[nm].shape
        delta[nm] = d_s.reshape(-1)[off:off + size].reshape(shape)
        new_m[nm] = m_s.reshape(-1)[off:off + size].reshape(shape)
        new_v[nm] = v_s.reshape(-1)[off:off + size].reshape(shape)
        off += size

    grad_x = gx.reshape(x.shape)
    return (loss, grad_x, *[grads[nm] for nm in names], *[delta[nm] for nm in names],
            *[new_m[nm] for nm in names], *[new_v[nm] for nm in names])
```

```python
import functools
import math

import jax
import jax.numpy as jnp
from jax import lax
from jax.experimental import pallas as pl
from jax.experimental.pallas import tpu as pltpu

F32 = jnp.float32
MM_DTYPE = jnp.bfloat16
WIRE_DTYPE = jnp.bfloat16
SDS = jax.ShapeDtypeStruct
MESH = pl.DeviceIdType.MESH

D = 1024
DA = 512
DB = 512
NQ = 4
PW_IN = 512
DFF = 2816
PW_FF = 1408
CHUNK = 128
N_HEADS = 8
CONV_W = 31
HALO = 32
CONV_ROWS = 64
EPS = 1e-6
LANES = 128
HALF = 64

ROW_TILE = 256
GRAD_ROW_TILE = 512
VMEM_LIMIT = 60 * 1024 * 1024

ADAM_LR = 0.001
ADAM_B1 = 0.9
ADAM_B2 = 0.999
ADAM_EPS = 1e-08
ADAM_WD = 0.01
ADAM_STEP = 10

PK_DMOD = 0
PK_G1 = 8
PK_G2 = 9
PK_GF = 10
PK_BIN = 11
PK_V512 = 13
PK_SB = 17
PK_CW = 18
PK_SW = 34
PK_ROWS = 168


def _dot(a, b):
    return jnp.dot(a, b, preferred_element_type=F32)


def _dot_nt(a, b):
    return lax.dot_general(a, b, (((1,), (1,)), ((), ())), preferred_element_type=F32)


def _dot_tn(a, b):
    return lax.dot_general(a, b, (((0,), (0,)), ((), ())), preferred_element_type=F32)


def _rowsum(x):
    return jnp.sum(x, axis=-1, keepdims=True)


def _colsum(x):
    return jnp.sum(x, axis=0, keepdims=True)


def _group_sum(x):
    rows, width = x.shape
    lo_mask = lax.broadcasted_iota(jnp.int32, (rows, LANES), 1) < HALF
    outs = []
    for jb in range(width // LANES):
        xb = x[:, jb * LANES:(jb + 1) * LANES]
        lo = _rowsum(jnp.where(lo_mask, xb, 0.0))
        hi = _rowsum(jnp.where(lo_mask, 0.0, xb))
        outs.append(jnp.where(lo_mask, lo, hi))
    return jnp.concatenate(outs, axis=-1)


def _sigmoid(x):
    return 1.0 / (1.0 + jnp.exp(-x))


def _gelu_parts(u):
    cdf = 0.5 * (1.0 + lax.erf(u * (1.0 / math.sqrt(2.0))))
    pdf = jnp.exp(-0.5 * u * u) * (1.0 / math.sqrt(2.0 * math.pi))
    return u * cdf, cdf + u * pdf


def _whole_vmem():
    return pl.BlockSpec(memory_space=pltpu.VMEM)


def _params(*semantics):
    return pltpu.CompilerParams(dimension_semantics=semantics, vmem_limit_bytes=VMEM_LIMIT)


def _mod_rows(mod_ref, modb_ref, first, count):
    m = mod_ref[...] + modb_ref[...]
    return [m[:, (first + k) * D:(first + k + 1) * D] for k in range(count)]


def _mixer_recompute(z_parts, lng, lnb, wm_ref, bst_ref, mix_ref):
    u, v, val, gate = z_parts
    rows = u.shape[0]
    gu, dgu = _gelu_parts(u)
    gv, dgv = _gelu_parts(v)
    mu = _rowsum(gv) * (1.0 / DA)
    vc = gv - mu
    rs = lax.rsqrt(_rowsum(vc * vc) * (1.0 / DA) + EPS)
    vhat = vc * rs
    vl = vhat * lng + lnb
    vlb = vl.astype(MM_DTYPE)
    lo_mask = lax.broadcasted_iota(jnp.int32, (CHUNK, LANES), 1) < HALF
    for ck in range(rows // CHUNK):
        for jb in range(DA // LANES):
            blk = vlb[ck * CHUNK:(ck + 1) * CHUNK, jb * LANES:(jb + 1) * LANES]
            a = _dot(wm_ref[2 * jb], blk)
            b = _dot(wm_ref[2 * jb + 1], blk)
            mix_ref[ck * CHUNK:(ck + 1) * CHUNK, jb * LANES:(jb + 1) * LANES] = (
                jnp.where(lo_mask, a, b) + bst_ref[:, jb * LANES:(jb + 1) * LANES])
    mixed = mix_ref[...]
    sg = _sigmoid(gate)
    yb0 = val * sg
    return dict(gu=gu, dgu=dgu, dgv=dgv, rs=rs, vhat=vhat, vlb=vlb, mixed=mixed, sg=sg, yb0=yb0)


def _conv_branch_tail(yb1, gng, gnb):
    gm = _group_sum(yb1) * (1.0 / HALF)
    gc = yb1 - gm
    grs = lax.rsqrt(_group_sum(gc * gc) * (1.0 / HALF) + EPS)
    ghat = gc * grs
    yb2 = ghat * gng + gnb
    s2 = _sigmoid(yb2)
    return dict(grs=grs, ghat=ghat, yb2=yb2, s2=s2, y_b=yb2 * s2)


def _shifted_copies(e_ref, sh_ref):
    n = sh_ref.shape[1]
    for b in range(1, 8):
        sh_ref[b - 1] = e_ref[pl.ds(b, n), :]


def _window(e_ref, sh_ref, offset, r0, nrows, cols):
    a, b = divmod(offset, 8)
    if b == 0:
        return e_ref[pl.ds(r0 + 8 * a, nrows), cols]
    return sh_ref[b - 1, pl.ds(r0 + 8 * a, nrows), cols]


def _conv_taps(e_ref, sh_ref, cw_ref, out_ref, ts, first_offset, flip, bias_ref=None, other_ref=None, tap_acc_ref=None):
    groups = CONV_ROWS // 8
    for cb in range(DB // LANES):
        cols = slice(cb * LANES, (cb + 1) * LANES)
        tap_acc = [jnp.zeros((8, LANES), F32) for _ in range(CONV_W)]
        for rb in range(ts // CONV_ROWS):
            r0 = rb * CONV_ROWS
            acc = jnp.zeros((CONV_ROWS, LANES), F32)
            if bias_ref is not None:
                acc = acc + bias_ref[:, cols]
            if other_ref is not None:
                other = other_ref[r0:r0 + CONV_ROWS, cols]
            for j in range(CONV_W):
                k = CONV_W - 1 - j if flip else j
                win = _window(e_ref, sh_ref, first_offset + j, r0, CONV_ROWS, cols)
                acc = acc + win * cw_ref[k:k + 1, cols]
                if other_ref is not None:
                    tap_acc[k] = tap_acc[k] + jnp.sum((other * win).reshape(groups, 8, LANES), axis=0)
            out_ref[r0:r0 + CONV_ROWS, cols] = acc
        if other_ref is not None:
            for k in range(CONV_W):
                tap_acc_ref[k:k + 1, cols] += _colsum(tap_acc[k])


def _gather_plan(ins, outs, send_sems, recv_sems):
    x, y, c = _place()
    sibling = (x, y, 1 - c)
    chips = _other_chips(x, y)
    myq = 2 * x + y

    def copy(w, k, q, hc, to, src=None):
        rows = ins[w].shape[0]
        dst = outs[w].at[q, pl.ds(hc * (rows // 2), rows // 2)]
        return pltpu.make_async_remote_copy(
            src_ref=dst if src is None else src, dst_ref=dst,
            send_sem=send_sems.at[w, k], recv_sem=recv_sems.at[w, k], device_id=to, device_id_type=MESH)

    def send():
        for w in range(len(ins)):
            rows = ins[w].shape[0]
            src = ins[w].at[pl.ds(c * (rows // 2), rows // 2)]
            for j, chip in enumerate(chips):
                copy(w, j, myq, c, (*chip, c), src=src).start()

    def forward():
        for w in range(len(ins)):
            for j, (qx, qy) in enumerate(chips):
                copy(w, j, 2 * qx + qy, c, sibling).wait_recv()
                copy(w, 3 + j, 2 * qx + qy, c, sibling).start()

    def finish():
        for w in range(len(ins)):
            for j, (qx, qy) in enumerate(chips):
                copy(w, 3 + j, 2 * qx + qy, 1 - c, sibling).wait_recv()
        for w in range(len(ins)):
            for k, (qx, qy) in enumerate(chips + chips):
                copy(w, k, 2 * qx + qy, c, sibling).wait_send()

    return send, forward, finish


def _mixer_fwd(x, mod, mods, norm1_g, w_in4, b_in, ln_g, ln_b, wm, bst, conv_w, conv_b, gn_g, gn_b, ga, gb, w_out,
               ffn_shards):
    s = x.shape[0]
    ts = min(ROW_TILE, s)
    nt = s // ts
    n_sh = len(ffn_shards)

    def body(x_ref, mod_ref, modb_ref, g1_ref, w4_ref, bin_ref, lng_ref, lnb_ref, wm_ref, bst_ref, cw_ref, cb_ref,
             gng_ref, gnb_ref, ga_ref, gb_ref, wout_ref, *rest):
        shard_refs, rest = rest[:n_sh], rest[n_sh:]
        z_ref, x1_ref, yb1_ref, y_ref = rest[:4]
        full_refs, rest = rest[4:4 + n_sh], rest[4 + n_sh:]
        e_ref, sh_ref, mix_ref, send_sems, recv_sems = rest
        i = pl.program_id(0)
        send, forward, finish = _gather_plan(shard_refs, full_refs, send_sems, recv_sems)

        @pl.when(i == 0)
        def _():
            send()
            e_ref[0:HALO, :] = jnp.zeros((HALO, DB), F32)

        @pl.when(i == (3 * nt) // 4)
        def _():
            forward()

        shift1, scale1, gate1 = _mod_rows(mod_ref, modb_ref, 0, 3)
        x_t = x_ref[...]
        r1 = lax.rsqrt(_rowsum(x_t * x_t) * (1.0 / D) + EPS)
        h = (x_t * r1 * g1_ref[...]) * (1.0 + scale1) + shift1
        hb = h.astype(MM_DTYPE)
        z_parts = []
        for q in range(NQ):
            zq = _dot(hb, w4_ref[q]) + bin_ref[:, q * PW_IN:(q + 1) * PW_IN]
            z_ref[:, q * PW_IN:(q + 1) * PW_IN] = zq
            z_parts.append(zq)
        r = _mixer_recompute(z_parts, lng_ref[...], lnb_ref[...], wm_ref, bst_ref, mix_ref)
        y_a = r["gu"] * r["mixed"]
        e_ref[HALO:HALO + ts, :] = r["yb0"]
        _shifted_copies(e_ref, sh_ref)
        _conv_taps(e_ref, sh_ref, cw_ref, yb1_ref, ts, HALO - (CONV_W - 1), False, bias_ref=cb_ref)
        e_ref[0:HALO, :] = e_ref[ts:ts + HALO, :]
        t = _conv_branch_tail(yb1_ref[...], gng_ref[...], gnb_ref[...])
        ra = lax.rsqrt(_rowsum(y_a * y_a) * (1.0 / DA) + EPS)
        rb = lax.rsqrt(_rowsum(t["y_b"] * t["y_b"]) * (1.0 / DB) + EPS)
        yan = (y_a * ra * ga_ref[...]).astype(MM_DTYPE)
        ybn = (t["y_b"] * rb * gb_ref[...]).astype(MM_DTYPE)
        y_ref[:, 0:DA] = yan
        y_ref[:, DA:D] = ybn
        o1 = _dot(yan, wout_ref[0:DA, :]) + _dot(ybn, wout_ref[DA:D, :])
        x1_ref[...] = x_t + gate1 * o1

        @pl.when(i == nt - 1)
        def _():
            finish()

    row = lambda w: pl.BlockSpec((ts, w), lambda i: (i, 0))
    full = lambda a: pl.BlockSpec(a.shape, lambda i: (0,) * a.ndim)
    return pl.pallas_call(
        body, name="mixer_fwd", grid=(nt,),
        in_specs=[row(D), full(mod), full(mods), full(norm1_g), _whole_vmem(), full(b_in), full(ln_g), full(ln_b),
                  _whole_vmem(), full(bst), full(conv_w), full(conv_b), full(gn_g), full(gn_b), full(ga), full(gb),
                  _whole_vmem()] + [_any()] * n_sh,
        out_specs=[row(4 * PW_IN), row(D), row(DB), row(D)] + [_any()] * n_sh,
        out_shape=[SDS((s, 4 * PW_IN), F32), SDS((s, D), F32), SDS((s, DB), F32), SDS((s, D), MM_DTYPE)]
        + [SDS((NQ,) + w.shape, w.dtype) for w in ffn_shards],
        scratch_shapes=[pltpu.VMEM((ts + HALO, DB), F32), pltpu.VMEM((7, ts + HALO - 8, DB), F32), pltpu.VMEM((ts, DA), F32),
                        pltpu.SemaphoreType.DMA((n_sh, 6)), pltpu.SemaphoreType.DMA((n_sh, 6))],
        compiler_params=_params("arbitrary"),
    )(x, mod, mods, norm1_g, w_in4, b_in, ln_g, ln_b, wm, bst, conv_w, conv_b, gn_g, gn_b, ga, gb, w_out, *ffn_shards)


def _ffn_fwd(x1, target, mod, mods, norm2_g, norm_f_g, w_ffn_in4, w_ffn_out):
    s = x1.shape[0]
    ts = min(ROW_TILE, s)
    nt = s // ts

    def body(x1_ref, tgt_ref, mod_ref, modb_ref, g2_ref, gf_ref, wf_ref, wo_ref,
             g_ref, up_ref, h2_ref, dx2_ref, acc_ref):
        i = pl.program_id(0)

        @pl.when(i == 0)
        def _():
            acc_ref[...] = jnp.zeros(acc_ref.shape, F32)

        shift2, scale2, gate2, shift_f, scale_f = _mod_rows(mod_ref, modb_ref, 3, 5)
        x1_t = x1_ref[...]
        r2 = lax.rsqrt(_rowsum(x1_t * x1_t) * (1.0 / D) + EPS)
        h2 = (x1_t * r2 * g2_ref[...]) * (1.0 + scale2) + shift2
        h2b = h2.astype(MM_DTYPE)
        h2_ref[...] = h2b
        o2 = jnp.zeros((ts, D), F32)
        for p in range(2):
            g = _dot(h2b, wf_ref[p])
            up = _dot(h2b, wf_ref[2 + p])
            g_ref[:, p * PW_FF:(p + 1) * PW_FF] = g.astype(MM_DTYPE)
            up_ref[:, p * PW_FF:(p + 1) * PW_FF] = up.astype(MM_DTYPE)
            a = (g * _sigmoid(g) * up).astype(MM_DTYPE)
            o2 = o2 + _dot(a, wo_ref[p * PW_FF:(p + 1) * PW_FF, :])
        x2 = x1_t + gate2 * o2
        rf = lax.rsqrt(_rowsum(x2 * x2) * (1.0 / D) + EPS)
        gf = gf_ref[...]
        nf = x2 * rf * gf
        err = nf * (1.0 + scale_f) + shift_f - tgt_ref[...]
        d_out = err * (1.0 / D)
        d_nf = d_out * (1.0 + scale_f)
        t = d_nf * gf
        dx2_ref[...] = rf * t - x2 * (rf * rf * rf) * (_rowsum(t * x2) * (1.0 / D))
        acc_ref[0:1, :] += _colsum(d_out)
        acc_ref[1:2, :] += _colsum(d_out * nf)
        acc_ref[2:3, :] += _colsum(d_nf * x2 * rf)
        acc_ref[3:4, :] += _colsum(err * err)

        @pl.when(i == nt - 1)
        def _():
            acc_ref[4:5, :] = jnp.zeros((1, D), F32) + _rowsum(acc_ref[3:4, :]) * (0.5 / D)

    row = lambda w: pl.BlockSpec((ts, w), lambda i: (i, 0))
    full = lambda a: pl.BlockSpec(a.shape, lambda i: (0,) * a.ndim)
    return pl.pallas_call(
        body, name="ffn_fwd", grid=(nt,),
        in_specs=[row(D), row(D), full(mod), full(mods), full(norm2_g), full(norm_f_g), _whole_vmem(), _whole_vmem()],
        out_specs=[row(DFF), row(DFF), row(D), row(D), pl.BlockSpec((8, D), lambda i: (0, 0))],
        out_shape=[SDS((s, DFF), MM_DTYPE), SDS((s, DFF), MM_DTYPE), SDS((s, D), MM_DTYPE), SDS((s, D), F32),
                   SDS((8, D), F32)],
        compiler_params=_params("arbitrary"),
    )(x1, target, mod, mods, norm2_g, norm_f_g, w_ffn_in4, w_ffn_out)


def _ffn_bwd(dx2, x1, g, up, mod, mods, norm2_g, w_ffn_in4, w_ffn_out):
    s = x1.shape[0]
    ts = min(ROW_TILE, s)
    nt = s // ts

    def body(dx2_ref, x1_ref, g_ref, up_ref, mod_ref, modb_ref, g2_ref, wf_ref, wo_ref,
             dff_ref, a_ref, dx1_ref, acc_ref):
        @pl.when(pl.program_id(0) == 0)
        def _():
            acc_ref[...] = jnp.zeros(acc_ref.shape, F32)

        shift2, scale2, gate2 = _mod_rows(mod_ref, modb_ref, 3, 3)
        dx2_t = dx2_ref[...]
        do2 = (dx2_t * gate2).astype(MM_DTYPE)
        dh2 = jnp.zeros((ts, D), F32)
        for p in range(2):
            da = _dot_nt(do2, wo_ref[p * PW_FF:(p + 1) * PW_FF, :])
            gp = g_ref[:, p * PW_FF:(p + 1) * PW_FF].astype(F32)
            upp = up_ref[:, p * PW_FF:(p + 1) * PW_FF].astype(F32)
            sg = _sigmoid(gp)
            silu = gp * sg
            a_ref[:, p * PW_FF:(p + 1) * PW_FF] = (silu * upp).astype(MM_DTYPE)
            dg = (da * upp * (sg * (1.0 + gp * (1.0 - sg)))).astype(MM_DTYPE)
            dup = (da * silu).astype(MM_DTYPE)
            dff_ref[:, p * PW_FF:(p + 1) * PW_FF] = dg
            dff_ref[:, DFF + p * PW_FF:DFF + (p + 1) * PW_FF] = dup
            dh2 = dh2 + _dot_nt(dg, wf_ref[p]) + _dot_nt(dup, wf_ref[2 + p])
        x1_t = x1_ref[...]
        r2 = lax.rsqrt(_rowsum(x1_t * x1_t) * (1.0 / D) + EPS)
        g2 = g2_ref[...]
        xr = x1_t * r2
        dn2 = dh2 * (1.0 + scale2)
        t = dn2 * g2
        dx1_ref[...] = dx2_t + r2 * t - x1_t * (r2 * r2 * r2) * (_rowsum(t * x1_t) * (1.0 / D))
        acc_ref[0:1, :] += _colsum(dh2)
        acc_ref[1:2, :] += _colsum(dh2 * (xr * g2))
        acc_ref[2:3, :] += _colsum(dn2 * xr)

    row = lambda w: pl.BlockSpec((ts, w), lambda i: (i, 0))
    full = lambda a: pl.BlockSpec(a.shape, lambda i: (0,) * a.ndim)
    return pl.pallas_call(
        body, name="ffn_bwd", grid=(nt,),
        in_specs=[row(D), row(D), row(DFF), row(DFF), full(mod), full(mods), full(norm2_g), _whole_vmem(), _whole_vmem()],
        out_specs=[row(2 * DFF), row(DFF), row(D), pl.BlockSpec((8, D), lambda i: (0, 0))],
        out_shape=[SDS((s, 2 * DFF), MM_DTYPE), SDS((s, DFF), MM_DTYPE), SDS((s, D), F32), SDS((8, D), F32)],
        compiler_params=_params("arbitrary"),
    )(dx2, x1, g, up, mod, mods, norm2_g, w_ffn_in4, w_ffn_out)


def _scatter_plan(ins, outs, send_sems, recv_sems):
    x, y, c = _place()
    chips = _other_chips(x, y)
    myq = 2 * x + y

    def copies():
        return [pltpu.make_async_remote_copy(
            src_ref=ins[w].at[2 * qx + qy], dst_ref=outs[w].at[myq],
            send_sem=send_sems.at[w, j], recv_sem=recv_sems.at[w, j], device_id=(qx, qy, c), device_id_type=MESH)
            for w in range(len(ins)) for j, (qx, qy) in enumerate(chips)]

    def send():
        for cp in copies():
            cp.start()

    def finish():
        for cp in copies():
            cp.wait()

    return send, finish


def _mixer_bwd(dx1, x, z, yb1, mod, mods, norm1_g, w_in4, ln_g, ln_b, wm, wmt, bst, conv_w, gn_g, gn_b, ga, gb, w_out,
               chip_sums):
    s = x.shape[0]
    ts = min(ROW_TILE, s)
    nt = s // ts
    n_cs = len(chip_sums)

    def body(dx1_ref, x_ref, z_ref, yb1_ref, mod_ref, modb_ref, g1_ref, w4_ref, lng_ref, lnb_ref, wm_ref, wmt_ref,
             bst_ref, cw_ref, gng_ref, gnb_ref, ga_ref, gb_ref, wout_ref, *rest):
        cs_refs, rest = rest[:n_cs], rest[n_cs:]
        gx_ref, dz_ref, h_ref, a1_ref, a2_ref, a5_ref, acw_ref, asw_ref, asb_ref = rest[:9]
        arrived_refs, rest = rest[9:9 + n_cs], rest[9 + n_cs:]
        e_ref, sh_ref, mix_ref, dvl_ref, send_sems, recv_sems = rest
        i = pl.program_id(0)
        send, finish = _scatter_plan(cs_refs, arrived_refs, send_sems, recv_sems)

        @pl.when(i == 0)
        def _():
            send()
            e_ref[ts:ts + HALO, :] = jnp.zeros((HALO, DB), F32)
            for r in (a1_ref, a2_ref, a5_ref, acw_ref, asw_ref, asb_ref):
                r[...] = jnp.zeros(r.shape, F32)

        shift1, scale1, gate1 = _mod_rows(mod_ref, modb_ref, 0, 3)
        dx1_t = dx1_ref[...]
        do1 = (dx1_t * gate1).astype(MM_DTYPE)
        d_yan = _dot_nt(do1, wout_ref[0:DA, :])
        d_ybn = _dot_nt(do1, wout_ref[DA:D, :])

        z_parts = [z_ref[:, q * PW_IN:(q + 1) * PW_IN] for q in range(NQ)]
        u, v, val, gate = z_parts
        lng = lng_ref[...]
        r = _mixer_recompute(z_parts, lng, lnb_ref[...], wm_ref, bst_ref, mix_ref)
        gng = gng_ref[...]
        t = _conv_branch_tail(yb1_ref[...], gng, gnb_ref[...])
        y_a = r["gu"] * r["mixed"]
        y_b = t["y_b"]
        ga_v, gb_v = ga_ref[...], gb_ref[...]
        ra = lax.rsqrt(_rowsum(y_a * y_a) * (1.0 / DA) + EPS)
        rb = lax.rsqrt(_rowsum(y_b * y_b) * (1.0 / DB) + EPS)

        a5_ref[0:1, :] += _colsum(d_yan * y_a * ra)
        a5_ref[1:2, :] += _colsum(d_ybn * y_b * rb)
        ta = d_yan * ga_v
        d_ya = ra * ta - y_a * (ra * ra * ra) * (_rowsum(ta * y_a) * (1.0 / DA))
        tb = d_ybn * gb_v
        d_yb = rb * tb - y_b * (rb * rb * rb) * (_rowsum(tb * y_b) * (1.0 / DB))

        d_u = d_ya * r["mixed"] * r["dgu"]
        d_mixed = d_ya * r["gu"]
        dmb = d_mixed.astype(MM_DTYPE)
        lo_mask = lax.broadcasted_iota(jnp.int32, (CHUNK, LANES), 1) < HALF
        zero_blk = jnp.zeros((CHUNK, LANES), MM_DTYPE)
        sb_acc = jnp.zeros((CHUNK, DA), F32)
        for ck in range(ts // CHUNK):
            rows = slice(ck * CHUNK, (ck + 1) * CHUNK)
            sb_acc = sb_acc + d_mixed[rows, :]
            for jb in range(DA // LANES):
                cols = slice(jb * LANES, (jb + 1) * LANES)
                dm_blk = dmb[rows, cols]
                vl_blk = r["vlb"][rows, cols]
                da_ = _dot(wmt_ref[2 * jb], dm_blk)
                db_ = _dot(wmt_ref[2 * jb + 1], dm_blk)
                dvl_ref[rows, cols] = jnp.where(lo_mask, da_, db_)
                asw_ref[2 * jb] += _dot_nt(jnp.where(lo_mask, dm_blk, zero_blk), vl_blk)
                asw_ref[2 * jb + 1] += _dot_nt(jnp.where(lo_mask, zero_blk, dm_blk), vl_blk)
        asb_ref[...] += sb_acc
        d_vl = dvl_ref[...]
        a5_ref[2:3, :] += _colsum(d_vl * r["vhat"])
        a5_ref[3:4, :] += _colsum(d_vl)
        dvh = d_vl * lng
        d_gv = r["rs"] * (dvh - _rowsum(dvh) * (1.0 / DA) - r["vhat"] * (_rowsum(dvh * r["vhat"]) * (1.0 / DA)))
        d_v = d_gv * r["dgv"]

        yb2, s2 = t["yb2"], t["s2"]
        d_yb2 = d_yb * (s2 * (1.0 + yb2 * (1.0 - s2)))
        a5_ref[4:5, :] += _colsum(d_yb2 * t["ghat"])
        a5_ref[5:6, :] += _colsum(d_yb2)
        dgh = d_yb2 * gng
        d_yb1 = t["grs"] * (dgh - _group_sum(dgh) * (1.0 / HALF) - t["ghat"] * (_group_sum(dgh * t["ghat"]) * (1.0 / HALF)))
        a5_ref[6:7, :] += _colsum(d_yb1)
        e_ref[0:ts, :] = d_yb1
        _shifted_copies(e_ref, sh_ref)
        mix_ref[...] = r["yb0"]
        _conv_taps(e_ref, sh_ref, cw_ref, dvl_ref, ts, 0, True, other_ref=mix_ref, tap_acc_ref=acw_ref)
        d_yb0 = dvl_ref[...]
        e_ref[ts:ts + HALO, :] = e_ref[0:HALO, :]
        sg = r["sg"]
        d_val = d_yb0 * sg
        d_gate = d_yb0 * val * sg * (1.0 - sg)

        dh = jnp.zeros((ts, D), F32)
        for q, dzq in enumerate((d_u, d_v, d_val, d_gate)):
            a2_ref[0:1, q * PW_IN:(q + 1) * PW_IN] += _colsum(dzq)
            dzb = dzq.astype(MM_DTYPE)
            dz_ref[:, q * PW_IN:(q + 1) * PW_IN] = dzb
            dh = dh + _dot_nt(dzb, w4_ref[q])
        x_t = x_ref[...]
        r1 = lax.rsqrt(_rowsum(x_t * x_t) * (1.0 / D) + EPS)
        g1 = g1_ref[...]
        xr = x_t * r1
        n1 = xr * g1
        h_ref[...] = (n1 * (1.0 + scale1) + shift1).astype(MM_DTYPE)
        dn1 = dh * (1.0 + scale1)
        t1 = dn1 * g1
        gx_ref[...] = dx1_t + r1 * t1 - x_t * (r1 * r1 * r1) * (_rowsum(t1 * x_t) * (1.0 / D))
        a1_ref[0:1, :] += _colsum(dh)
        a1_ref[1:2, :] += _colsum(dh * n1)
        a1_ref[2:3, :] += _colsum(dn1 * xr)

        @pl.when(i == nt - 1)
        def _():
            asb_ref[...] = _group_sum(asb_ref[...])
            finish()

    row = lambda w: pl.BlockSpec((ts, w), lambda i: (nt - 1 - i, 0))
    full = lambda a: pl.BlockSpec(a.shape, lambda i: (0,) * a.ndim)
    keep = lambda shape: pl.BlockSpec(shape, lambda i: (0,) * len(shape))
    return pl.pallas_call(
        body, name="mixer_bwd", grid=(nt,),
        in_specs=[row(D), row(D), row(4 * PW_IN), row(DB), full(mod), full(mods), full(norm1_g), _whole_vmem(),
                  full(ln_g), full(ln_b), _whole_vmem(), _whole_vmem(), full(bst), full(conv_w), full(gn_g), full(gn_b),
                  full(ga), full(gb), _whole_vmem()] + [_any()] * n_cs,
        out_specs=[row(D), row(4 * PW_IN), row(D), keep((8, D)), keep((8, 4 * PW_IN)), keep((8, DA)),
                   keep((HALO, DB)), keep((N_HEADS, CHUNK, CHUNK)), keep((CHUNK, DA))] + [_any()] * n_cs,
        out_shape=[SDS((s, D), F32), SDS((s, 4 * PW_IN), MM_DTYPE), SDS((s, D), MM_DTYPE), SDS((8, D), F32),
                   SDS((8, 4 * PW_IN), F32), SDS((8, DA), F32), SDS((HALO, DB), F32),
                   SDS((N_HEADS, CHUNK, CHUNK), F32), SDS((CHUNK, DA), F32)]
        + [SDS(p.shape, p.dtype) for p in chip_sums],
        scratch_shapes=[pltpu.VMEM((ts + HALO, DB), F32), pltpu.VMEM((7, ts + HALO - 8, DB), F32),
                        pltpu.VMEM((ts, DA), F32), pltpu.VMEM((ts, DA), F32),
                        pltpu.SemaphoreType.DMA((n_cs, 3)), pltpu.SemaphoreType.DMA((n_cs, 3))],
        compiler_params=_params("arbitrary"),
    )(dx1, x, z, yb1, mod, mods, norm1_g, w_in4, ln_g, ln_b, wm, wmt, bst, conv_w, gn_g, gn_b, ga, gb, w_out, *chip_sums)


def _grad_matmul(name, a, b, ka_tile, nb_tile, pieces):
    s, ka = a.shape
    nb = b.shape[1]
    ts = min(GRAD_ROW_TILE, s)
    nt = s // ts
    nja, njb = ka // ka_tile, nb // nb_tile

    def body(a_ref, b_ref, o_ref):
        @pl.when(pl.program_id(2) == 0)
        def _():
            o_ref[...] = jnp.zeros(o_ref.shape, F32)

        prod = _dot_tn(a_ref[...].astype(MM_DTYPE), b_ref[...].astype(MM_DTYPE))
        o_ref[...] += prod.reshape(o_ref.shape)

    if pieces:
        assert nja == 1
        out_shape = SDS((njb, ka, nb_tile), F32)
        out_spec = pl.BlockSpec((1, ka, nb_tile), lambda ja, jb, i: (jb, 0, 0))
    else:
        out_shape = SDS((ka, nb), F32)
        out_spec = pl.BlockSpec((ka_tile, nb_tile), lambda ja, jb, i: (ja, jb))
    return pl.pallas_call(
        body, name=name, grid=(nja, njb, nt),
        in_specs=[pl.BlockSpec((ts, ka_tile), lambda ja, jb, i: (i, ja)),
                  pl.BlockSpec((ts, nb_tile), lambda ja, jb, i: (i, jb))],
        out_specs=out_spec, out_shape=out_shape,
        compiler_params=_params("parallel", "parallel", "arbitrary"),
    )(a, b)


def _gate_finalize(name, gmat, w, gate):
    k, n = gmat.shape
    kt = 256 if k % 256 == 0 else 352
    nk = k // kt

    def body(g_ref, w_ref, gate_ref, o_ref, dg_ref):
        @pl.when(pl.program_id(0) == 0)
        def _():
            dg_ref[...] = jnp.zeros(dg_ref.shape, F32)

        gm = g_ref[...]
        o_ref[...] = gm * gate_ref[...]
        dg_ref[0:1, :] += _colsum(gm * w_ref[...].astype(F32))

    return pl.pallas_call(
        body, name=name, grid=(nk,),
        in_specs=[pl.BlockSpec((kt, n), lambda i: (i, 0)), pl.BlockSpec((kt, n), lambda i: (i, 0)),
                  pl.BlockSpec((1, n), lambda i: (0, 0))],
        out_specs=[pl.BlockSpec((kt, n), lambda i: (i, 0)), pl.BlockSpec((8, n), lambda i: (0, 0))],
        out_shape=[SDS((k, n), F32), SDS((8, n), F32)],
        compiler_params=_params("arbitrary"),
    )(gmat, w, gate)


def _cond_partial(c_all, w_cat):
    n = w_cat.shape[1]
    nt_cols = 512

    def body(c_ref, w_ref, o_ref):
        c_t = c_ref[...]
        ca = (c_t * _sigmoid(c_t)).astype(MM_DTYPE)
        o_ref[...] = _dot(ca, w_ref[...].astype(MM_DTYPE))

    return pl.pallas_call(
        body, name="cond_partial", grid=(n // nt_cols,),
        in_specs=[pl.BlockSpec((8, D), lambda j: (0, 0)), pl.BlockSpec((D, nt_cols), lambda j: (0, j))],
        out_specs=pl.BlockSpec((8, nt_cols), lambda j: (0, j)), out_shape=SDS((8, n), F32),
        compiler_params=_params("parallel"),
    )(c_all, w_cat)


def _cond_grad(c_all, dmod_cols):
    n = dmod_cols.shape[1]
    nt_cols = 512

    def body(c_ref, d_ref, o_ref):
        c_t = c_ref[...]
        ca = jnp.concatenate([c_t * _sigmoid(c_t), jnp.zeros((8, D), F32)], axis=0).astype(MM_DTYPE)
        dm = jnp.concatenate([d_ref[...], jnp.zeros((8, nt_cols), F32)], axis=0).astype(MM_DTYPE)
        o_ref[...] = _dot_tn(ca, dm)

    return pl.pallas_call(
        body, name="cond_grad", grid=(n // nt_cols,),
        in_specs=[pl.BlockSpec((8, D), lambda j: (0, 0)), pl.BlockSpec((8, nt_cols), lambda j: (0, j))],
        out_specs=pl.BlockSpec((D, nt_cols), lambda j: (0, j)), out_shape=SDS((D, n), F32),
        compiler_params=_params("parallel"),
    )(c_all, dmod_cols)


def _row_tile(rows, cap=256):
    if rows <= cap:
        return rows
    for t in range(cap, 7, -8):
        if rows % t == 0:
            return t
    return rows


def _ordered_sum(name, parts, out_dtype=F32):
    n, rows, cols = parts.shape
    rt = _row_tile(rows)

    def body(p_ref, o_ref):
        acc = p_ref[0].astype(F32)
        for k in range(1, n):
            acc = acc + p_ref[k].astype(F32)
        o_ref[...] = acc.astype(out_dtype)

    return pl.pallas_call(
        body, name=name, grid=(rows // rt,),
        in_specs=[pl.BlockSpec((n, rt, cols), lambda i: (0, i, 0))],
        out_specs=pl.BlockSpec((rt, cols), lambda i: (i, 0)), out_shape=SDS((rows, cols), out_dtype),
        compiler_params=_params("parallel"),
    )(parts)


def _adamw(name, w, g, m, v):
    rows, cols = w.shape
    rt = _row_tile(rows)
    c1 = 1.0 - ADAM_B1 ** ADAM_STEP
    c2 = 1.0 - ADAM_B2 ** ADAM_STEP

    def body(w_ref, g_ref, m_ref, v_ref, d_ref, nm_ref, nv_ref):
        g_t = g_ref[...]
        m_new = ADAM_B1 * m_ref[...] + (1.0 - ADAM_B1) * g_t
        v_new = ADAM_B2 * v_ref[...] + (1.0 - ADAM_B2) * (g_t * g_t)
        nm_ref[...] = m_new
        nv_ref[...] = v_new
        d_ref[...] = -ADAM_LR * ((m_new / c1) / (jnp.sqrt(v_new / c2) + ADAM_EPS) + ADAM_WD * w_ref[...])

    spec = pl.BlockSpec((rt, cols), lambda i: (i, 0))
    out = SDS((rows, cols), F32)
    return pl.pallas_call(body, name=name, grid=(rows // rt,), in_specs=[spec] * 4, out_specs=[spec] * 3,
                          out_shape=[out, out, out], compiler_params=_params("parallel"))(w, g, m, v)


def _place():
    return lax.axis_index("x"), lax.axis_index("y"), lax.axis_index("c")


def _other_chips(x, y):
    return [(1 - x, y), (x, 1 - y), (1 - x, 1 - y)]


def _all_gather8(name, blk):
    m, n = blk.shape

    def body(x_ref, out_ref, send_sems, recv_sems, local_sem):
        x, y, c = _place()
        me, sibling = (x, y, c), (x, y, 1 - c)
        chips = _other_chips(x, y)

        def slot(px, py, pc):
            return out_ref.at[4 * px + 2 * py + pc]

        def copy(k, block, to, src=None):
            return pltpu.make_async_remote_copy(
                src_ref=slot(*block) if src is None else src, dst_ref=slot(*block),
                send_sem=send_sems.at[k], recv_sem=recv_sems.at[k], device_id=to, device_id_type=MESH)

        mine = pltpu.make_async_copy(x_ref, slot(*me), local_sem)
        mine.start()
        first = [copy(0, me, sibling, src=x_ref)]
        first += [copy(1 + j, me, (*chip, c), src=x_ref) for j, chip in enumerate(chips)]
        for cp in first:
            cp.start()
        passed = [copy(4 + j, (*chip, c), sibling) for j, chip in enumerate(chips)]
        for j, chip in enumerate(chips):
            copy(1 + j, (*chip, c), me).wait_recv()
            passed[j].start()
        copy(0, sibling, me).wait_recv()
        for j, chip in enumerate(chips):
            copy(4 + j, (*chip, 1 - c), me).wait_recv()
        for cp in first + passed:
            cp.wait_send()
        mine.wait()

    return pl.pallas_call(
        body, name=name, out_shape=SDS((8, m, n), blk.dtype),
        in_specs=[_whole_vmem()], out_specs=_whole_vmem(),
        scratch_shapes=[pltpu.SemaphoreType.DMA((7,)), pltpu.SemaphoreType.DMA((7,)), pltpu.SemaphoreType.DMA],
        compiler_params=pltpu.CompilerParams(vmem_limit_bytes=VMEM_LIMIT),
    )(blk)


def _any():
    return pl.BlockSpec(memory_space=pl.ANY)


def _gather_weights(shards):
    n = len(shards)

    def body(*refs):
        send, forward, finish = _gather_plan(refs[:n], refs[n:2 * n], *refs[2 * n:])
        send()
        forward()
        finish()

    return pl.pallas_call(
        body, name="gather_weights",
        out_shape=[SDS((NQ,) + s.shape, s.dtype) for s in shards],
        in_specs=[_any()] * n, out_specs=[_any()] * n,
        scratch_shapes=[pltpu.SemaphoreType.DMA((n, 6)), pltpu.SemaphoreType.DMA((n, 6))],
    )(*shards)


def _own_piece(gathered, shard):
    myq = 2 * lax.axis_index("x") + lax.axis_index("y")
    return lax.dynamic_update_slice(gathered, shard[None], (myq,) + (0,) * shard.ndim)


def _swap_halves(name, parts):
    n = len(parts)

    def body(*refs):
        ins, outs = refs[:n], refs[n:2 * n]
        send_sems, recv_sems = refs[2 * n:]
        x, y, c = _place()
        sibling = (x, y, 1 - c)
        cps = []
        for w in range(n):
            rows = ins[w].shape[1]
            src = ins[w].at[:, pl.ds((1 - c) * (rows // 2), rows // 2)]
            cp = pltpu.make_async_remote_copy(src_ref=src, dst_ref=outs[w], send_sem=send_sems.at[w],
                                              recv_sem=recv_sems.at[w], device_id=sibling, device_id_type=MESH)
            cp.start()
            cps.append(cp)
        for cp in cps:
            cp.wait()

    return pl.pallas_call(
        body, name=name,
        out_shape=[SDS((p.shape[0], p.shape[1] // 2, p.shape[2]), p.dtype) for p in parts],
        in_specs=[_any()] * n, out_specs=[_any()] * n,
        scratch_shapes=[pltpu.SemaphoreType.DMA((n,)), pltpu.SemaphoreType.DMA((n,))],
    )(*parts)


def _scatter_to_chips(parts):
    n = len(parts)

    def body(*refs):
        send, finish = _scatter_plan(refs[:n], refs[n:2 * n], *refs[2 * n:])
        send()
        finish()

    return pl.pallas_call(
        body, name="scatter_to_chips",
        out_shape=[SDS(p.shape, p.dtype) for p in parts],
        in_specs=[_any()] * n, out_specs=[_any()] * n,
        scratch_shapes=[pltpu.SemaphoreType.DMA((n, 3)), pltpu.SemaphoreType.DMA((n, 3))],
    )(*parts)


def _own_chip_sum(arrived, chip_sums):
    myq = 2 * lax.axis_index("x") + lax.axis_index("y")
    return lax.dynamic_update_slice(arrived, lax.dynamic_slice_in_dim(chip_sums, myq, 1, axis=0), (myq, 0, 0))


def _join_halves(halves):
    n = len(halves)

    def body(*refs):
        ins, outs = refs[:n], refs[n:2 * n]
        send_sems, recv_sems = refs[2 * n:]
        x, y, c = _place()
        sibling = (x, y, 1 - c)
        cps = []
        for w in range(n):
            h = ins[w].shape[0]
            cp = pltpu.make_async_remote_copy(src_ref=ins[w], dst_ref=outs[w].at[pl.ds(c * h, h)], send_sem=send_sems.at[w],
                                              recv_sem=recv_sems.at[w], device_id=sibling, device_id_type=MESH)
            cp.start()
            cps.append(cp)
        for cp in cps:
            cp.wait()

    joined = pl.pallas_call(
        body, name="join_halves",
        out_shape=[SDS((2 * h.shape[0], h.shape[1]), h.dtype) for h in halves],
        in_specs=[_any()] * n, out_specs=[_any()] * n,
        scratch_shapes=[pltpu.SemaphoreType.DMA((n,)), pltpu.SemaphoreType.DMA((n,))],
    )(*halves)
    c = lax.axis_index("c")
    return [lax.dynamic_update_slice(j, h, (c * h.shape[0], 0)) for j, h in zip(joined, halves)]


def _chip_sum(name, part, other):
    nq, rows, cols = part.shape
    h = rows // 2
    rt = _row_tile(h)
    nb = h // rt

    def body(c_ref, p_ref, o_ref, out_ref):
        out_ref[...] = (p_ref[...] + o_ref[...]).astype(WIRE_DTYPE)

    grid_spec = pltpu.PrefetchScalarGridSpec(
        num_scalar_prefetch=1, grid=(nq, nb),
        in_specs=[pl.BlockSpec((1, rt, cols), lambda q, i, c_ref: (q, c_ref[0] * nb + i, 0)),
                  pl.BlockSpec((1, rt, cols), lambda q, i, c_ref: (q, i, 0))],
        out_specs=pl.BlockSpec((1, rt, cols), lambda q, i, c_ref: (q, i, 0)))
    return pl.pallas_call(
        body, name=name, grid_spec=grid_spec, out_shape=SDS((nq, h, cols), WIRE_DTYPE),
        compiler_params=_params("parallel", "parallel"),
    )(lax.axis_index("c").astype(jnp.int32).reshape(1), part, other)


def _chip_sums(tag, parts):
    from_sibling = _swap_halves("swap_halves_" + tag, parts)
    return [_chip_sum(f"chip_sum_{tag}_{w}", p, o) for w, (p, o) in enumerate(zip(parts, from_sibling))]


def _chip_order_sums(tag, arrived, chip_sums):
    return [_ordered_sum(f"chip_order_sum_{tag}_{w}", _own_chip_sum(a, s)) for w, (a, s) in enumerate(zip(arrived, chip_sums))]


def _pad_rows(a, rows):
    return jnp.pad(a, ((0, rows - a.shape[0]),) + ((0, 0),) * (a.ndim - 1))


def _pack_small(dmod, g1, g2, gf, b_in, ln_g, ln_b, conv_b, gn_g, gn_b, ga, gb, sb, cw32, sw):
    v512 = jnp.concatenate([ln_g, ln_b, conv_b, gn_g, gn_b, ga, gb, jnp.zeros((1, DA), F32)], axis=1).reshape(4, D)
    rows = [dmod.reshape(8, D), g1, g2, gf, b_in.reshape(2, D), v512, sb.reshape(1, D), cw32.reshape(16, D),
            sw.reshape(CHUNK, D)]
    packed = jnp.concatenate(rows, axis=0)
    return _pad_rows(packed, PK_ROWS)


def _unpack_small(p):
    v512 = p[PK_V512:PK_V512 + 4].reshape(1, 8 * DA)
    pieces = [v512[:, k * DA:(k + 1) * DA] for k in range(7)]
    return dict(
        dmod=p[PK_DMOD:PK_DMOD + 8].reshape(1, 8 * D), norm1_g=p[PK_G1:PK_G1 + 1], norm2_g=p[PK_G2:PK_G2 + 1],
        norm_f_g=p[PK_GF:PK_GF + 1], b_in=p[PK_BIN:PK_BIN + 2].reshape(1, 2 * D),
        a_ln_g=pieces[0], a_ln_b=pieces[1], b_conv_b=pieces[2], b_gn_g=pieces[3], b_gn_b=pieces[4],
        out_norm_a_g=pieces[5], out_norm_b_g=pieces[6],
        a_spatial_b=p[PK_SB:PK_SB + 1].reshape(N_HEADS, CHUNK),
        b_conv_w=p[PK_CW:PK_CW + 16].reshape(HALO, DB),
        a_spatial_w=p[PK_SW:PK_SW + CHUNK].reshape(N_HEADS, CHUNK, CHUNK))


def kernel(x, c, ada_w, ada_b, norm1_g, w_in, b_in, a_ln_g, a_ln_b, a_spatial_w, a_spatial_b, b_conv_w, b_conv_b, b_gn_g, b_gn_b, out_norm_a_g, out_norm_b_g, w_out, norm2_g, w_ffn_in, w_ffn_out, ada_f_w, ada_f_b, norm_f_g, loss_target, m_ada_w, m_ada_b, m_norm1_g, m_w_in, m_b_in, m_a_ln_g, m_a_ln_b, m_a_spatial_w, m_a_spatial_b, m_b_conv_w, m_b_conv_b, m_b_gn_g, m_b_gn_b, m_out_norm_a_g, m_out_norm_b_g, m_w_out, m_norm2_g, m_w_ffn_in, m_w_ffn_out, m_ada_f_w, m_ada_f_b, m_norm_f_g, v_ada_w, v_ada_b, v_norm1_g, v_w_in, v_b_in, v_a_ln_g, v_a_ln_b, v_a_spatial_w, v_a_spatial_b, v_b_conv_w, v_b_conv_b, v_b_gn_g, v_b_gn_b, v_out_norm_a_g, v_out_norm_b_g, v_w_out, v_norm2_g, v_w_ffn_in, v_w_ffn_out, v_ada_f_w, v_ada_f_b, v_norm_f_g):
    mx, my, mc = _place()
    me = 4 * mx + 2 * my + mc
    myq = 2 * mx + my
    xs = x[0]
    target = loss_target[0]
    s = xs.shape[0]
    ada_w_cat = jnp.concatenate([ada_w[0], ada_f_w], axis=1)
    n_ada = ada_w.shape[2]

    cw_shard = _pad_rows(b_conv_w[0], HALO)
    first = _all_gather8("gather_c", jnp.concatenate([c.reshape(8, LANES), cw_shard], axis=0))
    c_all = first[:, 0:8, :].reshape(8, D)
    conv_w = jnp.concatenate([first[4 * (q // 2) + 2 * (q % 2), 8:8 + HALO, :] for q in range(NQ)], axis=1)
    cond_part = _cond_partial(c_all, ada_w_cat)
    cond_all = _all_gather8("gather_cond", cond_part)
    cond_q = [cond_all[4 * (q // 2) + 2 * (q % 2)] for q in range(NQ)]
    mod_all = jnp.concatenate([cq[:, :n_ada] for cq in cond_q] + [cq[:, n_ada:] for cq in cond_q], axis=1)
    mod = lax.dynamic_slice_in_dim(mod_all, me, 1, axis=0)
    mods = jnp.concatenate([ada_b, ada_f_b.reshape(1, 2 * D)], axis=1)

    mix_shards = [w_in[0].astype(MM_DTYPE), w_out[0].astype(MM_DTYPE)]
    ffn_shards = [w_ffn_in[0].astype(MM_DTYPE), w_ffn_out[0].astype(MM_DTYPE)]
    w_in4, w_out4 = [_own_piece(g_, s_) for g_, s_ in zip(_gather_weights(mix_shards), mix_shards)]
    w_out_f = w_out4.reshape(D, D)

    causal = jnp.tril(jnp.ones((CHUNK, CHUNK), dtype=bool))
    wm_f = jnp.where(causal[None], a_spatial_w[0], 0.0)
    wm = wm_f.astype(MM_DTYPE)
    wmt = jnp.swapaxes(wm_f, 1, 2).astype(MM_DTYPE)
    bst = jnp.repeat(a_spatial_b[0].T, HALF, axis=1)

    z, x1, yb1, y, w_ffn_in4, w_ffn_out4 = _mixer_fwd(
        xs, mod, mods, norm1_g, w_in4, b_in, a_ln_g, a_ln_b, wm, bst, conv_w, b_conv_b, b_gn_g, b_gn_b,
        out_norm_a_g, out_norm_b_g, w_out_f, ffn_shards)
    w_ffn_in4 = _own_piece(w_ffn_in4, ffn_shards[0])
    w_ffn_out_f = _own_piece(w_ffn_out4, ffn_shards[1]).reshape(DFF, D)
    g, up, h2, dx2, acc_f = _ffn_fwd(x1, target, mod, mods, norm2_g, norm_f_g, w_ffn_in4, w_ffn_out_f)
    loss = lax.psum(acc_f[4, 0], ("x", "y", "c"))

    dff, a_act, dx1, acc_2 = _ffn_bwd(dx2, x1, g, up, mod, mods, norm2_g, w_ffn_in4, w_ffn_out_f)
    gw_ffn_in4 = _grad_matmul("grad_w_ffn_in", h2, dff, D, PW_FF, True)
    gm2 = _grad_matmul("grad_w_ffn_out", a_act, dx2, PW_FF, D, False)
    modv = mod + mods
    gw_ffn_out, dgate2 = _gate_finalize("gate2_finalize", gm2, w_ffn_out_f, modv[:, 5 * D:6 * D])
    ffn_chip_sums = _chip_sums("ffn", [gw_ffn_in4, gw_ffn_out.reshape(NQ, DFF // NQ, D)])
    gx, dz, h, acc_1, acc_bin, acc_5, acc_cw, acc_sw, acc_sb, *ffn_arrived = _mixer_bwd(
        dx1, xs, z, yb1, mod, mods, norm1_g, w_in4, a_ln_g, a_ln_b, wm, wmt, bst, conv_w, b_gn_g, b_gn_b,
        out_norm_a_g, out_norm_b_g, w_out_f, ffn_chip_sums)
    gw_in4 = _grad_matmul("grad_w_in", h, dz, D, PW_IN, True)
    gm1 = _grad_matmul("grad_w_out", y, dx1, D, D, False)
    gw_out, dgate1 = _gate_finalize("gate1_finalize", gm1, w_out_f, modv[:, 2 * D:3 * D])
    mix_chip_sums = _chip_sums("mix", [gw_in4, gw_out.reshape(NQ, D // NQ, D)])
    mix_arrived = _scatter_to_chips(mix_chip_sums)
    g_w_in, g_w_out, g_w_ffn_in, g_w_ffn_out = _join_halves(
        _chip_order_sums("mix", mix_arrived, mix_chip_sums) + _chip_order_sums("ffn", ffn_arrived, ffn_chip_sums))

    dmod = jnp.concatenate([acc_1[0:1], acc_1[1:2], dgate1[0:1], acc_2[0:1], acc_2[1:2], dgate2[0:1],
                            acc_f[0:1], acc_f[1:2]], axis=1)
    sw_grad = jnp.where(causal[None], acc_sw, 0.0)
    sb_grad = acc_sb[:, ::HALF].T
    packed = _pack_small(dmod, acc_1[2:3], acc_2[2:3], acc_f[2:3], acc_bin[0:1], acc_5[2:3], acc_5[3:4], acc_5[6:7],
                         acc_5[4:5], acc_5[5:6], acc_5[0:1], acc_5[1:2], sb_grad, acc_cw, sw_grad)
    gathered = _all_gather8("gather_small_grads", packed)
    small = _unpack_small(_ordered_sum("small_grad_sum", gathered))
    dmod_all = gathered[:, PK_DMOD:PK_DMOD + 8, :].reshape(8, 8 * D)
    dmod_cols = jnp.concatenate([lax.dynamic_slice_in_dim(dmod_all, myq * n_ada, n_ada, axis=1),
                                 lax.dynamic_slice_in_dim(dmod_all, 6 * D + myq * PW_IN, PW_IN, axis=1)], axis=1)
    g_ada_cat = _cond_grad(c_all, dmod_cols)

    grads = dict(
        ada_w=g_ada_cat[:, :n_ada], ada_b=small["dmod"][:, :6 * D], norm1_g=small["norm1_g"], w_in=g_w_in,
        b_in=small["b_in"], a_ln_g=small["a_ln_g"], a_ln_b=small["a_ln_b"], a_spatial_w=small["a_spatial_w"],
        a_spatial_b=small["a_spatial_b"],
        b_conv_w=lax.dynamic_slice_in_dim(small["b_conv_w"], myq * LANES, LANES, axis=1)[:CONV_W],
        b_conv_b=small["b_conv_b"], b_gn_g=small["b_gn_g"], b_gn_b=small["b_gn_b"],
        out_norm_a_g=small["out_norm_a_g"], out_norm_b_g=small["out_norm_b_g"], w_out=g_w_out,
        norm2_g=small["norm2_g"], w_ffn_in=g_w_ffn_in, w_ffn_out=g_w_ffn_out, ada_f_w=g_ada_cat[:, n_ada:],
        ada_f_b=small["dmod"][:, 6 * D:], norm_f_g=small["norm_f_g"])

    weights = dict(ada_w=ada_w, ada_b=ada_b, norm1_g=norm1_g, w_in=w_in, b_in=b_in, a_ln_g=a_ln_g, a_ln_b=a_ln_b,
                   a_spatial_w=a_spatial_w, a_spatial_b=a_spatial_b, b_conv_w=b_conv_w, b_conv_b=b_conv_b, b_gn_g=b_gn_g,
                   b_gn_b=b_gn_b, out_norm_a_g=out_norm_a_g, out_norm_b_g=out_norm_b_g, w_out=w_out, norm2_g=norm2_g,
                   w_ffn_in=w_ffn_in, w_ffn_out=w_ffn_out, ada_f_w=ada_f_w, ada_f_b=ada_f_b, norm_f_g=norm_f_g)
    m_in = dict(ada_w=m_ada_w, ada_b=m_ada_b, norm1_g=m_norm1_g, w_in=m_w_in, b_in=m_b_in, a_ln_g=m_a_ln_g, a_ln_b=m_a_ln_b,
                a_spatial_w=m_a_spatial_w, a_spatial_b=m_a_spatial_b, b_conv_w=m_b_conv_w, b_conv_b=m_b_conv_b,
                b_gn_g=m_b_gn_g, b_gn_b=m_b_gn_b, out_norm_a_g=m_out_norm_a_g, out_norm_b_g=m_out_norm_b_g, w_out=m_w_out,
                norm2_g=m_norm2_g, w_ffn_in=m_w_ffn_in, w_ffn_out=m_w_ffn_out, ada_f_w=m_ada_f_w, ada_f_b=m_ada_f_b,
                norm_f_g=m_norm_f_g)
    v_in = dict(ada_w=v_ada_w, ada_b=v_ada_b, norm1_g=v_norm1_g, w_in=v_w_in, b_in=v_b_in, a_ln_g=v_a_ln_g, a_ln_b=v_a_ln_b,
                a_spatial_w=v_a_spatial_w, a_spatial_b=v_a_spatial_b, b_conv_w=v_b_conv_w, b_conv_b=v_b_conv_b,
                b_gn_g=v_b_gn_g, b_gn_b=v_b_gn_b, out_norm_a_g=v_out_norm_a_g, out_norm_b_g=v_out_norm_b_g, w_out=v_w_out,
                norm2_g=v_norm2_g, w_ffn_in=v_w_ffn_in, w_ffn_out=v_w_ffn_out, ada_f_w=v_ada_f_w, ada_f_b=v_ada_f_b,
                norm_f_g=v_norm_f_g)
    names = list(weights)
    big = ("ada_w", "w_in", "w_out", "w_ffn_in", "w_ffn_out", "ada_f_w")

    def flat2(a):
        return a.reshape(-1, a.shape[-1])

    delta, new_m, new_v = {}, {}, {}
    for nm in big:
        shape = weights[nm].shape
        grads[nm] = grads[nm].reshape(shape)
        d_, m_, v_ = _adamw("adamw_" + nm, flat2(weights[nm]), flat2(grads[nm]), flat2(m_in[nm]), flat2(v_in[nm]))
        delta[nm], new_m[nm], new_v[nm] = d_.reshape(shape), m_.reshape(shape), v_.reshape(shape)

    small_names = [nm for nm in names if nm not in big]
    sizes = [math.prod(weights[nm].shape) for nm in small_names]
    total = sum(sizes)
    rows = -(-total // (8 * D)) * 8

    def pack(tree):
        flat = jnp.concatenate([tree[nm].reshape(-1) for nm in small_names] + [jnp.ones((rows * D - total,), F32)])
        return flat.reshape(rows, D)

    for nm in small_names:
        grads[nm] = grads[nm].reshape(weights[nm].shape)
    d_s, m_s, v_s = _adamw("adamw_small", pack(weights), pack(grads), pack(m_in), pack(v_in))
    off = 0
    for nm, size in zip(small_names, sizes):
        shape = weights[nm].shape
        delta[nm] = d_s.reshape(-1)[off:off + size].reshape(shape)
        new_m[nm] = m_s.reshape(-1)[off:off + size].reshape(shape)
        new_v[nm] = v_s.reshape(-1)[off:off + size].reshape(shape)
        off += size

    grad_x = gx.reshape(x.shape)
    return (loss, grad_x, *[grads[nm] for nm in names], *[delta[nm] for nm in names],
            *[new_m[nm] for nm in names], *[new_v[nm] for nm in names])
```

```python
import functools
import math

import jax
import jax.numpy as jnp
from jax import lax
from jax.experimental import pallas as pl
from jax.experimental.pallas import tpu as pltpu

F32 = jnp.float32
MM_DTYPE = jnp.bfloat16
WIRE_DTYPE = jnp.bfloat16
SDS = jax.ShapeDtypeStruct
MESH = pl.DeviceIdType.MESH

D = 1024
DA = 512
DB = 512
NQ = 4
PW_IN = 512
DFF = 2816
PW_FF = 1408
CHUNK = 128
N_HEADS = 8
CONV_W = 31
HALO = 32
CONV_ROWS = 64
EPS = 1e-6
LANES = 128
HALF = 64

ROW_TILE = 256
FFN_ROW_TILE = 256
GRAD_ROW_TILE = 2048
VMEM_LIMIT = 60 * 1024 * 1024

ADAM_LR = 0.001
ADAM_B1 = 0.9
ADAM_B2 = 0.999
ADAM_EPS = 1e-08
ADAM_WD = 0.01
ADAM_STEP = 10

PK_DMOD = 0
PK_G1 = 8
PK_G2 = 9
PK_GF = 10
PK_BIN = 11
PK_V512 = 13
PK_SB = 17
PK_CW = 18
PK_SW = 34
PK_LOSS = 162
PK_ROWS = 168


def _dot(a, b):
    return jnp.dot(a, b, preferred_element_type=F32)


def _dot_nt(a, b):
    return lax.dot_general(a, b, (((1,), (1,)), ((), ())), preferred_element_type=F32)


def _dot_tn(a, b):
    return lax.dot_general(a, b, (((0,), (0,)), ((), ())), preferred_element_type=F32)


def _rowsum(x):
    return jnp.sum(x, axis=-1, keepdims=True)


def _colsum(x):
    return jnp.sum(x, axis=0, keepdims=True)


def _group_sum(x):
    rows, width = x.shape
    lo_mask = lax.broadcasted_iota(jnp.int32, (rows, LANES), 1) < HALF
    outs = []
    for jb in range(width // LANES):
        xb = x[:, jb * LANES:(jb + 1) * LANES]
        lo = _rowsum(jnp.where(lo_mask, xb, 0.0))
        hi = _rowsum(jnp.where(lo_mask, 0.0, xb))
        outs.append(jnp.where(lo_mask, lo, hi))
    return jnp.concatenate(outs, axis=-1)


def _sigmoid(x):
    return 1.0 / (1.0 + jnp.exp(-x))


def _gelu_parts(u):
    cdf = 0.5 * (1.0 + lax.erf(u * (1.0 / math.sqrt(2.0))))
    pdf = jnp.exp(-0.5 * u * u) * (1.0 / math.sqrt(2.0 * math.pi))
    return u * cdf, cdf + u * pdf


def _whole_vmem():
    return pl.BlockSpec(memory_space=pltpu.VMEM)


def _params(*semantics):
    return pltpu.CompilerParams(dimension_semantics=semantics, vmem_limit_bytes=VMEM_LIMIT)


def _mod_rows(mod_ref, modb_ref, first, count):
    m = mod_ref[...] + modb_ref[...]
    return [m[:, (first + k) * D:(first + k + 1) * D] for k in range(count)]


def _mixer_recompute(z_parts, lng, lnb, wm_ref, bst_ref, mix_ref):
    u, v, val, gate = z_parts
    rows = u.shape[0]
    gu, dgu = _gelu_parts(u)
    gv, dgv = _gelu_parts(v)
    mu = _rowsum(gv) * (1.0 / DA)
    vc = gv - mu
    rs = lax.rsqrt(_rowsum(vc * vc) * (1.0 / DA) + EPS)
    vhat = vc * rs
    vl = vhat * lng + lnb
    vlb = vl.astype(MM_DTYPE)
    lo_mask = lax.broadcasted_iota(jnp.int32, (CHUNK, LANES), 1) < HALF
    for ck in range(rows // CHUNK):
        for jb in range(DA // LANES):
            blk = vlb[ck * CHUNK:(ck + 1) * CHUNK, jb * LANES:(jb + 1) * LANES]
            a = _dot(wm_ref[2 * jb], blk)
            b = _dot(wm_ref[2 * jb + 1], blk)
            mix_ref[ck * CHUNK:(ck + 1) * CHUNK, jb * LANES:(jb + 1) * LANES] = (
                jnp.where(lo_mask, a, b) + bst_ref[:, jb * LANES:(jb + 1) * LANES])
    mixed = mix_ref[...]
    sg = _sigmoid(gate)
    yb0 = val * sg
    return dict(gu=gu, dgu=dgu, dgv=dgv, rs=rs, vhat=vhat, vlb=vlb, mixed=mixed, sg=sg, yb0=yb0)


def _conv_branch_tail(yb1, gng, gnb):
    gm = _group_sum(yb1) * (1.0 / HALF)
    gc = yb1 - gm
    grs = lax.rsqrt(_group_sum(gc * gc) * (1.0 / HALF) + EPS)
    ghat = gc * grs
    yb2 = ghat * gng + gnb
    s2 = _sigmoid(yb2)
    return dict(grs=grs, ghat=ghat, yb2=yb2, s2=s2, y_b=yb2 * s2)


def _shifted_copies(e_ref, sh_ref):
    n = sh_ref.shape[1]
    for b in range(1, 8):
        sh_ref[b - 1] = e_ref[pl.ds(b, n), :]


def _window(e_ref, sh_ref, offset, r0, nrows, cols):
    a, b = divmod(offset, 8)
    if b == 0:
        return e_ref[pl.ds(r0 + 8 * a, nrows), cols]
    return sh_ref[b - 1, pl.ds(r0 + 8 * a, nrows), cols]


def _conv_taps(e_ref, sh_ref, cw_ref, out_ref, ts, first_offset, flip, bias_ref=None, other_ref=None, tap_acc_ref=None):
    groups = CONV_ROWS // 8
    for cb in range(DB // LANES):
        cols = slice(cb * LANES, (cb + 1) * LANES)
        tap_acc = [jnp.zeros((8, LANES), F32) for _ in range(CONV_W)]
        for rb in range(ts // CONV_ROWS):
            r0 = rb * CONV_ROWS
            acc = jnp.zeros((CONV_ROWS, LANES), F32)
            if bias_ref is not None:
                acc = acc + bias_ref[:, cols]
            if other_ref is not None:
                other = other_ref[r0:r0 + CONV_ROWS, cols]
            for j in range(CONV_W):
                k = CONV_W - 1 - j if flip else j
                win = _window(e_ref, sh_ref, first_offset + j, r0, CONV_ROWS, cols)
                acc = acc + win * cw_ref[k:k + 1, cols]
                if other_ref is not None:
                    tap_acc[k] = tap_acc[k] + jnp.sum((other * win).reshape(groups, 8, LANES), axis=0)
            out_ref[r0:r0 + CONV_ROWS, cols] = acc
        if other_ref is not None:
            for k in range(CONV_W):
                tap_acc_ref[k:k + 1, cols] += _colsum(tap_acc[k])


def _gather_plan(ins, outs, send_sems, recv_sems):
    x, y, c = _place()
    sibling = (x, y, 1 - c)
    chips = _other_chips(x, y)
    myq = 2 * x + y

    def copy(w, k, q, hc, to, src=None):
        rows = ins[w].shape[0]
        dst = outs[w].at[q, pl.ds(hc * (rows // 2), rows // 2)]
        return pltpu.make_async_remote_copy(
            src_ref=dst if src is None else src, dst_ref=dst,
            send_sem=send_sems.at[w, k], recv_sem=recv_sems.at[w, k], device_id=to, device_id_type=MESH)

    def send():
        for w in range(len(ins)):
            rows = ins[w].shape[0]
            src = ins[w].at[pl.ds(c * (rows // 2), rows // 2)]
            for j, chip in enumerate(chips):
                copy(w, j, myq, c, (*chip, c), src=src).start()

    def forward():
        for w in range(len(ins)):
            for j, (qx, qy) in enumerate(chips):
                copy(w, j, 2 * qx + qy, c, sibling).wait_recv()
                copy(w, 3 + j, 2 * qx + qy, c, sibling).start()

    def finish():
        for w in range(len(ins)):
            for j, (qx, qy) in enumerate(chips):
                copy(w, 3 + j, 2 * qx + qy, 1 - c, sibling).wait_recv()
        for w in range(len(ins)):
            for k, (qx, qy) in enumerate(chips + chips):
                copy(w, k, 2 * qx + qy, c, sibling).wait_send()

    return send, forward, finish


def _mixer_fwd(x, mod, mods, norm1_g, w_in4, b_in, ln_g, ln_b, wm, bst, conv_w, conv_b, gn_g, gn_b, ga, gb, w_out,
               ffn_shards):
    s = x.shape[0]
    ts = min(ROW_TILE, s)
    nt = s // ts
    n_sh = len(ffn_shards)

    def body(x_ref, mod_ref, modb_ref, g1_ref, w4_ref, bin_ref, lng_ref, lnb_ref, wm_ref, bst_ref, cw_ref, cb_ref,
             gng_ref, gnb_ref, ga_ref, gb_ref, wout_ref, *rest):
        shard_refs, rest = rest[:n_sh], rest[n_sh:]
        z_ref, x1_ref, yb1_ref, y_ref = rest[:4]
        full_refs, rest = rest[4:4 + n_sh], rest[4 + n_sh:]
        e_ref, sh_ref, mix_ref, send_sems, recv_sems = rest
        i = pl.program_id(0)
        send, forward, finish = _gather_plan(shard_refs, full_refs, send_sems, recv_sems)

        @pl.when(i == 0)
        def _():
            send()
            e_ref[0:HALO, :] = jnp.zeros((HALO, DB), F32)

        @pl.when(i == (3 * nt) // 4)
        def _():
            forward()

        shift1, scale1, gate1 = _mod_rows(mod_ref, modb_ref, 0, 3)
        x_t = x_ref[...]
        r1 = lax.rsqrt(_rowsum(x_t * x_t) * (1.0 / D) + EPS)
        h = (x_t * r1 * g1_ref[...]) * (1.0 + scale1) + shift1
        hb = h.astype(MM_DTYPE)
        z_parts = []
        for q in range(NQ):
            zq = _dot(hb, w4_ref[q]) + bin_ref[:, q * PW_IN:(q + 1) * PW_IN]
            z_ref[:, q * PW_IN:(q + 1) * PW_IN] = zq
            z_parts.append(zq)
        r = _mixer_recompute(z_parts, lng_ref[...], lnb_ref[...], wm_ref, bst_ref, mix_ref)
        y_a = r["gu"] * r["mixed"]
        e_ref[HALO:HALO + ts, :] = r["yb0"]
        _shifted_copies(e_ref, sh_ref)
        _conv_taps(e_ref, sh_ref, cw_ref, yb1_ref, ts, HALO - (CONV_W - 1), False, bias_ref=cb_ref)
        e_ref[0:HALO, :] = e_ref[ts:ts + HALO, :]
        t = _conv_branch_tail(yb1_ref[...], gng_ref[...], gnb_ref[...])
        ra = lax.rsqrt(_rowsum(y_a * y_a) * (1.0 / DA) + EPS)
        rb = lax.rsqrt(_rowsum(t["y_b"] * t["y_b"]) * (1.0 / DB) + EPS)
        yan = (y_a * ra * ga_ref[...]).astype(MM_DTYPE)
        ybn = (t["y_b"] * rb * gb_ref[...]).astype(MM_DTYPE)
        y_ref[:, 0:DA] = yan
        y_ref[:, DA:D] = ybn
        o1 = _dot(yan, wout_ref[0:DA, :]) + _dot(ybn, wout_ref[DA:D, :])
        x1_ref[...] = x_t + gate1 * o1

        @pl.when(i == nt - 1)
        def _():
            finish()

    row = lambda w: pl.BlockSpec((ts, w), lambda i: (i, 0))
    full = lambda a: pl.BlockSpec(a.shape, lambda i: (0,) * a.ndim)
    return pl.pallas_call(
        body, name="mixer_fwd", grid=(nt,),
        in_specs=[row(D), full(mod), full(mods), full(norm1_g), _whole_vmem(), full(b_in), full(ln_g), full(ln_b),
                  _whole_vmem(), full(bst), full(conv_w), full(conv_b), full(gn_g), full(gn_b), full(ga), full(gb),
                  _whole_vmem()] + [_any()] * n_sh,
        out_specs=[row(4 * PW_IN), row(D), row(DB), row(D)] + [_any()] * n_sh,
        out_shape=[SDS((s, 4 * PW_IN), F32), SDS((s, D), F32), SDS((s, DB), F32), SDS((s, D), MM_DTYPE)]
        + [SDS((NQ,) + w.shape, w.dtype) for w in ffn_shards],
        scratch_shapes=[pltpu.VMEM((ts + HALO, DB), F32), pltpu.VMEM((7, ts + HALO - 8, DB), F32), pltpu.VMEM((ts, DA), F32),
                        pltpu.SemaphoreType.DMA((n_sh, 6)), pltpu.SemaphoreType.DMA((n_sh, 6))],
        compiler_params=_params("arbitrary"),
    )(x, mod, mods, norm1_g, w_in4, b_in, ln_g, ln_b, wm, bst, conv_w, conv_b, gn_g, gn_b, ga, gb, w_out, *ffn_shards)


def _ffn_fwd(x1, target, mod, mods, norm2_g, norm_f_g, w_ffn_in4, w_ffn_out):
    s = x1.shape[0]
    ts = min(FFN_ROW_TILE, s)
    nt = s // ts

    def body(x1_ref, tgt_ref, mod_ref, modb_ref, g2_ref, gf_ref, wf_ref, wo_ref,
             g_ref, up_ref, h2_ref, dx2_ref, acc_ref):
        i = pl.program_id(0)

        @pl.when(i == 0)
        def _():
            acc_ref[...] = jnp.zeros(acc_ref.shape, F32)

        shift2, scale2, gate2, shift_f, scale_f = _mod_rows(mod_ref, modb_ref, 3, 5)
        x1_t = x1_ref[...]
        r2 = lax.rsqrt(_rowsum(x1_t * x1_t) * (1.0 / D) + EPS)
        h2 = (x1_t * r2 * g2_ref[...]) * (1.0 + scale2) + shift2
        h2b = h2.astype(MM_DTYPE)
        h2_ref[...] = h2b
        o2 = jnp.zeros((ts, D), F32)
        for p in range(2):
            g = _dot(h2b, wf_ref[p])
            up = _dot(h2b, wf_ref[2 + p])
            g_ref[:, p * PW_FF:(p + 1) * PW_FF] = g.astype(MM_DTYPE)
            up_ref[:, p * PW_FF:(p + 1) * PW_FF] = up.astype(MM_DTYPE)
            a = (g * _sigmoid(g) * up).astype(MM_DTYPE)
            o2 = o2 + _dot(a, wo_ref[p * PW_FF:(p + 1) * PW_FF, :])
        x2 = x1_t + gate2 * o2
        rf = lax.rsqrt(_rowsum(x2 * x2) * (1.0 / D) + EPS)
        gf = gf_ref[...]
        nf = x2 * rf * gf
        err = nf * (1.0 + scale_f) + shift_f - tgt_ref[...]
        d_out = err * (1.0 / D)
        d_nf = d_out * (1.0 + scale_f)
        t = d_nf * gf
        dx2_ref[...] = rf * t - x2 * (rf * rf * rf) * (_rowsum(t * x2) * (1.0 / D))
        acc_ref[0:1, :] += _colsum(d_out)
        acc_ref[1:2, :] += _colsum(d_out * nf)
        acc_ref[2:3, :] += _colsum(d_nf * x2 * rf)
        acc_ref[3:4, :] += _colsum(err * err)

        @pl.when(i == nt - 1)
        def _():
            acc_ref[4:5, :] = jnp.zeros((1, D), F32) + _rowsum(acc_ref[3:4, :]) * (0.5 / D)

    row = lambda w: pl.BlockSpec((ts, w), lambda i: (i, 0))
    full = lambda a: pl.BlockSpec(a.shape, lambda i: (0,) * a.ndim)
    return pl.pallas_call(
        body, name="ffn_fwd", grid=(nt,),
        in_specs=[row(D), row(D), full(mod), full(mods), full(norm2_g), full(norm_f_g), _whole_vmem(), _whole_vmem()],
        out_specs=[row(DFF), row(DFF), row(D), row(D), pl.BlockSpec((8, D), lambda i: (0, 0))],
        out_shape=[SDS((s, DFF), MM_DTYPE), SDS((s, DFF), MM_DTYPE), SDS((s, D), MM_DTYPE), SDS((s, D), F32),
                   SDS((8, D), F32)],
        compiler_params=_params("arbitrary"),
    )(x1, target, mod, mods, norm2_g, norm_f_g, w_ffn_in4, w_ffn_out)


def _ffn_bwd(dx2, x1, g, up, mod, mods, norm2_g, w_ffn_in4, w_ffn_out):
    s = x1.shape[0]
    ts = min(FFN_ROW_TILE, s)
    nt = s // ts

    def body(dx2_ref, x1_ref, g_ref, up_ref, mod_ref, modb_ref, g2_ref, wf_ref, wo_ref,
             dff_ref, a_ref, dx1_ref, acc_ref):
        @pl.when(pl.program_id(0) == 0)
        def _():
            acc_ref[...] = jnp.zeros(acc_ref.shape, F32)

        shift2, scale2, gate2 = _mod_rows(mod_ref, modb_ref, 3, 3)
        dx2_t = dx2_ref[...]
        do2 = (dx2_t * gate2).astype(MM_DTYPE)
        dh2 = jnp.zeros((ts, D), F32)
        for p in range(2):
            da = _dot_nt(do2, wo_ref[p * PW_FF:(p + 1) * PW_FF, :])
            gp = g_ref[:, p * PW_FF:(p + 1) * PW_FF].astype(F32)
            upp = up_ref[:, p * PW_FF:(p + 1) * PW_FF].astype(F32)
            sg = _sigmoid(gp)
            silu = gp * sg
            a_ref[:, p * PW_FF:(p + 1) * PW_FF] = (silu * upp).astype(MM_DTYPE)
            dg = (da * upp * (sg * (1.0 + gp * (1.0 - sg)))).astype(MM_DTYPE)
            dup = (da * silu).astype(MM_DTYPE)
            dff_ref[:, p * PW_FF:(p + 1) * PW_FF] = dg
            dff_ref[:, DFF + p * PW_FF:DFF + (p + 1) * PW_FF] = dup
            dh2 = dh2 + _dot_nt(dg, wf_ref[p]) + _dot_nt(dup, wf_ref[2 + p])
        x1_t = x1_ref[...]
        r2 = lax.rsqrt(_rowsum(x1_t * x1_t) * (1.0 / D) + EPS)
        g2 = g2_ref[...]
        xr = x1_t * r2
        dn2 = dh2 * (1.0 + scale2)
        t = dn2 * g2
        dx1_ref[...] = dx2_t + r2 * t - x1_t * (r2 * r2 * r2) * (_rowsum(t * x1_t) * (1.0 / D))
        acc_ref[0:1, :] += _colsum(dh2)
        acc_ref[1:2, :] += _colsum(dh2 * (xr * g2))
        acc_ref[2:3, :] += _colsum(dn2 * xr)

    row = lambda w: pl.BlockSpec((ts, w), lambda i: (i, 0))
    full = lambda a: pl.BlockSpec(a.shape, lambda i: (0,) * a.ndim)
    return pl.pallas_call(
        body, name="ffn_bwd", grid=(nt,),
        in_specs=[row(D), row(D), row(DFF), row(DFF), full(mod), full(mods), full(norm2_g), _whole_vmem(), _whole_vmem()],
        out_specs=[row(2 * DFF), row(DFF), row(D), pl.BlockSpec((8, D), lambda i: (0, 0))],
        out_shape=[SDS((s, 2 * DFF), MM_DTYPE), SDS((s, DFF), MM_DTYPE), SDS((s, D), F32), SDS((8, D), F32)],
        compiler_params=_params("arbitrary"),
    )(dx2, x1, g, up, mod, mods, norm2_g, w_ffn_in4, w_ffn_out)


def _scatter_plan(ins, outs, send_sems, recv_sems):
    x, y, c = _place()
    chips = _other_chips(x, y)
    myq = 2 * x + y

    def copies():
        return [pltpu.make_async_remote_copy(
            src_ref=ins[w].at[2 * qx + qy], dst_ref=outs[w].at[myq],
            send_sem=send_sems.at[w, j], recv_sem=recv_sems.at[w, j], device_id=(qx, qy, c), device_id_type=MESH)
            for w in range(len(ins)) for j, (qx, qy) in enumerate(chips)]

    def send():
        for cp in copies():
            cp.start()

    def finish():
        for cp in copies():
            cp.wait()

    return send, finish


def _mixer_bwd(dx1, x, z, yb1, mod, mods, norm1_g, w_in4, ln_g, ln_b, wm, wmt, bst, conv_w, gn_g, gn_b, ga, gb, w_out,
               chip_sums):
    s = x.shape[0]
    ts = min(ROW_TILE, s)
    nt = s // ts
    n_cs = len(chip_sums)

    def body(dx1_ref, x_ref, z_ref, yb1_ref, mod_ref, modb_ref, g1_ref, w4_ref, lng_ref, lnb_ref, wm_ref, wmt_ref,
             bst_ref, cw_ref, gng_ref, gnb_ref, ga_ref, gb_ref, wout_ref, *rest):
        cs_refs, rest = rest[:n_cs], rest[n_cs:]
        gx_ref, dz_ref, h_ref, a1_ref, a2_ref, a5_ref, acw_ref, asw_ref, asb_ref = rest[:9]
        arrived_refs, rest = rest[9:9 + n_cs], rest[9 + n_cs:]
        e_ref, sh_ref, mix_ref, dvl_ref, send_sems, recv_sems = rest
        i = pl.program_id(0)
        send, finish = _scatter_plan(cs_refs, arrived_refs, send_sems, recv_sems)

        @pl.when(i == 0)
        def _():
            send()
            e_ref[ts:ts + HALO, :] = jnp.zeros((HALO, DB), F32)
            for r in (a1_ref, a2_ref, a5_ref, acw_ref, asw_ref, asb_ref):
                r[...] = jnp.zeros(r.shape, F32)

        shift1, scale1, gate1 = _mod_rows(mod_ref, modb_ref, 0, 3)
        dx1_t = dx1_ref[...]
        do1 = (dx1_t * gate1).astype(MM_DTYPE)
        d_yan = _dot_nt(do1, wout_ref[0:DA, :])
        d_ybn = _dot_nt(do1, wout_ref[DA:D, :])

        z_parts = [z_ref[:, q * PW_IN:(q + 1) * PW_IN] for q in range(NQ)]
        u, v, val, gate = z_parts
        lng = lng_ref[...]
        r = _mixer_recompute(z_parts, lng, lnb_ref[...], wm_ref, bst_ref, mix_ref)
        gng = gng_ref[...]
        t = _conv_branch_tail(yb1_ref[...], gng, gnb_ref[...])
        y_a = r["gu"] * r["mixed"]
        y_b = t["y_b"]
        ga_v, gb_v = ga_ref[...], gb_ref[...]
        ra = lax.rsqrt(_rowsum(y_a * y_a) * (1.0 / DA) + EPS)
        rb = lax.rsqrt(_rowsum(y_b * y_b) * (1.0 / DB) + EPS)

        a5_ref[0:1, :] += _colsum(d_yan * y_a * ra)
        a5_ref[1:2, :] += _colsum(d_ybn * y_b * rb)
        ta = d_yan * ga_v
        d_ya = ra * ta - y_a * (ra * ra * ra) * (_rowsum(ta * y_a) * (1.0 / DA))
        tb = d_ybn * gb_v
        d_yb = rb * tb - y_b * (rb * rb * rb) * (_rowsum(tb * y_b) * (1.0 / DB))

        d_u = d_ya * r["mixed"] * r["dgu"]
        d_mixed = d_ya * r["gu"]
        dmb = d_mixed.astype(MM_DTYPE)
        lo_mask = lax.broadcasted_iota(jnp.int32, (CHUNK, LANES), 1) < HALF
        zero_blk = jnp.zeros((CHUNK, LANES), MM_DTYPE)
        sb_acc = jnp.zeros((CHUNK, DA), F32)
        for ck in range(ts // CHUNK):
            rows = slice(ck * CHUNK, (ck + 1) * CHUNK)
            sb_acc = sb_acc + d_mixed[rows, :]
            for jb in range(DA // LANES):
                cols = slice(jb * LANES, (jb + 1) * LANES)
                dm_blk = dmb[rows, cols]
                vl_blk = r["vlb"][rows, cols]
                da_ = _dot(wmt_ref[2 * jb], dm_blk)
                db_ = _dot(wmt_ref[2 * jb + 1], dm_blk)
                dvl_ref[rows, cols] = jnp.where(lo_mask, da_, db_)
                asw_ref[2 * jb] += _dot_nt(jnp.where(lo_mask, dm_blk, zero_blk), vl_blk)
                asw_ref[2 * jb + 1] += _dot_nt(jnp.where(lo_mask, zero_blk, dm_blk), vl_blk)
        asb_ref[...] += sb_acc
        d_vl = dvl_ref[...]
        a5_ref[2:3, :] += _colsum(d_vl * r["vhat"])
        a5_ref[3:4, :] += _colsum(d_vl)
        dvh = d_vl * lng
        d_gv = r["rs"] * (dvh - _rowsum(dvh) * (1.0 / DA) - r["vhat"] * (_rowsum(dvh * r["vhat"]) * (1.0 / DA)))
        d_v = d_gv * r["dgv"]

        yb2, s2 = t["yb2"], t["s2"]
        d_yb2 = d_yb * (s2 * (1.0 + yb2 * (1.0 - s2)))
        a5_ref[4:5, :] += _colsum(d_yb2 * t["ghat"])
        a5_ref[5:6, :] += _colsum(d_yb2)
        dgh = d_yb2 * gng
        d_yb1 = t["grs"] * (dgh - _group_sum(dgh) * (1.0 / HALF) - t["ghat"] * (_group_sum(dgh * t["ghat"]) * (1.0 / HALF)))
        a5_ref[6:7, :] += _colsum(d_yb1)
        e_ref[0:ts, :] = d_yb1
        _shifted_copies(e_ref, sh_ref)
        mix_ref[...] = r["yb0"]
        _conv_taps(e_ref, sh_ref, cw_ref, dvl_ref, ts, 0, True, other_ref=mix_ref, tap_acc_ref=acw_ref)
        d_yb0 = dvl_ref[...]
        e_ref[ts:ts + HALO, :] = e_ref[0:HALO, :]
        sg = r["sg"]
        d_val = d_yb0 * sg
        d_gate = d_yb0 * val * sg * (1.0 - sg)

        dh = jnp.zeros((ts, D), F32)
        for q, dzq in enumerate((d_u, d_v, d_val, d_gate)):
            a2_ref[0:1, q * PW_IN:(q + 1) * PW_IN] += _colsum(dzq)
            dzb = dzq.astype(MM_DTYPE)
            dz_ref[:, q * PW_IN:(q + 1) * PW_IN] = dzb
            dh = dh + _dot_nt(dzb, w4_ref[q])
        x_t = x_ref[...]
        r1 = lax.rsqrt(_rowsum(x_t * x_t) * (1.0 / D) + EPS)
        g1 = g1_ref[...]
        xr = x_t * r1
        n1 = xr * g1
        h_ref[...] = (n1 * (1.0 + scale1) + shift1).astype(MM_DTYPE)
        dn1 = dh * (1.0 + scale1)
        t1 = dn1 * g1
        gx_ref[...] = dx1_t + r1 * t1 - x_t * (r1 * r1 * r1) * (_rowsum(t1 * x_t) * (1.0 / D))
        a1_ref[0:1, :] += _colsum(dh)
        a1_ref[1:2, :] += _colsum(dh * n1)
        a1_ref[2:3, :] += _colsum(dn1 * xr)

        @pl.when(i == nt - 1)
        def _():
            asb_ref[...] = _group_sum(asb_ref[...])
            finish()

    row = lambda w: pl.BlockSpec((ts, w), lambda i: (nt - 1 - i, 0))
    full = lambda a: pl.BlockSpec(a.shape, lambda i: (0,) * a.ndim)
    keep = lambda shape: pl.BlockSpec(shape, lambda i: (0,) * len(shape))
    return pl.pallas_call(
        body, name="mixer_bwd", grid=(nt,),
        in_specs=[row(D), row(D), row(4 * PW_IN), row(DB), full(mod), full(mods), full(norm1_g), _whole_vmem(),
                  full(ln_g), full(ln_b), _whole_vmem(), _whole_vmem(), full(bst), full(conv_w), full(gn_g), full(gn_b),
                  full(ga), full(gb), _whole_vmem()] + [_any()] * n_cs,
        out_specs=[row(D), row(4 * PW_IN), row(D), keep((8, D)), keep((8, 4 * PW_IN)), keep((8, DA)),
                   keep((HALO, DB)), keep((N_HEADS, CHUNK, CHUNK)), keep((CHUNK, DA))] + [_any()] * n_cs,
        out_shape=[SDS((s, D), F32), SDS((s, 4 * PW_IN), MM_DTYPE), SDS((s, D), MM_DTYPE), SDS((8, D), F32),
                   SDS((8, 4 * PW_IN), F32), SDS((8, DA), F32), SDS((HALO, DB), F32),
                   SDS((N_HEADS, CHUNK, CHUNK), F32), SDS((CHUNK, DA), F32)]
        + [SDS(p.shape, p.dtype) for p in chip_sums],
        scratch_shapes=[pltpu.VMEM((ts + HALO, DB), F32), pltpu.VMEM((7, ts + HALO - 8, DB), F32),
                        pltpu.VMEM((ts, DA), F32), pltpu.VMEM((ts, DA), F32),
                        pltpu.SemaphoreType.DMA((n_cs, 3)), pltpu.SemaphoreType.DMA((n_cs, 3))],
        compiler_params=_params("arbitrary"),
    )(dx1, x, z, yb1, mod, mods, norm1_g, w_in4, ln_g, ln_b, wm, wmt, bst, conv_w, gn_g, gn_b, ga, gb, w_out, *chip_sums)


def _gather8_plan(x_ref, out_ref, send_sems, recv_sems):
    x, y, c = _place()
    me, sibling = (x, y, c), (x, y, 1 - c)
    chips = _other_chips(x, y)

    def copy(k, block, to, src=None):
        dst = out_ref.at[4 * block[0] + 2 * block[1] + block[2]]
        return pltpu.make_async_remote_copy(src_ref=dst if src is None else src, dst_ref=dst, send_sem=send_sems.at[k],
                                            recv_sem=recv_sems.at[k], device_id=to, device_id_type=MESH)

    def send():
        copy(0, me, sibling, src=x_ref).start()
        for j, chip in enumerate(chips):
            copy(1 + j, me, (*chip, c), src=x_ref).start()

    def forward():
        for j, chip in enumerate(chips):
            copy(1 + j, (*chip, c), me).wait_recv()
            copy(4 + j, (*chip, c), sibling).start()

    def finish():
        copy(0, sibling, me).wait_recv()
        for j, chip in enumerate(chips):
            copy(4 + j, (*chip, 1 - c), me).wait_recv()
        for k in range(7):
            copy(k, me, sibling).wait_send()

    return send, forward, finish


def _grad_matmul(name, a, b, ka_tile, nb_tile, pieces, gather_blk=None):
    s, ka = a.shape
    nb = b.shape[1]
    ts = min(GRAD_ROW_TILE, s)
    nt = s // ts
    nja, njb = ka // ka_tile, nb // nb_tile
    steps = nja * njb * nt

    def body(a_ref, b_ref, *rest):
        if gather_blk is None:
            (o_ref,) = rest
        else:
            blk_ref, o_ref, all_ref, send_sems, recv_sems = rest
            send, forward, finish = _gather8_plan(blk_ref, all_ref, send_sems, recv_sems)
            step = (pl.program_id(0) * njb + pl.program_id(1)) * nt + pl.program_id(2)

            @pl.when(step == 0)
            def _():
                send()

            @pl.when(step == steps // 2)
            def _():
                forward()

        @pl.when(pl.program_id(2) == 0)
        def _():
            o_ref[...] = jnp.zeros(o_ref.shape, F32)

        prod = _dot_tn(a_ref[...].astype(MM_DTYPE), b_ref[...].astype(MM_DTYPE))
        o_ref[...] += prod.reshape(o_ref.shape)

        if gather_blk is not None:
            @pl.when(step == steps - 1)
            def _():
                finish()

    if pieces:
        assert nja == 1
        out_shape = SDS((njb, ka, nb_tile), F32)
        out_spec = pl.BlockSpec((1, ka, nb_tile), lambda ja, jb, i: (jb, 0, 0))
    else:
        out_shape = SDS((ka, nb), F32)
        out_spec = pl.BlockSpec((ka_tile, nb_tile), lambda ja, jb, i: (ja, jb))
    in_specs = [pl.BlockSpec((ts, ka_tile), lambda ja, jb, i: (i, ja)),
                pl.BlockSpec((ts, nb_tile), lambda ja, jb, i: (i, jb))]
    if gather_blk is None:
        return pl.pallas_call(
            body, name=name, grid=(nja, njb, nt), in_specs=in_specs, out_specs=out_spec, out_shape=out_shape,
            compiler_params=_params("parallel", "parallel", "arbitrary"),
        )(a, b)
    return pl.pallas_call(
        body, name=name, grid=(nja, njb, nt), in_specs=in_specs + [_any()], out_specs=[out_spec, _any()],
        out_shape=[out_shape, SDS((8,) + gather_blk.shape, gather_blk.dtype)],
        scratch_shapes=[pltpu.SemaphoreType.DMA((7,)), pltpu.SemaphoreType.DMA((7,))],
        compiler_params=_params("arbitrary", "arbitrary", "arbitrary"),
    )(a, b, gather_blk)


def _gate_finalize(name, gmat, w, gate):
    k, n = gmat.shape
    kt = 256 if k % 256 == 0 else 352
    nk = k // kt

    def body(g_ref, w_ref, gate_ref, o_ref, dg_ref):
        @pl.when(pl.program_id(0) == 0)
        def _():
            dg_ref[...] = jnp.zeros(dg_ref.shape, F32)

        gm = g_ref[...]
        o_ref[...] = gm * gate_ref[...]
        dg_ref[0:1, :] += _colsum(gm * w_ref[...].astype(F32))

    return pl.pallas_call(
        body, name=name, grid=(nk,),
        in_specs=[pl.BlockSpec((kt, n), lambda i: (i, 0)), pl.BlockSpec((kt, n), lambda i: (i, 0)),
                  pl.BlockSpec((1, n), lambda i: (0, 0))],
        out_specs=[pl.BlockSpec((kt, n), lambda i: (i, 0)), pl.BlockSpec((8, n), lambda i: (0, 0))],
        out_shape=[SDS((k, n), F32), SDS((8, n), F32)],
        compiler_params=_params("arbitrary"),
    )(gmat, w, gate)


def _cond_partial(c_all, w_cat):
    n = w_cat.shape[1]
    nt_cols = 512

    def body(c_ref, w_ref, o_ref):
        c_t = c_ref[...]
        ca = (c_t * _sigmoid(c_t)).astype(MM_DTYPE)
        o_ref[...] = _dot(ca, w_ref[...].astype(MM_DTYPE))

    return pl.pallas_call(
        body, name="cond_partial", grid=(n // nt_cols,),
        in_specs=[pl.BlockSpec((8, D), lambda j: (0, 0)), pl.BlockSpec((D, nt_cols), lambda j: (0, j))],
        out_specs=pl.BlockSpec((8, nt_cols), lambda j: (0, j)), out_shape=SDS((8, n), F32),
        compiler_params=_params("parallel"),
    )(c_all, w_cat)


def _cond_grad(c_all, dmod_cols):
    n = dmod_cols.shape[1]
    nt_cols = 512

    def body(c_ref, d_ref, o_ref):
        c_t = c_ref[...]
        ca = jnp.concatenate([c_t * _sigmoid(c_t), jnp.zeros((8, D), F32)], axis=0).astype(MM_DTYPE)
        dm = jnp.concatenate([d_ref[...], jnp.zeros((8, nt_cols), F32)], axis=0).astype(MM_DTYPE)
        o_ref[...] = _dot_tn(ca, dm)

    return pl.pallas_call(
        body, name="cond_grad", grid=(n // nt_cols,),
        in_specs=[pl.BlockSpec((8, D), lambda j: (0, 0)), pl.BlockSpec((8, nt_cols), lambda j: (0, j))],
        out_specs=pl.BlockSpec((D, nt_cols), lambda j: (0, j)), out_shape=SDS((D, n), F32),
        compiler_params=_params("parallel"),
    )(c_all, dmod_cols)


def _row_tile(rows, cap=256):
    if rows <= cap:
        return rows
    for t in range(cap, 7, -8):
        if rows % t == 0:
            return t
    return rows


def _ordered_sum(name, parts, out_dtype=F32):
    n, rows, cols = parts.shape
    rt = _row_tile(rows)

    def body(p_ref, o_ref):
        acc = p_ref[0].astype(F32)
        for k in range(1, n):
            acc = acc + p_ref[k].astype(F32)
        o_ref[...] = acc.astype(out_dtype)

    return pl.pallas_call(
        body, name=name, grid=(rows // rt,),
        in_specs=[pl.BlockSpec((n, rt, cols), lambda i: (0, i, 0))],
        out_specs=pl.BlockSpec((rt, cols), lambda i: (i, 0)), out_shape=SDS((rows, cols), out_dtype),
        compiler_params=_params("parallel"),
    )(parts)


def _adamw_update(w_ref, g_ref, m_ref, v_ref, d_ref, nm_ref, nv_ref):
    c1 = 1.0 - ADAM_B1 ** ADAM_STEP
    c2 = 1.0 - ADAM_B2 ** ADAM_STEP
    g_t = g_ref[...]
    m_new = ADAM_B1 * m_ref[...] + (1.0 - ADAM_B1) * g_t
    v_new = ADAM_B2 * v_ref[...] + (1.0 - ADAM_B2) * (g_t * g_t)
    nm_ref[...] = m_new
    nv_ref[...] = v_new
    d_ref[...] = -ADAM_LR * ((m_new / c1) / (jnp.sqrt(v_new / c2) + ADAM_EPS) + ADAM_WD * w_ref[...])


def _adamw_many(name, ws, gs, ms, vs):
    n = len(ws)

    def body(*refs):
        ins, outs = refs[:4 * n], refs[4 * n:]
        for k in range(n):
            _adamw_update(ins[k], ins[n + k], ins[2 * n + k], ins[3 * n + k], *outs[3 * k:3 * k + 3])

    return pl.pallas_call(
        body, name=name, out_shape=[SDS(w.shape, F32) for w in ws for _ in range(3)],
        compiler_params=pltpu.CompilerParams(vmem_limit_bytes=VMEM_LIMIT),
    )(*ws, *gs, *ms, *vs)


def _adamw(name, w, g, m, v):
    rows, cols = w.shape
    rt = _row_tile(rows)

    def body(*refs):
        _adamw_update(*refs)

    spec = pl.BlockSpec((rt, cols), lambda i: (i, 0))
    out = SDS((rows, cols), F32)
    return pl.pallas_call(body, name=name, grid=(rows // rt,), in_specs=[spec] * 4, out_specs=[spec] * 3,
                          out_shape=[out, out, out], compiler_params=_params("parallel"))(w, g, m, v)


def _place():
    return lax.axis_index("x"), lax.axis_index("y"), lax.axis_index("c")


def _other_chips(x, y):
    return [(1 - x, y), (x, 1 - y), (1 - x, 1 - y)]


def _all_gather8(name, blk):
    m, n = blk.shape

    def body(x_ref, out_ref, send_sems, recv_sems, local_sem):
        x, y, c = _place()
        me, sibling = (x, y, c), (x, y, 1 - c)
        chips = _other_chips(x, y)

        def slot(px, py, pc):
            return out_ref.at[4 * px + 2 * py + pc]

        def copy(k, block, to, src=None):
            return pltpu.make_async_remote_copy(
                src_ref=slot(*block) if src is None else src, dst_ref=slot(*block),
                send_sem=send_sems.at[k], recv_sem=recv_sems.at[k], device_id=to, device_id_type=MESH)

        mine = pltpu.make_async_copy(x_ref, slot(*me), local_sem)
        mine.start()
        first = [copy(0, me, sibling, src=x_ref)]
        first += [copy(1 + j, me, (*chip, c), src=x_ref) for j, chip in enumerate(chips)]
        for cp in first:
            cp.start()
        passed = [copy(4 + j, (*chip, c), sibling) for j, chip in enumerate(chips)]
        for j, chip in enumerate(chips):
            copy(1 + j, (*chip, c), me).wait_recv()
            passed[j].start()
        copy(0, sibling, me).wait_recv()
        for j, chip in enumerate(chips):
            copy(4 + j, (*chip, 1 - c), me).wait_recv()
        for cp in first + passed:
            cp.wait_send()
        mine.wait()

    return pl.pallas_call(
        body, name=name, out_shape=SDS((8, m, n), blk.dtype),
        in_specs=[_whole_vmem()], out_specs=_whole_vmem(),
        scratch_shapes=[pltpu.SemaphoreType.DMA((7,)), pltpu.SemaphoreType.DMA((7,)), pltpu.SemaphoreType.DMA],
        compiler_params=pltpu.CompilerParams(vmem_limit_bytes=VMEM_LIMIT),
    )(blk)


def _any():
    return pl.BlockSpec(memory_space=pl.ANY)


def _gather_weights(shards):
    n = len(shards)

    def body(*refs):
        send, forward, finish = _gather_plan(refs[:n], refs[n:2 * n], *refs[2 * n:])
        send()
        forward()
        finish()

    return pl.pallas_call(
        body, name="gather_weights",
        out_shape=[SDS((NQ,) + s.shape, s.dtype) for s in shards],
        in_specs=[_any()] * n, out_specs=[_any()] * n,
        scratch_shapes=[pltpu.SemaphoreType.DMA((n, 6)), pltpu.SemaphoreType.DMA((n, 6))],
    )(*shards)


def _own_piece(gathered, shard):
    myq = 2 * lax.axis_index("x") + lax.axis_index("y")
    return lax.dynamic_update_slice(gathered, shard[None], (myq,) + (0,) * shard.ndim)


def _swap_halves(name, parts):
    n = len(parts)

    def body(*refs):
        ins, outs = refs[:n], refs[n:2 * n]
        send_sems, recv_sems = refs[2 * n:]
        x, y, c = _place()
        sibling = (x, y, 1 - c)
        cps = []
        for w in range(n):
            rows = ins[w].shape[1]
            src = ins[w].at[:, pl.ds((1 - c) * (rows // 2), rows // 2)]
            cp = pltpu.make_async_remote_copy(src_ref=src, dst_ref=outs[w], send_sem=send_sems.at[w],
                                              recv_sem=recv_sems.at[w], device_id=sibling, device_id_type=MESH)
            cp.start()
            cps.append(cp)
        for cp in cps:
            cp.wait()

    return pl.pallas_call(
        body, name=name,
        out_shape=[SDS((p.shape[0], p.shape[1] // 2, p.shape[2]), p.dtype) for p in parts],
        in_specs=[_any()] * n, out_specs=[_any()] * n,
        scratch_shapes=[pltpu.SemaphoreType.DMA((n,)), pltpu.SemaphoreType.DMA((n,))],
    )(*parts)


def _scatter_to_chips(parts):
    n = len(parts)

    def body(*refs):
        send, finish = _scatter_plan(refs[:n], refs[n:2 * n], *refs[2 * n:])
        send()
        finish()

    return pl.pallas_call(
        body, name="scatter_to_chips",
        out_shape=[SDS(p.shape, p.dtype) for p in parts],
        in_specs=[_any()] * n, out_specs=[_any()] * n,
        scratch_shapes=[pltpu.SemaphoreType.DMA((n, 3)), pltpu.SemaphoreType.DMA((n, 3))],
    )(*parts)


def _own_chip_sum(arrived, chip_sums):
    myq = 2 * lax.axis_index("x") + lax.axis_index("y")
    return lax.dynamic_update_slice(arrived, lax.dynamic_slice_in_dim(chip_sums, myq, 1, axis=0), (myq, 0, 0))


def _join_halves(halves):
    n = len(halves)

    def body(*refs):
        ins, outs = refs[:n], refs[n:2 * n]
        send_sems, recv_sems = refs[2 * n:]
        x, y, c = _place()
        sibling = (x, y, 1 - c)
        cps = []
        for w in range(n):
            h = ins[w].shape[0]
            cp = pltpu.make_async_remote_copy(src_ref=ins[w], dst_ref=outs[w].at[pl.ds(c * h, h)], send_sem=send_sems.at[w],
                                              recv_sem=recv_sems.at[w], device_id=sibling, device_id_type=MESH)
            cp.start()
            cps.append(cp)
        for cp in cps:
            cp.wait()

    joined = pl.pallas_call(
        body, name="join_halves",
        out_shape=[SDS((2 * h.shape[0], h.shape[1]), h.dtype) for h in halves],
        in_specs=[_any()] * n, out_specs=[_any()] * n,
        scratch_shapes=[pltpu.SemaphoreType.DMA((n,)), pltpu.SemaphoreType.DMA((n,))],
    )(*halves)
    c = lax.axis_index("c")
    return [lax.dynamic_update_slice(j, h, (c * h.shape[0], 0)) for j, h in zip(joined, halves)]


def _chip_sum(name, part, other):
    nq, rows, cols = part.shape
    h = rows // 2
    rt = _row_tile(h)
    nb = h // rt

    def body(c_ref, p_ref, o_ref, out_ref):
        out_ref[...] = (p_ref[...] + o_ref[...]).astype(WIRE_DTYPE)

    grid_spec = pltpu.PrefetchScalarGridSpec(
        num_scalar_prefetch=1, grid=(nq, nb),
        in_specs=[pl.BlockSpec((1, rt, cols), lambda q, i, c_ref: (q, c_ref[0] * nb + i, 0)),
                  pl.BlockSpec((1, rt, cols), lambda q, i, c_ref: (q, i, 0))],
        out_specs=pl.BlockSpec((1, rt, cols), lambda q, i, c_ref: (q, i, 0)))
    return pl.pallas_call(
        body, name=name, grid_spec=grid_spec, out_shape=SDS((nq, h, cols), WIRE_DTYPE),
        compiler_params=_params("parallel", "parallel"),
    )(lax.axis_index("c").astype(jnp.int32).reshape(1), part, other)


def _chip_sums(tag, parts):
    from_sibling = _swap_halves("swap_halves_" + tag, parts)
    return [_chip_sum(f"chip_sum_{tag}_{w}", p, o) for w, (p, o) in enumerate(zip(parts, from_sibling))]


def _chip_order_sums(tag, arrived, chip_sums):
    return [_ordered_sum(f"chip_order_sum_{tag}_{w}", _own_chip_sum(a, s)) for w, (a, s) in enumerate(zip(arrived, chip_sums))]


def _pad_rows(a, rows):
    return jnp.pad(a, ((0, rows - a.shape[0]),) + ((0, 0),) * (a.ndim - 1))


def _pack_small(dmod, g1, g2, gf, b_in, ln_g, ln_b, conv_b, gn_g, gn_b, ga, gb, sb, cw32, sw, loss_row):
    v512 = jnp.concatenate([ln_g, ln_b, conv_b, gn_g, gn_b, ga, gb, jnp.zeros((1, DA), F32)], axis=1).reshape(4, D)
    rows = [dmod.reshape(8, D), g1, g2, gf, b_in.reshape(2, D), v512, sb.reshape(1, D), cw32.reshape(16, D),
            sw.reshape(CHUNK, D), loss_row]
    packed = jnp.concatenate(rows, axis=0)
    return _pad_rows(packed, PK_ROWS)


def _unpack_small(p):
    v512 = p[PK_V512:PK_V512 + 4].reshape(1, 8 * DA)
    pieces = [v512[:, k * DA:(k + 1) * DA] for k in range(7)]
    return dict(
        dmod=p[PK_DMOD:PK_DMOD + 8].reshape(1, 8 * D), norm1_g=p[PK_G1:PK_G1 + 1], norm2_g=p[PK_G2:PK_G2 + 1],
        norm_f_g=p[PK_GF:PK_GF + 1], b_in=p[PK_BIN:PK_BIN + 2].reshape(1, 2 * D),
        a_ln_g=pieces[0], a_ln_b=pieces[1], b_conv_b=pieces[2], b_gn_g=pieces[3], b_gn_b=pieces[4],
        out_norm_a_g=pieces[5], out_norm_b_g=pieces[6],
        a_spatial_b=p[PK_SB:PK_SB + 1].reshape(N_HEADS, CHUNK),
        b_conv_w=p[PK_CW:PK_CW + 16].reshape(HALO, DB),
        a_spatial_w=p[PK_SW:PK_SW + CHUNK].reshape(N_HEADS, CHUNK, CHUNK))


def kernel(x, c, ada_w, ada_b, norm1_g, w_in, b_in, a_ln_g, a_ln_b, a_spatial_w, a_spatial_b, b_conv_w, b_conv_b, b_gn_g, b_gn_b, out_norm_a_g, out_norm_b_g, w_out, norm2_g, w_ffn_in, w_ffn_out, ada_f_w, ada_f_b, norm_f_g, loss_target, m_ada_w, m_ada_b, m_norm1_g, m_w_in, m_b_in, m_a_ln_g, m_a_ln_b, m_a_spatial_w, m_a_spatial_b, m_b_conv_w, m_b_conv_b, m_b_gn_g, m_b_gn_b, m_out_norm_a_g, m_out_norm_b_g, m_w_out, m_norm2_g, m_w_ffn_in, m_w_ffn_out, m_ada_f_w, m_ada_f_b, m_norm_f_g, v_ada_w, v_ada_b, v_norm1_g, v_w_in, v_b_in, v_a_ln_g, v_a_ln_b, v_a_spatial_w, v_a_spatial_b, v_b_conv_w, v_b_conv_b, v_b_gn_g, v_b_gn_b, v_out_norm_a_g, v_out_norm_b_g, v_w_out, v_norm2_g, v_w_ffn_in, v_w_ffn_out, v_ada_f_w, v_ada_f_b, v_norm_f_g):
    mx, my, mc = _place()
    me = 4 * mx + 2 * my + mc
    myq = 2 * mx + my
    xs = x[0]
    target = loss_target[0]
    s = xs.shape[0]
    ada_w_cat = jnp.concatenate([ada_w[0], ada_f_w], axis=1)
    n_ada = ada_w.shape[2]

    cw_shard = _pad_rows(b_conv_w[0], HALO)
    first = _all_gather8("gather_c", jnp.concatenate([c.reshape(8, LANES), cw_shard], axis=0))
    c_all = first[:, 0:8, :].reshape(8, D)
    conv_w = jnp.concatenate([first[4 * (q // 2) + 2 * (q % 2), 8:8 + HALO, :] for q in range(NQ)], axis=1)
    cond_part = _cond_partial(c_all, ada_w_cat)
    cond_all = _all_gather8("gather_cond", cond_part)
    cond_q = [cond_all[4 * (q // 2) + 2 * (q % 2)] for q in range(NQ)]
    mod_all = jnp.concatenate([cq[:, :n_ada] for cq in cond_q] + [cq[:, n_ada:] for cq in cond_q], axis=1)
    mod = lax.dynamic_slice_in_dim(mod_all, me, 1, axis=0)
    mods = jnp.concatenate([ada_b, ada_f_b.reshape(1, 2 * D)], axis=1)

    mix_shards = [w_in[0].astype(MM_DTYPE), w_out[0].astype(MM_DTYPE)]
    ffn_shards = [w_ffn_in[0].astype(MM_DTYPE), w_ffn_out[0].astype(MM_DTYPE)]
    w_in4, w_out4 = [_own_piece(g_, s_) for g_, s_ in zip(_gather_weights(mix_shards), mix_shards)]
    w_out_f = w_out4.reshape(D, D)

    causal = jnp.tril(jnp.ones((CHUNK, CHUNK), dtype=bool))
    wm_f = jnp.where(causal[None], a_spatial_w[0], 0.0)
    wm = wm_f.astype(MM_DTYPE)
    wmt = jnp.swapaxes(wm_f, 1, 2).astype(MM_DTYPE)
    bst = jnp.repeat(a_spatial_b[0].T, HALF, axis=1)

    z, x1, yb1, y, w_ffn_in4, w_ffn_out4 = _mixer_fwd(
        xs, mod, mods, norm1_g, w_in4, b_in, a_ln_g, a_ln_b, wm, bst, conv_w, b_conv_b, b_gn_g, b_gn_b,
        out_norm_a_g, out_norm_b_g, w_out_f, ffn_shards)
    w_ffn_in4 = _own_piece(w_ffn_in4, ffn_shards[0])
    w_ffn_out_f = _own_piece(w_ffn_out4, ffn_shards[1]).reshape(DFF, D)
    g, up, h2, dx2, acc_f = _ffn_fwd(x1, target, mod, mods, norm2_g, norm_f_g, w_ffn_in4, w_ffn_out_f)

    dff, a_act, dx1, acc_2 = _ffn_bwd(dx2, x1, g, up, mod, mods, norm2_g, w_ffn_in4, w_ffn_out_f)
    gw_ffn_in4 = _grad_matmul("grad_w_ffn_in", h2, dff, D, PW_FF, True)
    gm2 = _grad_matmul("grad_w_ffn_out", a_act, dx2, PW_FF, D, False)
    modv = mod + mods
    gw_ffn_out, dgate2 = _gate_finalize("gate2_finalize", gm2, w_ffn_out_f, modv[:, 5 * D:6 * D])
    gm1 = _grad_matmul("grad_w_out", y, dx1, D, D, False)
    gw_out, dgate1 = _gate_finalize("gate1_finalize", gm1, w_out_f, modv[:, 2 * D:3 * D])
    early_chip_sums = _chip_sums("early", [gw_ffn_in4, gw_ffn_out.reshape(NQ, DFF // NQ, D), gw_out.reshape(NQ, D // NQ, D)])
    gx, dz, h, acc_1, acc_bin, acc_5, acc_cw, acc_sw, acc_sb, *early_arrived = _mixer_bwd(
        dx1, xs, z, yb1, mod, mods, norm1_g, w_in4, a_ln_g, a_ln_b, wm, wmt, bst, conv_w, b_gn_g, b_gn_b,
        out_norm_a_g, out_norm_b_g, w_out_f, early_chip_sums)

    dmod = jnp.concatenate([acc_1[0:1], acc_1[1:2], dgate1[0:1], acc_2[0:1], acc_2[1:2], dgate2[0:1],
                            acc_f[0:1], acc_f[1:2]], axis=1)
    sw_grad = jnp.where(causal[None], acc_sw, 0.0)
    sb_grad = acc_sb[:, ::HALF].T
    packed = _pack_small(dmod, acc_1[2:3], acc_2[2:3], acc_f[2:3], acc_bin[0:1], acc_5[2:3], acc_5[3:4], acc_5[6:7],
                         acc_5[4:5], acc_5[5:6], acc_5[0:1], acc_5[1:2], sb_grad, acc_cw, sw_grad, acc_f[4:5])
    gw_in4, gathered = _grad_matmul("grad_w_in", h, dz, D, PW_IN, True, gather_blk=packed)
    gathered = lax.dynamic_update_slice(gathered, packed[None], (me, 0, 0))
    late_chip_sums = _chip_sums("late", [gw_in4])
    late_arrived = _scatter_to_chips(late_chip_sums)
    g_w_in, g_w_ffn_in, g_w_ffn_out, g_w_out = _join_halves(
        _chip_order_sums("late", late_arrived, late_chip_sums) + _chip_order_sums("early", early_arrived, early_chip_sums))
    summed = _ordered_sum("small_grad_sum", gathered)
    loss = summed[PK_LOSS, 0]
    small = _unpack_small(summed)
    dmod_all = gathered[:, PK_DMOD:PK_DMOD + 8, :].reshape(8, 8 * D)
    dmod_cols = jnp.concatenate([lax.dynamic_slice_in_dim(dmod_all, myq * n_ada, n_ada, axis=1),
                                 lax.dynamic_slice_in_dim(dmod_all, 6 * D + myq * PW_IN, PW_IN, axis=1)], axis=1)
    g_ada_cat = _cond_grad(c_all, dmod_cols)

    grads = dict(
        ada_w=g_ada_cat[:, :n_ada], ada_b=small["dmod"][:, :6 * D], norm1_g=small["norm1_g"], w_in=g_w_in,
        b_in=small["b_in"], a_ln_g=small["a_ln_g"], a_ln_b=small["a_ln_b"], a_spatial_w=small["a_spatial_w"],
        a_spatial_b=small["a_spatial_b"],
        b_conv_w=lax.dynamic_slice_in_dim(small["b_conv_w"], myq * LANES, LANES, axis=1)[:CONV_W],
        b_conv_b=small["b_conv_b"], b_gn_g=small["b_gn_g"], b_gn_b=small["b_gn_b"],
        out_norm_a_g=small["out_norm_a_g"], out_norm_b_g=small["out_norm_b_g"], w_out=g_w_out,
        norm2_g=small["norm2_g"], w_ffn_in=g_w_ffn_in, w_ffn_out=g_w_ffn_out, ada_f_w=g_ada_cat[:, n_ada:],
        ada_f_b=small["dmod"][:, 6 * D:], norm_f_g=small["norm_f_g"])

    weights = dict(ada_w=ada_w, ada_b=ada_b, norm1_g=norm1_g, w_in=w_in, b_in=b_in, a_ln_g=a_ln_g, a_ln_b=a_ln_b,
                   a_spatial_w=a_spatial_w, a_spatial_b=a_spatial_b, b_conv_w=b_conv_w, b_conv_b=b_conv_b, b_gn_g=b_gn_g,
                   b_gn_b=b_gn_b, out_norm_a_g=out_norm_a_g, out_norm_b_g=out_norm_b_g, w_out=w_out, norm2_g=norm2_g,
                   w_ffn_in=w_ffn_in, w_ffn_out=w_ffn_out, ada_f_w=ada_f_w, ada_f_b=ada_f_b, norm_f_g=norm_f_g)
    m_in = dict(ada_w=m_ada_w, ada_b=m_ada_b, norm1_g=m_norm1_g, w_in=m_w_in, b_in=m_b_in, a_ln_g=m_a_ln_g, a_ln_b=m_a_ln_b,
                a_spatial_w=m_a_spatial_w, a_spatial_b=m_a_spatial_b, b_conv_w=m_b_conv_w, b_conv_b=m_b_conv_b,
                b_gn_g=m_b_gn_g, b_gn_b=m_b_gn_b, out_norm_a_g=m_out_norm_a_g, out_norm_b_g=m_out_norm_b_g, w_out=m_w_out,
                norm2_g=m_norm2_g, w_ffn_in=m_w_ffn_in, w_ffn_out=m_w_ffn_out, ada_f_w=m_ada_f_w, ada_f_b=m_ada_f_b,
                norm_f_g=m_norm_f_g)
    v_in = dict(ada_w=v_ada_w, ada_b=v_ada_b, norm1_g=v_norm1_g, w_in=v_w_in, b_in=v_b_in, a_ln_g=v_a_ln_g, a_ln_b=v_a_ln_b,
                a_spatial_w=v_a_spatial_w, a_spatial_b=v_a_spatial_b, b_conv_w=v_b_conv_w, b_conv_b=v_b_conv_b,
                b_gn_g=v_b_gn_g, b_gn_b=v_b_gn_b, out_norm_a_g=v_out_norm_a_g, out_norm_b_g=v_out_norm_b_g, w_out=v_w_out,
                norm2_g=v_norm2_g, w_ffn_in=v_w_ffn_in, w_ffn_out=v_w_ffn_out, ada_f_w=v_ada_f_w, ada_f_b=v_ada_f_b,
                norm_f_g=v_norm_f_g)
    names = list(weights)
    big = ("ada_w", "w_in", "w_out", "w_ffn_in", "w_ffn_out", "ada_f_w")

    def flat2(a):
        return a.reshape(-1, a.shape[-1])

    delta, new_m, new_v = {}, {}, {}
    for nm in big:
        shape = weights[nm].shape
        grads[nm] = grads[nm].reshape(shape)
        d_, m_, v_ = _adamw("adamw_" + nm, flat2(weights[nm]), flat2(grads[nm]), flat2(m_in[nm]), flat2(v_in[nm]))
        delta[nm], new_m[nm], new_v[nm] = d_.reshape(shape), m_.reshape(shape), v_.reshape(shape)

    small_names = [nm for nm in names if nm not in big]
    for nm in small_names:
        grads[nm] = grads[nm].reshape(weights[nm].shape)
    small_out = _adamw_many("adamw_small", *[[flat2(tree[nm]) for nm in small_names] for tree in (weights, grads, m_in, v_in)])
    for k, nm in enumerate(small_names):
        shape = weights[nm].shape
        delta[nm], new_m[nm], new_v[nm] = [o.reshape(shape) for o in small_out[3 * k:3 * k + 3]]

    grad_x = gx.reshape(x.shape)
    return (loss, grad_x, *[grads[nm] for nm in names], *[delta[nm] for nm in names],
            *[new_m[nm] for nm in names], *[new_v[nm] for nm in names])
```

```python
import functools
import math

import jax
import jax.numpy as jnp
from jax import lax
from jax.experimental import pallas as pl
from jax.experimental.pallas import tpu as pltpu

F32 = jnp.float32
MM_DTYPE = jnp.bfloat16
WIRE_DTYPE = jnp.bfloat16
SDS = jax.ShapeDtypeStruct
MESH = pl.DeviceIdType.MESH

D = 1024
DA = 512
DB = 512
NQ = 4
PW_IN = 512
DFF = 2816
PW_FF = 1408
CHUNK = 128
N_HEADS = 8
CONV_W = 31
HALO = 32
CONV_ROWS = 64
EPS = 1e-6
LANES = 128
HALF = 64

ROW_TILE = 256
FFN_ROW_TILE = 256
GRAD_ROW_TILE = 2048
VMEM_LIMIT = 60 * 1024 * 1024

ADAM_LR = 0.001
ADAM_B1 = 0.9
ADAM_B2 = 0.999
ADAM_EPS = 1e-08
ADAM_WD = 0.01
ADAM_STEP = 10

PK_DMOD = 0
PK_G1 = 8
PK_G2 = 9
PK_GF = 10
PK_BIN = 11
PK_V512 = 13
PK_SB = 17
PK_CW = 18
PK_SW = 34
PK_LOSS = 162
PK_ROWS = 168


def _dot(a, b):
    return jnp.dot(a, b, preferred_element_type=F32)


def _dot_nt(a, b):
    return lax.dot_general(a, b, (((1,), (1,)), ((), ())), preferred_element_type=F32)


def _dot_tn(a, b):
    return lax.dot_general(a, b, (((0,), (0,)), ((), ())), preferred_element_type=F32)


def _rowsum(x):
    return jnp.sum(x, axis=-1, keepdims=True)


def _colsum(x):
    return jnp.sum(x, axis=0, keepdims=True)


def _group_sum(x):
    rows, width = x.shape
    lo_mask = lax.broadcasted_iota(jnp.int32, (rows, LANES), 1) < HALF
    outs = []
    for jb in range(width // LANES):
        xb = x[:, jb * LANES:(jb + 1) * LANES]
        lo = _rowsum(jnp.where(lo_mask, xb, 0.0))
        hi = _rowsum(jnp.where(lo_mask, 0.0, xb))
        outs.append(jnp.where(lo_mask, lo, hi))
    return jnp.concatenate(outs, axis=-1)


def _sigmoid(x):
    return 1.0 / (1.0 + jnp.exp(-x))


def _gelu_parts(u):
    cdf = 0.5 * (1.0 + lax.erf(u * (1.0 / math.sqrt(2.0))))
    pdf = jnp.exp(-0.5 * u * u) * (1.0 / math.sqrt(2.0 * math.pi))
    return u * cdf, cdf + u * pdf


def _whole_vmem():
    return pl.BlockSpec(memory_space=pltpu.VMEM)


def _params(*semantics):
    return pltpu.CompilerParams(dimension_semantics=semantics, vmem_limit_bytes=VMEM_LIMIT)


def _mod_rows(mod_ref, modb_ref, first, count):
    m = mod_ref[...] + modb_ref[...]
    return [m[:, (first + k) * D:(first + k + 1) * D] for k in range(count)]


def _mixer_recompute(z_parts, lng, lnb, wm_ref, bst_ref, mix_ref):
    u, v, val, gate = z_parts
    rows = u.shape[0]
    gu, dgu = _gelu_parts(u)
    gv, dgv = _gelu_parts(v)
    mu = _rowsum(gv) * (1.0 / DA)
    vc = gv - mu
    rs = lax.rsqrt(_rowsum(vc * vc) * (1.0 / DA) + EPS)
    vhat = vc * rs
    vl = vhat * lng + lnb
    vlb = vl.astype(MM_DTYPE)
    lo_mask = lax.broadcasted_iota(jnp.int32, (CHUNK, LANES), 1) < HALF
    for ck in range(rows // CHUNK):
        for jb in range(DA // LANES):
            blk = vlb[ck * CHUNK:(ck + 1) * CHUNK, jb * LANES:(jb + 1) * LANES]
            a = _dot(wm_ref[2 * jb], blk)
            b = _dot(wm_ref[2 * jb + 1], blk)
            mix_ref[ck * CHUNK:(ck + 1) * CHUNK, jb * LANES:(jb + 1) * LANES] = (
                jnp.where(lo_mask, a, b) + bst_ref[:, jb * LANES:(jb + 1) * LANES])
    mixed = mix_ref[...]
    sg = _sigmoid(gate)
    yb0 = val * sg
    return dict(gu=gu, dgu=dgu, dgv=dgv, rs=rs, vhat=vhat, vlb=vlb, mixed=mixed, sg=sg, yb0=yb0)


def _conv_branch_tail(yb1, gng, gnb):
    gm = _group_sum(yb1) * (1.0 / HALF)
    gc = yb1 - gm
    grs = lax.rsqrt(_group_sum(gc * gc) * (1.0 / HALF) + EPS)
    ghat = gc * grs
    yb2 = ghat * gng + gnb
    s2 = _sigmoid(yb2)
    return dict(grs=grs, ghat=ghat, yb2=yb2, s2=s2, y_b=yb2 * s2)


def _shifted_copies(e_ref, sh_ref):
    n = sh_ref.shape[1]
    for b in range(1, 8):
        sh_ref[b - 1] = e_ref[pl.ds(b, n), :]


def _window(e_ref, sh_ref, offset, r0, nrows, cols):
    a, b = divmod(offset, 8)
    if b == 0:
        return e_ref[pl.ds(r0 + 8 * a, nrows), cols]
    return sh_ref[b - 1, pl.ds(r0 + 8 * a, nrows), cols]


def _conv_taps(e_ref, sh_ref, cw_ref, out_ref, ts, first_offset, flip, bias_ref=None, other_ref=None, tap_acc_ref=None):
    groups = CONV_ROWS // 8
    for cb in range(DB // LANES):
        cols = slice(cb * LANES, (cb + 1) * LANES)
        tap_acc = [jnp.zeros((8, LANES), F32) for _ in range(CONV_W)]
        for rb in range(ts // CONV_ROWS):
            r0 = rb * CONV_ROWS
            acc = jnp.zeros((CONV_ROWS, LANES), F32)
            if bias_ref is not None:
                acc = acc + bias_ref[:, cols]
            if other_ref is not None:
                other = other_ref[r0:r0 + CONV_ROWS, cols]
            for j in range(CONV_W):
                k = CONV_W - 1 - j if flip else j
                win = _window(e_ref, sh_ref, first_offset + j, r0, CONV_ROWS, cols)
                acc = acc + win * cw_ref[k:k + 1, cols]
                if other_ref is not None:
                    tap_acc[k] = tap_acc[k] + jnp.sum((other * win).reshape(groups, 8, LANES), axis=0)
            out_ref[r0:r0 + CONV_ROWS, cols] = acc
        if other_ref is not None:
            for k in range(CONV_W):
                tap_acc_ref[k:k + 1, cols] += _colsum(tap_acc[k])


def _gather_plan(ins, outs, send_sems, recv_sems):
    x, y, c = _place()
    sibling = (x, y, 1 - c)
    chips = _other_chips(x, y)
    myq = 2 * x + y

    def copy(w, k, q, hc, to, src=None):
        rows = ins[w].shape[0]
        dst = outs[w].at[q, pl.ds(hc * (rows // 2), rows // 2)]
        return pltpu.make_async_remote_copy(
            src_ref=dst if src is None else src, dst_ref=dst,
            send_sem=send_sems.at[w, k], recv_sem=recv_sems.at[w, k], device_id=to, device_id_type=MESH)

    def send():
        for w in range(len(ins)):
            rows = ins[w].shape[0]
            src = ins[w].at[pl.ds(c * (rows // 2), rows // 2)]
            for j, chip in enumerate(chips):
                copy(w, j, myq, c, (*chip, c), src=src).start()

    def forward():
        for w in range(len(ins)):
            for j, (qx, qy) in enumerate(chips):
                copy(w, j, 2 * qx + qy, c, sibling).wait_recv()
                copy(w, 3 + j, 2 * qx + qy, c, sibling).start()

    def finish():
        for w in range(len(ins)):
            for j, (qx, qy) in enumerate(chips):
                copy(w, 3 + j, 2 * qx + qy, 1 - c, sibling).wait_recv()
        for w in range(len(ins)):
            for k, (qx, qy) in enumerate(chips + chips):
                copy(w, k, 2 * qx + qy, c, sibling).wait_send()

    return send, forward, finish


def _mixer_fwd(x, mod, mods, norm1_g, w_in4, b_in, ln_g, ln_b, wm, bst, conv_w, conv_b, gn_g, gn_b, ga, gb, w_out,
               ffn_shards):
    s = x.shape[0]
    ts = min(ROW_TILE, s)
    nt = s // ts
    n_sh = len(ffn_shards)

    def body(x_ref, mod_ref, modb_ref, g1_ref, w4_ref, bin_ref, lng_ref, lnb_ref, wm_ref, bst_ref, cw_ref, cb_ref,
             gng_ref, gnb_ref, ga_ref, gb_ref, wout_ref, *rest):
        shard_refs, rest = rest[:n_sh], rest[n_sh:]
        z_ref, x1_ref, yb1_ref, y_ref = rest[:4]
        full_refs, rest = rest[4:4 + n_sh], rest[4 + n_sh:]
        e_ref, sh_ref, mix_ref, send_sems, recv_sems = rest
        i = pl.program_id(0)
        send, forward, finish = _gather_plan(shard_refs, full_refs, send_sems, recv_sems)

        @pl.when(i == 0)
        def _():
            send()
            e_ref[0:HALO, :] = jnp.zeros((HALO, DB), F32)

        @pl.when(i == (3 * nt) // 4)
        def _():
            forward()

        shift1, scale1, gate1 = _mod_rows(mod_ref, modb_ref, 0, 3)
        x_t = x_ref[...]
        r1 = lax.rsqrt(_rowsum(x_t * x_t) * (1.0 / D) + EPS)
        h = (x_t * r1 * g1_ref[...]) * (1.0 + scale1) + shift1
        hb = h.astype(MM_DTYPE)
        z_parts = []
        for q in range(NQ):
            zq = _dot(hb, w4_ref[q]) + bin_ref[:, q * PW_IN:(q + 1) * PW_IN]
            z_ref[:, q * PW_IN:(q + 1) * PW_IN] = zq
            z_parts.append(zq)
        r = _mixer_recompute(z_parts, lng_ref[...], lnb_ref[...], wm_ref, bst_ref, mix_ref)
        y_a = r["gu"] * r["mixed"]
        e_ref[HALO:HALO + ts, :] = r["yb0"]
        _shifted_copies(e_ref, sh_ref)
        _conv_taps(e_ref, sh_ref, cw_ref, yb1_ref, ts, HALO - (CONV_W - 1), False, bias_ref=cb_ref)
        e_ref[0:HALO, :] = e_ref[ts:ts + HALO, :]
        t = _conv_branch_tail(yb1_ref[...], gng_ref[...], gnb_ref[...])
        ra = lax.rsqrt(_rowsum(y_a * y_a) * (1.0 / DA) + EPS)
        rb = lax.rsqrt(_rowsum(t["y_b"] * t["y_b"]) * (1.0 / DB) + EPS)
        yan = (y_a * ra * ga_ref[...]).astype(MM_DTYPE)
        ybn = (t["y_b"] * rb * gb_ref[...]).astype(MM_DTYPE)
        y_ref[:, 0:DA] = yan
        y_ref[:, DA:D] = ybn
        o1 = _dot(yan, wout_ref[0:DA, :]) + _dot(ybn, wout_ref[DA:D, :])
        x1_ref[...] = x_t + gate1 * o1

        @pl.when(i == nt - 1)
        def _():
            finish()

    row = lambda w: pl.BlockSpec((ts, w), lambda i: (i, 0))
    full = lambda a: pl.BlockSpec(a.shape, lambda i: (0,) * a.ndim)
    return pl.pallas_call(
        body, name="mixer_fwd", grid=(nt,),
        in_specs=[row(D), full(mod), full(mods), full(norm1_g), _whole_vmem(), full(b_in), full(ln_g), full(ln_b),
                  _whole_vmem(), full(bst), full(conv_w), full(conv_b), full(gn_g), full(gn_b), full(ga), full(gb),
                  _whole_vmem()] + [_any()] * n_sh,
        out_specs=[row(4 * PW_IN), row(D), row(DB), row(D)] + [_any()] * n_sh,
        out_shape=[SDS((s, 4 * PW_IN), F32), SDS((s, D), F32), SDS((s, DB), F32), SDS((s, D), MM_DTYPE)]
        + [SDS((NQ,) + w.shape, w.dtype) for w in ffn_shards],
        scratch_shapes=[pltpu.VMEM((ts + HALO, DB), F32), pltpu.VMEM((7, ts + HALO - 8, DB), F32), pltpu.VMEM((ts, DA), F32),
                        pltpu.SemaphoreType.DMA((n_sh, 6)), pltpu.SemaphoreType.DMA((n_sh, 6))],
        compiler_params=_params("arbitrary"),
    )(x, mod, mods, norm1_g, w_in4, b_in, ln_g, ln_b, wm, bst, conv_w, conv_b, gn_g, gn_b, ga, gb, w_out, *ffn_shards)


def _ffn_fwd(x1, target, mod, mods, norm2_g, norm_f_g, w_ffn_in4, w_ffn_out):
    s = x1.shape[0]
    sub_rows = min(FFN_ROW_TILE, s)
    ts = min(2 * sub_rows, s)
    nt = s // ts

    def body(x1_ref, tgt_ref, mod_ref, modb_ref, g2_ref, gf_ref, wf_ref, wo_ref,
             g_ref, up_ref, h2_ref, dx2_ref, acc_ref):
        i = pl.program_id(0)

        @pl.when(i == 0)
        def _():
            acc_ref[...] = jnp.zeros(acc_ref.shape, F32)

        shift2, scale2, gate2, shift_f, scale_f = _mod_rows(mod_ref, modb_ref, 3, 5)
        for sub in range(ts // sub_rows):
            rows = slice(sub * sub_rows, (sub + 1) * sub_rows)
            x1_t = x1_ref[rows, :]
            r2 = lax.rsqrt(_rowsum(x1_t * x1_t) * (1.0 / D) + EPS)
            h2 = (x1_t * r2 * g2_ref[...]) * (1.0 + scale2) + shift2
            h2b = h2.astype(MM_DTYPE)
            h2_ref[rows, :] = h2b
            o2 = jnp.zeros((sub_rows, D), F32)
            for p in range(2):
                g = _dot(h2b, wf_ref[p])
                up = _dot(h2b, wf_ref[2 + p])
                g_ref[rows, p * PW_FF:(p + 1) * PW_FF] = g.astype(MM_DTYPE)
                up_ref[rows, p * PW_FF:(p + 1) * PW_FF] = up.astype(MM_DTYPE)
                a = (g * _sigmoid(g) * up).astype(MM_DTYPE)
                o2 = o2 + _dot(a, wo_ref[p * PW_FF:(p + 1) * PW_FF, :])
            x2 = x1_t + gate2 * o2
            rf = lax.rsqrt(_rowsum(x2 * x2) * (1.0 / D) + EPS)
            gf = gf_ref[...]
            nf = x2 * rf * gf
            err = nf * (1.0 + scale_f) + shift_f - tgt_ref[rows, :]
            d_out = err * (1.0 / D)
            d_nf = d_out * (1.0 + scale_f)
            t = d_nf * gf
            dx2_ref[rows, :] = rf * t - x2 * (rf * rf * rf) * (_rowsum(t * x2) * (1.0 / D))
            acc_ref[0:1, :] += _colsum(d_out)
            acc_ref[1:2, :] += _colsum(d_out * nf)
            acc_ref[2:3, :] += _colsum(d_nf * x2 * rf)
            acc_ref[3:4, :] += _colsum(err * err)

        @pl.when(i == nt - 1)
        def _():
            acc_ref[4:5, :] = jnp.zeros((1, D), F32) + _rowsum(acc_ref[3:4, :]) * (0.5 / D)

    row = lambda w: pl.BlockSpec((ts, w), lambda i: (i, 0))
    full = lambda a: pl.BlockSpec(a.shape, lambda i: (0,) * a.ndim)
    return pl.pallas_call(
        body, name="ffn_fwd", grid=(nt,),
        in_specs=[row(D), row(D), full(mod), full(mods), full(norm2_g), full(norm_f_g), _whole_vmem(), _whole_vmem()],
        out_specs=[row(DFF), row(DFF), row(D), row(D), pl.BlockSpec((8, D), lambda i: (0, 0))],
        out_shape=[SDS((s, DFF), MM_DTYPE), SDS((s, DFF), MM_DTYPE), SDS((s, D), MM_DTYPE), SDS((s, D), F32),
                   SDS((8, D), F32)],
        compiler_params=_params("arbitrary"),
    )(x1, target, mod, mods, norm2_g, norm_f_g, w_ffn_in4, w_ffn_out)


def _ffn_bwd(dx2, x1, g, up, mod, mods, norm2_g, w_ffn_in4, w_ffn_out):
    s = x1.shape[0]
    ts = min(FFN_ROW_TILE, s)
    nt = s // ts

    def body(dx2_ref, x1_ref, g_ref, up_ref, mod_ref, modb_ref, g2_ref, wf_ref, wo_ref,
             dff_ref, a_ref, dx1_ref, acc_ref):
        @pl.when(pl.program_id(0) == 0)
        def _():
            acc_ref[...] = jnp.zeros(acc_ref.shape, F32)

        shift2, scale2, gate2 = _mod_rows(mod_ref, modb_ref, 3, 3)
        dx2_t = dx2_ref[...]
        do2 = (dx2_t * gate2).astype(MM_DTYPE)
        dh2 = jnp.zeros((ts, D), F32)
        for p in range(2):
            da = _dot_nt(do2, wo_ref[p * PW_FF:(p + 1) * PW_FF, :])
            gp = g_ref[:, p * PW_FF:(p + 1) * PW_FF].astype(F32)
            upp = up_ref[:, p * PW_FF:(p + 1) * PW_FF].astype(F32)
            sg = _sigmoid(gp)
            silu = gp * sg
            a_ref[:, p * PW_FF:(p + 1) * PW_FF] = (silu * upp).astype(MM_DTYPE)
            dg = (da * upp * (sg * (1.0 + gp * (1.0 - sg)))).astype(MM_DTYPE)
            dup = (da * silu).astype(MM_DTYPE)
            dff_ref[:, p * PW_FF:(p + 1) * PW_FF] = dg
            dff_ref[:, DFF + p * PW_FF:DFF + (p + 1) * PW_FF] = dup
            dh2 = dh2 + _dot_nt(dg, wf_ref[p]) + _dot_nt(dup, wf_ref[2 + p])
        x1_t = x1_ref[...]
        r2 = lax.rsqrt(_rowsum(x1_t * x1_t) * (1.0 / D) + EPS)
        g2 = g2_ref[...]
        xr = x1_t * r2
        dn2 = dh2 * (1.0 + scale2)
        t = dn2 * g2
        dx1_ref[...] = dx2_t + r2 * t - x1_t * (r2 * r2 * r2) * (_rowsum(t * x1_t) * (1.0 / D))
        acc_ref[0:1, :] += _colsum(dh2)
        acc_ref[1:2, :] += _colsum(dh2 * (xr * g2))
        acc_ref[2:3, :] += _colsum(dn2 * xr)

    row = lambda w: pl.BlockSpec((ts, w), lambda i: (i, 0))
    full = lambda a: pl.BlockSpec(a.shape, lambda i: (0,) * a.ndim)
    return pl.pallas_call(
        body, name="ffn_bwd", grid=(nt,),
        in_specs=[row(D), row(D), row(DFF), row(DFF), full(mod), full(mods), full(norm2_g), _whole_vmem(), _whole_vmem()],
        out_specs=[row(2 * DFF), row(DFF), row(D), pl.BlockSpec((8, D), lambda i: (0, 0))],
        out_shape=[SDS((s, 2 * DFF), MM_DTYPE), SDS((s, DFF), MM_DTYPE), SDS((s, D), F32), SDS((8, D), F32)],
        compiler_params=_params("arbitrary"),
    )(dx2, x1, g, up, mod, mods, norm2_g, w_ffn_in4, w_ffn_out)


def _scatter_plan(ins, outs, send_sems, recv_sems):
    x, y, c = _place()
    me = 4 * x + 2 * y + c

    def copies():
        cps = []
        for w in range(len(ins)):
            h = ins[w].shape[1] // 2
            for k in range(1, 8):
                px, py, pc = (1 - x if k & 4 else x), (1 - y if k & 2 else y), (1 - c if k & 1 else c)
                cps.append(pltpu.make_async_remote_copy(
                    src_ref=ins[w].at[2 * px + py, pl.ds(pc * h, h)], dst_ref=outs[w].at[me],
                    send_sem=send_sems.at[w, k - 1], recv_sem=recv_sems.at[w, k - 1],
                    device_id=(px, py, pc), device_id_type=MESH))
        return cps

    def send():
        for cp in copies():
            cp.start()

    def finish():
        for cp in copies():
            cp.wait()

    return send, finish


def _scattered_shape(partial):
    nq, rows, cols = partial.shape
    return SDS((8, rows // 2, cols), partial.dtype)


def _mixer_bwd(dx1, x, z, yb1, mod, mods, norm1_g, w_in4, ln_g, ln_b, wm, wmt, bst, conv_w, gn_g, gn_b, ga, gb, w_out,
               partials):
    s = x.shape[0]
    ts = min(ROW_TILE, s)
    nt = s // ts
    n_cs = len(partials)

    def body(dx1_ref, x_ref, z_ref, yb1_ref, mod_ref, modb_ref, g1_ref, w4_ref, lng_ref, lnb_ref, wm_ref, wmt_ref,
             bst_ref, cw_ref, gng_ref, gnb_ref, ga_ref, gb_ref, wout_ref, *rest):
        cs_refs, rest = rest[:n_cs], rest[n_cs:]
        gx_ref, dz_ref, h_ref, a1_ref, a2_ref, a5_ref, acw_ref, asw_ref, asb_ref = rest[:9]
        arrived_refs, rest = rest[9:9 + n_cs], rest[9 + n_cs:]
        e_ref, sh_ref, mix_ref, dvl_ref, send_sems, recv_sems = rest
        i = pl.program_id(0)
        send, finish = _scatter_plan(cs_refs, arrived_refs, send_sems, recv_sems)

        @pl.when(i == 0)
        def _():
            send()
            e_ref[ts:ts + HALO, :] = jnp.zeros((HALO, DB), F32)
            for r in (a1_ref, a2_ref, a5_ref, acw_ref, asw_ref, asb_ref):
                r[...] = jnp.zeros(r.shape, F32)

        shift1, scale1, gate1 = _mod_rows(mod_ref, modb_ref, 0, 3)
        dx1_t = dx1_ref[...]
        do1 = (dx1_t * gate1).astype(MM_DTYPE)
        d_yan = _dot_nt(do1, wout_ref[0:DA, :])
        d_ybn = _dot_nt(do1, wout_ref[DA:D, :])

        z_parts = [z_ref[:, q * PW_IN:(q + 1) * PW_IN] for q in range(NQ)]
        u, v, val, gate = z_parts
        lng = lng_ref[...]
        r = _mixer_recompute(z_parts, lng, lnb_ref[...], wm_ref, bst_ref, mix_ref)
        gng = gng_ref[...]
        t = _conv_branch_tail(yb1_ref[...], gng, gnb_ref[...])
        y_a = r["gu"] * r["mixed"]
        y_b = t["y_b"]
        ga_v, gb_v = ga_ref[...], gb_ref[...]
        ra = lax.rsqrt(_rowsum(y_a * y_a) * (1.0 / DA) + EPS)
        rb = lax.rsqrt(_rowsum(y_b * y_b) * (1.0 / DB) + EPS)

        a5_ref[0:1, :] += _colsum(d_yan * y_a * ra)
        a5_ref[1:2, :] += _colsum(d_ybn * y_b * rb)
        ta = d_yan * ga_v
        d_ya = ra * ta - y_a * (ra * ra * ra) * (_rowsum(ta * y_a) * (1.0 / DA))
        tb = d_ybn * gb_v
        d_yb = rb * tb - y_b * (rb * rb * rb) * (_rowsum(tb * y_b) * (1.0 / DB))

        d_u = d_ya * r["mixed"] * r["dgu"]
        d_mixed = d_ya * r["gu"]
        dmb = d_mixed.astype(MM_DTYPE)
        lo_mask = lax.broadcasted_iota(jnp.int32, (CHUNK, LANES), 1) < HALF
        zero_blk = jnp.zeros((CHUNK, LANES), MM_DTYPE)
        sb_acc = jnp.zeros((CHUNK, DA), F32)
        for ck in range(ts // CHUNK):
            rows = slice(ck * CHUNK, (ck + 1) * CHUNK)
            sb_acc = sb_acc + d_mixed[rows, :]
            for jb in range(DA // LANES):
                cols = slice(jb * LANES, (jb + 1) * LANES)
                dm_blk = dmb[rows, cols]
                vl_blk = r["vlb"][rows, cols]
                da_ = _dot(wmt_ref[2 * jb], dm_blk)
                db_ = _dot(wmt_ref[2 * jb + 1], dm_blk)
                dvl_ref[rows, cols] = jnp.where(lo_mask, da_, db_)
                asw_ref[2 * jb] += _dot_nt(jnp.where(lo_mask, dm_blk, zero_blk), vl_blk)
                asw_ref[2 * jb + 1] += _dot_nt(jnp.where(lo_mask, zero_blk, dm_blk), vl_blk)
        asb_ref[...] += sb_acc
        d_vl = dvl_ref[...]
        a5_ref[2:3, :] += _colsum(d_vl * r["vhat"])
        a5_ref[3:4, :] += _colsum(d_vl)
        dvh = d_vl * lng
        d_gv = r["rs"] * (dvh - _rowsum(dvh) * (1.0 / DA) - r["vhat"] * (_rowsum(dvh * r["vhat"]) * (1.0 / DA)))
        d_v = d_gv * r["dgv"]

        yb2, s2 = t["yb2"], t["s2"]
        d_yb2 = d_yb * (s2 * (1.0 + yb2 * (1.0 - s2)))
        a5_ref[4:5, :] += _colsum(d_yb2 * t["ghat"])
        a5_ref[5:6, :] += _colsum(d_yb2)
        dgh = d_yb2 * gng
        d_yb1 = t["grs"] * (dgh - _group_sum(dgh) * (1.0 / HALF) - t["ghat"] * (_group_sum(dgh * t["ghat"]) * (1.0 / HALF)))
        a5_ref[6:7, :] += _colsum(d_yb1)
        e_ref[0:ts, :] = d_yb1
        _shifted_copies(e_ref, sh_ref)
        mix_ref[...] = r["yb0"]
        _conv_taps(e_ref, sh_ref, cw_ref, dvl_ref, ts, 0, True, other_ref=mix_ref, tap_acc_ref=acw_ref)
        d_yb0 = dvl_ref[...]
        e_ref[ts:ts + HALO, :] = e_ref[0:HALO, :]
        sg = r["sg"]
        d_val = d_yb0 * sg
        d_gate = d_yb0 * val * sg * (1.0 - sg)

        dh = jnp.zeros((ts, D), F32)
        for q, dzq in enumerate((d_u, d_v, d_val, d_gate)):
            a2_ref[0:1, q * PW_IN:(q + 1) * PW_IN] += _colsum(dzq)
            dzb = dzq.astype(MM_DTYPE)
            dz_ref[:, q * PW_IN:(q + 1) * PW_IN] = dzb
            dh = dh + _dot_nt(dzb, w4_ref[q])
        x_t = x_ref[...]
        r1 = lax.rsqrt(_rowsum(x_t * x_t) * (1.0 / D) + EPS)
        g1 = g1_ref[...]
        xr = x_t * r1
        n1 = xr * g1
        h_ref[...] = (n1 * (1.0 + scale1) + shift1).astype(MM_DTYPE)
        dn1 = dh * (1.0 + scale1)
        t1 = dn1 * g1
        gx_ref[...] = dx1_t + r1 * t1 - x_t * (r1 * r1 * r1) * (_rowsum(t1 * x_t) * (1.0 / D))
        a1_ref[0:1, :] += _colsum(dh)
        a1_ref[1:2, :] += _colsum(dh * n1)
        a1_ref[2:3, :] += _colsum(dn1 * xr)

        @pl.when(i == nt - 1)
        def _():
            asb_ref[...] = _group_sum(asb_ref[...])
            finish()

    row = lambda w: pl.BlockSpec((ts, w), lambda i: (nt - 1 - i, 0))
    full = lambda a: pl.BlockSpec(a.shape, lambda i: (0,) * a.ndim)
    keep = lambda shape: pl.BlockSpec(shape, lambda i: (0,) * len(shape))
    return pl.pallas_call(
        body, name="mixer_bwd", grid=(nt,),
        in_specs=[row(D), row(D), row(4 * PW_IN), row(DB), full(mod), full(mods), full(norm1_g), _whole_vmem(),
                  full(ln_g), full(ln_b), _whole_vmem(), _whole_vmem(), full(bst), full(conv_w), full(gn_g), full(gn_b),
                  full(ga), full(gb), _whole_vmem()] + [_any()] * n_cs,
        out_specs=[row(D), row(4 * PW_IN), row(D), keep((8, D)), keep((8, 4 * PW_IN)), keep((8, DA)),
                   keep((HALO, DB)), keep((N_HEADS, CHUNK, CHUNK)), keep((CHUNK, DA))] + [_any()] * n_cs,
        out_shape=[SDS((s, D), F32), SDS((s, 4 * PW_IN), MM_DTYPE), SDS((s, D), MM_DTYPE), SDS((8, D), F32),
                   SDS((8, 4 * PW_IN), F32), SDS((8, DA), F32), SDS((HALO, DB), F32),
                   SDS((N_HEADS, CHUNK, CHUNK), F32), SDS((CHUNK, DA), F32)]
        + [_scattered_shape(p) for p in partials],
        scratch_shapes=[pltpu.VMEM((ts + HALO, DB), F32), pltpu.VMEM((7, ts + HALO - 8, DB), F32),
                        pltpu.VMEM((ts, DA), F32), pltpu.VMEM((ts, DA), F32),
                        pltpu.SemaphoreType.DMA((n_cs, 7)), pltpu.SemaphoreType.DMA((n_cs, 7))],
        compiler_params=_params("arbitrary"),
    )(dx1, x, z, yb1, mod, mods, norm1_g, w_in4, ln_g, ln_b, wm, wmt, bst, conv_w, gn_g, gn_b, ga, gb, w_out, *partials)


def _gather8_plan(x_ref, out_ref, send_sems, recv_sems):
    x, y, c = _place()
    me, sibling = (x, y, c), (x, y, 1 - c)
    chips = _other_chips(x, y)

    def copy(k, block, to, src=None):
        dst = out_ref.at[4 * block[0] + 2 * block[1] + block[2]]
        return pltpu.make_async_remote_copy(src_ref=dst if src is None else src, dst_ref=dst, send_sem=send_sems.at[k],
                                            recv_sem=recv_sems.at[k], device_id=to, device_id_type=MESH)

    def send():
        copy(0, me, sibling, src=x_ref).start()
        for j, chip in enumerate(chips):
            copy(1 + j, me, (*chip, c), src=x_ref).start()

    def forward():
        for j, chip in enumerate(chips):
            copy(1 + j, (*chip, c), me).wait_recv()
            copy(4 + j, (*chip, c), sibling).start()

    def finish():
        copy(0, sibling, me).wait_recv()
        for j, chip in enumerate(chips):
            copy(4 + j, (*chip, 1 - c), me).wait_recv()
        for k in range(7):
            copy(k, me, sibling).wait_send()

    return send, forward, finish


def _grad_matmul(name, a, b, ka_tile, nb_tile, piece_w=None, gated=None, gather_blk=None):
    s, ka = a.shape
    nb = b.shape[1]
    ts = min(GRAD_ROW_TILE, s)
    nt = s // ts
    nja, njb = ka // ka_tile, nb // nb_tile
    steps = nja * njb * nt
    n_in = 2 + (2 if gated else 0) + (1 if gather_blk is not None else 0)
    n_out = 1 + (1 if gated else 0) + (1 if gather_blk is not None else 0)
    assert not (gated and njb != 1) and not (piece_w and nja != 1)

    def body(*refs):
        ins, outs, scratch = refs[:n_in], refs[n_in:n_in + n_out], refs[n_in + n_out:]
        a_ref, b_ref, o_ref, acc_ref = ins[0], ins[1], outs[0], scratch[0]
        step = (pl.program_id(0) * njb + pl.program_id(1)) * nt + pl.program_id(2)
        if gather_blk is not None:
            send, forward, finish = _gather8_plan(ins[-1], outs[-1], scratch[1], scratch[2])

            @pl.when(step == 0)
            def _():
                send()

            @pl.when(step == steps // 2)
            def _():
                forward()

        prod = _dot_tn(a_ref[...].astype(MM_DTYPE), b_ref[...].astype(MM_DTYPE))

        @pl.when(pl.program_id(2) == 0)
        def _():
            acc_ref[...] = prod

        @pl.when(pl.program_id(2) > 0)
        def _():
            acc_ref[...] += prod

        @pl.when(pl.program_id(2) == nt - 1)
        def _():
            gm = acc_ref[...]
            if gated:
                gate_ref, w_ref, dg_ref = ins[2], ins[3], outs[1]

                @pl.when(step == nt - 1)
                def _():
                    dg_ref[...] = jnp.zeros(dg_ref.shape, F32)

                dg_ref[0:1, :] += _colsum(gm * w_ref[...].astype(F32))
                gm = gm * gate_ref[...]
            if piece_w:
                for q in range(nb_tile // piece_w):
                    o_ref[q] = gm[:, q * piece_w:(q + 1) * piece_w].astype(WIRE_DTYPE)
            else:
                o_ref[...] = gm.astype(WIRE_DTYPE)

        if gather_blk is not None:
            @pl.when(step == steps - 1)
            def _():
                finish()

    in_specs = [pl.BlockSpec((ts, ka_tile), lambda ja, jb, i: (i, ja)),
                pl.BlockSpec((ts, nb_tile), lambda ja, jb, i: (i, jb))]
    operands = [a, b]
    if piece_w:
        out_shape = [SDS((nb // piece_w, ka, piece_w), WIRE_DTYPE)]
        out_specs = [pl.BlockSpec((nb_tile // piece_w, ka, piece_w), lambda ja, jb, i: (jb, 0, 0))]
    else:
        out_shape = [SDS((ka, nb), WIRE_DTYPE)]
        out_specs = [pl.BlockSpec((ka_tile, nb_tile), lambda ja, jb, i: (ja, jb))]
    scratch = [pltpu.VMEM((ka_tile, nb_tile), F32)]
    if gated:
        in_specs += [pl.BlockSpec((1, nb_tile), lambda ja, jb, i: (0, jb)),
                     pl.BlockSpec((ka_tile, nb_tile), lambda ja, jb, i: (ja, jb))]
        operands += list(gated)
        out_shape.append(SDS((8, nb), F32))
        out_specs.append(pl.BlockSpec((8, nb_tile), lambda ja, jb, i: (0, jb)))
    if gather_blk is not None:
        in_specs.append(_any())
        operands.append(gather_blk)
        out_shape.append(SDS((8,) + gather_blk.shape, gather_blk.dtype))
        out_specs.append(_any())
        scratch += [pltpu.SemaphoreType.DMA((7,)), pltpu.SemaphoreType.DMA((7,))]
    return pl.pallas_call(
        body, name=name, grid=(nja, njb, nt), in_specs=in_specs, out_specs=out_specs, out_shape=out_shape,
        scratch_shapes=scratch, compiler_params=_params("arbitrary", "arbitrary", "arbitrary"),
    )(*operands)


def _cond_partial(c_all, w_cat):
    n = w_cat.shape[1]
    nt_cols = 512

    def body(c_ref, w_ref, o_ref):
        c_t = c_ref[...]
        ca = (c_t * _sigmoid(c_t)).astype(MM_DTYPE)
        o_ref[...] = _dot(ca, w_ref[...].astype(MM_DTYPE))

    return pl.pallas_call(
        body, name="cond_partial", grid=(n // nt_cols,),
        in_specs=[pl.BlockSpec((8, D), lambda j: (0, 0)), pl.BlockSpec((D, nt_cols), lambda j: (0, j))],
        out_specs=pl.BlockSpec((8, nt_cols), lambda j: (0, j)), out_shape=SDS((8, n), F32),
        compiler_params=_params("parallel"),
    )(c_all, w_cat)


def _cond_grad(c_all, dmod_cols):
    n = dmod_cols.shape[1]
    nt_cols = 512

    def body(c_ref, d_ref, o_ref):
        c_t = c_ref[...]
        ca = jnp.concatenate([c_t * _sigmoid(c_t), jnp.zeros((8, D), F32)], axis=0).astype(MM_DTYPE)
        dm = jnp.concatenate([d_ref[...], jnp.zeros((8, nt_cols), F32)], axis=0).astype(MM_DTYPE)
        o_ref[...] = _dot_tn(ca, dm)

    return pl.pallas_call(
        body, name="cond_grad", grid=(n // nt_cols,),
        in_specs=[pl.BlockSpec((8, D), lambda j: (0, 0)), pl.BlockSpec((8, nt_cols), lambda j: (0, j))],
        out_specs=pl.BlockSpec((D, nt_cols), lambda j: (0, j)), out_shape=SDS((D, n), F32),
        compiler_params=_params("parallel"),
    )(c_all, dmod_cols)


def _row_tile(rows, cap=256):
    if rows <= cap:
        return rows
    for t in range(cap, 7, -8):
        if rows % t == 0:
            return t
    return rows


def _ordered_sum(name, parts, out_dtype=F32):
    n, rows, cols = parts.shape
    rt = _row_tile(rows)

    def body(p_ref, o_ref):
        acc = p_ref[0].astype(F32)
        for k in range(1, n):
            acc = acc + p_ref[k].astype(F32)
        o_ref[...] = acc.astype(out_dtype)

    return pl.pallas_call(
        body, name=name, grid=(rows // rt,),
        in_specs=[pl.BlockSpec((n, rt, cols), lambda i: (0, i, 0))],
        out_specs=pl.BlockSpec((rt, cols), lambda i: (i, 0)), out_shape=SDS((rows, cols), out_dtype),
        compiler_params=_params("parallel"),
    )(parts)


def _adamw_update(w_ref, g_ref, m_ref, v_ref, d_ref, nm_ref, nv_ref):
    c1 = 1.0 - ADAM_B1 ** ADAM_STEP
    c2 = 1.0 - ADAM_B2 ** ADAM_STEP
    g_t = g_ref[...]
    m_new = ADAM_B1 * m_ref[...] + (1.0 - ADAM_B1) * g_t
    v_new = ADAM_B2 * v_ref[...] + (1.0 - ADAM_B2) * (g_t * g_t)
    nm_ref[...] = m_new
    nv_ref[...] = v_new
    d_ref[...] = -ADAM_LR * ((m_new / c1) / (jnp.sqrt(v_new / c2) + ADAM_EPS) + ADAM_WD * w_ref[...])


def _adamw_many(name, ws, gs, ms, vs):
    n = len(ws)

    def body(*refs):
        ins, outs = refs[:4 * n], refs[4 * n:]
        for k in range(n):
            _adamw_update(ins[k], ins[n + k], ins[2 * n + k], ins[3 * n + k], *outs[3 * k:3 * k + 3])

    return pl.pallas_call(
        body, name=name, out_shape=[SDS(w.shape, F32) for w in ws for _ in range(3)],
        compiler_params=pltpu.CompilerParams(vmem_limit_bytes=VMEM_LIMIT),
    )(*ws, *gs, *ms, *vs)


def _adamw(name, w, g, m, v):
    rows, cols = w.shape
    rt = _row_tile(rows)

    def body(*refs):
        _adamw_update(*refs)

    spec = pl.BlockSpec((rt, cols), lambda i: (i, 0))
    out = SDS((rows, cols), F32)
    return pl.pallas_call(body, name=name, grid=(rows // rt,), in_specs=[spec] * 4, out_specs=[spec] * 3,
                          out_shape=[out, out, out], compiler_params=_params("parallel"))(w, g, m, v)


def _place():
    return lax.axis_index("x"), lax.axis_index("y"), lax.axis_index("c")


def _other_chips(x, y):
    return [(1 - x, y), (x, 1 - y), (1 - x, 1 - y)]


def _all_gather8(name, blk):
    m, n = blk.shape

    def body(x_ref, out_ref, send_sems, recv_sems, local_sem):
        x, y, c = _place()
        me, sibling = (x, y, c), (x, y, 1 - c)
        chips = _other_chips(x, y)

        def slot(px, py, pc):
            return out_ref.at[4 * px + 2 * py + pc]

        def copy(k, block, to, src=None):
            return pltpu.make_async_remote_copy(
                src_ref=slot(*block) if src is None else src, dst_ref=slot(*block),
                send_sem=send_sems.at[k], recv_sem=recv_sems.at[k], device_id=to, device_id_type=MESH)

        mine = pltpu.make_async_copy(x_ref, slot(*me), local_sem)
        mine.start()
        first = [copy(0, me, sibling, src=x_ref)]
        first += [copy(1 + j, me, (*chip, c), src=x_ref) for j, chip in enumerate(chips)]
        for cp in first:
            cp.start()
        passed = [copy(4 + j, (*chip, c), sibling) for j, chip in enumerate(chips)]
        for j, chip in enumerate(chips):
            copy(1 + j, (*chip, c), me).wait_recv()
            passed[j].start()
        copy(0, sibling, me).wait_recv()
        for j, chip in enumerate(chips):
            copy(4 + j, (*chip, 1 - c), me).wait_recv()
        for cp in first + passed:
            cp.wait_send()
        mine.wait()

    return pl.pallas_call(
        body, name=name, out_shape=SDS((8, m, n), blk.dtype),
        in_specs=[_whole_vmem()], out_specs=_whole_vmem(),
        scratch_shapes=[pltpu.SemaphoreType.DMA((7,)), pltpu.SemaphoreType.DMA((7,)), pltpu.SemaphoreType.DMA],
        compiler_params=pltpu.CompilerParams(vmem_limit_bytes=VMEM_LIMIT),
    )(blk)


def _any():
    return pl.BlockSpec(memory_space=pl.ANY)


def _gather_weights(shards):
    n = len(shards)

    def body(*refs):
        send, forward, finish = _gather_plan(refs[:n], refs[n:2 * n], *refs[2 * n:])
        send()
        forward()
        finish()

    return pl.pallas_call(
        body, name="gather_weights",
        out_shape=[SDS((NQ,) + s.shape, s.dtype) for s in shards],
        in_specs=[_any()] * n, out_specs=[_any()] * n,
        scratch_shapes=[pltpu.SemaphoreType.DMA((n, 6)), pltpu.SemaphoreType.DMA((n, 6))],
    )(*shards)


def _own_piece(gathered, shard):
    myq = 2 * lax.axis_index("x") + lax.axis_index("y")
    return lax.dynamic_update_slice(gathered, shard[None], (myq,) + (0,) * shard.ndim)


def _scatter_to_owners(partials):
    n = len(partials)

    def body(*refs):
        send, finish = _scatter_plan(refs[:n], refs[n:2 * n], *refs[2 * n:])
        send()
        finish()

    return pl.pallas_call(
        body, name="scatter_to_owners",
        out_shape=[_scattered_shape(p) for p in partials],
        in_specs=[_any()] * n, out_specs=[_any()] * n,
        scratch_shapes=[pltpu.SemaphoreType.DMA((n, 7)), pltpu.SemaphoreType.DMA((n, 7))],
    )(*partials)


def _owner_sums(tag, arrived, partials):
    x, y, c = _place()
    sums = []
    for w, (arr, part) in enumerate(zip(arrived, partials)):
        h = part.shape[1] // 2
        own = lax.dynamic_slice(part, (2 * x + y, c * h, 0), (1, h, part.shape[2]))
        sums.append(_ordered_sum(f"owner_sum_{tag}_{w}", lax.dynamic_update_slice(arr, own, (4 * x + 2 * y + c, 0, 0))))
    return sums


def _join_halves(halves):
    n = len(halves)

    def body(*refs):
        ins, outs = refs[:n], refs[n:2 * n]
        send_sems, recv_sems = refs[2 * n:]
        x, y, c = _place()
        sibling = (x, y, 1 - c)
        cps = []
        for w in range(n):
            h = ins[w].shape[0]
            cp = pltpu.make_async_remote_copy(src_ref=ins[w], dst_ref=outs[w].at[pl.ds(c * h, h)], send_sem=send_sems.at[w],
                                              recv_sem=recv_sems.at[w], device_id=sibling, device_id_type=MESH)
            cp.start()
            cps.append(cp)
        for cp in cps:
            cp.wait()

    joined = pl.pallas_call(
        body, name="join_halves",
        out_shape=[SDS((2 * h.shape[0], h.shape[1]), h.dtype) for h in halves],
        in_specs=[_any()] * n, out_specs=[_any()] * n,
        scratch_shapes=[pltpu.SemaphoreType.DMA((n,)), pltpu.SemaphoreType.DMA((n,))],
    )(*halves)
    c = lax.axis_index("c")
    return [lax.dynamic_update_slice(j, h, (c * h.shape[0], 0)) for j, h in zip(joined, halves)]


def _pad_rows(a, rows):
    return jnp.pad(a, ((0, rows - a.shape[0]),) + ((0, 0),) * (a.ndim - 1))


def _pack_small(dmod, g1, g2, gf, b_in, ln_g, ln_b, conv_b, gn_g, gn_b, ga, gb, sb, cw32, sw, loss_row):
    v512 = jnp.concatenate([ln_g, ln_b, conv_b, gn_g, gn_b, ga, gb, jnp.zeros((1, DA), F32)], axis=1).reshape(4, D)
    rows = [dmod.reshape(8, D), g1, g2, gf, b_in.reshape(2, D), v512, sb.reshape(1, D), cw32.reshape(16, D),
            sw.reshape(CHUNK, D), loss_row]
    packed = jnp.concatenate(rows, axis=0)
    return _pad_rows(packed, PK_ROWS)


def _unpack_small(p):
    v512 = p[PK_V512:PK_V512 + 4].reshape(1, 8 * DA)
    pieces = [v512[:, k * DA:(k + 1) * DA] for k in range(7)]
    return dict(
        dmod=p[PK_DMOD:PK_DMOD + 8].reshape(1, 8 * D), norm1_g=p[PK_G1:PK_G1 + 1], norm2_g=p[PK_G2:PK_G2 + 1],
        norm_f_g=p[PK_GF:PK_GF + 1], b_in=p[PK_BIN:PK_BIN + 2].reshape(1, 2 * D),
        a_ln_g=pieces[0], a_ln_b=pieces[1], b_conv_b=pieces[2], b_gn_g=pieces[3], b_gn_b=pieces[4],
        out_norm_a_g=pieces[5], out_norm_b_g=pieces[6],
        a_spatial_b=p[PK_SB:PK_SB + 1].reshape(N_HEADS, CHUNK),
        b_conv_w=p[PK_CW:PK_CW + 16].reshape(HALO, DB),
        a_spatial_w=p[PK_SW:PK_SW + CHUNK].reshape(N_HEADS, CHUNK, CHUNK))


def kernel(x, c, ada_w, ada_b, norm1_g, w_in, b_in, a_ln_g, a_ln_b, a_spatial_w, a_spatial_b, b_conv_w, b_conv_b, b_gn_g, b_gn_b, out_norm_a_g, out_norm_b_g, w_out, norm2_g, w_ffn_in, w_ffn_out, ada_f_w, ada_f_b, norm_f_g, loss_target, m_ada_w, m_ada_b, m_norm1_g, m_w_in, m_b_in, m_a_ln_g, m_a_ln_b, m_a_spatial_w, m_a_spatial_b, m_b_conv_w, m_b_conv_b, m_b_gn_g, m_b_gn_b, m_out_norm_a_g, m_out_norm_b_g, m_w_out, m_norm2_g, m_w_ffn_in, m_w_ffn_out, m_ada_f_w, m_ada_f_b, m_norm_f_g, v_ada_w, v_ada_b, v_norm1_g, v_w_in, v_b_in, v_a_ln_g, v_a_ln_b, v_a_spatial_w, v_a_spatial_b, v_b_conv_w, v_b_conv_b, v_b_gn_g, v_b_gn_b, v_out_norm_a_g, v_out_norm_b_g, v_w_out, v_norm2_g, v_w_ffn_in, v_w_ffn_out, v_ada_f_w, v_ada_f_b, v_norm_f_g):
    mx, my, mc = _place()
    me = 4 * mx + 2 * my + mc
    myq = 2 * mx + my
    xs = x[0]
    target = loss_target[0]
    s = xs.shape[0]
    ada_w_cat = jnp.concatenate([ada_w[0], ada_f_w], axis=1)
    n_ada = ada_w.shape[2]

    cw_shard = _pad_rows(b_conv_w[0], HALO)
    first = _all_gather8("gather_c", jnp.concatenate([c.reshape(8, LANES), cw_shard], axis=0))
    c_all = first[:, 0:8, :].reshape(8, D)
    conv_w = jnp.concatenate([first[4 * (q // 2) + 2 * (q % 2), 8:8 + HALO, :] for q in range(NQ)], axis=1)
    cond_part = _cond_partial(c_all, ada_w_cat)
    cond_all = _all_gather8("gather_cond", cond_part)
    cond_q = [cond_all[4 * (q // 2) + 2 * (q % 2)] for q in range(NQ)]
    mod_all = jnp.concatenate([cq[:, :n_ada] for cq in cond_q] + [cq[:, n_ada:] for cq in cond_q], axis=1)
    mod = lax.dynamic_slice_in_dim(mod_all, me, 1, axis=0)
    mods = jnp.concatenate([ada_b, ada_f_b.reshape(1, 2 * D)], axis=1)

    mix_shards = [w_in[0].astype(MM_DTYPE), w_out[0].astype(MM_DTYPE)]
    ffn_shards = [w_ffn_in[0].astype(MM_DTYPE), w_ffn_out[0].astype(MM_DTYPE)]
    w_in4, w_out4 = [_own_piece(g_, s_) for g_, s_ in zip(_gather_weights(mix_shards), mix_shards)]
    w_out_f = w_out4.reshape(D, D)

    causal = jnp.tril(jnp.ones((CHUNK, CHUNK), dtype=bool))
    wm_f = jnp.where(causal[None], a_spatial_w[0], 0.0)
    wm = wm_f.astype(MM_DTYPE)
    wmt = jnp.swapaxes(wm_f, 1, 2).astype(MM_DTYPE)
    bst = jnp.repeat(a_spatial_b[0].T, HALF, axis=1)

    z, x1, yb1, y, w_ffn_in4, w_ffn_out4 = _mixer_fwd(
        xs, mod, mods, norm1_g, w_in4, b_in, a_ln_g, a_ln_b, wm, bst, conv_w, b_conv_b, b_gn_g, b_gn_b,
        out_norm_a_g, out_norm_b_g, w_out_f, ffn_shards)
    w_ffn_in4 = _own_piece(w_ffn_in4, ffn_shards[0])
    w_ffn_out_f = _own_piece(w_ffn_out4, ffn_shards[1]).reshape(DFF, D)
    g, up, h2, dx2, acc_f = _ffn_fwd(x1, target, mod, mods, norm2_g, norm_f_g, w_ffn_in4, w_ffn_out_f)

    dff, a_act, dx1, acc_2 = _ffn_bwd(dx2, x1, g, up, mod, mods, norm2_g, w_ffn_in4, w_ffn_out_f)
    (gw_ffn_in4,) = _grad_matmul("grad_w_ffn_in", h2, dff, D, PW_FF, piece_w=PW_FF)
    modv = mod + mods
    gw_ffn_out, dgate2 = _grad_matmul("grad_w_ffn_out", a_act, dx2, PW_FF, D, gated=(modv[:, 5 * D:6 * D], w_ffn_out_f))
    gw_out, dgate1 = _grad_matmul("grad_w_out", y, dx1, D, D, gated=(modv[:, 2 * D:3 * D], w_out_f))
    early_partials = [gw_ffn_in4, gw_ffn_out.reshape(NQ, DFF // NQ, D), gw_out.reshape(NQ, D // NQ, D)]
    gx, dz, h, acc_1, acc_bin, acc_5, acc_cw, acc_sw, acc_sb, *early_arrived = _mixer_bwd(
        dx1, xs, z, yb1, mod, mods, norm1_g, w_in4, a_ln_g, a_ln_b, wm, wmt, bst, conv_w, b_gn_g, b_gn_b,
        out_norm_a_g, out_norm_b_g, w_out_f, early_partials)

    dmod = jnp.concatenate([acc_1[0:1], acc_1[1:2], dgate1[0:1], acc_2[0:1], acc_2[1:2], dgate2[0:1],
                            acc_f[0:1], acc_f[1:2]], axis=1)
    sw_grad = jnp.where(causal[None], acc_sw, 0.0)
    sb_grad = acc_sb[:, ::HALF].T
    packed = _pack_small(dmod, acc_1[2:3], acc_2[2:3], acc_f[2:3], acc_bin[0:1], acc_5[2:3], acc_5[3:4], acc_5[6:7],
                         acc_5[4:5], acc_5[5:6], acc_5[0:1], acc_5[1:2], sb_grad, acc_cw, sw_grad, acc_f[4:5])
    gw_in4, gathered = _grad_matmul("grad_w_in", h, dz, D, NQ * PW_IN, piece_w=PW_IN, gather_blk=packed)
    gathered = lax.dynamic_update_slice(gathered, packed[None], (me, 0, 0))
    late_arrived = _scatter_to_owners([gw_in4])
    g_w_in, g_w_ffn_in, g_w_ffn_out, g_w_out = _join_halves(
        _owner_sums("late", late_arrived, [gw_in4]) + _owner_sums("early", early_arrived, early_partials))
    summed = _ordered_sum("small_grad_sum", gathered)
    loss = summed[PK_LOSS, 0]
    small = _unpack_small(summed)
    dmod_all = gathered[:, PK_DMOD:PK_DMOD + 8, :].reshape(8, 8 * D)
    dmod_cols = jnp.concatenate([lax.dynamic_slice_in_dim(dmod_all, myq * n_ada, n_ada, axis=1),
                                 lax.dynamic_slice_in_dim(dmod_all, 6 * D + myq * PW_IN, PW_IN, axis=1)], axis=1)
    g_ada_cat = _cond_grad(c_all, dmod_cols)

    grads = dict(
        ada_w=g_ada_cat[:, :n_ada], ada_b=small["dmod"][:, :6 * D], norm1_g=small["norm1_g"], w_in=g_w_in,
        b_in=small["b_in"], a_ln_g=small["a_ln_g"], a_ln_b=small["a_ln_b"], a_spatial_w=small["a_spatial_w"],
        a_spatial_b=small["a_spatial_b"],
        b_conv_w=lax.dynamic_slice_in_dim(small["b_conv_w"], myq * LANES, LANES, axis=1)[:CONV_W],
        b_conv_b=small["b_conv_b"], b_gn_g=small["b_gn_g"], b_gn_b=small["b_gn_b"],
        out_norm_a_g=small["out_norm_a_g"], out_norm_b_g=small["out_norm_b_g"], w_out=g_w_out,
        norm2_g=small["norm2_g"], w_ffn_in=g_w_ffn_in, w_ffn_out=g_w_ffn_out, ada_f_w=g_ada_cat[:, n_ada:],
        ada_f_b=small["dmod"][:, 6 * D:], norm_f_g=small["norm_f_g"])

    weights = dict(ada_w=ada_w, ada_b=ada_b, norm1_g=norm1_g, w_in=w_in, b_in=b_in, a_ln_g=a_ln_g, a_ln_b=a_ln_b,
                   a_spatial_w=a_spatial_w, a_spatial_b=a_spatial_b, b_conv_w=b_conv_w, b_conv_b=b_conv_b, b_gn_g=b_gn_g,
                   b_gn_b=b_gn_b, out_norm_a_g=out_norm_a_g, out_norm_b_g=out_norm_b_g, w_out=w_out, norm2_g=norm2_g,
                   w_ffn_in=w_ffn_in, w_ffn_out=w_ffn_out, ada_f_w=ada_f_w, ada_f_b=ada_f_b, norm_f_g=norm_f_g)
    m_in = dict(ada_w=m_ada_w, ada_b=m_ada_b, norm1_g=m_norm1_g, w_in=m_w_in, b_in=m_b_in, a_ln_g=m_a_ln_g, a_ln_b=m_a_ln_b,
                a_spatial_w=m_a_spatial_w, a_spatial_b=m_a_spatial_b, b_conv_w=m_b_conv_w, b_conv_b=m_b_conv_b,
                b_gn_g=m_b_gn_g, b_gn_b=m_b_gn_b, out_norm_a_g=m_out_norm_a_g, out_norm_b_g=m_out_norm_b_g, w_out=m_w_out,
                norm2_g=m_norm2_g, w_ffn_in=m_w_ffn_in, w_ffn_out=m_w_ffn_out, ada_f_w=m_ada_f_w, ada_f_b=m_ada_f_b,
                norm_f_g=m_norm_f_g)
    v_in = dict(ada_w=v_ada_w, ada_b=v_ada_b, norm1_g=v_norm1_g, w_in=v_w_in, b_in=v_b_in, a_ln_g=v_a_ln_g, a_ln_b=v_a_ln_b,
                a_spatial_w=v_a_spatial_w, a_spatial_b=v_a_spatial_b, b_conv_w=v_b_conv_w, b_conv_b=v_b_conv_b,
                b_gn_g=v_b_gn_g, b_gn_b=v_b_gn_b, out_norm_a_g=v_out_norm_a_g, out_norm_b_g=v_out_norm_b_g, w_out=v_w_out,
                norm2_g=v_norm2_g, w_ffn_in=v_w_ffn_in, w_ffn_out=v_w_ffn_out, ada_f_w=v_ada_f_w, ada_f_b=v_ada_f_b,
                norm_f_g=v_norm_f_g)
    names = list(weights)
    big = ("ada_w", "w_in", "w_out", "w_ffn_in", "w_ffn_out", "ada_f_w")

    def flat2(a):
        return a.reshape(-1, a.shape[-1])

    delta, new_m, new_v = {}, {}, {}
    for nm in big:
        shape = weights[nm].shape
        grads[nm] = grads[nm].reshape(shape)
        d_, m_, v_ = _adamw("adamw_" + nm, flat2(weights[nm]), flat2(grads[nm]), flat2(m_in[nm]), flat2(v_in[nm]))
        delta[nm], new_m[nm], new_v[nm] = d_.reshape(shape), m_.reshape(shape), v_.reshape(shape)

    small_names = [nm for nm in names if nm not in big]
    for nm in small_names:
        grads[nm] = grads[nm].reshape(weights[nm].shape)
    small_out = _adamw_many("adamw_small", *[[flat2(tree[nm]) for nm in small_names] for tree in (weights, grads, m_in, v_in)])
    for k, nm in enumerate(small_names):
        shape = weights[nm].shape
        delta[nm], new_m[nm], new_v[nm] = [o.reshape(shape) for o in small_out[3 * k:3 * k + 3]]

    grad_x = gx.reshape(x.shape)
    return (loss, grad_x, *[grads[nm] for nm in names], *[delta[nm] for nm in names],
            *[new_m[nm] for nm in names], *[new_v[nm] for nm in names])
```

```python
import functools
import math

import jax
import jax.numpy as jnp
from jax import lax
from jax.experimental import pallas as pl
from jax.experimental.pallas import tpu as pltpu

F32 = jnp.float32
MM_DTYPE = jnp.bfloat16
WIRE_DTYPE = jnp.bfloat16
SDS = jax.ShapeDtypeStruct
MESH = pl.DeviceIdType.MESH

D = 1024
DA = 512
DB = 512
NQ = 4
PW_IN = 512
DFF = 2816
PW_FF = 1408
CHUNK = 128
N_HEADS = 8
CONV_W = 31
HALO = 32
CONV_ROWS = 64
EPS = 1e-6
LANES = 128
HALF = 64

ROW_TILE = 256
FWD_ROW_TILE = 512
FFN_ROW_TILE = 256
GRAD_ROW_TILE = 2048
VMEM_LIMIT = 60 * 1024 * 1024

ADAM_LR = 0.001
ADAM_B1 = 0.9
ADAM_B2 = 0.999
ADAM_EPS = 1e-08
ADAM_WD = 0.01
ADAM_STEP = 10

PK_DMOD = 0
PK_G1 = 8
PK_G2 = 9
PK_GF = 10
PK_BIN = 11
PK_V512 = 13
PK_SB = 17
PK_CW = 18
PK_SW = 34
PK_LOSS = 162
PK_ROWS = 168


def _dot(a, b):
    return jnp.dot(a, b, preferred_element_type=F32)


def _dot_nt(a, b):
    return lax.dot_general(a, b, (((1,), (1,)), ((), ())), preferred_element_type=F32)


def _dot_tn(a, b):
    return lax.dot_general(a, b, (((0,), (0,)), ((), ())), preferred_element_type=F32)


def _rowsum(x):
    return jnp.sum(x, axis=-1, keepdims=True)


def _colsum(x):
    return jnp.sum(x, axis=0, keepdims=True)


def _group_sum(x):
    rows, width = x.shape
    lo_mask = lax.broadcasted_iota(jnp.int32, (rows, LANES), 1) < HALF
    outs = []
    for jb in range(width // LANES):
        xb = x[:, jb * LANES:(jb + 1) * LANES]
        lo = _rowsum(jnp.where(lo_mask, xb, 0.0))
        hi = _rowsum(jnp.where(lo_mask, 0.0, xb))
        outs.append(jnp.where(lo_mask, lo, hi))
    return jnp.concatenate(outs, axis=-1)


def _sigmoid(x):
    return 1.0 / (1.0 + jnp.exp(-x))


def _gelu_parts(u):
    cdf = 0.5 * (1.0 + lax.erf(u * (1.0 / math.sqrt(2.0))))
    pdf = jnp.exp(-0.5 * u * u) * (1.0 / math.sqrt(2.0 * math.pi))
    return u * cdf, cdf + u * pdf


def _whole_vmem():
    return pl.BlockSpec(memory_space=pltpu.VMEM)


def _params(*semantics):
    return pltpu.CompilerParams(dimension_semantics=semantics, vmem_limit_bytes=VMEM_LIMIT)


def _mod_rows(mod_ref, modb_ref, first, count):
    m = mod_ref[...] + modb_ref[...]
    return [m[:, (first + k) * D:(first + k + 1) * D] for k in range(count)]


def _mixer_recompute(z_parts, lng, lnb, wm_ref, bst_ref, mix_ref):
    u, v, val, gate = z_parts
    rows = u.shape[0]
    gu, dgu = _gelu_parts(u)
    gv, dgv = _gelu_parts(v)
    mu = _rowsum(gv) * (1.0 / DA)
    vc = gv - mu
    rs = lax.rsqrt(_rowsum(vc * vc) * (1.0 / DA) + EPS)
    vhat = vc * rs
    vl = vhat * lng + lnb
    vlb = vl.astype(MM_DTYPE)
    lo_mask = lax.broadcasted_iota(jnp.int32, (CHUNK, LANES), 1) < HALF
    for ck in range(rows // CHUNK):
        for jb in range(DA // LANES):
            blk = vlb[ck * CHUNK:(ck + 1) * CHUNK, jb * LANES:(jb + 1) * LANES]
            a = _dot(wm_ref[2 * jb], blk)
            b = _dot(wm_ref[2 * jb + 1], blk)
            mix_ref[ck * CHUNK:(ck + 1) * CHUNK, jb * LANES:(jb + 1) * LANES] = (
                jnp.where(lo_mask, a, b) + bst_ref[:, jb * LANES:(jb + 1) * LANES])
    mixed = mix_ref[...]
    sg = _sigmoid(gate)
    yb0 = val * sg
    return dict(gu=gu, dgu=dgu, dgv=dgv, rs=rs, vhat=vhat, vlb=vlb, mixed=mixed, sg=sg, yb0=yb0)


def _conv_branch_tail(yb1, gng, gnb):
    gm = _group_sum(yb1) * (1.0 / HALF)
    gc = yb1 - gm
    grs = lax.rsqrt(_group_sum(gc * gc) * (1.0 / HALF) + EPS)
    ghat = gc * grs
    yb2 = ghat * gng + gnb
    s2 = _sigmoid(yb2)
    return dict(grs=grs, ghat=ghat, yb2=yb2, s2=s2, y_b=yb2 * s2)


def _shifted_copies(e_ref, sh_ref):
    n = sh_ref.shape[1]
    for b in range(1, 8):
        sh_ref[b - 1] = e_ref[pl.ds(b, n), :]


def _window(e_ref, sh_ref, offset, r0, nrows, cols):
    a, b = divmod(offset, 8)
    if b == 0:
        return e_ref[pl.ds(r0 + 8 * a, nrows), cols]
    return sh_ref[b - 1, pl.ds(r0 + 8 * a, nrows), cols]


def _conv_taps(e_ref, sh_ref, cw_ref, out_ref, ts, first_offset, flip, bias_ref=None, other_ref=None, tap_acc_ref=None):
    groups = CONV_ROWS // 8
    for cb in range(DB // LANES):
        cols = slice(cb * LANES, (cb + 1) * LANES)
        tap_acc = [jnp.zeros((8, LANES), F32) for _ in range(CONV_W)]
        for rb in range(ts // CONV_ROWS):
            r0 = rb * CONV_ROWS
            acc = jnp.zeros((CONV_ROWS, LANES), F32)
            if bias_ref is not None:
                acc = acc + bias_ref[:, cols]
            if other_ref is not None:
                other = other_ref[r0:r0 + CONV_ROWS, cols]
            for j in range(CONV_W):
                k = CONV_W - 1 - j if flip else j
                win = _window(e_ref, sh_ref, first_offset + j, r0, CONV_ROWS, cols)
                acc = acc + win * cw_ref[k:k + 1, cols]
                if other_ref is not None:
                    tap_acc[k] = tap_acc[k] + jnp.sum((other * win).reshape(groups, 8, LANES), axis=0)
            out_ref[r0:r0 + CONV_ROWS, cols] = acc
        if other_ref is not None:
            for k in range(CONV_W):
                tap_acc_ref[k:k + 1, cols] += _colsum(tap_acc[k])


def _gather_plan(ins, outs, send_sems, recv_sems, local_sems=None):
    x, y, c = _place()
    sibling = (x, y, 1 - c)
    chips = _other_chips(x, y)
    myq = 2 * x + y

    def copy(w, k, q, hc, to, src=None):
        rows = ins[w].shape[0]
        dst = outs[w].at[q, pl.ds(hc * (rows // 2), rows // 2)]
        return pltpu.make_async_remote_copy(
            src_ref=dst if src is None else src, dst_ref=dst,
            send_sem=send_sems.at[w, k], recv_sem=recv_sems.at[w, k], device_id=to, device_id_type=MESH)

    def own(w):
        return pltpu.make_async_copy(ins[w], outs[w].at[myq], local_sems.at[w])

    def send():
        for w in range(len(ins)):
            rows = ins[w].shape[0]
            src = ins[w].at[pl.ds(c * (rows // 2), rows // 2)]
            for j, chip in enumerate(chips):
                copy(w, j, myq, c, (*chip, c), src=src).start()
            if local_sems is not None:
                own(w).start()

    def forward():
        for w in range(len(ins)):
            for j, (qx, qy) in enumerate(chips):
                copy(w, j, 2 * qx + qy, c, sibling).wait_recv()
                copy(w, 3 + j, 2 * qx + qy, c, sibling).start()

    def finish():
        for w in range(len(ins)):
            for j, (qx, qy) in enumerate(chips):
                copy(w, 3 + j, 2 * qx + qy, 1 - c, sibling).wait_recv()
        for w in range(len(ins)):
            for k, (qx, qy) in enumerate(chips + chips):
                copy(w, k, 2 * qx + qy, c, sibling).wait_send()
            if local_sems is not None:
                own(w).wait()

    return send, forward, finish


def _mixer_fwd(x, mod, mods, norm1_g, w_in4, b_in, ln_g, ln_b, wm, bst, conv_w, conv_b, gn_g, gn_b, ga, gb, w_out,
               ffn_shards):
    s = x.shape[0]
    ts = min(FWD_ROW_TILE, s)
    nt = s // ts
    n_sh = len(ffn_shards)

    def body(x_ref, mod_ref, modb_ref, g1_ref, w4_ref, bin_ref, lng_ref, lnb_ref, wm_ref, bst_ref, cw_ref, cb_ref,
             gng_ref, gnb_ref, ga_ref, gb_ref, wout_ref, *rest):
        shard_refs, rest = rest[:n_sh], rest[n_sh:]
        z_ref, x1_ref, yb1_ref, y_ref = rest[:4]
        full_refs, rest = rest[4:4 + n_sh], rest[4 + n_sh:]
        e_ref, sh_ref, mix_ref, send_sems, recv_sems, local_sems = rest
        i = pl.program_id(0)
        send, forward, finish = _gather_plan(shard_refs, full_refs, send_sems, recv_sems, local_sems)

        @pl.when(i == 0)
        def _():
            send()
            e_ref[0:HALO, :] = jnp.zeros((HALO, DB), F32)

        @pl.when(i == (3 * nt) // 4)
        def _():
            forward()

        shift1, scale1, gate1 = _mod_rows(mod_ref, modb_ref, 0, 3)
        x_t = x_ref[...]
        r1 = lax.rsqrt(_rowsum(x_t * x_t) * (1.0 / D) + EPS)
        h = (x_t * r1 * g1_ref[...]) * (1.0 + scale1) + shift1
        hb = h.astype(MM_DTYPE)
        z_parts = []
        for q in range(NQ):
            zq = _dot(hb, w4_ref[q]) + bin_ref[:, q * PW_IN:(q + 1) * PW_IN]
            z_ref[:, q * PW_IN:(q + 1) * PW_IN] = zq
            z_parts.append(zq)
        r = _mixer_recompute(z_parts, lng_ref[...], lnb_ref[...], wm_ref, bst_ref, mix_ref)
        y_a = r["gu"] * r["mixed"]
        e_ref[HALO:HALO + ts, :] = r["yb0"]
        _shifted_copies(e_ref, sh_ref)
        _conv_taps(e_ref, sh_ref, cw_ref, yb1_ref, ts, HALO - (CONV_W - 1), False, bias_ref=cb_ref)
        e_ref[0:HALO, :] = e_ref[ts:ts + HALO, :]
        t = _conv_branch_tail(yb1_ref[...], gng_ref[...], gnb_ref[...])
        ra = lax.rsqrt(_rowsum(y_a * y_a) * (1.0 / DA) + EPS)
        rb = lax.rsqrt(_rowsum(t["y_b"] * t["y_b"]) * (1.0 / DB) + EPS)
        yan = (y_a * ra * ga_ref[...]).astype(MM_DTYPE)
        ybn = (t["y_b"] * rb * gb_ref[...]).astype(MM_DTYPE)
        y_ref[:, 0:DA] = yan
        y_ref[:, DA:D] = ybn
        o1 = _dot(yan, wout_ref[0:DA, :]) + _dot(ybn, wout_ref[DA:D, :])
        x1_ref[...] = x_t + gate1 * o1

        @pl.when(i == nt - 1)
        def _():
            finish()

    row = lambda w: pl.BlockSpec((ts, w), lambda i: (i, 0))
    full = lambda a: pl.BlockSpec(a.shape, lambda i: (0,) * a.ndim)
    return pl.pallas_call(
        body, name="mixer_fwd", grid=(nt,),
        in_specs=[row(D), full(mod), full(mods), full(norm1_g), _whole_vmem(), full(b_in), full(ln_g), full(ln_b),
                  _whole_vmem(), full(bst), full(conv_w), full(conv_b), full(gn_g), full(gn_b), full(ga), full(gb),
                  _whole_vmem()] + [_any()] * n_sh,
        out_specs=[row(4 * PW_IN), row(D), row(DB), row(D)] + [_any()] * n_sh,
        out_shape=[SDS((s, 4 * PW_IN), F32), SDS((s, D), F32), SDS((s, DB), F32), SDS((s, D), MM_DTYPE)]
        + [SDS((NQ,) + w.shape, w.dtype) for w in ffn_shards],
        scratch_shapes=[pltpu.VMEM((ts + HALO, DB), F32), pltpu.VMEM((7, ts + HALO - 8, DB), F32), pltpu.VMEM((ts, DA), F32),
                        pltpu.SemaphoreType.DMA((n_sh, 6)), pltpu.SemaphoreType.DMA((n_sh, 6)),
                        pltpu.SemaphoreType.DMA((n_sh,))],
        compiler_params=_params("arbitrary"),
    )(x, mod, mods, norm1_g, w_in4, b_in, ln_g, ln_b, wm, bst, conv_w, conv_b, gn_g, gn_b, ga, gb, w_out, *ffn_shards)


def _ffn_fwd(x1, target, mod, mods, norm2_g, norm_f_g, w_ffn_in4, w_ffn_out):
    s = x1.shape[0]
    sub_rows = min(FFN_ROW_TILE, s)
    ts = min(2 * sub_rows, s)
    nt = s // ts

    def body(x1_ref, tgt_ref, mod_ref, modb_ref, g2_ref, gf_ref, wf_ref, wo_ref,
             g_ref, up_ref, h2_ref, dx2_ref, acc_ref):
        i = pl.program_id(0)

        @pl.when(i == 0)
        def _():
            acc_ref[...] = jnp.zeros(acc_ref.shape, F32)

        shift2, scale2, gate2, shift_f, scale_f = _mod_rows(mod_ref, modb_ref, 3, 5)
        for sub in range(ts // sub_rows):
            rows = slice(sub * sub_rows, (sub + 1) * sub_rows)
            x1_t = x1_ref[rows, :]
            r2 = lax.rsqrt(_rowsum(x1_t * x1_t) * (1.0 / D) + EPS)
            h2 = (x1_t * r2 * g2_ref[...]) * (1.0 + scale2) + shift2
            h2b = h2.astype(MM_DTYPE)
            h2_ref[rows, :] = h2b
            o2 = jnp.zeros((sub_rows, D), F32)
            for p in range(2):
                g = _dot(h2b, wf_ref[p])
                up = _dot(h2b, wf_ref[2 + p])
                g_ref[rows, p * PW_FF:(p + 1) * PW_FF] = g.astype(MM_DTYPE)
                up_ref[rows, p * PW_FF:(p + 1) * PW_FF] = up.astype(MM_DTYPE)
                a = (g * _sigmoid(g) * up).astype(MM_DTYPE)
                o2 = o2 + _dot(a, wo_ref[p * PW_FF:(p + 1) * PW_FF, :])
            x2 = x1_t + gate2 * o2
            rf = lax.rsqrt(_rowsum(x2 * x2) * (1.0 / D) + EPS)
            gf = gf_ref[...]
            nf = x2 * rf * gf
            err = nf * (1.0 + scale_f) + shift_f - tgt_ref[rows, :]
            d_out = err * (1.0 / D)
            d_nf = d_out * (1.0 + scale_f)
            t = d_nf * gf
            dx2_ref[rows, :] = rf * t - x2 * (rf * rf * rf) * (_rowsum(t * x2) * (1.0 / D))
            acc_ref[0:1, :] += _colsum(d_out)
            acc_ref[1:2, :] += _colsum(d_out * nf)
            acc_ref[2:3, :] += _colsum(d_nf * x2 * rf)
            acc_ref[3:4, :] += _colsum(err * err)

        @pl.when(i == nt - 1)
        def _():
            acc_ref[4:5, :] = jnp.zeros((1, D), F32) + _rowsum(acc_ref[3:4, :]) * (0.5 / D)

    row = lambda w: pl.BlockSpec((ts, w), lambda i: (i, 0))
    full = lambda a: pl.BlockSpec(a.shape, lambda i: (0,) * a.ndim)
    return pl.pallas_call(
        body, name="ffn_fwd", grid=(nt,),
        in_specs=[row(D), row(D), full(mod), full(mods), full(norm2_g), full(norm_f_g), _whole_vmem(), _whole_vmem()],
        out_specs=[row(DFF), row(DFF), row(D), row(D), pl.BlockSpec((8, D), lambda i: (0, 0))],
        out_shape=[SDS((s, DFF), MM_DTYPE), SDS((s, DFF), MM_DTYPE), SDS((s, D), MM_DTYPE), SDS((s, D), F32),
                   SDS((8, D), F32)],
        compiler_params=_params("arbitrary"),
    )(x1, target, mod, mods, norm2_g, norm_f_g, w_ffn_in4, w_ffn_out)


def _ffn_bwd(dx2, x1, g, up, mod, mods, norm2_g, w_ffn_in4, w_ffn_out):
    s = x1.shape[0]
    ts = min(FFN_ROW_TILE, s)
    nt = s // ts

    def body(dx2_ref, x1_ref, g_ref, up_ref, mod_ref, modb_ref, g2_ref, wf_ref, wo_ref,
             dff_ref, a_ref, dx1_ref, acc_ref):
        @pl.when(pl.program_id(0) == 0)
        def _():
            acc_ref[...] = jnp.zeros(acc_ref.shape, F32)

        shift2, scale2, gate2 = _mod_rows(mod_ref, modb_ref, 3, 3)
        dx2_t = dx2_ref[...]
        do2 = (dx2_t * gate2).astype(MM_DTYPE)
        dh2 = jnp.zeros((ts, D), F32)
        for p in range(2):
            da = _dot_nt(do2, wo_ref[p * PW_FF:(p + 1) * PW_FF, :])
            gp = g_ref[:, p * PW_FF:(p + 1) * PW_FF].astype(F32)
            upp = up_ref[:, p * PW_FF:(p + 1) * PW_FF].astype(F32)
            sg = _sigmoid(gp)
            silu = gp * sg
            a_ref[:, p * PW_FF:(p + 1) * PW_FF] = (silu * upp).astype(MM_DTYPE)
            dg = (da * upp * (sg * (1.0 + gp * (1.0 - sg)))).astype(MM_DTYPE)
            dup = (da * silu).astype(MM_DTYPE)
            dff_ref[:, p * PW_FF:(p + 1) * PW_FF] = dg
            dff_ref[:, DFF + p * PW_FF:DFF + (p + 1) * PW_FF] = dup
            dh2 = dh2 + _dot_nt(dg, wf_ref[p]) + _dot_nt(dup, wf_ref[2 + p])
        x1_t = x1_ref[...]
        r2 = lax.rsqrt(_rowsum(x1_t * x1_t) * (1.0 / D) + EPS)
        g2 = g2_ref[...]
        xr = x1_t * r2
        dn2 = dh2 * (1.0 + scale2)
        t = dn2 * g2
        dx1_ref[...] = dx2_t + r2 * t - x1_t * (r2 * r2 * r2) * (_rowsum(t * x1_t) * (1.0 / D))
        acc_ref[0:1, :] += _colsum(dh2)
        acc_ref[1:2, :] += _colsum(dh2 * (xr * g2))
        acc_ref[2:3, :] += _colsum(dn2 * xr)

    row = lambda w: pl.BlockSpec((ts, w), lambda i: (i, 0))
    full = lambda a: pl.BlockSpec(a.shape, lambda i: (0,) * a.ndim)
    return pl.pallas_call(
        body, name="ffn_bwd", grid=(nt,),
        in_specs=[row(D), row(D), row(DFF), row(DFF), full(mod), full(mods), full(norm2_g), _whole_vmem(), _whole_vmem()],
        out_specs=[row(2 * DFF), row(DFF), row(D), pl.BlockSpec((8, D), lambda i: (0, 0))],
        out_shape=[SDS((s, 2 * DFF), MM_DTYPE), SDS((s, DFF), MM_DTYPE), SDS((s, D), F32), SDS((8, D), F32)],
        compiler_params=_params("arbitrary"),
    )(dx2, x1, g, up, mod, mods, norm2_g, w_ffn_in4, w_ffn_out)


def _scatter_plan(ins, outs, send_sems, recv_sems, local_sems=None):
    x, y, c = _place()
    me = 4 * x + 2 * y + c

    def copies():
        cps = []
        for w in range(len(ins)):
            h = ins[w].shape[1] // 2
            for k in range(1, 8):
                px, py, pc = (1 - x if k & 4 else x), (1 - y if k & 2 else y), (1 - c if k & 1 else c)
                cps.append(pltpu.make_async_remote_copy(
                    src_ref=ins[w].at[2 * px + py, pl.ds(pc * h, h)], dst_ref=outs[w].at[me],
                    send_sem=send_sems.at[w, k - 1], recv_sem=recv_sems.at[w, k - 1],
                    device_id=(px, py, pc), device_id_type=MESH))
        return cps

    def own():
        if local_sems is None:
            return []
        return [pltpu.make_async_copy(ins[w].at[2 * x + y, pl.ds(c * (ins[w].shape[1] // 2), ins[w].shape[1] // 2)],
                                      outs[w].at[me], local_sems.at[w]) for w in range(len(ins))]

    def send():
        for cp in copies() + own():
            cp.start()

    def finish():
        for cp in copies() + own():
            cp.wait()

    return send, finish


def _scattered_shape(partial):
    nq, rows, cols = partial.shape
    return SDS((8, rows // 2, cols), partial.dtype)


def _mixer_bwd(dx1, x, z, yb1, mod, mods, norm1_g, w_in4, ln_g, ln_b, wm, wmt, bst, conv_w, gn_g, gn_b, ga, gb, w_out,
               partials):
    s = x.shape[0]
    ts = min(ROW_TILE, s)
    nt = s // ts
    n_cs = len(partials)

    def body(dx1_ref, x_ref, z_ref, yb1_ref, mod_ref, modb_ref, g1_ref, w4_ref, lng_ref, lnb_ref, wm_ref, wmt_ref,
             bst_ref, cw_ref, gng_ref, gnb_ref, ga_ref, gb_ref, wout_ref, *rest):
        cs_refs, rest = rest[:n_cs], rest[n_cs:]
        gx_ref, dz_ref, h_ref, a1_ref, a2_ref, a5_ref, acw_ref, asw_ref, asb_ref = rest[:9]
        arrived_refs, rest = rest[9:9 + n_cs], rest[9 + n_cs:]
        e_ref, sh_ref, mix_ref, dvl_ref, send_sems, recv_sems, local_sems = rest
        i = pl.program_id(0)
        send, finish = _scatter_plan(cs_refs, arrived_refs, send_sems, recv_sems, local_sems)

        @pl.when(i == 0)
        def _():
            send()
            e_ref[ts:ts + HALO, :] = jnp.zeros((HALO, DB), F32)
            for r in (a1_ref, a2_ref, a5_ref, acw_ref, asw_ref, asb_ref):
                r[...] = jnp.zeros(r.shape, F32)

        shift1, scale1, gate1 = _mod_rows(mod_ref, modb_ref, 0, 3)
        dx1_t = dx1_ref[...]
        do1 = (dx1_t * gate1).astype(MM_DTYPE)
        d_yan = _dot_nt(do1, wout_ref[0:DA, :])
        d_ybn = _dot_nt(do1, wout_ref[DA:D, :])

        z_parts = [z_ref[:, q * PW_IN:(q + 1) * PW_IN] for q in range(NQ)]
        u, v, val, gate = z_parts
        lng = lng_ref[...]
        r = _mixer_recompute(z_parts, lng, lnb_ref[...], wm_ref, bst_ref, mix_ref)
        gng = gng_ref[...]
        t = _conv_branch_tail(yb1_ref[...], gng, gnb_ref[...])
        y_a = r["gu"] * r["mixed"]
        y_b = t["y_b"]
        ga_v, gb_v = ga_ref[...], gb_ref[...]
        ra = lax.rsqrt(_rowsum(y_a * y_a) * (1.0 / DA) + EPS)
        rb = lax.rsqrt(_rowsum(y_b * y_b) * (1.0 / DB) + EPS)

        a5_ref[0:1, :] += _colsum(d_yan * y_a * ra)
        a5_ref[1:2, :] += _colsum(d_ybn * y_b * rb)
        ta = d_yan * ga_v
        d_ya = ra * ta - y_a * (ra * ra * ra) * (_rowsum(ta * y_a) * (1.0 / DA))
        tb = d_ybn * gb_v
        d_yb = rb * tb - y_b * (rb * rb * rb) * (_rowsum(tb * y_b) * (1.0 / DB))

        d_u = d_ya * r["mixed"] * r["dgu"]
        d_mixed = d_ya * r["gu"]
        dmb = d_mixed.astype(MM_DTYPE)
        lo_mask = lax.broadcasted_iota(jnp.int32, (CHUNK, LANES), 1) < HALF
        zero_blk = jnp.zeros((CHUNK, LANES), MM_DTYPE)
        sb_acc = jnp.zeros((CHUNK, DA), F32)
        for ck in range(ts // CHUNK):
            rows = slice(ck * CHUNK, (ck + 1) * CHUNK)
            sb_acc = sb_acc + d_mixed[rows, :]
            for jb in range(DA // LANES):
                cols = slice(jb * LANES, (jb + 1) * LANES)
                dm_blk = dmb[rows, cols]
                vl_blk = r["vlb"][rows, cols]
                da_ = _dot(wmt_ref[2 * jb], dm_blk)
                db_ = _dot(wmt_ref[2 * jb + 1], dm_blk)
                dvl_ref[rows, cols] = jnp.where(lo_mask, da_, db_)
                asw_ref[2 * jb] += _dot_nt(jnp.where(lo_mask, dm_blk, zero_blk), vl_blk)
                asw_ref[2 * jb + 1] += _dot_nt(jnp.where(lo_mask, zero_blk, dm_blk), vl_blk)
        asb_ref[...] += sb_acc
        d_vl = dvl_ref[...]
        a5_ref[2:3, :] += _colsum(d_vl * r["vhat"])
        a5_ref[3:4, :] += _colsum(d_vl)
        dvh = d_vl * lng
        d_gv = r["rs"] * (dvh - _rowsum(dvh) * (1.0 / DA) - r["vhat"] * (_rowsum(dvh * r["vhat"]) * (1.0 / DA)))
        d_v = d_gv * r["dgv"]

        yb2, s2 = t["yb2"], t["s2"]
        d_yb2 = d_yb * (s2 * (1.0 + yb2 * (1.0 - s2)))
        a5_ref[4:5, :] += _colsum(d_yb2 * t["ghat"])
        a5_ref[5:6, :] += _colsum(d_yb2)
        dgh = d_yb2 * gng
        d_yb1 = t["grs"] * (dgh - _group_sum(dgh) * (1.0 / HALF) - t["ghat"] * (_group_sum(dgh * t["ghat"]) * (1.0 / HALF)))
        a5_ref[6:7, :] += _colsum(d_yb1)
        e_ref[0:ts, :] = d_yb1
        _shifted_copies(e_ref, sh_ref)
        mix_ref[...] = r["yb0"]
        _conv_taps(e_ref, sh_ref, cw_ref, dvl_ref, ts, 0, True, other_ref=mix_ref, tap_acc_ref=acw_ref)
        d_yb0 = dvl_ref[...]
        e_ref[ts:ts + HALO, :] = e_ref[0:HALO, :]
        sg = r["sg"]
        d_val = d_yb0 * sg
        d_gate = d_yb0 * val * sg * (1.0 - sg)

        dh = jnp.zeros((ts, D), F32)
        for q, dzq in enumerate((d_u, d_v, d_val, d_gate)):
            a2_ref[0:1, q * PW_IN:(q + 1) * PW_IN] += _colsum(dzq)
            dzb = dzq.astype(MM_DTYPE)
            dz_ref[:, q * PW_IN:(q + 1) * PW_IN] = dzb
            dh = dh + _dot_nt(dzb, w4_ref[q])
        x_t = x_ref[...]
        r1 = lax.rsqrt(_rowsum(x_t * x_t) * (1.0 / D) + EPS)
        g1 = g1_ref[...]
        xr = x_t * r1
        n1 = xr * g1
        h_ref[...] = (n1 * (1.0 + scale1) + shift1).astype(MM_DTYPE)
        dn1 = dh * (1.0 + scale1)
        t1 = dn1 * g1
        gx_ref[...] = dx1_t + r1 * t1 - x_t * (r1 * r1 * r1) * (_rowsum(t1 * x_t) * (1.0 / D))
        a1_ref[0:1, :] += _colsum(dh)
        a1_ref[1:2, :] += _colsum(dh * n1)
        a1_ref[2:3, :] += _colsum(dn1 * xr)

        @pl.when(i == nt - 1)
        def _():
            asb_ref[...] = _group_sum(asb_ref[...])
            finish()

    row = lambda w: pl.BlockSpec((ts, w), lambda i: (nt - 1 - i, 0))
    full = lambda a: pl.BlockSpec(a.shape, lambda i: (0,) * a.ndim)
    keep = lambda shape: pl.BlockSpec(shape, lambda i: (0,) * len(shape))
    return pl.pallas_call(
        body, name="mixer_bwd", grid=(nt,),
        in_specs=[row(D), row(D), row(4 * PW_IN), row(DB), full(mod), full(mods), full(norm1_g), _whole_vmem(),
                  full(ln_g), full(ln_b), _whole_vmem(), _whole_vmem(), full(bst), full(conv_w), full(gn_g), full(gn_b),
                  full(ga), full(gb), _whole_vmem()] + [_any()] * n_cs,
        out_specs=[row(D), row(4 * PW_IN), row(D), keep((8, D)), keep((8, 4 * PW_IN)), keep((8, DA)),
                   keep((HALO, DB)), keep((N_HEADS, CHUNK, CHUNK)), keep((CHUNK, DA))] + [_any()] * n_cs,
        out_shape=[SDS((s, D), F32), SDS((s, 4 * PW_IN), MM_DTYPE), SDS((s, D), MM_DTYPE), SDS((8, D), F32),
                   SDS((8, 4 * PW_IN), F32), SDS((8, DA), F32), SDS((HALO, DB), F32),
                   SDS((N_HEADS, CHUNK, CHUNK), F32), SDS((CHUNK, DA), F32)]
        + [_scattered_shape(p) for p in partials],
        scratch_shapes=[pltpu.VMEM((ts + HALO, DB), F32), pltpu.VMEM((7, ts + HALO - 8, DB), F32),
                        pltpu.VMEM((ts, DA), F32), pltpu.VMEM((ts, DA), F32),
                        pltpu.SemaphoreType.DMA((n_cs, 7)), pltpu.SemaphoreType.DMA((n_cs, 7)),
                        pltpu.SemaphoreType.DMA((n_cs,))],
        compiler_params=_params("arbitrary"),
    )(dx1, x, z, yb1, mod, mods, norm1_g, w_in4, ln_g, ln_b, wm, wmt, bst, conv_w, gn_g, gn_b, ga, gb, w_out, *partials)


def _gather8_plan(x_ref, out_ref, send_sems, recv_sems):
    x, y, c = _place()
    me, sibling = (x, y, c), (x, y, 1 - c)
    chips = _other_chips(x, y)

    def copy(k, block, to, src=None):
        dst = out_ref.at[4 * block[0] + 2 * block[1] + block[2]]
        return pltpu.make_async_remote_copy(src_ref=dst if src is None else src, dst_ref=dst, send_sem=send_sems.at[k],
                                            recv_sem=recv_sems.at[k], device_id=to, device_id_type=MESH)

    def send():
        copy(0, me, sibling, src=x_ref).start()
        for j, chip in enumerate(chips):
            copy(1 + j, me, (*chip, c), src=x_ref).start()

    def forward():
        for j, chip in enumerate(chips):
            copy(1 + j, (*chip, c), me).wait_recv()
            copy(4 + j, (*chip, c), sibling).start()

    def finish():
        copy(0, sibling, me).wait_recv()
        for j, chip in enumerate(chips):
            copy(4 + j, (*chip, 1 - c), me).wait_recv()
        for k in range(7):
            copy(k, me, sibling).wait_send()

    return send, forward, finish


def _grad_matmul(name, a, b, ka_tile, nb_tile, piece_w=None, gated=None, gather_blk=None):
    s, ka = a.shape
    nb = b.shape[1]
    ts = min(GRAD_ROW_TILE, s)
    nt = s // ts
    nja, njb = ka // ka_tile, nb // nb_tile
    steps = nja * njb * nt
    n_in = 2 + (2 if gated else 0) + (1 if gather_blk is not None else 0)
    n_out = 1 + (1 if gated else 0) + (1 if gather_blk is not None else 0)
    assert not (gated and njb != 1) and not (piece_w and nja != 1)

    def body(*refs):
        ins, outs, scratch = refs[:n_in], refs[n_in:n_in + n_out], refs[n_in + n_out:]
        a_ref, b_ref, o_ref, acc_ref = ins[0], ins[1], outs[0], scratch[0]
        step = (pl.program_id(0) * njb + pl.program_id(1)) * nt + pl.program_id(2)
        if gather_blk is not None:
            send, forward, finish = _gather8_plan(ins[-1], outs[-1], scratch[1], scratch[2])

            @pl.when(step == 0)
            def _():
                send()

            @pl.when(step == (3 * steps) // 4)
            def _():
                forward()

        prod = _dot_tn(a_ref[...].astype(MM_DTYPE), b_ref[...].astype(MM_DTYPE))

        @pl.when(pl.program_id(2) == 0)
        def _():
            acc_ref[...] = prod

        @pl.when(pl.program_id(2) > 0)
        def _():
            acc_ref[...] += prod

        @pl.when(pl.program_id(2) == nt - 1)
        def _():
            gm = acc_ref[...]
            if gated:
                gate_ref, w_ref, dg_ref = ins[2], ins[3], outs[1]

                @pl.when(step == nt - 1)
                def _():
                    dg_ref[...] = jnp.zeros(dg_ref.shape, F32)

                dg_ref[0:1, :] += _colsum(gm * w_ref[...].astype(F32))
                gm = gm * gate_ref[...]
            if piece_w:
                for q in range(nb_tile // piece_w):
                    o_ref[q] = gm[:, q * piece_w:(q + 1) * piece_w].astype(WIRE_DTYPE)
            else:
                o_ref[...] = gm.astype(WIRE_DTYPE)

        if gather_blk is not None:
            @pl.when(step == steps - 1)
            def _():
                finish()

    in_specs = [pl.BlockSpec((ts, ka_tile), lambda ja, jb, i: (i, ja)),
                pl.BlockSpec((ts, nb_tile), lambda ja, jb, i: (i, jb))]
    operands = [a, b]
    if piece_w:
        out_shape = [SDS((nb // piece_w, ka, piece_w), WIRE_DTYPE)]
        out_specs = [pl.BlockSpec((nb_tile // piece_w, ka, piece_w), lambda ja, jb, i: (jb, 0, 0))]
    else:
        out_shape = [SDS((ka, nb), WIRE_DTYPE)]
        out_specs = [pl.BlockSpec((ka_tile, nb_tile), lambda ja, jb, i: (ja, jb))]
    scratch = [pltpu.VMEM((ka_tile, nb_tile), F32)]
    if gated:
        in_specs += [pl.BlockSpec((1, nb_tile), lambda ja, jb, i: (0, jb)),
                     pl.BlockSpec((ka_tile, nb_tile), lambda ja, jb, i: (ja, jb))]
        operands += list(gated)
        out_shape.append(SDS((8, nb), F32))
        out_specs.append(pl.BlockSpec((8, nb_tile), lambda ja, jb, i: (0, jb)))
    if gather_blk is not None:
        in_specs.append(_any())
        operands.append(gather_blk)
        out_shape.append(SDS((8,) + gather_blk.shape, gather_blk.dtype))
        out_specs.append(_any())
        scratch += [pltpu.SemaphoreType.DMA((7,)), pltpu.SemaphoreType.DMA((7,))]
    return pl.pallas_call(
        body, name=name, grid=(nja, njb, nt), in_specs=in_specs, out_specs=out_specs, out_shape=out_shape,
        scratch_shapes=scratch, compiler_params=_params("arbitrary", "arbitrary", "arbitrary"),
    )(*operands)


def _cond_partial(c_all, w_cat):
    n = w_cat.shape[1]
    nt_cols = 512

    def body(c_ref, w_ref, o_ref):
        c_t = c_ref[...]
        ca = (c_t * _sigmoid(c_t)).astype(MM_DTYPE)
        o_ref[...] = _dot(ca, w_ref[...].astype(MM_DTYPE))

    return pl.pallas_call(
        body, name="cond_partial", grid=(n // nt_cols,),
        in_specs=[pl.BlockSpec((8, D), lambda j: (0, 0)), pl.BlockSpec((D, nt_cols), lambda j: (0, j))],
        out_specs=pl.BlockSpec((8, nt_cols), lambda j: (0, j)), out_shape=SDS((8, n), F32),
        compiler_params=_params("parallel"),
    )(c_all, w_cat)


def _cond_grad(c_all, dmod_cols):
    n = dmod_cols.shape[1]
    nt_cols = 512

    def body(c_ref, d_ref, o_ref):
        c_t = c_ref[...]
        ca = jnp.concatenate([c_t * _sigmoid(c_t), jnp.zeros((8, D), F32)], axis=0).astype(MM_DTYPE)
        dm = jnp.concatenate([d_ref[...], jnp.zeros((8, nt_cols), F32)], axis=0).astype(MM_DTYPE)
        o_ref[...] = _dot_tn(ca, dm)

    return pl.pallas_call(
        body, name="cond_grad", grid=(n // nt_cols,),
        in_specs=[pl.BlockSpec((8, D), lambda j: (0, 0)), pl.BlockSpec((8, nt_cols), lambda j: (0, j))],
        out_specs=pl.BlockSpec((D, nt_cols), lambda j: (0, j)), out_shape=SDS((D, n), F32),
        compiler_params=_params("parallel"),
    )(c_all, dmod_cols)


def _row_tile(rows, cap=256):
    if rows <= cap:
        return rows
    for t in range(cap, 7, -8):
        if rows % t == 0:
            return t
    return rows


def _ordered_sum(name, parts, out_dtype=F32):
    n, rows, cols = parts.shape
    rt = _row_tile(rows)

    def body(p_ref, o_ref):
        acc = p_ref[0].astype(F32)
        for k in range(1, n):
            acc = acc + p_ref[k].astype(F32)
        o_ref[...] = acc.astype(out_dtype)

    return pl.pallas_call(
        body, name=name, grid=(rows // rt,),
        in_specs=[pl.BlockSpec((n, rt, cols), lambda i: (0, i, 0))],
        out_specs=pl.BlockSpec((rt, cols), lambda i: (i, 0)), out_shape=SDS((rows, cols), out_dtype),
        compiler_params=_params("parallel"),
    )(parts)


def _adamw_update(w_ref, g_ref, m_ref, v_ref, d_ref, nm_ref, nv_ref):
    c1 = 1.0 - ADAM_B1 ** ADAM_STEP
    c2 = 1.0 - ADAM_B2 ** ADAM_STEP
    g_t = g_ref[...]
    m_new = ADAM_B1 * m_ref[...] + (1.0 - ADAM_B1) * g_t
    v_new = ADAM_B2 * v_ref[...] + (1.0 - ADAM_B2) * (g_t * g_t)
    nm_ref[...] = m_new
    nv_ref[...] = v_new
    d_ref[...] = -ADAM_LR * ((m_new / c1) / (jnp.sqrt(v_new / c2) + ADAM_EPS) + ADAM_WD * w_ref[...])


def _adamw_many(name, ws, gs, ms, vs):
    n = len(ws)

    def body(*refs):
        ins, outs = refs[:4 * n], refs[4 * n:]
        for k in range(n):
            _adamw_update(ins[k], ins[n + k], ins[2 * n + k], ins[3 * n + k], *outs[3 * k:3 * k + 3])

    return pl.pallas_call(
        body, name=name, out_shape=[SDS(w.shape, F32) for w in ws for _ in range(3)],
        compiler_params=pltpu.CompilerParams(vmem_limit_bytes=VMEM_LIMIT),
    )(*ws, *gs, *ms, *vs)


def _adamw(name, w, g, m, v):
    rows, cols = w.shape
    rt = _row_tile(rows)

    def body(*refs):
        _adamw_update(*refs)

    spec = pl.BlockSpec((rt, cols), lambda i: (i, 0))
    out = SDS((rows, cols), F32)
    return pl.pallas_call(body, name=name, grid=(rows // rt,), in_specs=[spec] * 4, out_specs=[spec] * 3,
                          out_shape=[out, out, out], compiler_params=_params("parallel"))(w, g, m, v)


def _place():
    return lax.axis_index("x"), lax.axis_index("y"), lax.axis_index("c")


def _other_chips(x, y):
    return [(1 - x, y), (x, 1 - y), (1 - x, 1 - y)]


def _all_gather8(name, blk):
    m, n = blk.shape

    def body(x_ref, out_ref, send_sems, recv_sems, local_sem):
        x, y, c = _place()
        me, sibling = (x, y, c), (x, y, 1 - c)
        chips = _other_chips(x, y)

        def slot(px, py, pc):
            return out_ref.at[4 * px + 2 * py + pc]

        def copy(k, block, to, src=None):
            return pltpu.make_async_remote_copy(
                src_ref=slot(*block) if src is None else src, dst_ref=slot(*block),
                send_sem=send_sems.at[k], recv_sem=recv_sems.at[k], device_id=to, device_id_type=MESH)

        mine = pltpu.make_async_copy(x_ref, slot(*me), local_sem)
        mine.start()
        first = [copy(0, me, sibling, src=x_ref)]
        first += [copy(1 + j, me, (*chip, c), src=x_ref) for j, chip in enumerate(chips)]
        for cp in first:
            cp.start()
        passed = [copy(4 + j, (*chip, c), sibling) for j, chip in enumerate(chips)]
        for j, chip in enumerate(chips):
            copy(1 + j, (*chip, c), me).wait_recv()
            passed[j].start()
        copy(0, sibling, me).wait_recv()
        for j, chip in enumerate(chips):
            copy(4 + j, (*chip, 1 - c), me).wait_recv()
        for cp in first + passed:
            cp.wait_send()
        mine.wait()

    return pl.pallas_call(
        body, name=name, out_shape=SDS((8, m, n), blk.dtype),
        in_specs=[_whole_vmem()], out_specs=_whole_vmem(),
        scratch_shapes=[pltpu.SemaphoreType.DMA((7,)), pltpu.SemaphoreType.DMA((7,)), pltpu.SemaphoreType.DMA],
        compiler_params=pltpu.CompilerParams(vmem_limit_bytes=VMEM_LIMIT),
    )(blk)


def _any():
    return pl.BlockSpec(memory_space=pl.ANY)


def _gather_weights(shards):
    n = len(shards)

    def body(*refs):
        send, forward, finish = _gather_plan(refs[:n], refs[n:2 * n], *refs[2 * n:])
        send()
        forward()
        finish()

    return pl.pallas_call(
        body, name="gather_weights",
        out_shape=[SDS((NQ,) + s.shape, s.dtype) for s in shards],
        in_specs=[_any()] * n, out_specs=[_any()] * n,
        scratch_shapes=[pltpu.SemaphoreType.DMA((n, 6)), pltpu.SemaphoreType.DMA((n, 6))],
    )(*shards)


def _own_piece(gathered, shard):
    myq = 2 * lax.axis_index("x") + lax.axis_index("y")
    return lax.dynamic_update_slice(gathered, shard[None], (myq,) + (0,) * shard.ndim)


def _scatter_to_owners(partials):
    n = len(partials)

    def body(*refs):
        send, finish = _scatter_plan(refs[:n], refs[n:2 * n], *refs[2 * n:])
        send()
        finish()

    return pl.pallas_call(
        body, name="scatter_to_owners",
        out_shape=[_scattered_shape(p) for p in partials],
        in_specs=[_any()] * n, out_specs=[_any()] * n,
        scratch_shapes=[pltpu.SemaphoreType.DMA((n, 7)), pltpu.SemaphoreType.DMA((n, 7))],
    )(*partials)


def _owner_sums(tag, arrived, partials=None):
    x, y, c = _place()
    sums = []
    for w, arr in enumerate(arrived):
        if partials is not None:
            part = partials[w]
            h = part.shape[1] // 2
            own = lax.dynamic_slice(part, (2 * x + y, c * h, 0), (1, h, part.shape[2]))
            arr = lax.dynamic_update_slice(arr, own, (4 * x + 2 * y + c, 0, 0))
        sums.append(_ordered_sum(f"owner_sum_{tag}_{w}", arr))
    return sums


def _join_halves(halves):
    n = len(halves)

    def body(*refs):
        ins, outs = refs[:n], refs[n:2 * n]
        send_sems, recv_sems = refs[2 * n:]
        x, y, c = _place()
        sibling = (x, y, 1 - c)
        cps = []
        for w in range(n):
            h = ins[w].shape[0]
            cp = pltpu.make_async_remote_copy(src_ref=ins[w], dst_ref=outs[w].at[pl.ds(c * h, h)], send_sem=send_sems.at[w],
                                              recv_sem=recv_sems.at[w], device_id=sibling, device_id_type=MESH)
            cp.start()
            cps.append(cp)
        for cp in cps:
            cp.wait()

    joined = pl.pallas_call(
        body, name="join_halves",
        out_shape=[SDS((2 * h.shape[0], h.shape[1]), h.dtype) for h in halves],
        in_specs=[_any()] * n, out_specs=[_any()] * n,
        scratch_shapes=[pltpu.SemaphoreType.DMA((n,)), pltpu.SemaphoreType.DMA((n,))],
    )(*halves)
    c = lax.axis_index("c")
    return [lax.dynamic_update_slice(j, h, (c * h.shape[0], 0)) for j, h in zip(joined, halves)]


def _pad_rows(a, rows):
    return jnp.pad(a, ((0, rows - a.shape[0]),) + ((0, 0),) * (a.ndim - 1))


def _pack_small(dmod, g1, g2, gf, b_in, ln_g, ln_b, conv_b, gn_g, gn_b, ga, gb, sb, cw32, sw, loss_row):
    v512 = jnp.concatenate([ln_g, ln_b, conv_b, gn_g, gn_b, ga, gb, jnp.zeros((1, DA), F32)], axis=1).reshape(4, D)
    rows = [dmod.reshape(8, D), g1, g2, gf, b_in.reshape(2, D), v512, sb.reshape(1, D), cw32.reshape(16, D),
            sw.reshape(CHUNK, D), loss_row]
    packed = jnp.concatenate(rows, axis=0)
    return _pad_rows(packed, PK_ROWS)


def _unpack_small(p):
    v512 = p[PK_V512:PK_V512 + 4].reshape(1, 8 * DA)
    pieces = [v512[:, k * DA:(k + 1) * DA] for k in range(7)]
    return dict(
        dmod=p[PK_DMOD:PK_DMOD + 8].reshape(1, 8 * D), norm1_g=p[PK_G1:PK_G1 + 1], norm2_g=p[PK_G2:PK_G2 + 1],
        norm_f_g=p[PK_GF:PK_GF + 1], b_in=p[PK_BIN:PK_BIN + 2].reshape(1, 2 * D),
        a_ln_g=pieces[0], a_ln_b=pieces[1], b_conv_b=pieces[2], b_gn_g=pieces[3], b_gn_b=pieces[4],
        out_norm_a_g=pieces[5], out_norm_b_g=pieces[6],
        a_spatial_b=p[PK_SB:PK_SB + 1].reshape(N_HEADS, CHUNK),
        b_conv_w=p[PK_CW:PK_CW + 16].reshape(HALO, DB),
        a_spatial_w=p[PK_SW:PK_SW + CHUNK].reshape(N_HEADS, CHUNK, CHUNK))


def kernel(x, c, ada_w, ada_b, norm1_g, w_in, b_in, a_ln_g, a_ln_b, a_spatial_w, a_spatial_b, b_conv_w, b_conv_b, b_gn_g, b_gn_b, out_norm_a_g, out_norm_b_g, w_out, norm2_g, w_ffn_in, w_ffn_out, ada_f_w, ada_f_b, norm_f_g, loss_target, m_ada_w, m_ada_b, m_norm1_g, m_w_in, m_b_in, m_a_ln_g, m_a_ln_b, m_a_spatial_w, m_a_spatial_b, m_b_conv_w, m_b_conv_b, m_b_gn_g, m_b_gn_b, m_out_norm_a_g, m_out_norm_b_g, m_w_out, m_norm2_g, m_w_ffn_in, m_w_ffn_out, m_ada_f_w, m_ada_f_b, m_norm_f_g, v_ada_w, v_ada_b, v_norm1_g, v_w_in, v_b_in, v_a_ln_g, v_a_ln_b, v_a_spatial_w, v_a_spatial_b, v_b_conv_w, v_b_conv_b, v_b_gn_g, v_b_gn_b, v_out_norm_a_g, v_out_norm_b_g, v_w_out, v_norm2_g, v_w_ffn_in, v_w_ffn_out, v_ada_f_w, v_ada_f_b, v_norm_f_g):
    mx, my, mc = _place()
    me = 4 * mx + 2 * my + mc
    myq = 2 * mx + my
    xs = x[0]
    target = loss_target[0]
    s = xs.shape[0]
    ada_w_cat = jnp.concatenate([ada_w[0], ada_f_w], axis=1)
    n_ada = ada_w.shape[2]

    cw_shard = _pad_rows(b_conv_w[0], HALO)
    first = _all_gather8("gather_c", jnp.concatenate([c.reshape(8, LANES), cw_shard], axis=0))
    c_all = first[:, 0:8, :].reshape(8, D)
    conv_w = jnp.concatenate([first[4 * (q // 2) + 2 * (q % 2), 8:8 + HALO, :] for q in range(NQ)], axis=1)
    cond_part = _cond_partial(c_all, ada_w_cat)
    cond_all = _all_gather8("gather_cond", cond_part)
    cond_q = [cond_all[4 * (q // 2) + 2 * (q % 2)] for q in range(NQ)]
    mod_all = jnp.concatenate([cq[:, :n_ada] for cq in cond_q] + [cq[:, n_ada:] for cq in cond_q], axis=1)
    mod = lax.dynamic_slice_in_dim(mod_all, me, 1, axis=0)
    mods = jnp.concatenate([ada_b, ada_f_b.reshape(1, 2 * D)], axis=1)

    mix_shards = [w_in[0].astype(MM_DTYPE), w_out[0].astype(MM_DTYPE)]
    ffn_shards = [w_ffn_in[0].astype(MM_DTYPE), w_ffn_out[0].astype(MM_DTYPE)]
    w_in4, w_out4 = [_own_piece(g_, s_) for g_, s_ in zip(_gather_weights(mix_shards), mix_shards)]
    w_out_f = w_out4.reshape(D, D)

    causal = jnp.tril(jnp.ones((CHUNK, CHUNK), dtype=bool))
    wm_f = jnp.where(causal[None], a_spatial_w[0], 0.0)
    wm = wm_f.astype(MM_DTYPE)
    wmt = jnp.swapaxes(wm_f, 1, 2).astype(MM_DTYPE)
    bst = jnp.repeat(a_spatial_b[0].T, HALF, axis=1)

    z, x1, yb1, y, w_ffn_in4, w_ffn_out4 = _mixer_fwd(
        xs, mod, mods, norm1_g, w_in4, b_in, a_ln_g, a_ln_b, wm, bst, conv_w, b_conv_b, b_gn_g, b_gn_b,
        out_norm_a_g, out_norm_b_g, w_out_f, ffn_shards)
    w_ffn_out_f = w_ffn_out4.reshape(DFF, D)
    g, up, h2, dx2, acc_f = _ffn_fwd(x1, target, mod, mods, norm2_g, norm_f_g, w_ffn_in4, w_ffn_out_f)

    dff, a_act, dx1, acc_2 = _ffn_bwd(dx2, x1, g, up, mod, mods, norm2_g, w_ffn_in4, w_ffn_out_f)
    (gw_ffn_in4,) = _grad_matmul("grad_w_ffn_in", h2, dff, D, PW_FF, piece_w=PW_FF)
    modv = mod + mods
    gw_ffn_out, dgate2 = _grad_matmul("grad_w_ffn_out", a_act, dx2, PW_FF, D, gated=(modv[:, 5 * D:6 * D], w_ffn_out_f))
    gw_out, dgate1 = _grad_matmul("grad_w_out", y, dx1, D, D, gated=(modv[:, 2 * D:3 * D], w_out_f))
    early_partials = [gw_ffn_in4, gw_ffn_out.reshape(NQ, DFF // NQ, D), gw_out.reshape(NQ, D // NQ, D)]
    gx, dz, h, acc_1, acc_bin, acc_5, acc_cw, acc_sw, acc_sb, *early_arrived = _mixer_bwd(
        dx1, xs, z, yb1, mod, mods, norm1_g, w_in4, a_ln_g, a_ln_b, wm, wmt, bst, conv_w, b_gn_g, b_gn_b,
        out_norm_a_g, out_norm_b_g, w_out_f, early_partials)

    dmod = jnp.concatenate([acc_1[0:1], acc_1[1:2], dgate1[0:1], acc_2[0:1], acc_2[1:2], dgate2[0:1],
                            acc_f[0:1], acc_f[1:2]], axis=1)
    sw_grad = jnp.where(causal[None], acc_sw, 0.0)
    sb_grad = acc_sb[:, ::HALF].T
    packed = _pack_small(dmod, acc_1[2:3], acc_2[2:3], acc_f[2:3], acc_bin[0:1], acc_5[2:3], acc_5[3:4], acc_5[6:7],
                         acc_5[4:5], acc_5[5:6], acc_5[0:1], acc_5[1:2], sb_grad, acc_cw, sw_grad, acc_f[4:5])
    gw_in4, gathered = _grad_matmul("grad_w_in", h, dz, D, NQ * PW_IN, piece_w=PW_IN, gather_blk=packed)
    gathered = lax.dynamic_update_slice(gathered, packed[None], (me, 0, 0))
    late_arrived = _scatter_to_owners([gw_in4])
    g_w_in, g_w_ffn_in, g_w_ffn_out, g_w_out = _join_halves(
        _owner_sums("late", late_arrived, [gw_in4]) + _owner_sums("early", early_arrived))
    summed = _ordered_sum("small_grad_sum", gathered)
    loss = summed[PK_LOSS, 0]
    small = _unpack_small(summed)
    dmod_all = gathered[:, PK_DMOD:PK_DMOD + 8, :].reshape(8, 8 * D)
    dmod_cols = jnp.concatenate([lax.dynamic_slice_in_dim(dmod_all, myq * n_ada, n_ada, axis=1),
                                 lax.dynamic_slice_in_dim(dmod_all, 6 * D + myq * PW_IN, PW_IN, axis=1)], axis=1)
    g_ada_cat = _cond_grad(c_all, dmod_cols)

    grads = dict(
        ada_w=g_ada_cat[:, :n_ada], ada_b=small["dmod"][:, :6 * D], norm1_g=small["norm1_g"], w_in=g_w_in,
        b_in=small["b_in"], a_ln_g=small["a_ln_g"], a_ln_b=small["a_ln_b"], a_spatial_w=small["a_spatial_w"],
        a_spatial_b=small["a_spatial_b"],
        b_conv_w=lax.dynamic_slice_in_dim(small["b_conv_w"], myq * LANES, LANES, axis=1)[:CONV_W],
        b_conv_b=small["b_conv_b"], b_gn_g=small["b_gn_g"], b_gn_b=small["b_gn_b"],
        out_norm_a_g=small["out_norm_a_g"], out_norm_b_g=small["out_norm_b_g"], w_out=g_w_out,
        norm2_g=small["norm2_g"], w_ffn_in=g_w_ffn_in, w_ffn_out=g_w_ffn_out, ada_f_w=g_ada_cat[:, n_ada:],
        ada_f_b=small["dmod"][:, 6 * D:], norm_f_g=small["norm_f_g"])

    weights = dict(ada_w=ada_w, ada_b=ada_b, norm1_g=norm1_g, w_in=w_in, b_in=b_in, a_ln_g=a_ln_g, a_ln_b=a_ln_b,
                   a_spatial_w=a_spatial_w, a_spatial_b=a_spatial_b, b_conv_w=b_conv_w, b_conv_b=b_conv_b, b_gn_g=b_gn_g,
                   b_gn_b=b_gn_b, out_norm_a_g=out_norm_a_g, out_norm_b_g=out_norm_b_g, w_out=w_out, norm2_g=norm2_g,
                   w_ffn_in=w_ffn_in, w_ffn_out=w_ffn_out, ada_f_w=ada_f_w, ada_f_b=ada_f_b, norm_f_g=norm_f_g)
    m_in = dict(ada_w=m_ada_w, ada_b=m_ada_b, norm1_g=m_norm1_g, w_in=m_w_in, b_in=m_b_in, a_ln_g=m_a_ln_g, a_ln_b=m_a_ln_b,
                a_spatial_w=m_a_spatial_w, a_spatial_b=m_a_spatial_b, b_conv_w=m_b_conv_w, b_conv_b=m_b_conv_b,
                b_gn_g=m_b_gn_g, b_gn_b=m_b_gn_b, out_norm_a_g=m_out_norm_a_g, out_norm_b_g=m_out_norm_b_g, w_out=m_w_out,
                norm2_g=m_norm2_g, w_ffn_in=m_w_ffn_in, w_ffn_out=m_w_ffn_out, ada_f_w=m_ada_f_w, ada_f_b=m_ada_f_b,
                norm_f_g=m_norm_f_g)
    v_in = dict(ada_w=v_ada_w, ada_b=v_ada_b, norm1_g=v_norm1_g, w_in=v_w_in, b_in=v_b_in, a_ln_g=v_a_ln_g, a_ln_b=v_a_ln_b,
                a_spatial_w=v_a_spatial_w, a_spatial_b=v_a_spatial_b, b_conv_w=v_b_conv_w, b_conv_b=v_b_conv_b,
                b_gn_g=v_b_gn_g, b_gn_b=v_b_gn_b, out_norm_a_g=v_out_norm_a_g, out_norm_b_g=v_out_norm_b_g, w_out=v_w_out,
                norm2_g=v_norm2_g, w_ffn_in=v_w_ffn_in, w_ffn_out=v_w_ffn_out, ada_f_w=v_ada_f_w, ada_f_b=v_ada_f_b,
                norm_f_g=v_norm_f_g)
    names = list(weights)
    big = ("ada_w", "w_in", "w_out", "w_ffn_in", "w_ffn_out", "ada_f_w")

    def flat2(a):
        return a.reshape(-1, a.shape[-1])

    delta, new_m, new_v = {}, {}, {}
    for nm in big:
        shape = weights[nm].shape
        grads[nm] = grads[nm].reshape(shape)
        d_, m_, v_ = _adamw("adamw_" + nm, flat2(weights[nm]), flat2(grads[nm]), flat2(m_in[nm]), flat2(v_in[nm]))
        delta[nm], new_m[nm], new_v[nm] = d_.reshape(shape), m_.reshape(shape), v_.reshape(shape)

    small_names = [nm for nm in names if nm not in big]
    for nm in small_names:
        grads[nm] = grads[nm].reshape(weights[nm].shape)
    small_out = _adamw_many("adamw_small", *[[flat2(tree[nm]) for nm in small_names] for tree in (weights, grads, m_in, v_in)])
    for k, nm in enumerate(small_names):
        shape = weights[nm].shape
        delta[nm], new_m[nm], new_v[nm] = [o.reshape(shape) for o in small_out[3 * k:3 * k + 3]]

    grad_x = gx.reshape(x.shape)
    return (loss, grad_x, *[grads[nm] for nm in names], *[delta[nm] for nm in names],
            *[new_m[nm] for nm in names], *[new_v[nm] for nm in names])
```

```python
import functools
import math

import jax
import jax.numpy as jnp
from jax import lax
from jax.experimental import pallas as pl
from jax.experimental.pallas import tpu as pltpu

F32 = jnp.float32
MM_DTYPE = jnp.bfloat16
WIRE_DTYPE = jnp.bfloat16
SDS = jax.ShapeDtypeStruct
MESH = pl.DeviceIdType.MESH

D = 1024
DA = 512
DB = 512
NQ = 4
PW_IN = 512
DFF = 2816
PW_FF = 1408
CHUNK = 128
N_HEADS = 8
CONV_W = 31
HALO = 32
CONV_ROWS = 64
EPS = 1e-6
LANES = 128
HALF = 64

ROW_TILE = 256
FWD_ROW_TILE = 512
FFN_ROW_TILE = 256
GRAD_ROW_TILE = 2048
VMEM_LIMIT = 60 * 1024 * 1024

ADAM_LR = 0.001
ADAM_B1 = 0.9
ADAM_B2 = 0.999
ADAM_EPS = 1e-08
ADAM_WD = 0.01
ADAM_STEP = 10

PK_DMOD = 0
PK_G1 = 8
PK_G2 = 9
PK_GF = 10
PK_BIN = 11
PK_V512 = 13
PK_SB = 17
PK_CW = 18
PK_SW = 34
PK_LOSS = 162
PK_ROWS = 168


def _dot(a, b):
    return jnp.dot(a, b, preferred_element_type=F32)


def _dot_nt(a, b):
    return lax.dot_general(a, b, (((1,), (1,)), ((), ())), preferred_element_type=F32)


def _dot_tn(a, b):
    return lax.dot_general(a, b, (((0,), (0,)), ((), ())), preferred_element_type=F32)


def _rowsum(x):
    return jnp.sum(x, axis=-1, keepdims=True)


def _colsum(x):
    return jnp.sum(x, axis=0, keepdims=True)


def _group_sum(x):
    rows, width = x.shape
    lo_mask = lax.broadcasted_iota(jnp.int32, (rows, LANES), 1) < HALF
    outs = []
    for jb in range(width // LANES):
        xb = x[:, jb * LANES:(jb + 1) * LANES]
        lo = _rowsum(jnp.where(lo_mask, xb, 0.0))
        hi = _rowsum(jnp.where(lo_mask, 0.0, xb))
        outs.append(jnp.where(lo_mask, lo, hi))
    return jnp.concatenate(outs, axis=-1)


def _sigmoid(x):
    return 1.0 / (1.0 + jnp.exp(-x))


def _gelu_parts(u):
    cdf = 0.5 * (1.0 + lax.erf(u * (1.0 / math.sqrt(2.0))))
    pdf = jnp.exp(-0.5 * u * u) * (1.0 / math.sqrt(2.0 * math.pi))
    return u * cdf, cdf + u * pdf


def _whole_vmem():
    return pl.BlockSpec(memory_space=pltpu.VMEM)


def _params(*semantics):
    return pltpu.CompilerParams(dimension_semantics=semantics, vmem_limit_bytes=VMEM_LIMIT)


def _mod_rows(mod_ref, modb_ref, first, count):
    m = mod_ref[...] + modb_ref[...]
    return [m[:, (first + k) * D:(first + k + 1) * D] for k in range(count)]


def _mixer_recompute(z_parts, lng, lnb, wm_ref, bst_ref, mix_ref):
    u, v, val, gate = z_parts
    rows = u.shape[0]
    gu, dgu = _gelu_parts(u)
    gv, dgv = _gelu_parts(v)
    mu = _rowsum(gv) * (1.0 / DA)
    vc = gv - mu
    rs = lax.rsqrt(_rowsum(vc * vc) * (1.0 / DA) + EPS)
    vhat = vc * rs
    vl = vhat * lng + lnb
    vlb = vl.astype(MM_DTYPE)
    lo_mask = lax.broadcasted_iota(jnp.int32, (CHUNK, LANES), 1) < HALF
    for ck in range(rows // CHUNK):
        for jb in range(DA // LANES):
            blk = vlb[ck * CHUNK:(ck + 1) * CHUNK, jb * LANES:(jb + 1) * LANES]
            a = _dot(wm_ref[2 * jb], blk)
            b = _dot(wm_ref[2 * jb + 1], blk)
            mix_ref[ck * CHUNK:(ck + 1) * CHUNK, jb * LANES:(jb + 1) * LANES] = (
                jnp.where(lo_mask, a, b) + bst_ref[:, jb * LANES:(jb + 1) * LANES])
    mixed = mix_ref[...]
    sg = _sigmoid(gate)
    yb0 = val * sg
    return dict(gu=gu, dgu=dgu, dgv=dgv, rs=rs, vhat=vhat, vlb=vlb, mixed=mixed, sg=sg, yb0=yb0)


def _conv_branch_tail(yb1, gng, gnb):
    gm = _group_sum(yb1) * (1.0 / HALF)
    gc = yb1 - gm
    grs = lax.rsqrt(_group_sum(gc * gc) * (1.0 / HALF) + EPS)
    ghat = gc * grs
    yb2 = ghat * gng + gnb
    s2 = _sigmoid(yb2)
    return dict(grs=grs, ghat=ghat, yb2=yb2, s2=s2, y_b=yb2 * s2)


def _shifted_copies(e_ref, sh_ref):
    n = sh_ref.shape[1]
    for b in range(1, 8):
        sh_ref[b - 1] = e_ref[pl.ds(b, n), :]


def _window(e_ref, sh_ref, offset, r0, nrows, cols):
    a, b = divmod(offset, 8)
    if b == 0:
        return e_ref[pl.ds(r0 + 8 * a, nrows), cols]
    return sh_ref[b - 1, pl.ds(r0 + 8 * a, nrows), cols]


def _conv_taps(e_ref, sh_ref, cw_ref, out_ref, ts, first_offset, flip, bias_ref=None, other_ref=None, tap_acc_ref=None):
    groups = CONV_ROWS // 8
    for cb in range(DB // LANES):
        cols = slice(cb * LANES, (cb + 1) * LANES)
        tap_acc = [jnp.zeros((8, LANES), F32) for _ in range(CONV_W)]
        for rb in range(ts // CONV_ROWS):
            r0 = rb * CONV_ROWS
            acc = jnp.zeros((CONV_ROWS, LANES), F32)
            if bias_ref is not None:
                acc = acc + bias_ref[:, cols]
            if other_ref is not None:
                other = other_ref[r0:r0 + CONV_ROWS, cols]
            for j in range(CONV_W):
                k = CONV_W - 1 - j if flip else j
                win = _window(e_ref, sh_ref, first_offset + j, r0, CONV_ROWS, cols)
                acc = acc + win * cw_ref[k:k + 1, cols]
                if other_ref is not None:
                    tap_acc[k] = tap_acc[k] + jnp.sum((other * win).reshape(groups, 8, LANES), axis=0)
            out_ref[r0:r0 + CONV_ROWS, cols] = acc
        if other_ref is not None:
            for k in range(CONV_W):
                tap_acc_ref[k:k + 1, cols] += _colsum(tap_acc[k])


def _gather_plan(ins, outs, send_sems, recv_sems, local_sems=None):
    x, y, c = _place()
    sibling = (x, y, 1 - c)
    chips = _other_chips(x, y)
    myq = 2 * x + y

    def copy(w, k, q, hc, to, src=None):
        rows = ins[w].shape[0]
        dst = outs[w].at[q, pl.ds(hc * (rows // 2), rows // 2)]
        return pltpu.make_async_remote_copy(
            src_ref=dst if src is None else src, dst_ref=dst,
            send_sem=send_sems.at[w, k], recv_sem=recv_sems.at[w, k], device_id=to, device_id_type=MESH)

    def own(w):
        return pltpu.make_async_copy(ins[w], outs[w].at[myq], local_sems.at[w])

    def send():
        for w in range(len(ins)):
            rows = ins[w].shape[0]
            src = ins[w].at[pl.ds(c * (rows // 2), rows // 2)]
            for j, chip in enumerate(chips):
                copy(w, j, myq, c, (*chip, c), src=src).start()
            if local_sems is not None:
                own(w).start()

    def forward():
        for w in range(len(ins)):
            for j, (qx, qy) in enumerate(chips):
                copy(w, j, 2 * qx + qy, c, sibling).wait_recv()
                copy(w, 3 + j, 2 * qx + qy, c, sibling).start()

    def finish():
        for w in range(len(ins)):
            for j, (qx, qy) in enumerate(chips):
                copy(w, 3 + j, 2 * qx + qy, 1 - c, sibling).wait_recv()
        for w in range(len(ins)):
            for k, (qx, qy) in enumerate(chips + chips):
                copy(w, k, 2 * qx + qy, c, sibling).wait_send()
            if local_sems is not None:
                own(w).wait()

    return send, forward, finish


def _mixer_fwd(x, mod, mods, norm1_g, w_in4, b_in, ln_g, ln_b, wm, bst, conv_w, conv_b, gn_g, gn_b, ga, gb, w_out,
               ffn_shards):
    s = x.shape[0]
    ts = min(FWD_ROW_TILE, s)
    nt = s // ts
    n_sh = len(ffn_shards)

    def body(x_ref, mod_ref, modb_ref, g1_ref, w4_ref, bin_ref, lng_ref, lnb_ref, wm_ref, bst_ref, cw_ref, cb_ref,
             gng_ref, gnb_ref, ga_ref, gb_ref, wout_ref, *rest):
        shard_refs, rest = rest[:n_sh], rest[n_sh:]
        z_ref, x1_ref, yb1_ref, y_ref = rest[:4]
        full_refs, rest = rest[4:4 + n_sh], rest[4 + n_sh:]
        e_ref, sh_ref, mix_ref, send_sems, recv_sems, local_sems = rest
        i = pl.program_id(0)
        send, forward, finish = _gather_plan(shard_refs, full_refs, send_sems, recv_sems, local_sems)

        @pl.when(i == 0)
        def _():
            send()
            e_ref[0:HALO, :] = jnp.zeros((HALO, DB), F32)

        @pl.when(i == (3 * nt) // 4)
        def _():
            forward()

        shift1, scale1, gate1 = _mod_rows(mod_ref, modb_ref, 0, 3)
        x_t = x_ref[...]
        r1 = lax.rsqrt(_rowsum(x_t * x_t) * (1.0 / D) + EPS)
        h = (x_t * r1 * g1_ref[...]) * (1.0 + scale1) + shift1
        hb = h.astype(MM_DTYPE)
        z_parts = []
        for q in range(NQ):
            zq = _dot(hb, w4_ref[q]) + bin_ref[:, q * PW_IN:(q + 1) * PW_IN]
            z_ref[:, q * PW_IN:(q + 1) * PW_IN] = zq
            z_parts.append(zq)
        r = _mixer_recompute(z_parts, lng_ref[...], lnb_ref[...], wm_ref, bst_ref, mix_ref)
        y_a = r["gu"] * r["mixed"]
        e_ref[HALO:HALO + ts, :] = r["yb0"]
        _shifted_copies(e_ref, sh_ref)
        _conv_taps(e_ref, sh_ref, cw_ref, yb1_ref, ts, HALO - (CONV_W - 1), False, bias_ref=cb_ref)
        e_ref[0:HALO, :] = e_ref[ts:ts + HALO, :]
        t = _conv_branch_tail(yb1_ref[...], gng_ref[...], gnb_ref[...])
        ra = lax.rsqrt(_rowsum(y_a * y_a) * (1.0 / DA) + EPS)
        rb = lax.rsqrt(_rowsum(t["y_b"] * t["y_b"]) * (1.0 / DB) + EPS)
        yan = (y_a * ra * ga_ref[...]).astype(MM_DTYPE)
        ybn = (t["y_b"] * rb * gb_ref[...]).astype(MM_DTYPE)
        y_ref[:, 0:DA] = yan
        y_ref[:, DA:D] = ybn
        o1 = _dot(yan, wout_ref[0:DA, :]) + _dot(ybn, wout_ref[DA:D, :])
        x1_ref[...] = x_t + gate1 * o1

        @pl.when(i == nt - 1)
        def _():
            finish()

    row = lambda w: pl.BlockSpec((ts, w), lambda i: (i, 0))
    full = lambda a: pl.BlockSpec(a.shape, lambda i: (0,) * a.ndim)
    return pl.pallas_call(
        body, name="mixer_fwd", grid=(nt,),
        in_specs=[row(D), full(mod), full(mods), full(norm1_g), _whole_vmem(), full(b_in), full(ln_g), full(ln_b),
                  _whole_vmem(), full(bst), full(conv_w), full(conv_b), full(gn_g), full(gn_b), full(ga), full(gb),
                  _whole_vmem()] + [_any()] * n_sh,
        out_specs=[row(4 * PW_IN), row(D), row(DB), row(D)] + [_any()] * n_sh,
        out_shape=[SDS((s, 4 * PW_IN), F32), SDS((s, D), F32), SDS((s, DB), F32), SDS((s, D), MM_DTYPE)]
        + [SDS((NQ,) + w.shape, w.dtype) for w in ffn_shards],
        scratch_shapes=[pltpu.VMEM((ts + HALO, DB), F32), pltpu.VMEM((7, ts + HALO - 8, DB), F32), pltpu.VMEM((ts, DA), F32),
                        pltpu.SemaphoreType.DMA((n_sh, 6)), pltpu.SemaphoreType.DMA((n_sh, 6)),
                        pltpu.SemaphoreType.DMA((n_sh,))],
        compiler_params=_params("arbitrary"),
    )(x, mod, mods, norm1_g, w_in4, b_in, ln_g, ln_b, wm, bst, conv_w, conv_b, gn_g, gn_b, ga, gb, w_out, *ffn_shards)


def _ffn_fwd(x1, target, mod, mods, norm2_g, norm_f_g, w_ffn_in4, w_ffn_out):
    s = x1.shape[0]
    sub_rows = min(FFN_ROW_TILE, s)
    ts = min(2 * sub_rows, s)
    nt = s // ts

    def body(x1_ref, tgt_ref, mod_ref, modb_ref, g2_ref, gf_ref, wf_ref, wo_ref,
             g_ref, up_ref, h2_ref, dx2_ref, acc_ref):
        i = pl.program_id(0)

        @pl.when(i == 0)
        def _():
            acc_ref[...] = jnp.zeros(acc_ref.shape, F32)

        shift2, scale2, gate2, shift_f, scale_f = _mod_rows(mod_ref, modb_ref, 3, 5)
        for sub in range(ts // sub_rows):
            rows = slice(sub * sub_rows, (sub + 1) * sub_rows)
            x1_t = x1_ref[rows, :]
            r2 = lax.rsqrt(_rowsum(x1_t * x1_t) * (1.0 / D) + EPS)
            h2 = (x1_t * r2 * g2_ref[...]) * (1.0 + scale2) + shift2
            h2b = h2.astype(MM_DTYPE)
            h2_ref[rows, :] = h2b
            o2 = jnp.zeros((sub_rows, D), F32)
            for p in range(2):
                g = _dot(h2b, wf_ref[p])
                up = _dot(h2b, wf_ref[2 + p])
                g_ref[rows, p * PW_FF:(p + 1) * PW_FF] = g.astype(MM_DTYPE)
                up_ref[rows, p * PW_FF:(p + 1) * PW_FF] = up.astype(MM_DTYPE)
                a = (g * _sigmoid(g) * up).astype(MM_DTYPE)
                o2 = o2 + _dot(a, wo_ref[p * PW_FF:(p + 1) * PW_FF, :])
            x2 = x1_t + gate2 * o2
            rf = lax.rsqrt(_rowsum(x2 * x2) * (1.0 / D) + EPS)
            gf = gf_ref[...]
            nf = x2 * rf * gf
            err = nf * (1.0 + scale_f) + shift_f - tgt_ref[rows, :]
            d_out = err * (1.0 / D)
            d_nf = d_out * (1.0 + scale_f)
            t = d_nf * gf
            dx2_ref[rows, :] = rf * t - x2 * (rf * rf * rf) * (_rowsum(t * x2) * (1.0 / D))
            acc_ref[0:1, :] += _colsum(d_out)
            acc_ref[1:2, :] += _colsum(d_out * nf)
            acc_ref[2:3, :] += _colsum(d_nf * x2 * rf)
            acc_ref[3:4, :] += _colsum(err * err)

        @pl.when(i == nt - 1)
        def _():
            acc_ref[4:5, :] = jnp.zeros((1, D), F32) + _rowsum(acc_ref[3:4, :]) * (0.5 / D)

    row = lambda w: pl.BlockSpec((ts, w), lambda i: (i, 0))
    full = lambda a: pl.BlockSpec(a.shape, lambda i: (0,) * a.ndim)
    return pl.pallas_call(
        body, name="ffn_fwd", grid=(nt,),
        in_specs=[row(D), row(D), full(mod), full(mods), full(norm2_g), full(norm_f_g), _whole_vmem(), _whole_vmem()],
        out_specs=[row(DFF), row(DFF), row(D), row(D), pl.BlockSpec((8, D), lambda i: (0, 0))],
        out_shape=[SDS((s, DFF), MM_DTYPE), SDS((s, DFF), MM_DTYPE), SDS((s, D), MM_DTYPE), SDS((s, D), F32),
                   SDS((8, D), F32)],
        compiler_params=_params("arbitrary"),
    )(x1, target, mod, mods, norm2_g, norm_f_g, w_ffn_in4, w_ffn_out)


def _ffn_bwd(dx2, x1, g, up, mod, mods, norm2_g, w_ffn_in4, w_ffn_out):
    s = x1.shape[0]
    ts = min(FFN_ROW_TILE, s)
    nt = s // ts

    def body(dx2_ref, x1_ref, g_ref, up_ref, mod_ref, modb_ref, g2_ref, wf_ref, wo_ref,
             dff_ref, a_ref, dx1_ref, acc_ref):
        @pl.when(pl.program_id(0) == 0)
        def _():
            acc_ref[...] = jnp.zeros(acc_ref.shape, F32)

        shift2, scale2, gate2 = _mod_rows(mod_ref, modb_ref, 3, 3)
        dx2_t = dx2_ref[...]
        do2 = (dx2_t * gate2).astype(MM_DTYPE)
        dh2 = jnp.zeros((ts, D), F32)
        for p in range(2):
            da = _dot_nt(do2, wo_ref[p * PW_FF:(p + 1) * PW_FF, :])
            gp = g_ref[:, p * PW_FF:(p + 1) * PW_FF].astype(F32)
            upp = up_ref[:, p * PW_FF:(p + 1) * PW_FF].astype(F32)
            sg = _sigmoid(gp)
            silu = gp * sg
            a_ref[:, p * PW_FF:(p + 1) * PW_FF] = (silu * upp).astype(MM_DTYPE)
            dg = (da * upp * (sg * (1.0 + gp * (1.0 - sg)))).astype(MM_DTYPE)
            dup = (da * silu).astype(MM_DTYPE)
            dff_ref[:, p * PW_FF:(p + 1) * PW_FF] = dg
            dff_ref[:, DFF + p * PW_FF:DFF + (p + 1) * PW_FF] = dup
            dh2 = dh2 + _dot_nt(dg, wf_ref[p]) + _dot_nt(dup, wf_ref[2 + p])
        x1_t = x1_ref[...]
        r2 = lax.rsqrt(_rowsum(x1_t * x1_t) * (1.0 / D) + EPS)
        g2 = g2_ref[...]
        xr = x1_t * r2
        dn2 = dh2 * (1.0 + scale2)
        t = dn2 * g2
        dx1_ref[...] = dx2_t + r2 * t - x1_t * (r2 * r2 * r2) * (_rowsum(t * x1_t) * (1.0 / D))
        acc_ref[0:1, :] += _colsum(dh2)
        acc_ref[1:2, :] += _colsum(dh2 * (xr * g2))
        acc_ref[2:3, :] += _colsum(dn2 * xr)

    row = lambda w: pl.BlockSpec((ts, w), lambda i: (i, 0))
    full = lambda a: pl.BlockSpec(a.shape, lambda i: (0,) * a.ndim)
    return pl.pallas_call(
        body, name="ffn_bwd", grid=(nt,),
        in_specs=[row(D), row(D), row(DFF), row(DFF), full(mod), full(mods), full(norm2_g), _whole_vmem(), _whole_vmem()],
        out_specs=[row(2 * DFF), row(DFF), row(D), pl.BlockSpec((8, D), lambda i: (0, 0))],
        out_shape=[SDS((s, 2 * DFF), MM_DTYPE), SDS((s, DFF), MM_DTYPE), SDS((s, D), F32), SDS((8, D), F32)],
        compiler_params=_params("arbitrary"),
    )(dx2, x1, g, up, mod, mods, norm2_g, w_ffn_in4, w_ffn_out)


def _scatter_plan(ins, outs, send_sems, recv_sems, local_sems=None):
    x, y, c = _place()
    me = 4 * x + 2 * y + c

    def copies():
        cps = []
        for w in range(len(ins)):
            h = ins[w].shape[1] // 2
            for k in range(1, 8):
                px, py, pc = (1 - x if k & 4 else x), (1 - y if k & 2 else y), (1 - c if k & 1 else c)
                cps.append(pltpu.make_async_remote_copy(
                    src_ref=ins[w].at[2 * px + py, pl.ds(pc * h, h)], dst_ref=outs[w].at[me],
                    send_sem=send_sems.at[w, k - 1], recv_sem=recv_sems.at[w, k - 1],
                    device_id=(px, py, pc), device_id_type=MESH))
        return cps

    def own():
        if local_sems is None:
            return []
        return [pltpu.make_async_copy(ins[w].at[2 * x + y, pl.ds(c * (ins[w].shape[1] // 2), ins[w].shape[1] // 2)],
                                      outs[w].at[me], local_sems.at[w]) for w in range(len(ins))]

    def send():
        for cp in copies() + own():
            cp.start()

    def finish():
        for cp in copies() + own():
            cp.wait()

    return send, finish


def _scattered_shape(partial):
    nq, rows, cols = partial.shape
    return SDS((8, rows // 2, cols), partial.dtype)


def _mixer_bwd(dx1, x, z, yb1, mod, mods, norm1_g, w_in4, ln_g, ln_b, wm, wmt, bst, conv_w, gn_g, gn_b, ga, gb, w_out,
               partials):
    s = x.shape[0]
    ts = min(ROW_TILE, s)
    nt = s // ts
    n_cs = len(partials)

    def body(dx1_ref, x_ref, z_ref, yb1_ref, mod_ref, modb_ref, g1_ref, w4_ref, lng_ref, lnb_ref, wm_ref, wmt_ref,
             bst_ref, cw_ref, gng_ref, gnb_ref, ga_ref, gb_ref, wout_ref, *rest):
        cs_refs, rest = rest[:n_cs], rest[n_cs:]
        gx_ref, dz_ref, h_ref, a1_ref, a2_ref, a5_ref, acw_ref, asw_ref, asb_ref = rest[:9]
        arrived_refs, rest = rest[9:9 + n_cs], rest[9 + n_cs:]
        e_ref, sh_ref, mix_ref, dvl_ref, send_sems, recv_sems, local_sems = rest
        i = pl.program_id(0)
        send, finish = _scatter_plan(cs_refs, arrived_refs, send_sems, recv_sems, local_sems)

        @pl.when(i == 0)
        def _():
            send()
            e_ref[ts:ts + HALO, :] = jnp.zeros((HALO, DB), F32)
            for r in (a1_ref, a2_ref, a5_ref, acw_ref, asw_ref, asb_ref):
                r[...] = jnp.zeros(r.shape, F32)

        shift1, scale1, gate1 = _mod_rows(mod_ref, modb_ref, 0, 3)
        dx1_t = dx1_ref[...]
        do1 = (dx1_t * gate1).astype(MM_DTYPE)
        d_yan = _dot_nt(do1, wout_ref[0:DA, :])
        d_ybn = _dot_nt(do1, wout_ref[DA:D, :])

        z_parts = [z_ref[:, q * PW_IN:(q + 1) * PW_IN] for q in range(NQ)]
        u, v, val, gate = z_parts
        lng = lng_ref[...]
        r = _mixer_recompute(z_parts, lng, lnb_ref[...], wm_ref, bst_ref, mix_ref)
        gng = gng_ref[...]
        t = _conv_branch_tail(yb1_ref[...], gng, gnb_ref[...])
        y_a = r["gu"] * r["mixed"]
        y_b = t["y_b"]
        ga_v, gb_v = ga_ref[...], gb_ref[...]
        ra = lax.rsqrt(_rowsum(y_a * y_a) * (1.0 / DA) + EPS)
        rb = lax.rsqrt(_rowsum(y_b * y_b) * (1.0 / DB) + EPS)

        a5_ref[0:1, :] += _colsum(d_yan * y_a * ra)
        a5_ref[1:2, :] += _colsum(d_ybn * y_b * rb)
        ta = d_yan * ga_v
        d_ya = ra * ta - y_a * (ra * ra * ra) * (_rowsum(ta * y_a) * (1.0 / DA))
        tb = d_ybn * gb_v
        d_yb = rb * tb - y_b * (rb * rb * rb) * (_rowsum(tb * y_b) * (1.0 / DB))

        d_u = d_ya * r["mixed"] * r["dgu"]
        d_mixed = d_ya * r["gu"]
        dmb = d_mixed.astype(MM_DTYPE)
        lo_mask = lax.broadcasted_iota(jnp.int32, (CHUNK, LANES), 1) < HALF
        zero_blk = jnp.zeros((CHUNK, LANES), MM_DTYPE)
        sb_acc = jnp.zeros((CHUNK, DA), F32)
        for ck in range(ts // CHUNK):
            rows = slice(ck * CHUNK, (ck + 1) * CHUNK)
            sb_acc = sb_acc + d_mixed[rows, :]
            for jb in range(DA // LANES):
                cols = slice(jb * LANES, (jb + 1) * LANES)
                dm_blk = dmb[rows, cols]
                vl_blk = r["vlb"][rows, cols]
                da_ = _dot(wmt_ref[2 * jb], dm_blk)
                db_ = _dot(wmt_ref[2 * jb + 1], dm_blk)
                dvl_ref[rows, cols] = jnp.where(lo_mask, da_, db_)
                asw_ref[2 * jb] += _dot_nt(jnp.where(lo_mask, dm_blk, zero_blk), vl_blk)
                asw_ref[2 * jb + 1] += _dot_nt(jnp.where(lo_mask, zero_blk, dm_blk), vl_blk)
        asb_ref[...] += sb_acc
        d_vl = dvl_ref[...]
        a5_ref[2:3, :] += _colsum(d_vl * r["vhat"])
        a5_ref[3:4, :] += _colsum(d_vl)
        dvh = d_vl * lng
        d_gv = r["rs"] * (dvh - _rowsum(dvh) * (1.0 / DA) - r["vhat"] * (_rowsum(dvh * r["vhat"]) * (1.0 / DA)))
        d_v = d_gv * r["dgv"]

        yb2, s2 = t["yb2"], t["s2"]
        d_yb2 = d_yb * (s2 * (1.0 + yb2 * (1.0 - s2)))
        a5_ref[4:5, :] += _colsum(d_yb2 * t["ghat"])
        a5_ref[5:6, :] += _colsum(d_yb2)
        dgh = d_yb2 * gng
        d_yb1 = t["grs"] * (dgh - _group_sum(dgh) * (1.0 / HALF) - t["ghat"] * (_group_sum(dgh * t["ghat"]) * (1.0 / HALF)))
        a5_ref[6:7, :] += _colsum(d_yb1)
        e_ref[0:ts, :] = d_yb1
        _shifted_copies(e_ref, sh_ref)
        mix_ref[...] = r["yb0"]
        _conv_taps(e_ref, sh_ref, cw_ref, dvl_ref, ts, 0, True, other_ref=mix_ref, tap_acc_ref=acw_ref)
        d_yb0 = dvl_ref[...]
        e_ref[ts:ts + HALO, :] = e_ref[0:HALO, :]
        sg = r["sg"]
        d_val = d_yb0 * sg
        d_gate = d_yb0 * val * sg * (1.0 - sg)

        dh = jnp.zeros((ts, D), F32)
        for q, dzq in enumerate((d_u, d_v, d_val, d_gate)):
            a2_ref[0:1, q * PW_IN:(q + 1) * PW_IN] += _colsum(dzq)
            dzb = dzq.astype(MM_DTYPE)
            dz_ref[:, q * PW_IN:(q + 1) * PW_IN] = dzb
            dh = dh + _dot_nt(dzb, w4_ref[q])
        x_t = x_ref[...]
        r1 = lax.rsqrt(_rowsum(x_t * x_t) * (1.0 / D) + EPS)
        g1 = g1_ref[...]
        xr = x_t * r1
        n1 = xr * g1
        h_ref[...] = (n1 * (1.0 + scale1) + shift1).astype(MM_DTYPE)
        dn1 = dh * (1.0 + scale1)
        t1 = dn1 * g1
        gx_ref[...] = dx1_t + r1 * t1 - x_t * (r1 * r1 * r1) * (_rowsum(t1 * x_t) * (1.0 / D))
        a1_ref[0:1, :] += _colsum(dh)
        a1_ref[1:2, :] += _colsum(dh * n1)
        a1_ref[2:3, :] += _colsum(dn1 * xr)

        @pl.when(i == nt - 1)
        def _():
            asb_ref[...] = _group_sum(asb_ref[...])
            finish()

    row = lambda w: pl.BlockSpec((ts, w), lambda i: (nt - 1 - i, 0))
    full = lambda a: pl.BlockSpec(a.shape, lambda i: (0,) * a.ndim)
    keep = lambda shape: pl.BlockSpec(shape, lambda i: (0,) * len(shape))
    return pl.pallas_call(
        body, name="mixer_bwd", grid=(nt,),
        in_specs=[row(D), row(D), row(4 * PW_IN), row(DB), full(mod), full(mods), full(norm1_g), _whole_vmem(),
                  full(ln_g), full(ln_b), _whole_vmem(), _whole_vmem(), full(bst), full(conv_w), full(gn_g), full(gn_b),
                  full(ga), full(gb), _whole_vmem()] + [_any()] * n_cs,
        out_specs=[row(D), row(4 * PW_IN), row(D), keep((8, D)), keep((8, 4 * PW_IN)), keep((8, DA)),
                   keep((HALO, DB)), keep((N_HEADS, CHUNK, CHUNK)), keep((CHUNK, DA))] + [_any()] * n_cs,
        out_shape=[SDS((s, D), F32), SDS((s, 4 * PW_IN), MM_DTYPE), SDS((s, D), MM_DTYPE), SDS((8, D), F32),
                   SDS((8, 4 * PW_IN), F32), SDS((8, DA), F32), SDS((HALO, DB), F32),
                   SDS((N_HEADS, CHUNK, CHUNK), F32), SDS((CHUNK, DA), F32)]
        + [_scattered_shape(p) for p in partials],
        scratch_shapes=[pltpu.VMEM((ts + HALO, DB), F32), pltpu.VMEM((7, ts + HALO - 8, DB), F32),
                        pltpu.VMEM((ts, DA), F32), pltpu.VMEM((ts, DA), F32),
                        pltpu.SemaphoreType.DMA((n_cs, 7)), pltpu.SemaphoreType.DMA((n_cs, 7)),
                        pltpu.SemaphoreType.DMA((n_cs,))],
        compiler_params=_params("arbitrary"),
    )(dx1, x, z, yb1, mod, mods, norm1_g, w_in4, ln_g, ln_b, wm, wmt, bst, conv_w, gn_g, gn_b, ga, gb, w_out, *partials)


def _gather8_plan(x_ref, out_ref, send_sems, recv_sems, local_sem):
    x, y, c = _place()
    me, sibling = (x, y, c), (x, y, 1 - c)
    chips = _other_chips(x, y)

    def copy(k, block, to, src=None):
        dst = out_ref.at[4 * block[0] + 2 * block[1] + block[2]]
        return pltpu.make_async_remote_copy(src_ref=dst if src is None else src, dst_ref=dst, send_sem=send_sems.at[k],
                                            recv_sem=recv_sems.at[k], device_id=to, device_id_type=MESH)

    def own():
        return pltpu.make_async_copy(x_ref, out_ref.at[4 * x + 2 * y + c], local_sem)

    def send():
        own().start()
        copy(0, me, sibling, src=x_ref).start()
        for j, chip in enumerate(chips):
            copy(1 + j, me, (*chip, c), src=x_ref).start()

    def forward():
        for j, chip in enumerate(chips):
            copy(1 + j, (*chip, c), me).wait_recv()
            copy(4 + j, (*chip, c), sibling).start()

    def finish():
        copy(0, sibling, me).wait_recv()
        for j, chip in enumerate(chips):
            copy(4 + j, (*chip, 1 - c), me).wait_recv()
        for k in range(7):
            copy(k, me, sibling).wait_send()
        own().wait()

    return send, forward, finish


def _grad_matmul(name, a, b, ka_tile, nb_tile, piece_w=None, gated=None, gather_blk=None):
    s, ka = a.shape
    nb = b.shape[1]
    ts = min(GRAD_ROW_TILE, s)
    nt = s // ts
    nja, njb = ka // ka_tile, nb // nb_tile
    steps = nja * njb * nt
    n_in = 2 + (2 if gated else 0) + (1 if gather_blk is not None else 0)
    n_out = 1 + (1 if gated else 0) + (1 if gather_blk is not None else 0)
    assert not (gated and njb != 1) and not (piece_w and nja != 1)

    def body(*refs):
        ins, outs, scratch = refs[:n_in], refs[n_in:n_in + n_out], refs[n_in + n_out:]
        a_ref, b_ref, o_ref, acc_ref = ins[0], ins[1], outs[0], scratch[0]
        step = (pl.program_id(0) * njb + pl.program_id(1)) * nt + pl.program_id(2)
        if gather_blk is not None:
            send, forward, finish = _gather8_plan(ins[-1], outs[-1], *scratch[1:])

            @pl.when(step == 0)
            def _():
                send()

            @pl.when(step == (3 * steps) // 4)
            def _():
                forward()

        prod = _dot_tn(a_ref[...].astype(MM_DTYPE), b_ref[...].astype(MM_DTYPE))

        @pl.when(pl.program_id(2) == 0)
        def _():
            acc_ref[...] = prod

        @pl.when(pl.program_id(2) > 0)
        def _():
            acc_ref[...] += prod

        @pl.when(pl.program_id(2) == nt - 1)
        def _():
            gm = acc_ref[...]
            if gated:
                gate_ref, w_ref, dg_ref = ins[2], ins[3], outs[1]

                @pl.when(step == nt - 1)
                def _():
                    dg_ref[...] = jnp.zeros(dg_ref.shape, F32)

                dg_ref[0:1, :] += _colsum(gm * w_ref[...].astype(F32))
                gm = gm * gate_ref[...]
            if piece_w:
                for q in range(nb_tile // piece_w):
                    o_ref[q] = gm[:, q * piece_w:(q + 1) * piece_w].astype(WIRE_DTYPE)
            else:
                o_ref[...] = gm.astype(WIRE_DTYPE)

        if gather_blk is not None:
            @pl.when(step == steps - 1)
            def _():
                finish()

    in_specs = [pl.BlockSpec((ts, ka_tile), lambda ja, jb, i: (i, ja)),
                pl.BlockSpec((ts, nb_tile), lambda ja, jb, i: (i, jb))]
    operands = [a, b]
    if piece_w:
        out_shape = [SDS((nb // piece_w, ka, piece_w), WIRE_DTYPE)]
        out_specs = [pl.BlockSpec((nb_tile // piece_w, ka, piece_w), lambda ja, jb, i: (jb, 0, 0))]
    else:
        out_shape = [SDS((ka, nb), WIRE_DTYPE)]
        out_specs = [pl.BlockSpec((ka_tile, nb_tile), lambda ja, jb, i: (ja, jb))]
    scratch = [pltpu.VMEM((ka_tile, nb_tile), F32)]
    if gated:
        in_specs += [pl.BlockSpec((1, nb_tile), lambda ja, jb, i: (0, jb)),
                     pl.BlockSpec((ka_tile, nb_tile), lambda ja, jb, i: (ja, jb))]
        operands += list(gated)
        out_shape.append(SDS((8, nb), F32))
        out_specs.append(pl.BlockSpec((8, nb_tile), lambda ja, jb, i: (0, jb)))
    if gather_blk is not None:
        in_specs.append(_any())
        operands.append(gather_blk)
        out_shape.append(SDS((8,) + gather_blk.shape, gather_blk.dtype))
        out_specs.append(_any())
        scratch += [pltpu.SemaphoreType.DMA((7,)), pltpu.SemaphoreType.DMA((7,)), pltpu.SemaphoreType.DMA]
    return pl.pallas_call(
        body, name=name, grid=(nja, njb, nt), in_specs=in_specs, out_specs=out_specs, out_shape=out_shape,
        scratch_shapes=scratch, compiler_params=_params("arbitrary", "arbitrary", "arbitrary"),
    )(*operands)


COND_COLS = 512


def _cond_partial(c_all, w_a, w_f):
    na = w_a.shape[1]

    def body(c_ref, wa_ref, wf_ref, oa_ref, of_ref):
        c_t = c_ref[...]
        ca = (c_t * _sigmoid(c_t)).astype(MM_DTYPE)
        oa_ref[...] = _dot(ca, wa_ref[...].astype(MM_DTYPE))

        @pl.when(pl.program_id(0) == 0)
        def _():
            of_ref[...] = _dot(ca, wf_ref[...].astype(MM_DTYPE))

    keep = lambda shape: pl.BlockSpec(shape, lambda j: (0, 0))
    return pl.pallas_call(
        body, name="cond_partial", grid=(na // COND_COLS,),
        in_specs=[keep((8, D)), pl.BlockSpec((D, COND_COLS), lambda j: (0, j)), keep(w_f.shape)],
        out_specs=[pl.BlockSpec((8, COND_COLS), lambda j: (0, j)), keep((8, w_f.shape[1]))],
        out_shape=[SDS((8, na), F32), SDS((8, w_f.shape[1]), F32)],
        compiler_params=_params("arbitrary"),
    )(c_all, w_a, w_f)


def _cond_grad(c_all, dmod_a, dmod_f):
    na = dmod_a.shape[1]

    def body(c_ref, da_ref, df_ref, oa_ref, of_ref):
        c_t = c_ref[...]
        ca = jnp.concatenate([c_t * _sigmoid(c_t), jnp.zeros((8, D), F32)], axis=0).astype(MM_DTYPE)

        def outer(d_ref):
            dm = jnp.concatenate([d_ref[...], jnp.zeros(d_ref.shape, F32)], axis=0).astype(MM_DTYPE)
            return _dot_tn(ca, dm)

        oa_ref[...] = outer(da_ref)

        @pl.when(pl.program_id(0) == 0)
        def _():
            of_ref[...] = outer(df_ref)

    keep = lambda shape: pl.BlockSpec(shape, lambda j: (0, 0))
    return pl.pallas_call(
        body, name="cond_grad", grid=(na // COND_COLS,),
        in_specs=[keep((8, D)), pl.BlockSpec((8, COND_COLS), lambda j: (0, j)), keep(dmod_f.shape)],
        out_specs=[pl.BlockSpec((D, COND_COLS), lambda j: (0, j)), keep((D, dmod_f.shape[1]))],
        out_shape=[SDS((D, na), F32), SDS((D, dmod_f.shape[1]), F32)],
        compiler_params=_params("arbitrary"),
    )(c_all, dmod_a, dmod_f)


def _row_tile(rows, cap=256):
    if rows <= cap:
        return rows
    for t in range(cap, 7, -8):
        if rows % t == 0:
            return t
    return rows


def _ordered_sum(name, parts, out_dtype=F32):
    n, rows, cols = parts.shape
    rt = _row_tile(rows)

    def body(p_ref, o_ref):
        acc = p_ref[0].astype(F32)
        for k in range(1, n):
            acc = acc + p_ref[k].astype(F32)
        o_ref[...] = acc.astype(out_dtype)

    return pl.pallas_call(
        body, name=name, grid=(rows // rt,),
        in_specs=[pl.BlockSpec((n, rt, cols), lambda i: (0, i, 0))],
        out_specs=pl.BlockSpec((rt, cols), lambda i: (i, 0)), out_shape=SDS((rows, cols), out_dtype),
        compiler_params=_params("parallel"),
    )(parts)


def _adamw_update(w_ref, g_ref, m_ref, v_ref, d_ref, nm_ref, nv_ref):
    c1 = 1.0 - ADAM_B1 ** ADAM_STEP
    c2 = 1.0 - ADAM_B2 ** ADAM_STEP
    g_t = g_ref[...]
    m_new = ADAM_B1 * m_ref[...] + (1.0 - ADAM_B1) * g_t
    v_new = ADAM_B2 * v_ref[...] + (1.0 - ADAM_B2) * (g_t * g_t)
    nm_ref[...] = m_new
    nv_ref[...] = v_new
    d_ref[...] = -ADAM_LR * ((m_new / c1) / (jnp.sqrt(v_new / c2) + ADAM_EPS) + ADAM_WD * w_ref[...])


def _adamw_many(name, ws, gs, ms, vs):
    n = len(ws)

    def body(*refs):
        ins, outs = refs[:4 * n], refs[4 * n:]
        for k in range(n):
            _adamw_update(ins[k], ins[n + k], ins[2 * n + k], ins[3 * n + k], *outs[3 * k:3 * k + 3])

    return pl.pallas_call(
        body, name=name, out_shape=[SDS(w.shape, F32) for w in ws for _ in range(3)],
        compiler_params=pltpu.CompilerParams(vmem_limit_bytes=VMEM_LIMIT),
    )(*ws, *gs, *ms, *vs)


def _adamw(name, w, g, m, v):
    rows, cols = w.shape
    rt = _row_tile(rows)

    def body(*refs):
        _adamw_update(*refs)

    spec = pl.BlockSpec((rt, cols), lambda i: (i, 0))
    out = SDS((rows, cols), F32)
    return pl.pallas_call(body, name=name, grid=(rows // rt,), in_specs=[spec] * 4, out_specs=[spec] * 3,
                          out_shape=[out, out, out], compiler_params=_params("parallel"))(w, g, m, v)


def _place():
    return lax.axis_index("x"), lax.axis_index("y"), lax.axis_index("c")


def _other_chips(x, y):
    return [(1 - x, y), (x, 1 - y), (1 - x, 1 - y)]


def _all_gather8(name, blk):
    m, n = blk.shape

    def body(x_ref, out_ref, send_sems, recv_sems, local_sem):
        x, y, c = _place()
        me, sibling = (x, y, c), (x, y, 1 - c)
        chips = _other_chips(x, y)

        def slot(px, py, pc):
            return out_ref.at[4 * px + 2 * py + pc]

        def copy(k, block, to, src=None):
            return pltpu.make_async_remote_copy(
                src_ref=slot(*block) if src is None else src, dst_ref=slot(*block),
                send_sem=send_sems.at[k], recv_sem=recv_sems.at[k], device_id=to, device_id_type=MESH)

        mine = pltpu.make_async_copy(x_ref, slot(*me), local_sem)
        mine.start()
        first = [copy(0, me, sibling, src=x_ref)]
        first += [copy(1 + j, me, (*chip, c), src=x_ref) for j, chip in enumerate(chips)]
        for cp in first:
            cp.start()
        passed = [copy(4 + j, (*chip, c), sibling) for j, chip in enumerate(chips)]
        for j, chip in enumerate(chips):
            copy(1 + j, (*chip, c), me).wait_recv()
            passed[j].start()
        copy(0, sibling, me).wait_recv()
        for j, chip in enumerate(chips):
            copy(4 + j, (*chip, 1 - c), me).wait_recv()
        for cp in first + passed:
            cp.wait_send()
        mine.wait()

    return pl.pallas_call(
        body, name=name, out_shape=SDS((8, m, n), blk.dtype),
        in_specs=[_whole_vmem()], out_specs=_whole_vmem(),
        scratch_shapes=[pltpu.SemaphoreType.DMA((7,)), pltpu.SemaphoreType.DMA((7,)), pltpu.SemaphoreType.DMA],
        compiler_params=pltpu.CompilerParams(vmem_limit_bytes=VMEM_LIMIT),
    )(blk)


def _any():
    return pl.BlockSpec(memory_space=pl.ANY)


def _gather_weights(shards, blk):
    n = len(shards)

    def body(*refs):
        ins, outs, sems = refs[:n + 1], refs[n + 1:2 * n + 2], refs[2 * n + 2:]
        send, forward, finish = _gather_plan(ins[:n], outs[:n], sems[0], sems[1])
        send8, forward8, finish8 = _gather8_plan(ins[n], outs[n], *sems[2:])
        send8()
        send()
        forward8()
        forward()
        finish8()
        finish()

    return pl.pallas_call(
        body, name="gather_weights",
        out_shape=[SDS((NQ,) + s.shape, s.dtype) for s in shards] + [SDS((8,) + blk.shape, blk.dtype)],
        in_specs=[_any()] * (n + 1), out_specs=[_any()] * (n + 1),
        scratch_shapes=[pltpu.SemaphoreType.DMA((n, 6)), pltpu.SemaphoreType.DMA((n, 6)),
                        pltpu.SemaphoreType.DMA((7,)), pltpu.SemaphoreType.DMA((7,)), pltpu.SemaphoreType.DMA],
    )(*shards, blk)


def _own_piece(gathered, shard):
    myq = 2 * lax.axis_index("x") + lax.axis_index("y")
    return lax.dynamic_update_slice(gathered, shard[None], (myq,) + (0,) * shard.ndim)


def _scatter_to_owners(partials):
    n = len(partials)

    def body(*refs):
        send, finish = _scatter_plan(refs[:n], refs[n:2 * n], *refs[2 * n:])
        send()
        finish()

    return pl.pallas_call(
        body, name="scatter_to_owners",
        out_shape=[_scattered_shape(p) for p in partials],
        in_specs=[_any()] * n, out_specs=[_any()] * n,
        scratch_shapes=[pltpu.SemaphoreType.DMA((n, 7)), pltpu.SemaphoreType.DMA((n, 7))],
    )(*partials)


def _owner_sums(tag, arrived, partials=None):
    x, y, c = _place()
    sums = []
    for w, arr in enumerate(arrived):
        if partials is not None:
            part = partials[w]
            h = part.shape[1] // 2
            own = lax.dynamic_slice(part, (2 * x + y, c * h, 0), (1, h, part.shape[2]))
            arr = lax.dynamic_update_slice(arr, own, (4 * x + 2 * y + c, 0, 0))
        sums.append(_ordered_sum(f"owner_sum_{tag}_{w}", arr))
    return sums


def _join_halves(halves):
    n = len(halves)

    def body(*refs):
        ins, outs = refs[:n], refs[n:2 * n]
        send_sems, recv_sems = refs[2 * n:]
        x, y, c = _place()
        sibling = (x, y, 1 - c)
        cps = []
        for w in range(n):
            h = ins[w].shape[0]
            cp = pltpu.make_async_remote_copy(src_ref=ins[w], dst_ref=outs[w].at[pl.ds(c * h, h)], send_sem=send_sems.at[w],
                                              recv_sem=recv_sems.at[w], device_id=sibling, device_id_type=MESH)
            cp.start()
            cps.append(cp)
        for cp in cps:
            cp.wait()

    joined = pl.pallas_call(
        body, name="join_halves",
        out_shape=[SDS((2 * h.shape[0], h.shape[1]), h.dtype) for h in halves],
        in_specs=[_any()] * n, out_specs=[_any()] * n,
        scratch_shapes=[pltpu.SemaphoreType.DMA((n,)), pltpu.SemaphoreType.DMA((n,))],
    )(*halves)
    c = lax.axis_index("c")
    return [lax.dynamic_update_slice(j, h, (c * h.shape[0], 0)) for j, h in zip(joined, halves)]


def _pad_rows(a, rows):
    return jnp.pad(a, ((0, rows - a.shape[0]),) + ((0, 0),) * (a.ndim - 1))


def _pack_small(dmod, g1, g2, gf, b_in, ln_g, ln_b, conv_b, gn_g, gn_b, ga, gb, sb, cw32, sw, loss_row):
    v512 = jnp.concatenate([ln_g, ln_b, conv_b, gn_g, gn_b, ga, gb, jnp.zeros((1, DA), F32)], axis=1).reshape(4, D)
    rows = [dmod.reshape(8, D), g1, g2, gf, b_in.reshape(2, D), v512, sb.reshape(1, D), cw32.reshape(16, D),
            sw.reshape(CHUNK, D), loss_row]
    packed = jnp.concatenate(rows, axis=0)
    return _pad_rows(packed, PK_ROWS)


def _unpack_small(p):
    v512 = p[PK_V512:PK_V512 + 4].reshape(1, 8 * DA)
    pieces = [v512[:, k * DA:(k + 1) * DA] for k in range(7)]
    return dict(
        dmod=p[PK_DMOD:PK_DMOD + 8].reshape(1, 8 * D), norm1_g=p[PK_G1:PK_G1 + 1], norm2_g=p[PK_G2:PK_G2 + 1],
        norm_f_g=p[PK_GF:PK_GF + 1], b_in=p[PK_BIN:PK_BIN + 2].reshape(1, 2 * D),
        a_ln_g=pieces[0], a_ln_b=pieces[1], b_conv_b=pieces[2], b_gn_g=pieces[3], b_gn_b=pieces[4],
        out_norm_a_g=pieces[5], out_norm_b_g=pieces[6],
        a_spatial_b=p[PK_SB:PK_SB + 1].reshape(N_HEADS, CHUNK),
        b_conv_w=p[PK_CW:PK_CW + 16].reshape(HALO, DB),
        a_spatial_w=p[PK_SW:PK_SW + CHUNK].reshape(N_HEADS, CHUNK, CHUNK))


def kernel(x, c, ada_w, ada_b, norm1_g, w_in, b_in, a_ln_g, a_ln_b, a_spatial_w, a_spatial_b, b_conv_w, b_conv_b, b_gn_g, b_gn_b, out_norm_a_g, out_norm_b_g, w_out, norm2_g, w_ffn_in, w_ffn_out, ada_f_w, ada_f_b, norm_f_g, loss_target, m_ada_w, m_ada_b, m_norm1_g, m_w_in, m_b_in, m_a_ln_g, m_a_ln_b, m_a_spatial_w, m_a_spatial_b, m_b_conv_w, m_b_conv_b, m_b_gn_g, m_b_gn_b, m_out_norm_a_g, m_out_norm_b_g, m_w_out, m_norm2_g, m_w_ffn_in, m_w_ffn_out, m_ada_f_w, m_ada_f_b, m_norm_f_g, v_ada_w, v_ada_b, v_norm1_g, v_w_in, v_b_in, v_a_ln_g, v_a_ln_b, v_a_spatial_w, v_a_spatial_b, v_b_conv_w, v_b_conv_b, v_b_gn_g, v_b_gn_b, v_out_norm_a_g, v_out_norm_b_g, v_w_out, v_norm2_g, v_w_ffn_in, v_w_ffn_out, v_ada_f_w, v_ada_f_b, v_norm_f_g):
    mx, my, mc = _place()
    me = 4 * mx + 2 * my + mc
    myq = 2 * mx + my
    xs = x[0]
    target = loss_target[0]
    s = xs.shape[0]
    n_ada = ada_w.shape[2]

    cw_shard = _pad_rows(b_conv_w[0], HALO)
    mix_shards = [w_in[0].astype(MM_DTYPE), w_out[0].astype(MM_DTYPE)]
    ffn_shards = [w_ffn_in[0].astype(MM_DTYPE), w_ffn_out[0].astype(MM_DTYPE)]
    w_in4, w_out4, first = _gather_weights(mix_shards, jnp.concatenate([c.reshape(8, LANES), cw_shard], axis=0))
    w_in4, w_out4 = _own_piece(w_in4, mix_shards[0]), _own_piece(w_out4, mix_shards[1])
    w_out_f = w_out4.reshape(D, D)
    c_all = first[:, 0:8, :].reshape(8, D)
    conv_w = jnp.concatenate([first[4 * (q // 2) + 2 * (q % 2), 8:8 + HALO, :] for q in range(NQ)], axis=1)
    cond_part = jnp.concatenate(_cond_partial(c_all, ada_w[0], ada_f_w), axis=1)
    cond_all = _all_gather8("gather_cond", cond_part)
    cond_q = [cond_all[4 * (q // 2) + 2 * (q % 2)] for q in range(NQ)]
    mod_all = jnp.concatenate([cq[:, :n_ada] for cq in cond_q] + [cq[:, n_ada:] for cq in cond_q], axis=1)
    mod = lax.dynamic_slice_in_dim(mod_all, me, 1, axis=0)
    mods = jnp.concatenate([ada_b, ada_f_b.reshape(1, 2 * D)], axis=1)

    causal = jnp.tril(jnp.ones((CHUNK, CHUNK), dtype=bool))
    wm_f = jnp.where(causal[None], a_spatial_w[0], 0.0)
    wm = wm_f.astype(MM_DTYPE)
    wmt = jnp.swapaxes(wm_f, 1, 2).astype(MM_DTYPE)
    bst = jnp.repeat(a_spatial_b[0].T, HALF, axis=1)

    z, x1, yb1, y, w_ffn_in4, w_ffn_out4 = _mixer_fwd(
        xs, mod, mods, norm1_g, w_in4, b_in, a_ln_g, a_ln_b, wm, bst, conv_w, b_conv_b, b_gn_g, b_gn_b,
        out_norm_a_g, out_norm_b_g, w_out_f, ffn_shards)
    w_ffn_out_f = w_ffn_out4.reshape(DFF, D)
    g, up, h2, dx2, acc_f = _ffn_fwd(x1, target, mod, mods, norm2_g, norm_f_g, w_ffn_in4, w_ffn_out_f)

    dff, a_act, dx1, acc_2 = _ffn_bwd(dx2, x1, g, up, mod, mods, norm2_g, w_ffn_in4, w_ffn_out_f)
    (gw_ffn_in4,) = _grad_matmul("grad_w_ffn_in", h2, dff, D, PW_FF, piece_w=PW_FF)
    modv = mod + mods
    gw_ffn_out, dgate2 = _grad_matmul("grad_w_ffn_out", a_act, dx2, PW_FF, D, gated=(modv[:, 5 * D:6 * D], w_ffn_out_f))
    gw_out, dgate1 = _grad_matmul("grad_w_out", y, dx1, D, D, gated=(modv[:, 2 * D:3 * D], w_out_f))
    early_partials = [gw_ffn_in4, gw_ffn_out.reshape(NQ, DFF // NQ, D), gw_out.reshape(NQ, D // NQ, D)]
    gx, dz, h, acc_1, acc_bin, acc_5, acc_cw, acc_sw, acc_sb, *early_arrived = _mixer_bwd(
        dx1, xs, z, yb1, mod, mods, norm1_g, w_in4, a_ln_g, a_ln_b, wm, wmt, bst, conv_w, b_gn_g, b_gn_b,
        out_norm_a_g, out_norm_b_g, w_out_f, early_partials)

    dmod = jnp.concatenate([acc_1[0:1], acc_1[1:2], dgate1[0:1], acc_2[0:1], acc_2[1:2], dgate2[0:1],
                            acc_f[0:1], acc_f[1:2]], axis=1)
    sw_grad = jnp.where(causal[None], acc_sw, 0.0)
    sb_grad = acc_sb[:, ::HALF].T
    packed = _pack_small(dmod, acc_1[2:3], acc_2[2:3], acc_f[2:3], acc_bin[0:1], acc_5[2:3], acc_5[3:4], acc_5[6:7],
                         acc_5[4:5], acc_5[5:6], acc_5[0:1], acc_5[1:2], sb_grad, acc_cw, sw_grad, acc_f[4:5])
    gw_in4, gathered = _grad_matmul("grad_w_in", h, dz, D, NQ * PW_IN, piece_w=PW_IN, gather_blk=packed)
    late_arrived = _scatter_to_owners([gw_in4])
    g_w_in, g_w_ffn_in, g_w_ffn_out, g_w_out = _join_halves(
        _owner_sums("late", late_arrived, [gw_in4]) + _owner_sums("early", early_arrived))
    summed = _ordered_sum("small_grad_sum", gathered)
    loss = summed[PK_LOSS, 0]
    small = _unpack_small(summed)
    dmod_all = gathered[:, PK_DMOD:PK_DMOD + 8, :].reshape(8, 8 * D)
    g_ada_w, g_ada_f_w = _cond_grad(c_all, lax.dynamic_slice_in_dim(dmod_all, myq * n_ada, n_ada, axis=1),
                                    lax.dynamic_slice_in_dim(dmod_all, 6 * D + myq * PW_IN, PW_IN, axis=1))

    grads = dict(
        ada_w=g_ada_w, ada_b=small["dmod"][:, :6 * D], norm1_g=small["norm1_g"], w_in=g_w_in,
        b_in=small["b_in"], a_ln_g=small["a_ln_g"], a_ln_b=small["a_ln_b"], a_spatial_w=small["a_spatial_w"],
        a_spatial_b=small["a_spatial_b"],
        b_conv_w=lax.dynamic_slice_in_dim(small["b_conv_w"], myq * LANES, LANES, axis=1)[:CONV_W],
        b_conv_b=small["b_conv_b"], b_gn_g=small["b_gn_g"], b_gn_b=small["b_gn_b"],
        out_norm_a_g=small["out_norm_a_g"], out_norm_b_g=small["out_norm_b_g"], w_out=g_w_out,
        norm2_g=small["norm2_g"], w_ffn_in=g_w_ffn_in, w_ffn_out=g_w_ffn_out, ada_f_w=g_ada_f_w,
        ada_f_b=small["dmod"][:, 6 * D:], norm_f_g=small["norm_f_g"])

    weights = dict(ada_w=ada_w, ada_b=ada_b, norm1_g=norm1_g, w_in=w_in, b_in=b_in, a_ln_g=a_ln_g, a_ln_b=a_ln_b,
                   a_spatial_w=a_spatial_w, a_spatial_b=a_spatial_b, b_conv_w=b_conv_w, b_conv_b=b_conv_b, b_gn_g=b_gn_g,
                   b_gn_b=b_gn_b, out_norm_a_g=out_norm_a_g, out_norm_b_g=out_norm_b_g, w_out=w_out, norm2_g=norm2_g,
                   w_ffn_in=w_ffn_in, w_ffn_out=w_ffn_out, ada_f_w=ada_f_w, ada_f_b=ada_f_b, norm_f_g=norm_f_g)
    m_in = dict(ada_w=m_ada_w, ada_b=m_ada_b, norm1_g=m_norm1_g, w_in=m_w_in, b_in=m_b_in, a_ln_g=m_a_ln_g, a_ln_b=m_a_ln_b,
                a_spatial_w=m_a_spatial_w, a_spatial_b=m_a_spatial_b, b_conv_w=m_b_conv_w, b_conv_b=m_b_conv_b,
                b_gn_g=m_b_gn_g, b_gn_b=m_b_gn_b, out_norm_a_g=m_out_norm_a_g, out_norm_b_g=m_out_norm_b_g, w_out=m_w_out,
                norm2_g=m_norm2_g, w_ffn_in=m_w_ffn_in, w_ffn_out=m_w_ffn_out, ada_f_w=m_ada_f_w, ada_f_b=m_ada_f_b,
                norm_f_g=m_norm_f_g)
    v_in = dict(ada_w=v_ada_w, ada_b=v_ada_b, norm1_g=v_norm1_g, w_in=v_w_in, b_in=v_b_in, a_ln_g=v_a_ln_g, a_ln_b=v_a_ln_b,
                a_spatial_w=v_a_spatial_w, a_spatial_b=v_a_spatial_b, b_conv_w=v_b_conv_w, b_conv_b=v_b_conv_b,
                b_gn_g=v_b_gn_g, b_gn_b=v_b_gn_b, out_norm_a_g=v_out_norm_a_g, out_norm_b_g=v_out_norm_b_g, w_out=v_w_out,
                norm2_g=v_norm2_g, w_ffn_in=v_w_ffn_in, w_ffn_out=v_w_ffn_out, ada_f_w=v_ada_f_w, ada_f_b=v_ada_f_b,
                norm_f_g=v_norm_f_g)
    names = list(weights)
    big = ("ada_w", "w_in", "w_out", "w_ffn_in", "w_ffn_out", "ada_f_w")

    def flat2(a):
        return a.reshape(-1, a.shape[-1])

    delta, new_m, new_v = {}, {}, {}
    for nm in big:
        shape = weights[nm].shape
        grads[nm] = grads[nm].reshape(shape)
        d_, m_, v_ = _adamw("adamw_" + nm, flat2(weights[nm]), flat2(grads[nm]), flat2(m_in[nm]), flat2(v_in[nm]))
        delta[nm], new_m[nm], new_v[nm] = d_.reshape(shape), m_.reshape(shape), v_.reshape(shape)

    small_names = [nm for nm in names if nm not in big]
    for nm in small_names:
        grads[nm] = grads[nm].reshape(weights[nm].shape)
    small_out = _adamw_many("adamw_small", *[[flat2(tree[nm]) for nm in small_names] for tree in (weights, grads, m_in, v_in)])
    for k, nm in enumerate(small_names):
        shape = weights[nm].shape
        delta[nm], new_m[nm], new_v[nm] = [o.reshape(shape) for o in small_out[3 * k:3 * k + 3]]

    grad_x = gx.reshape(x.shape)
    return (loss, grad_x, *[grads[nm] for nm in names], *[delta[nm] for nm in names],
            *[new_m[nm] for nm in names], *[new_v[nm] for nm in names])
```

```python
import functools
import math

import jax
import jax.numpy as jnp
from jax import lax
from jax.experimental import pallas as pl
from jax.experimental.pallas import tpu as pltpu

F32 = jnp.float32
MM_DTYPE = jnp.bfloat16
WIRE_DTYPE = jnp.bfloat16
SDS = jax.ShapeDtypeStruct
MESH = pl.DeviceIdType.MESH

D = 1024
DA = 512
DB = 512
NQ = 4
PW_IN = 512
DFF = 2816
PW_FF = 1408
CHUNK = 128
N_HEADS = 8
CONV_W = 31
HALO = 32
CONV_ROWS = 64
EPS = 1e-6
LANES = 128
HALF = 64

ROW_TILE = 256
FWD_ROW_TILE = 512
FFN_ROW_TILE = 256
GRAD_ROW_TILE = 2048
VMEM_LIMIT = 60 * 1024 * 1024

ADAM_LR = 0.001
ADAM_B1 = 0.9
ADAM_B2 = 0.999
ADAM_EPS = 1e-08
ADAM_WD = 0.01
ADAM_STEP = 10

PK_DMOD = 0
PK_G1 = 8
PK_G2 = 9
PK_GF = 10
PK_BIN = 11
PK_V512 = 13
PK_SB = 17
PK_CW = 18
PK_SW = 34
PK_LOSS = 162
PK_ROWS = 168


def _dot(a, b):
    return jnp.dot(a, b, preferred_element_type=F32)


def _dot_nt(a, b):
    return lax.dot_general(a, b, (((1,), (1,)), ((), ())), preferred_element_type=F32)


def _dot_tn(a, b):
    return lax.dot_general(a, b, (((0,), (0,)), ((), ())), preferred_element_type=F32)


def _rowsum(x):
    return jnp.sum(x, axis=-1, keepdims=True)


def _colsum(x):
    return jnp.sum(x, axis=0, keepdims=True)


def _group_sum(x):
    rows, width = x.shape
    lo_mask = lax.broadcasted_iota(jnp.int32, (rows, LANES), 1) < HALF
    outs = []
    for jb in range(width // LANES):
        xb = x[:, jb * LANES:(jb + 1) * LANES]
        lo = _rowsum(jnp.where(lo_mask, xb, 0.0))
        hi = _rowsum(jnp.where(lo_mask, 0.0, xb))
        outs.append(jnp.where(lo_mask, lo, hi))
    return jnp.concatenate(outs, axis=-1)


def _sigmoid(x):
    return 1.0 / (1.0 + jnp.exp(-x))


def _gelu_parts(u):
    cdf = 0.5 * (1.0 + lax.erf(u * (1.0 / math.sqrt(2.0))))
    pdf = jnp.exp(-0.5 * u * u) * (1.0 / math.sqrt(2.0 * math.pi))
    return u * cdf, cdf + u * pdf


def _whole_vmem():
    return pl.BlockSpec(memory_space=pltpu.VMEM)


def _params(*semantics):
    return pltpu.CompilerParams(dimension_semantics=semantics, vmem_limit_bytes=VMEM_LIMIT)


def _mod_rows(mod_ref, modb_ref, first, count):
    m = mod_ref[...] + modb_ref[...]
    return [m[:, (first + k) * D:(first + k + 1) * D] for k in range(count)]


def _mixer_recompute(z_parts, lng, lnb, wm_ref, bst_ref, mix_ref):
    u, v, val, gate = z_parts
    rows = u.shape[0]
    gu, dgu = _gelu_parts(u)
    gv, dgv = _gelu_parts(v)
    mu = _rowsum(gv) * (1.0 / DA)
    vc = gv - mu
    rs = lax.rsqrt(_rowsum(vc * vc) * (1.0 / DA) + EPS)
    vhat = vc * rs
    vl = vhat * lng + lnb
    vlb = vl.astype(MM_DTYPE)
    lo_mask = lax.broadcasted_iota(jnp.int32, (CHUNK, LANES), 1) < HALF
    for ck in range(rows // CHUNK):
        for jb in range(DA // LANES):
            blk = vlb[ck * CHUNK:(ck + 1) * CHUNK, jb * LANES:(jb + 1) * LANES]
            a = _dot(wm_ref[2 * jb], blk)
            b = _dot(wm_ref[2 * jb + 1], blk)
            mix_ref[ck * CHUNK:(ck + 1) * CHUNK, jb * LANES:(jb + 1) * LANES] = (
                jnp.where(lo_mask, a, b) + bst_ref[:, jb * LANES:(jb + 1) * LANES])
    mixed = mix_ref[...]
    sg = _sigmoid(gate)
    yb0 = val * sg
    return dict(gu=gu, dgu=dgu, dgv=dgv, rs=rs, vhat=vhat, vlb=vlb, mixed=mixed, sg=sg, yb0=yb0)


def _conv_branch_tail(yb1, gng, gnb):
    gm = _group_sum(yb1) * (1.0 / HALF)
    gc = yb1 - gm
    grs = lax.rsqrt(_group_sum(gc * gc) * (1.0 / HALF) + EPS)
    ghat = gc * grs
    yb2 = ghat * gng + gnb
    s2 = _sigmoid(yb2)
    return dict(grs=grs, ghat=ghat, yb2=yb2, s2=s2, y_b=yb2 * s2)


def _shifted_copies(e_ref, sh_ref):
    n = sh_ref.shape[1]
    for b in range(1, 8):
        sh_ref[b - 1] = e_ref[pl.ds(b, n), :]


def _window(e_ref, sh_ref, offset, r0, nrows, cols):
    a, b = divmod(offset, 8)
    if b == 0:
        return e_ref[pl.ds(r0 + 8 * a, nrows), cols]
    return sh_ref[b - 1, pl.ds(r0 + 8 * a, nrows), cols]


def _conv_taps(e_ref, sh_ref, cw_ref, out_ref, ts, first_offset, flip, bias_ref=None, other_ref=None, tap_acc_ref=None):
    groups = CONV_ROWS // 8
    for cb in range(DB // LANES):
        cols = slice(cb * LANES, (cb + 1) * LANES)
        tap_acc = [jnp.zeros((8, LANES), F32) for _ in range(CONV_W)]
        for rb in range(ts // CONV_ROWS):
            r0 = rb * CONV_ROWS
            acc = jnp.zeros((CONV_ROWS, LANES), F32)
            if bias_ref is not None:
                acc = acc + bias_ref[:, cols]
            if other_ref is not None:
                other = other_ref[r0:r0 + CONV_ROWS, cols]
            for j in range(CONV_W):
                k = CONV_W - 1 - j if flip else j
                win = _window(e_ref, sh_ref, first_offset + j, r0, CONV_ROWS, cols)
                acc = acc + win * cw_ref[k:k + 1, cols]
                if other_ref is not None:
                    tap_acc[k] = tap_acc[k] + jnp.sum((other * win).reshape(groups, 8, LANES), axis=0)
            out_ref[r0:r0 + CONV_ROWS, cols] = acc
        if other_ref is not None:
            for k in range(CONV_W):
                tap_acc_ref[k:k + 1, cols] += _colsum(tap_acc[k])


def _gather_plan(ins, outs, send_sems, recv_sems, local_sems=None):
    x, y, c = _place()
    sibling = (x, y, 1 - c)
    chips = _other_chips(x, y)
    myq = 2 * x + y

    def copy(w, k, q, hc, to, src=None):
        rows = ins[w].shape[0]
        dst = outs[w].at[q, pl.ds(hc * (rows // 2), rows // 2)]
        return pltpu.make_async_remote_copy(
            src_ref=dst if src is None else src, dst_ref=dst,
            send_sem=send_sems.at[w, k], recv_sem=recv_sems.at[w, k], device_id=to, device_id_type=MESH)

    def own(w):
        return pltpu.make_async_copy(ins[w], outs[w].at[myq], local_sems.at[w])

    def send():
        for w in range(len(ins)):
            rows = ins[w].shape[0]
            src = ins[w].at[pl.ds(c * (rows // 2), rows // 2)]
            for j, chip in enumerate(chips):
                copy(w, j, myq, c, (*chip, c), src=src).start()
            if local_sems is not None:
                own(w).start()

    def forward():
        for w in range(len(ins)):
            for j, (qx, qy) in enumerate(chips):
                copy(w, j, 2 * qx + qy, c, sibling).wait_recv()
                copy(w, 3 + j, 2 * qx + qy, c, sibling).start()

    def finish():
        for w in range(len(ins)):
            for j, (qx, qy) in enumerate(chips):
                copy(w, 3 + j, 2 * qx + qy, 1 - c, sibling).wait_recv()
        for w in range(len(ins)):
            for k, (qx, qy) in enumerate(chips + chips):
                copy(w, k, 2 * qx + qy, c, sibling).wait_send()
            if local_sems is not None:
                own(w).wait()

    return send, forward, finish


def _mixer_fwd(x, mod, mods, norm1_g, w_in4, b_in, ln_g, ln_b, wm, bst, conv_w, conv_b, gn_g, gn_b, ga, gb, w_out,
               ffn_shards):
    s = x.shape[0]
    ts = min(FWD_ROW_TILE, s)
    nt = s // ts
    n_sh = len(ffn_shards)

    def body(x_ref, mod_ref, modb_ref, g1_ref, w4_ref, bin_ref, lng_ref, lnb_ref, wm_ref, bst_ref, cw_ref, cb_ref,
             gng_ref, gnb_ref, ga_ref, gb_ref, wout_ref, *rest):
        shard_refs, rest = rest[:n_sh], rest[n_sh:]
        z_ref, x1_ref, yb1_ref, y_ref = rest[:4]
        full_refs, rest = rest[4:4 + n_sh], rest[4 + n_sh:]
        e_ref, sh_ref, mix_ref, send_sems, recv_sems, local_sems = rest
        i = pl.program_id(0)
        send, forward, finish = _gather_plan(shard_refs, full_refs, send_sems, recv_sems, local_sems)

        @pl.when(i == 0)
        def _():
            send()
            e_ref[0:HALO, :] = jnp.zeros((HALO, DB), F32)

        @pl.when(i == (3 * nt) // 4)
        def _():
            forward()

        shift1, scale1, gate1 = _mod_rows(mod_ref, modb_ref, 0, 3)
        x_t = x_ref[...]
        r1 = lax.rsqrt(_rowsum(x_t * x_t) * (1.0 / D) + EPS)
        h = (x_t * r1 * g1_ref[...]) * (1.0 + scale1) + shift1
        hb = h.astype(MM_DTYPE)
        z_parts = []
        for q in range(NQ):
            zq = _dot(hb, w4_ref[q]) + bin_ref[:, q * PW_IN:(q + 1) * PW_IN]
            z_ref[:, q * PW_IN:(q + 1) * PW_IN] = zq
            z_parts.append(zq)
        r = _mixer_recompute(z_parts, lng_ref[...], lnb_ref[...], wm_ref, bst_ref, mix_ref)
        y_a = r["gu"] * r["mixed"]
        e_ref[HALO:HALO + ts, :] = r["yb0"]
        _shifted_copies(e_ref, sh_ref)
        _conv_taps(e_ref, sh_ref, cw_ref, yb1_ref, ts, HALO - (CONV_W - 1), False, bias_ref=cb_ref)
        e_ref[0:HALO, :] = e_ref[ts:ts + HALO, :]
        t = _conv_branch_tail(yb1_ref[...], gng_ref[...], gnb_ref[...])
        ra = lax.rsqrt(_rowsum(y_a * y_a) * (1.0 / DA) + EPS)
        rb = lax.rsqrt(_rowsum(t["y_b"] * t["y_b"]) * (1.0 / DB) + EPS)
        yan = (y_a * ra * ga_ref[...]).astype(MM_DTYPE)
        ybn = (t["y_b"] * rb * gb_ref[...]).astype(MM_DTYPE)
        y_ref[:, 0:DA] = yan
        y_ref[:, DA:D] = ybn
        o1 = _dot(yan, wout_ref[0:DA, :]) + _dot(ybn, wout_ref[DA:D, :])
        x1_ref[...] = x_t + gate1 * o1

        @pl.when(i == nt - 1)
        def _():
            finish()

    row = lambda w: pl.BlockSpec((ts, w), lambda i: (i, 0))
    full = lambda a: pl.BlockSpec(a.shape, lambda i: (0,) * a.ndim)
    return pl.pallas_call(
        body, name="mixer_fwd", grid=(nt,),
        in_specs=[row(D), full(mod), full(mods), full(norm1_g), _whole_vmem(), full(b_in), full(ln_g), full(ln_b),
                  _whole_vmem(), full(bst), full(conv_w), full(conv_b), full(gn_g), full(gn_b), full(ga), full(gb),
                  _whole_vmem()] + [_any()] * n_sh,
        out_specs=[row(4 * PW_IN), row(D), row(DB), row(D)] + [_any()] * n_sh,
        out_shape=[SDS((s, 4 * PW_IN), F32), SDS((s, D), F32), SDS((s, DB), F32), SDS((s, D), MM_DTYPE)]
        + [SDS((NQ,) + w.shape, w.dtype) for w in ffn_shards],
        scratch_shapes=[pltpu.VMEM((ts + HALO, DB), F32), pltpu.VMEM((7, ts + HALO - 8, DB), F32), pltpu.VMEM((ts, DA), F32),
                        pltpu.SemaphoreType.DMA((n_sh, 6)), pltpu.SemaphoreType.DMA((n_sh, 6)),
                        pltpu.SemaphoreType.DMA((n_sh,))],
        compiler_params=_params("arbitrary"),
    )(x, mod, mods, norm1_g, w_in4, b_in, ln_g, ln_b, wm, bst, conv_w, conv_b, gn_g, gn_b, ga, gb, w_out, *ffn_shards)


def _ffn_fwd(x1, target, mod, mods, norm2_g, norm_f_g, w_ffn_in4, w_ffn_out):
    s = x1.shape[0]
    sub_rows = min(FFN_ROW_TILE, s)
    ts = min(2 * sub_rows, s)
    nt = s // ts

    def body(x1_ref, tgt_ref, mod_ref, modb_ref, g2_ref, gf_ref, wf_ref, wo_ref,
             g_ref, up_ref, h2_ref, dx2_ref, acc_ref):
        i = pl.program_id(0)

        @pl.when(i == 0)
        def _():
            acc_ref[...] = jnp.zeros(acc_ref.shape, F32)

        shift2, scale2, gate2, shift_f, scale_f = _mod_rows(mod_ref, modb_ref, 3, 5)
        for sub in range(ts // sub_rows):
            rows = slice(sub * sub_rows, (sub + 1) * sub_rows)
            x1_t = x1_ref[rows, :]
            r2 = lax.rsqrt(_rowsum(x1_t * x1_t) * (1.0 / D) + EPS)
            h2 = (x1_t * r2 * g2_ref[...]) * (1.0 + scale2) + shift2
            h2b = h2.astype(MM_DTYPE)
            h2_ref[rows, :] = h2b
            o2 = jnp.zeros((sub_rows, D), F32)
            for p in range(2):
                g = _dot(h2b, wf_ref[p])
                up = _dot(h2b, wf_ref[2 + p])
                g_ref[rows, p * PW_FF:(p + 1) * PW_FF] = g.astype(MM_DTYPE)
                up_ref[rows, p * PW_FF:(p + 1) * PW_FF] = up.astype(MM_DTYPE)
                a = (g * _sigmoid(g) * up).astype(MM_DTYPE)
                o2 = o2 + _dot(a, wo_ref[p * PW_FF:(p + 1) * PW_FF, :])
            x2 = x1_t + gate2 * o2
            rf = lax.rsqrt(_rowsum(x2 * x2) * (1.0 / D) + EPS)
            gf = gf_ref[...]
            nf = x2 * rf * gf
            err = nf * (1.0 + scale_f) + shift_f - tgt_ref[rows, :]
            d_out = err * (1.0 / D)
            d_nf = d_out * (1.0 + scale_f)
            t = d_nf * gf
            dx2_ref[rows, :] = rf * t - x2 * (rf * rf * rf) * (_rowsum(t * x2) * (1.0 / D))
            acc_ref[0:1, :] += _colsum(d_out)
            acc_ref[1:2, :] += _colsum(d_out * nf)
            acc_ref[2:3, :] += _colsum(d_nf * x2 * rf)
            acc_ref[3:4, :] += _colsum(err * err)

        @pl.when(i == nt - 1)
        def _():
            acc_ref[4:5, :] = jnp.zeros((1, D), F32) + _rowsum(acc_ref[3:4, :]) * (0.5 / D)

    row = lambda w: pl.BlockSpec((ts, w), lambda i: (i, 0))
    full = lambda a: pl.BlockSpec(a.shape, lambda i: (0,) * a.ndim)
    return pl.pallas_call(
        body, name="ffn_fwd", grid=(nt,),
        in_specs=[row(D), row(D), full(mod), full(mods), full(norm2_g), full(norm_f_g), _whole_vmem(), _whole_vmem()],
        out_specs=[row(DFF), row(DFF), row(D), row(D), pl.BlockSpec((8, D), lambda i: (0, 0))],
        out_shape=[SDS((s, DFF), MM_DTYPE), SDS((s, DFF), MM_DTYPE), SDS((s, D), MM_DTYPE), SDS((s, D), F32),
                   SDS((8, D), F32)],
        compiler_params=_params("arbitrary"),
    )(x1, target, mod, mods, norm2_g, norm_f_g, w_ffn_in4, w_ffn_out)


def _ffn_bwd(dx2, x1, g, up, mod, mods, norm2_g, w_ffn_in4, w_ffn_out):
    s = x1.shape[0]
    ts = min(FFN_ROW_TILE, s)
    nt = s // ts

    def body(dx2_ref, x1_ref, g_ref, up_ref, mod_ref, modb_ref, g2_ref, wf_ref, wo_ref,
             dff_ref, a_ref, dx1_ref, acc_ref):
        @pl.when(pl.program_id(0) == 0)
        def _():
            acc_ref[...] = jnp.zeros(acc_ref.shape, F32)

        shift2, scale2, gate2 = _mod_rows(mod_ref, modb_ref, 3, 3)
        dx2_t = dx2_ref[...]
        do2 = (dx2_t * gate2).astype(MM_DTYPE)
        dh2 = jnp.zeros((ts, D), F32)
        for p in range(2):
            da = _dot_nt(do2, wo_ref[p * PW_FF:(p + 1) * PW_FF, :])
            gp = g_ref[:, p * PW_FF:(p + 1) * PW_FF].astype(F32)
            upp = up_ref[:, p * PW_FF:(p + 1) * PW_FF].astype(F32)
            sg = _sigmoid(gp)
            silu = gp * sg
            a_ref[:, p * PW_FF:(p + 1) * PW_FF] = (silu * upp).astype(MM_DTYPE)
            dg = (da * upp * (sg * (1.0 + gp * (1.0 - sg)))).astype(MM_DTYPE)
            dup = (da * silu).astype(MM_DTYPE)
            dff_ref[:, p * PW_FF:(p + 1) * PW_FF] = dg
            dff_ref[:, DFF + p * PW_FF:DFF + (p + 1) * PW_FF] = dup
            dh2 = dh2 + _dot_nt(dg, wf_ref[p]) + _dot_nt(dup, wf_ref[2 + p])
        x1_t = x1_ref[...]
        r2 = lax.rsqrt(_rowsum(x1_t * x1_t) * (1.0 / D) + EPS)
        g2 = g2_ref[...]
        xr = x1_t * r2
        dn2 = dh2 * (1.0 + scale2)
        t = dn2 * g2
        dx1_ref[...] = dx2_t + r2 * t - x1_t * (r2 * r2 * r2) * (_rowsum(t * x1_t) * (1.0 / D))
        acc_ref[0:1, :] += _colsum(dh2)
        acc_ref[1:2, :] += _colsum(dh2 * (xr * g2))
        acc_ref[2:3, :] += _colsum(dn2 * xr)

    row = lambda w: pl.BlockSpec((ts, w), lambda i: (i, 0))
    full = lambda a: pl.BlockSpec(a.shape, lambda i: (0,) * a.ndim)
    return pl.pallas_call(
        body, name="ffn_bwd", grid=(nt,),
        in_specs=[row(D), row(D), row(DFF), row(DFF), full(mod), full(mods), full(norm2_g), _whole_vmem(), _whole_vmem()],
        out_specs=[row(2 * DFF), row(DFF), row(D), pl.BlockSpec((8, D), lambda i: (0, 0))],
        out_shape=[SDS((s, 2 * DFF), MM_DTYPE), SDS((s, DFF), MM_DTYPE), SDS((s, D), F32), SDS((8, D), F32)],
        compiler_params=_params("arbitrary"),
    )(dx2, x1, g, up, mod, mods, norm2_g, w_ffn_in4, w_ffn_out)


def _scatter_plan(ins, outs, send_sems, recv_sems, local_sems=None):
    x, y, c = _place()
    me = 4 * x + 2 * y + c

    def copies():
        cps = []
        for w in range(len(ins)):
            h = ins[w].shape[1] // 2
            for k in range(1, 8):
                px, py, pc = (1 - x if k & 4 else x), (1 - y if k & 2 else y), (1 - c if k & 1 else c)
                cps.append(pltpu.make_async_remote_copy(
                    src_ref=ins[w].at[2 * px + py, pl.ds(pc * h, h)], dst_ref=outs[w].at[me],
                    send_sem=send_sems.at[w, k - 1], recv_sem=recv_sems.at[w, k - 1],
                    device_id=(px, py, pc), device_id_type=MESH))
        return cps

    def own():
        if local_sems is None:
            return []
        return [pltpu.make_async_copy(ins[w].at[2 * x + y, pl.ds(c * (ins[w].shape[1] // 2), ins[w].shape[1] // 2)],
                                      outs[w].at[me], local_sems.at[w]) for w in range(len(ins))]

    def send():
        for cp in copies() + own():
            cp.start()

    def finish():
        for cp in copies() + own():
            cp.wait()

    return send, finish


def _scattered_shape(partial):
    nq, rows, cols = partial.shape
    return SDS((8, rows // 2, cols), partial.dtype)


def _mixer_bwd(dx1, x, z, yb1, mod, mods, norm1_g, w_in4, ln_g, ln_b, wm, wmt, bst, conv_w, gn_g, gn_b, ga, gb, w_out,
               partials):
    s = x.shape[0]
    ts = min(ROW_TILE, s)
    nt = s // ts
    n_cs = len(partials)

    def body(dx1_ref, x_ref, z_ref, yb1_ref, mod_ref, modb_ref, g1_ref, w4_ref, lng_ref, lnb_ref, wm_ref, wmt_ref,
             bst_ref, cw_ref, gng_ref, gnb_ref, ga_ref, gb_ref, wout_ref, *rest):
        cs_refs, rest = rest[:n_cs], rest[n_cs:]
        gx_ref, dz_ref, h_ref, a1_ref, a2_ref, a5_ref, acw_ref, asw_ref, asb_ref = rest[:9]
        arrived_refs, rest = rest[9:9 + n_cs], rest[9 + n_cs:]
        e_ref, sh_ref, mix_ref, dvl_ref, send_sems, recv_sems, local_sems = rest
        i = pl.program_id(0)
        send, finish = _scatter_plan(cs_refs, arrived_refs, send_sems, recv_sems, local_sems)

        @pl.when(i == 0)
        def _():
            send()
            e_ref[ts:ts + HALO, :] = jnp.zeros((HALO, DB), F32)
            for r in (a1_ref, a2_ref, a5_ref, acw_ref, asw_ref, asb_ref):
                r[...] = jnp.zeros(r.shape, F32)

        shift1, scale1, gate1 = _mod_rows(mod_ref, modb_ref, 0, 3)
        dx1_t = dx1_ref[...]
        do1 = (dx1_t * gate1).astype(MM_DTYPE)
        d_yan = _dot_nt(do1, wout_ref[0:DA, :])
        d_ybn = _dot_nt(do1, wout_ref[DA:D, :])

        z_parts = [z_ref[:, q * PW_IN:(q + 1) * PW_IN] for q in range(NQ)]
        u, v, val, gate = z_parts
        lng = lng_ref[...]
        r = _mixer_recompute(z_parts, lng, lnb_ref[...], wm_ref, bst_ref, mix_ref)
        gng = gng_ref[...]
        t = _conv_branch_tail(yb1_ref[...], gng, gnb_ref[...])
        y_a = r["gu"] * r["mixed"]
        y_b = t["y_b"]
        ga_v, gb_v = ga_ref[...], gb_ref[...]
        ra = lax.rsqrt(_rowsum(y_a * y_a) * (1.0 / DA) + EPS)
        rb = lax.rsqrt(_rowsum(y_b * y_b) * (1.0 / DB) + EPS)

        a5_ref[0:1, :] += _colsum(d_yan * y_a * ra)
        a5_ref[1:2, :] += _colsum(d_ybn * y_b * rb)
        ta = d_yan * ga_v
        d_ya = ra * ta - y_a * (ra * ra * ra) * (_rowsum(ta * y_a) * (1.0 / DA))
        tb = d_ybn * gb_v
        d_yb = rb * tb - y_b * (rb * rb * rb) * (_rowsum(tb * y_b) * (1.0 / DB))

        d_u = d_ya * r["mixed"] * r["dgu"]
        d_mixed = d_ya * r["gu"]
        dmb = d_mixed.astype(MM_DTYPE)
        lo_mask = lax.broadcasted_iota(jnp.int32, (CHUNK, LANES), 1) < HALF
        zero_blk = jnp.zeros((CHUNK, LANES), MM_DTYPE)
        sb_acc = jnp.zeros((CHUNK, DA), F32)
        for ck in range(ts // CHUNK):
            rows = slice(ck * CHUNK, (ck + 1) * CHUNK)
            sb_acc = sb_acc + d_mixed[rows, :]
            for jb in range(DA // LANES):
                cols = slice(jb * LANES, (jb + 1) * LANES)
                dm_blk = dmb[rows, cols]
                vl_blk = r["vlb"][rows, cols]
                da_ = _dot(wmt_ref[2 * jb], dm_blk)
                db_ = _dot(wmt_ref[2 * jb + 1], dm_blk)
                dvl_ref[rows, cols] = jnp.where(lo_mask, da_, db_)
                asw_ref[2 * jb] += _dot_nt(jnp.where(lo_mask, dm_blk, zero_blk), vl_blk)
                asw_ref[2 * jb + 1] += _dot_nt(jnp.where(lo_mask, zero_blk, dm_blk), vl_blk)
        asb_ref[...] += sb_acc
        d_vl = dvl_ref[...]
        a5_ref[2:3, :] += _colsum(d_vl * r["vhat"])
        a5_ref[3:4, :] += _colsum(d_vl)
        dvh = d_vl * lng
        d_gv = r["rs"] * (dvh - _rowsum(dvh) * (1.0 / DA) - r["vhat"] * (_rowsum(dvh * r["vhat"]) * (1.0 / DA)))
        d_v = d_gv * r["dgv"]

        yb2, s2 = t["yb2"], t["s2"]
        d_yb2 = d_yb * (s2 * (1.0 + yb2 * (1.0 - s2)))
        a5_ref[4:5, :] += _colsum(d_yb2 * t["ghat"])
        a5_ref[5:6, :] += _colsum(d_yb2)
        dgh = d_yb2 * gng
        d_yb1 = t["grs"] * (dgh - _group_sum(dgh) * (1.0 / HALF) - t["ghat"] * (_group_sum(dgh * t["ghat"]) * (1.0 / HALF)))
        a5_ref[6:7, :] += _colsum(d_yb1)
        e_ref[0:ts, :] = d_yb1
        _shifted_copies(e_ref, sh_ref)
        mix_ref[...] = r["yb0"]
        _conv_taps(e_ref, sh_ref, cw_ref, dvl_ref, ts, 0, True, other_ref=mix_ref, tap_acc_ref=acw_ref)
        d_yb0 = dvl_ref[...]
        e_ref[ts:ts + HALO, :] = e_ref[0:HALO, :]
        sg = r["sg"]
        d_val = d_yb0 * sg
        d_gate = d_yb0 * val * sg * (1.0 - sg)

        dh = jnp.zeros((ts, D), F32)
        for q, dzq in enumerate((d_u, d_v, d_val, d_gate)):
            a2_ref[0:1, q * PW_IN:(q + 1) * PW_IN] += _colsum(dzq)
            dzb = dzq.astype(MM_DTYPE)
            dz_ref[:, q * PW_IN:(q + 1) * PW_IN] = dzb
            dh = dh + _dot_nt(dzb, w4_ref[q])
        x_t = x_ref[...]
        r1 = lax.rsqrt(_rowsum(x_t * x_t) * (1.0 / D) + EPS)
        g1 = g1_ref[...]
        xr = x_t * r1
        n1 = xr * g1
        h_ref[...] = (n1 * (1.0 + scale1) + shift1).astype(MM_DTYPE)
        dn1 = dh * (1.0 + scale1)
        t1 = dn1 * g1
        gx_ref[...] = dx1_t + r1 * t1 - x_t * (r1 * r1 * r1) * (_rowsum(t1 * x_t) * (1.0 / D))
        a1_ref[0:1, :] += _colsum(dh)
        a1_ref[1:2, :] += _colsum(dh * n1)
        a1_ref[2:3, :] += _colsum(dn1 * xr)

        @pl.when(i == nt - 1)
        def _():
            asb_ref[...] = _group_sum(asb_ref[...])
            finish()

    row = lambda w: pl.BlockSpec((ts, w), lambda i: (nt - 1 - i, 0))
    full = lambda a: pl.BlockSpec(a.shape, lambda i: (0,) * a.ndim)
    keep = lambda shape: pl.BlockSpec(shape, lambda i: (0,) * len(shape))
    return pl.pallas_call(
        body, name="mixer_bwd", grid=(nt,),
        in_specs=[row(D), row(D), row(4 * PW_IN), row(DB), full(mod), full(mods), full(norm1_g), _whole_vmem(),
                  full(ln_g), full(ln_b), _whole_vmem(), _whole_vmem(), full(bst), full(conv_w), full(gn_g), full(gn_b),
                  full(ga), full(gb), _whole_vmem()] + [_any()] * n_cs,
        out_specs=[row(D), row(4 * PW_IN), row(D), keep((8, D)), keep((8, 4 * PW_IN)), keep((8, DA)),
                   keep((HALO, DB)), keep((N_HEADS, CHUNK, CHUNK)), keep((CHUNK, DA))] + [_any()] * n_cs,
        out_shape=[SDS((s, D), F32), SDS((s, 4 * PW_IN), MM_DTYPE), SDS((s, D), MM_DTYPE), SDS((8, D), F32),
                   SDS((8, 4 * PW_IN), F32), SDS((8, DA), F32), SDS((HALO, DB), F32),
                   SDS((N_HEADS, CHUNK, CHUNK), F32), SDS((CHUNK, DA), F32)]
        + [_scattered_shape(p) for p in partials],
        scratch_shapes=[pltpu.VMEM((ts + HALO, DB), F32), pltpu.VMEM((7, ts + HALO - 8, DB), F32),
                        pltpu.VMEM((ts, DA), F32), pltpu.VMEM((ts, DA), F32),
                        pltpu.SemaphoreType.DMA((n_cs, 7)), pltpu.SemaphoreType.DMA((n_cs, 7)),
                        pltpu.SemaphoreType.DMA((n_cs,))],
        compiler_params=_params("arbitrary"),
    )(dx1, x, z, yb1, mod, mods, norm1_g, w_in4, ln_g, ln_b, wm, wmt, bst, conv_w, gn_g, gn_b, ga, gb, w_out, *partials)


def _gather8_plan(x_ref, out_ref, send_sems, recv_sems, local_sem):
    x, y, c = _place()
    me, sibling = (x, y, c), (x, y, 1 - c)
    chips = _other_chips(x, y)

    def copy(k, block, to, src=None):
        dst = out_ref.at[4 * block[0] + 2 * block[1] + block[2]]
        return pltpu.make_async_remote_copy(src_ref=dst if src is None else src, dst_ref=dst, send_sem=send_sems.at[k],
                                            recv_sem=recv_sems.at[k], device_id=to, device_id_type=MESH)

    def own():
        return pltpu.make_async_copy(x_ref, out_ref.at[4 * x + 2 * y + c], local_sem)

    def send():
        own().start()
        copy(0, me, sibling, src=x_ref).start()
        for j, chip in enumerate(chips):
            copy(1 + j, me, (*chip, c), src=x_ref).start()

    def forward():
        for j, chip in enumerate(chips):
            copy(1 + j, (*chip, c), me).wait_recv()
            copy(4 + j, (*chip, c), sibling).start()

    def finish():
        copy(0, sibling, me).wait_recv()
        for j, chip in enumerate(chips):
            copy(4 + j, (*chip, 1 - c), me).wait_recv()
        for k in range(7):
            copy(k, me, sibling).wait_send()
        own().wait()

    return send, forward, finish


def _grad_matmul(name, a, b, ka_tile, nb_tile, piece_w=None, gated=None, gather_blk=None):
    s, ka = a.shape
    nb = b.shape[1]
    ts = min(GRAD_ROW_TILE, s)
    nt = s // ts
    nja, njb = ka // ka_tile, nb // nb_tile
    steps = nja * njb * nt
    n_in = 2 + (2 if gated else 0) + (1 if gather_blk is not None else 0)
    n_out = 1 + (1 if gated else 0) + (1 if gather_blk is not None else 0)
    assert not (gated and njb != 1) and not (piece_w and nja != 1)

    def body(*refs):
        ins, outs, scratch = refs[:n_in], refs[n_in:n_in + n_out], refs[n_in + n_out:]
        a_ref, b_ref, o_ref, acc_ref = ins[0], ins[1], outs[0], scratch[0]
        step = (pl.program_id(0) * njb + pl.program_id(1)) * nt + pl.program_id(2)
        if gather_blk is not None:
            send, forward, finish = _gather8_plan(ins[-1], outs[-1], *scratch[1:])

            @pl.when(step == 0)
            def _():
                send()

            @pl.when(step == (3 * steps) // 4)
            def _():
                forward()

        prod = _dot_tn(a_ref[...].astype(MM_DTYPE), b_ref[...].astype(MM_DTYPE))

        @pl.when(pl.program_id(2) == 0)
        def _():
            acc_ref[...] = prod

        @pl.when(pl.program_id(2) > 0)
        def _():
            acc_ref[...] += prod

        @pl.when(pl.program_id(2) == nt - 1)
        def _():
            gm = acc_ref[...]
            if gated:
                gate_ref, w_ref, dg_ref = ins[2], ins[3], outs[1]

                @pl.when(step == nt - 1)
                def _():
                    dg_ref[...] = jnp.zeros(dg_ref.shape, F32)

                dg_ref[0:1, :] += _colsum(gm * w_ref[...].astype(F32))
                gm = gm * gate_ref[...]
            if piece_w:
                for q in range(nb_tile // piece_w):
                    o_ref[q] = gm[:, q * piece_w:(q + 1) * piece_w].astype(WIRE_DTYPE)
            else:
                o_ref[...] = gm.astype(WIRE_DTYPE)

        if gather_blk is not None:
            @pl.when(step == steps - 1)
            def _():
                finish()

    in_specs = [pl.BlockSpec((ts, ka_tile), lambda ja, jb, i: (i, ja)),
                pl.BlockSpec((ts, nb_tile), lambda ja, jb, i: (i, jb))]
    operands = [a, b]
    if piece_w:
        out_shape = [SDS((nb // piece_w, ka, piece_w), WIRE_DTYPE)]
        out_specs = [pl.BlockSpec((nb_tile // piece_w, ka, piece_w), lambda ja, jb, i: (jb, 0, 0))]
    else:
        out_shape = [SDS((ka, nb), WIRE_DTYPE)]
        out_specs = [pl.BlockSpec((ka_tile, nb_tile), lambda ja, jb, i: (ja, jb))]
    scratch = [pltpu.VMEM((ka_tile, nb_tile), F32)]
    if gated:
        in_specs += [pl.BlockSpec((1, nb_tile), lambda ja, jb, i: (0, jb)),
                     pl.BlockSpec((ka_tile, nb_tile), lambda ja, jb, i: (ja, jb))]
        operands += list(gated)
        out_shape.append(SDS((8, nb), F32))
        out_specs.append(pl.BlockSpec((8, nb_tile), lambda ja, jb, i: (0, jb)))
    if gather_blk is not None:
        in_specs.append(_any())
        operands.append(gather_blk)
        out_shape.append(SDS((8,) + gather_blk.shape, gather_blk.dtype))
        out_specs.append(_any())
        scratch += [pltpu.SemaphoreType.DMA((7,)), pltpu.SemaphoreType.DMA((7,)), pltpu.SemaphoreType.DMA]
    return pl.pallas_call(
        body, name=name, grid=(nja, njb, nt), in_specs=in_specs, out_specs=out_specs, out_shape=out_shape,
        scratch_shapes=scratch, compiler_params=_params("arbitrary", "arbitrary", "arbitrary"),
    )(*operands)


COND_COLS = 512


def _cond_partial(c_all, w_a, w_f):
    na = w_a.shape[1]

    def body(c_ref, wa_ref, wf_ref, oa_ref, of_ref):
        c_t = c_ref[...]
        ca = (c_t * _sigmoid(c_t)).astype(MM_DTYPE)
        oa_ref[...] = _dot(ca, wa_ref[...].astype(MM_DTYPE))

        @pl.when(pl.program_id(0) == 0)
        def _():
            of_ref[...] = _dot(ca, wf_ref[...].astype(MM_DTYPE))

    keep = lambda shape: pl.BlockSpec(shape, lambda j: (0, 0))
    return pl.pallas_call(
        body, name="cond_partial", grid=(na // COND_COLS,),
        in_specs=[keep((8, D)), pl.BlockSpec((D, COND_COLS), lambda j: (0, j)), keep(w_f.shape)],
        out_specs=[pl.BlockSpec((8, COND_COLS), lambda j: (0, j)), keep((8, w_f.shape[1]))],
        out_shape=[SDS((8, na), F32), SDS((8, w_f.shape[1]), F32)],
        compiler_params=_params("arbitrary"),
    )(c_all, w_a, w_f)


def _cond_grad(c_all, dmod_a, dmod_f):
    na = dmod_a.shape[1]

    def body(c_ref, da_ref, df_ref, oa_ref, of_ref):
        c_t = c_ref[...]
        ca = jnp.concatenate([c_t * _sigmoid(c_t), jnp.zeros((8, D), F32)], axis=0).astype(MM_DTYPE)

        def outer(d_ref):
            dm = jnp.concatenate([d_ref[...], jnp.zeros(d_ref.shape, F32)], axis=0).astype(MM_DTYPE)
            return _dot_tn(ca, dm)

        oa_ref[...] = outer(da_ref)

        @pl.when(pl.program_id(0) == 0)
        def _():
            of_ref[...] = outer(df_ref)

    keep = lambda shape: pl.BlockSpec(shape, lambda j: (0, 0))
    return pl.pallas_call(
        body, name="cond_grad", grid=(na // COND_COLS,),
        in_specs=[keep((8, D)), pl.BlockSpec((8, COND_COLS), lambda j: (0, j)), keep(dmod_f.shape)],
        out_specs=[pl.BlockSpec((D, COND_COLS), lambda j: (0, j)), keep((D, dmod_f.shape[1]))],
        out_shape=[SDS((D, na), F32), SDS((D, dmod_f.shape[1]), F32)],
        compiler_params=_params("arbitrary"),
    )(c_all, dmod_a, dmod_f)


def _row_tile(rows, cap=256):
    if rows <= cap:
        return rows
    for t in range(cap, 7, -8):
        if rows % t == 0:
            return t
    return rows


def _ordered_sum(name, parts, into_half=None):
    n, rows, cols = parts.shape
    rt = _row_tile(rows)
    nb = rows // rt

    def body(*refs):
        p_ref, o_ref = refs[-2:]
        acc = p_ref[0].astype(F32)
        for k in range(1, n):
            acc = acc + p_ref[k].astype(F32)
        o_ref[...] = acc

    if into_half is None:
        return pl.pallas_call(
            body, name=name, grid=(nb,),
            in_specs=[pl.BlockSpec((n, rt, cols), lambda i: (0, i, 0))],
            out_specs=pl.BlockSpec((rt, cols), lambda i: (i, 0)), out_shape=SDS((rows, cols), F32),
            compiler_params=_params("parallel"),
        )(parts)
    grid_spec = pltpu.PrefetchScalarGridSpec(
        num_scalar_prefetch=1, grid=(nb,),
        in_specs=[pl.BlockSpec((n, rt, cols), lambda i, c_ref: (0, i, 0))],
        out_specs=pl.BlockSpec((rt, cols), lambda i, c_ref: (c_ref[0] * nb + i, 0)))
    return pl.pallas_call(
        body, name=name, grid_spec=grid_spec, out_shape=SDS((2 * rows, cols), F32),
        compiler_params=_params("parallel"),
    )(into_half.astype(jnp.int32).reshape(1), parts)


def _adamw_update(w_ref, g_ref, m_ref, v_ref, d_ref, nm_ref, nv_ref):
    c1 = 1.0 - ADAM_B1 ** ADAM_STEP
    c2 = 1.0 - ADAM_B2 ** ADAM_STEP
    g_t = g_ref[...]
    m_new = ADAM_B1 * m_ref[...] + (1.0 - ADAM_B1) * g_t
    v_new = ADAM_B2 * v_ref[...] + (1.0 - ADAM_B2) * (g_t * g_t)
    nm_ref[...] = m_new
    nv_ref[...] = v_new
    d_ref[...] = -ADAM_LR * ((m_new / c1) / (jnp.sqrt(v_new / c2) + ADAM_EPS) + ADAM_WD * w_ref[...])


def _adamw_many(name, ws, gs, ms, vs):
    n = len(ws)

    def body(*refs):
        ins, outs = refs[:4 * n], refs[4 * n:]
        for k in range(n):
            _adamw_update(ins[k], ins[n + k], ins[2 * n + k], ins[3 * n + k], *outs[3 * k:3 * k + 3])

    return pl.pallas_call(
        body, name=name, out_shape=[SDS(w.shape, F32) for w in ws for _ in range(3)],
        compiler_params=pltpu.CompilerParams(vmem_limit_bytes=VMEM_LIMIT),
    )(*ws, *gs, *ms, *vs)


def _adamw(name, w, g, m, v):
    rows, cols = w.shape
    rt = _row_tile(rows)

    def body(*refs):
        _adamw_update(*refs)

    spec = pl.BlockSpec((rt, cols), lambda i: (i, 0))
    out = SDS((rows, cols), F32)
    return pl.pallas_call(body, name=name, grid=(rows // rt,), in_specs=[spec] * 4, out_specs=[spec] * 3,
                          out_shape=[out, out, out], compiler_params=_params("parallel"))(w, g, m, v)


def _place():
    return lax.axis_index("x"), lax.axis_index("y"), lax.axis_index("c")


def _other_chips(x, y):
    return [(1 - x, y), (x, 1 - y), (1 - x, 1 - y)]


def _all_gather8(name, blk):
    m, n = blk.shape

    def body(x_ref, out_ref, send_sems, recv_sems, local_sem):
        x, y, c = _place()
        me, sibling = (x, y, c), (x, y, 1 - c)
        chips = _other_chips(x, y)

        def slot(px, py, pc):
            return out_ref.at[4 * px + 2 * py + pc]

        def copy(k, block, to, src=None):
            return pltpu.make_async_remote_copy(
                src_ref=slot(*block) if src is None else src, dst_ref=slot(*block),
                send_sem=send_sems.at[k], recv_sem=recv_sems.at[k], device_id=to, device_id_type=MESH)

        mine = pltpu.make_async_copy(x_ref, slot(*me), local_sem)
        mine.start()
        first = [copy(0, me, sibling, src=x_ref)]
        first += [copy(1 + j, me, (*chip, c), src=x_ref) for j, chip in enumerate(chips)]
        for cp in first:
            cp.start()
        passed = [copy(4 + j, (*chip, c), sibling) for j, chip in enumerate(chips)]
        for j, chip in enumerate(chips):
            copy(1 + j, (*chip, c), me).wait_recv()
            passed[j].start()
        copy(0, sibling, me).wait_recv()
        for j, chip in enumerate(chips):
            copy(4 + j, (*chip, 1 - c), me).wait_recv()
        for cp in first + passed:
            cp.wait_send()
        mine.wait()

    return pl.pallas_call(
        body, name=name, out_shape=SDS((8, m, n), blk.dtype),
        in_specs=[_whole_vmem()], out_specs=_whole_vmem(),
        scratch_shapes=[pltpu.SemaphoreType.DMA((7,)), pltpu.SemaphoreType.DMA((7,)), pltpu.SemaphoreType.DMA],
        compiler_params=pltpu.CompilerParams(vmem_limit_bytes=VMEM_LIMIT),
    )(blk)


def _any():
    return pl.BlockSpec(memory_space=pl.ANY)


def _gather_weights(shards, blk):
    n = len(shards)

    def body(*refs):
        ins, outs, sems = refs[:n + 1], refs[n + 1:2 * n + 2], refs[2 * n + 2:]
        send, forward, finish = _gather_plan(ins[:n], outs[:n], sems[0], sems[1])
        send8, forward8, finish8 = _gather8_plan(ins[n], outs[n], *sems[2:])
        send8()
        send()
        forward8()
        forward()
        finish8()
        finish()

    return pl.pallas_call(
        body, name="gather_weights",
        out_shape=[SDS((NQ,) + s.shape, s.dtype) for s in shards] + [SDS((8,) + blk.shape, blk.dtype)],
        in_specs=[_any()] * (n + 1), out_specs=[_any()] * (n + 1),
        scratch_shapes=[pltpu.SemaphoreType.DMA((n, 6)), pltpu.SemaphoreType.DMA((n, 6)),
                        pltpu.SemaphoreType.DMA((7,)), pltpu.SemaphoreType.DMA((7,)), pltpu.SemaphoreType.DMA],
    )(*shards, blk)


def _own_piece(gathered, shard):
    myq = 2 * lax.axis_index("x") + lax.axis_index("y")
    return lax.dynamic_update_slice(gathered, shard[None], (myq,) + (0,) * shard.ndim)


def _scatter_to_owners(partials):
    n = len(partials)

    def body(*refs):
        send, finish = _scatter_plan(refs[:n], refs[n:2 * n], *refs[2 * n:])
        send()
        finish()

    return pl.pallas_call(
        body, name="scatter_to_owners",
        out_shape=[_scattered_shape(p) for p in partials],
        in_specs=[_any()] * n, out_specs=[_any()] * n,
        scratch_shapes=[pltpu.SemaphoreType.DMA((n, 7)), pltpu.SemaphoreType.DMA((n, 7))],
    )(*partials)


def _owner_sums(tag, arrived, partials=None):
    x, y, c = _place()
    sums = []
    for w, arr in enumerate(arrived):
        if partials is not None:
            part = partials[w]
            h = part.shape[1] // 2
            own = lax.dynamic_slice(part, (2 * x + y, c * h, 0), (1, h, part.shape[2]))
            arr = lax.dynamic_update_slice(arr, own, (4 * x + 2 * y + c, 0, 0))
        sums.append(_ordered_sum(f"owner_sum_{tag}_{w}", arr, into_half=c))
    return sums


def _join_halves(bufs):
    n = len(bufs)

    def body(*refs):
        ins, outs = refs[:n], refs[n:2 * n]
        send_sems, recv_sems = refs[2 * n:]
        x, y, c = _place()
        cps = []
        for w in range(n):
            h = ins[w].shape[0] // 2
            mine = outs[w].at[pl.ds(c * h, h)]
            cp = pltpu.make_async_remote_copy(src_ref=mine, dst_ref=mine, send_sem=send_sems.at[w],
                                              recv_sem=recv_sems.at[w], device_id=(x, y, 1 - c), device_id_type=MESH)
            cp.start()
            cps.append(cp)
        for cp in cps:
            cp.wait()

    return pl.pallas_call(
        body, name="join_halves",
        out_shape=[SDS(b.shape, b.dtype) for b in bufs],
        in_specs=[_any()] * n, out_specs=[_any()] * n, input_output_aliases={w: w for w in range(n)},
        scratch_shapes=[pltpu.SemaphoreType.DMA((n,)), pltpu.SemaphoreType.DMA((n,))],
    )(*bufs)


def _pad_rows(a, rows):
    return jnp.pad(a, ((0, rows - a.shape[0]),) + ((0, 0),) * (a.ndim - 1))


def _pack_small(dmod, g1, g2, gf, b_in, ln_g, ln_b, conv_b, gn_g, gn_b, ga, gb, sb, cw32, sw, loss_row):
    v512 = jnp.concatenate([ln_g, ln_b, conv_b, gn_g, gn_b, ga, gb, jnp.zeros((1, DA), F32)], axis=1).reshape(4, D)
    rows = [dmod.reshape(8, D), g1, g2, gf, b_in.reshape(2, D), v512, sb.reshape(1, D), cw32.reshape(16, D),
            sw.reshape(CHUNK, D), loss_row]
    packed = jnp.concatenate(rows, axis=0)
    return _pad_rows(packed, PK_ROWS)


def _unpack_small(p):
    v512 = p[PK_V512:PK_V512 + 4].reshape(1, 8 * DA)
    pieces = [v512[:, k * DA:(k + 1) * DA] for k in range(7)]
    return dict(
        dmod=p[PK_DMOD:PK_DMOD + 8].reshape(1, 8 * D), norm1_g=p[PK_G1:PK_G1 + 1], norm2_g=p[PK_G2:PK_G2 + 1],
        norm_f_g=p[PK_GF:PK_GF + 1], b_in=p[PK_BIN:PK_BIN + 2].reshape(1, 2 * D),
        a_ln_g=pieces[0], a_ln_b=pieces[1], b_conv_b=pieces[2], b_gn_g=pieces[3], b_gn_b=pieces[4],
        out_norm_a_g=pieces[5], out_norm_b_g=pieces[6],
        a_spatial_b=p[PK_SB:PK_SB + 1].reshape(N_HEADS, CHUNK),
        b_conv_w=p[PK_CW:PK_CW + 16].reshape(HALO, DB),
        a_spatial_w=p[PK_SW:PK_SW + CHUNK].reshape(N_HEADS, CHUNK, CHUNK))


def kernel(x, c, ada_w, ada_b, norm1_g, w_in, b_in, a_ln_g, a_ln_b, a_spatial_w, a_spatial_b, b_conv_w, b_conv_b, b_gn_g, b_gn_b, out_norm_a_g, out_norm_b_g, w_out, norm2_g, w_ffn_in, w_ffn_out, ada_f_w, ada_f_b, norm_f_g, loss_target, m_ada_w, m_ada_b, m_norm1_g, m_w_in, m_b_in, m_a_ln_g, m_a_ln_b, m_a_spatial_w, m_a_spatial_b, m_b_conv_w, m_b_conv_b, m_b_gn_g, m_b_gn_b, m_out_norm_a_g, m_out_norm_b_g, m_w_out, m_norm2_g, m_w_ffn_in, m_w_ffn_out, m_ada_f_w, m_ada_f_b, m_norm_f_g, v_ada_w, v_ada_b, v_norm1_g, v_w_in, v_b_in, v_a_ln_g, v_a_ln_b, v_a_spatial_w, v_a_spatial_b, v_b_conv_w, v_b_conv_b, v_b_gn_g, v_b_gn_b, v_out_norm_a_g, v_out_norm_b_g, v_w_out, v_norm2_g, v_w_ffn_in, v_w_ffn_out, v_ada_f_w, v_ada_f_b, v_norm_f_g):
    mx, my, mc = _place()
    me = 4 * mx + 2 * my + mc
    myq = 2 * mx + my
    xs = x[0]
    target = loss_target[0]
    s = xs.shape[0]
    n_ada = ada_w.shape[2]

    cw_shard = _pad_rows(b_conv_w[0], HALO)
    mix_shards = [w_in[0].astype(MM_DTYPE), w_out[0].astype(MM_DTYPE)]
    ffn_shards = [w_ffn_in[0].astype(MM_DTYPE), w_ffn_out[0].astype(MM_DTYPE)]
    w_in4, w_out4, first = _gather_weights(mix_shards, jnp.concatenate([c.reshape(8, LANES), cw_shard], axis=0))
    w_in4, w_out4 = _own_piece(w_in4, mix_shards[0]), _own_piece(w_out4, mix_shards[1])
    w_out_f = w_out4.reshape(D, D)
    c_all = first[:, 0:8, :].reshape(8, D)
    conv_w = jnp.concatenate([first[4 * (q // 2) + 2 * (q % 2), 8:8 + HALO, :] for q in range(NQ)], axis=1)
    cond_part = jnp.concatenate(_cond_partial(c_all, ada_w[0], ada_f_w), axis=1)
    cond_all = _all_gather8("gather_cond", cond_part)
    cond_q = [cond_all[4 * (q // 2) + 2 * (q % 2)] for q in range(NQ)]
    mod_all = jnp.concatenate([cq[:, :n_ada] for cq in cond_q] + [cq[:, n_ada:] for cq in cond_q], axis=1)
    mod = lax.dynamic_slice_in_dim(mod_all, me, 1, axis=0)
    mods = jnp.concatenate([ada_b, ada_f_b.reshape(1, 2 * D)], axis=1)

    causal = jnp.tril(jnp.ones((CHUNK, CHUNK), dtype=bool))
    wm_f = jnp.where(causal[None], a_spatial_w[0], 0.0)
    wm = wm_f.astype(MM_DTYPE)
    wmt = jnp.swapaxes(wm_f, 1, 2).astype(MM_DTYPE)
    bst = jnp.repeat(a_spatial_b[0].T, HALF, axis=1)

    z, x1, yb1, y, w_ffn_in4, w_ffn_out4 = _mixer_fwd(
        xs, mod, mods, norm1_g, w_in4, b_in, a_ln_g, a_ln_b, wm, bst, conv_w, b_conv_b, b_gn_g, b_gn_b,
        out_norm_a_g, out_norm_b_g, w_out_f, ffn_shards)
    w_ffn_out_f = w_ffn_out4.reshape(DFF, D)
    g, up, h2, dx2, acc_f = _ffn_fwd(x1, target, mod, mods, norm2_g, norm_f_g, w_ffn_in4, w_ffn_out_f)

    dff, a_act, dx1, acc_2 = _ffn_bwd(dx2, x1, g, up, mod, mods, norm2_g, w_ffn_in4, w_ffn_out_f)
    (gw_ffn_in4,) = _grad_matmul("grad_w_ffn_in", h2, dff, D, PW_FF, piece_w=PW_FF)
    modv = mod + mods
    gw_ffn_out, dgate2 = _grad_matmul("grad_w_ffn_out", a_act, dx2, PW_FF, D, gated=(modv[:, 5 * D:6 * D], w_ffn_out_f))
    gw_out, dgate1 = _grad_matmul("grad_w_out", y, dx1, D, D, gated=(modv[:, 2 * D:3 * D], w_out_f))
    early_partials = [gw_ffn_in4, gw_ffn_out.reshape(NQ, DFF // NQ, D), gw_out.reshape(NQ, D // NQ, D)]
    gx, dz, h, acc_1, acc_bin, acc_5, acc_cw, acc_sw, acc_sb, *early_arrived = _mixer_bwd(
        dx1, xs, z, yb1, mod, mods, norm1_g, w_in4, a_ln_g, a_ln_b, wm, wmt, bst, conv_w, b_gn_g, b_gn_b,
        out_norm_a_g, out_norm_b_g, w_out_f, early_partials)

    dmod = jnp.concatenate([acc_1[0:1], acc_1[1:2], dgate1[0:1], acc_2[0:1], acc_2[1:2], dgate2[0:1],
                            acc_f[0:1], acc_f[1:2]], axis=1)
    sw_grad = jnp.where(causal[None], acc_sw, 0.0)
    sb_grad = acc_sb[:, ::HALF].T
    packed = _pack_small(dmod, acc_1[2:3], acc_2[2:3], acc_f[2:3], acc_bin[0:1], acc_5[2:3], acc_5[3:4], acc_5[6:7],
                         acc_5[4:5], acc_5[5:6], acc_5[0:1], acc_5[1:2], sb_grad, acc_cw, sw_grad, acc_f[4:5])
    gw_in4, gathered = _grad_matmul("grad_w_in", h, dz, D, NQ * PW_IN, piece_w=PW_IN, gather_blk=packed)
    late_arrived = _scatter_to_owners([gw_in4])
    g_w_in, g_w_ffn_in, g_w_ffn_out, g_w_out = _join_halves(
        _owner_sums("late", late_arrived, [gw_in4]) + _owner_sums("early", early_arrived))
    summed = _ordered_sum("small_grad_sum", gathered)
    loss = summed[PK_LOSS, 0]
    small = _unpack_small(summed)
    dmod_all = gathered[:, PK_DMOD:PK_DMOD + 8, :].reshape(8, 8 * D)
    g_ada_w, g_ada_f_w = _cond_grad(c_all, lax.dynamic_slice_in_dim(dmod_all, myq * n_ada, n_ada, axis=1),
                                    lax.dynamic_slice_in_dim(dmod_all, 6 * D + myq * PW_IN, PW_IN, axis=1))

    grads = dict(
        ada_w=g_ada_w, ada_b=small["dmod"][:, :6 * D], norm1_g=small["norm1_g"], w_in=g_w_in,
        b_in=small["b_in"], a_ln_g=small["a_ln_g"], a_ln_b=small["a_ln_b"], a_spatial_w=small["a_spatial_w"],
        a_spatial_b=small["a_spatial_b"],
        b_conv_w=lax.dynamic_slice_in_dim(small["b_conv_w"], myq * LANES, LANES, axis=1)[:CONV_W],
        b_conv_b=small["b_conv_b"], b_gn_g=small["b_gn_g"], b_gn_b=small["b_gn_b"],
        out_norm_a_g=small["out_norm_a_g"], out_norm_b_g=small["out_norm_b_g"], w_out=g_w_out,
        norm2_g=small["norm2_g"], w_ffn_in=g_w_ffn_in, w_ffn_out=g_w_ffn_out, ada_f_w=g_ada_f_w,
        ada_f_b=small["dmod"][:, 6 * D:], norm_f_g=small["norm_f_g"])

    weights = dict(ada_w=ada_w, ada_b=ada_b, norm1_g=norm1_g, w_in=w_in, b_in=b_in, a_ln_g=a_ln_g, a_ln_b=a_ln_b,
                   a_spatial_w=a_spatial_w, a_spatial_b=a_spatial_b, b_conv_w=b_conv_w, b_conv_b=b_conv_b, b_gn_g=b_gn_g,
                   b_gn_b=b_gn_b, out_norm_a_g=out_norm_a_g, out_norm_b_g=out_norm_b_g, w_out=w_out, norm2_g=norm2_g,
                   w_ffn_in=w_ffn_in, w_ffn_out=w_ffn_out, ada_f_w=ada_f_w, ada_f_b=ada_f_b, norm_f_g=norm_f_g)
    m_in = dict(ada_w=m_ada_w, ada_b=m_ada_b, norm1_g=m_norm1_g, w_in=m_w_in, b_in=m_b_in, a_ln_g=m_a_ln_g, a_ln_b=m_a_ln_b,
                a_spatial_w=m_a_spatial_w, a_spatial_b=m_a_spatial_b, b_conv_w=m_b_conv_w, b_conv_b=m_b_conv_b,
                b_gn_g=m_b_gn_g, b_gn_b=m_b_gn_b, out_norm_a_g=m_out_norm_a_g, out_norm_b_g=m_out_norm_b_g, w_out=m_w_out,
                norm2_g=m_norm2_g, w_ffn_in=m_w_ffn_in, w_ffn_out=m_w_ffn_out, ada_f_w=m_ada_f_w, ada_f_b=m_ada_f_b,
                norm_f_g=m_norm_f_g)
    v_in = dict(ada_w=v_ada_w, ada_b=v_ada_b, norm1_g=v_norm1_g, w_in=v_w_in, b_in=v_b_in, a_ln_g=v_a_ln_g, a_ln_b=v_a_ln_b,
                a_spatial_w=v_a_spatial_w, a_spatial_b=v_a_spatial_b, b_conv_w=v_b_conv_w, b_conv_b=v_b_conv_b,
                b_gn_g=v_b_gn_g, b_gn_b=v_b_gn_b, out_norm_a_g=v_out_norm_a_g, out_norm_b_g=v_out_norm_b_g, w_out=v_w_out,
                norm2_g=v_norm2_g, w_ffn_in=v_w_ffn_in, w_ffn_out=v_w_ffn_out, ada_f_w=v_ada_f_w, ada_f_b=v_ada_f_b,
                norm_f_g=v_norm_f_g)
    names = list(weights)
    big = ("ada_w", "w_in", "w_out", "w_ffn_in", "w_ffn_out", "ada_f_w")

    def flat2(a):
        return a.reshape(-1, a.shape[-1])

    delta, new_m, new_v = {}, {}, {}
    for nm in big:
        shape = weights[nm].shape
        grads[nm] = grads[nm].reshape(shape)
        d_, m_, v_ = _adamw("adamw_" + nm, flat2(weights[nm]), flat2(grads[nm]), flat2(m_in[nm]), flat2(v_in[nm]))
        delta[nm], new_m[nm], new_v[nm] = d_.reshape(shape), m_.reshape(shape), v_.reshape(shape)

    small_names = [nm for nm in names if nm not in big]
    for nm in small_names:
        grads[nm] = grads[nm].reshape(weights[nm].shape)
    small_out = _adamw_many("adamw_small", *[[flat2(tree[nm]) for nm in small_names] for tree in (weights, grads, m_in, v_in)])
    for k, nm in enumerate(small_names):
        shape = weights[nm].shape
        delta[nm], new_m[nm], new_v[nm] = [o.reshape(shape) for o in small_out[3 * k:3 * k + 3]]

    grad_x = gx.reshape(x.shape)
    return (loss, grad_x, *[grads[nm] for nm in names], *[delta[nm] for nm in names],
            *[new_m[nm] for nm in names], *[new_v[nm] for nm in names])
```

```python
import functools
import math

import jax
import jax.numpy as jnp
from jax import lax
from jax.experimental import pallas as pl
from jax.experimental.pallas import tpu as pltpu

F32 = jnp.float32
MM_DTYPE = jnp.bfloat16
WIRE_DTYPE = jnp.bfloat16
SDS = jax.ShapeDtypeStruct
MESH = pl.DeviceIdType.MESH

D = 1024
DA = 512
DB = 512
NQ = 4
PW_IN = 512
DFF = 2816
PW_FF = 1408
CHUNK = 128
N_HEADS = 8
CONV_W = 31
HALO = 32
CONV_ROWS = 64
EPS = 1e-6
LANES = 128
HALF = 64

ROW_TILE = 256
FWD_ROW_TILE = 512
FFN_ROW_TILE = 256
GRAD_ROW_TILE = 2048
ADAMW_STEPS = 8
VMEM_LIMIT = 60 * 1024 * 1024

ADAM_LR = 0.001
ADAM_B1 = 0.9
ADAM_B2 = 0.999
ADAM_EPS = 1e-08
ADAM_WD = 0.01
ADAM_STEP = 10

PK_DMOD = 0
PK_G1 = 8
PK_G2 = 9
PK_GF = 10
PK_BIN = 11
PK_V512 = 13
PK_SB = 17
PK_CW = 18
PK_SW = 34
PK_LOSS = 162
PK_ROWS = 168


def _dot(a, b):
    return jnp.dot(a, b, preferred_element_type=F32)


def _dot_nt(a, b):
    return lax.dot_general(a, b, (((1,), (1,)), ((), ())), preferred_element_type=F32)


def _dot_tn(a, b):
    return lax.dot_general(a, b, (((0,), (0,)), ((), ())), preferred_element_type=F32)


def _rowsum(x):
    return jnp.sum(x, axis=-1, keepdims=True)


def _colsum(x):
    return jnp.sum(x, axis=0, keepdims=True)


def _group_sum(x):
    rows, width = x.shape
    lo_mask = lax.broadcasted_iota(jnp.int32, (rows, LANES), 1) < HALF
    outs = []
    for jb in range(width // LANES):
        xb = x[:, jb * LANES:(jb + 1) * LANES]
        lo = _rowsum(jnp.where(lo_mask, xb, 0.0))
        hi = _rowsum(jnp.where(lo_mask, 0.0, xb))
        outs.append(jnp.where(lo_mask, lo, hi))
    return jnp.concatenate(outs, axis=-1)


def _sigmoid(x):
    return 1.0 / (1.0 + jnp.exp(-x))


def _gelu_parts(u):
    cdf = 0.5 * (1.0 + lax.erf(u * (1.0 / math.sqrt(2.0))))
    pdf = jnp.exp(-0.5 * u * u) * (1.0 / math.sqrt(2.0 * math.pi))
    return u * cdf, cdf + u * pdf


def _whole_vmem():
    return pl.BlockSpec(memory_space=pltpu.VMEM)


def _params(*semantics):
    return pltpu.CompilerParams(dimension_semantics=semantics, vmem_limit_bytes=VMEM_LIMIT)


def _mod_rows(mod_ref, modb_ref, first, count):
    m = mod_ref[...] + modb_ref[...]
    return [m[:, (first + k) * D:(first + k + 1) * D] for k in range(count)]


def _mixer_recompute(z_parts, lng, lnb, wm_ref, bst_ref, mix_ref):
    u, v, val, gate = z_parts
    rows = u.shape[0]
    gu, dgu = _gelu_parts(u)
    gv, dgv = _gelu_parts(v)
    mu = _rowsum(gv) * (1.0 / DA)
    vc = gv - mu
    rs = lax.rsqrt(_rowsum(vc * vc) * (1.0 / DA) + EPS)
    vhat = vc * rs
    vl = vhat * lng + lnb
    vlb = vl.astype(MM_DTYPE)
    lo_mask = lax.broadcasted_iota(jnp.int32, (CHUNK, LANES), 1) < HALF
    for ck in range(rows // CHUNK):
        for jb in range(DA // LANES):
            blk = vlb[ck * CHUNK:(ck + 1) * CHUNK, jb * LANES:(jb + 1) * LANES]
            a = _dot(wm_ref[2 * jb], blk)
            b = _dot(wm_ref[2 * jb + 1], blk)
            mix_ref[ck * CHUNK:(ck + 1) * CHUNK, jb * LANES:(jb + 1) * LANES] = (
                jnp.where(lo_mask, a, b) + bst_ref[:, jb * LANES:(jb + 1) * LANES])
    mixed = mix_ref[...]
    sg = _sigmoid(gate)
    yb0 = val * sg
    return dict(gu=gu, dgu=dgu, dgv=dgv, rs=rs, vhat=vhat, vlb=vlb, mixed=mixed, sg=sg, yb0=yb0)


def _conv_branch_tail(yb1, gng, gnb):
    gm = _group_sum(yb1) * (1.0 / HALF)
    gc = yb1 - gm
    grs = lax.rsqrt(_group_sum(gc * gc) * (1.0 / HALF) + EPS)
    ghat = gc * grs
    yb2 = ghat * gng + gnb
    s2 = _sigmoid(yb2)
    return dict(grs=grs, ghat=ghat, yb2=yb2, s2=s2, y_b=yb2 * s2)


def _shifted_copies(e_ref, sh_ref):
    n = sh_ref.shape[1]
    for b in range(1, 8):
        sh_ref[b - 1] = e_ref[pl.ds(b, n), :]


def _window(e_ref, sh_ref, offset, r0, nrows, cols):
    a, b = divmod(offset, 8)
    if b == 0:
        return e_ref[pl.ds(r0 + 8 * a, nrows), cols]
    return sh_ref[b - 1, pl.ds(r0 + 8 * a, nrows), cols]


def _conv_taps(e_ref, sh_ref, cw_ref, out_ref, ts, first_offset, flip, bias_ref=None, other_ref=None, tap_acc_ref=None):
    groups = CONV_ROWS // 8
    for cb in range(DB // LANES):
        cols = slice(cb * LANES, (cb + 1) * LANES)
        tap_acc = [jnp.zeros((8, LANES), F32) for _ in range(CONV_W)]
        for rb in range(ts // CONV_ROWS):
            r0 = rb * CONV_ROWS
            acc = jnp.zeros((CONV_ROWS, LANES), F32)
            if bias_ref is not None:
                acc = acc + bias_ref[:, cols]
            if other_ref is not None:
                other = other_ref[r0:r0 + CONV_ROWS, cols]
            for j in range(CONV_W):
                k = CONV_W - 1 - j if flip else j
                win = _window(e_ref, sh_ref, first_offset + j, r0, CONV_ROWS, cols)
                acc = acc + win * cw_ref[k:k + 1, cols]
                if other_ref is not None:
                    tap_acc[k] = tap_acc[k] + jnp.sum((other * win).reshape(groups, 8, LANES), axis=0)
            out_ref[r0:r0 + CONV_ROWS, cols] = acc
        if other_ref is not None:
            for k in range(CONV_W):
                tap_acc_ref[k:k + 1, cols] += _colsum(tap_acc[k])


def _gather_plan(ins, outs, send_sems, recv_sems, local_sems=None):
    x, y, c = _place()
    sibling = (x, y, 1 - c)
    chips = _other_chips(x, y)
    myq = 2 * x + y

    def copy(w, k, q, hc, to, src=None):
        rows = ins[w].shape[0]
        dst = outs[w].at[q, pl.ds(hc * (rows // 2), rows // 2)]
        return pltpu.make_async_remote_copy(
            src_ref=dst if src is None else src, dst_ref=dst,
            send_sem=send_sems.at[w, k], recv_sem=recv_sems.at[w, k], device_id=to, device_id_type=MESH)

    def own(w):
        return pltpu.make_async_copy(ins[w], outs[w].at[myq], local_sems.at[w])

    def send():
        for w in range(len(ins)):
            rows = ins[w].shape[0]
            src = ins[w].at[pl.ds(c * (rows // 2), rows // 2)]
            for j, chip in enumerate(chips):
                copy(w, j, myq, c, (*chip, c), src=src).start()
            if local_sems is not None:
                own(w).start()

    def forward():
        for w in range(len(ins)):
            for j, (qx, qy) in enumerate(chips):
                copy(w, j, 2 * qx + qy, c, sibling).wait_recv()
                copy(w, 3 + j, 2 * qx + qy, c, sibling).start()

    def finish():
        for w in range(len(ins)):
            for j, (qx, qy) in enumerate(chips):
                copy(w, 3 + j, 2 * qx + qy, 1 - c, sibling).wait_recv()
        for w in range(len(ins)):
            for k, (qx, qy) in enumerate(chips + chips):
                copy(w, k, 2 * qx + qy, c, sibling).wait_send()
            if local_sems is not None:
                own(w).wait()

    return send, forward, finish


def _mixer_fwd(x, mod, mods, norm1_g, w_in4, b_in, ln_g, ln_b, wm, bst, conv_w, conv_b, gn_g, gn_b, ga, gb, w_out,
               ffn_shards):
    s = x.shape[0]
    ts = min(FWD_ROW_TILE, s)
    nt = s // ts
    n_sh = len(ffn_shards)

    def body(x_ref, mod_ref, modb_ref, g1_ref, w4_ref, bin_ref, lng_ref, lnb_ref, wm_ref, bst_ref, cw_ref, cb_ref,
             gng_ref, gnb_ref, ga_ref, gb_ref, wout_ref, *rest):
        shard_refs, rest = rest[:n_sh], rest[n_sh:]
        z_ref, x1_ref, yb1_ref, y_ref = rest[:4]
        full_refs, rest = rest[4:4 + n_sh], rest[4 + n_sh:]
        e_ref, sh_ref, mix_ref, send_sems, recv_sems, local_sems = rest
        i = pl.program_id(0)
        send, forward, finish = _gather_plan(shard_refs, full_refs, send_sems, recv_sems, local_sems)

        @pl.when(i == 0)
        def _():
            send()
            e_ref[0:HALO, :] = jnp.zeros((HALO, DB), F32)

        @pl.when(i == (3 * nt) // 4)
        def _():
            forward()

        shift1, scale1, gate1 = _mod_rows(mod_ref, modb_ref, 0, 3)
        x_t = x_ref[...]
        r1 = lax.rsqrt(_rowsum(x_t * x_t) * (1.0 / D) + EPS)
        h = (x_t * r1 * g1_ref[...]) * (1.0 + scale1) + shift1
        hb = h.astype(MM_DTYPE)
        z_parts = []
        for q in range(NQ):
            zq = _dot(hb, w4_ref[q]) + bin_ref[:, q * PW_IN:(q + 1) * PW_IN]
            z_ref[:, q * PW_IN:(q + 1) * PW_IN] = zq
            z_parts.append(zq)
        r = _mixer_recompute(z_parts, lng_ref[...], lnb_ref[...], wm_ref, bst_ref, mix_ref)
        y_a = r["gu"] * r["mixed"]
        e_ref[HALO:HALO + ts, :] = r["yb0"]
        _shifted_copies(e_ref, sh_ref)
        _conv_taps(e_ref, sh_ref, cw_ref, yb1_ref, ts, HALO - (CONV_W - 1), False, bias_ref=cb_ref)
        e_ref[0:HALO, :] = e_ref[ts:ts + HALO, :]
        t = _conv_branch_tail(yb1_ref[...], gng_ref[...], gnb_ref[...])
        ra = lax.rsqrt(_rowsum(y_a * y_a) * (1.0 / DA) + EPS)
        rb = lax.rsqrt(_rowsum(t["y_b"] * t["y_b"]) * (1.0 / DB) + EPS)
        yan = (y_a * ra * ga_ref[...]).astype(MM_DTYPE)
        ybn = (t["y_b"] * rb * gb_ref[...]).astype(MM_DTYPE)
        y_ref[:, 0:DA] = yan
        y_ref[:, DA:D] = ybn
        o1 = _dot(yan, wout_ref[0:DA, :]) + _dot(ybn, wout_ref[DA:D, :])
        x1_ref[...] = x_t + gate1 * o1

        @pl.when(i == nt - 1)
        def _():
            finish()

    row = lambda w: pl.BlockSpec((ts, w), lambda i: (i, 0))
    full = lambda a: pl.BlockSpec(a.shape, lambda i: (0,) * a.ndim)
    return pl.pallas_call(
        body, name="mixer_fwd", grid=(nt,),
        in_specs=[row(D), full(mod), full(mods), full(norm1_g), _whole_vmem(), full(b_in), full(ln_g), full(ln_b),
                  _whole_vmem(), full(bst), full(conv_w), full(conv_b), full(gn_g), full(gn_b), full(ga), full(gb),
                  _whole_vmem()] + [_any()] * n_sh,
        out_specs=[row(4 * PW_IN), row(D), row(DB), row(D)] + [_any()] * n_sh,
        out_shape=[SDS((s, 4 * PW_IN), F32), SDS((s, D), F32), SDS((s, DB), F32), SDS((s, D), MM_DTYPE)]
        + [SDS((NQ,) + w.shape, w.dtype) for w in ffn_shards],
        scratch_shapes=[pltpu.VMEM((ts + HALO, DB), F32), pltpu.VMEM((7, ts + HALO - 8, DB), F32), pltpu.VMEM((ts, DA), F32),
                        pltpu.SemaphoreType.DMA((n_sh, 6)), pltpu.SemaphoreType.DMA((n_sh, 6)),
                        pltpu.SemaphoreType.DMA((n_sh,))],
        compiler_params=_params("arbitrary"),
    )(x, mod, mods, norm1_g, w_in4, b_in, ln_g, ln_b, wm, bst, conv_w, conv_b, gn_g, gn_b, ga, gb, w_out, *ffn_shards)


def _ffn_fwd(x1, target, mod, mods, norm2_g, norm_f_g, w_ffn_in4, w_ffn_out):
    s = x1.shape[0]
    sub_rows = min(FFN_ROW_TILE, s)
    ts = min(2 * sub_rows, s)
    nt = s // ts

    def body(x1_ref, tgt_ref, mod_ref, modb_ref, g2_ref, gf_ref, wf_ref, wo_ref,
             g_ref, up_ref, h2_ref, dx2_ref, acc_ref):
        i = pl.program_id(0)

        @pl.when(i == 0)
        def _():
            acc_ref[...] = jnp.zeros(acc_ref.shape, F32)

        shift2, scale2, gate2, shift_f, scale_f = _mod_rows(mod_ref, modb_ref, 3, 5)
        for sub in range(ts // sub_rows):
            rows = slice(sub * sub_rows, (sub + 1) * sub_rows)
            x1_t = x1_ref[rows, :]
            r2 = lax.rsqrt(_rowsum(x1_t * x1_t) * (1.0 / D) + EPS)
            h2 = (x1_t * r2 * g2_ref[...]) * (1.0 + scale2) + shift2
            h2b = h2.astype(MM_DTYPE)
            h2_ref[rows, :] = h2b
            o2 = jnp.zeros((sub_rows, D), F32)
            for p in range(2):
                g = _dot(h2b, wf_ref[p])
                up = _dot(h2b, wf_ref[2 + p])
                g_ref[rows, p * PW_FF:(p + 1) * PW_FF] = g.astype(MM_DTYPE)
                up_ref[rows, p * PW_FF:(p + 1) * PW_FF] = up.astype(MM_DTYPE)
                a = (g * _sigmoid(g) * up).astype(MM_DTYPE)
                o2 = o2 + _dot(a, wo_ref[p * PW_FF:(p + 1) * PW_FF, :])
            x2 = x1_t + gate2 * o2
            rf = lax.rsqrt(_rowsum(x2 * x2) * (1.0 / D) + EPS)
            gf = gf_ref[...]
            nf = x2 * rf * gf
            err = nf * (1.0 + scale_f) + shift_f - tgt_ref[rows, :]
            d_out = err * (1.0 / D)
            d_nf = d_out * (1.0 + scale_f)
            t = d_nf * gf
            dx2_ref[rows, :] = rf * t - x2 * (rf * rf * rf) * (_rowsum(t * x2) * (1.0 / D))
            acc_ref[0:1, :] += _colsum(d_out)
            acc_ref[1:2, :] += _colsum(d_out * nf)
            acc_ref[2:3, :] += _colsum(d_nf * x2 * rf)
            acc_ref[3:4, :] += _colsum(err * err)

        @pl.when(i == nt - 1)
        def _():
            acc_ref[4:5, :] = jnp.zeros((1, D), F32) + _rowsum(acc_ref[3:4, :]) * (0.5 / D)

    row = lambda w: pl.BlockSpec((ts, w), lambda i: (i, 0))
    full = lambda a: pl.BlockSpec(a.shape, lambda i: (0,) * a.ndim)
    return pl.pallas_call(
        body, name="ffn_fwd", grid=(nt,),
        in_specs=[row(D), row(D), full(mod), full(mods), full(norm2_g), full(norm_f_g), _whole_vmem(), _whole_vmem()],
        out_specs=[row(DFF), row(DFF), row(D), row(D), pl.BlockSpec((8, D), lambda i: (0, 0))],
        out_shape=[SDS((s, DFF), MM_DTYPE), SDS((s, DFF), MM_DTYPE), SDS((s, D), MM_DTYPE), SDS((s, D), F32),
                   SDS((8, D), F32)],
        compiler_params=_params("arbitrary"),
    )(x1, target, mod, mods, norm2_g, norm_f_g, w_ffn_in4, w_ffn_out)


def _ffn_bwd(dx2, x1, g, up, mod, mods, norm2_g, w_ffn_in4, w_ffn_out):
    s = x1.shape[0]
    ts = min(FFN_ROW_TILE, s)
    nt = s // ts

    def body(dx2_ref, x1_ref, g_ref, up_ref, mod_ref, modb_ref, g2_ref, wf_ref, wo_ref,
             dff_ref, a_ref, dx1_ref, acc_ref):
        @pl.when(pl.program_id(0) == 0)
        def _():
            acc_ref[...] = jnp.zeros(acc_ref.shape, F32)

        shift2, scale2, gate2 = _mod_rows(mod_ref, modb_ref, 3, 3)
        dx2_t = dx2_ref[...]
        do2 = (dx2_t * gate2).astype(MM_DTYPE)
        dh2 = jnp.zeros((ts, D), F32)
        for p in range(2):
            da = _dot_nt(do2, wo_ref[p * PW_FF:(p + 1) * PW_FF, :])
            gp = g_ref[:, p * PW_FF:(p + 1) * PW_FF].astype(F32)
            upp = up_ref[:, p * PW_FF:(p + 1) * PW_FF].astype(F32)
            sg = _sigmoid(gp)
            silu = gp * sg
            a_ref[:, p * PW_FF:(p + 1) * PW_FF] = (silu * upp).astype(MM_DTYPE)
            dg = (da * upp * (sg * (1.0 + gp * (1.0 - sg)))).astype(MM_DTYPE)
            dup = (da * silu).astype(MM_DTYPE)
            dff_ref[:, p * PW_FF:(p + 1) * PW_FF] = dg
            dff_ref[:, DFF + p * PW_FF:DFF + (p + 1) * PW_FF] = dup
            dh2 = dh2 + _dot_nt(dg, wf_ref[p]) + _dot_nt(dup, wf_ref[2 + p])
        x1_t = x1_ref[...]
        r2 = lax.rsqrt(_rowsum(x1_t * x1_t) * (1.0 / D) + EPS)
        g2 = g2_ref[...]
        xr = x1_t * r2
        dn2 = dh2 * (1.0 + scale2)
        t = dn2 * g2
        dx1_ref[...] = dx2_t + r2 * t - x1_t * (r2 * r2 * r2) * (_rowsum(t * x1_t) * (1.0 / D))
        acc_ref[0:1, :] += _colsum(dh2)
        acc_ref[1:2, :] += _colsum(dh2 * (xr * g2))
        acc_ref[2:3, :] += _colsum(dn2 * xr)

    row = lambda w: pl.BlockSpec((ts, w), lambda i: (i, 0))
    full = lambda a: pl.BlockSpec(a.shape, lambda i: (0,) * a.ndim)
    return pl.pallas_call(
        body, name="ffn_bwd", grid=(nt,),
        in_specs=[row(D), row(D), row(DFF), row(DFF), full(mod), full(mods), full(norm2_g), _whole_vmem(), _whole_vmem()],
        out_specs=[row(2 * DFF), row(DFF), row(D), pl.BlockSpec((8, D), lambda i: (0, 0))],
        out_shape=[SDS((s, 2 * DFF), MM_DTYPE), SDS((s, DFF), MM_DTYPE), SDS((s, D), F32), SDS((8, D), F32)],
        compiler_params=_params("arbitrary"),
    )(dx2, x1, g, up, mod, mods, norm2_g, w_ffn_in4, w_ffn_out)


def _scatter_plan(ins, outs, send_sems, recv_sems, local_sems=None):
    x, y, c = _place()
    me = 4 * x + 2 * y + c

    def copies():
        cps = []
        for w in range(len(ins)):
            h = ins[w].shape[1] // 2
            for k in range(1, 8):
                px, py, pc = (1 - x if k & 4 else x), (1 - y if k & 2 else y), (1 - c if k & 1 else c)
                cps.append(pltpu.make_async_remote_copy(
                    src_ref=ins[w].at[2 * px + py, pl.ds(pc * h, h)], dst_ref=outs[w].at[me],
                    send_sem=send_sems.at[w, k - 1], recv_sem=recv_sems.at[w, k - 1],
                    device_id=(px, py, pc), device_id_type=MESH))
        return cps

    def own():
        if local_sems is None:
            return []
        return [pltpu.make_async_copy(ins[w].at[2 * x + y, pl.ds(c * (ins[w].shape[1] // 2), ins[w].shape[1] // 2)],
                                      outs[w].at[me], local_sems.at[w]) for w in range(len(ins))]

    def send():
        for cp in copies() + own():
            cp.start()

    def finish():
        for cp in copies() + own():
            cp.wait()

    return send, finish


def _scattered_shape(partial):
    nq, rows, cols = partial.shape
    return SDS((8, rows // 2, cols), partial.dtype)


def _mixer_bwd(dx1, x, z, yb1, mod, mods, norm1_g, w_in4, ln_g, ln_b, wm, wmt, bst, conv_w, gn_g, gn_b, ga, gb, w_out,
               partials):
    s = x.shape[0]
    ts = min(ROW_TILE, s)
    nt = s // ts
    n_cs = len(partials)

    def body(dx1_ref, x_ref, z_ref, yb1_ref, mod_ref, modb_ref, g1_ref, w4_ref, lng_ref, lnb_ref, wm_ref, wmt_ref,
             bst_ref, cw_ref, gng_ref, gnb_ref, ga_ref, gb_ref, wout_ref, *rest):
        cs_refs, rest = rest[:n_cs], rest[n_cs:]
        gx_ref, dz_ref, h_ref, a1_ref, a2_ref, a5_ref, acw_ref, asw_ref, asb_ref = rest[:9]
        arrived_refs, rest = rest[9:9 + n_cs], rest[9 + n_cs:]
        e_ref, sh_ref, mix_ref, dvl_ref, send_sems, recv_sems, local_sems = rest
        i = pl.program_id(0)
        send, finish = _scatter_plan(cs_refs, arrived_refs, send_sems, recv_sems, local_sems)

        @pl.when(i == 0)
        def _():
            send()
            e_ref[ts:ts + HALO, :] = jnp.zeros((HALO, DB), F32)
            for r in (a1_ref, a2_ref, a5_ref, acw_ref, asw_ref, asb_ref):
                r[...] = jnp.zeros(r.shape, F32)

        shift1, scale1, gate1 = _mod_rows(mod_ref, modb_ref, 0, 3)
        dx1_t = dx1_ref[...]
        do1 = (dx1_t * gate1).astype(MM_DTYPE)
        d_yan = _dot_nt(do1, wout_ref[0:DA, :])
        d_ybn = _dot_nt(do1, wout_ref[DA:D, :])

        z_parts = [z_ref[:, q * PW_IN:(q + 1) * PW_IN] for q in range(NQ)]
        u, v, val, gate = z_parts
        lng = lng_ref[...]
        r = _mixer_recompute(z_parts, lng, lnb_ref[...], wm_ref, bst_ref, mix_ref)
        gng = gng_ref[...]
        t = _conv_branch_tail(yb1_ref[...], gng, gnb_ref[...])
        y_a = r["gu"] * r["mixed"]
        y_b = t["y_b"]
        ga_v, gb_v = ga_ref[...], gb_ref[...]
        ra = lax.rsqrt(_rowsum(y_a * y_a) * (1.0 / DA) + EPS)
        rb = lax.rsqrt(_rowsum(y_b * y_b) * (1.0 / DB) + EPS)

        a5_ref[0:1, :] += _colsum(d_yan * y_a * ra)
        a5_ref[1:2, :] += _colsum(d_ybn * y_b * rb)
        ta = d_yan * ga_v
        d_ya = ra * ta - y_a * (ra * ra * ra) * (_rowsum(ta * y_a) * (1.0 / DA))
        tb = d_ybn * gb_v
        d_yb = rb * tb - y_b * (rb * rb * rb) * (_rowsum(tb * y_b) * (1.0 / DB))

        d_u = d_ya * r["mixed"] * r["dgu"]
        d_mixed = d_ya * r["gu"]
        dmb = d_mixed.astype(MM_DTYPE)
        lo_mask = lax.broadcasted_iota(jnp.int32, (CHUNK, LANES), 1) < HALF
        zero_blk = jnp.zeros((CHUNK, LANES), MM_DTYPE)
        sb_acc = jnp.zeros((CHUNK, DA), F32)
        for ck in range(ts // CHUNK):
            rows = slice(ck * CHUNK, (ck + 1) * CHUNK)
            sb_acc = sb_acc + d_mixed[rows, :]
            for jb in range(DA // LANES):
                cols = slice(jb * LANES, (jb + 1) * LANES)
                dm_blk = dmb[rows, cols]
                vl_blk = r["vlb"][rows, cols]
                da_ = _dot(wmt_ref[2 * jb], dm_blk)
                db_ = _dot(wmt_ref[2 * jb + 1], dm_blk)
                dvl_ref[rows, cols] = jnp.where(lo_mask, da_, db_)
                asw_ref[2 * jb] += _dot_nt(jnp.where(lo_mask, dm_blk, zero_blk), vl_blk)
                asw_ref[2 * jb + 1] += _dot_nt(jnp.where(lo_mask, zero_blk, dm_blk), vl_blk)
        asb_ref[...] += sb_acc
        d_vl = dvl_ref[...]
        a5_ref[2:3, :] += _colsum(d_vl * r["vhat"])
        a5_ref[3:4, :] += _colsum(d_vl)
        dvh = d_vl * lng
        d_gv = r["rs"] * (dvh - _rowsum(dvh) * (1.0 / DA) - r["vhat"] * (_rowsum(dvh * r["vhat"]) * (1.0 / DA)))
        d_v = d_gv * r["dgv"]

        yb2, s2 = t["yb2"], t["s2"]
        d_yb2 = d_yb * (s2 * (1.0 + yb2 * (1.0 - s2)))
        a5_ref[4:5, :] += _colsum(d_yb2 * t["ghat"])
        a5_ref[5:6, :] += _colsum(d_yb2)
        dgh = d_yb2 * gng
        d_yb1 = t["grs"] * (dgh - _group_sum(dgh) * (1.0 / HALF) - t["ghat"] * (_group_sum(dgh * t["ghat"]) * (1.0 / HALF)))
        a5_ref[6:7, :] += _colsum(d_yb1)
        e_ref[0:ts, :] = d_yb1
        _shifted_copies(e_ref, sh_ref)
        mix_ref[...] = r["yb0"]
        _conv_taps(e_ref, sh_ref, cw_ref, dvl_ref, ts, 0, True, other_ref=mix_ref, tap_acc_ref=acw_ref)
        d_yb0 = dvl_ref[...]
        e_ref[ts:ts + HALO, :] = e_ref[0:HALO, :]
        sg = r["sg"]
        d_val = d_yb0 * sg
        d_gate = d_yb0 * val * sg * (1.0 - sg)

        dh = jnp.zeros((ts, D), F32)
        for q, dzq in enumerate((d_u, d_v, d_val, d_gate)):
            a2_ref[0:1, q * PW_IN:(q + 1) * PW_IN] += _colsum(dzq)
            dzb = dzq.astype(MM_DTYPE)
            dz_ref[:, q * PW_IN:(q + 1) * PW_IN] = dzb
            dh = dh + _dot_nt(dzb, w4_ref[q])
        x_t = x_ref[...]
        r1 = lax.rsqrt(_rowsum(x_t * x_t) * (1.0 / D) + EPS)
        g1 = g1_ref[...]
        xr = x_t * r1
        n1 = xr * g1
        h_ref[...] = (n1 * (1.0 + scale1) + shift1).astype(MM_DTYPE)
        dn1 = dh * (1.0 + scale1)
        t1 = dn1 * g1
        gx_ref[...] = dx1_t + r1 * t1 - x_t * (r1 * r1 * r1) * (_rowsum(t1 * x_t) * (1.0 / D))
        a1_ref[0:1, :] += _colsum(dh)
        a1_ref[1:2, :] += _colsum(dh * n1)
        a1_ref[2:3, :] += _colsum(dn1 * xr)

        @pl.when(i == nt - 1)
        def _():
            asb_ref[...] = _group_sum(asb_ref[...])
            finish()

    row = lambda w: pl.BlockSpec((ts, w), lambda i: (nt - 1 - i, 0))
    full = lambda a: pl.BlockSpec(a.shape, lambda i: (0,) * a.ndim)
    keep = lambda shape: pl.BlockSpec(shape, lambda i: (0,) * len(shape))
    return pl.pallas_call(
        body, name="mixer_bwd", grid=(nt,),
        in_specs=[row(D), row(D), row(4 * PW_IN), row(DB), full(mod), full(mods), full(norm1_g), _whole_vmem(),
                  full(ln_g), full(ln_b), _whole_vmem(), _whole_vmem(), full(bst), full(conv_w), full(gn_g), full(gn_b),
                  full(ga), full(gb), _whole_vmem()] + [_any()] * n_cs,
        out_specs=[row(D), row(4 * PW_IN), row(D), keep((8, D)), keep((8, 4 * PW_IN)), keep((8, DA)),
                   keep((HALO, DB)), keep((N_HEADS, CHUNK, CHUNK)), keep((CHUNK, DA))] + [_any()] * n_cs,
        out_shape=[SDS((s, D), F32), SDS((s, 4 * PW_IN), MM_DTYPE), SDS((s, D), MM_DTYPE), SDS((8, D), F32),
                   SDS((8, 4 * PW_IN), F32), SDS((8, DA), F32), SDS((HALO, DB), F32),
                   SDS((N_HEADS, CHUNK, CHUNK), F32), SDS((CHUNK, DA), F32)]
        + [_scattered_shape(p) for p in partials],
        scratch_shapes=[pltpu.VMEM((ts + HALO, DB), F32), pltpu.VMEM((7, ts + HALO - 8, DB), F32),
                        pltpu.VMEM((ts, DA), F32), pltpu.VMEM((ts, DA), F32),
                        pltpu.SemaphoreType.DMA((n_cs, 7)), pltpu.SemaphoreType.DMA((n_cs, 7)),
                        pltpu.SemaphoreType.DMA((n_cs,))],
        compiler_params=_params("arbitrary"),
    )(dx1, x, z, yb1, mod, mods, norm1_g, w_in4, ln_g, ln_b, wm, wmt, bst, conv_w, gn_g, gn_b, ga, gb, w_out, *partials)


def _gather8_plan(x_ref, out_ref, send_sems, recv_sems, local_sem):
    x, y, c = _place()
    me, sibling = (x, y, c), (x, y, 1 - c)
    chips = _other_chips(x, y)

    def copy(k, block, to, src=None):
        dst = out_ref.at[4 * block[0] + 2 * block[1] + block[2]]
        return pltpu.make_async_remote_copy(src_ref=dst if src is None else src, dst_ref=dst, send_sem=send_sems.at[k],
                                            recv_sem=recv_sems.at[k], device_id=to, device_id_type=MESH)

    def own():
        return pltpu.make_async_copy(x_ref, out_ref.at[4 * x + 2 * y + c], local_sem)

    def send():
        own().start()
        copy(0, me, sibling, src=x_ref).start()
        for j, chip in enumerate(chips):
            copy(1 + j, me, (*chip, c), src=x_ref).start()

    def forward():
        for j, chip in enumerate(chips):
            copy(1 + j, (*chip, c), me).wait_recv()
            copy(4 + j, (*chip, c), sibling).start()

    def finish():
        copy(0, sibling, me).wait_recv()
        for j, chip in enumerate(chips):
            copy(4 + j, (*chip, 1 - c), me).wait_recv()
        for k in range(7):
            copy(k, me, sibling).wait_send()
        own().wait()

    return send, forward, finish


def _grad_matmul(name, a, b, ka_tile, nb_tile, piece_w=None, gated=None, gather_blk=None):
    s, ka = a.shape
    nb = b.shape[1]
    ts = min(GRAD_ROW_TILE, s)
    nt = s // ts
    nja, njb = ka // ka_tile, nb // nb_tile
    steps = nja * njb * nt
    n_in = 2 + (2 if gated else 0) + (1 if gather_blk is not None else 0)
    n_out = 1 + (1 if gated else 0) + (1 if gather_blk is not None else 0)
    assert not (gated and njb != 1) and not (piece_w and nja != 1)

    def body(*refs):
        ins, outs, scratch = refs[:n_in], refs[n_in:n_in + n_out], refs[n_in + n_out:]
        a_ref, b_ref, o_ref, acc_ref = ins[0], ins[1], outs[0], scratch[0]
        step = (pl.program_id(0) * njb + pl.program_id(1)) * nt + pl.program_id(2)
        if gather_blk is not None:
            send, forward, finish = _gather8_plan(ins[-1], outs[-1], *scratch[1:])

            @pl.when(step == 0)
            def _():
                send()

            @pl.when(step == (3 * steps) // 4)
            def _():
                forward()

        prod = _dot_tn(a_ref[...].astype(MM_DTYPE), b_ref[...].astype(MM_DTYPE))

        @pl.when(pl.program_id(2) == 0)
        def _():
            acc_ref[...] = prod

        @pl.when(pl.program_id(2) > 0)
        def _():
            acc_ref[...] += prod

        @pl.when(pl.program_id(2) == nt - 1)
        def _():
            gm = acc_ref[...]
            if gated:
                gate_ref, w_ref, dg_ref = ins[2], ins[3], outs[1]

                @pl.when(step == nt - 1)
                def _():
                    dg_ref[...] = jnp.zeros(dg_ref.shape, F32)

                dg_ref[0:1, :] += _colsum(gm * w_ref[...].astype(F32))
                gm = gm * gate_ref[...]
            if piece_w:
                for q in range(nb_tile // piece_w):
                    o_ref[q] = gm[:, q * piece_w:(q + 1) * piece_w].astype(WIRE_DTYPE)
            else:
                o_ref[...] = gm.astype(WIRE_DTYPE)

        if gather_blk is not None:
            @pl.when(step == steps - 1)
            def _():
                finish()

    in_specs = [pl.BlockSpec((ts, ka_tile), lambda ja, jb, i: (i, ja)),
                pl.BlockSpec((ts, nb_tile), lambda ja, jb, i: (i, jb))]
    operands = [a, b]
    if piece_w:
        out_shape = [SDS((nb // piece_w, ka, piece_w), WIRE_DTYPE)]
        out_specs = [pl.BlockSpec((nb_tile // piece_w, ka, piece_w), lambda ja, jb, i: (jb, 0, 0))]
    else:
        out_shape = [SDS((ka, nb), WIRE_DTYPE)]
        out_specs = [pl.BlockSpec((ka_tile, nb_tile), lambda ja, jb, i: (ja, jb))]
    scratch = [pltpu.VMEM((ka_tile, nb_tile), F32)]
    if gated:
        in_specs += [pl.BlockSpec((1, nb_tile), lambda ja, jb, i: (0, jb)),
                     pl.BlockSpec((ka_tile, nb_tile), lambda ja, jb, i: (ja, jb))]
        operands += list(gated)
        out_shape.append(SDS((8, nb), F32))
        out_specs.append(pl.BlockSpec((8, nb_tile), lambda ja, jb, i: (0, jb)))
    if gather_blk is not None:
        in_specs.append(_any())
        operands.append(gather_blk)
        out_shape.append(SDS((8,) + gather_blk.shape, gather_blk.dtype))
        out_specs.append(_any())
        scratch += [pltpu.SemaphoreType.DMA((7,)), pltpu.SemaphoreType.DMA((7,)), pltpu.SemaphoreType.DMA]
    return pl.pallas_call(
        body, name=name, grid=(nja, njb, nt), in_specs=in_specs, out_specs=out_specs, out_shape=out_shape,
        scratch_shapes=scratch, compiler_params=_params("arbitrary", "arbitrary", "arbitrary"),
    )(*operands)


COND_COLS = 512


def _cond_partial(c_all, w_a, w_f):
    na = w_a.shape[1]

    def body(c_ref, wa_ref, wf_ref, oa_ref, of_ref):
        c_t = c_ref[...]
        ca = (c_t * _sigmoid(c_t)).astype(MM_DTYPE)
        oa_ref[...] = _dot(ca, wa_ref[...].astype(MM_DTYPE))

        @pl.when(pl.program_id(0) == 0)
        def _():
            of_ref[...] = _dot(ca, wf_ref[...].astype(MM_DTYPE))

    keep = lambda shape: pl.BlockSpec(shape, lambda j: (0, 0))
    return pl.pallas_call(
        body, name="cond_partial", grid=(na // COND_COLS,),
        in_specs=[keep((8, D)), pl.BlockSpec((D, COND_COLS), lambda j: (0, j)), keep(w_f.shape)],
        out_specs=[pl.BlockSpec((8, COND_COLS), lambda j: (0, j)), keep((8, w_f.shape[1]))],
        out_shape=[SDS((8, na), F32), SDS((8, w_f.shape[1]), F32)],
        compiler_params=_params("arbitrary"),
    )(c_all, w_a, w_f)


def _cond_grad(c_all, dmod_a, dmod_f):
    na = dmod_a.shape[1]

    def body(c_ref, da_ref, df_ref, oa_ref, of_ref):
        c_t = c_ref[...]
        ca = jnp.concatenate([c_t * _sigmoid(c_t), jnp.zeros((8, D), F32)], axis=0).astype(MM_DTYPE)

        def outer(d_ref):
            dm = jnp.concatenate([d_ref[...], jnp.zeros(d_ref.shape, F32)], axis=0).astype(MM_DTYPE)
            return _dot_tn(ca, dm)

        oa_ref[...] = outer(da_ref)

        @pl.when(pl.program_id(0) == 0)
        def _():
            of_ref[...] = outer(df_ref)

    keep = lambda shape: pl.BlockSpec(shape, lambda j: (0, 0))
    return pl.pallas_call(
        body, name="cond_grad", grid=(na // COND_COLS,),
        in_specs=[keep((8, D)), pl.BlockSpec((8, COND_COLS), lambda j: (0, j)), keep(dmod_f.shape)],
        out_specs=[pl.BlockSpec((D, COND_COLS), lambda j: (0, j)), keep((D, dmod_f.shape[1]))],
        out_shape=[SDS((D, na), F32), SDS((D, dmod_f.shape[1]), F32)],
        compiler_params=_params("arbitrary"),
    )(c_all, dmod_a, dmod_f)


def _row_tile(rows, cap=256):
    if rows <= cap:
        return rows
    for t in range(cap, 7, -8):
        if rows % t == 0:
            return t
    return rows


def _ordered_sum(name, parts, into_half=None):
    n, rows, cols = parts.shape
    rt = _row_tile(rows)
    nb = rows // rt

    def body(*refs):
        p_ref, o_ref = refs[-2:]
        acc = p_ref[0].astype(F32)
        for k in range(1, n):
            acc = acc + p_ref[k].astype(F32)
        o_ref[...] = acc

    if into_half is None:
        return pl.pallas_call(
            body, name=name, grid=(nb,),
            in_specs=[pl.BlockSpec((n, rt, cols), lambda i: (0, i, 0))],
            out_specs=pl.BlockSpec((rt, cols), lambda i: (i, 0)), out_shape=SDS((rows, cols), F32),
            compiler_params=_params("parallel"),
        )(parts)
    grid_spec = pltpu.PrefetchScalarGridSpec(
        num_scalar_prefetch=1, grid=(nb,),
        in_specs=[pl.BlockSpec((n, rt, cols), lambda i, c_ref: (0, i, 0))],
        out_specs=pl.BlockSpec((rt, cols), lambda i, c_ref: (c_ref[0] * nb + i, 0)))
    return pl.pallas_call(
        body, name=name, grid_spec=grid_spec, out_shape=SDS((2 * rows, cols), F32),
        compiler_params=_params("parallel"),
    )(into_half.astype(jnp.int32).reshape(1), parts)


def _adamw_update(w_ref, g_ref, m_ref, v_ref, d_ref, nm_ref, nv_ref):
    c1 = 1.0 - ADAM_B1 ** ADAM_STEP
    c2 = 1.0 - ADAM_B2 ** ADAM_STEP
    g_t = g_ref[...]
    m_new = ADAM_B1 * m_ref[...] + (1.0 - ADAM_B1) * g_t
    v_new = ADAM_B2 * v_ref[...] + (1.0 - ADAM_B2) * (g_t * g_t)
    nm_ref[...] = m_new
    nv_ref[...] = v_new
    d_ref[...] = -ADAM_LR * ((m_new / c1) / (jnp.sqrt(v_new / c2) + ADAM_EPS) + ADAM_WD * w_ref[...])


def _adamw_many(name, ws, gs, ms, vs):
    n = len(ws)

    def body(*refs):
        ins, outs = refs[:4 * n], refs[4 * n:]
        for k in range(n):
            _adamw_update(ins[k], ins[n + k], ins[2 * n + k], ins[3 * n + k], *outs[3 * k:3 * k + 3])

    return pl.pallas_call(
        body, name=name, out_shape=[SDS(w.shape, F32) for w in ws for _ in range(3)],
        compiler_params=pltpu.CompilerParams(vmem_limit_bytes=VMEM_LIMIT),
    )(*ws, *gs, *ms, *vs)


def _adamw_tiled(name, ws, gs, ms, vs):
    n = len(ws)

    def body(*refs):
        ins, outs = refs[:4 * n], refs[4 * n:]
        for k in range(n):
            _adamw_update(ins[k], ins[n + k], ins[2 * n + k], ins[3 * n + k], *outs[3 * k:3 * k + 3])

    specs = [pl.BlockSpec((w.shape[0] // ADAMW_STEPS, w.shape[1]), lambda i: (i, 0)) for w in ws]
    return pl.pallas_call(
        body, name=name, grid=(ADAMW_STEPS,), in_specs=specs * 4, out_specs=[s for s in specs for _ in range(3)],
        out_shape=[SDS(w.shape, F32) for w in ws for _ in range(3)], compiler_params=_params("parallel"),
    )(*ws, *gs, *ms, *vs)


def _place():
    return lax.axis_index("x"), lax.axis_index("y"), lax.axis_index("c")


def _other_chips(x, y):
    return [(1 - x, y), (x, 1 - y), (1 - x, 1 - y)]


def _all_gather8(name, blk):
    m, n = blk.shape

    def body(x_ref, out_ref, send_sems, recv_sems, local_sem):
        x, y, c = _place()
        me, sibling = (x, y, c), (x, y, 1 - c)
        chips = _other_chips(x, y)

        def slot(px, py, pc):
            return out_ref.at[4 * px + 2 * py + pc]

        def copy(k, block, to, src=None):
            return pltpu.make_async_remote_copy(
                src_ref=slot(*block) if src is None else src, dst_ref=slot(*block),
                send_sem=send_sems.at[k], recv_sem=recv_sems.at[k], device_id=to, device_id_type=MESH)

        mine = pltpu.make_async_copy(x_ref, slot(*me), local_sem)
        mine.start()
        first = [copy(0, me, sibling, src=x_ref)]
        first += [copy(1 + j, me, (*chip, c), src=x_ref) for j, chip in enumerate(chips)]
        for cp in first:
            cp.start()
        passed = [copy(4 + j, (*chip, c), sibling) for j, chip in enumerate(chips)]
        for j, chip in enumerate(chips):
            copy(1 + j, (*chip, c), me).wait_recv()
            passed[j].start()
        copy(0, sibling, me).wait_recv()
        for j, chip in enumerate(chips):
            copy(4 + j, (*chip, 1 - c), me).wait_recv()
        for cp in first + passed:
            cp.wait_send()
        mine.wait()

    return pl.pallas_call(
        body, name=name, out_shape=SDS((8, m, n), blk.dtype),
        in_specs=[_whole_vmem()], out_specs=_whole_vmem(),
        scratch_shapes=[pltpu.SemaphoreType.DMA((7,)), pltpu.SemaphoreType.DMA((7,)), pltpu.SemaphoreType.DMA],
        compiler_params=pltpu.CompilerParams(vmem_limit_bytes=VMEM_LIMIT),
    )(blk)


def _any():
    return pl.BlockSpec(memory_space=pl.ANY)


def _gather_weights(shards, blk):
    n = len(shards)

    def body(*refs):
        ins, outs, sems = refs[:n + 1], refs[n + 1:2 * n + 2], refs[2 * n + 2:]
        send, forward, finish = _gather_plan(ins[:n], outs[:n], sems[0], sems[1])
        send8, forward8, finish8 = _gather8_plan(ins[n], outs[n], *sems[2:])
        send8()
        send()
        forward8()
        forward()
        finish8()
        finish()

    return pl.pallas_call(
        body, name="gather_weights",
        out_shape=[SDS((NQ,) + s.shape, s.dtype) for s in shards] + [SDS((8,) + blk.shape, blk.dtype)],
        in_specs=[_any()] * (n + 1), out_specs=[_any()] * (n + 1),
        scratch_shapes=[pltpu.SemaphoreType.DMA((n, 6)), pltpu.SemaphoreType.DMA((n, 6)),
                        pltpu.SemaphoreType.DMA((7,)), pltpu.SemaphoreType.DMA((7,)), pltpu.SemaphoreType.DMA],
    )(*shards, blk)


def _own_piece(gathered, shard):
    myq = 2 * lax.axis_index("x") + lax.axis_index("y")
    return lax.dynamic_update_slice(gathered, shard[None], (myq,) + (0,) * shard.ndim)


def _scatter_to_owners(partials):
    n = len(partials)

    def body(*refs):
        send, finish = _scatter_plan(refs[:n], refs[n:2 * n], *refs[2 * n:])
        send()
        finish()

    return pl.pallas_call(
        body, name="scatter_to_owners",
        out_shape=[_scattered_shape(p) for p in partials],
        in_specs=[_any()] * n, out_specs=[_any()] * n,
        scratch_shapes=[pltpu.SemaphoreType.DMA((n, 7)), pltpu.SemaphoreType.DMA((n, 7))],
    )(*partials)


def _owner_sums(tag, arrived, partials=None):
    x, y, c = _place()
    sums = []
    for w, arr in enumerate(arrived):
        if partials is not None:
            part = partials[w]
            h = part.shape[1] // 2
            own = lax.dynamic_slice(part, (2 * x + y, c * h, 0), (1, h, part.shape[2]))
            arr = lax.dynamic_update_slice(arr, own, (4 * x + 2 * y + c, 0, 0))
        sums.append(_ordered_sum(f"owner_sum_{tag}_{w}", arr, into_half=c))
    return sums


def _join_halves(bufs):
    n = len(bufs)

    def body(*refs):
        ins, outs = refs[:n], refs[n:2 * n]
        send_sems, recv_sems = refs[2 * n:]
        x, y, c = _place()
        cps = []
        for w in range(n):
            h = ins[w].shape[0] // 2
            mine = outs[w].at[pl.ds(c * h, h)]
            cp = pltpu.make_async_remote_copy(src_ref=mine, dst_ref=mine, send_sem=send_sems.at[w],
                                              recv_sem=recv_sems.at[w], device_id=(x, y, 1 - c), device_id_type=MESH)
            cp.start()
            cps.append(cp)
        for cp in cps:
            cp.wait()

    return pl.pallas_call(
        body, name="join_halves",
        out_shape=[SDS(b.shape, b.dtype) for b in bufs],
        in_specs=[_any()] * n, out_specs=[_any()] * n, input_output_aliases={w: w for w in range(n)},
        scratch_shapes=[pltpu.SemaphoreType.DMA((n,)), pltpu.SemaphoreType.DMA((n,))],
    )(*bufs)


def _pad_rows(a, rows):
    return jnp.pad(a, ((0, rows - a.shape[0]),) + ((0, 0),) * (a.ndim - 1))


def _pack_small(dmod, g1, g2, gf, b_in, ln_g, ln_b, conv_b, gn_g, gn_b, ga, gb, sb, cw32, sw, loss_row):
    v512 = jnp.concatenate([ln_g, ln_b, conv_b, gn_g, gn_b, ga, gb, jnp.zeros((1, DA), F32)], axis=1).reshape(4, D)
    rows = [dmod.reshape(8, D), g1, g2, gf, b_in.reshape(2, D), v512, sb.reshape(1, D), cw32.reshape(16, D),
            sw.reshape(CHUNK, D), loss_row]
    packed = jnp.concatenate(rows, axis=0)
    return _pad_rows(packed, PK_ROWS)


def _unpack_small(p):
    v512 = p[PK_V512:PK_V512 + 4].reshape(1, 8 * DA)
    pieces = [v512[:, k * DA:(k + 1) * DA] for k in range(7)]
    return dict(
        dmod=p[PK_DMOD:PK_DMOD + 8].reshape(1, 8 * D), norm1_g=p[PK_G1:PK_G1 + 1], norm2_g=p[PK_G2:PK_G2 + 1],
        norm_f_g=p[PK_GF:PK_GF + 1], b_in=p[PK_BIN:PK_BIN + 2].reshape(1, 2 * D),
        a_ln_g=pieces[0], a_ln_b=pieces[1], b_conv_b=pieces[2], b_gn_g=pieces[3], b_gn_b=pieces[4],
        out_norm_a_g=pieces[5], out_norm_b_g=pieces[6],
        a_spatial_b=p[PK_SB:PK_SB + 1].reshape(N_HEADS, CHUNK),
        b_conv_w=p[PK_CW:PK_CW + 16].reshape(HALO, DB),
        a_spatial_w=p[PK_SW:PK_SW + CHUNK].reshape(N_HEADS, CHUNK, CHUNK))


def kernel(x, c, ada_w, ada_b, norm1_g, w_in, b_in, a_ln_g, a_ln_b, a_spatial_w, a_spatial_b, b_conv_w, b_conv_b, b_gn_g, b_gn_b, out_norm_a_g, out_norm_b_g, w_out, norm2_g, w_ffn_in, w_ffn_out, ada_f_w, ada_f_b, norm_f_g, loss_target, m_ada_w, m_ada_b, m_norm1_g, m_w_in, m_b_in, m_a_ln_g, m_a_ln_b, m_a_spatial_w, m_a_spatial_b, m_b_conv_w, m_b_conv_b, m_b_gn_g, m_b_gn_b, m_out_norm_a_g, m_out_norm_b_g, m_w_out, m_norm2_g, m_w_ffn_in, m_w_ffn_out, m_ada_f_w, m_ada_f_b, m_norm_f_g, v_ada_w, v_ada_b, v_norm1_g, v_w_in, v_b_in, v_a_ln_g, v_a_ln_b, v_a_spatial_w, v_a_spatial_b, v_b_conv_w, v_b_conv_b, v_b_gn_g, v_b_gn_b, v_out_norm_a_g, v_out_norm_b_g, v_w_out, v_norm2_g, v_w_ffn_in, v_w_ffn_out, v_ada_f_w, v_ada_f_b, v_norm_f_g):
    mx, my, mc = _place()
    me = 4 * mx + 2 * my + mc
    myq = 2 * mx + my
    xs = x[0]
    target = loss_target[0]
    s = xs.shape[0]
    n_ada = ada_w.shape[2]

    cw_shard = _pad_rows(b_conv_w[0], HALO)
    mix_shards = [w_in[0].astype(MM_DTYPE), w_out[0].astype(MM_DTYPE)]
    ffn_shards = [w_ffn_in[0].astype(MM_DTYPE), w_ffn_out[0].astype(MM_DTYPE)]
    w_in4, w_out4, first = _gather_weights(mix_shards, jnp.concatenate([c.reshape(8, LANES), cw_shard], axis=0))
    w_in4, w_out4 = _own_piece(w_in4, mix_shards[0]), _own_piece(w_out4, mix_shards[1])
    w_out_f = w_out4.reshape(D, D)
    c_all = first[:, 0:8, :].reshape(8, D)
    conv_w = jnp.concatenate([first[4 * (q // 2) + 2 * (q % 2), 8:8 + HALO, :] for q in range(NQ)], axis=1)
    cond_part = jnp.concatenate(_cond_partial(c_all, ada_w[0], ada_f_w), axis=1)
    cond_all = _all_gather8("gather_cond", cond_part)
    cond_q = [cond_all[4 * (q // 2) + 2 * (q % 2)] for q in range(NQ)]
    mod_all = jnp.concatenate([cq[:, :n_ada] for cq in cond_q] + [cq[:, n_ada:] for cq in cond_q], axis=1)
    mod = lax.dynamic_slice_in_dim(mod_all, me, 1, axis=0)
    mods = jnp.concatenate([ada_b, ada_f_b.reshape(1, 2 * D)], axis=1)

    causal = jnp.tril(jnp.ones((CHUNK, CHUNK), dtype=bool))
    wm_f = jnp.where(causal[None], a_spatial_w[0], 0.0)
    wm = wm_f.astype(MM_DTYPE)
    wmt = jnp.swapaxes(wm_f, 1, 2).astype(MM_DTYPE)
    bst = jnp.repeat(a_spatial_b[0].T, HALF, axis=1)

    z, x1, yb1, y, w_ffn_in4, w_ffn_out4 = _mixer_fwd(
        xs, mod, mods, norm1_g, w_in4, b_in, a_ln_g, a_ln_b, wm, bst, conv_w, b_conv_b, b_gn_g, b_gn_b,
        out_norm_a_g, out_norm_b_g, w_out_f, ffn_shards)
    w_ffn_out_f = w_ffn_out4.reshape(DFF, D)
    g, up, h2, dx2, acc_f = _ffn_fwd(x1, target, mod, mods, norm2_g, norm_f_g, w_ffn_in4, w_ffn_out_f)

    dff, a_act, dx1, acc_2 = _ffn_bwd(dx2, x1, g, up, mod, mods, norm2_g, w_ffn_in4, w_ffn_out_f)
    (gw_ffn_in4,) = _grad_matmul("grad_w_ffn_in", h2, dff, D, PW_FF, piece_w=PW_FF)
    modv = mod + mods
    gw_ffn_out, dgate2 = _grad_matmul("grad_w_ffn_out", a_act, dx2, PW_FF, D, gated=(modv[:, 5 * D:6 * D], w_ffn_out_f))
    gw_out, dgate1 = _grad_matmul("grad_w_out", y, dx1, D, D, gated=(modv[:, 2 * D:3 * D], w_out_f))
    early_partials = [gw_ffn_in4, gw_ffn_out.reshape(NQ, DFF // NQ, D), gw_out.reshape(NQ, D // NQ, D)]
    gx, dz, h, acc_1, acc_bin, acc_5, acc_cw, acc_sw, acc_sb, *early_arrived = _mixer_bwd(
        dx1, xs, z, yb1, mod, mods, norm1_g, w_in4, a_ln_g, a_ln_b, wm, wmt, bst, conv_w, b_gn_g, b_gn_b,
        out_norm_a_g, out_norm_b_g, w_out_f, early_partials)

    dmod = jnp.concatenate([acc_1[0:1], acc_1[1:2], dgate1[0:1], acc_2[0:1], acc_2[1:2], dgate2[0:1],
                            acc_f[0:1], acc_f[1:2]], axis=1)
    sw_grad = jnp.where(causal[None], acc_sw, 0.0)
    sb_grad = acc_sb[:, ::HALF].T
    packed = _pack_small(dmod, acc_1[2:3], acc_2[2:3], acc_f[2:3], acc_bin[0:1], acc_5[2:3], acc_5[3:4], acc_5[6:7],
                         acc_5[4:5], acc_5[5:6], acc_5[0:1], acc_5[1:2], sb_grad, acc_cw, sw_grad, acc_f[4:5])
    gw_in4, gathered = _grad_matmul("grad_w_in", h, dz, D, NQ * PW_IN, piece_w=PW_IN, gather_blk=packed)
    late_arrived = _scatter_to_owners([gw_in4])
    g_w_in, g_w_ffn_in, g_w_ffn_out, g_w_out = _join_halves(
        _owner_sums("late", late_arrived, [gw_in4]) + _owner_sums("early", early_arrived))
    summed = _ordered_sum("small_grad_sum", gathered)
    loss = summed[PK_LOSS, 0]
    small = _unpack_small(summed)
    dmod_all = gathered[:, PK_DMOD:PK_DMOD + 8, :].reshape(8, 8 * D)
    g_ada_w, g_ada_f_w = _cond_grad(c_all, lax.dynamic_slice_in_dim(dmod_all, myq * n_ada, n_ada, axis=1),
                                    lax.dynamic_slice_in_dim(dmod_all, 6 * D + myq * PW_IN, PW_IN, axis=1))

    grads = dict(
        ada_w=g_ada_w, ada_b=small["dmod"][:, :6 * D], norm1_g=small["norm1_g"], w_in=g_w_in,
        b_in=small["b_in"], a_ln_g=small["a_ln_g"], a_ln_b=small["a_ln_b"], a_spatial_w=small["a_spatial_w"],
        a_spatial_b=small["a_spatial_b"],
        b_conv_w=lax.dynamic_slice_in_dim(small["b_conv_w"], myq * LANES, LANES, axis=1)[:CONV_W],
        b_conv_b=small["b_conv_b"], b_gn_g=small["b_gn_g"], b_gn_b=small["b_gn_b"],
        out_norm_a_g=small["out_norm_a_g"], out_norm_b_g=small["out_norm_b_g"], w_out=g_w_out,
        norm2_g=small["norm2_g"], w_ffn_in=g_w_ffn_in, w_ffn_out=g_w_ffn_out, ada_f_w=g_ada_f_w,
        ada_f_b=small["dmod"][:, 6 * D:], norm_f_g=small["norm_f_g"])

    weights = dict(ada_w=ada_w, ada_b=ada_b, norm1_g=norm1_g, w_in=w_in, b_in=b_in, a_ln_g=a_ln_g, a_ln_b=a_ln_b,
                   a_spatial_w=a_spatial_w, a_spatial_b=a_spatial_b, b_conv_w=b_conv_w, b_conv_b=b_conv_b, b_gn_g=b_gn_g,
                   b_gn_b=b_gn_b, out_norm_a_g=out_norm_a_g, out_norm_b_g=out_norm_b_g, w_out=w_out, norm2_g=norm2_g,
                   w_ffn_in=w_ffn_in, w_ffn_out=w_ffn_out, ada_f_w=ada_f_w, ada_f_b=ada_f_b, norm_f_g=norm_f_g)
    m_in = dict(ada_w=m_ada_w, ada_b=m_ada_b, norm1_g=m_norm1_g, w_in=m_w_in, b_in=m_b_in, a_ln_g=m_a_ln_g, a_ln_b=m_a_ln_b,
                a_spatial_w=m_a_spatial_w, a_spatial_b=m_a_spatial_b, b_conv_w=m_b_conv_w, b_conv_b=m_b_conv_b,
                b_gn_g=m_b_gn_g, b_gn_b=m_b_gn_b, out_norm_a_g=m_out_norm_a_g, out_norm_b_g=m_out_norm_b_g, w_out=m_w_out,
                norm2_g=m_norm2_g, w_ffn_in=m_w_ffn_in, w_ffn_out=m_w_ffn_out, ada_f_w=m_ada_f_w, ada_f_b=m_ada_f_b,
                norm_f_g=m_norm_f_g)
    v_in = dict(ada_w=v_ada_w, ada_b=v_ada_b, norm1_g=v_norm1_g, w_in=v_w_in, b_in=v_b_in, a_ln_g=v_a_ln_g, a_ln_b=v_a_ln_b,
                a_spatial_w=v_a_spatial_w, a_spatial_b=v_a_spatial_b, b_conv_w=v_b_conv_w, b_conv_b=v_b_conv_b,
                b_gn_g=v_b_gn_g, b_gn_b=v_b_gn_b, out_norm_a_g=v_out_norm_a_g, out_norm_b_g=v_out_norm_b_g, w_out=v_w_out,
                norm2_g=v_norm2_g, w_ffn_in=v_w_ffn_in, w_ffn_out=v_w_ffn_out, ada_f_w=v_ada_f_w, ada_f_b=v_ada_f_b,
                norm_f_g=v_norm_f_g)
    names = list(weights)
    big = ("ada_w", "w_in", "w_out", "w_ffn_in", "w_ffn_out", "ada_f_w")

    def flat2(a):
        return a.reshape(-1, a.shape[-1])

    delta, new_m, new_v = {}, {}, {}
    for nm in big:
        grads[nm] = grads[nm].reshape(weights[nm].shape)
    big_out = _adamw_tiled("adamw_large", *[[flat2(tree[nm]) for nm in big] for tree in (weights, grads, m_in, v_in)])
    for k, nm in enumerate(big):
        shape = weights[nm].shape
        delta[nm], new_m[nm], new_v[nm] = [o.reshape(shape) for o in big_out[3 * k:3 * k + 3]]

    small_names = [nm for nm in names if nm not in big]
    for nm in small_names:
        grads[nm] = grads[nm].reshape(weights[nm].shape)
    small_out = _adamw_many("adamw_small", *[[flat2(tree[nm]) for nm in small_names] for tree in (weights, grads, m_in, v_in)])
    for k, nm in enumerate(small_names):
        shape = weights[nm].shape
        delta[nm], new_m[nm], new_v[nm] = [o.reshape(shape) for o in small_out[3 * k:3 * k + 3]]

    grad_x = gx.reshape(x.shape)
    return (loss, grad_x, *[grads[nm] for nm in names], *[delta[nm] for nm in names],
            *[new_m[nm] for nm in names], *[new_v[nm] for nm in names])
```

```python
import functools
import math

import jax
import jax.numpy as jnp
from jax import lax
from jax.experimental import pallas as pl
from jax.experimental.pallas import tpu as pltpu

F32 = jnp.float32
MM_DTYPE = jnp.bfloat16
WIRE_DTYPE = jnp.bfloat16
SDS = jax.ShapeDtypeStruct
MESH = pl.DeviceIdType.MESH

D = 1024
DA = 512
DB = 512
NQ = 4
PW_IN = 512
DFF = 2816
PW_FF = 1408
CHUNK = 128
N_HEADS = 8
CONV_W = 31
HALO = 32
CONV_ROWS = 64
EPS = 1e-6
LANES = 128
HALF = 64

ROW_TILE = 256
FWD_ROW_TILE = 512
FFN_ROW_TILE = 256
GRAD_ROW_TILE = 2048
ADAMW_STEPS = 8
VMEM_LIMIT = 60 * 1024 * 1024

ADAM_LR = 0.001
ADAM_B1 = 0.9
ADAM_B2 = 0.999
ADAM_EPS = 1e-08
ADAM_WD = 0.01
ADAM_STEP = 10

PK_DMOD = 0
PK_G1 = 8
PK_G2 = 9
PK_GF = 10
PK_BIN = 11
PK_V512 = 13
PK_SB = 17
PK_CW = 18
PK_SW = 34
PK_LOSS = 162
PK_ROWS = 168


def _dot(a, b):
    return jnp.dot(a, b, preferred_element_type=F32)


def _dot_nt(a, b):
    return lax.dot_general(a, b, (((1,), (1,)), ((), ())), preferred_element_type=F32)


def _dot_tn(a, b):
    return lax.dot_general(a, b, (((0,), (0,)), ((), ())), preferred_element_type=F32)


def _rowsum(x):
    return jnp.sum(x, axis=-1, keepdims=True)


def _colsum(x):
    return jnp.sum(x, axis=0, keepdims=True)


def _group_sum(x):
    rows, width = x.shape
    lo_mask = lax.broadcasted_iota(jnp.int32, (rows, LANES), 1) < HALF
    outs = []
    for jb in range(width // LANES):
        xb = x[:, jb * LANES:(jb + 1) * LANES]
        lo = _rowsum(jnp.where(lo_mask, xb, 0.0))
        hi = _rowsum(jnp.where(lo_mask, 0.0, xb))
        outs.append(jnp.where(lo_mask, lo, hi))
    return jnp.concatenate(outs, axis=-1)


def _sigmoid(x):
    return 1.0 / (1.0 + jnp.exp(-x))


def _gelu_parts(u):
    cdf = 0.5 * (1.0 + lax.erf(u * (1.0 / math.sqrt(2.0))))
    pdf = jnp.exp(-0.5 * u * u) * (1.0 / math.sqrt(2.0 * math.pi))
    return u * cdf, cdf + u * pdf


def _whole_vmem():
    return pl.BlockSpec(memory_space=pltpu.VMEM)


def _params(*semantics):
    return pltpu.CompilerParams(dimension_semantics=semantics, vmem_limit_bytes=VMEM_LIMIT)


def _mod_rows(mod_ref, modb_ref, first, count):
    m = mod_ref[...] + modb_ref[...]
    return [m[:, (first + k) * D:(first + k + 1) * D] for k in range(count)]


def _mixer_recompute(z_parts, lng, lnb, wm_ref, bst_ref, mix_ref):
    u, v, val, gate = z_parts
    rows = u.shape[0]
    gu, dgu = _gelu_parts(u)
    gv, dgv = _gelu_parts(v)
    mu = _rowsum(gv) * (1.0 / DA)
    vc = gv - mu
    rs = lax.rsqrt(_rowsum(vc * vc) * (1.0 / DA) + EPS)
    vhat = vc * rs
    vl = vhat * lng + lnb
    vlb = vl.astype(MM_DTYPE)
    lo_mask = lax.broadcasted_iota(jnp.int32, (CHUNK, LANES), 1) < HALF
    for ck in range(rows // CHUNK):
        for jb in range(DA // LANES):
            blk = vlb[ck * CHUNK:(ck + 1) * CHUNK, jb * LANES:(jb + 1) * LANES]
            a = _dot(wm_ref[2 * jb], blk)
            b = _dot(wm_ref[2 * jb + 1], blk)
            mix_ref[ck * CHUNK:(ck + 1) * CHUNK, jb * LANES:(jb + 1) * LANES] = (
                jnp.where(lo_mask, a, b) + bst_ref[:, jb * LANES:(jb + 1) * LANES])
    mixed = mix_ref[...]
    sg = _sigmoid(gate)
    yb0 = val * sg
    return dict(gu=gu, dgu=dgu, dgv=dgv, rs=rs, vhat=vhat, vlb=vlb, mixed=mixed, sg=sg, yb0=yb0)


def _conv_branch_tail(yb1, gng, gnb):
    gm = _group_sum(yb1) * (1.0 / HALF)
    gc = yb1 - gm
    grs = lax.rsqrt(_group_sum(gc * gc) * (1.0 / HALF) + EPS)
    ghat = gc * grs
    yb2 = ghat * gng + gnb
    s2 = _sigmoid(yb2)
    return dict(grs=grs, ghat=ghat, yb2=yb2, s2=s2, y_b=yb2 * s2)


def _shifted_copies(e_ref, sh_ref):
    n = sh_ref.shape[1]
    for b in range(1, 8):
        sh_ref[b - 1] = e_ref[pl.ds(b, n), :]


def _window(e_ref, sh_ref, offset, r0, nrows, cols):
    a, b = divmod(offset, 8)
    if b == 0:
        return e_ref[pl.ds(r0 + 8 * a, nrows), cols]
    return sh_ref[b - 1, pl.ds(r0 + 8 * a, nrows), cols]


def _conv_taps(e_ref, sh_ref, cw_ref, out_ref, ts, first_offset, flip, bias_ref=None, other_ref=None, tap_acc_ref=None):
    groups = CONV_ROWS // 8
    for cb in range(DB // LANES):
        cols = slice(cb * LANES, (cb + 1) * LANES)
        tap_acc = [jnp.zeros((8, LANES), F32) for _ in range(CONV_W)]
        for rb in range(ts // CONV_ROWS):
            r0 = rb * CONV_ROWS
            acc = jnp.zeros((CONV_ROWS, LANES), F32)
            if bias_ref is not None:
                acc = acc + bias_ref[:, cols]
            if other_ref is not None:
                other = other_ref[r0:r0 + CONV_ROWS, cols]
            for j in range(CONV_W):
                k = CONV_W - 1 - j if flip else j
                win = _window(e_ref, sh_ref, first_offset + j, r0, CONV_ROWS, cols)
                acc = acc + win * cw_ref[k:k + 1, cols]
                if other_ref is not None:
                    tap_acc[k] = tap_acc[k] + jnp.sum((other * win).reshape(groups, 8, LANES), axis=0)
            out_ref[r0:r0 + CONV_ROWS, cols] = acc
        if other_ref is not None:
            for k in range(CONV_W):
                tap_acc_ref[k:k + 1, cols] += _colsum(tap_acc[k])


def _gather_plan(ins, outs, send_sems, recv_sems, local_sems=None):
    x, y, c = _place()
    sibling = (x, y, 1 - c)
    chips = _other_chips(x, y)
    myq = 2 * x + y

    def copy(w, k, q, hc, to, src=None):
        rows = ins[w].shape[0]
        dst = outs[w].at[q, pl.ds(hc * (rows // 2), rows // 2)]
        return pltpu.make_async_remote_copy(
            src_ref=dst if src is None else src, dst_ref=dst,
            send_sem=send_sems.at[w, k], recv_sem=recv_sems.at[w, k], device_id=to, device_id_type=MESH)

    def own(w):
        return pltpu.make_async_copy(ins[w], outs[w].at[myq], local_sems.at[w])

    def send():
        for w in range(len(ins)):
            rows = ins[w].shape[0]
            src = ins[w].at[pl.ds(c * (rows // 2), rows // 2)]
            for j, chip in enumerate(chips):
                copy(w, j, myq, c, (*chip, c), src=src).start()
            if local_sems is not None:
                own(w).start()

    def forward():
        for w in range(len(ins)):
            for j, (qx, qy) in enumerate(chips):
                copy(w, j, 2 * qx + qy, c, sibling).wait_recv()
                copy(w, 3 + j, 2 * qx + qy, c, sibling).start()

    def finish():
        for w in range(len(ins)):
            for j, (qx, qy) in enumerate(chips):
                copy(w, 3 + j, 2 * qx + qy, 1 - c, sibling).wait_recv()
        for w in range(len(ins)):
            for k, (qx, qy) in enumerate(chips + chips):
                copy(w, k, 2 * qx + qy, c, sibling).wait_send()
            if local_sems is not None:
                own(w).wait()

    return send, forward, finish


def _mixer_fwd(x, mod, mods, norm1_g, w_in4, b_in, ln_g, ln_b, wm, bst, conv_w, conv_b, gn_g, gn_b, ga, gb, w_out,
               ffn_shards):
    s = x.shape[0]
    ts = min(FWD_ROW_TILE, s)
    nt = s // ts
    n_sh = len(ffn_shards)

    def body(x_ref, mod_ref, modb_ref, g1_ref, w4_ref, bin_ref, lng_ref, lnb_ref, wm_ref, bst_ref, cw_ref, cb_ref,
             gng_ref, gnb_ref, ga_ref, gb_ref, wout_ref, *rest):
        shard_refs, rest = rest[:n_sh], rest[n_sh:]
        z_ref, x1_ref, yb1_ref, y_ref = rest[:4]
        full_refs, rest = rest[4:4 + n_sh], rest[4 + n_sh:]
        e_ref, sh_ref, mix_ref, send_sems, recv_sems, local_sems = rest
        i = pl.program_id(0)
        send, forward, finish = _gather_plan(shard_refs, full_refs, send_sems, recv_sems, local_sems)

        @pl.when(i == 0)
        def _():
            send()
            e_ref[0:HALO, :] = jnp.zeros((HALO, DB), F32)

        @pl.when(i == (3 * nt) // 4)
        def _():
            forward()

        shift1, scale1, gate1 = _mod_rows(mod_ref, modb_ref, 0, 3)
        x_t = x_ref[...]
        r1 = lax.rsqrt(_rowsum(x_t * x_t) * (1.0 / D) + EPS)
        h = (x_t * r1 * g1_ref[...]) * (1.0 + scale1) + shift1
        hb = h.astype(MM_DTYPE)
        z_parts = []
        for q in range(NQ):
            zq = _dot(hb, w4_ref[q]) + bin_ref[:, q * PW_IN:(q + 1) * PW_IN]
            z_ref[:, q * PW_IN:(q + 1) * PW_IN] = zq
            z_parts.append(zq)
        r = _mixer_recompute(z_parts, lng_ref[...], lnb_ref[...], wm_ref, bst_ref, mix_ref)
        y_a = r["gu"] * r["mixed"]
        e_ref[HALO:HALO + ts, :] = r["yb0"]
        _shifted_copies(e_ref, sh_ref)
        _conv_taps(e_ref, sh_ref, cw_ref, yb1_ref, ts, HALO - (CONV_W - 1), False, bias_ref=cb_ref)
        e_ref[0:HALO, :] = e_ref[ts:ts + HALO, :]
        t = _conv_branch_tail(yb1_ref[...], gng_ref[...], gnb_ref[...])
        ra = lax.rsqrt(_rowsum(y_a * y_a) * (1.0 / DA) + EPS)
        rb = lax.rsqrt(_rowsum(t["y_b"] * t["y_b"]) * (1.0 / DB) + EPS)
        yan = (y_a * ra * ga_ref[...]).astype(MM_DTYPE)
        ybn = (t["y_b"] * rb * gb_ref[...]).astype(MM_DTYPE)
        y_ref[:, 0:DA] = yan
        y_ref[:, DA:D] = ybn
        o1 = _dot(yan, wout_ref[0:DA, :]) + _dot(ybn, wout_ref[DA:D, :])
        x1_ref[...] = x_t + gate1 * o1

        @pl.when(i == nt - 1)
        def _():
            finish()

    row = lambda w: pl.BlockSpec((ts, w), lambda i: (i, 0))
    full = lambda a: pl.BlockSpec(a.shape, lambda i: (0,) * a.ndim)
    return pl.pallas_call(
        body, name="mixer_fwd", grid=(nt,),
        in_specs=[row(D), full(mod), full(mods), full(norm1_g), _whole_vmem(), full(b_in), full(ln_g), full(ln_b),
                  _whole_vmem(), full(bst), full(conv_w), full(conv_b), full(gn_g), full(gn_b), full(ga), full(gb),
                  _whole_vmem()] + [_any()] * n_sh,
        out_specs=[row(4 * PW_IN), row(D), row(DB), row(D)] + [_any()] * n_sh,
        out_shape=[SDS((s, 4 * PW_IN), F32), SDS((s, D), F32), SDS((s, DB), F32), SDS((s, D), MM_DTYPE)]
        + [SDS((NQ,) + w.shape, w.dtype) for w in ffn_shards],
        scratch_shapes=[pltpu.VMEM((ts + HALO, DB), F32), pltpu.VMEM((7, ts + HALO - 8, DB), F32), pltpu.VMEM((ts, DA), F32),
                        pltpu.SemaphoreType.DMA((n_sh, 6)), pltpu.SemaphoreType.DMA((n_sh, 6)),
                        pltpu.SemaphoreType.DMA((n_sh,))],
        compiler_params=_params("arbitrary"),
    )(x, mod, mods, norm1_g, w_in4, b_in, ln_g, ln_b, wm, bst, conv_w, conv_b, gn_g, gn_b, ga, gb, w_out, *ffn_shards)


def _ffn_fwd(x1, target, mod, mods, norm2_g, norm_f_g, w_ffn_in4, w_ffn_out):
    s = x1.shape[0]
    sub_rows = min(FFN_ROW_TILE, s)
    ts = min(2 * sub_rows, s)
    nt = s // ts

    def body(x1_ref, tgt_ref, mod_ref, modb_ref, g2_ref, gf_ref, wf_ref, wo_ref,
             g_ref, up_ref, a_ref, h2_ref, dx2_ref, acc_ref):
        i = pl.program_id(0)

        @pl.when(i == 0)
        def _():
            acc_ref[...] = jnp.zeros(acc_ref.shape, F32)

        shift2, scale2, gate2, shift_f, scale_f = _mod_rows(mod_ref, modb_ref, 3, 5)
        for sub in range(ts // sub_rows):
            rows = slice(sub * sub_rows, (sub + 1) * sub_rows)
            x1_t = x1_ref[rows, :]
            r2 = lax.rsqrt(_rowsum(x1_t * x1_t) * (1.0 / D) + EPS)
            h2 = (x1_t * r2 * g2_ref[...]) * (1.0 + scale2) + shift2
            h2b = h2.astype(MM_DTYPE)
            h2_ref[rows, :] = h2b
            o2 = jnp.zeros((sub_rows, D), F32)
            for p in range(2):
                g = _dot(h2b, wf_ref[p])
                up = _dot(h2b, wf_ref[2 + p])
                g_ref[rows, p * PW_FF:(p + 1) * PW_FF] = g.astype(MM_DTYPE)
                up_ref[rows, p * PW_FF:(p + 1) * PW_FF] = up.astype(MM_DTYPE)
                a = (g * _sigmoid(g) * up).astype(MM_DTYPE)
                a_ref[rows, p * PW_FF:(p + 1) * PW_FF] = a
                o2 = o2 + _dot(a, wo_ref[p * PW_FF:(p + 1) * PW_FF, :])
            x2 = x1_t + gate2 * o2
            rf = lax.rsqrt(_rowsum(x2 * x2) * (1.0 / D) + EPS)
            gf = gf_ref[...]
            nf = x2 * rf * gf
            err = nf * (1.0 + scale_f) + shift_f - tgt_ref[rows, :]
            d_out = err * (1.0 / D)
            d_nf = d_out * (1.0 + scale_f)
            t = d_nf * gf
            dx2_ref[rows, :] = rf * t - x2 * (rf * rf * rf) * (_rowsum(t * x2) * (1.0 / D))
            acc_ref[0:1, :] += _colsum(d_out)
            acc_ref[1:2, :] += _colsum(d_out * nf)
            acc_ref[2:3, :] += _colsum(d_nf * x2 * rf)
            acc_ref[3:4, :] += _colsum(err * err)

        @pl.when(i == nt - 1)
        def _():
            acc_ref[4:5, :] = jnp.zeros((1, D), F32) + _rowsum(acc_ref[3:4, :]) * (0.5 / D)

    row = lambda w: pl.BlockSpec((ts, w), lambda i: (i, 0))
    full = lambda a: pl.BlockSpec(a.shape, lambda i: (0,) * a.ndim)
    return pl.pallas_call(
        body, name="ffn_fwd", grid=(nt,),
        in_specs=[row(D), row(D), full(mod), full(mods), full(norm2_g), full(norm_f_g), _whole_vmem(), _whole_vmem()],
        out_specs=[row(DFF), row(DFF), row(DFF), row(D), row(D), pl.BlockSpec((8, D), lambda i: (0, 0))],
        out_shape=[SDS((s, DFF), MM_DTYPE), SDS((s, DFF), MM_DTYPE), SDS((s, DFF), MM_DTYPE), SDS((s, D), MM_DTYPE),
                   SDS((s, D), F32), SDS((8, D), F32)],
        compiler_params=_params("arbitrary"),
    )(x1, target, mod, mods, norm2_g, norm_f_g, w_ffn_in4, w_ffn_out)


def _ffn_bwd(dx2, x1, g, up, mod, mods, norm2_g, w_ffn_in4, w_ffn_out):
    s = x1.shape[0]
    sub_rows = min(FFN_ROW_TILE, s)
    ts = min(2 * sub_rows, s)
    nt = s // ts

    def body(dx2_ref, x1_ref, g_ref, up_ref, mod_ref, modb_ref, g2_ref, wf_ref, wo_ref,
             dff_ref, dx1_ref, acc_ref):
        @pl.when(pl.program_id(0) == 0)
        def _():
            acc_ref[...] = jnp.zeros(acc_ref.shape, F32)

        shift2, scale2, gate2 = _mod_rows(mod_ref, modb_ref, 3, 3)
        for sub in range(ts // sub_rows):
            rows = slice(sub * sub_rows, (sub + 1) * sub_rows)
            dx2_t = dx2_ref[rows, :]
            do2 = (dx2_t * gate2).astype(MM_DTYPE)
            dh2 = jnp.zeros((sub_rows, D), F32)
            for p in range(2):
                da = _dot_nt(do2, wo_ref[p * PW_FF:(p + 1) * PW_FF, :])
                gp = g_ref[rows, p * PW_FF:(p + 1) * PW_FF].astype(F32)
                upp = up_ref[rows, p * PW_FF:(p + 1) * PW_FF].astype(F32)
                sg = _sigmoid(gp)
                silu = gp * sg
                dg = (da * upp * (sg * (1.0 + gp * (1.0 - sg)))).astype(MM_DTYPE)
                dup = (da * silu).astype(MM_DTYPE)
                dff_ref[rows, p * PW_FF:(p + 1) * PW_FF] = dg
                dff_ref[rows, DFF + p * PW_FF:DFF + (p + 1) * PW_FF] = dup
                dh2 = dh2 + _dot_nt(dg, wf_ref[p]) + _dot_nt(dup, wf_ref[2 + p])
            x1_t = x1_ref[rows, :]
            r2 = lax.rsqrt(_rowsum(x1_t * x1_t) * (1.0 / D) + EPS)
            g2 = g2_ref[...]
            xr = x1_t * r2
            dn2 = dh2 * (1.0 + scale2)
            t = dn2 * g2
            dx1_ref[rows, :] = dx2_t + r2 * t - x1_t * (r2 * r2 * r2) * (_rowsum(t * x1_t) * (1.0 / D))
            acc_ref[0:1, :] += _colsum(dh2)
            acc_ref[1:2, :] += _colsum(dh2 * (xr * g2))
            acc_ref[2:3, :] += _colsum(dn2 * xr)

    row = lambda w: pl.BlockSpec((ts, w), lambda i: (i, 0))
    full = lambda a: pl.BlockSpec(a.shape, lambda i: (0,) * a.ndim)
    return pl.pallas_call(
        body, name="ffn_bwd", grid=(nt,),
        in_specs=[row(D), row(D), row(DFF), row(DFF), full(mod), full(mods), full(norm2_g), _whole_vmem(), _whole_vmem()],
        out_specs=[row(2 * DFF), row(D), pl.BlockSpec((8, D), lambda i: (0, 0))],
        out_shape=[SDS((s, 2 * DFF), MM_DTYPE), SDS((s, D), F32), SDS((8, D), F32)],
        compiler_params=_params("arbitrary"),
    )(dx2, x1, g, up, mod, mods, norm2_g, w_ffn_in4, w_ffn_out)


def _scatter_plan(ins, outs, send_sems, recv_sems, local_sems=None):
    x, y, c = _place()
    me = 4 * x + 2 * y + c

    def copies():
        cps = []
        for w in range(len(ins)):
            h = ins[w].shape[1] // 2
            for k in range(1, 8):
                px, py, pc = (1 - x if k & 4 else x), (1 - y if k & 2 else y), (1 - c if k & 1 else c)
                cps.append(pltpu.make_async_remote_copy(
                    src_ref=ins[w].at[2 * px + py, pl.ds(pc * h, h)], dst_ref=outs[w].at[me],
                    send_sem=send_sems.at[w, k - 1], recv_sem=recv_sems.at[w, k - 1],
                    device_id=(px, py, pc), device_id_type=MESH))
        return cps

    def own():
        if local_sems is None:
            return []
        return [pltpu.make_async_copy(ins[w].at[2 * x + y, pl.ds(c * (ins[w].shape[1] // 2), ins[w].shape[1] // 2)],
                                      outs[w].at[me], local_sems.at[w]) for w in range(len(ins))]

    def send():
        for cp in copies() + own():
            cp.start()

    def finish():
        for cp in copies() + own():
            cp.wait()

    return send, finish


def _scattered_shape(partial):
    nq, rows, cols = partial.shape
    return SDS((8, rows // 2, cols), partial.dtype)


def _mixer_bwd(dx1, x, z, yb1, mod, mods, norm1_g, w_in4, ln_g, ln_b, wm, wmt, bst, conv_w, gn_g, gn_b, ga, gb, w_out,
               partials):
    s = x.shape[0]
    ts = min(ROW_TILE, s)
    nt = s // ts
    n_cs = len(partials)

    def body(dx1_ref, x_ref, z_ref, yb1_ref, mod_ref, modb_ref, g1_ref, w4_ref, lng_ref, lnb_ref, wm_ref, wmt_ref,
             bst_ref, cw_ref, gng_ref, gnb_ref, ga_ref, gb_ref, wout_ref, *rest):
        cs_refs, rest = rest[:n_cs], rest[n_cs:]
        gx_ref, dz_ref, h_ref, a1_ref, a2_ref, a5_ref, acw_ref, asw_ref, asb_ref = rest[:9]
        arrived_refs, rest = rest[9:9 + n_cs], rest[9 + n_cs:]
        e_ref, sh_ref, mix_ref, dvl_ref, send_sems, recv_sems, local_sems = rest
        i = pl.program_id(0)
        send, finish = _scatter_plan(cs_refs, arrived_refs, send_sems, recv_sems, local_sems)

        @pl.when(i == 0)
        def _():
            send()
            e_ref[ts:ts + HALO, :] = jnp.zeros((HALO, DB), F32)
            for r in (a1_ref, a2_ref, a5_ref, acw_ref, asw_ref, asb_ref):
                r[...] = jnp.zeros(r.shape, F32)

        shift1, scale1, gate1 = _mod_rows(mod_ref, modb_ref, 0, 3)
        dx1_t = dx1_ref[...]
        do1 = (dx1_t * gate1).astype(MM_DTYPE)
        d_yan = _dot_nt(do1, wout_ref[0:DA, :])
        d_ybn = _dot_nt(do1, wout_ref[DA:D, :])

        z_parts = [z_ref[:, q * PW_IN:(q + 1) * PW_IN] for q in range(NQ)]
        u, v, val, gate = z_parts
        lng = lng_ref[...]
        r = _mixer_recompute(z_parts, lng, lnb_ref[...], wm_ref, bst_ref, mix_ref)
        gng = gng_ref[...]
        t = _conv_branch_tail(yb1_ref[...], gng, gnb_ref[...])
        y_a = r["gu"] * r["mixed"]
        y_b = t["y_b"]
        ga_v, gb_v = ga_ref[...], gb_ref[...]
        ra = lax.rsqrt(_rowsum(y_a * y_a) * (1.0 / DA) + EPS)
        rb = lax.rsqrt(_rowsum(y_b * y_b) * (1.0 / DB) + EPS)

        a5_ref[0:1, :] += _colsum(d_yan * y_a * ra)
        a5_ref[1:2, :] += _colsum(d_ybn * y_b * rb)
        ta = d_yan * ga_v
        d_ya = ra * ta - y_a * (ra * ra * ra) * (_rowsum(ta * y_a) * (1.0 / DA))
        tb = d_ybn * gb_v
        d_yb = rb * tb - y_b * (rb * rb * rb) * (_rowsum(tb * y_b) * (1.0 / DB))

        d_u = d_ya * r["mixed"] * r["dgu"]
        d_mixed = d_ya * r["gu"]
        dmb = d_mixed.astype(MM_DTYPE)
        lo_mask = lax.broadcasted_iota(jnp.int32, (CHUNK, LANES), 1) < HALF
        zero_blk = jnp.zeros((CHUNK, LANES), MM_DTYPE)
        sb_acc = jnp.zeros((CHUNK, DA), F32)
        for ck in range(ts // CHUNK):
            rows = slice(ck * CHUNK, (ck + 1) * CHUNK)
            sb_acc = sb_acc + d_mixed[rows, :]
            for jb in range(DA // LANES):
                cols = slice(jb * LANES, (jb + 1) * LANES)
                dm_blk = dmb[rows, cols]
                vl_blk = r["vlb"][rows, cols]
                da_ = _dot(wmt_ref[2 * jb], dm_blk)
                db_ = _dot(wmt_ref[2 * jb + 1], dm_blk)
                dvl_ref[rows, cols] = jnp.where(lo_mask, da_, db_)
                asw_ref[2 * jb] += _dot_nt(jnp.where(lo_mask, dm_blk, zero_blk), vl_blk)
                asw_ref[2 * jb + 1] += _dot_nt(jnp.where(lo_mask, zero_blk, dm_blk), vl_blk)
        asb_ref[...] += sb_acc
        d_vl = dvl_ref[...]
        a5_ref[2:3, :] += _colsum(d_vl * r["vhat"])
        a5_ref[3:4, :] += _colsum(d_vl)
        dvh = d_vl * lng
        d_gv = r["rs"] * (dvh - _rowsum(dvh) * (1.0 / DA) - r["vhat"] * (_rowsum(dvh * r["vhat"]) * (1.0 / DA)))
        d_v = d_gv * r["dgv"]

        yb2, s2 = t["yb2"], t["s2"]
        d_yb2 = d_yb * (s2 * (1.0 + yb2 * (1.0 - s2)))
        a5_ref[4:5, :] += _colsum(d_yb2 * t["ghat"])
        a5_ref[5:6, :] += _colsum(d_yb2)
        dgh = d_yb2 * gng
        d_yb1 = t["grs"] * (dgh - _group_sum(dgh) * (1.0 / HALF) - t["ghat"] * (_group_sum(dgh * t["ghat"]) * (1.0 / HALF)))
        a5_ref[6:7, :] += _colsum(d_yb1)
        e_ref[0:ts, :] = d_yb1
        _shifted_copies(e_ref, sh_ref)
        mix_ref[...] = r["yb0"]
        _conv_taps(e_ref, sh_ref, cw_ref, dvl_ref, ts, 0, True, other_ref=mix_ref, tap_acc_ref=acw_ref)
        d_yb0 = dvl_ref[...]
        e_ref[ts:ts + HALO, :] = e_ref[0:HALO, :]
        sg = r["sg"]
        d_val = d_yb0 * sg
        d_gate = d_yb0 * val * sg * (1.0 - sg)

        dh = jnp.zeros((ts, D), F32)
        for q, dzq in enumerate((d_u, d_v, d_val, d_gate)):
            a2_ref[0:1, q * PW_IN:(q + 1) * PW_IN] += _colsum(dzq)
            dzb = dzq.astype(MM_DTYPE)
            dz_ref[:, q * PW_IN:(q + 1) * PW_IN] = dzb
            dh = dh + _dot_nt(dzb, w4_ref[q])
        x_t = x_ref[...]
        r1 = lax.rsqrt(_rowsum(x_t * x_t) * (1.0 / D) + EPS)
        g1 = g1_ref[...]
        xr = x_t * r1
        n1 = xr * g1
        h_ref[...] = (n1 * (1.0 + scale1) + shift1).astype(MM_DTYPE)
        dn1 = dh * (1.0 + scale1)
        t1 = dn1 * g1
        gx_ref[...] = dx1_t + r1 * t1 - x_t * (r1 * r1 * r1) * (_rowsum(t1 * x_t) * (1.0 / D))
        a1_ref[0:1, :] += _colsum(dh)
        a1_ref[1:2, :] += _colsum(dh * n1)
        a1_ref[2:3, :] += _colsum(dn1 * xr)

        @pl.when(i == nt - 1)
        def _():
            asb_ref[...] = _group_sum(asb_ref[...])
            finish()

    row = lambda w: pl.BlockSpec((ts, w), lambda i: (nt - 1 - i, 0))
    full = lambda a: pl.BlockSpec(a.shape, lambda i: (0,) * a.ndim)
    keep = lambda shape: pl.BlockSpec(shape, lambda i: (0,) * len(shape))
    return pl.pallas_call(
        body, name="mixer_bwd", grid=(nt,),
        in_specs=[row(D), row(D), row(4 * PW_IN), row(DB), full(mod), full(mods), full(norm1_g), _whole_vmem(),
                  full(ln_g), full(ln_b), _whole_vmem(), _whole_vmem(), full(bst), full(conv_w), full(gn_g), full(gn_b),
                  full(ga), full(gb), _whole_vmem()] + [_any()] * n_cs,
        out_specs=[row(D), row(4 * PW_IN), row(D), keep((8, D)), keep((8, 4 * PW_IN)), keep((8, DA)),
                   keep((HALO, DB)), keep((N_HEADS, CHUNK, CHUNK)), keep((CHUNK, DA))] + [_any()] * n_cs,
        out_shape=[SDS((s, D), F32), SDS((s, 4 * PW_IN), MM_DTYPE), SDS((s, D), MM_DTYPE), SDS((8, D), F32),
                   SDS((8, 4 * PW_IN), F32), SDS((8, DA), F32), SDS((HALO, DB), F32),
                   SDS((N_HEADS, CHUNK, CHUNK), F32), SDS((CHUNK, DA), F32)]
        + [_scattered_shape(p) for p in partials],
        scratch_shapes=[pltpu.VMEM((ts + HALO, DB), F32), pltpu.VMEM((7, ts + HALO - 8, DB), F32),
                        pltpu.VMEM((ts, DA), F32), pltpu.VMEM((ts, DA), F32),
                        pltpu.SemaphoreType.DMA((n_cs, 7)), pltpu.SemaphoreType.DMA((n_cs, 7)),
                        pltpu.SemaphoreType.DMA((n_cs,))],
        compiler_params=_params("arbitrary"),
    )(dx1, x, z, yb1, mod, mods, norm1_g, w_in4, ln_g, ln_b, wm, wmt, bst, conv_w, gn_g, gn_b, ga, gb, w_out, *partials)


def _gather8_plan(x_ref, out_ref, send_sems, recv_sems, local_sem):
    x, y, c = _place()
    me, sibling = (x, y, c), (x, y, 1 - c)
    chips = _other_chips(x, y)

    def copy(k, block, to, src=None):
        dst = out_ref.at[4 * block[0] + 2 * block[1] + block[2]]
        return pltpu.make_async_remote_copy(src_ref=dst if src is None else src, dst_ref=dst, send_sem=send_sems.at[k],
                                            recv_sem=recv_sems.at[k], device_id=to, device_id_type=MESH)

    def own():
        return pltpu.make_async_copy(x_ref, out_ref.at[4 * x + 2 * y + c], local_sem)

    def send():
        own().start()
        copy(0, me, sibling, src=x_ref).start()
        for j, chip in enumerate(chips):
            copy(1 + j, me, (*chip, c), src=x_ref).start()

    def forward():
        for j, chip in enumerate(chips):
            copy(1 + j, (*chip, c), me).wait_recv()
            copy(4 + j, (*chip, c), sibling).start()

    def finish():
        copy(0, sibling, me).wait_recv()
        for j, chip in enumerate(chips):
            copy(4 + j, (*chip, 1 - c), me).wait_recv()
        for k in range(7):
            copy(k, me, sibling).wait_send()
        own().wait()

    return send, forward, finish


def _grad_matmul(name, a, b, ka_tile, nb_tile, piece_w=None, gated=None, gather_blk=None):
    s, ka = a.shape
    nb = b.shape[1]
    ts = min(GRAD_ROW_TILE, s)
    nt = s // ts
    nja, njb = ka // ka_tile, nb // nb_tile
    steps = nja * njb * nt
    n_in = 2 + (2 if gated else 0) + (1 if gather_blk is not None else 0)
    n_out = 1 + (1 if gated else 0) + (1 if gather_blk is not None else 0)
    assert not (gated and njb != 1) and not (piece_w and nja != 1)

    def body(*refs):
        ins, outs, scratch = refs[:n_in], refs[n_in:n_in + n_out], refs[n_in + n_out:]
        a_ref, b_ref, o_ref, acc_ref = ins[0], ins[1], outs[0], scratch[0]
        ins = ins[2:]
        step = (pl.program_id(0) * njb + pl.program_id(1)) * nt + pl.program_id(2)
        if gather_blk is not None:
            send, forward, finish = _gather8_plan(ins[-1], outs[-1], *scratch[1:])

            @pl.when(step == 0)
            def _():
                send()

            @pl.when(step == (3 * steps) // 4)
            def _():
                forward()

        prod = _dot_tn(a_ref[...].astype(MM_DTYPE), b_ref[...].astype(MM_DTYPE))

        @pl.when(pl.program_id(2) == 0)
        def _():
            acc_ref[...] = prod

        @pl.when(pl.program_id(2) > 0)
        def _():
            acc_ref[...] += prod

        @pl.when(pl.program_id(2) == nt - 1)
        def _():
            gm = acc_ref[...]
            if gated:
                gate_ref, w_ref, dg_ref = ins[0], ins[1], outs[1]

                @pl.when(step == nt - 1)
                def _():
                    dg_ref[...] = jnp.zeros(dg_ref.shape, F32)

                dg_ref[0:1, :] += _colsum(gm * w_ref[...].astype(F32))
                gm = gm * gate_ref[...]
            if piece_w:
                for q in range(nb_tile // piece_w):
                    o_ref[q] = gm[:, q * piece_w:(q + 1) * piece_w].astype(WIRE_DTYPE)
            else:
                o_ref[...] = gm.astype(WIRE_DTYPE)

        if gather_blk is not None:
            @pl.when(step == steps - 1)
            def _():
                finish()

    in_specs = [pl.BlockSpec((ts, ka_tile), lambda ja, jb, i: (i, ja)),
                pl.BlockSpec((ts, nb_tile), lambda ja, jb, i: (i, jb))]
    operands = [a, b]
    if piece_w:
        out_shape = [SDS((nb // piece_w, ka, piece_w), WIRE_DTYPE)]
        out_specs = [pl.BlockSpec((nb_tile // piece_w, ka, piece_w), lambda ja, jb, i: (jb, 0, 0))]
    else:
        out_shape = [SDS((ka, nb), WIRE_DTYPE)]
        out_specs = [pl.BlockSpec((ka_tile, nb_tile), lambda ja, jb, i: (ja, jb))]
    scratch = [pltpu.VMEM((ka_tile, nb_tile), F32)]
    if gated:
        in_specs += [pl.BlockSpec((1, nb_tile), lambda ja, jb, i: (0, jb)),
                     pl.BlockSpec((ka_tile, nb_tile), lambda ja, jb, i: (ja, jb))]
        operands += list(gated)
        out_shape.append(SDS((8, nb), F32))
        out_specs.append(pl.BlockSpec((8, nb_tile), lambda ja, jb, i: (0, jb)))
    if gather_blk is not None:
        in_specs.append(_any())
        operands.append(gather_blk)
        out_shape.append(SDS((8,) + gather_blk.shape, gather_blk.dtype))
        out_specs.append(_any())
        scratch += [pltpu.SemaphoreType.DMA((7,)), pltpu.SemaphoreType.DMA((7,)), pltpu.SemaphoreType.DMA]
    return pl.pallas_call(
        body, name=name, grid=(nja, njb, nt), in_specs=in_specs, out_specs=out_specs, out_shape=out_shape,
        scratch_shapes=scratch, compiler_params=_params("arbitrary", "arbitrary", "arbitrary"),
    )(*operands)


COND_COLS = 512


def _cond_partial(c_all, w_a, w_f):
    na = w_a.shape[1]

    def body(c_ref, wa_ref, wf_ref, oa_ref, of_ref):
        c_t = c_ref[...]
        ca = (c_t * _sigmoid(c_t)).astype(MM_DTYPE)
        oa_ref[...] = _dot(ca, wa_ref[...].astype(MM_DTYPE))

        @pl.when(pl.program_id(0) == 0)
        def _():
            of_ref[...] = _dot(ca, wf_ref[...].astype(MM_DTYPE))

    keep = lambda shape: pl.BlockSpec(shape, lambda j: (0, 0))
    return pl.pallas_call(
        body, name="cond_partial", grid=(na // COND_COLS,),
        in_specs=[keep((8, D)), pl.BlockSpec((D, COND_COLS), lambda j: (0, j)), keep(w_f.shape)],
        out_specs=[pl.BlockSpec((8, COND_COLS), lambda j: (0, j)), keep((8, w_f.shape[1]))],
        out_shape=[SDS((8, na), F32), SDS((8, w_f.shape[1]), F32)],
        compiler_params=_params("arbitrary"),
    )(c_all, w_a, w_f)


def _cond_grad(c_all, dmod_a, dmod_f):
    na = dmod_a.shape[1]

    def body(c_ref, da_ref, df_ref, oa_ref, of_ref):
        c_t = c_ref[...]
        ca = jnp.concatenate([c_t * _sigmoid(c_t), jnp.zeros((8, D), F32)], axis=0).astype(MM_DTYPE)

        def outer(d_ref):
            dm = jnp.concatenate([d_ref[...], jnp.zeros(d_ref.shape, F32)], axis=0).astype(MM_DTYPE)
            return _dot_tn(ca, dm)

        oa_ref[...] = outer(da_ref)

        @pl.when(pl.program_id(0) == 0)
        def _():
            of_ref[...] = outer(df_ref)

    keep = lambda shape: pl.BlockSpec(shape, lambda j: (0, 0))
    return pl.pallas_call(
        body, name="cond_grad", grid=(na // COND_COLS,),
        in_specs=[keep((8, D)), pl.BlockSpec((8, COND_COLS), lambda j: (0, j)), keep(dmod_f.shape)],
        out_specs=[pl.BlockSpec((D, COND_COLS), lambda j: (0, j)), keep((D, dmod_f.shape[1]))],
        out_shape=[SDS((D, na), F32), SDS((D, dmod_f.shape[1]), F32)],
        compiler_params=_params("arbitrary"),
    )(c_all, dmod_a, dmod_f)


def _row_tile(rows, cap=256):
    if rows <= cap:
        return rows
    for t in range(cap, 7, -8):
        if rows % t == 0:
            return t
    return rows


def _ordered_sum(name, parts, into_half=None):
    n, rows, cols = parts.shape
    rt = _row_tile(rows)
    nb = rows // rt

    def body(*refs):
        p_ref, o_ref = refs[-2:]
        acc = p_ref[0].astype(F32)
        for k in range(1, n):
            acc = acc + p_ref[k].astype(F32)
        o_ref[...] = acc

    if into_half is None:
        return pl.pallas_call(
            body, name=name, grid=(nb,),
            in_specs=[pl.BlockSpec((n, rt, cols), lambda i: (0, i, 0))],
            out_specs=pl.BlockSpec((rt, cols), lambda i: (i, 0)), out_shape=SDS((rows, cols), F32),
            compiler_params=_params("parallel"),
        )(parts)
    grid_spec = pltpu.PrefetchScalarGridSpec(
        num_scalar_prefetch=1, grid=(nb,),
        in_specs=[pl.BlockSpec((n, rt, cols), lambda i, c_ref: (0, i, 0))],
        out_specs=pl.BlockSpec((rt, cols), lambda i, c_ref: (c_ref[0] * nb + i, 0)))
    return pl.pallas_call(
        body, name=name, grid_spec=grid_spec, out_shape=SDS((2 * rows, cols), F32),
        compiler_params=_params("parallel"),
    )(into_half.astype(jnp.int32).reshape(1), parts)


def _adamw_update(w_ref, g_ref, m_ref, v_ref, d_ref, nm_ref, nv_ref):
    c1 = 1.0 - ADAM_B1 ** ADAM_STEP
    c2 = 1.0 - ADAM_B2 ** ADAM_STEP
    g_t = g_ref[...]
    m_new = ADAM_B1 * m_ref[...] + (1.0 - ADAM_B1) * g_t
    v_new = ADAM_B2 * v_ref[...] + (1.0 - ADAM_B2) * (g_t * g_t)
    nm_ref[...] = m_new
    nv_ref[...] = v_new
    d_ref[...] = -ADAM_LR * ((m_new / c1) / (jnp.sqrt(v_new / c2) + ADAM_EPS) + ADAM_WD * w_ref[...])


def _adamw_many(name, ws, gs, ms, vs):
    n = len(ws)

    def body(*refs):
        ins, outs = refs[:4 * n], refs[4 * n:]
        for k in range(n):
            _adamw_update(ins[k], ins[n + k], ins[2 * n + k], ins[3 * n + k], *outs[3 * k:3 * k + 3])

    return pl.pallas_call(
        body, name=name, out_shape=[SDS(w.shape, F32) for w in ws for _ in range(3)],
        compiler_params=pltpu.CompilerParams(vmem_limit_bytes=VMEM_LIMIT),
    )(*ws, *gs, *ms, *vs)


def _adamw_tiled(name, ws, gs, ms, vs):
    n = len(ws)

    def body(*refs):
        ins, outs = refs[:4 * n], refs[4 * n:]
        for k in range(n):
            _adamw_update(ins[k], ins[n + k], ins[2 * n + k], ins[3 * n + k], *outs[3 * k:3 * k + 3])

    specs = [pl.BlockSpec((w.shape[0] // ADAMW_STEPS, w.shape[1]), lambda i: (i, 0)) for w in ws]
    return pl.pallas_call(
        body, name=name, grid=(ADAMW_STEPS,), in_specs=specs * 4, out_specs=[s for s in specs for _ in range(3)],
        out_shape=[SDS(w.shape, F32) for w in ws for _ in range(3)], compiler_params=_params("parallel"),
    )(*ws, *gs, *ms, *vs)


def _place():
    return lax.axis_index("x"), lax.axis_index("y"), lax.axis_index("c")


def _other_chips(x, y):
    return [(1 - x, y), (x, 1 - y), (1 - x, 1 - y)]


def _all_gather8(name, blk):
    m, n = blk.shape

    def body(x_ref, out_ref, send_sems, recv_sems, local_sem):
        x, y, c = _place()
        me, sibling = (x, y, c), (x, y, 1 - c)
        chips = _other_chips(x, y)

        def slot(px, py, pc):
            return out_ref.at[4 * px + 2 * py + pc]

        def copy(k, block, to, src=None):
            return pltpu.make_async_remote_copy(
                src_ref=slot(*block) if src is None else src, dst_ref=slot(*block),
                send_sem=send_sems.at[k], recv_sem=recv_sems.at[k], device_id=to, device_id_type=MESH)

        mine = pltpu.make_async_copy(x_ref, slot(*me), local_sem)
        mine.start()
        first = [copy(0, me, sibling, src=x_ref)]
        first += [copy(1 + j, me, (*chip, c), src=x_ref) for j, chip in enumerate(chips)]
        for cp in first:
            cp.start()
        passed = [copy(4 + j, (*chip, c), sibling) for j, chip in enumerate(chips)]
        for j, chip in enumerate(chips):
            copy(1 + j, (*chip, c), me).wait_recv()
            passed[j].start()
        copy(0, sibling, me).wait_recv()
        for j, chip in enumerate(chips):
            copy(4 + j, (*chip, 1 - c), me).wait_recv()
        for cp in first + passed:
            cp.wait_send()
        mine.wait()

    return pl.pallas_call(
        body, name=name, out_shape=SDS((8, m, n), blk.dtype),
        in_specs=[_whole_vmem()], out_specs=_whole_vmem(),
        scratch_shapes=[pltpu.SemaphoreType.DMA((7,)), pltpu.SemaphoreType.DMA((7,)), pltpu.SemaphoreType.DMA],
        compiler_params=pltpu.CompilerParams(vmem_limit_bytes=VMEM_LIMIT),
    )(blk)


def _any():
    return pl.BlockSpec(memory_space=pl.ANY)


def _gather_weights(shards, blk):
    n = len(shards)

    def body(*refs):
        ins, outs, sems = refs[:n + 1], refs[n + 1:2 * n + 2], refs[2 * n + 2:]
        send, forward, finish = _gather_plan(ins[:n], outs[:n], sems[0], sems[1])
        send8, forward8, finish8 = _gather8_plan(ins[n], outs[n], *sems[2:])
        send8()
        send()
        forward8()
        forward()
        finish8()
        finish()

    return pl.pallas_call(
        body, name="gather_weights",
        out_shape=[SDS((NQ,) + s.shape, s.dtype) for s in shards] + [SDS((8,) + blk.shape, blk.dtype)],
        in_specs=[_any()] * (n + 1), out_specs=[_any()] * (n + 1),
        scratch_shapes=[pltpu.SemaphoreType.DMA((n, 6)), pltpu.SemaphoreType.DMA((n, 6)),
                        pltpu.SemaphoreType.DMA((7,)), pltpu.SemaphoreType.DMA((7,)), pltpu.SemaphoreType.DMA],
    )(*shards, blk)


def _own_piece(gathered, shard):
    myq = 2 * lax.axis_index("x") + lax.axis_index("y")
    return lax.dynamic_update_slice(gathered, shard[None], (myq,) + (0,) * shard.ndim)


def _scatter_to_owners(partials):
    n = len(partials)

    def body(*refs):
        send, finish = _scatter_plan(refs[:n], refs[n:2 * n], *refs[2 * n:])
        send()
        finish()

    return pl.pallas_call(
        body, name="scatter_to_owners",
        out_shape=[_scattered_shape(p) for p in partials],
        in_specs=[_any()] * n, out_specs=[_any()] * n,
        scratch_shapes=[pltpu.SemaphoreType.DMA((n, 7)), pltpu.SemaphoreType.DMA((n, 7))],
    )(*partials)


def _owner_sums(tag, arrived, partials=None):
    x, y, c = _place()
    sums = []
    for w, arr in enumerate(arrived):
        if partials is not None:
            part = partials[w]
            h = part.shape[1] // 2
            own = lax.dynamic_slice(part, (2 * x + y, c * h, 0), (1, h, part.shape[2]))
            arr = lax.dynamic_update_slice(arr, own, (4 * x + 2 * y + c, 0, 0))
        sums.append(_ordered_sum(f"owner_sum_{tag}_{w}", arr, into_half=c))
    return sums


def _join_halves(bufs):
    n = len(bufs)

    def body(*refs):
        ins, outs = refs[:n], refs[n:2 * n]
        send_sems, recv_sems = refs[2 * n:]
        x, y, c = _place()
        cps = []
        for w in range(n):
            h = ins[w].shape[0] // 2
            mine = outs[w].at[pl.ds(c * h, h)]
            cp = pltpu.make_async_remote_copy(src_ref=mine, dst_ref=mine, send_sem=send_sems.at[w],
                                              recv_sem=recv_sems.at[w], device_id=(x, y, 1 - c), device_id_type=MESH)
            cp.start()
            cps.append(cp)
        for cp in cps:
            cp.wait()

    return pl.pallas_call(
        body, name="join_halves",
        out_shape=[SDS(b.shape, b.dtype) for b in bufs],
        in_specs=[_any()] * n, out_specs=[_any()] * n, input_output_aliases={w: w for w in range(n)},
        scratch_shapes=[pltpu.SemaphoreType.DMA((n,)), pltpu.SemaphoreType.DMA((n,))],
    )(*bufs)


def _pad_rows(a, rows):
    return jnp.pad(a, ((0, rows - a.shape[0]),) + ((0, 0),) * (a.ndim - 1))


def _pack_small(dmod, g1, g2, gf, b_in, ln_g, ln_b, conv_b, gn_g, gn_b, ga, gb, sb, cw32, sw, loss_row):
    v512 = jnp.concatenate([ln_g, ln_b, conv_b, gn_g, gn_b, ga, gb, jnp.zeros((1, DA), F32)], axis=1).reshape(4, D)
    rows = [dmod.reshape(8, D), g1, g2, gf, b_in.reshape(2, D), v512, sb.reshape(1, D), cw32.reshape(16, D),
            sw.reshape(CHUNK, D), loss_row]
    packed = jnp.concatenate(rows, axis=0)
    return _pad_rows(packed, PK_ROWS)


def _unpack_small(p):
    v512 = p[PK_V512:PK_V512 + 4].reshape(1, 8 * DA)
    pieces = [v512[:, k * DA:(k + 1) * DA] for k in range(7)]
    return dict(
        dmod=p[PK_DMOD:PK_DMOD + 8].reshape(1, 8 * D), norm1_g=p[PK_G1:PK_G1 + 1], norm2_g=p[PK_G2:PK_G2 + 1],
        norm_f_g=p[PK_GF:PK_GF + 1], b_in=p[PK_BIN:PK_BIN + 2].reshape(1, 2 * D),
        a_ln_g=pieces[0], a_ln_b=pieces[1], b_conv_b=pieces[2], b_gn_g=pieces[3], b_gn_b=pieces[4],
        out_norm_a_g=pieces[5], out_norm_b_g=pieces[6],
        a_spatial_b=p[PK_SB:PK_SB + 1].reshape(N_HEADS, CHUNK),
        b_conv_w=p[PK_CW:PK_CW + 16].reshape(HALO, DB),
        a_spatial_w=p[PK_SW:PK_SW + CHUNK].reshape(N_HEADS, CHUNK, CHUNK))


def kernel(x, c, ada_w, ada_b, norm1_g, w_in, b_in, a_ln_g, a_ln_b, a_spatial_w, a_spatial_b, b_conv_w, b_conv_b, b_gn_g, b_gn_b, out_norm_a_g, out_norm_b_g, w_out, norm2_g, w_ffn_in, w_ffn_out, ada_f_w, ada_f_b, norm_f_g, loss_target, m_ada_w, m_ada_b, m_norm1_g, m_w_in, m_b_in, m_a_ln_g, m_a_ln_b, m_a_spatial_w, m_a_spatial_b, m_b_conv_w, m_b_conv_b, m_b_gn_g, m_b_gn_b, m_out_norm_a_g, m_out_norm_b_g, m_w_out, m_norm2_g, m_w_ffn_in, m_w_ffn_out, m_ada_f_w, m_ada_f_b, m_norm_f_g, v_ada_w, v_ada_b, v_norm1_g, v_w_in, v_b_in, v_a_ln_g, v_a_ln_b, v_a_spatial_w, v_a_spatial_b, v_b_conv_w, v_b_conv_b, v_b_gn_g, v_b_gn_b, v_out_norm_a_g, v_out_norm_b_g, v_w_out, v_norm2_g, v_w_ffn_in, v_w_ffn_out, v_ada_f_w, v_ada_f_b, v_norm_f_g):
    mx, my, mc = _place()
    me = 4 * mx + 2 * my + mc
    myq = 2 * mx + my
    xs = x[0]
    target = loss_target[0]
    s = xs.shape[0]
    n_ada = ada_w.shape[2]

    cw_shard = _pad_rows(b_conv_w[0], HALO)
    mix_shards = [w_in[0].astype(MM_DTYPE), w_out[0].astype(MM_DTYPE)]
    ffn_shards = [w_ffn_in[0].astype(MM_DTYPE), w_ffn_out[0].astype(MM_DTYPE)]
    w_in4, w_out4, first = _gather_weights(mix_shards, jnp.concatenate([c.reshape(8, LANES), cw_shard], axis=0))
    w_in4, w_out4 = _own_piece(w_in4, mix_shards[0]), _own_piece(w_out4, mix_shards[1])
    w_out_f = w_out4.reshape(D, D)
    c_all = first[:, 0:8, :].reshape(8, D)
    conv_w = jnp.concatenate([first[4 * (q // 2) + 2 * (q % 2), 8:8 + HALO, :] for q in range(NQ)], axis=1)
    cond_part = jnp.concatenate(_cond_partial(c_all, ada_w[0], ada_f_w), axis=1)
    cond_all = _all_gather8("gather_cond", cond_part)
    cond_q = [cond_all[4 * (q // 2) + 2 * (q % 2)] for q in range(NQ)]
    mod_all = jnp.concatenate([cq[:, :n_ada] for cq in cond_q] + [cq[:, n_ada:] for cq in cond_q], axis=1)
    mod = lax.dynamic_slice_in_dim(mod_all, me, 1, axis=0)
    mods = jnp.concatenate([ada_b, ada_f_b.reshape(1, 2 * D)], axis=1)

    causal = jnp.tril(jnp.ones((CHUNK, CHUNK), dtype=bool))
    wm_f = jnp.where(causal[None], a_spatial_w[0], 0.0)
    wm = wm_f.astype(MM_DTYPE)
    wmt = jnp.swapaxes(wm_f, 1, 2).astype(MM_DTYPE)
    bst = jnp.repeat(a_spatial_b[0].T, HALF, axis=1)

    z, x1, yb1, y, w_ffn_in4, w_ffn_out4 = _mixer_fwd(
        xs, mod, mods, norm1_g, w_in4, b_in, a_ln_g, a_ln_b, wm, bst, conv_w, b_conv_b, b_gn_g, b_gn_b,
        out_norm_a_g, out_norm_b_g, w_out_f, ffn_shards)
    w_ffn_out_f = w_ffn_out4.reshape(DFF, D)
    g, up, a_act, h2, dx2, acc_f = _ffn_fwd(x1, target, mod, mods, norm2_g, norm_f_g, w_ffn_in4, w_ffn_out_f)

    dff, dx1, acc_2 = _ffn_bwd(dx2, x1, g, up, mod, mods, norm2_g, w_ffn_in4, w_ffn_out_f)
    (gw_ffn_in4,) = _grad_matmul("grad_w_ffn_in", h2, dff, D, PW_FF, piece_w=PW_FF)
    modv = mod + mods
    gw_ffn_out, dgate2 = _grad_matmul("grad_w_ffn_out", a_act, dx2, PW_FF, D, gated=(modv[:, 5 * D:6 * D], w_ffn_out_f))
    gw_out, dgate1 = _grad_matmul("grad_w_out", y, dx1, D, D, gated=(modv[:, 2 * D:3 * D], w_out_f))
    early_partials = [gw_ffn_in4, gw_ffn_out.reshape(NQ, DFF // NQ, D), gw_out.reshape(NQ, D // NQ, D)]
    gx, dz, h, acc_1, acc_bin, acc_5, acc_cw, acc_sw, acc_sb, *early_arrived = _mixer_bwd(
        dx1, xs, z, yb1, mod, mods, norm1_g, w_in4, a_ln_g, a_ln_b, wm, wmt, bst, conv_w, b_gn_g, b_gn_b,
        out_norm_a_g, out_norm_b_g, w_out_f, early_partials)

    dmod = jnp.concatenate([acc_1[0:1], acc_1[1:2], dgate1[0:1], acc_2[0:1], acc_2[1:2], dgate2[0:1],
                            acc_f[0:1], acc_f[1:2]], axis=1)
    sw_grad = jnp.where(causal[None], acc_sw, 0.0)
    sb_grad = acc_sb[:, ::HALF].T
    packed = _pack_small(dmod, acc_1[2:3], acc_2[2:3], acc_f[2:3], acc_bin[0:1], acc_5[2:3], acc_5[3:4], acc_5[6:7],
                         acc_5[4:5], acc_5[5:6], acc_5[0:1], acc_5[1:2], sb_grad, acc_cw, sw_grad, acc_f[4:5])
    gw_in4, gathered = _grad_matmul("grad_w_in", h, dz, D, NQ * PW_IN, piece_w=PW_IN, gather_blk=packed)
    late_arrived = _scatter_to_owners([gw_in4])
    g_w_in, g_w_ffn_in, g_w_ffn_out, g_w_out = _join_halves(
        _owner_sums("late", late_arrived, [gw_in4]) + _owner_sums("early", early_arrived))
    summed = _ordered_sum("small_grad_sum", gathered)
    loss = summed[PK_LOSS, 0]
    small = _unpack_small(summed)
    dmod_all = gathered[:, PK_DMOD:PK_DMOD + 8, :].reshape(8, 8 * D)
    g_ada_w, g_ada_f_w = _cond_grad(c_all, lax.dynamic_slice_in_dim(dmod_all, myq * n_ada, n_ada, axis=1),
                                    lax.dynamic_slice_in_dim(dmod_all, 6 * D + myq * PW_IN, PW_IN, axis=1))

    grads = dict(
        ada_w=g_ada_w, ada_b=small["dmod"][:, :6 * D], norm1_g=small["norm1_g"], w_in=g_w_in,
        b_in=small["b_in"], a_ln_g=small["a_ln_g"], a_ln_b=small["a_ln_b"], a_spatial_w=small["a_spatial_w"],
        a_spatial_b=small["a_spatial_b"],
        b_conv_w=lax.dynamic_slice_in_dim(small["b_conv_w"], myq * LANES, LANES, axis=1)[:CONV_W],
        b_conv_b=small["b_conv_b"], b_gn_g=small["b_gn_g"], b_gn_b=small["b_gn_b"],
        out_norm_a_g=small["out_norm_a_g"], out_norm_b_g=small["out_norm_b_g"], w_out=g_w_out,
        norm2_g=small["norm2_g"], w_ffn_in=g_w_ffn_in, w_ffn_out=g_w_ffn_out, ada_f_w=g_ada_f_w,
        ada_f_b=small["dmod"][:, 6 * D:], norm_f_g=small["norm_f_g"])

    weights = dict(ada_w=ada_w, ada_b=ada_b, norm1_g=norm1_g, w_in=w_in, b_in=b_in, a_ln_g=a_ln_g, a_ln_b=a_ln_b,
                   a_spatial_w=a_spatial_w, a_spatial_b=a_spatial_b, b_conv_w=b_conv_w, b_conv_b=b_conv_b, b_gn_g=b_gn_g,
                   b_gn_b=b_gn_b, out_norm_a_g=out_norm_a_g, out_norm_b_g=out_norm_b_g, w_out=w_out, norm2_g=norm2_g,
                   w_ffn_in=w_ffn_in, w_ffn_out=w_ffn_out, ada_f_w=ada_f_w, ada_f_b=ada_f_b, norm_f_g=norm_f_g)
    m_in = dict(ada_w=m_ada_w, ada_b=m_ada_b, norm1_g=m_norm1_g, w_in=m_w_in, b_in=m_b_in, a_ln_g=m_a_ln_g, a_ln_b=m_a_ln_b,
                a_spatial_w=m_a_spatial_w, a_spatial_b=m_a_spatial_b, b_conv_w=m_b_conv_w, b_conv_b=m_b_conv_b,
                b_gn_g=m_b_gn_g, b_gn_b=m_b_gn_b, out_norm_a_g=m_out_norm_a_g, out_norm_b_g=m_out_norm_b_g, w_out=m_w_out,
                norm2_g=m_norm2_g, w_ffn_in=m_w_ffn_in, w_ffn_out=m_w_ffn_out, ada_f_w=m_ada_f_w, ada_f_b=m_ada_f_b,
                norm_f_g=m_norm_f_g)
    v_in = dict(ada_w=v_ada_w, ada_b=v_ada_b, norm1_g=v_norm1_g, w_in=v_w_in, b_in=v_b_in, a_ln_g=v_a_ln_g, a_ln_b=v_a_ln_b,
                a_spatial_w=v_a_spatial_w, a_spatial_b=v_a_spatial_b, b_conv_w=v_b_conv_w, b_conv_b=v_b_conv_b,
                b_gn_g=v_b_gn_g, b_gn_b=v_b_gn_b, out_norm_a_g=v_out_norm_a_g, out_norm_b_g=v_out_norm_b_g, w_out=v_w_out,
                norm2_g=v_norm2_g, w_ffn_in=v_w_ffn_in, w_ffn_out=v_w_ffn_out, ada_f_w=v_ada_f_w, ada_f_b=v_ada_f_b,
                norm_f_g=v_norm_f_g)
    names = list(weights)
    big = ("ada_w", "w_in", "w_out", "w_ffn_in", "w_ffn_out", "ada_f_w")

    def flat2(a):
        return a.reshape(-1, a.shape[-1])

    delta, new_m, new_v = {}, {}, {}
    for nm in big:
        grads[nm] = grads[nm].reshape(weights[nm].shape)
    big_out = _adamw_tiled("adamw_large", *[[flat2(tree[nm]) for nm in big] for tree in (weights, grads, m_in, v_in)])
    for k, nm in enumerate(big):
        shape = weights[nm].shape
        delta[nm], new_m[nm], new_v[nm] = [o.reshape(shape) for o in big_out[3 * k:3 * k + 3]]

    small_names = [nm for nm in names if nm not in big]
    for nm in small_names:
        grads[nm] = grads[nm].reshape(weights[nm].shape)
    small_out = _adamw_many("adamw_small", *[[flat2(tree[nm]) for nm in small_names] for tree in (weights, grads, m_in, v_in)])
    for k, nm in enumerate(small_names):
        shape = weights[nm].shape
        delta[nm], new_m[nm], new_v[nm] = [o.reshape(shape) for o in small_out[3 * k:3 * k + 3]]

    grad_x = gx.reshape(x.shape)
    return (loss, grad_x, *[grads[nm] for nm in names], *[delta[nm] for nm in names],
            *[new_m[nm] for nm in names], *[new_v[nm] for nm in names])
```

```python
import functools
import math

import jax
import jax.numpy as jnp
from jax import lax
from jax.experimental import pallas as pl
from jax.experimental.pallas import tpu as pltpu

F32 = jnp.float32
MM_DTYPE = jnp.bfloat16
WIRE_DTYPE = jnp.bfloat16
SDS = jax.ShapeDtypeStruct
MESH = pl.DeviceIdType.MESH

D = 1024
DA = 512
DB = 512
NQ = 4
PW_IN = 512
DFF = 2816
PW_FF = 1408
CHUNK = 128
N_HEADS = 8
CONV_W = 31
HALO = 32
CONV_ROWS = 64
EPS = 1e-6
LANES = 128
HALF = 64

ROW_TILE = 256
FWD_ROW_TILE = 512
FFN_ROW_TILE = 256
GRAD_ROW_TILE = 2048
ADAMW_STEPS = 8
VMEM_LIMIT = 60 * 1024 * 1024

ADAM_LR = 0.001
ADAM_B1 = 0.9
ADAM_B2 = 0.999
ADAM_EPS = 1e-08
ADAM_WD = 0.01
ADAM_STEP = 10

PK_DMOD = 0
PK_G1 = 8
PK_G2 = 9
PK_GF = 10
PK_BIN = 11
PK_V512 = 13
PK_SB = 17
PK_CW = 18
PK_SW = 34
PK_LOSS = 162
PK_ROWS = 168


def _dot(a, b):
    return jnp.dot(a, b, preferred_element_type=F32)


def _dot_nt(a, b):
    return lax.dot_general(a, b, (((1,), (1,)), ((), ())), preferred_element_type=F32)


def _dot_tn(a, b):
    return lax.dot_general(a, b, (((0,), (0,)), ((), ())), preferred_element_type=F32)


def _rowsum(x):
    return jnp.sum(x, axis=-1, keepdims=True)


def _colsum(x):
    return jnp.sum(x, axis=0, keepdims=True)


def _group_sum(x):
    rows, width = x.shape
    lo_mask = lax.broadcasted_iota(jnp.int32, (rows, LANES), 1) < HALF
    outs = []
    for jb in range(width // LANES):
        xb = x[:, jb * LANES:(jb + 1) * LANES]
        lo = _rowsum(jnp.where(lo_mask, xb, 0.0))
        hi = _rowsum(jnp.where(lo_mask, 0.0, xb))
        outs.append(jnp.where(lo_mask, lo, hi))
    return jnp.concatenate(outs, axis=-1)


def _sigmoid(x):
    return 1.0 / (1.0 + jnp.exp(-x))


def _gelu_parts(u):
    cdf = 0.5 * (1.0 + lax.erf(u * (1.0 / math.sqrt(2.0))))
    pdf = jnp.exp(-0.5 * u * u) * (1.0 / math.sqrt(2.0 * math.pi))
    return u * cdf, cdf + u * pdf


def _whole_vmem():
    return pl.BlockSpec(memory_space=pltpu.VMEM)


def _params(*semantics):
    return pltpu.CompilerParams(dimension_semantics=semantics, vmem_limit_bytes=VMEM_LIMIT)


def _mod_rows(mod_ref, modb_ref, first, count):
    m = mod_ref[...] + modb_ref[...]
    return [m[:, (first + k) * D:(first + k + 1) * D] for k in range(count)]


def _mixer_recompute(z_parts, lng, lnb, wm_ref, bst_ref, mix_ref):
    u, v, val, gate = z_parts
    rows = u.shape[0]
    gu, dgu = _gelu_parts(u)
    gv, dgv = _gelu_parts(v)
    mu = _rowsum(gv) * (1.0 / DA)
    vc = gv - mu
    rs = lax.rsqrt(_rowsum(vc * vc) * (1.0 / DA) + EPS)
    vhat = vc * rs
    vl = vhat * lng + lnb
    vlb = vl.astype(MM_DTYPE)
    lo_mask = lax.broadcasted_iota(jnp.int32, (CHUNK, LANES), 1) < HALF
    for ck in range(rows // CHUNK):
        for jb in range(DA // LANES):
            blk = vlb[ck * CHUNK:(ck + 1) * CHUNK, jb * LANES:(jb + 1) * LANES]
            a = _dot(wm_ref[2 * jb], blk)
            b = _dot(wm_ref[2 * jb + 1], blk)
            mix_ref[ck * CHUNK:(ck + 1) * CHUNK, jb * LANES:(jb + 1) * LANES] = (
                jnp.where(lo_mask, a, b) + bst_ref[:, jb * LANES:(jb + 1) * LANES])
    mixed = mix_ref[...]
    sg = _sigmoid(gate)
    yb0 = val * sg
    return dict(gu=gu, dgu=dgu, dgv=dgv, rs=rs, vhat=vhat, vlb=vlb, mixed=mixed, sg=sg, yb0=yb0)


def _conv_branch_tail(yb1, gng, gnb):
    gm = _group_sum(yb1) * (1.0 / HALF)
    gc = yb1 - gm
    grs = lax.rsqrt(_group_sum(gc * gc) * (1.0 / HALF) + EPS)
    ghat = gc * grs
    yb2 = ghat * gng + gnb
    s2 = _sigmoid(yb2)
    return dict(grs=grs, ghat=ghat, yb2=yb2, s2=s2, y_b=yb2 * s2)


def _shifted_copies(e_ref, sh_ref):
    n = sh_ref.shape[1]
    for b in range(1, 8):
        sh_ref[b - 1] = e_ref[pl.ds(b, n), :]


def _window(e_ref, sh_ref, offset, r0, nrows, cols):
    a, b = divmod(offset, 8)
    if b == 0:
        return e_ref[pl.ds(r0 + 8 * a, nrows), cols]
    return sh_ref[b - 1, pl.ds(r0 + 8 * a, nrows), cols]


def _conv_taps(e_ref, sh_ref, cw_ref, out_ref, ts, first_offset, flip, bias_ref=None, other_ref=None, tap_acc_ref=None):
    groups = CONV_ROWS // 8
    for cb in range(DB // LANES):
        cols = slice(cb * LANES, (cb + 1) * LANES)
        tap_acc = [jnp.zeros((8, LANES), F32) for _ in range(CONV_W)]
        for rb in range(ts // CONV_ROWS):
            r0 = rb * CONV_ROWS
            acc = jnp.zeros((CONV_ROWS, LANES), F32)
            if bias_ref is not None:
                acc = acc + bias_ref[:, cols]
            if other_ref is not None:
                other = other_ref[r0:r0 + CONV_ROWS, cols]
            for j in range(CONV_W):
                k = CONV_W - 1 - j if flip else j
                win = _window(e_ref, sh_ref, first_offset + j, r0, CONV_ROWS, cols)
                acc = acc + win * cw_ref[k:k + 1, cols]
                if other_ref is not None:
                    tap_acc[k] = tap_acc[k] + jnp.sum((other * win).reshape(groups, 8, LANES), axis=0)
            out_ref[r0:r0 + CONV_ROWS, cols] = acc
        if other_ref is not None:
            for k in range(CONV_W):
                tap_acc_ref[k:k + 1, cols] += _colsum(tap_acc[k])


def _gather_plan(ins, outs, send_sems, recv_sems, local_sems=None):
    x, y, c = _place()
    sibling = (x, y, 1 - c)
    chips = _other_chips(x, y)
    myq = 2 * x + y

    def copy(w, k, q, hc, to, src=None):
        rows = ins[w].shape[0]
        dst = outs[w].at[q, pl.ds(hc * (rows // 2), rows // 2)]
        return pltpu.make_async_remote_copy(
            src_ref=dst if src is None else src, dst_ref=dst,
            send_sem=send_sems.at[w, k], recv_sem=recv_sems.at[w, k], device_id=to, device_id_type=MESH)

    def own(w):
        return pltpu.make_async_copy(ins[w], outs[w].at[myq], local_sems.at[w])

    def send():
        for w in range(len(ins)):
            rows = ins[w].shape[0]
            src = ins[w].at[pl.ds(c * (rows // 2), rows // 2)]
            for j, chip in enumerate(chips):
                copy(w, j, myq, c, (*chip, c), src=src).start()
            if local_sems is not None:
                own(w).start()

    def forward():
        for w in range(len(ins)):
            for j, (qx, qy) in enumerate(chips):
                copy(w, j, 2 * qx + qy, c, sibling).wait_recv()
                copy(w, 3 + j, 2 * qx + qy, c, sibling).start()

    def finish():
        for w in range(len(ins)):
            for j, (qx, qy) in enumerate(chips):
                copy(w, 3 + j, 2 * qx + qy, 1 - c, sibling).wait_recv()
        for w in range(len(ins)):
            for k, (qx, qy) in enumerate(chips + chips):
                copy(w, k, 2 * qx + qy, c, sibling).wait_send()
            if local_sems is not None:
                own(w).wait()

    return send, forward, finish


def _mixer_fwd(x, mod, mods, norm1_g, w_in4, b_in, ln_g, ln_b, wm, bst, conv_w, conv_b, gn_g, gn_b, ga, gb, w_out,
               ffn_shards):
    s = x.shape[0]
    ts = min(FWD_ROW_TILE, s)
    nt = s // ts
    n_sh = len(ffn_shards)

    def body(x_ref, mod_ref, modb_ref, g1_ref, w4_ref, bin_ref, lng_ref, lnb_ref, wm_ref, bst_ref, cw_ref, cb_ref,
             gng_ref, gnb_ref, ga_ref, gb_ref, wout_ref, *rest):
        shard_refs, rest = rest[:n_sh], rest[n_sh:]
        z_ref, x1_ref, yb1_ref, y_ref = rest[:4]
        full_refs, rest = rest[4:4 + n_sh], rest[4 + n_sh:]
        e_ref, sh_ref, mix_ref, send_sems, recv_sems, local_sems = rest
        i = pl.program_id(0)
        send, forward, finish = _gather_plan(shard_refs, full_refs, send_sems, recv_sems, local_sems)

        @pl.when(i == 0)
        def _():
            send()
            e_ref[0:HALO, :] = jnp.zeros((HALO, DB), F32)

        @pl.when(i == (3 * nt) // 4)
        def _():
            forward()

        shift1, scale1, gate1 = _mod_rows(mod_ref, modb_ref, 0, 3)
        x_t = x_ref[...]
        r1 = lax.rsqrt(_rowsum(x_t * x_t) * (1.0 / D) + EPS)
        h = (x_t * r1 * g1_ref[...]) * (1.0 + scale1) + shift1
        hb = h.astype(MM_DTYPE)
        z_parts = []
        for q in range(NQ):
            zq = _dot(hb, w4_ref[q]) + bin_ref[:, q * PW_IN:(q + 1) * PW_IN]
            z_ref[:, q * PW_IN:(q + 1) * PW_IN] = zq
            z_parts.append(zq)
        r = _mixer_recompute(z_parts, lng_ref[...], lnb_ref[...], wm_ref, bst_ref, mix_ref)
        y_a = r["gu"] * r["mixed"]
        e_ref[HALO:HALO + ts, :] = r["yb0"]
        _shifted_copies(e_ref, sh_ref)
        _conv_taps(e_ref, sh_ref, cw_ref, yb1_ref, ts, HALO - (CONV_W - 1), False, bias_ref=cb_ref)
        e_ref[0:HALO, :] = e_ref[ts:ts + HALO, :]
        t = _conv_branch_tail(yb1_ref[...], gng_ref[...], gnb_ref[...])
        ra = lax.rsqrt(_rowsum(y_a * y_a) * (1.0 / DA) + EPS)
        rb = lax.rsqrt(_rowsum(t["y_b"] * t["y_b"]) * (1.0 / DB) + EPS)
        yan = (y_a * ra * ga_ref[...]).astype(MM_DTYPE)
        ybn = (t["y_b"] * rb * gb_ref[...]).astype(MM_DTYPE)
        y_ref[:, 0:DA] = yan
        y_ref[:, DA:D] = ybn
        o1 = _dot(yan, wout_ref[0:DA, :]) + _dot(ybn, wout_ref[DA:D, :])
        x1_ref[...] = x_t + gate1 * o1

        @pl.when(i == nt - 1)
        def _():
            finish()

    row = lambda w: pl.BlockSpec((ts, w), lambda i: (i, 0))
    full = lambda a: pl.BlockSpec(a.shape, lambda i: (0,) * a.ndim)
    return pl.pallas_call(
        body, name="mixer_fwd", grid=(nt,),
        in_specs=[row(D), full(mod), full(mods), full(norm1_g), _whole_vmem(), full(b_in), full(ln_g), full(ln_b),
                  _whole_vmem(), full(bst), full(conv_w), full(conv_b), full(gn_g), full(gn_b), full(ga), full(gb),
                  _whole_vmem()] + [_any()] * n_sh,
        out_specs=[row(4 * PW_IN), row(D), row(DB), row(D)] + [_any()] * n_sh,
        out_shape=[SDS((s, 4 * PW_IN), F32), SDS((s, D), F32), SDS((s, DB), F32), SDS((s, D), MM_DTYPE)]
        + [SDS((NQ,) + w.shape, w.dtype) for w in ffn_shards],
        scratch_shapes=[pltpu.VMEM((ts + HALO, DB), F32), pltpu.VMEM((7, ts + HALO - 8, DB), F32), pltpu.VMEM((ts, DA), F32),
                        pltpu.SemaphoreType.DMA((n_sh, 6)), pltpu.SemaphoreType.DMA((n_sh, 6)),
                        pltpu.SemaphoreType.DMA((n_sh,))],
        compiler_params=_params("arbitrary"),
    )(x, mod, mods, norm1_g, w_in4, b_in, ln_g, ln_b, wm, bst, conv_w, conv_b, gn_g, gn_b, ga, gb, w_out, *ffn_shards)


def _ffn_fwd(x1, target, mod, mods, norm2_g, norm_f_g, w_ffn_in4, w_ffn_out):
    s = x1.shape[0]
    sub_rows = min(FFN_ROW_TILE, s)
    ts = min(2 * sub_rows, s)
    nt = s // ts

    def body(x1_ref, tgt_ref, mod_ref, modb_ref, g2_ref, gf_ref, wf_ref, wo_ref,
             g_ref, up_ref, a_ref, h2_ref, dx2_ref, acc_ref):
        i = pl.program_id(0)

        @pl.when(i == 0)
        def _():
            acc_ref[...] = jnp.zeros(acc_ref.shape, F32)

        shift2, scale2, gate2, shift_f, scale_f = _mod_rows(mod_ref, modb_ref, 3, 5)
        for sub in range(ts // sub_rows):
            rows = slice(sub * sub_rows, (sub + 1) * sub_rows)
            x1_t = x1_ref[rows, :]
            r2 = lax.rsqrt(_rowsum(x1_t * x1_t) * (1.0 / D) + EPS)
            h2 = (x1_t * r2 * g2_ref[...]) * (1.0 + scale2) + shift2
            h2b = h2.astype(MM_DTYPE)
            h2_ref[rows, :] = h2b
            o2 = jnp.zeros((sub_rows, D), F32)
            for p in range(2):
                g = _dot(h2b, wf_ref[p])
                up = _dot(h2b, wf_ref[2 + p])
                g_ref[rows, p * PW_FF:(p + 1) * PW_FF] = g.astype(MM_DTYPE)
                up_ref[rows, p * PW_FF:(p + 1) * PW_FF] = up.astype(MM_DTYPE)
                a = (g * _sigmoid(g) * up).astype(MM_DTYPE)
                a_ref[rows, p * PW_FF:(p + 1) * PW_FF] = a
                o2 = o2 + _dot(a, wo_ref[p * PW_FF:(p + 1) * PW_FF, :])
            x2 = x1_t + gate2 * o2
            rf = lax.rsqrt(_rowsum(x2 * x2) * (1.0 / D) + EPS)
            gf = gf_ref[...]
            nf = x2 * rf * gf
            err = nf * (1.0 + scale_f) + shift_f - tgt_ref[rows, :]
            d_out = err * (1.0 / D)
            d_nf = d_out * (1.0 + scale_f)
            t = d_nf * gf
            dx2_ref[rows, :] = rf * t - x2 * (rf * rf * rf) * (_rowsum(t * x2) * (1.0 / D))
            acc_ref[0:1, :] += _colsum(d_out)
            acc_ref[1:2, :] += _colsum(d_out * nf)
            acc_ref[2:3, :] += _colsum(d_nf * x2 * rf)
            acc_ref[3:4, :] += _colsum(err * err)

        @pl.when(i == nt - 1)
        def _():
            acc_ref[4:5, :] = jnp.zeros((1, D), F32) + _rowsum(acc_ref[3:4, :]) * (0.5 / D)

    row = lambda w: pl.BlockSpec((ts, w), lambda i: (i, 0))
    full = lambda a: pl.BlockSpec(a.shape, lambda i: (0,) * a.ndim)
    return pl.pallas_call(
        body, name="ffn_fwd", grid=(nt,),
        in_specs=[row(D), row(D), full(mod), full(mods), full(norm2_g), full(norm_f_g), _whole_vmem(), _whole_vmem()],
        out_specs=[row(DFF), row(DFF), row(DFF), row(D), row(D), pl.BlockSpec((8, D), lambda i: (0, 0))],
        out_shape=[SDS((s, DFF), MM_DTYPE), SDS((s, DFF), MM_DTYPE), SDS((s, DFF), MM_DTYPE), SDS((s, D), MM_DTYPE),
                   SDS((s, D), F32), SDS((8, D), F32)],
        compiler_params=_params("arbitrary"),
    )(x1, target, mod, mods, norm2_g, norm_f_g, w_ffn_in4, w_ffn_out)


def _ffn_bwd(dx2, x1, g, up, mod, mods, norm2_g, w_ffn_in4, w_ffn_out):
    s = x1.shape[0]
    sub_rows = min(FFN_ROW_TILE, s)
    ts = min(2 * sub_rows, s)
    nt = s // ts

    def body(dx2_ref, x1_ref, g_ref, up_ref, mod_ref, modb_ref, g2_ref, wf_ref, wo_ref,
             dff_ref, dx1_ref, acc_ref):
        @pl.when(pl.program_id(0) == 0)
        def _():
            acc_ref[...] = jnp.zeros(acc_ref.shape, F32)

        shift2, scale2, gate2 = _mod_rows(mod_ref, modb_ref, 3, 3)
        for sub in range(ts // sub_rows):
            rows = slice(sub * sub_rows, (sub + 1) * sub_rows)
            dx2_t = dx2_ref[rows, :]
            do2 = (dx2_t * gate2).astype(MM_DTYPE)
            dh2 = jnp.zeros((sub_rows, D), F32)
            for p in range(2):
                da = _dot_nt(do2, wo_ref[p * PW_FF:(p + 1) * PW_FF, :])
                gp = g_ref[rows, p * PW_FF:(p + 1) * PW_FF].astype(F32)
                upp = up_ref[rows, p * PW_FF:(p + 1) * PW_FF].astype(F32)
                sg = _sigmoid(gp)
                silu = gp * sg
                dg = (da * upp * (sg * (1.0 + gp * (1.0 - sg)))).astype(MM_DTYPE)
                dup = (da * silu).astype(MM_DTYPE)
                dff_ref[rows, p * PW_FF:(p + 1) * PW_FF] = dg
                dff_ref[rows, DFF + p * PW_FF:DFF + (p + 1) * PW_FF] = dup
                dh2 = dh2 + _dot_nt(dg, wf_ref[p]) + _dot_nt(dup, wf_ref[2 + p])
            x1_t = x1_ref[rows, :]
            r2 = lax.rsqrt(_rowsum(x1_t * x1_t) * (1.0 / D) + EPS)
            g2 = g2_ref[...]
            xr = x1_t * r2
            dn2 = dh2 * (1.0 + scale2)
            t = dn2 * g2
            dx1_ref[rows, :] = dx2_t + r2 * t - x1_t * (r2 * r2 * r2) * (_rowsum(t * x1_t) * (1.0 / D))
            acc_ref[0:1, :] += _colsum(dh2)
            acc_ref[1:2, :] += _colsum(dh2 * (xr * g2))
            acc_ref[2:3, :] += _colsum(dn2 * xr)

    row = lambda w: pl.BlockSpec((ts, w), lambda i: (i, 0))
    full = lambda a: pl.BlockSpec(a.shape, lambda i: (0,) * a.ndim)
    return pl.pallas_call(
        body, name="ffn_bwd", grid=(nt,),
        in_specs=[row(D), row(D), row(DFF), row(DFF), full(mod), full(mods), full(norm2_g), _whole_vmem(), _whole_vmem()],
        out_specs=[row(2 * DFF), row(D), pl.BlockSpec((8, D), lambda i: (0, 0))],
        out_shape=[SDS((s, 2 * DFF), MM_DTYPE), SDS((s, D), F32), SDS((8, D), F32)],
        compiler_params=_params("arbitrary"),
    )(dx2, x1, g, up, mod, mods, norm2_g, w_ffn_in4, w_ffn_out)


def _scatter_plan(ins, outs, send_sems, recv_sems, local_sems=None):
    x, y, c = _place()
    me = 4 * x + 2 * y + c

    def copies():
        cps = []
        for w in range(len(ins)):
            h = ins[w].shape[1] // 2
            for k in range(1, 8):
                px, py, pc = (1 - x if k & 4 else x), (1 - y if k & 2 else y), (1 - c if k & 1 else c)
                cps.append(pltpu.make_async_remote_copy(
                    src_ref=ins[w].at[2 * px + py, pl.ds(pc * h, h)], dst_ref=outs[w].at[me],
                    send_sem=send_sems.at[w, k - 1], recv_sem=recv_sems.at[w, k - 1],
                    device_id=(px, py, pc), device_id_type=MESH))
        return cps

    def own():
        if local_sems is None:
            return []
        return [pltpu.make_async_copy(ins[w].at[2 * x + y, pl.ds(c * (ins[w].shape[1] // 2), ins[w].shape[1] // 2)],
                                      outs[w].at[me], local_sems.at[w]) for w in range(len(ins))]

    def send():
        for cp in copies() + own():
            cp.start()

    def finish():
        for cp in copies() + own():
            cp.wait()

    return send, finish


def _scattered_shape(partial):
    nq, rows, cols = partial.shape
    return SDS((8, rows // 2, cols), partial.dtype)


def _mixer_bwd(dx1, x, z, yb1, mod, mods, norm1_g, w_in4, ln_g, ln_b, wm, wmt, bst, conv_w, gn_g, gn_b, ga, gb, w_out,
               partials):
    s = x.shape[0]
    ts = min(ROW_TILE, s)
    nt = s // ts
    n_cs = len(partials)

    def body(dx1_ref, x_ref, z_ref, yb1_ref, mod_ref, modb_ref, g1_ref, w4_ref, lng_ref, lnb_ref, wm_ref, wmt_ref,
             bst_ref, cw_ref, gng_ref, gnb_ref, ga_ref, gb_ref, wout_ref, *rest):
        cs_refs, rest = rest[:n_cs], rest[n_cs:]
        gx_ref, dz_ref, h_ref, a1_ref, a2_ref, a5_ref, acw_ref, asw_ref, asb_ref = rest[:9]
        arrived_refs, rest = rest[9:9 + n_cs], rest[9 + n_cs:]
        e_ref, sh_ref, mix_ref, dvl_ref, send_sems, recv_sems, local_sems = rest
        i = pl.program_id(0)
        send, finish = _scatter_plan(cs_refs, arrived_refs, send_sems, recv_sems, local_sems)

        @pl.when(i == 0)
        def _():
            send()
            e_ref[ts:ts + HALO, :] = jnp.zeros((HALO, DB), F32)
            for r in (a1_ref, a2_ref, a5_ref, acw_ref, asw_ref, asb_ref):
                r[...] = jnp.zeros(r.shape, F32)

        shift1, scale1, gate1 = _mod_rows(mod_ref, modb_ref, 0, 3)
        dx1_t = dx1_ref[...]
        do1 = (dx1_t * gate1).astype(MM_DTYPE)
        d_yan = _dot_nt(do1, wout_ref[0:DA, :])
        d_ybn = _dot_nt(do1, wout_ref[DA:D, :])

        z_parts = [z_ref[:, q * PW_IN:(q + 1) * PW_IN] for q in range(NQ)]
        u, v, val, gate = z_parts
        lng = lng_ref[...]
        r = _mixer_recompute(z_parts, lng, lnb_ref[...], wm_ref, bst_ref, mix_ref)
        gng = gng_ref[...]
        t = _conv_branch_tail(yb1_ref[...], gng, gnb_ref[...])
        y_a = r["gu"] * r["mixed"]
        y_b = t["y_b"]
        ga_v, gb_v = ga_ref[...], gb_ref[...]
        ra = lax.rsqrt(_rowsum(y_a * y_a) * (1.0 / DA) + EPS)
        rb = lax.rsqrt(_rowsum(y_b * y_b) * (1.0 / DB) + EPS)

        a5_ref[0:1, :] += _colsum(d_yan * y_a * ra)
        a5_ref[1:2, :] += _colsum(d_ybn * y_b * rb)
        ta = d_yan * ga_v
        d_ya = ra * ta - y_a * (ra * ra * ra) * (_rowsum(ta * y_a) * (1.0 / DA))
        tb = d_ybn * gb_v
        d_yb = rb * tb - y_b * (rb * rb * rb) * (_rowsum(tb * y_b) * (1.0 / DB))

        d_u = d_ya * r["mixed"] * r["dgu"]
        d_mixed = d_ya * r["gu"]
        dmb = d_mixed.astype(MM_DTYPE)
        lo_mask = lax.broadcasted_iota(jnp.int32, (CHUNK, LANES), 1) < HALF
        zero_blk = jnp.zeros((CHUNK, LANES), MM_DTYPE)
        sb_acc = jnp.zeros((CHUNK, DA), F32)
        for ck in range(ts // CHUNK):
            rows = slice(ck * CHUNK, (ck + 1) * CHUNK)
            sb_acc = sb_acc + d_mixed[rows, :]
            for jb in range(DA // LANES):
                cols = slice(jb * LANES, (jb + 1) * LANES)
                dm_blk = dmb[rows, cols]
                vl_blk = r["vlb"][rows, cols]
                da_ = _dot(wmt_ref[2 * jb], dm_blk)
                db_ = _dot(wmt_ref[2 * jb + 1], dm_blk)
                dvl_ref[rows, cols] = jnp.where(lo_mask, da_, db_)
                asw_ref[2 * jb] += _dot_nt(jnp.where(lo_mask, dm_blk, zero_blk), vl_blk)
                asw_ref[2 * jb + 1] += _dot_nt(jnp.where(lo_mask, zero_blk, dm_blk), vl_blk)
        asb_ref[...] += sb_acc
        d_vl = dvl_ref[...]
        a5_ref[2:3, :] += _colsum(d_vl * r["vhat"])
        a5_ref[3:4, :] += _colsum(d_vl)
        dvh = d_vl * lng
        d_gv = r["rs"] * (dvh - _rowsum(dvh) * (1.0 / DA) - r["vhat"] * (_rowsum(dvh * r["vhat"]) * (1.0 / DA)))
        d_v = d_gv * r["dgv"]

        yb2, s2 = t["yb2"], t["s2"]
        d_yb2 = d_yb * (s2 * (1.0 + yb2 * (1.0 - s2)))
        a5_ref[4:5, :] += _colsum(d_yb2 * t["ghat"])
        a5_ref[5:6, :] += _colsum(d_yb2)
        dgh = d_yb2 * gng
        d_yb1 = t["grs"] * (dgh - _group_sum(dgh) * (1.0 / HALF) - t["ghat"] * (_group_sum(dgh * t["ghat"]) * (1.0 / HALF)))
        a5_ref[6:7, :] += _colsum(d_yb1)
        e_ref[0:ts, :] = d_yb1
        _shifted_copies(e_ref, sh_ref)
        mix_ref[...] = r["yb0"]
        _conv_taps(e_ref, sh_ref, cw_ref, dvl_ref, ts, 0, True, other_ref=mix_ref, tap_acc_ref=acw_ref)
        d_yb0 = dvl_ref[...]
        e_ref[ts:ts + HALO, :] = e_ref[0:HALO, :]
        sg = r["sg"]
        d_val = d_yb0 * sg
        d_gate = d_yb0 * val * sg * (1.0 - sg)

        dh = jnp.zeros((ts, D), F32)
        for q, dzq in enumerate((d_u, d_v, d_val, d_gate)):
            a2_ref[0:1, q * PW_IN:(q + 1) * PW_IN] += _colsum(dzq)
            dzb = dzq.astype(MM_DTYPE)
            dz_ref[:, q * PW_IN:(q + 1) * PW_IN] = dzb
            dh = dh + _dot_nt(dzb, w4_ref[q])
        x_t = x_ref[...]
        r1 = lax.rsqrt(_rowsum(x_t * x_t) * (1.0 / D) + EPS)
        g1 = g1_ref[...]
        xr = x_t * r1
        n1 = xr * g1
        h_ref[...] = (n1 * (1.0 + scale1) + shift1).astype(MM_DTYPE)
        dn1 = dh * (1.0 + scale1)
        t1 = dn1 * g1
        gx_ref[...] = dx1_t + r1 * t1 - x_t * (r1 * r1 * r1) * (_rowsum(t1 * x_t) * (1.0 / D))
        a1_ref[0:1, :] += _colsum(dh)
        a1_ref[1:2, :] += _colsum(dh * n1)
        a1_ref[2:3, :] += _colsum(dn1 * xr)

        @pl.when(i == nt - 1)
        def _():
            asb_ref[...] = _group_sum(asb_ref[...])
            finish()

    row = lambda w: pl.BlockSpec((ts, w), lambda i: (nt - 1 - i, 0))
    full = lambda a: pl.BlockSpec(a.shape, lambda i: (0,) * a.ndim)
    keep = lambda shape: pl.BlockSpec(shape, lambda i: (0,) * len(shape))
    return pl.pallas_call(
        body, name="mixer_bwd", grid=(nt,),
        in_specs=[row(D), row(D), row(4 * PW_IN), row(DB), full(mod), full(mods), full(norm1_g), _whole_vmem(),
                  full(ln_g), full(ln_b), _whole_vmem(), _whole_vmem(), full(bst), full(conv_w), full(gn_g), full(gn_b),
                  full(ga), full(gb), _whole_vmem()] + [_any()] * n_cs,
        out_specs=[row(D), row(4 * PW_IN), row(D), keep((8, D)), keep((8, 4 * PW_IN)), keep((8, DA)),
                   keep((HALO, DB)), keep((N_HEADS, CHUNK, CHUNK)), keep((CHUNK, DA))] + [_any()] * n_cs,
        out_shape=[SDS((s, D), F32), SDS((s, 4 * PW_IN), MM_DTYPE), SDS((s, D), MM_DTYPE), SDS((8, D), F32),
                   SDS((8, 4 * PW_IN), F32), SDS((8, DA), F32), SDS((HALO, DB), F32),
                   SDS((N_HEADS, CHUNK, CHUNK), F32), SDS((CHUNK, DA), F32)]
        + [_scattered_shape(p) for p in partials],
        scratch_shapes=[pltpu.VMEM((ts + HALO, DB), F32), pltpu.VMEM((7, ts + HALO - 8, DB), F32),
                        pltpu.VMEM((ts, DA), F32), pltpu.VMEM((ts, DA), F32),
                        pltpu.SemaphoreType.DMA((n_cs, 7)), pltpu.SemaphoreType.DMA((n_cs, 7)),
                        pltpu.SemaphoreType.DMA((n_cs,))],
        compiler_params=_params("arbitrary"),
    )(dx1, x, z, yb1, mod, mods, norm1_g, w_in4, ln_g, ln_b, wm, wmt, bst, conv_w, gn_g, gn_b, ga, gb, w_out, *partials)


def _gather8_plan(x_ref, out_ref, send_sems, recv_sems, local_sem):
    x, y, c = _place()
    me, sibling = (x, y, c), (x, y, 1 - c)
    chips = _other_chips(x, y)

    def copy(k, block, to, src=None):
        dst = out_ref.at[4 * block[0] + 2 * block[1] + block[2]]
        return pltpu.make_async_remote_copy(src_ref=dst if src is None else src, dst_ref=dst, send_sem=send_sems.at[k],
                                            recv_sem=recv_sems.at[k], device_id=to, device_id_type=MESH)

    def own():
        return pltpu.make_async_copy(x_ref, out_ref.at[4 * x + 2 * y + c], local_sem)

    def send():
        own().start()
        copy(0, me, sibling, src=x_ref).start()
        for j, chip in enumerate(chips):
            copy(1 + j, me, (*chip, c), src=x_ref).start()

    def forward():
        for j, chip in enumerate(chips):
            copy(1 + j, (*chip, c), me).wait_recv()
            copy(4 + j, (*chip, c), sibling).start()

    def finish():
        copy(0, sibling, me).wait_recv()
        for j, chip in enumerate(chips):
            copy(4 + j, (*chip, 1 - c), me).wait_recv()
        for k in range(7):
            copy(k, me, sibling).wait_send()
        own().wait()

    return send, forward, finish


def _grad_matmul(name, a, b, ka_tile, nb_tile, piece_w=None, gated=None, gather_blk=None):
    s, ka = a.shape
    nb = b.shape[1]
    ts = min(GRAD_ROW_TILE, s)
    nt = s // ts
    nja, njb = ka // ka_tile, nb // nb_tile
    steps = nja * njb * nt
    n_in = 2 + (2 if gated else 0) + (1 if gather_blk is not None else 0)
    n_out = 1 + (1 if gated else 0) + (1 if gather_blk is not None else 0)
    assert not (gated and njb != 1) and not (piece_w and nja != 1)

    def body(*refs):
        ins, outs, scratch = refs[:n_in], refs[n_in:n_in + n_out], refs[n_in + n_out:]
        a_ref, b_ref, o_ref, acc_ref = ins[0], ins[1], outs[0], scratch[0]
        ins = ins[2:]
        step = (pl.program_id(0) * njb + pl.program_id(1)) * nt + pl.program_id(2)
        if gather_blk is not None:
            send, forward, finish = _gather8_plan(ins[-1], outs[-1], *scratch[1:])

            @pl.when(step == 0)
            def _():
                send()

            @pl.when(step == (3 * steps) // 4)
            def _():
                forward()

        prod = _dot_tn(a_ref[...].astype(MM_DTYPE), b_ref[...].astype(MM_DTYPE))

        @pl.when(pl.program_id(2) == 0)
        def _():
            acc_ref[...] = prod

        @pl.when(pl.program_id(2) > 0)
        def _():
            acc_ref[...] += prod

        @pl.when(pl.program_id(2) == nt - 1)
        def _():
            gm = acc_ref[...]
            if gated:
                gate_ref, w_ref, dg_ref = ins[0], ins[1], outs[1]

                @pl.when(step == nt - 1)
                def _():
                    dg_ref[...] = jnp.zeros(dg_ref.shape, F32)

                dg_ref[0:1, :] += _colsum(gm * w_ref[...].astype(F32))
                gm = gm * gate_ref[...]
            if piece_w:
                for q in range(nb_tile // piece_w):
                    o_ref[q] = gm[:, q * piece_w:(q + 1) * piece_w].astype(WIRE_DTYPE)
            else:
                o_ref[...] = gm.astype(WIRE_DTYPE)

        if gather_blk is not None:
            @pl.when(step == steps - 1)
            def _():
                finish()

    in_specs = [pl.BlockSpec((ts, ka_tile), lambda ja, jb, i: (i, ja)),
                pl.BlockSpec((ts, nb_tile), lambda ja, jb, i: (i, jb))]
    operands = [a, b]
    if piece_w:
        out_shape = [SDS((nb // piece_w, ka, piece_w), WIRE_DTYPE)]
        out_specs = [pl.BlockSpec((nb_tile // piece_w, ka, piece_w), lambda ja, jb, i: (jb, 0, 0))]
    else:
        out_shape = [SDS((ka, nb), WIRE_DTYPE)]
        out_specs = [pl.BlockSpec((ka_tile, nb_tile), lambda ja, jb, i: (ja, jb))]
    scratch = [pltpu.VMEM((ka_tile, nb_tile), F32)]
    if gated:
        in_specs += [pl.BlockSpec((1, nb_tile), lambda ja, jb, i: (0, jb)),
                     pl.BlockSpec((ka_tile, nb_tile), lambda ja, jb, i: (ja, jb))]
        operands += list(gated)
        out_shape.append(SDS((8, nb), F32))
        out_specs.append(pl.BlockSpec((8, nb_tile), lambda ja, jb, i: (0, jb)))
    if gather_blk is not None:
        in_specs.append(_any())
        operands.append(gather_blk)
        out_shape.append(SDS((8,) + gather_blk.shape, gather_blk.dtype))
        out_specs.append(_any())
        scratch += [pltpu.SemaphoreType.DMA((7,)), pltpu.SemaphoreType.DMA((7,)), pltpu.SemaphoreType.DMA]
    return pl.pallas_call(
        body, name=name, grid=(nja, njb, nt), in_specs=in_specs, out_specs=out_specs, out_shape=out_shape,
        scratch_shapes=scratch, compiler_params=_params("arbitrary", "arbitrary", "arbitrary"),
    )(*operands)


def _grad_w_in_scattered(h, dz, blk):
    s = h.shape[0]
    ts = min(GRAD_ROW_TILE, s)
    nt = s // ts
    steps = 2 * nt
    half_rows = D // 2

    def body(a_ref, b_ref, blk_ref, arr_ref, all_ref, acc_ref, stage_ref, send_sems, recv_sems, g_send, g_recv, g_local):
        ja, i = pl.program_id(0), pl.program_id(1)
        step = ja * nt + i
        x, y, c = _place()
        me = 4 * x + 2 * y + c
        gsend, gforward, gfinish = _gather8_plan(blk_ref, all_ref, g_send, g_recv, g_local)

        @pl.when(step == 0)
        def _():
            gsend()

        @pl.when(step == (3 * steps) // 4)
        def _():
            gforward()

        prod = _dot_tn(a_ref[...], b_ref[...])

        @pl.when(i == 0)
        def _():
            acc_ref[...] = prod

        @pl.when(i > 0)
        def _():
            acc_ref[...] += prod

        def copies(half):
            out = []
            for q in range(NQ):
                src, dst = stage_ref.at[half, q], arr_ref.at[me]
                remote = pltpu.make_async_remote_copy(src_ref=src, dst_ref=dst, send_sem=send_sems.at[half, q],
                                                      recv_sem=recv_sems.at[me], device_id=(q // 2, q % 2, half),
                                                      device_id_type=MESH)
                local = pltpu.make_async_copy(src, dst, send_sems.at[half, q])
                out.append((remote, local, (2 * x + y == q) & (c == half)))
            return out

        for half in range(2):
            @pl.when((ja == half) & (i == nt - 1))
            def _():
                gm = acc_ref[...]
                for q in range(NQ):
                    stage_ref[half, q] = gm[:, q * PW_IN:(q + 1) * PW_IN].astype(WIRE_DTYPE)
                for remote, local, is_self in copies(half):
                    @pl.when(is_self)
                    def _():
                        local.start()

                    @pl.when(jnp.logical_not(is_self))
                    def _():
                        remote.start()

        @pl.when(step == steps - 1)
        def _():
            gfinish()
            for d in range(8):
                @pl.when(me != d)
                def _():
                    pltpu.make_async_remote_copy(src_ref=stage_ref.at[0, 0], dst_ref=arr_ref.at[d], send_sem=send_sems.at[0, 0],
                                                 recv_sem=recv_sems.at[d], device_id=(x, y, c), device_id_type=MESH).wait_recv()
            for half in range(2):
                for remote, local, is_self in copies(half):
                    @pl.when(is_self)
                    def _():
                        local.wait()

                    @pl.when(jnp.logical_not(is_self))
                    def _():
                        remote.wait_send()

    return pl.pallas_call(
        body, name="grad_w_in", grid=(2, nt),
        in_specs=[pl.BlockSpec((ts, half_rows), lambda ja, i: (i, ja)), pl.BlockSpec((ts, NQ * PW_IN), lambda ja, i: (i, 0)),
                  _any()],
        out_specs=[_any(), _any()],
        out_shape=[SDS((8, half_rows, PW_IN), WIRE_DTYPE), SDS((8,) + blk.shape, blk.dtype)],
        scratch_shapes=[pltpu.VMEM((half_rows, NQ * PW_IN), F32), pltpu.VMEM((2, NQ, half_rows, PW_IN), WIRE_DTYPE),
                        pltpu.SemaphoreType.DMA((2, NQ)), pltpu.SemaphoreType.DMA((8,)),
                        pltpu.SemaphoreType.DMA((7,)), pltpu.SemaphoreType.DMA((7,)), pltpu.SemaphoreType.DMA],
        compiler_params=_params("arbitrary", "arbitrary"),
    )(h, dz, blk)


COND_COLS = 512


def _cond_partial(c_all, w_a, w_f):
    na = w_a.shape[1]

    def body(c_ref, wa_ref, wf_ref, oa_ref, of_ref):
        c_t = c_ref[...]
        ca = (c_t * _sigmoid(c_t)).astype(MM_DTYPE)
        oa_ref[...] = _dot(ca, wa_ref[...].astype(MM_DTYPE))

        @pl.when(pl.program_id(0) == 0)
        def _():
            of_ref[...] = _dot(ca, wf_ref[...].astype(MM_DTYPE))

    keep = lambda shape: pl.BlockSpec(shape, lambda j: (0, 0))
    return pl.pallas_call(
        body, name="cond_partial", grid=(na // COND_COLS,),
        in_specs=[keep((8, D)), pl.BlockSpec((D, COND_COLS), lambda j: (0, j)), keep(w_f.shape)],
        out_specs=[pl.BlockSpec((8, COND_COLS), lambda j: (0, j)), keep((8, w_f.shape[1]))],
        out_shape=[SDS((8, na), F32), SDS((8, w_f.shape[1]), F32)],
        compiler_params=_params("arbitrary"),
    )(c_all, w_a, w_f)


def _cond_grad(c_all, dmod_a, dmod_f):
    na = dmod_a.shape[1]

    def body(c_ref, da_ref, df_ref, oa_ref, of_ref):
        c_t = c_ref[...]
        ca = jnp.concatenate([c_t * _sigmoid(c_t), jnp.zeros((8, D), F32)], axis=0).astype(MM_DTYPE)

        def outer(d_ref):
            dm = jnp.concatenate([d_ref[...], jnp.zeros(d_ref.shape, F32)], axis=0).astype(MM_DTYPE)
            return _dot_tn(ca, dm)

        oa_ref[...] = outer(da_ref)

        @pl.when(pl.program_id(0) == 0)
        def _():
            of_ref[...] = outer(df_ref)

    keep = lambda shape: pl.BlockSpec(shape, lambda j: (0, 0))
    return pl.pallas_call(
        body, name="cond_grad", grid=(na // COND_COLS,),
        in_specs=[keep((8, D)), pl.BlockSpec((8, COND_COLS), lambda j: (0, j)), keep(dmod_f.shape)],
        out_specs=[pl.BlockSpec((D, COND_COLS), lambda j: (0, j)), keep((D, dmod_f.shape[1]))],
        out_shape=[SDS((D, na), F32), SDS((D, dmod_f.shape[1]), F32)],
        compiler_params=_params("arbitrary"),
    )(c_all, dmod_a, dmod_f)


def _row_tile(rows, cap=256):
    if rows <= cap:
        return rows
    for t in range(cap, 7, -8):
        if rows % t == 0:
            return t
    return rows


def _ordered_sum(name, parts, into_half=None):
    n, rows, cols = parts.shape
    rt = _row_tile(rows)
    nb = rows // rt

    def body(*refs):
        p_ref, o_ref = refs[-2:]
        acc = p_ref[0].astype(F32)
        for k in range(1, n):
            acc = acc + p_ref[k].astype(F32)
        o_ref[...] = acc

    if into_half is None:
        return pl.pallas_call(
            body, name=name, grid=(nb,),
            in_specs=[pl.BlockSpec((n, rt, cols), lambda i: (0, i, 0))],
            out_specs=pl.BlockSpec((rt, cols), lambda i: (i, 0)), out_shape=SDS((rows, cols), F32),
            compiler_params=_params("parallel"),
        )(parts)
    grid_spec = pltpu.PrefetchScalarGridSpec(
        num_scalar_prefetch=1, grid=(nb,),
        in_specs=[pl.BlockSpec((n, rt, cols), lambda i, c_ref: (0, i, 0))],
        out_specs=pl.BlockSpec((rt, cols), lambda i, c_ref: (c_ref[0] * nb + i, 0)))
    return pl.pallas_call(
        body, name=name, grid_spec=grid_spec, out_shape=SDS((2 * rows, cols), F32),
        compiler_params=_params("parallel"),
    )(into_half.astype(jnp.int32).reshape(1), parts)


def _adamw_update(w_ref, g_ref, m_ref, v_ref, d_ref, nm_ref, nv_ref):
    c1 = 1.0 - ADAM_B1 ** ADAM_STEP
    c2 = 1.0 - ADAM_B2 ** ADAM_STEP
    g_t = g_ref[...]
    m_new = ADAM_B1 * m_ref[...] + (1.0 - ADAM_B1) * g_t
    v_new = ADAM_B2 * v_ref[...] + (1.0 - ADAM_B2) * (g_t * g_t)
    nm_ref[...] = m_new
    nv_ref[...] = v_new
    d_ref[...] = -ADAM_LR * ((m_new / c1) / (jnp.sqrt(v_new / c2) + ADAM_EPS) + ADAM_WD * w_ref[...])


def _adamw_many(name, ws, gs, ms, vs):
    n = len(ws)

    def body(*refs):
        ins, outs = refs[:4 * n], refs[4 * n:]
        for k in range(n):
            _adamw_update(ins[k], ins[n + k], ins[2 * n + k], ins[3 * n + k], *outs[3 * k:3 * k + 3])

    return pl.pallas_call(
        body, name=name, out_shape=[SDS(w.shape, F32) for w in ws for _ in range(3)],
        compiler_params=pltpu.CompilerParams(vmem_limit_bytes=VMEM_LIMIT),
    )(*ws, *gs, *ms, *vs)


def _adamw_tiled(name, ws, gs, ms, vs):
    n = len(ws)

    def body(*refs):
        ins, outs = refs[:4 * n], refs[4 * n:]
        for k in range(n):
            _adamw_update(ins[k], ins[n + k], ins[2 * n + k], ins[3 * n + k], *outs[3 * k:3 * k + 3])

    specs = [pl.BlockSpec((w.shape[0] // ADAMW_STEPS, w.shape[1]), lambda i: (i, 0)) for w in ws]
    return pl.pallas_call(
        body, name=name, grid=(ADAMW_STEPS,), in_specs=specs * 4, out_specs=[s for s in specs for _ in range(3)],
        out_shape=[SDS(w.shape, F32) for w in ws for _ in range(3)], compiler_params=_params("parallel"),
    )(*ws, *gs, *ms, *vs)


def _place():
    return lax.axis_index("x"), lax.axis_index("y"), lax.axis_index("c")


def _other_chips(x, y):
    return [(1 - x, y), (x, 1 - y), (1 - x, 1 - y)]


def _all_gather8(name, blk):
    m, n = blk.shape

    def body(x_ref, out_ref, send_sems, recv_sems, local_sem):
        x, y, c = _place()
        me, sibling = (x, y, c), (x, y, 1 - c)
        chips = _other_chips(x, y)

        def slot(px, py, pc):
            return out_ref.at[4 * px + 2 * py + pc]

        def copy(k, block, to, src=None):
            return pltpu.make_async_remote_copy(
                src_ref=slot(*block) if src is None else src, dst_ref=slot(*block),
                send_sem=send_sems.at[k], recv_sem=recv_sems.at[k], device_id=to, device_id_type=MESH)

        mine = pltpu.make_async_copy(x_ref, slot(*me), local_sem)
        mine.start()
        first = [copy(0, me, sibling, src=x_ref)]
        first += [copy(1 + j, me, (*chip, c), src=x_ref) for j, chip in enumerate(chips)]
        for cp in first:
            cp.start()
        passed = [copy(4 + j, (*chip, c), sibling) for j, chip in enumerate(chips)]
        for j, chip in enumerate(chips):
            copy(1 + j, (*chip, c), me).wait_recv()
            passed[j].start()
        copy(0, sibling, me).wait_recv()
        for j, chip in enumerate(chips):
            copy(4 + j, (*chip, 1 - c), me).wait_recv()
        for cp in first + passed:
            cp.wait_send()
        mine.wait()

    return pl.pallas_call(
        body, name=name, out_shape=SDS((8, m, n), blk.dtype),
        in_specs=[_whole_vmem()], out_specs=_whole_vmem(),
        scratch_shapes=[pltpu.SemaphoreType.DMA((7,)), pltpu.SemaphoreType.DMA((7,)), pltpu.SemaphoreType.DMA],
        compiler_params=pltpu.CompilerParams(vmem_limit_bytes=VMEM_LIMIT),
    )(blk)


def _any():
    return pl.BlockSpec(memory_space=pl.ANY)


def _gather_weights(shards, blk):
    n = len(shards)

    def body(*refs):
        ins, outs, sems = refs[:n + 1], refs[n + 1:2 * n + 2], refs[2 * n + 2:]
        send, forward, finish = _gather_plan(ins[:n], outs[:n], sems[0], sems[1])
        send8, forward8, finish8 = _gather8_plan(ins[n], outs[n], *sems[2:])
        send8()
        send()
        forward8()
        forward()
        finish8()
        finish()

    return pl.pallas_call(
        body, name="gather_weights",
        out_shape=[SDS((NQ,) + s.shape, s.dtype) for s in shards] + [SDS((8,) + blk.shape, blk.dtype)],
        in_specs=[_any()] * (n + 1), out_specs=[_any()] * (n + 1),
        scratch_shapes=[pltpu.SemaphoreType.DMA((n, 6)), pltpu.SemaphoreType.DMA((n, 6)),
                        pltpu.SemaphoreType.DMA((7,)), pltpu.SemaphoreType.DMA((7,)), pltpu.SemaphoreType.DMA],
    )(*shards, blk)


def _own_piece(gathered, shard):
    myq = 2 * lax.axis_index("x") + lax.axis_index("y")
    return lax.dynamic_update_slice(gathered, shard[None], (myq,) + (0,) * shard.ndim)


def _scatter_to_owners(partials):
    n = len(partials)

    def body(*refs):
        send, finish = _scatter_plan(refs[:n], refs[n:2 * n], *refs[2 * n:])
        send()
        finish()

    return pl.pallas_call(
        body, name="scatter_to_owners",
        out_shape=[_scattered_shape(p) for p in partials],
        in_specs=[_any()] * n, out_specs=[_any()] * n,
        scratch_shapes=[pltpu.SemaphoreType.DMA((n, 7)), pltpu.SemaphoreType.DMA((n, 7))],
    )(*partials)


def _owner_sums(tag, arrived, partials=None):
    x, y, c = _place()
    sums = []
    for w, arr in enumerate(arrived):
        if partials is not None:
            part = partials[w]
            h = part.shape[1] // 2
            own = lax.dynamic_slice(part, (2 * x + y, c * h, 0), (1, h, part.shape[2]))
            arr = lax.dynamic_update_slice(arr, own, (4 * x + 2 * y + c, 0, 0))
        sums.append(_ordered_sum(f"owner_sum_{tag}_{w}", arr, into_half=c))
    return sums


def _join_halves(bufs):
    n = len(bufs)

    def body(*refs):
        ins, outs = refs[:n], refs[n:2 * n]
        send_sems, recv_sems = refs[2 * n:]
        x, y, c = _place()
        cps = []
        for w in range(n):
            h = ins[w].shape[0] // 2
            mine = outs[w].at[pl.ds(c * h, h)]
            cp = pltpu.make_async_remote_copy(src_ref=mine, dst_ref=mine, send_sem=send_sems.at[w],
                                              recv_sem=recv_sems.at[w], device_id=(x, y, 1 - c), device_id_type=MESH)
            cp.start()
            cps.append(cp)
        for cp in cps:
            cp.wait()

    return pl.pallas_call(
        body, name="join_halves",
        out_shape=[SDS(b.shape, b.dtype) for b in bufs],
        in_specs=[_any()] * n, out_specs=[_any()] * n, input_output_aliases={w: w for w in range(n)},
        scratch_shapes=[pltpu.SemaphoreType.DMA((n,)), pltpu.SemaphoreType.DMA((n,))],
    )(*bufs)


def _pad_rows(a, rows):
    return jnp.pad(a, ((0, rows - a.shape[0]),) + ((0, 0),) * (a.ndim - 1))


def _pack_small(dmod, g1, g2, gf, b_in, ln_g, ln_b, conv_b, gn_g, gn_b, ga, gb, sb, cw32, sw, loss_row):
    v512 = jnp.concatenate([ln_g, ln_b, conv_b, gn_g, gn_b, ga, gb, jnp.zeros((1, DA), F32)], axis=1).reshape(4, D)
    rows = [dmod.reshape(8, D), g1, g2, gf, b_in.reshape(2, D), v512, sb.reshape(1, D), cw32.reshape(16, D),
            sw.reshape(CHUNK, D), loss_row]
    packed = jnp.concatenate(rows, axis=0)
    return _pad_rows(packed, PK_ROWS)


def _unpack_small(p):
    v512 = p[PK_V512:PK_V512 + 4].reshape(1, 8 * DA)
    pieces = [v512[:, k * DA:(k + 1) * DA] for k in range(7)]
    return dict(
        dmod=p[PK_DMOD:PK_DMOD + 8].reshape(1, 8 * D), norm1_g=p[PK_G1:PK_G1 + 1], norm2_g=p[PK_G2:PK_G2 + 1],
        norm_f_g=p[PK_GF:PK_GF + 1], b_in=p[PK_BIN:PK_BIN + 2].reshape(1, 2 * D),
        a_ln_g=pieces[0], a_ln_b=pieces[1], b_conv_b=pieces[2], b_gn_g=pieces[3], b_gn_b=pieces[4],
        out_norm_a_g=pieces[5], out_norm_b_g=pieces[6],
        a_spatial_b=p[PK_SB:PK_SB + 1].reshape(N_HEADS, CHUNK),
        b_conv_w=p[PK_CW:PK_CW + 16].reshape(HALO, DB),
        a_spatial_w=p[PK_SW:PK_SW + CHUNK].reshape(N_HEADS, CHUNK, CHUNK))


def kernel(x, c, ada_w, ada_b, norm1_g, w_in, b_in, a_ln_g, a_ln_b, a_spatial_w, a_spatial_b, b_conv_w, b_conv_b, b_gn_g, b_gn_b, out_norm_a_g, out_norm_b_g, w_out, norm2_g, w_ffn_in, w_ffn_out, ada_f_w, ada_f_b, norm_f_g, loss_target, m_ada_w, m_ada_b, m_norm1_g, m_w_in, m_b_in, m_a_ln_g, m_a_ln_b, m_a_spatial_w, m_a_spatial_b, m_b_conv_w, m_b_conv_b, m_b_gn_g, m_b_gn_b, m_out_norm_a_g, m_out_norm_b_g, m_w_out, m_norm2_g, m_w_ffn_in, m_w_ffn_out, m_ada_f_w, m_ada_f_b, m_norm_f_g, v_ada_w, v_ada_b, v_norm1_g, v_w_in, v_b_in, v_a_ln_g, v_a_ln_b, v_a_spatial_w, v_a_spatial_b, v_b_conv_w, v_b_conv_b, v_b_gn_g, v_b_gn_b, v_out_norm_a_g, v_out_norm_b_g, v_w_out, v_norm2_g, v_w_ffn_in, v_w_ffn_out, v_ada_f_w, v_ada_f_b, v_norm_f_g):
    mx, my, mc = _place()
    me = 4 * mx + 2 * my + mc
    myq = 2 * mx + my
    xs = x[0]
    target = loss_target[0]
    s = xs.shape[0]
    n_ada = ada_w.shape[2]

    cw_shard = _pad_rows(b_conv_w[0], HALO)
    mix_shards = [w_in[0].astype(MM_DTYPE), w_out[0].astype(MM_DTYPE)]
    ffn_shards = [w_ffn_in[0].astype(MM_DTYPE), w_ffn_out[0].astype(MM_DTYPE)]
    w_in4, w_out4, first = _gather_weights(mix_shards, jnp.concatenate([c.reshape(8, LANES), cw_shard], axis=0))
    w_in4, w_out4 = _own_piece(w_in4, mix_shards[0]), _own_piece(w_out4, mix_shards[1])
    w_out_f = w_out4.reshape(D, D)
    c_all = first[:, 0:8, :].reshape(8, D)
    conv_w = jnp.concatenate([first[4 * (q // 2) + 2 * (q % 2), 8:8 + HALO, :] for q in range(NQ)], axis=1)
    cond_part = jnp.concatenate(_cond_partial(c_all, ada_w[0], ada_f_w), axis=1)
    cond_all = _all_gather8("gather_cond", cond_part)
    cond_q = [cond_all[4 * (q // 2) + 2 * (q % 2)] for q in range(NQ)]
    mod_all = jnp.concatenate([cq[:, :n_ada] for cq in cond_q] + [cq[:, n_ada:] for cq in cond_q], axis=1)
    mod = lax.dynamic_slice_in_dim(mod_all, me, 1, axis=0)
    mods = jnp.concatenate([ada_b, ada_f_b.reshape(1, 2 * D)], axis=1)

    causal = jnp.tril(jnp.ones((CHUNK, CHUNK), dtype=bool))
    wm_f = jnp.where(causal[None], a_spatial_w[0], 0.0)
    wm = wm_f.astype(MM_DTYPE)
    wmt = jnp.swapaxes(wm_f, 1, 2).astype(MM_DTYPE)
    bst = jnp.repeat(a_spatial_b[0].T, HALF, axis=1)

    z, x1, yb1, y, w_ffn_in4, w_ffn_out4 = _mixer_fwd(
        xs, mod, mods, norm1_g, w_in4, b_in, a_ln_g, a_ln_b, wm, bst, conv_w, b_conv_b, b_gn_g, b_gn_b,
        out_norm_a_g, out_norm_b_g, w_out_f, ffn_shards)
    w_ffn_out_f = w_ffn_out4.reshape(DFF, D)
    g, up, a_act, h2, dx2, acc_f = _ffn_fwd(x1, target, mod, mods, norm2_g, norm_f_g, w_ffn_in4, w_ffn_out_f)

    dff, dx1, acc_2 = _ffn_bwd(dx2, x1, g, up, mod, mods, norm2_g, w_ffn_in4, w_ffn_out_f)
    (gw_ffn_in4,) = _grad_matmul("grad_w_ffn_in", h2, dff, D, PW_FF, piece_w=PW_FF)
    modv = mod + mods
    gw_ffn_out, dgate2 = _grad_matmul("grad_w_ffn_out", a_act, dx2, PW_FF, D, gated=(modv[:, 5 * D:6 * D], w_ffn_out_f))
    gw_out, dgate1 = _grad_matmul("grad_w_out", y, dx1, D, D, gated=(modv[:, 2 * D:3 * D], w_out_f))
    early_partials = [gw_ffn_in4, gw_ffn_out.reshape(NQ, DFF // NQ, D), gw_out.reshape(NQ, D // NQ, D)]
    gx, dz, h, acc_1, acc_bin, acc_5, acc_cw, acc_sw, acc_sb, *early_arrived = _mixer_bwd(
        dx1, xs, z, yb1, mod, mods, norm1_g, w_in4, a_ln_g, a_ln_b, wm, wmt, bst, conv_w, b_gn_g, b_gn_b,
        out_norm_a_g, out_norm_b_g, w_out_f, early_partials)

    dmod = jnp.concatenate([acc_1[0:1], acc_1[1:2], dgate1[0:1], acc_2[0:1], acc_2[1:2], dgate2[0:1],
                            acc_f[0:1], acc_f[1:2]], axis=1)
    sw_grad = jnp.where(causal[None], acc_sw, 0.0)
    sb_grad = acc_sb[:, ::HALF].T
    packed = _pack_small(dmod, acc_1[2:3], acc_2[2:3], acc_f[2:3], acc_bin[0:1], acc_5[2:3], acc_5[3:4], acc_5[6:7],
                         acc_5[4:5], acc_5[5:6], acc_5[0:1], acc_5[1:2], sb_grad, acc_cw, sw_grad, acc_f[4:5])
    late_arrived, gathered = _grad_w_in_scattered(h, dz, packed)
    g_w_in, g_w_ffn_in, g_w_ffn_out, g_w_out = _join_halves(
        _owner_sums("late", [late_arrived]) + _owner_sums("early", early_arrived))
    summed = _ordered_sum("small_grad_sum", gathered)
    loss = summed[PK_LOSS, 0]
    small = _unpack_small(summed)
    dmod_all = gathered[:, PK_DMOD:PK_DMOD + 8, :].reshape(8, 8 * D)
    g_ada_w, g_ada_f_w = _cond_grad(c_all, lax.dynamic_slice_in_dim(dmod_all, myq * n_ada, n_ada, axis=1),
                                    lax.dynamic_slice_in_dim(dmod_all, 6 * D + myq * PW_IN, PW_IN, axis=1))

    grads = dict(
        ada_w=g_ada_w, ada_b=small["dmod"][:, :6 * D], norm1_g=small["norm1_g"], w_in=g_w_in,
        b_in=small["b_in"], a_ln_g=small["a_ln_g"], a_ln_b=small["a_ln_b"], a_spatial_w=small["a_spatial_w"],
        a_spatial_b=small["a_spatial_b"],
        b_conv_w=lax.dynamic_slice_in_dim(small["b_conv_w"], myq * LANES, LANES, axis=1)[:CONV_W],
        b_conv_b=small["b_conv_b"], b_gn_g=small["b_gn_g"], b_gn_b=small["b_gn_b"],
        out_norm_a_g=small["out_norm_a_g"], out_norm_b_g=small["out_norm_b_g"], w_out=g_w_out,
        norm2_g=small["norm2_g"], w_ffn_in=g_w_ffn_in, w_ffn_out=g_w_ffn_out, ada_f_w=g_ada_f_w,
        ada_f_b=small["dmod"][:, 6 * D:], norm_f_g=small["norm_f_g"])

    weights = dict(ada_w=ada_w, ada_b=ada_b, norm1_g=norm1_g, w_in=w_in, b_in=b_in, a_ln_g=a_ln_g, a_ln_b=a_ln_b,
                   a_spatial_w=a_spatial_w, a_spatial_b=a_spatial_b, b_conv_w=b_conv_w, b_conv_b=b_conv_b, b_gn_g=b_gn_g,
                   b_gn_b=b_gn_b, out_norm_a_g=out_norm_a_g, out_norm_b_g=out_norm_b_g, w_out=w_out, norm2_g=norm2_g,
                   w_ffn_in=w_ffn_in, w_ffn_out=w_ffn_out, ada_f_w=ada_f_w, ada_f_b=ada_f_b, norm_f_g=norm_f_g)
    m_in = dict(ada_w=m_ada_w, ada_b=m_ada_b, norm1_g=m_norm1_g, w_in=m_w_in, b_in=m_b_in, a_ln_g=m_a_ln_g, a_ln_b=m_a_ln_b,
                a_spatial_w=m_a_spatial_w, a_spatial_b=m_a_spatial_b, b_conv_w=m_b_conv_w, b_conv_b=m_b_conv_b,
                b_gn_g=m_b_gn_g, b_gn_b=m_b_gn_b, out_norm_a_g=m_out_norm_a_g, out_norm_b_g=m_out_norm_b_g, w_out=m_w_out,
                norm2_g=m_norm2_g, w_ffn_in=m_w_ffn_in, w_ffn_out=m_w_ffn_out, ada_f_w=m_ada_f_w, ada_f_b=m_ada_f_b,
                norm_f_g=m_norm_f_g)
    v_in = dict(ada_w=v_ada_w, ada_b=v_ada_b, norm1_g=v_norm1_g, w_in=v_w_in, b_in=v_b_in, a_ln_g=v_a_ln_g, a_ln_b=v_a_ln_b,
                a_spatial_w=v_a_spatial_w, a_spatial_b=v_a_spatial_b, b_conv_w=v_b_conv_w, b_conv_b=v_b_conv_b,
                b_gn_g=v_b_gn_g, b_gn_b=v_b_gn_b, out_norm_a_g=v_out_norm_a_g, out_norm_b_g=v_out_norm_b_g, w_out=v_w_out,
                norm2_g=v_norm2_g, w_ffn_in=v_w_ffn_in, w_ffn_out=v_w_ffn_out, ada_f_w=v_ada_f_w, ada_f_b=v_ada_f_b,
                norm_f_g=v_norm_f_g)
    names = list(weights)
    big = ("ada_w", "w_in", "w_out", "w_ffn_in", "w_ffn_out", "ada_f_w")

    def flat2(a):
        return a.reshape(-1, a.shape[-1])

    delta, new_m, new_v = {}, {}, {}
    for nm in big:
        grads[nm] = grads[nm].reshape(weights[nm].shape)
    big_out = _adamw_tiled("adamw_large", *[[flat2(tree[nm]) for nm in big] for tree in (weights, grads, m_in, v_in)])
    for k, nm in enumerate(big):
        shape = weights[nm].shape
        delta[nm], new_m[nm], new_v[nm] = [o.reshape(shape) for o in big_out[3 * k:3 * k + 3]]

    small_names = [nm for nm in names if nm not in big]
    for nm in small_names:
        grads[nm] = grads[nm].reshape(weights[nm].shape)
    small_out = _adamw_many("adamw_small", *[[flat2(tree[nm]) for nm in small_names] for tree in (weights, grads, m_in, v_in)])
    for k, nm in enumerate(small_names):
        shape = weights[nm].shape
        delta[nm], new_m[nm], new_v[nm] = [o.reshape(shape) for o in small_out[3 * k:3 * k + 3]]

    grad_x = gx.reshape(x.shape)
    return (loss, grad_x, *[grads[nm] for nm in names], *[delta[nm] for nm in names],
            *[new_m[nm] for nm in names], *[new_v[nm] for nm in names])
```

```python
import functools
import math

import jax
import jax.numpy as jnp
from jax import lax
from jax.experimental import pallas as pl
from jax.experimental.pallas import tpu as pltpu

F32 = jnp.float32
MM_DTYPE = jnp.bfloat16
WIRE_DTYPE = jnp.bfloat16
SDS = jax.ShapeDtypeStruct
MESH = pl.DeviceIdType.MESH

D = 1024
DA = 512
DB = 512
NQ = 4
PW_IN = 512
DFF = 2816
PW_FF = 1408
CHUNK = 128
N_HEADS = 8
CONV_W = 31
HALO = 32
CONV_ROWS = 64
EPS = 1e-6
LANES = 128
HALF = 64

ROW_TILE = 256
FWD_ROW_TILE = 512
FFN_ROW_TILE = 256
GRAD_ROW_TILE = 2048
ADAMW_STEPS = 8
W_IN_SPLIT = 2
VMEM_LIMIT = 60 * 1024 * 1024

ADAM_LR = 0.001
ADAM_B1 = 0.9
ADAM_B2 = 0.999
ADAM_EPS = 1e-08
ADAM_WD = 0.01
ADAM_STEP = 10

PK_DMOD = 0
PK_G1 = 8
PK_G2 = 9
PK_GF = 10
PK_BIN = 11
PK_V512 = 13
PK_SB = 17
PK_CW = 18
PK_SW = 34
PK_LOSS = 162
PK_ROWS = 168


def _dot(a, b):
    return jnp.dot(a, b, preferred_element_type=F32)


def _dot_nt(a, b):
    return lax.dot_general(a, b, (((1,), (1,)), ((), ())), preferred_element_type=F32)


def _dot_tn(a, b):
    return lax.dot_general(a, b, (((0,), (0,)), ((), ())), preferred_element_type=F32)


def _rowsum(x):
    return jnp.sum(x, axis=-1, keepdims=True)


def _colsum(x):
    return jnp.sum(x, axis=0, keepdims=True)


def _group_sum(x):
    rows, width = x.shape
    lo_mask = lax.broadcasted_iota(jnp.int32, (rows, LANES), 1) < HALF
    outs = []
    for jb in range(width // LANES):
        xb = x[:, jb * LANES:(jb + 1) * LANES]
        lo = _rowsum(jnp.where(lo_mask, xb, 0.0))
        hi = _rowsum(jnp.where(lo_mask, 0.0, xb))
        outs.append(jnp.where(lo_mask, lo, hi))
    return jnp.concatenate(outs, axis=-1)


def _sigmoid(x):
    return 1.0 / (1.0 + jnp.exp(-x))


def _gelu_parts(u):
    cdf = 0.5 * (1.0 + lax.erf(u * (1.0 / math.sqrt(2.0))))
    pdf = jnp.exp(-0.5 * u * u) * (1.0 / math.sqrt(2.0 * math.pi))
    return u * cdf, cdf + u * pdf


def _whole_vmem():
    return pl.BlockSpec(memory_space=pltpu.VMEM)


def _params(*semantics):
    return pltpu.CompilerParams(dimension_semantics=semantics, vmem_limit_bytes=VMEM_LIMIT)


def _mod_rows(mod_ref, modb_ref, first, count):
    m = mod_ref[...] + modb_ref[...]
    return [m[:, (first + k) * D:(first + k + 1) * D] for k in range(count)]


def _mixer_recompute(z_parts, lng, lnb, wm_ref, bst_ref, mix_ref):
    u, v, val, gate = z_parts
    rows = u.shape[0]
    gu, dgu = _gelu_parts(u)
    gv, dgv = _gelu_parts(v)
    mu = _rowsum(gv) * (1.0 / DA)
    vc = gv - mu
    rs = lax.rsqrt(_rowsum(vc * vc) * (1.0 / DA) + EPS)
    vhat = vc * rs
    vl = vhat * lng + lnb
    vlb = vl.astype(MM_DTYPE)
    lo_mask = lax.broadcasted_iota(jnp.int32, (CHUNK, LANES), 1) < HALF
    for ck in range(rows // CHUNK):
        for jb in range(DA // LANES):
            blk = vlb[ck * CHUNK:(ck + 1) * CHUNK, jb * LANES:(jb + 1) * LANES]
            a = _dot(wm_ref[2 * jb], blk)
            b = _dot(wm_ref[2 * jb + 1], blk)
            mix_ref[ck * CHUNK:(ck + 1) * CHUNK, jb * LANES:(jb + 1) * LANES] = (
                jnp.where(lo_mask, a, b) + bst_ref[:, jb * LANES:(jb + 1) * LANES])
    mixed = mix_ref[...]
    sg = _sigmoid(gate)
    yb0 = val * sg
    return dict(gu=gu, dgu=dgu, dgv=dgv, rs=rs, vhat=vhat, vlb=vlb, mixed=mixed, sg=sg, yb0=yb0)


def _conv_branch_tail(yb1, gng, gnb):
    gm = _group_sum(yb1) * (1.0 / HALF)
    gc = yb1 - gm
    grs = lax.rsqrt(_group_sum(gc * gc) * (1.0 / HALF) + EPS)
    ghat = gc * grs
    yb2 = ghat * gng + gnb
    s2 = _sigmoid(yb2)
    return dict(grs=grs, ghat=ghat, yb2=yb2, s2=s2, y_b=yb2 * s2)


def _shifted_copies(e_ref, sh_ref):
    n = sh_ref.shape[1]
    for b in range(1, 8):
        sh_ref[b - 1] = e_ref[pl.ds(b, n), :]


def _window(e_ref, sh_ref, offset, r0, nrows, cols):
    a, b = divmod(offset, 8)
    if b == 0:
        return e_ref[pl.ds(r0 + 8 * a, nrows), cols]
    return sh_ref[b - 1, pl.ds(r0 + 8 * a, nrows), cols]


def _conv_taps(e_ref, sh_ref, cw_ref, out_ref, ts, first_offset, flip, bias_ref=None, other_ref=None, tap_acc_ref=None):
    groups = CONV_ROWS // 8
    for cb in range(DB // LANES):
        cols = slice(cb * LANES, (cb + 1) * LANES)
        tap_acc = [jnp.zeros((8, LANES), F32) for _ in range(CONV_W)]
        for rb in range(ts // CONV_ROWS):
            r0 = rb * CONV_ROWS
            acc = jnp.zeros((CONV_ROWS, LANES), F32)
            if bias_ref is not None:
                acc = acc + bias_ref[:, cols]
            if other_ref is not None:
                other = other_ref[r0:r0 + CONV_ROWS, cols]
            for j in range(CONV_W):
                k = CONV_W - 1 - j if flip else j
                win = _window(e_ref, sh_ref, first_offset + j, r0, CONV_ROWS, cols)
                acc = acc + win * cw_ref[k:k + 1, cols]
                if other_ref is not None:
                    tap_acc[k] = tap_acc[k] + jnp.sum((other * win).reshape(groups, 8, LANES), axis=0)
            out_ref[r0:r0 + CONV_ROWS, cols] = acc
        if other_ref is not None:
            for k in range(CONV_W):
                tap_acc_ref[k:k + 1, cols] += _colsum(tap_acc[k])


def _gather_plan(ins, outs, send_sems, recv_sems, local_sems=None):
    x, y, c = _place()
    sibling = (x, y, 1 - c)
    chips = _other_chips(x, y)
    myq = 2 * x + y

    def copy(w, k, q, hc, to, src=None):
        rows = ins[w].shape[0]
        dst = outs[w].at[q, pl.ds(hc * (rows // 2), rows // 2)]
        return pltpu.make_async_remote_copy(
            src_ref=dst if src is None else src, dst_ref=dst,
            send_sem=send_sems.at[w, k], recv_sem=recv_sems.at[w, k], device_id=to, device_id_type=MESH)

    def own(w):
        return pltpu.make_async_copy(ins[w], outs[w].at[myq], local_sems.at[w])

    def send():
        for w in range(len(ins)):
            rows = ins[w].shape[0]
            src = ins[w].at[pl.ds(c * (rows // 2), rows // 2)]
            for j, chip in enumerate(chips):
                copy(w, j, myq, c, (*chip, c), src=src).start()
            if local_sems is not None:
                own(w).start()

    def forward():
        for w in range(len(ins)):
            for j, (qx, qy) in enumerate(chips):
                copy(w, j, 2 * qx + qy, c, sibling).wait_recv()
                copy(w, 3 + j, 2 * qx + qy, c, sibling).start()

    def finish():
        for w in range(len(ins)):
            for j, (qx, qy) in enumerate(chips):
                copy(w, 3 + j, 2 * qx + qy, 1 - c, sibling).wait_recv()
        for w in range(len(ins)):
            for k, (qx, qy) in enumerate(chips + chips):
                copy(w, k, 2 * qx + qy, c, sibling).wait_send()
            if local_sems is not None:
                own(w).wait()

    return send, forward, finish


def _mixer_fwd(x, mod, mods, norm1_g, w_in4, b_in, ln_g, ln_b, wm, bst, conv_w, conv_b, gn_g, gn_b, ga, gb, w_out,
               ffn_shards):
    s = x.shape[0]
    ts = min(FWD_ROW_TILE, s)
    nt = s // ts
    n_sh = len(ffn_shards)

    def body(x_ref, mod_ref, modb_ref, g1_ref, w4_ref, bin_ref, lng_ref, lnb_ref, wm_ref, bst_ref, cw_ref, cb_ref,
             gng_ref, gnb_ref, ga_ref, gb_ref, wout_ref, *rest):
        shard_refs, rest = rest[:n_sh], rest[n_sh:]
        z_ref, x1_ref, yb1_ref, y_ref = rest[:4]
        full_refs, rest = rest[4:4 + n_sh], rest[4 + n_sh:]
        e_ref, sh_ref, mix_ref, send_sems, recv_sems, local_sems = rest
        i = pl.program_id(0)
        send, forward, finish = _gather_plan(shard_refs, full_refs, send_sems, recv_sems, local_sems)

        @pl.when(i == 0)
        def _():
            send()
            e_ref[0:HALO, :] = jnp.zeros((HALO, DB), F32)

        @pl.when(i == (3 * nt) // 4)
        def _():
            forward()

        shift1, scale1, gate1 = _mod_rows(mod_ref, modb_ref, 0, 3)
        x_t = x_ref[...]
        r1 = lax.rsqrt(_rowsum(x_t * x_t) * (1.0 / D) + EPS)
        h = (x_t * r1 * g1_ref[...]) * (1.0 + scale1) + shift1
        hb = h.astype(MM_DTYPE)
        z_parts = []
        for q in range(NQ):
            zq = _dot(hb, w4_ref[q]) + bin_ref[:, q * PW_IN:(q + 1) * PW_IN]
            z_ref[:, q * PW_IN:(q + 1) * PW_IN] = zq
            z_parts.append(zq)
        r = _mixer_recompute(z_parts, lng_ref[...], lnb_ref[...], wm_ref, bst_ref, mix_ref)
        y_a = r["gu"] * r["mixed"]
        e_ref[HALO:HALO + ts, :] = r["yb0"]
        _shifted_copies(e_ref, sh_ref)
        _conv_taps(e_ref, sh_ref, cw_ref, yb1_ref, ts, HALO - (CONV_W - 1), False, bias_ref=cb_ref)
        e_ref[0:HALO, :] = e_ref[ts:ts + HALO, :]
        t = _conv_branch_tail(yb1_ref[...], gng_ref[...], gnb_ref[...])
        ra = lax.rsqrt(_rowsum(y_a * y_a) * (1.0 / DA) + EPS)
        rb = lax.rsqrt(_rowsum(t["y_b"] * t["y_b"]) * (1.0 / DB) + EPS)
        yan = (y_a * ra * ga_ref[...]).astype(MM_DTYPE)
        ybn = (t["y_b"] * rb * gb_ref[...]).astype(MM_DTYPE)
        y_ref[:, 0:DA] = yan
        y_ref[:, DA:D] = ybn
        o1 = _dot(yan, wout_ref[0:DA, :]) + _dot(ybn, wout_ref[DA:D, :])
        x1_ref[...] = x_t + gate1 * o1

        @pl.when(i == nt - 1)
        def _():
            finish()

    row = lambda w: pl.BlockSpec((ts, w), lambda i: (i, 0))
    full = lambda a: pl.BlockSpec(a.shape, lambda i: (0,) * a.ndim)
    return pl.pallas_call(
        body, name="mixer_fwd", grid=(nt,),
        in_specs=[row(D), full(mod), full(mods), full(norm1_g), _whole_vmem(), full(b_in), full(ln_g), full(ln_b),
                  _whole_vmem(), full(bst), full(conv_w), full(conv_b), full(gn_g), full(gn_b), full(ga), full(gb),
                  _whole_vmem()] + [_any()] * n_sh,
        out_specs=[row(4 * PW_IN), row(D), row(DB), row(D)] + [_any()] * n_sh,
        out_shape=[SDS((s, 4 * PW_IN), F32), SDS((s, D), F32), SDS((s, DB), F32), SDS((s, D), MM_DTYPE)]
        + [SDS((NQ,) + w.shape, w.dtype) for w in ffn_shards],
        scratch_shapes=[pltpu.VMEM((ts + HALO, DB), F32), pltpu.VMEM((7, ts + HALO - 8, DB), F32), pltpu.VMEM((ts, DA), F32),
                        pltpu.SemaphoreType.DMA((n_sh, 6)), pltpu.SemaphoreType.DMA((n_sh, 6)),
                        pltpu.SemaphoreType.DMA((n_sh,))],
        compiler_params=_params("arbitrary"),
    )(x, mod, mods, norm1_g, w_in4, b_in, ln_g, ln_b, wm, bst, conv_w, conv_b, gn_g, gn_b, ga, gb, w_out, *ffn_shards)


def _ffn_fwd(x1, target, mod, mods, norm2_g, norm_f_g, w_ffn_in4, w_ffn_out):
    s = x1.shape[0]
    sub_rows = min(FFN_ROW_TILE, s)
    ts = min(2 * sub_rows, s)
    nt = s // ts

    def body(x1_ref, tgt_ref, mod_ref, modb_ref, g2_ref, gf_ref, wf_ref, wo_ref,
             g_ref, up_ref, a_ref, h2_ref, dx2_ref, acc_ref):
        i = pl.program_id(0)

        @pl.when(i == 0)
        def _():
            acc_ref[...] = jnp.zeros(acc_ref.shape, F32)

        shift2, scale2, gate2, shift_f, scale_f = _mod_rows(mod_ref, modb_ref, 3, 5)
        for sub in range(ts // sub_rows):
            rows = slice(sub * sub_rows, (sub + 1) * sub_rows)
            x1_t = x1_ref[rows, :]
            r2 = lax.rsqrt(_rowsum(x1_t * x1_t) * (1.0 / D) + EPS)
            h2 = (x1_t * r2 * g2_ref[...]) * (1.0 + scale2) + shift2
            h2b = h2.astype(MM_DTYPE)
            h2_ref[rows, :] = h2b
            o2 = jnp.zeros((sub_rows, D), F32)
            for p in range(2):
                g = _dot(h2b, wf_ref[p])
                up = _dot(h2b, wf_ref[2 + p])
                g_ref[rows, p * PW_FF:(p + 1) * PW_FF] = g.astype(MM_DTYPE)
                up_ref[rows, p * PW_FF:(p + 1) * PW_FF] = up.astype(MM_DTYPE)
                a = (g * _sigmoid(g) * up).astype(MM_DTYPE)
                a_ref[rows, p * PW_FF:(p + 1) * PW_FF] = a
                o2 = o2 + _dot(a, wo_ref[p * PW_FF:(p + 1) * PW_FF, :])
            x2 = x1_t + gate2 * o2
            rf = lax.rsqrt(_rowsum(x2 * x2) * (1.0 / D) + EPS)
            gf = gf_ref[...]
            nf = x2 * rf * gf
            err = nf * (1.0 + scale_f) + shift_f - tgt_ref[rows, :]
            d_out = err * (1.0 / D)
            d_nf = d_out * (1.0 + scale_f)
            t = d_nf * gf
            dx2_ref[rows, :] = rf * t - x2 * (rf * rf * rf) * (_rowsum(t * x2) * (1.0 / D))
            acc_ref[0:1, :] += _colsum(d_out)
            acc_ref[1:2, :] += _colsum(d_out * nf)
            acc_ref[2:3, :] += _colsum(d_nf * x2 * rf)
            acc_ref[3:4, :] += _colsum(err * err)

        @pl.when(i == nt - 1)
        def _():
            acc_ref[4:5, :] = jnp.zeros((1, D), F32) + _rowsum(acc_ref[3:4, :]) * (0.5 / D)

    row = lambda w: pl.BlockSpec((ts, w), lambda i: (i, 0))
    full = lambda a: pl.BlockSpec(a.shape, lambda i: (0,) * a.ndim)
    return pl.pallas_call(
        body, name="ffn_fwd", grid=(nt,),
        in_specs=[row(D), row(D), full(mod), full(mods), full(norm2_g), full(norm_f_g), _whole_vmem(), _whole_vmem()],
        out_specs=[row(DFF), row(DFF), row(DFF), row(D), row(D), pl.BlockSpec((8, D), lambda i: (0, 0))],
        out_shape=[SDS((s, DFF), MM_DTYPE), SDS((s, DFF), MM_DTYPE), SDS((s, DFF), MM_DTYPE), SDS((s, D), MM_DTYPE),
                   SDS((s, D), F32), SDS((8, D), F32)],
        compiler_params=_params("arbitrary"),
    )(x1, target, mod, mods, norm2_g, norm_f_g, w_ffn_in4, w_ffn_out)


def _ffn_bwd(dx2, x1, g, up, mod, mods, norm2_g, w_ffn_in4, w_ffn_out):
    s = x1.shape[0]
    sub_rows = min(FFN_ROW_TILE, s)
    ts = min(2 * sub_rows, s)
    nt = s // ts

    def body(dx2_ref, x1_ref, g_ref, up_ref, mod_ref, modb_ref, g2_ref, wf_ref, wo_ref,
             dff_ref, dx1_ref, acc_ref):
        @pl.when(pl.program_id(0) == 0)
        def _():
            acc_ref[...] = jnp.zeros(acc_ref.shape, F32)

        shift2, scale2, gate2 = _mod_rows(mod_ref, modb_ref, 3, 3)
        for sub in range(ts // sub_rows):
            rows = slice(sub * sub_rows, (sub + 1) * sub_rows)
            dx2_t = dx2_ref[rows, :]
            do2 = (dx2_t * gate2).astype(MM_DTYPE)
            dh2 = jnp.zeros((sub_rows, D), F32)
            for p in range(2):
                da = _dot_nt(do2, wo_ref[p * PW_FF:(p + 1) * PW_FF, :])
                gp = g_ref[rows, p * PW_FF:(p + 1) * PW_FF].astype(F32)
                upp = up_ref[rows, p * PW_FF:(p + 1) * PW_FF].astype(F32)
                sg = _sigmoid(gp)
                silu = gp * sg
                dg = (da * upp * (sg * (1.0 + gp * (1.0 - sg)))).astype(MM_DTYPE)
                dup = (da * silu).astype(MM_DTYPE)
                dff_ref[rows, p * PW_FF:(p + 1) * PW_FF] = dg
                dff_ref[rows, DFF + p * PW_FF:DFF + (p + 1) * PW_FF] = dup
                dh2 = dh2 + _dot_nt(dg, wf_ref[p]) + _dot_nt(dup, wf_ref[2 + p])
            x1_t = x1_ref[rows, :]
            r2 = lax.rsqrt(_rowsum(x1_t * x1_t) * (1.0 / D) + EPS)
            g2 = g2_ref[...]
            xr = x1_t * r2
            dn2 = dh2 * (1.0 + scale2)
            t = dn2 * g2
            dx1_ref[rows, :] = dx2_t + r2 * t - x1_t * (r2 * r2 * r2) * (_rowsum(t * x1_t) * (1.0 / D))
            acc_ref[0:1, :] += _colsum(dh2)
            acc_ref[1:2, :] += _colsum(dh2 * (xr * g2))
            acc_ref[2:3, :] += _colsum(dn2 * xr)

    row = lambda w: pl.BlockSpec((ts, w), lambda i: (i, 0))
    full = lambda a: pl.BlockSpec(a.shape, lambda i: (0,) * a.ndim)
    return pl.pallas_call(
        body, name="ffn_bwd", grid=(nt,),
        in_specs=[row(D), row(D), row(DFF), row(DFF), full(mod), full(mods), full(norm2_g), _whole_vmem(), _whole_vmem()],
        out_specs=[row(2 * DFF), row(D), pl.BlockSpec((8, D), lambda i: (0, 0))],
        out_shape=[SDS((s, 2 * DFF), MM_DTYPE), SDS((s, D), F32), SDS((8, D), F32)],
        compiler_params=_params("arbitrary"),
    )(dx2, x1, g, up, mod, mods, norm2_g, w_ffn_in4, w_ffn_out)


def _scatter_plan(ins, outs, send_sems, recv_sems, local_sems=None):
    x, y, c = _place()
    me = 4 * x + 2 * y + c

    def copies():
        cps = []
        for w in range(len(ins)):
            h = ins[w].shape[1] // 2
            for k in range(1, 8):
                px, py, pc = (1 - x if k & 4 else x), (1 - y if k & 2 else y), (1 - c if k & 1 else c)
                cps.append(pltpu.make_async_remote_copy(
                    src_ref=ins[w].at[2 * px + py, pl.ds(pc * h, h)], dst_ref=outs[w].at[me],
                    send_sem=send_sems.at[w, k - 1], recv_sem=recv_sems.at[w, k - 1],
                    device_id=(px, py, pc), device_id_type=MESH))
        return cps

    def own():
        if local_sems is None:
            return []
        return [pltpu.make_async_copy(ins[w].at[2 * x + y, pl.ds(c * (ins[w].shape[1] // 2), ins[w].shape[1] // 2)],
                                      outs[w].at[me], local_sems.at[w]) for w in range(len(ins))]

    def send():
        for cp in copies() + own():
            cp.start()

    def finish():
        for cp in copies() + own():
            cp.wait()

    return send, finish


def _scattered_shape(partial):
    nq, rows, cols = partial.shape
    return SDS((8, rows // 2, cols), partial.dtype)


def _mixer_bwd(dx1, x, z, yb1, mod, mods, norm1_g, w_in4, ln_g, ln_b, wm, wmt, bst, conv_w, gn_g, gn_b, ga, gb, w_out,
               partials):
    s = x.shape[0]
    ts = min(ROW_TILE, s)
    nt = s // ts
    n_cs = len(partials)

    def body(dx1_ref, x_ref, z_ref, yb1_ref, mod_ref, modb_ref, g1_ref, w4_ref, lng_ref, lnb_ref, wm_ref, wmt_ref,
             bst_ref, cw_ref, gng_ref, gnb_ref, ga_ref, gb_ref, wout_ref, *rest):
        cs_refs, rest = rest[:n_cs], rest[n_cs:]
        gx_ref, dz_ref, h_ref, a1_ref, a2_ref, a5_ref, acw_ref, asw_ref, asb_ref = rest[:9]
        arrived_refs, rest = rest[9:9 + n_cs], rest[9 + n_cs:]
        e_ref, sh_ref, mix_ref, dvl_ref, send_sems, recv_sems, local_sems = rest
        i = pl.program_id(0)
        send, finish = _scatter_plan(cs_refs, arrived_refs, send_sems, recv_sems, local_sems)

        @pl.when(i == 0)
        def _():
            send()
            e_ref[ts:ts + HALO, :] = jnp.zeros((HALO, DB), F32)
            for r in (a1_ref, a2_ref, a5_ref, acw_ref, asw_ref, asb_ref):
                r[...] = jnp.zeros(r.shape, F32)

        shift1, scale1, gate1 = _mod_rows(mod_ref, modb_ref, 0, 3)
        dx1_t = dx1_ref[...]
        do1 = (dx1_t * gate1).astype(MM_DTYPE)
        d_yan = _dot_nt(do1, wout_ref[0:DA, :])
        d_ybn = _dot_nt(do1, wout_ref[DA:D, :])

        z_parts = [z_ref[:, q * PW_IN:(q + 1) * PW_IN] for q in range(NQ)]
        u, v, val, gate = z_parts
        lng = lng_ref[...]
        r = _mixer_recompute(z_parts, lng, lnb_ref[...], wm_ref, bst_ref, mix_ref)
        gng = gng_ref[...]
        t = _conv_branch_tail(yb1_ref[...], gng, gnb_ref[...])
        y_a = r["gu"] * r["mixed"]
        y_b = t["y_b"]
        ga_v, gb_v = ga_ref[...], gb_ref[...]
        ra = lax.rsqrt(_rowsum(y_a * y_a) * (1.0 / DA) + EPS)
        rb = lax.rsqrt(_rowsum(y_b * y_b) * (1.0 / DB) + EPS)

        a5_ref[0:1, :] += _colsum(d_yan * y_a * ra)
        a5_ref[1:2, :] += _colsum(d_ybn * y_b * rb)
        ta = d_yan * ga_v
        d_ya = ra * ta - y_a * (ra * ra * ra) * (_rowsum(ta * y_a) * (1.0 / DA))
        tb = d_ybn * gb_v
        d_yb = rb * tb - y_b * (rb * rb * rb) * (_rowsum(tb * y_b) * (1.0 / DB))

        d_u = d_ya * r["mixed"] * r["dgu"]
        d_mixed = d_ya * r["gu"]
        dmb = d_mixed.astype(MM_DTYPE)
        lo_mask = lax.broadcasted_iota(jnp.int32, (CHUNK, LANES), 1) < HALF
        zero_blk = jnp.zeros((CHUNK, LANES), MM_DTYPE)
        sb_acc = jnp.zeros((CHUNK, DA), F32)
        for ck in range(ts // CHUNK):
            rows = slice(ck * CHUNK, (ck + 1) * CHUNK)
            sb_acc = sb_acc + d_mixed[rows, :]
            for jb in range(DA // LANES):
                cols = slice(jb * LANES, (jb + 1) * LANES)
                dm_blk = dmb[rows, cols]
                vl_blk = r["vlb"][rows, cols]
                da_ = _dot(wmt_ref[2 * jb], dm_blk)
                db_ = _dot(wmt_ref[2 * jb + 1], dm_blk)
                dvl_ref[rows, cols] = jnp.where(lo_mask, da_, db_)
                asw_ref[2 * jb] += _dot_nt(jnp.where(lo_mask, dm_blk, zero_blk), vl_blk)
                asw_ref[2 * jb + 1] += _dot_nt(jnp.where(lo_mask, zero_blk, dm_blk), vl_blk)
        asb_ref[...] += sb_acc
        d_vl = dvl_ref[...]
        a5_ref[2:3, :] += _colsum(d_vl * r["vhat"])
        a5_ref[3:4, :] += _colsum(d_vl)
        dvh = d_vl * lng
        d_gv = r["rs"] * (dvh - _rowsum(dvh) * (1.0 / DA) - r["vhat"] * (_rowsum(dvh * r["vhat"]) * (1.0 / DA)))
        d_v = d_gv * r["dgv"]

        yb2, s2 = t["yb2"], t["s2"]
        d_yb2 = d_yb * (s2 * (1.0 + yb2 * (1.0 - s2)))
        a5_ref[4:5, :] += _colsum(d_yb2 * t["ghat"])
        a5_ref[5:6, :] += _colsum(d_yb2)
        dgh = d_yb2 * gng
        d_yb1 = t["grs"] * (dgh - _group_sum(dgh) * (1.0 / HALF) - t["ghat"] * (_group_sum(dgh * t["ghat"]) * (1.0 / HALF)))
        a5_ref[6:7, :] += _colsum(d_yb1)
        e_ref[0:ts, :] = d_yb1
        _shifted_copies(e_ref, sh_ref)
        mix_ref[...] = r["yb0"]
        _conv_taps(e_ref, sh_ref, cw_ref, dvl_ref, ts, 0, True, other_ref=mix_ref, tap_acc_ref=acw_ref)
        d_yb0 = dvl_ref[...]
        e_ref[ts:ts + HALO, :] = e_ref[0:HALO, :]
        sg = r["sg"]
        d_val = d_yb0 * sg
        d_gate = d_yb0 * val * sg * (1.0 - sg)

        dh = jnp.zeros((ts, D), F32)
        for q, dzq in enumerate((d_u, d_v, d_val, d_gate)):
            a2_ref[0:1, q * PW_IN:(q + 1) * PW_IN] += _colsum(dzq)
            dzb = dzq.astype(MM_DTYPE)
            dz_ref[:, q * PW_IN:(q + 1) * PW_IN] = dzb
            dh = dh + _dot_nt(dzb, w4_ref[q])
        x_t = x_ref[...]
        r1 = lax.rsqrt(_rowsum(x_t * x_t) * (1.0 / D) + EPS)
        g1 = g1_ref[...]
        xr = x_t * r1
        n1 = xr * g1
        h_ref[...] = (n1 * (1.0 + scale1) + shift1).astype(MM_DTYPE)
        dn1 = dh * (1.0 + scale1)
        t1 = dn1 * g1
        gx_ref[...] = dx1_t + r1 * t1 - x_t * (r1 * r1 * r1) * (_rowsum(t1 * x_t) * (1.0 / D))
        a1_ref[0:1, :] += _colsum(dh)
        a1_ref[1:2, :] += _colsum(dh * n1)
        a1_ref[2:3, :] += _colsum(dn1 * xr)

        @pl.when(i == nt - 1)
        def _():
            asb_ref[...] = _group_sum(asb_ref[...])
            finish()

    row = lambda w: pl.BlockSpec((ts, w), lambda i: (nt - 1 - i, 0))
    full = lambda a: pl.BlockSpec(a.shape, lambda i: (0,) * a.ndim)
    keep = lambda shape: pl.BlockSpec(shape, lambda i: (0,) * len(shape))
    return pl.pallas_call(
        body, name="mixer_bwd", grid=(nt,),
        in_specs=[row(D), row(D), row(4 * PW_IN), row(DB), full(mod), full(mods), full(norm1_g), _whole_vmem(),
                  full(ln_g), full(ln_b), _whole_vmem(), _whole_vmem(), full(bst), full(conv_w), full(gn_g), full(gn_b),
                  full(ga), full(gb), _whole_vmem()] + [_any()] * n_cs,
        out_specs=[row(D), row(4 * PW_IN), row(D), keep((8, D)), keep((8, 4 * PW_IN)), keep((8, DA)),
                   keep((HALO, DB)), keep((N_HEADS, CHUNK, CHUNK)), keep((CHUNK, DA))] + [_any()] * n_cs,
        out_shape=[SDS((s, D), F32), SDS((s, 4 * PW_IN), MM_DTYPE), SDS((s, D), MM_DTYPE), SDS((8, D), F32),
                   SDS((8, 4 * PW_IN), F32), SDS((8, DA), F32), SDS((HALO, DB), F32),
                   SDS((N_HEADS, CHUNK, CHUNK), F32), SDS((CHUNK, DA), F32)]
        + [_scattered_shape(p) for p in partials],
        scratch_shapes=[pltpu.VMEM((ts + HALO, DB), F32), pltpu.VMEM((7, ts + HALO - 8, DB), F32),
                        pltpu.VMEM((ts, DA), F32), pltpu.VMEM((ts, DA), F32),
                        pltpu.SemaphoreType.DMA((n_cs, 7)), pltpu.SemaphoreType.DMA((n_cs, 7)),
                        pltpu.SemaphoreType.DMA((n_cs,))],
        compiler_params=_params("arbitrary"),
    )(dx1, x, z, yb1, mod, mods, norm1_g, w_in4, ln_g, ln_b, wm, wmt, bst, conv_w, gn_g, gn_b, ga, gb, w_out, *partials)


def _gather8_plan(x_ref, out_ref, send_sems, recv_sems, local_sem):
    x, y, c = _place()
    me, sibling = (x, y, c), (x, y, 1 - c)
    chips = _other_chips(x, y)

    def copy(k, block, to, src=None):
        dst = out_ref.at[4 * block[0] + 2 * block[1] + block[2]]
        return pltpu.make_async_remote_copy(src_ref=dst if src is None else src, dst_ref=dst, send_sem=send_sems.at[k],
                                            recv_sem=recv_sems.at[k], device_id=to, device_id_type=MESH)

    def own():
        return pltpu.make_async_copy(x_ref, out_ref.at[4 * x + 2 * y + c], local_sem)

    def send():
        own().start()
        copy(0, me, sibling, src=x_ref).start()
        for j, chip in enumerate(chips):
            copy(1 + j, me, (*chip, c), src=x_ref).start()

    def forward():
        for j, chip in enumerate(chips):
            copy(1 + j, (*chip, c), me).wait_recv()
            copy(4 + j, (*chip, c), sibling).start()

    def finish():
        copy(0, sibling, me).wait_recv()
        for j, chip in enumerate(chips):
            copy(4 + j, (*chip, 1 - c), me).wait_recv()
        for k in range(7):
            copy(k, me, sibling).wait_send()
        own().wait()

    return send, forward, finish


def _grad_matmul(name, a, b, ka_tile, nb_tile, piece_w=None, gated=None, gather_blk=None):
    s, ka = a.shape
    nb = b.shape[1]
    ts = min(GRAD_ROW_TILE, s)
    nt = s // ts
    nja, njb = ka // ka_tile, nb // nb_tile
    steps = nja * njb * nt
    n_in = 2 + (2 if gated else 0) + (1 if gather_blk is not None else 0)
    n_out = 1 + (1 if gated else 0) + (1 if gather_blk is not None else 0)
    assert not (gated and njb != 1) and not (piece_w and nja != 1)

    def body(*refs):
        ins, outs, scratch = refs[:n_in], refs[n_in:n_in + n_out], refs[n_in + n_out:]
        a_ref, b_ref, o_ref, acc_ref = ins[0], ins[1], outs[0], scratch[0]
        ins = ins[2:]
        step = (pl.program_id(0) * njb + pl.program_id(1)) * nt + pl.program_id(2)
        if gather_blk is not None:
            send, forward, finish = _gather8_plan(ins[-1], outs[-1], *scratch[1:])

            @pl.when(step == 0)
            def _():
                send()

            @pl.when(step == (3 * steps) // 4)
            def _():
                forward()

        prod = _dot_tn(a_ref[...].astype(MM_DTYPE), b_ref[...].astype(MM_DTYPE))

        @pl.when(pl.program_id(2) == 0)
        def _():
            acc_ref[...] = prod

        @pl.when(pl.program_id(2) > 0)
        def _():
            acc_ref[...] += prod

        @pl.when(pl.program_id(2) == nt - 1)
        def _():
            gm = acc_ref[...]
            if gated:
                gate_ref, w_ref, dg_ref = ins[0], ins[1], outs[1]

                @pl.when(step == nt - 1)
                def _():
                    dg_ref[...] = jnp.zeros(dg_ref.shape, F32)

                dg_ref[0:1, :] += _colsum(gm * w_ref[...].astype(F32))
                gm = gm * gate_ref[...]
            if piece_w:
                for q in range(nb_tile // piece_w):
                    o_ref[q] = gm[:, q * piece_w:(q + 1) * piece_w].astype(WIRE_DTYPE)
            else:
                o_ref[...] = gm.astype(WIRE_DTYPE)

        if gather_blk is not None:
            @pl.when(step == steps - 1)
            def _():
                finish()

    in_specs = [pl.BlockSpec((ts, ka_tile), lambda ja, jb, i: (i, ja)),
                pl.BlockSpec((ts, nb_tile), lambda ja, jb, i: (i, jb))]
    operands = [a, b]
    if piece_w:
        out_shape = [SDS((nb // piece_w, ka, piece_w), WIRE_DTYPE)]
        out_specs = [pl.BlockSpec((nb_tile // piece_w, ka, piece_w), lambda ja, jb, i: (jb, 0, 0))]
    else:
        out_shape = [SDS((ka, nb), WIRE_DTYPE)]
        out_specs = [pl.BlockSpec((ka_tile, nb_tile), lambda ja, jb, i: (ja, jb))]
    scratch = [pltpu.VMEM((ka_tile, nb_tile), F32)]
    if gated:
        in_specs += [pl.BlockSpec((1, nb_tile), lambda ja, jb, i: (0, jb)),
                     pl.BlockSpec((ka_tile, nb_tile), lambda ja, jb, i: (ja, jb))]
        operands += list(gated)
        out_shape.append(SDS((8, nb), F32))
        out_specs.append(pl.BlockSpec((8, nb_tile), lambda ja, jb, i: (0, jb)))
    if gather_blk is not None:
        in_specs.append(_any())
        operands.append(gather_blk)
        out_shape.append(SDS((8,) + gather_blk.shape, gather_blk.dtype))
        out_specs.append(_any())
        scratch += [pltpu.SemaphoreType.DMA((7,)), pltpu.SemaphoreType.DMA((7,)), pltpu.SemaphoreType.DMA]
    return pl.pallas_call(
        body, name=name, grid=(nja, njb, nt), in_specs=in_specs, out_specs=out_specs, out_shape=out_shape,
        scratch_shapes=scratch, compiler_params=_params("arbitrary", "arbitrary", "arbitrary"),
    )(*operands)


def _grad_w_in_scattered(h, dz, blk):
    s = h.shape[0]
    ts = min(GRAD_ROW_TILE, s)
    nt = s // ts
    parts = 2 * W_IN_SPLIT
    steps = parts * nt
    half_rows = D // 2
    part_rows = D // parts

    def body(a_ref, b_ref, blk_ref, arr_ref, all_ref, acc_ref, stage_ref, send_sems, recv_sems, g_send, g_recv, g_local):
        ja, i = pl.program_id(0), pl.program_id(1)
        step = ja * nt + i
        x, y, c = _place()
        me = 4 * x + 2 * y + c
        gsend, gforward, gfinish = _gather8_plan(blk_ref, all_ref, g_send, g_recv, g_local)

        @pl.when(step == 0)
        def _():
            gsend()

        @pl.when(step == (3 * steps) // 4)
        def _():
            gforward()

        prod = _dot_tn(a_ref[...], b_ref[...])

        @pl.when(i == 0)
        def _():
            acc_ref[...] = prod

        @pl.when(i > 0)
        def _():
            acc_ref[...] += prod

        def copies(part):
            half, r = divmod(part, W_IN_SPLIT)
            out = []
            for q in range(NQ):
                src, dst = stage_ref.at[part, q], arr_ref.at[me, pl.ds(r * part_rows, part_rows)]
                remote = pltpu.make_async_remote_copy(src_ref=src, dst_ref=dst, send_sem=send_sems.at[part, q],
                                                      recv_sem=recv_sems.at[me, r], device_id=(q // 2, q % 2, half),
                                                      device_id_type=MESH)
                local = pltpu.make_async_copy(src, dst, send_sems.at[part, q])
                out.append((remote, local, (2 * x + y == q) & (c == half)))
            return out

        for part in range(parts):
            @pl.when((ja == part) & (i == nt - 1))
            def _():
                gm = acc_ref[...]
                for q in range(NQ):
                    stage_ref[part, q] = gm[:, q * PW_IN:(q + 1) * PW_IN].astype(WIRE_DTYPE)
                for remote, local, is_self in copies(part):
                    @pl.when(is_self)
                    def _():
                        local.start()

                    @pl.when(jnp.logical_not(is_self))
                    def _():
                        remote.start()

        @pl.when(step == steps - 1)
        def _():
            gfinish()
            for d in range(8):
                for r in range(W_IN_SPLIT):
                    @pl.when(me != d)
                    def _():
                        dst = arr_ref.at[d, pl.ds(r * part_rows, part_rows)]
                        pltpu.make_async_remote_copy(src_ref=stage_ref.at[0, 0], dst_ref=dst, send_sem=send_sems.at[0, 0],
                                                     recv_sem=recv_sems.at[d, r], device_id=(x, y, c),
                                                     device_id_type=MESH).wait_recv()
            for part in range(parts):
                for remote, local, is_self in copies(part):
                    @pl.when(is_self)
                    def _():
                        local.wait()

                    @pl.when(jnp.logical_not(is_self))
                    def _():
                        remote.wait_send()

    return pl.pallas_call(
        body, name="grad_w_in", grid=(parts, nt),
        in_specs=[pl.BlockSpec((ts, part_rows), lambda ja, i: (i, ja)), pl.BlockSpec((ts, NQ * PW_IN), lambda ja, i: (i, 0)),
                  _any()],
        out_specs=[_any(), _any()],
        out_shape=[SDS((8, half_rows, PW_IN), WIRE_DTYPE), SDS((8,) + blk.shape, blk.dtype)],
        scratch_shapes=[pltpu.VMEM((part_rows, NQ * PW_IN), F32), pltpu.VMEM((parts, NQ, part_rows, PW_IN), WIRE_DTYPE),
                        pltpu.SemaphoreType.DMA((parts, NQ)), pltpu.SemaphoreType.DMA((8, W_IN_SPLIT)),
                        pltpu.SemaphoreType.DMA((7,)), pltpu.SemaphoreType.DMA((7,)), pltpu.SemaphoreType.DMA],
        compiler_params=_params("arbitrary", "arbitrary"),
    )(h, dz, blk)


COND_COLS = 512


def _cond_partial(c_all, w_a, w_f):
    na = w_a.shape[1]

    def body(c_ref, wa_ref, wf_ref, oa_ref, of_ref):
        c_t = c_ref[...]
        ca = (c_t * _sigmoid(c_t)).astype(MM_DTYPE)
        oa_ref[...] = _dot(ca, wa_ref[...].astype(MM_DTYPE))

        @pl.when(pl.program_id(0) == 0)
        def _():
            of_ref[...] = _dot(ca, wf_ref[...].astype(MM_DTYPE))

    keep = lambda shape: pl.BlockSpec(shape, lambda j: (0, 0))
    return pl.pallas_call(
        body, name="cond_partial", grid=(na // COND_COLS,),
        in_specs=[keep((8, D)), pl.BlockSpec((D, COND_COLS), lambda j: (0, j)), keep(w_f.shape)],
        out_specs=[pl.BlockSpec((8, COND_COLS), lambda j: (0, j)), keep((8, w_f.shape[1]))],
        out_shape=[SDS((8, na), F32), SDS((8, w_f.shape[1]), F32)],
        compiler_params=_params("arbitrary"),
    )(c_all, w_a, w_f)


def _cond_grad(c_all, dmod_a, dmod_f):
    na = dmod_a.shape[1]

    def body(c_ref, da_ref, df_ref, oa_ref, of_ref):
        c_t = c_ref[...]
        ca = jnp.concatenate([c_t * _sigmoid(c_t), jnp.zeros((8, D), F32)], axis=0).astype(MM_DTYPE)

        def outer(d_ref):
            dm = jnp.concatenate([d_ref[...], jnp.zeros(d_ref.shape, F32)], axis=0).astype(MM_DTYPE)
            return _dot_tn(ca, dm)

        oa_ref[...] = outer(da_ref)

        @pl.when(pl.program_id(0) == 0)
        def _():
            of_ref[...] = outer(df_ref)

    keep = lambda shape: pl.BlockSpec(shape, lambda j: (0, 0))
    return pl.pallas_call(
        body, name="cond_grad", grid=(na // COND_COLS,),
        in_specs=[keep((8, D)), pl.BlockSpec((8, COND_COLS), lambda j: (0, j)), keep(dmod_f.shape)],
        out_specs=[pl.BlockSpec((D, COND_COLS), lambda j: (0, j)), keep((D, dmod_f.shape[1]))],
        out_shape=[SDS((D, na), F32), SDS((D, dmod_f.shape[1]), F32)],
        compiler_params=_params("arbitrary"),
    )(c_all, dmod_a, dmod_f)


def _row_tile(rows, cap=256):
    if rows <= cap:
        return rows
    for t in range(cap, 7, -8):
        if rows % t == 0:
            return t
    return rows


def _ordered_sum(name, parts, into_half=None):
    n, rows, cols = parts.shape
    rt = _row_tile(rows)
    nb = rows // rt

    def body(*refs):
        p_ref, o_ref = refs[-2:]
        acc = p_ref[0].astype(F32)
        for k in range(1, n):
            acc = acc + p_ref[k].astype(F32)
        o_ref[...] = acc

    if into_half is None:
        return pl.pallas_call(
            body, name=name, grid=(nb,),
            in_specs=[pl.BlockSpec((n, rt, cols), lambda i: (0, i, 0))],
            out_specs=pl.BlockSpec((rt, cols), lambda i: (i, 0)), out_shape=SDS((rows, cols), F32),
            compiler_params=_params("parallel"),
        )(parts)
    grid_spec = pltpu.PrefetchScalarGridSpec(
        num_scalar_prefetch=1, grid=(nb,),
        in_specs=[pl.BlockSpec((n, rt, cols), lambda i, c_ref: (0, i, 0))],
        out_specs=pl.BlockSpec((rt, cols), lambda i, c_ref: (c_ref[0] * nb + i, 0)))
    return pl.pallas_call(
        body, name=name, grid_spec=grid_spec, out_shape=SDS((2 * rows, cols), F32),
        compiler_params=_params("parallel"),
    )(into_half.astype(jnp.int32).reshape(1), parts)


def _adamw_update(w_ref, g_ref, m_ref, v_ref, d_ref, nm_ref, nv_ref):
    c1 = 1.0 - ADAM_B1 ** ADAM_STEP
    c2 = 1.0 - ADAM_B2 ** ADAM_STEP
    g_t = g_ref[...]
    m_new = ADAM_B1 * m_ref[...] + (1.0 - ADAM_B1) * g_t
    v_new = ADAM_B2 * v_ref[...] + (1.0 - ADAM_B2) * (g_t * g_t)
    nm_ref[...] = m_new
    nv_ref[...] = v_new
    d_ref[...] = -ADAM_LR * ((m_new / c1) / (jnp.sqrt(v_new / c2) + ADAM_EPS) + ADAM_WD * w_ref[...])


def _adamw_many(name, ws, gs, ms, vs):
    n = len(ws)

    def body(*refs):
        ins, outs = refs[:4 * n], refs[4 * n:]
        for k in range(n):
            _adamw_update(ins[k], ins[n + k], ins[2 * n + k], ins[3 * n + k], *outs[3 * k:3 * k + 3])

    return pl.pallas_call(
        body, name=name, out_shape=[SDS(w.shape, F32) for w in ws for _ in range(3)],
        compiler_params=pltpu.CompilerParams(vmem_limit_bytes=VMEM_LIMIT),
    )(*ws, *gs, *ms, *vs)


def _adamw_tiled(name, ws, gs, ms, vs):
    n = len(ws)

    def body(*refs):
        ins, outs = refs[:4 * n], refs[4 * n:]
        for k in range(n):
            _adamw_update(ins[k], ins[n + k], ins[2 * n + k], ins[3 * n + k], *outs[3 * k:3 * k + 3])

    specs = [pl.BlockSpec((w.shape[0] // ADAMW_STEPS, w.shape[1]), lambda i: (i, 0)) for w in ws]
    return pl.pallas_call(
        body, name=name, grid=(ADAMW_STEPS,), in_specs=specs * 4, out_specs=[s for s in specs for _ in range(3)],
        out_shape=[SDS(w.shape, F32) for w in ws for _ in range(3)], compiler_params=_params("parallel"),
    )(*ws, *gs, *ms, *vs)


def _place():
    return lax.axis_index("x"), lax.axis_index("y"), lax.axis_index("c")


def _other_chips(x, y):
    return [(1 - x, y), (x, 1 - y), (1 - x, 1 - y)]


def _all_gather8(name, blk):
    m, n = blk.shape

    def body(x_ref, out_ref, send_sems, recv_sems, local_sem):
        x, y, c = _place()
        me, sibling = (x, y, c), (x, y, 1 - c)
        chips = _other_chips(x, y)

        def slot(px, py, pc):
            return out_ref.at[4 * px + 2 * py + pc]

        def copy(k, block, to, src=None):
            return pltpu.make_async_remote_copy(
                src_ref=slot(*block) if src is None else src, dst_ref=slot(*block),
                send_sem=send_sems.at[k], recv_sem=recv_sems.at[k], device_id=to, device_id_type=MESH)

        mine = pltpu.make_async_copy(x_ref, slot(*me), local_sem)
        mine.start()
        first = [copy(0, me, sibling, src=x_ref)]
        first += [copy(1 + j, me, (*chip, c), src=x_ref) for j, chip in enumerate(chips)]
        for cp in first:
            cp.start()
        passed = [copy(4 + j, (*chip, c), sibling) for j, chip in enumerate(chips)]
        for j, chip in enumerate(chips):
            copy(1 + j, (*chip, c), me).wait_recv()
            passed[j].start()
        copy(0, sibling, me).wait_recv()
        for j, chip in enumerate(chips):
            copy(4 + j, (*chip, 1 - c), me).wait_recv()
        for cp in first + passed:
            cp.wait_send()
        mine.wait()

    return pl.pallas_call(
        body, name=name, out_shape=SDS((8, m, n), blk.dtype),
        in_specs=[_whole_vmem()], out_specs=_whole_vmem(),
        scratch_shapes=[pltpu.SemaphoreType.DMA((7,)), pltpu.SemaphoreType.DMA((7,)), pltpu.SemaphoreType.DMA],
        compiler_params=pltpu.CompilerParams(vmem_limit_bytes=VMEM_LIMIT),
    )(blk)


def _any():
    return pl.BlockSpec(memory_space=pl.ANY)


def _gather_weights(shards, blk):
    n = len(shards)

    def body(*refs):
        ins, outs, sems = refs[:n + 1], refs[n + 1:2 * n + 2], refs[2 * n + 2:]
        send, forward, finish = _gather_plan(ins[:n], outs[:n], sems[0], sems[1])
        send8, forward8, finish8 = _gather8_plan(ins[n], outs[n], *sems[2:])
        send8()
        send()
        forward8()
        forward()
        finish8()
        finish()

    return pl.pallas_call(
        body, name="gather_weights",
        out_shape=[SDS((NQ,) + s.shape, s.dtype) for s in shards] + [SDS((8,) + blk.shape, blk.dtype)],
        in_specs=[_any()] * (n + 1), out_specs=[_any()] * (n + 1),
        scratch_shapes=[pltpu.SemaphoreType.DMA((n, 6)), pltpu.SemaphoreType.DMA((n, 6)),
                        pltpu.SemaphoreType.DMA((7,)), pltpu.SemaphoreType.DMA((7,)), pltpu.SemaphoreType.DMA],
    )(*shards, blk)


def _own_piece(gathered, shard):
    myq = 2 * lax.axis_index("x") + lax.axis_index("y")
    return lax.dynamic_update_slice(gathered, shard[None], (myq,) + (0,) * shard.ndim)


def _scatter_to_owners(partials):
    n = len(partials)

    def body(*refs):
        send, finish = _scatter_plan(refs[:n], refs[n:2 * n], *refs[2 * n:])
        send()
        finish()

    return pl.pallas_call(
        body, name="scatter_to_owners",
        out_shape=[_scattered_shape(p) for p in partials],
        in_specs=[_any()] * n, out_specs=[_any()] * n,
        scratch_shapes=[pltpu.SemaphoreType.DMA((n, 7)), pltpu.SemaphoreType.DMA((n, 7))],
    )(*partials)


def _owner_sums(tag, arrived, partials=None):
    x, y, c = _place()
    sums = []
    for w, arr in enumerate(arrived):
        if partials is not None:
            part = partials[w]
            h = part.shape[1] // 2
            own = lax.dynamic_slice(part, (2 * x + y, c * h, 0), (1, h, part.shape[2]))
            arr = lax.dynamic_update_slice(arr, own, (4 * x + 2 * y + c, 0, 0))
        sums.append(_ordered_sum(f"owner_sum_{tag}_{w}", arr, into_half=c))
    return sums


def _join_halves(bufs):
    n = len(bufs)

    def body(*refs):
        ins, outs = refs[:n], refs[n:2 * n]
        send_sems, recv_sems = refs[2 * n:]
        x, y, c = _place()
        cps = []
        for w in range(n):
            h = ins[w].shape[0] // 2
            mine = outs[w].at[pl.ds(c * h, h)]
            cp = pltpu.make_async_remote_copy(src_ref=mine, dst_ref=mine, send_sem=send_sems.at[w],
                                              recv_sem=recv_sems.at[w], device_id=(x, y, 1 - c), device_id_type=MESH)
            cp.start()
            cps.append(cp)
        for cp in cps:
            cp.wait()

    return pl.pallas_call(
        body, name="join_halves",
        out_shape=[SDS(b.shape, b.dtype) for b in bufs],
        in_specs=[_any()] * n, out_specs=[_any()] * n, input_output_aliases={w: w for w in range(n)},
        scratch_shapes=[pltpu.SemaphoreType.DMA((n,)), pltpu.SemaphoreType.DMA((n,))],
    )(*bufs)


def _pad_rows(a, rows):
    return jnp.pad(a, ((0, rows - a.shape[0]),) + ((0, 0),) * (a.ndim - 1))


def _pack_small(dmod, g1, g2, gf, b_in, ln_g, ln_b, conv_b, gn_g, gn_b, ga, gb, sb, cw32, sw, loss_row):
    v512 = jnp.concatenate([ln_g, ln_b, conv_b, gn_g, gn_b, ga, gb, jnp.zeros((1, DA), F32)], axis=1).reshape(4, D)
    rows = [dmod.reshape(8, D), g1, g2, gf, b_in.reshape(2, D), v512, sb.reshape(1, D), cw32.reshape(16, D),
            sw.reshape(CHUNK, D), loss_row]
    packed = jnp.concatenate(rows, axis=0)
    return _pad_rows(packed, PK_ROWS)


def _unpack_small(p):
    v512 = p[PK_V512:PK_V512 + 4].reshape(1, 8 * DA)
    pieces = [v512[:, k * DA:(k + 1) * DA] for k in range(7)]
    return dict(
        dmod=p[PK_DMOD:PK_DMOD + 8].reshape(1, 8 * D), norm1_g=p[PK_G1:PK_G1 + 1], norm2_g=p[PK_G2:PK_G2 + 1],
        norm_f_g=p[PK_GF:PK_GF + 1], b_in=p[PK_BIN:PK_BIN + 2].reshape(1, 2 * D),
        a_ln_g=pieces[0], a_ln_b=pieces[1], b_conv_b=pieces[2], b_gn_g=pieces[3], b_gn_b=pieces[4],
        out_norm_a_g=pieces[5], out_norm_b_g=pieces[6],
        a_spatial_b=p[PK_SB:PK_SB + 1].reshape(N_HEADS, CHUNK),
        b_conv_w=p[PK_CW:PK_CW + 16].reshape(HALO, DB),
        a_spatial_w=p[PK_SW:PK_SW + CHUNK].reshape(N_HEADS, CHUNK, CHUNK))


def kernel(x, c, ada_w, ada_b, norm1_g, w_in, b_in, a_ln_g, a_ln_b, a_spatial_w, a_spatial_b, b_conv_w, b_conv_b, b_gn_g, b_gn_b, out_norm_a_g, out_norm_b_g, w_out, norm2_g, w_ffn_in, w_ffn_out, ada_f_w, ada_f_b, norm_f_g, loss_target, m_ada_w, m_ada_b, m_norm1_g, m_w_in, m_b_in, m_a_ln_g, m_a_ln_b, m_a_spatial_w, m_a_spatial_b, m_b_conv_w, m_b_conv_b, m_b_gn_g, m_b_gn_b, m_out_norm_a_g, m_out_norm_b_g, m_w_out, m_norm2_g, m_w_ffn_in, m_w_ffn_out, m_ada_f_w, m_ada_f_b, m_norm_f_g, v_ada_w, v_ada_b, v_norm1_g, v_w_in, v_b_in, v_a_ln_g, v_a_ln_b, v_a_spatial_w, v_a_spatial_b, v_b_conv_w, v_b_conv_b, v_b_gn_g, v_b_gn_b, v_out_norm_a_g, v_out_norm_b_g, v_w_out, v_norm2_g, v_w_ffn_in, v_w_ffn_out, v_ada_f_w, v_ada_f_b, v_norm_f_g):
    mx, my, mc = _place()
    me = 4 * mx + 2 * my + mc
    myq = 2 * mx + my
    xs = x[0]
    target = loss_target[0]
    s = xs.shape[0]
    n_ada = ada_w.shape[2]

    cw_shard = _pad_rows(b_conv_w[0], HALO)
    mix_shards = [w_in[0].astype(MM_DTYPE), w_out[0].astype(MM_DTYPE)]
    ffn_shards = [w_ffn_in[0].astype(MM_DTYPE), w_ffn_out[0].astype(MM_DTYPE)]
    w_in4, w_out4, first = _gather_weights(mix_shards, jnp.concatenate([c.reshape(8, LANES), cw_shard], axis=0))
    w_in4, w_out4 = _own_piece(w_in4, mix_shards[0]), _own_piece(w_out4, mix_shards[1])
    w_out_f = w_out4.reshape(D, D)
    c_all = first[:, 0:8, :].reshape(8, D)
    conv_w = jnp.concatenate([first[4 * (q // 2) + 2 * (q % 2), 8:8 + HALO, :] for q in range(NQ)], axis=1)
    cond_part = jnp.concatenate(_cond_partial(c_all, ada_w[0], ada_f_w), axis=1)
    cond_all = _all_gather8("gather_cond", cond_part)
    cond_q = [cond_all[4 * (q // 2) + 2 * (q % 2)] for q in range(NQ)]
    mod_all = jnp.concatenate([cq[:, :n_ada] for cq in cond_q] + [cq[:, n_ada:] for cq in cond_q], axis=1)
    mod = lax.dynamic_slice_in_dim(mod_all, me, 1, axis=0)
    mods = jnp.concatenate([ada_b, ada_f_b.reshape(1, 2 * D)], axis=1)

    causal = jnp.tril(jnp.ones((CHUNK, CHUNK), dtype=bool))
    wm_f = jnp.where(causal[None], a_spatial_w[0], 0.0)
    wm = wm_f.astype(MM_DTYPE)
    wmt = jnp.swapaxes(wm_f, 1, 2).astype(MM_DTYPE)
    bst = jnp.repeat(a_spatial_b[0].T, HALF, axis=1)

    z, x1, yb1, y, w_ffn_in4, w_ffn_out4 = _mixer_fwd(
        xs, mod, mods, norm1_g, w_in4, b_in, a_ln_g, a_ln_b, wm, bst, conv_w, b_conv_b, b_gn_g, b_gn_b,
        out_norm_a_g, out_norm_b_g, w_out_f, ffn_shards)
    w_ffn_out_f = w_ffn_out4.reshape(DFF, D)
    g, up, a_act, h2, dx2, acc_f = _ffn_fwd(x1, target, mod, mods, norm2_g, norm_f_g, w_ffn_in4, w_ffn_out_f)

    dff, dx1, acc_2 = _ffn_bwd(dx2, x1, g, up, mod, mods, norm2_g, w_ffn_in4, w_ffn_out_f)
    (gw_ffn_in4,) = _grad_matmul("grad_w_ffn_in", h2, dff, D, PW_FF, piece_w=PW_FF)
    modv = mod + mods
    gw_ffn_out, dgate2 = _grad_matmul("grad_w_ffn_out", a_act, dx2, PW_FF, D, gated=(modv[:, 5 * D:6 * D], w_ffn_out_f))
    gw_out, dgate1 = _grad_matmul("grad_w_out", y, dx1, D, D, gated=(modv[:, 2 * D:3 * D], w_out_f))
    early_partials = [gw_ffn_in4, gw_ffn_out.reshape(NQ, DFF // NQ, D), gw_out.reshape(NQ, D // NQ, D)]
    gx, dz, h, acc_1, acc_bin, acc_5, acc_cw, acc_sw, acc_sb, *early_arrived = _mixer_bwd(
        dx1, xs, z, yb1, mod, mods, norm1_g, w_in4, a_ln_g, a_ln_b, wm, wmt, bst, conv_w, b_gn_g, b_gn_b,
        out_norm_a_g, out_norm_b_g, w_out_f, early_partials)

    dmod = jnp.concatenate([acc_1[0:1], acc_1[1:2], dgate1[0:1], acc_2[0:1], acc_2[1:2], dgate2[0:1],
                            acc_f[0:1], acc_f[1:2]], axis=1)
    sw_grad = jnp.where(causal[None], acc_sw, 0.0)
    sb_grad = acc_sb[:, ::HALF].T
    packed = _pack_small(dmod, acc_1[2:3], acc_2[2:3], acc_f[2:3], acc_bin[0:1], acc_5[2:3], acc_5[3:4], acc_5[6:7],
                         acc_5[4:5], acc_5[5:6], acc_5[0:1], acc_5[1:2], sb_grad, acc_cw, sw_grad, acc_f[4:5])
    late_arrived, gathered = _grad_w_in_scattered(h, dz, packed)
    g_w_in, g_w_ffn_in, g_w_ffn_out, g_w_out = _join_halves(
        _owner_sums("late", [late_arrived]) + _owner_sums("early", early_arrived))
    summed = _ordered_sum("small_grad_sum", gathered)
    loss = summed[PK_LOSS, 0]
    small = _unpack_small(summed)
    dmod_all = gathered[:, PK_DMOD:PK_DMOD + 8, :].reshape(8, 8 * D)
    g_ada_w, g_ada_f_w = _cond_grad(c_all, lax.dynamic_slice_in_dim(dmod_all, myq * n_ada, n_ada, axis=1),
                                    lax.dynamic_slice_in_dim(dmod_all, 6 * D + myq * PW_IN, PW_IN, axis=1))

    grads = dict(
        ada_w=g_ada_w, ada_b=small["dmod"][:, :6 * D], norm1_g=small["norm1_g"], w_in=g_w_in,
        b_in=small["b_in"], a_ln_g=small["a_ln_g"], a_ln_b=small["a_ln_b"], a_spatial_w=small["a_spatial_w"],
        a_spatial_b=small["a_spatial_b"],
        b_conv_w=lax.dynamic_slice_in_dim(small["b_conv_w"], myq * LANES, LANES, axis=1)[:CONV_W],
        b_conv_b=small["b_conv_b"], b_gn_g=small["b_gn_g"], b_gn_b=small["b_gn_b"],
        out_norm_a_g=small["out_norm_a_g"], out_norm_b_g=small["out_norm_b_g"], w_out=g_w_out,
        norm2_g=small["norm2_g"], w_ffn_in=g_w_ffn_in, w_ffn_out=g_w_ffn_out, ada_f_w=g_ada_f_w,
        ada_f_b=small["dmod"][:, 6 * D:], norm_f_g=small["norm_f_g"])

    weights = dict(ada_w=ada_w, ada_b=ada_b, norm1_g=norm1_g, w_in=w_in, b_in=b_in, a_ln_g=a_ln_g, a_ln_b=a_ln_b,
                   a_spatial_w=a_spatial_w, a_spatial_b=a_spatial_b, b_conv_w=b_conv_w, b_conv_b=b_conv_b, b_gn_g=b_gn_g,
                   b_gn_b=b_gn_b, out_norm_a_g=out_norm_a_g, out_norm_b_g=out_norm_b_g, w_out=w_out, norm2_g=norm2_g,
                   w_ffn_in=w_ffn_in, w_ffn_out=w_ffn_out, ada_f_w=ada_f_w, ada_f_b=ada_f_b, norm_f_g=norm_f_g)
    m_in = dict(ada_w=m_ada_w, ada_b=m_ada_b, norm1_g=m_norm1_g, w_in=m_w_in, b_in=m_b_in, a_ln_g=m_a_ln_g, a_ln_b=m_a_ln_b,
                a_spatial_w=m_a_spatial_w, a_spatial_b=m_a_spatial_b, b_conv_w=m_b_conv_w, b_conv_b=m_b_conv_b,
                b_gn_g=m_b_gn_g, b_gn_b=m_b_gn_b, out_norm_a_g=m_out_norm_a_g, out_norm_b_g=m_out_norm_b_g, w_out=m_w_out,
                norm2_g=m_norm2_g, w_ffn_in=m_w_ffn_in, w_ffn_out=m_w_ffn_out, ada_f_w=m_ada_f_w, ada_f_b=m_ada_f_b,
                norm_f_g=m_norm_f_g)
    v_in = dict(ada_w=v_ada_w, ada_b=v_ada_b, norm1_g=v_norm1_g, w_in=v_w_in, b_in=v_b_in, a_ln_g=v_a_ln_g, a_ln_b=v_a_ln_b,
                a_spatial_w=v_a_spatial_w, a_spatial_b=v_a_spatial_b, b_conv_w=v_b_conv_w, b_conv_b=v_b_conv_b,
                b_gn_g=v_b_gn_g, b_gn_b=v_b_gn_b, out_norm_a_g=v_out_norm_a_g, out_norm_b_g=v_out_norm_b_g, w_out=v_w_out,
                norm2_g=v_norm2_g, w_ffn_in=v_w_ffn_in, w_ffn_out=v_w_ffn_out, ada_f_w=v_ada_f_w, ada_f_b=v_ada_f_b,
                norm_f_g=v_norm_f_g)
    names = list(weights)
    big = ("ada_w", "w_in", "w_out", "w_ffn_in", "w_ffn_out", "ada_f_w")

    def flat2(a):
        return a.reshape(-1, a.shape[-1])

    delta, new_m, new_v = {}, {}, {}
    for nm in big:
        grads[nm] = grads[nm].reshape(weights[nm].shape)
    big_out = _adamw_tiled("adamw_large", *[[flat2(tree[nm]) for nm in big] for tree in (weights, grads, m_in, v_in)])
    for k, nm in enumerate(big):
        shape = weights[nm].shape
        delta[nm], new_m[nm], new_v[nm] = [o.reshape(shape) for o in big_out[3 * k:3 * k + 3]]

    small_names = [nm for nm in names if nm not in big]
    for nm in small_names:
        grads[nm] = grads[nm].reshape(weights[nm].shape)
    small_out = _adamw_many("adamw_small", *[[flat2(tree[nm]) for nm in small_names] for tree in (weights, grads, m_in, v_in)])
    for k, nm in enumerate(small_names):
        shape = weights[nm].shape
        delta[nm], new_m[nm], new_v[nm] = [o.reshape(shape) for o in small_out[3 * k:3 * k + 3]]

    grad_x = gx.reshape(x.shape)
    return (loss, grad_x, *[grads[nm] for nm in names], *[delta[nm] for nm in names],
            *[new_m[nm] for nm in names], *[new_v[nm] for nm in names])
```

```python
import functools
import math

import jax
import jax.numpy as jnp
from jax import lax
from jax.experimental import pallas as pl
from jax.experimental.pallas import tpu as pltpu

F32 = jnp.float32
MM_DTYPE = jnp.bfloat16
WIRE_DTYPE = jnp.bfloat16
SDS = jax.ShapeDtypeStruct
MESH = pl.DeviceIdType.MESH

D = 1024
DA = 512
DB = 512
NQ = 4
PW_IN = 512
DFF = 2816
PW_FF = 1408
CHUNK = 128
N_HEADS = 8
CONV_W = 31
HALO = 32
CONV_ROWS = 64
EPS = 1e-6
LANES = 128
HALF = 64

ROW_TILE = 256
FWD_ROW_TILE = 512
FFN_ROW_TILE = 256
GRAD_ROW_TILE = 2048
ADAMW_STEPS = 8
W_IN_SPLIT = 2
VMEM_LIMIT = 60 * 1024 * 1024

ADAM_LR = 0.001
ADAM_B1 = 0.9
ADAM_B2 = 0.999
ADAM_EPS = 1e-08
ADAM_WD = 0.01
ADAM_STEP = 10

PK_DMOD = 0
PK_G1 = 8
PK_G2 = 9
PK_GF = 10
PK_BIN = 11
PK_V512 = 13
PK_SB = 17
PK_CW = 18
PK_SW = 34
PK_LOSS = 162
PK_ROWS = 168


def _dot(a, b):
    return jnp.dot(a, b, preferred_element_type=F32)


def _dot_nt(a, b):
    return lax.dot_general(a, b, (((1,), (1,)), ((), ())), preferred_element_type=F32)


def _dot_tn(a, b):
    return lax.dot_general(a, b, (((0,), (0,)), ((), ())), preferred_element_type=F32)


def _rowsum(x):
    return jnp.sum(x, axis=-1, keepdims=True)


def _colsum(x):
    return jnp.sum(x, axis=0, keepdims=True)


def _group_sum(x):
    rows, width = x.shape
    lo_mask = lax.broadcasted_iota(jnp.int32, (rows, LANES), 1) < HALF
    outs = []
    for jb in range(width // LANES):
        xb = x[:, jb * LANES:(jb + 1) * LANES]
        lo = _rowsum(jnp.where(lo_mask, xb, 0.0))
        hi = _rowsum(jnp.where(lo_mask, 0.0, xb))
        outs.append(jnp.where(lo_mask, lo, hi))
    return jnp.concatenate(outs, axis=-1)


def _sigmoid(x):
    return 1.0 / (1.0 + jnp.exp(-x))


def _gelu_parts(u):
    cdf = 0.5 * (1.0 + lax.erf(u * (1.0 / math.sqrt(2.0))))
    pdf = jnp.exp(-0.5 * u * u) * (1.0 / math.sqrt(2.0 * math.pi))
    return u * cdf, cdf + u * pdf


def _whole_vmem():
    return pl.BlockSpec(memory_space=pltpu.VMEM)


def _params(*semantics):
    return pltpu.CompilerParams(dimension_semantics=semantics, vmem_limit_bytes=VMEM_LIMIT)


def _mod_rows(mod_ref, modb_ref, first, count):
    m = mod_ref[...] + modb_ref[...]
    return [m[:, (first + k) * D:(first + k + 1) * D] for k in range(count)]


def _mixer_recompute(z_parts, lng, lnb, wm_ref, bst_ref, mix_ref):
    u, v, val, gate = z_parts
    rows = u.shape[0]
    gu, dgu = _gelu_parts(u)
    gv, dgv = _gelu_parts(v)
    mu = _rowsum(gv) * (1.0 / DA)
    vc = gv - mu
    rs = lax.rsqrt(_rowsum(vc * vc) * (1.0 / DA) + EPS)
    vhat = vc * rs
    vl = vhat * lng + lnb
    vlb = vl.astype(MM_DTYPE)
    lo_mask = lax.broadcasted_iota(jnp.int32, (CHUNK, LANES), 1) < HALF
    for ck in range(rows // CHUNK):
        for jb in range(DA // LANES):
            blk = vlb[ck * CHUNK:(ck + 1) * CHUNK, jb * LANES:(jb + 1) * LANES]
            a = _dot(wm_ref[2 * jb], blk)
            b = _dot(wm_ref[2 * jb + 1], blk)
            mix_ref[ck * CHUNK:(ck + 1) * CHUNK, jb * LANES:(jb + 1) * LANES] = (
                jnp.where(lo_mask, a, b) + bst_ref[:, jb * LANES:(jb + 1) * LANES])
    mixed = mix_ref[...]
    sg = _sigmoid(gate)
    yb0 = val * sg
    return dict(gu=gu, dgu=dgu, dgv=dgv, rs=rs, vhat=vhat, vlb=vlb, mixed=mixed, sg=sg, yb0=yb0)


def _conv_branch_tail(yb1, gng, gnb):
    gm = _group_sum(yb1) * (1.0 / HALF)
    gc = yb1 - gm
    grs = lax.rsqrt(_group_sum(gc * gc) * (1.0 / HALF) + EPS)
    ghat = gc * grs
    yb2 = ghat * gng + gnb
    s2 = _sigmoid(yb2)
    return dict(grs=grs, ghat=ghat, yb2=yb2, s2=s2, y_b=yb2 * s2)


def _shifted_copies(e_ref, sh_ref):
    n = sh_ref.shape[1]
    for b in range(1, 8):
        sh_ref[b - 1] = e_ref[pl.ds(b, n), :]


def _window(e_ref, sh_ref, offset, r0, nrows, cols):
    a, b = divmod(offset, 8)
    if b == 0:
        return e_ref[pl.ds(r0 + 8 * a, nrows), cols]
    return sh_ref[b - 1, pl.ds(r0 + 8 * a, nrows), cols]


def _conv_taps(e_ref, sh_ref, cw_ref, out_ref, ts, first_offset, flip, bias_ref=None, other_ref=None, tap_acc_ref=None):
    groups = CONV_ROWS // 8
    for cb in range(DB // LANES):
        cols = slice(cb * LANES, (cb + 1) * LANES)
        tap_acc = [jnp.zeros((8, LANES), F32) for _ in range(CONV_W)]
        for rb in range(ts // CONV_ROWS):
            r0 = rb * CONV_ROWS
            acc = jnp.zeros((CONV_ROWS, LANES), F32)
            if bias_ref is not None:
                acc = acc + bias_ref[:, cols]
            if other_ref is not None:
                other = other_ref[r0:r0 + CONV_ROWS, cols]
            for j in range(CONV_W):
                k = CONV_W - 1 - j if flip else j
                win = _window(e_ref, sh_ref, first_offset + j, r0, CONV_ROWS, cols)
                acc = acc + win * cw_ref[k:k + 1, cols]
                if other_ref is not None:
                    tap_acc[k] = tap_acc[k] + jnp.sum((other * win).reshape(groups, 8, LANES), axis=0)
            out_ref[r0:r0 + CONV_ROWS, cols] = acc
        if other_ref is not None:
            for k in range(CONV_W):
                tap_acc_ref[k:k + 1, cols] += _colsum(tap_acc[k])


def _gather_plan(ins, outs, send_sems, recv_sems, local_sems=None):
    x, y, c = _place()
    sibling = (x, y, 1 - c)
    chips = _other_chips(x, y)
    myq = 2 * x + y

    def copy(w, k, q, hc, to, src=None):
        rows = ins[w].shape[0]
        dst = outs[w].at[q, pl.ds(hc * (rows // 2), rows // 2)]
        return pltpu.make_async_remote_copy(
            src_ref=dst if src is None else src, dst_ref=dst,
            send_sem=send_sems.at[w, k], recv_sem=recv_sems.at[w, k], device_id=to, device_id_type=MESH)

    def own(w):
        return pltpu.make_async_copy(ins[w], outs[w].at[myq], local_sems.at[w])

    def send():
        for w in range(len(ins)):
            rows = ins[w].shape[0]
            src = ins[w].at[pl.ds(c * (rows // 2), rows // 2)]
            for j, chip in enumerate(chips):
                copy(w, j, myq, c, (*chip, c), src=src).start()
            if local_sems is not None:
                own(w).start()

    def forward():
        for w in range(len(ins)):
            for j, (qx, qy) in enumerate(chips):
                copy(w, j, 2 * qx + qy, c, sibling).wait_recv()
                copy(w, 3 + j, 2 * qx + qy, c, sibling).start()

    def finish():
        for w in range(len(ins)):
            for j, (qx, qy) in enumerate(chips):
                copy(w, 3 + j, 2 * qx + qy, 1 - c, sibling).wait_recv()
        for w in range(len(ins)):
            for k, (qx, qy) in enumerate(chips + chips):
                copy(w, k, 2 * qx + qy, c, sibling).wait_send()
            if local_sems is not None:
                own(w).wait()

    return send, forward, finish


def _mixer_fwd(x, mod, mods, norm1_g, w_in4, b_in, ln_g, ln_b, wm, bst, conv_w, conv_b, gn_g, gn_b, ga, gb, w_out,
               ffn_shards):
    s = x.shape[0]
    ts = min(FWD_ROW_TILE, s)
    nt = s // ts
    n_sh = len(ffn_shards)

    def body(x_ref, mod_ref, modb_ref, g1_ref, w4_ref, bin_ref, lng_ref, lnb_ref, wm_ref, bst_ref, cw_ref, cb_ref,
             gng_ref, gnb_ref, ga_ref, gb_ref, wout_ref, *rest):
        shard_refs, rest = rest[:n_sh], rest[n_sh:]
        z_ref, x1_ref, yb1_ref, y_ref = rest[:4]
        full_refs, rest = rest[4:4 + n_sh], rest[4 + n_sh:]
        e_ref, sh_ref, mix_ref, send_sems, recv_sems, local_sems = rest
        i = pl.program_id(0)
        send, forward, finish = _gather_plan(shard_refs, full_refs, send_sems, recv_sems, local_sems)

        @pl.when(i == 0)
        def _():
            send()
            e_ref[0:HALO, :] = jnp.zeros((HALO, DB), F32)

        @pl.when(i == (3 * nt) // 4)
        def _():
            forward()

        shift1, scale1, gate1 = _mod_rows(mod_ref, modb_ref, 0, 3)
        x_t = x_ref[...]
        r1 = lax.rsqrt(_rowsum(x_t * x_t) * (1.0 / D) + EPS)
        h = (x_t * r1 * g1_ref[...]) * (1.0 + scale1) + shift1
        hb = h.astype(MM_DTYPE)
        z_parts = []
        for q in range(NQ):
            zq = _dot(hb, w4_ref[q]) + bin_ref[:, q * PW_IN:(q + 1) * PW_IN]
            z_ref[:, q * PW_IN:(q + 1) * PW_IN] = zq
            z_parts.append(zq)
        r = _mixer_recompute(z_parts, lng_ref[...], lnb_ref[...], wm_ref, bst_ref, mix_ref)
        y_a = r["gu"] * r["mixed"]
        e_ref[HALO:HALO + ts, :] = r["yb0"]
        _shifted_copies(e_ref, sh_ref)
        _conv_taps(e_ref, sh_ref, cw_ref, yb1_ref, ts, HALO - (CONV_W - 1), False, bias_ref=cb_ref)
        e_ref[0:HALO, :] = e_ref[ts:ts + HALO, :]
        t = _conv_branch_tail(yb1_ref[...], gng_ref[...], gnb_ref[...])
        ra = lax.rsqrt(_rowsum(y_a * y_a) * (1.0 / DA) + EPS)
        rb = lax.rsqrt(_rowsum(t["y_b"] * t["y_b"]) * (1.0 / DB) + EPS)
        yan = (y_a * ra * ga_ref[...]).astype(MM_DTYPE)
        ybn = (t["y_b"] * rb * gb_ref[...]).astype(MM_DTYPE)
        y_ref[:, 0:DA] = yan
        y_ref[:, DA:D] = ybn
        o1 = _dot(yan, wout_ref[0:DA, :]) + _dot(ybn, wout_ref[DA:D, :])
        x1_ref[...] = x_t + gate1 * o1

        @pl.when(i == nt - 1)
        def _():
            finish()

    row = lambda w: pl.BlockSpec((ts, w), lambda i: (i, 0))
    full = lambda a: pl.BlockSpec(a.shape, lambda i: (0,) * a.ndim)
    return pl.pallas_call(
        body, name="mixer_fwd", grid=(nt,),
        in_specs=[row(D), full(mod), full(mods), full(norm1_g), _whole_vmem(), full(b_in), full(ln_g), full(ln_b),
                  _whole_vmem(), full(bst), full(conv_w), full(conv_b), full(gn_g), full(gn_b), full(ga), full(gb),
                  _whole_vmem()] + [_any()] * n_sh,
        out_specs=[row(4 * PW_IN), row(D), row(DB), row(D)] + [_any()] * n_sh,
        out_shape=[SDS((s, 4 * PW_IN), F32), SDS((s, D), F32), SDS((s, DB), F32), SDS((s, D), MM_DTYPE)]
        + [SDS((NQ,) + w.shape, w.dtype) for w in ffn_shards],
        scratch_shapes=[pltpu.VMEM((ts + HALO, DB), F32), pltpu.VMEM((7, ts + HALO - 8, DB), F32), pltpu.VMEM((ts, DA), F32),
                        pltpu.SemaphoreType.DMA((n_sh, 6)), pltpu.SemaphoreType.DMA((n_sh, 6)),
                        pltpu.SemaphoreType.DMA((n_sh,))],
        compiler_params=_params("arbitrary"),
    )(x, mod, mods, norm1_g, w_in4, b_in, ln_g, ln_b, wm, bst, conv_w, conv_b, gn_g, gn_b, ga, gb, w_out, *ffn_shards)


def _ffn_fwd(x1, target, mod, mods, norm2_g, norm_f_g, w_ffn_in4, w_ffn_out):
    s = x1.shape[0]
    sub_rows = min(FFN_ROW_TILE, s)
    ts = min(2 * sub_rows, s)
    nt = s // ts

    def body(x1_ref, tgt_ref, mod_ref, modb_ref, g2_ref, gf_ref, wf_ref, wo_ref,
             g_ref, up_ref, a_ref, h2_ref, dx2_ref, acc_ref):
        i = pl.program_id(0)

        @pl.when(i == 0)
        def _():
            acc_ref[...] = jnp.zeros(acc_ref.shape, F32)

        shift2, scale2, gate2, shift_f, scale_f = _mod_rows(mod_ref, modb_ref, 3, 5)
        for sub in range(ts // sub_rows):
            rows = slice(sub * sub_rows, (sub + 1) * sub_rows)
            x1_t = x1_ref[rows, :]
            r2 = lax.rsqrt(_rowsum(x1_t * x1_t) * (1.0 / D) + EPS)
            h2 = (x1_t * r2 * g2_ref[...]) * (1.0 + scale2) + shift2
            h2b = h2.astype(MM_DTYPE)
            h2_ref[rows, :] = h2b
            o2 = jnp.zeros((sub_rows, D), F32)
            for p in range(2):
                g = _dot(h2b, wf_ref[p])
                up = _dot(h2b, wf_ref[2 + p])
                g_ref[rows, p * PW_FF:(p + 1) * PW_FF] = g.astype(MM_DTYPE)
                up_ref[rows, p * PW_FF:(p + 1) * PW_FF] = up.astype(MM_DTYPE)
                a = (g * _sigmoid(g) * up).astype(MM_DTYPE)
                a_ref[rows, p * PW_FF:(p + 1) * PW_FF] = a
                o2 = o2 + _dot(a, wo_ref[p * PW_FF:(p + 1) * PW_FF, :])
            x2 = x1_t + gate2 * o2
            rf = lax.rsqrt(_rowsum(x2 * x2) * (1.0 / D) + EPS)
            gf = gf_ref[...]
            nf = x2 * rf * gf
            err = nf * (1.0 + scale_f) + shift_f - tgt_ref[rows, :]
            d_out = err * (1.0 / D)
            d_nf = d_out * (1.0 + scale_f)
            t = d_nf * gf
            dx2_ref[rows, :] = rf * t - x2 * (rf * rf * rf) * (_rowsum(t * x2) * (1.0 / D))
            acc_ref[0:1, :] += _colsum(d_out)
            acc_ref[1:2, :] += _colsum(d_out * nf)
            acc_ref[2:3, :] += _colsum(d_nf * x2 * rf)
            acc_ref[3:4, :] += _colsum(err * err)

        @pl.when(i == nt - 1)
        def _():
            acc_ref[4:5, :] = jnp.zeros((1, D), F32) + _rowsum(acc_ref[3:4, :]) * (0.5 / D)

    row = lambda w: pl.BlockSpec((ts, w), lambda i: (i, 0))
    full = lambda a: pl.BlockSpec(a.shape, lambda i: (0,) * a.ndim)
    return pl.pallas_call(
        body, name="ffn_fwd", grid=(nt,),
        in_specs=[row(D), row(D), full(mod), full(mods), full(norm2_g), full(norm_f_g), _whole_vmem(), _whole_vmem()],
        out_specs=[row(DFF), row(DFF), row(DFF), row(D), row(D), pl.BlockSpec((8, D), lambda i: (0, 0))],
        out_shape=[SDS((s, DFF), MM_DTYPE), SDS((s, DFF), MM_DTYPE), SDS((s, DFF), MM_DTYPE), SDS((s, D), MM_DTYPE),
                   SDS((s, D), F32), SDS((8, D), F32)],
        compiler_params=_params("arbitrary"),
    )(x1, target, mod, mods, norm2_g, norm_f_g, w_ffn_in4, w_ffn_out)


def _ffn_bwd(dx2, x1, g, up, mod, mods, norm2_g, w_ffn_in4, w_ffn_out):
    s = x1.shape[0]
    sub_rows = min(FFN_ROW_TILE, s)
    ts = min(2 * sub_rows, s)
    nt = s // ts

    def body(dx2_ref, x1_ref, g_ref, up_ref, mod_ref, modb_ref, g2_ref, wf_ref, wo_ref,
             dff_ref, dx1_ref, acc_ref):
        @pl.when(pl.program_id(0) == 0)
        def _():
            acc_ref[...] = jnp.zeros(acc_ref.shape, F32)

        shift2, scale2, gate2 = _mod_rows(mod_ref, modb_ref, 3, 3)
        for sub in range(ts // sub_rows):
            rows = slice(sub * sub_rows, (sub + 1) * sub_rows)
            dx2_t = dx2_ref[rows, :]
            do2 = (dx2_t * gate2).astype(MM_DTYPE)
            dh2 = jnp.zeros((sub_rows, D), F32)
            for p in range(2):
                da = _dot_nt(do2, wo_ref[p * PW_FF:(p + 1) * PW_FF, :])
                gp = g_ref[rows, p * PW_FF:(p + 1) * PW_FF].astype(F32)
                upp = up_ref[rows, p * PW_FF:(p + 1) * PW_FF].astype(F32)
                sg = _sigmoid(gp)
                silu = gp * sg
                dg = (da * upp * (sg * (1.0 + gp * (1.0 - sg)))).astype(MM_DTYPE)
                dup = (da * silu).astype(MM_DTYPE)
                dff_ref[rows, p * PW_FF:(p + 1) * PW_FF] = dg
                dff_ref[rows, DFF + p * PW_FF:DFF + (p + 1) * PW_FF] = dup
                dh2 = dh2 + _dot_nt(dg, wf_ref[p]) + _dot_nt(dup, wf_ref[2 + p])
            x1_t = x1_ref[rows, :]
            r2 = lax.rsqrt(_rowsum(x1_t * x1_t) * (1.0 / D) + EPS)
            g2 = g2_ref[...]
            xr = x1_t * r2
            dn2 = dh2 * (1.0 + scale2)
            t = dn2 * g2
            dx1_ref[rows, :] = dx2_t + r2 * t - x1_t * (r2 * r2 * r2) * (_rowsum(t * x1_t) * (1.0 / D))
            acc_ref[0:1, :] += _colsum(dh2)
            acc_ref[1:2, :] += _colsum(dh2 * (xr * g2))
            acc_ref[2:3, :] += _colsum(dn2 * xr)

    row = lambda w: pl.BlockSpec((ts, w), lambda i: (i, 0))
    full = lambda a: pl.BlockSpec(a.shape, lambda i: (0,) * a.ndim)
    return pl.pallas_call(
        body, name="ffn_bwd", grid=(nt,),
        in_specs=[row(D), row(D), row(DFF), row(DFF), full(mod), full(mods), full(norm2_g), _whole_vmem(), _whole_vmem()],
        out_specs=[row(2 * DFF), row(D), pl.BlockSpec((8, D), lambda i: (0, 0))],
        out_shape=[SDS((s, 2 * DFF), MM_DTYPE), SDS((s, D), F32), SDS((8, D), F32)],
        compiler_params=_params("arbitrary"),
    )(dx2, x1, g, up, mod, mods, norm2_g, w_ffn_in4, w_ffn_out)


def _scatter_plan(ins, outs, send_sems, recv_sems, local_sems=None):
    x, y, c = _place()
    me = 4 * x + 2 * y + c

    def copies():
        cps = []
        for w in range(len(ins)):
            h = ins[w].shape[1] // 2
            for k in range(1, 8):
                px, py, pc = (1 - x if k & 4 else x), (1 - y if k & 2 else y), (1 - c if k & 1 else c)
                cps.append(pltpu.make_async_remote_copy(
                    src_ref=ins[w].at[2 * px + py, pl.ds(pc * h, h)], dst_ref=outs[w].at[me],
                    send_sem=send_sems.at[w, k - 1], recv_sem=recv_sems.at[w, k - 1],
                    device_id=(px, py, pc), device_id_type=MESH))
        return cps

    def own():
        if local_sems is None:
            return []
        return [pltpu.make_async_copy(ins[w].at[2 * x + y, pl.ds(c * (ins[w].shape[1] // 2), ins[w].shape[1] // 2)],
                                      outs[w].at[me], local_sems.at[w]) for w in range(len(ins))]

    def send():
        for cp in copies() + own():
            cp.start()

    def finish():
        for cp in copies() + own():
            cp.wait()

    return send, finish


def _scattered_shape(partial):
    nq, rows, cols = partial.shape
    return SDS((8, rows // 2, cols), partial.dtype)


def _mixer_bwd(dx1, x, z, yb1, mod, mods, norm1_g, w_in4, ln_g, ln_b, wm, wmt, bst, conv_w, gn_g, gn_b, ga, gb, w_out,
               partials):
    s = x.shape[0]
    ts = min(ROW_TILE, s)
    nt = s // ts
    n_cs = len(partials)

    def body(dx1_ref, x_ref, z_ref, yb1_ref, mod_ref, modb_ref, g1_ref, w4_ref, lng_ref, lnb_ref, wm_ref, wmt_ref,
             bst_ref, cw_ref, gng_ref, gnb_ref, ga_ref, gb_ref, wout_ref, *rest):
        cs_refs, rest = rest[:n_cs], rest[n_cs:]
        gx_ref, dz_ref, h_ref, a1_ref, a2_ref, a5_ref, acw_ref, asw_ref, asb_ref = rest[:9]
        arrived_refs, rest = rest[9:9 + n_cs], rest[9 + n_cs:]
        e_ref, sh_ref, mix_ref, dvl_ref, send_sems, recv_sems, local_sems = rest
        i = pl.program_id(0)
        send, finish = _scatter_plan(cs_refs, arrived_refs, send_sems, recv_sems, local_sems)

        @pl.when(i == 0)
        def _():
            send()
            e_ref[ts:ts + HALO, :] = jnp.zeros((HALO, DB), F32)
            for r in (a1_ref, a2_ref, a5_ref, acw_ref, asw_ref, asb_ref):
                r[...] = jnp.zeros(r.shape, F32)

        shift1, scale1, gate1 = _mod_rows(mod_ref, modb_ref, 0, 3)
        dx1_t = dx1_ref[...]
        do1 = (dx1_t * gate1).astype(MM_DTYPE)
        d_yan = _dot_nt(do1, wout_ref[0:DA, :])
        d_ybn = _dot_nt(do1, wout_ref[DA:D, :])

        z_parts = [z_ref[:, q * PW_IN:(q + 1) * PW_IN] for q in range(NQ)]
        u, v, val, gate = z_parts
        lng = lng_ref[...]
        r = _mixer_recompute(z_parts, lng, lnb_ref[...], wm_ref, bst_ref, mix_ref)
        gng = gng_ref[...]
        t = _conv_branch_tail(yb1_ref[...], gng, gnb_ref[...])
        y_a = r["gu"] * r["mixed"]
        y_b = t["y_b"]
        ga_v, gb_v = ga_ref[...], gb_ref[...]
        ra = lax.rsqrt(_rowsum(y_a * y_a) * (1.0 / DA) + EPS)
        rb = lax.rsqrt(_rowsum(y_b * y_b) * (1.0 / DB) + EPS)

        a5_ref[0:1, :] += _colsum(d_yan * y_a * ra)
        a5_ref[1:2, :] += _colsum(d_ybn * y_b * rb)
        ta = d_yan * ga_v
        d_ya = ra * ta - y_a * (ra * ra * ra) * (_rowsum(ta * y_a) * (1.0 / DA))
        tb = d_ybn * gb_v
        d_yb = rb * tb - y_b * (rb * rb * rb) * (_rowsum(tb * y_b) * (1.0 / DB))

        d_u = d_ya * r["mixed"] * r["dgu"]
        d_mixed = d_ya * r["gu"]
        dmb = d_mixed.astype(MM_DTYPE)
        lo_mask = lax.broadcasted_iota(jnp.int32, (CHUNK, LANES), 1) < HALF
        zero_blk = jnp.zeros((CHUNK, LANES), MM_DTYPE)
        sb_acc = jnp.zeros((CHUNK, DA), F32)
        for ck in range(ts // CHUNK):
            rows = slice(ck * CHUNK, (ck + 1) * CHUNK)
            sb_acc = sb_acc + d_mixed[rows, :]
            for jb in range(DA // LANES):
                cols = slice(jb * LANES, (jb + 1) * LANES)
                dm_blk = dmb[rows, cols]
                vl_blk = r["vlb"][rows, cols]
                da_ = _dot(wmt_ref[2 * jb], dm_blk)
                db_ = _dot(wmt_ref[2 * jb + 1], dm_blk)
                dvl_ref[rows, cols] = jnp.where(lo_mask, da_, db_)
                asw_ref[2 * jb] += _dot_nt(jnp.where(lo_mask, dm_blk, zero_blk), vl_blk)
                asw_ref[2 * jb + 1] += _dot_nt(jnp.where(lo_mask, zero_blk, dm_blk), vl_blk)
        asb_ref[...] += sb_acc
        d_vl = dvl_ref[...]
        a5_ref[2:3, :] += _colsum(d_vl * r["vhat"])
        a5_ref[3:4, :] += _colsum(d_vl)
        dvh = d_vl * lng
        d_gv = r["rs"] * (dvh - _rowsum(dvh) * (1.0 / DA) - r["vhat"] * (_rowsum(dvh * r["vhat"]) * (1.0 / DA)))
        d_v = d_gv * r["dgv"]

        yb2, s2 = t["yb2"], t["s2"]
        d_yb2 = d_yb * (s2 * (1.0 + yb2 * (1.0 - s2)))
        a5_ref[4:5, :] += _colsum(d_yb2 * t["ghat"])
        a5_ref[5:6, :] += _colsum(d_yb2)
        dgh = d_yb2 * gng
        d_yb1 = t["grs"] * (dgh - _group_sum(dgh) * (1.0 / HALF) - t["ghat"] * (_group_sum(dgh * t["ghat"]) * (1.0 / HALF)))
        a5_ref[6:7, :] += _colsum(d_yb1)
        e_ref[0:ts, :] = d_yb1
        _shifted_copies(e_ref, sh_ref)
        mix_ref[...] = r["yb0"]
        _conv_taps(e_ref, sh_ref, cw_ref, dvl_ref, ts, 0, True, other_ref=mix_ref, tap_acc_ref=acw_ref)
        d_yb0 = dvl_ref[...]
        e_ref[ts:ts + HALO, :] = e_ref[0:HALO, :]
        sg = r["sg"]
        d_val = d_yb0 * sg
        d_gate = d_yb0 * val * sg * (1.0 - sg)

        dh = jnp.zeros((ts, D), F32)
        for q, dzq in enumerate((d_u, d_v, d_val, d_gate)):
            a2_ref[0:1, q * PW_IN:(q + 1) * PW_IN] += _colsum(dzq)
            dzb = dzq.astype(MM_DTYPE)
            dz_ref[:, q * PW_IN:(q + 1) * PW_IN] = dzb
            dh = dh + _dot_nt(dzb, w4_ref[q])
        x_t = x_ref[...]
        r1 = lax.rsqrt(_rowsum(x_t * x_t) * (1.0 / D) + EPS)
        g1 = g1_ref[...]
        xr = x_t * r1
        n1 = xr * g1
        h_ref[...] = (n1 * (1.0 + scale1) + shift1).astype(MM_DTYPE)
        dn1 = dh * (1.0 + scale1)
        t1 = dn1 * g1
        gx_ref[...] = dx1_t + r1 * t1 - x_t * (r1 * r1 * r1) * (_rowsum(t1 * x_t) * (1.0 / D))
        a1_ref[0:1, :] += _colsum(dh)
        a1_ref[1:2, :] += _colsum(dh * n1)
        a1_ref[2:3, :] += _colsum(dn1 * xr)

        @pl.when(i == nt - 1)
        def _():
            asb_ref[...] = _group_sum(asb_ref[...])
            finish()

    row = lambda w: pl.BlockSpec((ts, w), lambda i: (nt - 1 - i, 0))
    full = lambda a: pl.BlockSpec(a.shape, lambda i: (0,) * a.ndim)
    keep = lambda shape: pl.BlockSpec(shape, lambda i: (0,) * len(shape))
    return pl.pallas_call(
        body, name="mixer_bwd", grid=(nt,),
        in_specs=[row(D), row(D), row(4 * PW_IN), row(DB), full(mod), full(mods), full(norm1_g), _whole_vmem(),
                  full(ln_g), full(ln_b), _whole_vmem(), _whole_vmem(), full(bst), full(conv_w), full(gn_g), full(gn_b),
                  full(ga), full(gb), _whole_vmem()] + [_any()] * n_cs,
        out_specs=[row(D), row(4 * PW_IN), row(D), keep((8, D)), keep((8, 4 * PW_IN)), keep((8, DA)),
                   keep((HALO, DB)), keep((N_HEADS, CHUNK, CHUNK)), keep((CHUNK, DA))] + [_any()] * n_cs,
        out_shape=[SDS((s, D), F32), SDS((s, 4 * PW_IN), MM_DTYPE), SDS((s, D), MM_DTYPE), SDS((8, D), F32),
                   SDS((8, 4 * PW_IN), F32), SDS((8, DA), F32), SDS((HALO, DB), F32),
                   SDS((N_HEADS, CHUNK, CHUNK), F32), SDS((CHUNK, DA), F32)]
        + [_scattered_shape(p) for p in partials],
        scratch_shapes=[pltpu.VMEM((ts + HALO, DB), F32), pltpu.VMEM((7, ts + HALO - 8, DB), F32),
                        pltpu.VMEM((ts, DA), F32), pltpu.VMEM((ts, DA), F32),
                        pltpu.SemaphoreType.DMA((n_cs, 7)), pltpu.SemaphoreType.DMA((n_cs, 7)),
                        pltpu.SemaphoreType.DMA((n_cs,))],
        compiler_params=_params("arbitrary"),
    )(dx1, x, z, yb1, mod, mods, norm1_g, w_in4, ln_g, ln_b, wm, wmt, bst, conv_w, gn_g, gn_b, ga, gb, w_out, *partials)


def _gather8_plan(x_ref, out_ref, send_sems, recv_sems, local_sem):
    x, y, c = _place()
    me, sibling = (x, y, c), (x, y, 1 - c)
    chips = _other_chips(x, y)

    def copy(k, block, to, src=None):
        dst = out_ref.at[4 * block[0] + 2 * block[1] + block[2]]
        return pltpu.make_async_remote_copy(src_ref=dst if src is None else src, dst_ref=dst, send_sem=send_sems.at[k],
                                            recv_sem=recv_sems.at[k], device_id=to, device_id_type=MESH)

    def own():
        return pltpu.make_async_copy(x_ref, out_ref.at[4 * x + 2 * y + c], local_sem)

    def send():
        own().start()
        copy(0, me, sibling, src=x_ref).start()
        for j, chip in enumerate(chips):
            copy(1 + j, me, (*chip, c), src=x_ref).start()

    def forward():
        for j, chip in enumerate(chips):
            copy(1 + j, (*chip, c), me).wait_recv()
            copy(4 + j, (*chip, c), sibling).start()

    def finish():
        copy(0, sibling, me).wait_recv()
        for j, chip in enumerate(chips):
            copy(4 + j, (*chip, 1 - c), me).wait_recv()
        for k in range(7):
            copy(k, me, sibling).wait_send()
        own().wait()

    return send, forward, finish


def _grad_matmul(name, a, b, ka_tile, nb_tile, piece_w=None, gated=None, gather_blk=None):
    s, ka = a.shape
    nb = b.shape[1]
    ts = min(GRAD_ROW_TILE, s)
    nt = s // ts
    nja, njb = ka // ka_tile, nb // nb_tile
    steps = nja * njb * nt
    n_in = 2 + (2 if gated else 0) + (1 if gather_blk is not None else 0)
    n_out = 1 + (1 if gated else 0) + (1 if gather_blk is not None else 0)
    assert not (gated and njb != 1) and not (piece_w and nja != 1)

    def body(*refs):
        ins, outs, scratch = refs[:n_in], refs[n_in:n_in + n_out], refs[n_in + n_out:]
        a_ref, b_ref, o_ref, acc_ref = ins[0], ins[1], outs[0], scratch[0]
        ins = ins[2:]
        step = (pl.program_id(0) * njb + pl.program_id(1)) * nt + pl.program_id(2)
        if gather_blk is not None:
            send, forward, finish = _gather8_plan(ins[-1], outs[-1], *scratch[1:])

            @pl.when(step == 0)
            def _():
                send()

            @pl.when(step == (3 * steps) // 4)
            def _():
                forward()

        prod = _dot_tn(a_ref[...].astype(MM_DTYPE), b_ref[...].astype(MM_DTYPE))

        @pl.when(pl.program_id(2) == 0)
        def _():
            acc_ref[...] = prod

        @pl.when(pl.program_id(2) > 0)
        def _():
            acc_ref[...] += prod

        @pl.when(pl.program_id(2) == nt - 1)
        def _():
            gm = acc_ref[...]
            if gated:
                gate_ref, w_ref, dg_ref = ins[0], ins[1], outs[1]

                @pl.when(step == nt - 1)
                def _():
                    dg_ref[...] = jnp.zeros(dg_ref.shape, F32)

                dg_ref[0:1, :] += _colsum(gm * w_ref[...].astype(F32))
                gm = gm * gate_ref[...]
            if piece_w:
                for q in range(nb_tile // piece_w):
                    o_ref[q] = gm[:, q * piece_w:(q + 1) * piece_w].astype(WIRE_DTYPE)
            else:
                o_ref[...] = gm.astype(WIRE_DTYPE)

        if gather_blk is not None:
            @pl.when(step == steps - 1)
            def _():
                finish()

    in_specs = [pl.BlockSpec((ts, ka_tile), lambda ja, jb, i: (i, ja)),
                pl.BlockSpec((ts, nb_tile), lambda ja, jb, i: (i, jb))]
    operands = [a, b]
    if piece_w:
        out_shape = [SDS((nb // piece_w, ka, piece_w), WIRE_DTYPE)]
        out_specs = [pl.BlockSpec((nb_tile // piece_w, ka, piece_w), lambda ja, jb, i: (jb, 0, 0))]
    else:
        out_shape = [SDS((ka, nb), WIRE_DTYPE)]
        out_specs = [pl.BlockSpec((ka_tile, nb_tile), lambda ja, jb, i: (ja, jb))]
    scratch = [pltpu.VMEM((ka_tile, nb_tile), F32)]
    if gated:
        in_specs += [pl.BlockSpec((1, nb_tile), lambda ja, jb, i: (0, jb)),
                     pl.BlockSpec((ka_tile, nb_tile), lambda ja, jb, i: (ja, jb))]
        operands += list(gated)
        out_shape.append(SDS((8, nb), F32))
        out_specs.append(pl.BlockSpec((8, nb_tile), lambda ja, jb, i: (0, jb)))
    if gather_blk is not None:
        in_specs.append(_any())
        operands.append(gather_blk)
        out_shape.append(SDS((8,) + gather_blk.shape, gather_blk.dtype))
        out_specs.append(_any())
        scratch += [pltpu.SemaphoreType.DMA((7,)), pltpu.SemaphoreType.DMA((7,)), pltpu.SemaphoreType.DMA]
    return pl.pallas_call(
        body, name=name, grid=(nja, njb, nt), in_specs=in_specs, out_specs=out_specs, out_shape=out_shape,
        scratch_shapes=scratch, compiler_params=_params("arbitrary", "arbitrary", "arbitrary"),
    )(*operands)


def _grad_w_in_scattered(h, dz, blk):
    s = h.shape[0]
    ts = min(GRAD_ROW_TILE, s)
    nt = s // ts
    parts = 2 * W_IN_SPLIT
    steps = parts * nt
    half_rows = D // 2
    part_rows = D // parts

    def body(a_ref, b_ref, blk_ref, arr_ref, all_ref, acc_ref, stage_ref, send_sems, recv_sems, g_send, g_recv, g_local):
        ja, i = pl.program_id(0), pl.program_id(1)
        step = ja * nt + i
        x, y, c = _place()
        me = 4 * x + 2 * y + c
        gsend, gforward, gfinish = _gather8_plan(blk_ref, all_ref, g_send, g_recv, g_local)

        @pl.when(step == 0)
        def _():
            gsend()

        @pl.when(step == (7 * steps) // 8)
        def _():
            gforward()

        prod = _dot_tn(a_ref[...], b_ref[...])

        @pl.when(i == 0)
        def _():
            acc_ref[...] = prod

        @pl.when(i > 0)
        def _():
            acc_ref[...] += prod

        def copies(part):
            half, r = divmod(part, W_IN_SPLIT)
            out = []
            for q in range(NQ):
                src, dst = stage_ref.at[part, q], arr_ref.at[me, pl.ds(r * part_rows, part_rows)]
                remote = pltpu.make_async_remote_copy(src_ref=src, dst_ref=dst, send_sem=send_sems.at[part, q],
                                                      recv_sem=recv_sems.at[me, r], device_id=(q // 2, q % 2, half),
                                                      device_id_type=MESH)
                local = pltpu.make_async_copy(src, dst, send_sems.at[part, q])
                out.append((remote, local, (2 * x + y == q) & (c == half)))
            return out

        for part in range(parts):
            @pl.when((ja == part) & (i == nt - 1))
            def _():
                gm = acc_ref[...]
                for q in range(NQ):
                    stage_ref[part, q] = gm[:, q * PW_IN:(q + 1) * PW_IN].astype(WIRE_DTYPE)
                for remote, local, is_self in copies(part):
                    @pl.when(is_self)
                    def _():
                        local.start()

                    @pl.when(jnp.logical_not(is_self))
                    def _():
                        remote.start()

        @pl.when(step == steps - 1)
        def _():
            gfinish()
            for d in range(8):
                for r in range(W_IN_SPLIT):
                    @pl.when(me != d)
                    def _():
                        dst = arr_ref.at[d, pl.ds(r * part_rows, part_rows)]
                        pltpu.make_async_remote_copy(src_ref=stage_ref.at[0, 0], dst_ref=dst, send_sem=send_sems.at[0, 0],
                                                     recv_sem=recv_sems.at[d, r], device_id=(x, y, c),
                                                     device_id_type=MESH).wait_recv()
            for part in range(parts):
                for remote, local, is_self in copies(part):
                    @pl.when(is_self)
                    def _():
                        local.wait()

                    @pl.when(jnp.logical_not(is_self))
                    def _():
                        remote.wait_send()

    return pl.pallas_call(
        body, name="grad_w_in", grid=(parts, nt),
        in_specs=[pl.BlockSpec((ts, part_rows), lambda ja, i: (i, ja)), pl.BlockSpec((ts, NQ * PW_IN), lambda ja, i: (i, 0)),
                  _any()],
        out_specs=[_any(), _any()],
        out_shape=[SDS((8, half_rows, PW_IN), WIRE_DTYPE), SDS((8,) + blk.shape, blk.dtype)],
        scratch_shapes=[pltpu.VMEM((part_rows, NQ * PW_IN), F32), pltpu.VMEM((parts, NQ, part_rows, PW_IN), WIRE_DTYPE),
                        pltpu.SemaphoreType.DMA((parts, NQ)), pltpu.SemaphoreType.DMA((8, W_IN_SPLIT)),
                        pltpu.SemaphoreType.DMA((7,)), pltpu.SemaphoreType.DMA((7,)), pltpu.SemaphoreType.DMA],
        compiler_params=_params("arbitrary", "arbitrary"),
    )(h, dz, blk)


COND_COLS = 512


def _cond_partial(c_all, w_a, w_f):
    na = w_a.shape[1]

    def body(c_ref, wa_ref, wf_ref, oa_ref, of_ref):
        c_t = c_ref[...]
        ca = (c_t * _sigmoid(c_t)).astype(MM_DTYPE)
        oa_ref[...] = _dot(ca, wa_ref[...].astype(MM_DTYPE))

        @pl.when(pl.program_id(0) == 0)
        def _():
            of_ref[...] = _dot(ca, wf_ref[...].astype(MM_DTYPE))

    keep = lambda shape: pl.BlockSpec(shape, lambda j: (0, 0))
    return pl.pallas_call(
        body, name="cond_partial", grid=(na // COND_COLS,),
        in_specs=[keep((8, D)), pl.BlockSpec((D, COND_COLS), lambda j: (0, j)), keep(w_f.shape)],
        out_specs=[pl.BlockSpec((8, COND_COLS), lambda j: (0, j)), keep((8, w_f.shape[1]))],
        out_shape=[SDS((8, na), F32), SDS((8, w_f.shape[1]), F32)],
        compiler_params=_params("arbitrary"),
    )(c_all, w_a, w_f)


def _cond_grad(c_all, dmod_a, dmod_f):
    na = dmod_a.shape[1]

    def body(c_ref, da_ref, df_ref, oa_ref, of_ref):
        c_t = c_ref[...]
        ca = jnp.concatenate([c_t * _sigmoid(c_t), jnp.zeros((8, D), F32)], axis=0).astype(MM_DTYPE)

        def outer(d_ref):
            dm = jnp.concatenate([d_ref[...], jnp.zeros(d_ref.shape, F32)], axis=0).astype(MM_DTYPE)
            return _dot_tn(ca, dm)

        oa_ref[...] = outer(da_ref)

        @pl.when(pl.program_id(0) == 0)
        def _():
            of_ref[...] = outer(df_ref)

    keep = lambda shape: pl.BlockSpec(shape, lambda j: (0, 0))
    return pl.pallas_call(
        body, name="cond_grad", grid=(na // COND_COLS,),
        in_specs=[keep((8, D)), pl.BlockSpec((8, COND_COLS), lambda j: (0, j)), keep(dmod_f.shape)],
        out_specs=[pl.BlockSpec((D, COND_COLS), lambda j: (0, j)), keep((D, dmod_f.shape[1]))],
        out_shape=[SDS((D, na), F32), SDS((D, dmod_f.shape[1]), F32)],
        compiler_params=_params("arbitrary"),
    )(c_all, dmod_a, dmod_f)


def _row_tile(rows, cap=256):
    if rows <= cap:
        return rows
    for t in range(cap, 7, -8):
        if rows % t == 0:
            return t
    return rows


def _ordered_sum(name, parts, into_half=None):
    n, rows, cols = parts.shape
    rt = _row_tile(rows)
    nb = rows // rt

    def body(*refs):
        p_ref, o_ref = refs[-2:]
        acc = p_ref[0].astype(F32)
        for k in range(1, n):
            acc = acc + p_ref[k].astype(F32)
        o_ref[...] = acc

    if into_half is None:
        return pl.pallas_call(
            body, name=name, grid=(nb,),
            in_specs=[pl.BlockSpec((n, rt, cols), lambda i: (0, i, 0))],
            out_specs=pl.BlockSpec((rt, cols), lambda i: (i, 0)), out_shape=SDS((rows, cols), F32),
            compiler_params=_params("parallel"),
        )(parts)
    grid_spec = pltpu.PrefetchScalarGridSpec(
        num_scalar_prefetch=1, grid=(nb,),
        in_specs=[pl.BlockSpec((n, rt, cols), lambda i, c_ref: (0, i, 0))],
        out_specs=pl.BlockSpec((rt, cols), lambda i, c_ref: (c_ref[0] * nb + i, 0)))
    return pl.pallas_call(
        body, name=name, grid_spec=grid_spec, out_shape=SDS((2 * rows, cols), F32),
        compiler_params=_params("parallel"),
    )(into_half.astype(jnp.int32).reshape(1), parts)


def _adamw_update(w_ref, g_ref, m_ref, v_ref, d_ref, nm_ref, nv_ref):
    c1 = 1.0 - ADAM_B1 ** ADAM_STEP
    c2 = 1.0 - ADAM_B2 ** ADAM_STEP
    g_t = g_ref[...]
    m_new = ADAM_B1 * m_ref[...] + (1.0 - ADAM_B1) * g_t
    v_new = ADAM_B2 * v_ref[...] + (1.0 - ADAM_B2) * (g_t * g_t)
    nm_ref[...] = m_new
    nv_ref[...] = v_new
    d_ref[...] = -ADAM_LR * ((m_new / c1) / (jnp.sqrt(v_new / c2) + ADAM_EPS) + ADAM_WD * w_ref[...])


def _adamw_many(name, ws, gs, ms, vs):
    n = len(ws)

    def body(*refs):
        ins, outs = refs[:4 * n], refs[4 * n:]
        for k in range(n):
            _adamw_update(ins[k], ins[n + k], ins[2 * n + k], ins[3 * n + k], *outs[3 * k:3 * k + 3])

    return pl.pallas_call(
        body, name=name, out_shape=[SDS(w.shape, F32) for w in ws for _ in range(3)],
        compiler_params=pltpu.CompilerParams(vmem_limit_bytes=VMEM_LIMIT),
    )(*ws, *gs, *ms, *vs)


def _adamw_tiled(name, ws, gs, ms, vs):
    n = len(ws)

    def body(*refs):
        ins, outs = refs[:4 * n], refs[4 * n:]
        for k in range(n):
            _adamw_update(ins[k], ins[n + k], ins[2 * n + k], ins[3 * n + k], *outs[3 * k:3 * k + 3])

    specs = [pl.BlockSpec((w.shape[0] // ADAMW_STEPS, w.shape[1]), lambda i: (i, 0)) for w in ws]
    return pl.pallas_call(
        body, name=name, grid=(ADAMW_STEPS,), in_specs=specs * 4, out_specs=[s for s in specs for _ in range(3)],
        out_shape=[SDS(w.shape, F32) for w in ws for _ in range(3)], compiler_params=_params("parallel"),
    )(*ws, *gs, *ms, *vs)


def _place():
    return lax.axis_index("x"), lax.axis_index("y"), lax.axis_index("c")


def _other_chips(x, y):
    return [(1 - x, y), (x, 1 - y), (1 - x, 1 - y)]


def _all_gather8(name, blk):
    m, n = blk.shape

    def body(x_ref, out_ref, send_sems, recv_sems, local_sem):
        x, y, c = _place()
        me, sibling = (x, y, c), (x, y, 1 - c)
        chips = _other_chips(x, y)

        def slot(px, py, pc):
            return out_ref.at[4 * px + 2 * py + pc]

        def copy(k, block, to, src=None):
            return pltpu.make_async_remote_copy(
                src_ref=slot(*block) if src is None else src, dst_ref=slot(*block),
                send_sem=send_sems.at[k], recv_sem=recv_sems.at[k], device_id=to, device_id_type=MESH)

        mine = pltpu.make_async_copy(x_ref, slot(*me), local_sem)
        mine.start()
        first = [copy(0, me, sibling, src=x_ref)]
        first += [copy(1 + j, me, (*chip, c), src=x_ref) for j, chip in enumerate(chips)]
        for cp in first:
            cp.start()
        passed = [copy(4 + j, (*chip, c), sibling) for j, chip in enumerate(chips)]
        for j, chip in enumerate(chips):
            copy(1 + j, (*chip, c), me).wait_recv()
            passed[j].start()
        copy(0, sibling, me).wait_recv()
        for j, chip in enumerate(chips):
            copy(4 + j, (*chip, 1 - c), me).wait_recv()
        for cp in first + passed:
            cp.wait_send()
        mine.wait()

    return pl.pallas_call(
        body, name=name, out_shape=SDS((8, m, n), blk.dtype),
        in_specs=[_whole_vmem()], out_specs=_whole_vmem(),
        scratch_shapes=[pltpu.SemaphoreType.DMA((7,)), pltpu.SemaphoreType.DMA((7,)), pltpu.SemaphoreType.DMA],
        compiler_params=pltpu.CompilerParams(vmem_limit_bytes=VMEM_LIMIT),
    )(blk)


def _any():
    return pl.BlockSpec(memory_space=pl.ANY)


def _gather_weights(shards, blk):
    n = len(shards)

    def body(*refs):
        ins, outs, sems = refs[:n + 1], refs[n + 1:2 * n + 2], refs[2 * n + 2:]
        send, forward, finish = _gather_plan(ins[:n], outs[:n], sems[0], sems[1])
        send8, forward8, finish8 = _gather8_plan(ins[n], outs[n], *sems[2:])
        send8()
        send()
        forward8()
        forward()
        finish8()
        finish()

    return pl.pallas_call(
        body, name="gather_weights",
        out_shape=[SDS((NQ,) + s.shape, s.dtype) for s in shards] + [SDS((8,) + blk.shape, blk.dtype)],
        in_specs=[_any()] * (n + 1), out_specs=[_any()] * (n + 1),
        scratch_shapes=[pltpu.SemaphoreType.DMA((n, 6)), pltpu.SemaphoreType.DMA((n, 6)),
                        pltpu.SemaphoreType.DMA((7,)), pltpu.SemaphoreType.DMA((7,)), pltpu.SemaphoreType.DMA],
    )(*shards, blk)


def _own_piece(gathered, shard):
    myq = 2 * lax.axis_index("x") + lax.axis_index("y")
    return lax.dynamic_update_slice(gathered, shard[None], (myq,) + (0,) * shard.ndim)


def _scatter_to_owners(partials):
    n = len(partials)

    def body(*refs):
        send, finish = _scatter_plan(refs[:n], refs[n:2 * n], *refs[2 * n:])
        send()
        finish()

    return pl.pallas_call(
        body, name="scatter_to_owners",
        out_shape=[_scattered_shape(p) for p in partials],
        in_specs=[_any()] * n, out_specs=[_any()] * n,
        scratch_shapes=[pltpu.SemaphoreType.DMA((n, 7)), pltpu.SemaphoreType.DMA((n, 7))],
    )(*partials)


def _owner_sums(tag, arrived, partials=None):
    x, y, c = _place()
    sums = []
    for w, arr in enumerate(arrived):
        if partials is not None:
            part = partials[w]
            h = part.shape[1] // 2
            own = lax.dynamic_slice(part, (2 * x + y, c * h, 0), (1, h, part.shape[2]))
            arr = lax.dynamic_update_slice(arr, own, (4 * x + 2 * y + c, 0, 0))
        sums.append(_ordered_sum(f"owner_sum_{tag}_{w}", arr, into_half=c))
    return sums


def _join_halves(bufs):
    n = len(bufs)

    def body(*refs):
        ins, outs = refs[:n], refs[n:2 * n]
        send_sems, recv_sems = refs[2 * n:]
        x, y, c = _place()
        cps = []
        for w in range(n):
            h = ins[w].shape[0] // 2
            mine = outs[w].at[pl.ds(c * h, h)]
            cp = pltpu.make_async_remote_copy(src_ref=mine, dst_ref=mine, send_sem=send_sems.at[w],
                                              recv_sem=recv_sems.at[w], device_id=(x, y, 1 - c), device_id_type=MESH)
            cp.start()
            cps.append(cp)
        for cp in cps:
            cp.wait()

    return pl.pallas_call(
        body, name="join_halves",
        out_shape=[SDS(b.shape, b.dtype) for b in bufs],
        in_specs=[_any()] * n, out_specs=[_any()] * n, input_output_aliases={w: w for w in range(n)},
        scratch_shapes=[pltpu.SemaphoreType.DMA((n,)), pltpu.SemaphoreType.DMA((n,))],
    )(*bufs)


def _pad_rows(a, rows):
    return jnp.pad(a, ((0, rows - a.shape[0]),) + ((0, 0),) * (a.ndim - 1))


def _pack_small(dmod, g1, g2, gf, b_in, ln_g, ln_b, conv_b, gn_g, gn_b, ga, gb, sb, cw32, sw, loss_row):
    v512 = jnp.concatenate([ln_g, ln_b, conv_b, gn_g, gn_b, ga, gb, jnp.zeros((1, DA), F32)], axis=1).reshape(4, D)
    rows = [dmod.reshape(8, D), g1, g2, gf, b_in.reshape(2, D), v512, sb.reshape(1, D), cw32.reshape(16, D),
            sw.reshape(CHUNK, D), loss_row]
    packed = jnp.concatenate(rows, axis=0)
    return _pad_rows(packed, PK_ROWS)


def _unpack_small(p):
    v512 = p[PK_V512:PK_V512 + 4].reshape(1, 8 * DA)
    pieces = [v512[:, k * DA:(k + 1) * DA] for k in range(7)]
    return dict(
        dmod=p[PK_DMOD:PK_DMOD + 8].reshape(1, 8 * D), norm1_g=p[PK_G1:PK_G1 + 1], norm2_g=p[PK_G2:PK_G2 + 1],
        norm_f_g=p[PK_GF:PK_GF + 1], b_in=p[PK_BIN:PK_BIN + 2].reshape(1, 2 * D),
        a_ln_g=pieces[0], a_ln_b=pieces[1], b_conv_b=pieces[2], b_gn_g=pieces[3], b_gn_b=pieces[4],
        out_norm_a_g=pieces[5], out_norm_b_g=pieces[6],
        a_spatial_b=p[PK_SB:PK_SB + 1].reshape(N_HEADS, CHUNK),
        b_conv_w=p[PK_CW:PK_CW + 16].reshape(HALO, DB),
        a_spatial_w=p[PK_SW:PK_SW + CHUNK].reshape(N_HEADS, CHUNK, CHUNK))


def kernel(x, c, ada_w, ada_b, norm1_g, w_in, b_in, a_ln_g, a_ln_b, a_spatial_w, a_spatial_b, b_conv_w, b_conv_b, b_gn_g, b_gn_b, out_norm_a_g, out_norm_b_g, w_out, norm2_g, w_ffn_in, w_ffn_out, ada_f_w, ada_f_b, norm_f_g, loss_target, m_ada_w, m_ada_b, m_norm1_g, m_w_in, m_b_in, m_a_ln_g, m_a_ln_b, m_a_spatial_w, m_a_spatial_b, m_b_conv_w, m_b_conv_b, m_b_gn_g, m_b_gn_b, m_out_norm_a_g, m_out_norm_b_g, m_w_out, m_norm2_g, m_w_ffn_in, m_w_ffn_out, m_ada_f_w, m_ada_f_b, m_norm_f_g, v_ada_w, v_ada_b, v_norm1_g, v_w_in, v_b_in, v_a_ln_g, v_a_ln_b, v_a_spatial_w, v_a_spatial_b, v_b_conv_w, v_b_conv_b, v_b_gn_g, v_b_gn_b, v_out_norm_a_g, v_out_norm_b_g, v_w_out, v_norm2_g, v_w_ffn_in, v_w_ffn_out, v_ada_f_w, v_ada_f_b, v_norm_f_g):
    mx, my, mc = _place()
    me = 4 * mx + 2 * my + mc
    myq = 2 * mx + my
    xs = x[0]
    target = loss_target[0]
    s = xs.shape[0]
    n_ada = ada_w.shape[2]

    cw_shard = _pad_rows(b_conv_w[0], HALO)
    mix_shards = [w_in[0].astype(MM_DTYPE), w_out[0].astype(MM_DTYPE)]
    ffn_shards = [w_ffn_in[0].astype(MM_DTYPE), w_ffn_out[0].astype(MM_DTYPE)]
    w_in4, w_out4, first = _gather_weights(mix_shards, jnp.concatenate([c.reshape(8, LANES), cw_shard], axis=0))
    w_in4, w_out4 = _own_piece(w_in4, mix_shards[0]), _own_piece(w_out4, mix_shards[1])
    w_out_f = w_out4.reshape(D, D)
    c_all = first[:, 0:8, :].reshape(8, D)
    conv_w = jnp.concatenate([first[4 * (q // 2) + 2 * (q % 2), 8:8 + HALO, :] for q in range(NQ)], axis=1)
    cond_part = jnp.concatenate(_cond_partial(c_all, ada_w[0], ada_f_w), axis=1)
    cond_all = _all_gather8("gather_cond", cond_part)
    cond_q = [cond_all[4 * (q // 2) + 2 * (q % 2)] for q in range(NQ)]
    mod_all = jnp.concatenate([cq[:, :n_ada] for cq in cond_q] + [cq[:, n_ada:] for cq in cond_q], axis=1)
    mod = lax.dynamic_slice_in_dim(mod_all, me, 1, axis=0)
    mods = jnp.concatenate([ada_b, ada_f_b.reshape(1, 2 * D)], axis=1)

    causal = jnp.tril(jnp.ones((CHUNK, CHUNK), dtype=bool))
    wm_f = jnp.where(causal[None], a_spatial_w[0], 0.0)
    wm = wm_f.astype(MM_DTYPE)
    wmt = jnp.swapaxes(wm_f, 1, 2).astype(MM_DTYPE)
    bst = jnp.repeat(a_spatial_b[0].T, HALF, axis=1)

    z, x1, yb1, y, w_ffn_in4, w_ffn_out4 = _mixer_fwd(
        xs, mod, mods, norm1_g, w_in4, b_in, a_ln_g, a_ln_b, wm, bst, conv_w, b_conv_b, b_gn_g, b_gn_b,
        out_norm_a_g, out_norm_b_g, w_out_f, ffn_shards)
    w_ffn_out_f = w_ffn_out4.reshape(DFF, D)
    g, up, a_act, h2, dx2, acc_f = _ffn_fwd(x1, target, mod, mods, norm2_g, norm_f_g, w_ffn_in4, w_ffn_out_f)

    dff, dx1, acc_2 = _ffn_bwd(dx2, x1, g, up, mod, mods, norm2_g, w_ffn_in4, w_ffn_out_f)
    (gw_ffn_in4,) = _grad_matmul("grad_w_ffn_in", h2, dff, D, PW_FF, piece_w=PW_FF)
    modv = mod + mods
    gw_ffn_out, dgate2 = _grad_matmul("grad_w_ffn_out", a_act, dx2, PW_FF, D, gated=(modv[:, 5 * D:6 * D], w_ffn_out_f))
    gw_out, dgate1 = _grad_matmul("grad_w_out", y, dx1, D, D, gated=(modv[:, 2 * D:3 * D], w_out_f))
    early_partials = [gw_ffn_in4, gw_ffn_out.reshape(NQ, DFF // NQ, D), gw_out.reshape(NQ, D // NQ, D)]
    gx, dz, h, acc_1, acc_bin, acc_5, acc_cw, acc_sw, acc_sb, *early_arrived = _mixer_bwd(
        dx1, xs, z, yb1, mod, mods, norm1_g, w_in4, a_ln_g, a_ln_b, wm, wmt, bst, conv_w, b_gn_g, b_gn_b,
        out_norm_a_g, out_norm_b_g, w_out_f, early_partials)

    dmod = jnp.concatenate([acc_1[0:1], acc_1[1:2], dgate1[0:1], acc_2[0:1], acc_2[1:2], dgate2[0:1],
                            acc_f[0:1], acc_f[1:2]], axis=1)
    sw_grad = jnp.where(causal[None], acc_sw, 0.0)
    sb_grad = acc_sb[:, ::HALF].T
    packed = _pack_small(dmod, acc_1[2:3], acc_2[2:3], acc_f[2:3], acc_bin[0:1], acc_5[2:3], acc_5[3:4], acc_5[6:7],
                         acc_5[4:5], acc_5[5:6], acc_5[0:1], acc_5[1:2], sb_grad, acc_cw, sw_grad, acc_f[4:5])
    late_arrived, gathered = _grad_w_in_scattered(h, dz, packed)
    g_w_in, g_w_ffn_in, g_w_ffn_out, g_w_out = _join_halves(
        _owner_sums("late", [late_arrived]) + _owner_sums("early", early_arrived))
    summed = _ordered_sum("small_grad_sum", gathered)
    loss = summed[PK_LOSS, 0]
    small = _unpack_small(summed)
    dmod_all = gathered[:, PK_DMOD:PK_DMOD + 8, :].reshape(8, 8 * D)
    g_ada_w, g_ada_f_w = _cond_grad(c_all, lax.dynamic_slice_in_dim(dmod_all, myq * n_ada, n_ada, axis=1),
                                    lax.dynamic_slice_in_dim(dmod_all, 6 * D + myq * PW_IN, PW_IN, axis=1))

    grads = dict(
        ada_w=g_ada_w, ada_b=small["dmod"][:, :6 * D], norm1_g=small["norm1_g"], w_in=g_w_in,
        b_in=small["b_in"], a_ln_g=small["a_ln_g"], a_ln_b=small["a_ln_b"], a_spatial_w=small["a_spatial_w"],
        a_spatial_b=small["a_spatial_b"],
        b_conv_w=lax.dynamic_slice_in_dim(small["b_conv_w"], myq * LANES, LANES, axis=1)[:CONV_W],
        b_conv_b=small["b_conv_b"], b_gn_g=small["b_gn_g"], b_gn_b=small["b_gn_b"],
        out_norm_a_g=small["out_norm_a_g"], out_norm_b_g=small["out_norm_b_g"], w_out=g_w_out,
        norm2_g=small["norm2_g"], w_ffn_in=g_w_ffn_in, w_ffn_out=g_w_ffn_out, ada_f_w=g_ada_f_w,
        ada_f_b=small["dmod"][:, 6 * D:], norm_f_g=small["norm_f_g"])

    weights = dict(ada_w=ada_w, ada_b=ada_b, norm1_g=norm1_g, w_in=w_in, b_in=b_in, a_ln_g=a_ln_g, a_ln_b=a_ln_b,
                   a_spatial_w=a_spatial_w, a_spatial_b=a_spatial_b, b_conv_w=b_conv_w, b_conv_b=b_conv_b, b_gn_g=b_gn_g,
                   b_gn_b=b_gn_b, out_norm_a_g=out_norm_a_g, out_norm_b_g=out_norm_b_g, w_out=w_out, norm2_g=norm2_g,
                   w_ffn_in=w_ffn_in, w_ffn_out=w_ffn_out, ada_f_w=ada_f_w, ada_f_b=ada_f_b, norm_f_g=norm_f_g)
    m_in = dict(ada_w=m_ada_w, ada_b=m_ada_b, norm1_g=m_norm1_g, w_in=m_w_in, b_in=m_b_in, a_ln_g=m_a_ln_g, a_ln_b=m_a_ln_b,
                a_spatial_w=m_a_spatial_w, a_spatial_b=m_a_spatial_b, b_conv_w=m_b_conv_w, b_conv_b=m_b_conv_b,
                b_gn_g=m_b_gn_g, b_gn_b=m_b_gn_b, out_norm_a_g=m_out_norm_a_g, out_norm_b_g=m_out_norm_b_g, w_out=m_w_out,
                norm2_g=m_norm2_g, w_ffn_in=m_w_ffn_in, w_ffn_out=m_w_ffn_out, ada_f_w=m_ada_f_w, ada_f_b=m_ada_f_b,
                norm_f_g=m_norm_f_g)
    v_in = dict(ada_w=v_ada_w, ada_b=v_ada_b, norm1_g=v_norm1_g, w_in=v_w_in, b_in=v_b_in, a_ln_g=v_a_ln_g, a_ln_b=v_a_ln_b,
                a_spatial_w=v_a_spatial_w, a_spatial_b=v_a_spatial_b, b_conv_w=v_b_conv_w, b_conv_b=v_b_conv_b,
                b_gn_g=v_b_gn_g, b_gn_b=v_b_gn_b, out_norm_a_g=v_out_norm_a_g, out_norm_b_g=v_out_norm_b_g, w_out=v_w_out,
                norm2_g=v_norm2_g, w_ffn_in=v_w_ffn_in, w_ffn_out=v_w_ffn_out, ada_f_w=v_ada_f_w, ada_f_b=v_ada_f_b,
                norm_f_g=v_norm_f_g)
    names = list(weights)
    big = ("ada_w", "w_in", "w_out", "w_ffn_in", "w_ffn_out", "ada_f_w")

    def flat2(a):
        return a.reshape(-1, a.shape[-1])

    delta, new_m, new_v = {}, {}, {}
    for nm in big:
        grads[nm] = grads[nm].reshape(weights[nm].shape)
    big_out = _adamw_tiled("adamw_large", *[[flat2(tree[nm]) for nm in big] for tree in (weights, grads, m_in, v_in)])
    for k, nm in enumerate(big):
        shape = weights[nm].shape
        delta[nm], new_m[nm], new_v[nm] = [o.reshape(shape) for o in big_out[3 * k:3 * k + 3]]

    small_names = [nm for nm in names if nm not in big]
    for nm in small_names:
        grads[nm] = grads[nm].reshape(weights[nm].shape)
    small_out = _adamw_many("adamw_small", *[[flat2(tree[nm]) for nm in small_names] for tree in (weights, grads, m_in, v_in)])
    for k, nm in enumerate(small_names):
        shape = weights[nm].shape
        delta[nm], new_m[nm], new_v[nm] = [o.reshape(shape) for o in small_out[3 * k:3 * k + 3]]

    grad_x = gx.reshape(x.shape)
    return (loss, grad_x, *[grads[nm] for nm in names], *[delta[nm] for nm in names],
            *[new_m[nm] for nm in names], *[new_v[nm] for nm in names])
```

```python
import functools
import math

import jax
import jax.numpy as jnp
from jax import lax
from jax.experimental import pallas as pl
from jax.experimental.pallas import tpu as pltpu

F32 = jnp.float32
MM_DTYPE = jnp.bfloat16
WIRE_DTYPE = jnp.bfloat16
SDS = jax.ShapeDtypeStruct
MESH = pl.DeviceIdType.MESH

D = 1024
DA = 512
DB = 512
NQ = 4
PW_IN = 512
DFF = 2816
PW_FF = 1408
CHUNK = 128
N_HEADS = 8
CONV_W = 31
HALO = 32
CONV_ROWS = 64
EPS = 1e-6
LANES = 128
HALF = 64

ROW_TILE = 256
FWD_ROW_TILE = 512
FFN_ROW_TILE = 256
GRAD_ROW_TILE = 2048
ADAMW_STEPS = 8
W_IN_SPLIT = 2
VMEM_LIMIT = 60 * 1024 * 1024

ADAM_LR = 0.001
ADAM_B1 = 0.9
ADAM_B2 = 0.999
ADAM_EPS = 1e-08
ADAM_WD = 0.01
ADAM_STEP = 10

PK_DMOD = 0
PK_G1 = 8
PK_G2 = 9
PK_GF = 10
PK_BIN = 11
PK_V512 = 13
PK_SB = 17
PK_CW = 18
PK_SW = 34
PK_LOSS = 162
PK_ROWS = 168


def _dot(a, b):
    return jnp.dot(a, b, preferred_element_type=F32)


def _dot_nt(a, b):
    return lax.dot_general(a, b, (((1,), (1,)), ((), ())), preferred_element_type=F32)


def _dot_tn(a, b):
    return lax.dot_general(a, b, (((0,), (0,)), ((), ())), preferred_element_type=F32)


def _rowsum(x):
    return jnp.sum(x, axis=-1, keepdims=True)


def _colsum(x):
    return jnp.sum(x, axis=0, keepdims=True)


def _group_sum(x):
    rows, width = x.shape
    lo_mask = lax.broadcasted_iota(jnp.int32, (rows, LANES), 1) < HALF
    outs = []
    for jb in range(width // LANES):
        xb = x[:, jb * LANES:(jb + 1) * LANES]
        lo = _rowsum(jnp.where(lo_mask, xb, 0.0))
        hi = _rowsum(jnp.where(lo_mask, 0.0, xb))
        outs.append(jnp.where(lo_mask, lo, hi))
    return jnp.concatenate(outs, axis=-1)


def _sigmoid(x):
    return 1.0 / (1.0 + jnp.exp(-x))


def _gelu_parts(u):
    cdf = 0.5 * (1.0 + lax.erf(u * (1.0 / math.sqrt(2.0))))
    pdf = jnp.exp(-0.5 * u * u) * (1.0 / math.sqrt(2.0 * math.pi))
    return u * cdf, cdf + u * pdf


def _whole_vmem():
    return pl.BlockSpec(memory_space=pltpu.VMEM)


def _params(*semantics):
    return pltpu.CompilerParams(dimension_semantics=semantics, vmem_limit_bytes=VMEM_LIMIT)


def _mod_rows(mod_ref, modb_ref, first, count):
    m = mod_ref[...] + modb_ref[...]
    return [m[:, (first + k) * D:(first + k + 1) * D] for k in range(count)]


def _mixer_recompute(z_parts, lng, lnb, wm_ref, bst_ref, mix_ref):
    u, v, val, gate = z_parts
    rows = u.shape[0]
    gu, dgu = _gelu_parts(u)
    gv, dgv = _gelu_parts(v)
    mu = _rowsum(gv) * (1.0 / DA)
    vc = gv - mu
    rs = lax.rsqrt(_rowsum(vc * vc) * (1.0 / DA) + EPS)
    vhat = vc * rs
    vl = vhat * lng + lnb
    vlb = vl.astype(MM_DTYPE)
    lo_mask = lax.broadcasted_iota(jnp.int32, (CHUNK, LANES), 1) < HALF
    for ck in range(rows // CHUNK):
        for jb in range(DA // LANES):
            blk = vlb[ck * CHUNK:(ck + 1) * CHUNK, jb * LANES:(jb + 1) * LANES]
            a = _dot(wm_ref[2 * jb], blk)
            b = _dot(wm_ref[2 * jb + 1], blk)
            mix_ref[ck * CHUNK:(ck + 1) * CHUNK, jb * LANES:(jb + 1) * LANES] = (
                jnp.where(lo_mask, a, b) + bst_ref[:, jb * LANES:(jb + 1) * LANES])
    mixed = mix_ref[...]
    sg = _sigmoid(gate)
    yb0 = val * sg
    return dict(gu=gu, dgu=dgu, dgv=dgv, rs=rs, vhat=vhat, vlb=vlb, mixed=mixed, sg=sg, yb0=yb0)


def _conv_branch_tail(yb1, gng, gnb):
    gm = _group_sum(yb1) * (1.0 / HALF)
    gc = yb1 - gm
    grs = lax.rsqrt(_group_sum(gc * gc) * (1.0 / HALF) + EPS)
    ghat = gc * grs
    yb2 = ghat * gng + gnb
    s2 = _sigmoid(yb2)
    return dict(grs=grs, ghat=ghat, yb2=yb2, s2=s2, y_b=yb2 * s2)


def _shifted_copies(e_ref, sh_ref):
    n = sh_ref.shape[1]
    for b in range(1, 8):
        sh_ref[b - 1] = e_ref[pl.ds(b, n), :]


def _window(e_ref, sh_ref, offset, r0, nrows, cols):
    a, b = divmod(offset, 8)
    if b == 0:
        return e_ref[pl.ds(r0 + 8 * a, nrows), cols]
    return sh_ref[b - 1, pl.ds(r0 + 8 * a, nrows), cols]


def _conv_taps(e_ref, sh_ref, cw_ref, out_ref, ts, first_offset, flip, bias_ref=None, other_ref=None, tap_acc_ref=None):
    groups = CONV_ROWS // 8
    for cb in range(DB // LANES):
        cols = slice(cb * LANES, (cb + 1) * LANES)
        tap_acc = [jnp.zeros((8, LANES), F32) for _ in range(CONV_W)]
        for rb in range(ts // CONV_ROWS):
            r0 = rb * CONV_ROWS
            acc = jnp.zeros((CONV_ROWS, LANES), F32)
            if bias_ref is not None:
                acc = acc + bias_ref[:, cols]
            if other_ref is not None:
                other = other_ref[r0:r0 + CONV_ROWS, cols]
            for j in range(CONV_W):
                k = CONV_W - 1 - j if flip else j
                win = _window(e_ref, sh_ref, first_offset + j, r0, CONV_ROWS, cols)
                acc = acc + win * cw_ref[k:k + 1, cols]
                if other_ref is not None:
                    tap_acc[k] = tap_acc[k] + jnp.sum((other * win).reshape(groups, 8, LANES), axis=0)
            out_ref[r0:r0 + CONV_ROWS, cols] = acc
        if other_ref is not None:
            for k in range(CONV_W):
                tap_acc_ref[k:k + 1, cols] += _colsum(tap_acc[k])


def _gather_plan(ins, outs, send_sems, recv_sems, local_sems=None):
    x, y, c = _place()
    sibling = (x, y, 1 - c)
    chips = _other_chips(x, y)
    myq = 2 * x + y

    def copy(w, k, q, hc, to, src=None):
        rows = ins[w].shape[0]
        dst = outs[w].at[q, pl.ds(hc * (rows // 2), rows // 2)]
        return pltpu.make_async_remote_copy(
            src_ref=dst if src is None else src, dst_ref=dst,
            send_sem=send_sems.at[w, k], recv_sem=recv_sems.at[w, k], device_id=to, device_id_type=MESH)

    def own(w):
        return pltpu.make_async_copy(ins[w], outs[w].at[myq], local_sems.at[w])

    def send():
        for w in range(len(ins)):
            rows = ins[w].shape[0]
            src = ins[w].at[pl.ds(c * (rows // 2), rows // 2)]
            for j, chip in enumerate(chips):
                copy(w, j, myq, c, (*chip, c), src=src).start()
            if local_sems is not None:
                own(w).start()

    def forward():
        for w in range(len(ins)):
            for j, (qx, qy) in enumerate(chips):
                copy(w, j, 2 * qx + qy, c, sibling).wait_recv()
                copy(w, 3 + j, 2 * qx + qy, c, sibling).start()

    def finish():
        for w in range(len(ins)):
            for j, (qx, qy) in enumerate(chips):
                copy(w, 3 + j, 2 * qx + qy, 1 - c, sibling).wait_recv()
        for w in range(len(ins)):
            for k, (qx, qy) in enumerate(chips + chips):
                copy(w, k, 2 * qx + qy, c, sibling).wait_send()
            if local_sems is not None:
                own(w).wait()

    return send, forward, finish


def _mixer_fwd(x, mod, mods, norm1_g, w_in4, b_in, ln_g, ln_b, wm, bst, conv_w, conv_b, gn_g, gn_b, ga, gb, w_out,
               ffn_shards):
    s = x.shape[0]
    ts = min(FWD_ROW_TILE, s)
    nt = s // ts
    n_sh = len(ffn_shards)

    def body(x_ref, mod_ref, modb_ref, g1_ref, w4_ref, bin_ref, lng_ref, lnb_ref, wm_ref, bst_ref, cw_ref, cb_ref,
             gng_ref, gnb_ref, ga_ref, gb_ref, wout_ref, *rest):
        shard_refs, rest = rest[:n_sh], rest[n_sh:]
        z_ref, x1_ref, yb1_ref, y_ref = rest[:4]
        full_refs, rest = rest[4:4 + n_sh], rest[4 + n_sh:]
        e_ref, sh_ref, mix_ref, send_sems, recv_sems, local_sems = rest
        i = pl.program_id(0)
        send, forward, finish = _gather_plan(shard_refs, full_refs, send_sems, recv_sems, local_sems)

        @pl.when(i == 0)
        def _():
            send()
            e_ref[0:HALO, :] = jnp.zeros((HALO, DB), F32)

        @pl.when(i == (3 * nt) // 4)
        def _():
            forward()

        shift1, scale1, gate1 = _mod_rows(mod_ref, modb_ref, 0, 3)
        x_t = x_ref[...]
        r1 = lax.rsqrt(_rowsum(x_t * x_t) * (1.0 / D) + EPS)
        h = (x_t * r1 * g1_ref[...]) * (1.0 + scale1) + shift1
        hb = h.astype(MM_DTYPE)
        z_parts = []
        for q in range(NQ):
            zq = _dot(hb, w4_ref[q]) + bin_ref[:, q * PW_IN:(q + 1) * PW_IN]
            z_ref[:, q * PW_IN:(q + 1) * PW_IN] = zq
            z_parts.append(zq)
        r = _mixer_recompute(z_parts, lng_ref[...], lnb_ref[...], wm_ref, bst_ref, mix_ref)
        y_a = r["gu"] * r["mixed"]
        e_ref[HALO:HALO + ts, :] = r["yb0"]
        _shifted_copies(e_ref, sh_ref)
        _conv_taps(e_ref, sh_ref, cw_ref, yb1_ref, ts, HALO - (CONV_W - 1), False, bias_ref=cb_ref)
        e_ref[0:HALO, :] = e_ref[ts:ts + HALO, :]
        t = _conv_branch_tail(yb1_ref[...], gng_ref[...], gnb_ref[...])
        ra = lax.rsqrt(_rowsum(y_a * y_a) * (1.0 / DA) + EPS)
        rb = lax.rsqrt(_rowsum(t["y_b"] * t["y_b"]) * (1.0 / DB) + EPS)
        yan = (y_a * ra * ga_ref[...]).astype(MM_DTYPE)
        ybn = (t["y_b"] * rb * gb_ref[...]).astype(MM_DTYPE)
        y_ref[:, 0:DA] = yan
        y_ref[:, DA:D] = ybn
        o1 = _dot(yan, wout_ref[0:DA, :]) + _dot(ybn, wout_ref[DA:D, :])
        x1_ref[...] = x_t + gate1 * o1

        @pl.when(i == nt - 1)
        def _():
            finish()

    row = lambda w: pl.BlockSpec((ts, w), lambda i: (i, 0))
    full = lambda a: pl.BlockSpec(a.shape, lambda i: (0,) * a.ndim)
    return pl.pallas_call(
        body, name="mixer_fwd", grid=(nt,),
        in_specs=[row(D), full(mod), full(mods), full(norm1_g), _whole_vmem(), full(b_in), full(ln_g), full(ln_b),
                  _whole_vmem(), full(bst), full(conv_w), full(conv_b), full(gn_g), full(gn_b), full(ga), full(gb),
                  _whole_vmem()] + [_any()] * n_sh,
        out_specs=[row(4 * PW_IN), row(D), row(DB), row(D)] + [_any()] * n_sh,
        out_shape=[SDS((s, 4 * PW_IN), F32), SDS((s, D), F32), SDS((s, DB), F32), SDS((s, D), MM_DTYPE)]
        + [SDS((NQ,) + w.shape, w.dtype) for w in ffn_shards],
        scratch_shapes=[pltpu.VMEM((ts + HALO, DB), F32), pltpu.VMEM((7, ts + HALO - 8, DB), F32), pltpu.VMEM((ts, DA), F32),
                        pltpu.SemaphoreType.DMA((n_sh, 6)), pltpu.SemaphoreType.DMA((n_sh, 6)),
                        pltpu.SemaphoreType.DMA((n_sh,))],
        compiler_params=_params("arbitrary"),
    )(x, mod, mods, norm1_g, w_in4, b_in, ln_g, ln_b, wm, bst, conv_w, conv_b, gn_g, gn_b, ga, gb, w_out, *ffn_shards)


def _ffn_fwd(x1, target, mod, mods, norm2_g, norm_f_g, w_ffn_in4, w_ffn_out):
    s = x1.shape[0]
    sub_rows = min(FFN_ROW_TILE, s)
    ts = min(2 * sub_rows, s)
    nt = s // ts

    def body(x1_ref, tgt_ref, mod_ref, modb_ref, g2_ref, gf_ref, wf_ref, wo_ref,
             g_ref, up_ref, a_ref, h2_ref, dx2_ref, dx2b_ref, acc_ref):
        i = pl.program_id(0)

        @pl.when(i == 0)
        def _():
            acc_ref[...] = jnp.zeros(acc_ref.shape, F32)

        shift2, scale2, gate2, shift_f, scale_f = _mod_rows(mod_ref, modb_ref, 3, 5)
        for sub in range(ts // sub_rows):
            rows = slice(sub * sub_rows, (sub + 1) * sub_rows)
            x1_t = x1_ref[rows, :]
            r2 = lax.rsqrt(_rowsum(x1_t * x1_t) * (1.0 / D) + EPS)
            h2 = (x1_t * r2 * g2_ref[...]) * (1.0 + scale2) + shift2
            h2b = h2.astype(MM_DTYPE)
            h2_ref[rows, :] = h2b
            o2 = jnp.zeros((sub_rows, D), F32)
            for p in range(2):
                g = _dot(h2b, wf_ref[p])
                up = _dot(h2b, wf_ref[2 + p])
                g_ref[rows, p * PW_FF:(p + 1) * PW_FF] = g.astype(MM_DTYPE)
                up_ref[rows, p * PW_FF:(p + 1) * PW_FF] = up.astype(MM_DTYPE)
                a = (g * _sigmoid(g) * up).astype(MM_DTYPE)
                a_ref[rows, p * PW_FF:(p + 1) * PW_FF] = a
                o2 = o2 + _dot(a, wo_ref[p * PW_FF:(p + 1) * PW_FF, :])
            x2 = x1_t + gate2 * o2
            rf = lax.rsqrt(_rowsum(x2 * x2) * (1.0 / D) + EPS)
            gf = gf_ref[...]
            nf = x2 * rf * gf
            err = nf * (1.0 + scale_f) + shift_f - tgt_ref[rows, :]
            d_out = err * (1.0 / D)
            d_nf = d_out * (1.0 + scale_f)
            t = d_nf * gf
            dx2 = rf * t - x2 * (rf * rf * rf) * (_rowsum(t * x2) * (1.0 / D))
            dx2_ref[rows, :] = dx2
            dx2b_ref[rows, :] = dx2.astype(MM_DTYPE)
            acc_ref[0:1, :] += _colsum(d_out)
            acc_ref[1:2, :] += _colsum(d_out * nf)
            acc_ref[2:3, :] += _colsum(d_nf * x2 * rf)
            acc_ref[3:4, :] += _colsum(err * err)

        @pl.when(i == nt - 1)
        def _():
            acc_ref[4:5, :] = jnp.zeros((1, D), F32) + _rowsum(acc_ref[3:4, :]) * (0.5 / D)

    row = lambda w: pl.BlockSpec((ts, w), lambda i: (i, 0))
    full = lambda a: pl.BlockSpec(a.shape, lambda i: (0,) * a.ndim)
    return pl.pallas_call(
        body, name="ffn_fwd", grid=(nt,),
        in_specs=[row(D), row(D), full(mod), full(mods), full(norm2_g), full(norm_f_g), _whole_vmem(), _whole_vmem()],
        out_specs=[row(DFF), row(DFF), row(DFF), row(D), row(D), row(D), pl.BlockSpec((8, D), lambda i: (0, 0))],
        out_shape=[SDS((s, DFF), MM_DTYPE), SDS((s, DFF), MM_DTYPE), SDS((s, DFF), MM_DTYPE), SDS((s, D), MM_DTYPE),
                   SDS((s, D), F32), SDS((s, D), MM_DTYPE), SDS((8, D), F32)],
        compiler_params=_params("arbitrary"),
    )(x1, target, mod, mods, norm2_g, norm_f_g, w_ffn_in4, w_ffn_out)


def _ffn_bwd(dx2, x1, g, up, mod, mods, norm2_g, w_ffn_in4, w_ffn_out):
    s = x1.shape[0]
    sub_rows = min(FFN_ROW_TILE, s)
    ts = min(2 * sub_rows, s)
    nt = s // ts

    def body(dx2_ref, x1_ref, g_ref, up_ref, mod_ref, modb_ref, g2_ref, wf_ref, wo_ref,
             dff_ref, dx1_ref, acc_ref):
        @pl.when(pl.program_id(0) == 0)
        def _():
            acc_ref[...] = jnp.zeros(acc_ref.shape, F32)

        shift2, scale2, gate2 = _mod_rows(mod_ref, modb_ref, 3, 3)
        for sub in range(ts // sub_rows):
            rows = slice(sub * sub_rows, (sub + 1) * sub_rows)
            dx2_t = dx2_ref[rows, :]
            do2 = (dx2_t * gate2).astype(MM_DTYPE)
            dh2 = jnp.zeros((sub_rows, D), F32)
            for p in range(2):
                da = _dot_nt(do2, wo_ref[p * PW_FF:(p + 1) * PW_FF, :])
                gp = g_ref[rows, p * PW_FF:(p + 1) * PW_FF].astype(F32)
                upp = up_ref[rows, p * PW_FF:(p + 1) * PW_FF].astype(F32)
                sg = _sigmoid(gp)
                silu = gp * sg
                dg = (da * upp * (sg * (1.0 + gp * (1.0 - sg)))).astype(MM_DTYPE)
                dup = (da * silu).astype(MM_DTYPE)
                dff_ref[rows, p * PW_FF:(p + 1) * PW_FF] = dg
                dff_ref[rows, DFF + p * PW_FF:DFF + (p + 1) * PW_FF] = dup
                dh2 = dh2 + _dot_nt(dg, wf_ref[p]) + _dot_nt(dup, wf_ref[2 + p])
            x1_t = x1_ref[rows, :]
            r2 = lax.rsqrt(_rowsum(x1_t * x1_t) * (1.0 / D) + EPS)
            g2 = g2_ref[...]
            xr = x1_t * r2
            dn2 = dh2 * (1.0 + scale2)
            t = dn2 * g2
            dx1_ref[rows, :] = dx2_t + r2 * t - x1_t * (r2 * r2 * r2) * (_rowsum(t * x1_t) * (1.0 / D))
            acc_ref[0:1, :] += _colsum(dh2)
            acc_ref[1:2, :] += _colsum(dh2 * (xr * g2))
            acc_ref[2:3, :] += _colsum(dn2 * xr)

    row = lambda w: pl.BlockSpec((ts, w), lambda i: (i, 0))
    full = lambda a: pl.BlockSpec(a.shape, lambda i: (0,) * a.ndim)
    return pl.pallas_call(
        body, name="ffn_bwd", grid=(nt,),
        in_specs=[row(D), row(D), row(DFF), row(DFF), full(mod), full(mods), full(norm2_g), _whole_vmem(), _whole_vmem()],
        out_specs=[row(2 * DFF), row(D), pl.BlockSpec((8, D), lambda i: (0, 0))],
        out_shape=[SDS((s, 2 * DFF), MM_DTYPE), SDS((s, D), F32), SDS((8, D), F32)],
        compiler_params=_params("arbitrary"),
    )(dx2, x1, g, up, mod, mods, norm2_g, w_ffn_in4, w_ffn_out)


def _scatter_plan(ins, outs, send_sems, recv_sems, local_sems=None):
    x, y, c = _place()
    me = 4 * x + 2 * y + c

    def copies():
        cps = []
        for w in range(len(ins)):
            h = ins[w].shape[1] // 2
            for k in range(1, 8):
                px, py, pc = (1 - x if k & 4 else x), (1 - y if k & 2 else y), (1 - c if k & 1 else c)
                cps.append(pltpu.make_async_remote_copy(
                    src_ref=ins[w].at[2 * px + py, pl.ds(pc * h, h)], dst_ref=outs[w].at[me],
                    send_sem=send_sems.at[w, k - 1], recv_sem=recv_sems.at[w, k - 1],
                    device_id=(px, py, pc), device_id_type=MESH))
        return cps

    def own():
        if local_sems is None:
            return []
        return [pltpu.make_async_copy(ins[w].at[2 * x + y, pl.ds(c * (ins[w].shape[1] // 2), ins[w].shape[1] // 2)],
                                      outs[w].at[me], local_sems.at[w]) for w in range(len(ins))]

    def send():
        for cp in copies() + own():
            cp.start()

    def finish():
        for cp in copies() + own():
            cp.wait()

    return send, finish


def _scattered_shape(partial):
    nq, rows, cols = partial.shape
    return SDS((8, rows // 2, cols), partial.dtype)


def _mixer_bwd(dx1, x, z, yb1, mod, mods, norm1_g, w_in4, ln_g, ln_b, wm, wmt, bst, conv_w, gn_g, gn_b, ga, gb, w_out,
               partials):
    s = x.shape[0]
    ts = min(ROW_TILE, s)
    nt = s // ts
    n_cs = len(partials)

    def body(dx1_ref, x_ref, z_ref, yb1_ref, mod_ref, modb_ref, g1_ref, w4_ref, lng_ref, lnb_ref, wm_ref, wmt_ref,
             bst_ref, cw_ref, gng_ref, gnb_ref, ga_ref, gb_ref, wout_ref, *rest):
        cs_refs, rest = rest[:n_cs], rest[n_cs:]
        gx_ref, dz_ref, h_ref, a1_ref, a2_ref, a5_ref, acw_ref, asw_ref, asb_ref = rest[:9]
        arrived_refs, rest = rest[9:9 + n_cs], rest[9 + n_cs:]
        e_ref, sh_ref, mix_ref, dvl_ref, send_sems, recv_sems, local_sems = rest
        i = pl.program_id(0)
        send, finish = _scatter_plan(cs_refs, arrived_refs, send_sems, recv_sems, local_sems)

        @pl.when(i == 0)
        def _():
            send()
            e_ref[ts:ts + HALO, :] = jnp.zeros((HALO, DB), F32)
            for r in (a1_ref, a2_ref, a5_ref, acw_ref, asw_ref, asb_ref):
                r[...] = jnp.zeros(r.shape, F32)

        shift1, scale1, gate1 = _mod_rows(mod_ref, modb_ref, 0, 3)
        dx1_t = dx1_ref[...]
        do1 = (dx1_t * gate1).astype(MM_DTYPE)
        d_yan = _dot_nt(do1, wout_ref[0:DA, :])
        d_ybn = _dot_nt(do1, wout_ref[DA:D, :])

        z_parts = [z_ref[:, q * PW_IN:(q + 1) * PW_IN] for q in range(NQ)]
        u, v, val, gate = z_parts
        lng = lng_ref[...]
        r = _mixer_recompute(z_parts, lng, lnb_ref[...], wm_ref, bst_ref, mix_ref)
        gng = gng_ref[...]
        t = _conv_branch_tail(yb1_ref[...], gng, gnb_ref[...])
        y_a = r["gu"] * r["mixed"]
        y_b = t["y_b"]
        ga_v, gb_v = ga_ref[...], gb_ref[...]
        ra = lax.rsqrt(_rowsum(y_a * y_a) * (1.0 / DA) + EPS)
        rb = lax.rsqrt(_rowsum(y_b * y_b) * (1.0 / DB) + EPS)

        a5_ref[0:1, :] += _colsum(d_yan * y_a * ra)
        a5_ref[1:2, :] += _colsum(d_ybn * y_b * rb)
        ta = d_yan * ga_v
        d_ya = ra * ta - y_a * (ra * ra * ra) * (_rowsum(ta * y_a) * (1.0 / DA))
        tb = d_ybn * gb_v
        d_yb = rb * tb - y_b * (rb * rb * rb) * (_rowsum(tb * y_b) * (1.0 / DB))

        d_u = d_ya * r["mixed"] * r["dgu"]
        d_mixed = d_ya * r["gu"]
        dmb = d_mixed.astype(MM_DTYPE)
        lo_mask = lax.broadcasted_iota(jnp.int32, (CHUNK, LANES), 1) < HALF
        zero_blk = jnp.zeros((CHUNK, LANES), MM_DTYPE)
        sb_acc = jnp.zeros((CHUNK, DA), F32)
        for ck in range(ts // CHUNK):
            rows = slice(ck * CHUNK, (ck + 1) * CHUNK)
            sb_acc = sb_acc + d_mixed[rows, :]
            for jb in range(DA // LANES):
                cols = slice(jb * LANES, (jb + 1) * LANES)
                dm_blk = dmb[rows, cols]
                vl_blk = r["vlb"][rows, cols]
                da_ = _dot(wmt_ref[2 * jb], dm_blk)
                db_ = _dot(wmt_ref[2 * jb + 1], dm_blk)
                dvl_ref[rows, cols] = jnp.where(lo_mask, da_, db_)
                asw_ref[2 * jb] += _dot_nt(jnp.where(lo_mask, dm_blk, zero_blk), vl_blk)
                asw_ref[2 * jb + 1] += _dot_nt(jnp.where(lo_mask, zero_blk, dm_blk), vl_blk)
        asb_ref[...] += sb_acc
        d_vl = dvl_ref[...]
        a5_ref[2:3, :] += _colsum(d_vl * r["vhat"])
        a5_ref[3:4, :] += _colsum(d_vl)
        dvh = d_vl * lng
        d_gv = r["rs"] * (dvh - _rowsum(dvh) * (1.0 / DA) - r["vhat"] * (_rowsum(dvh * r["vhat"]) * (1.0 / DA)))
        d_v = d_gv * r["dgv"]

        yb2, s2 = t["yb2"], t["s2"]
        d_yb2 = d_yb * (s2 * (1.0 + yb2 * (1.0 - s2)))
        a5_ref[4:5, :] += _colsum(d_yb2 * t["ghat"])
        a5_ref[5:6, :] += _colsum(d_yb2)
        dgh = d_yb2 * gng
        d_yb1 = t["grs"] * (dgh - _group_sum(dgh) * (1.0 / HALF) - t["ghat"] * (_group_sum(dgh * t["ghat"]) * (1.0 / HALF)))
        a5_ref[6:7, :] += _colsum(d_yb1)
        e_ref[0:ts, :] = d_yb1
        _shifted_copies(e_ref, sh_ref)
        mix_ref[...] = r["yb0"]
        _conv_taps(e_ref, sh_ref, cw_ref, dvl_ref, ts, 0, True, other_ref=mix_ref, tap_acc_ref=acw_ref)
        d_yb0 = dvl_ref[...]
        e_ref[ts:ts + HALO, :] = e_ref[0:HALO, :]
        sg = r["sg"]
        d_val = d_yb0 * sg
        d_gate = d_yb0 * val * sg * (1.0 - sg)

        dh = jnp.zeros((ts, D), F32)
        for q, dzq in enumerate((d_u, d_v, d_val, d_gate)):
            a2_ref[0:1, q * PW_IN:(q + 1) * PW_IN] += _colsum(dzq)
            dzb = dzq.astype(MM_DTYPE)
            dz_ref[:, q * PW_IN:(q + 1) * PW_IN] = dzb
            dh = dh + _dot_nt(dzb, w4_ref[q])
        x_t = x_ref[...]
        r1 = lax.rsqrt(_rowsum(x_t * x_t) * (1.0 / D) + EPS)
        g1 = g1_ref[...]
        xr = x_t * r1
        n1 = xr * g1
        h_ref[...] = (n1 * (1.0 + scale1) + shift1).astype(MM_DTYPE)
        dn1 = dh * (1.0 + scale1)
        t1 = dn1 * g1
        gx_ref[...] = dx1_t + r1 * t1 - x_t * (r1 * r1 * r1) * (_rowsum(t1 * x_t) * (1.0 / D))
        a1_ref[0:1, :] += _colsum(dh)
        a1_ref[1:2, :] += _colsum(dh * n1)
        a1_ref[2:3, :] += _colsum(dn1 * xr)

        @pl.when(i == nt - 1)
        def _():
            asb_ref[...] = _group_sum(asb_ref[...])
            finish()

    row = lambda w: pl.BlockSpec((ts, w), lambda i: (nt - 1 - i, 0))
    full = lambda a: pl.BlockSpec(a.shape, lambda i: (0,) * a.ndim)
    keep = lambda shape: pl.BlockSpec(shape, lambda i: (0,) * len(shape))
    return pl.pallas_call(
        body, name="mixer_bwd", grid=(nt,),
        in_specs=[row(D), row(D), row(4 * PW_IN), row(DB), full(mod), full(mods), full(norm1_g), _whole_vmem(),
                  full(ln_g), full(ln_b), _whole_vmem(), _whole_vmem(), full(bst), full(conv_w), full(gn_g), full(gn_b),
                  full(ga), full(gb), _whole_vmem()] + [_any()] * n_cs,
        out_specs=[row(D), row(4 * PW_IN), row(D), keep((8, D)), keep((8, 4 * PW_IN)), keep((8, DA)),
                   keep((HALO, DB)), keep((N_HEADS, CHUNK, CHUNK)), keep((CHUNK, DA))] + [_any()] * n_cs,
        out_shape=[SDS((s, D), F32), SDS((s, 4 * PW_IN), MM_DTYPE), SDS((s, D), MM_DTYPE), SDS((8, D), F32),
                   SDS((8, 4 * PW_IN), F32), SDS((8, DA), F32), SDS((HALO, DB), F32),
                   SDS((N_HEADS, CHUNK, CHUNK), F32), SDS((CHUNK, DA), F32)]
        + [_scattered_shape(p) for p in partials],
        scratch_shapes=[pltpu.VMEM((ts + HALO, DB), F32), pltpu.VMEM((7, ts + HALO - 8, DB), F32),
                        pltpu.VMEM((ts, DA), F32), pltpu.VMEM((ts, DA), F32),
                        pltpu.SemaphoreType.DMA((n_cs, 7)), pltpu.SemaphoreType.DMA((n_cs, 7)),
                        pltpu.SemaphoreType.DMA((n_cs,))],
        compiler_params=_params("arbitrary"),
    )(dx1, x, z, yb1, mod, mods, norm1_g, w_in4, ln_g, ln_b, wm, wmt, bst, conv_w, gn_g, gn_b, ga, gb, w_out, *partials)


def _gather8_plan(x_ref, out_ref, send_sems, recv_sems, local_sem):
    x, y, c = _place()
    me, sibling = (x, y, c), (x, y, 1 - c)
    chips = _other_chips(x, y)

    def copy(k, block, to, src=None):
        dst = out_ref.at[4 * block[0] + 2 * block[1] + block[2]]
        return pltpu.make_async_remote_copy(src_ref=dst if src is None else src, dst_ref=dst, send_sem=send_sems.at[k],
                                            recv_sem=recv_sems.at[k], device_id=to, device_id_type=MESH)

    def own():
        return pltpu.make_async_copy(x_ref, out_ref.at[4 * x + 2 * y + c], local_sem)

    def send():
        own().start()
        copy(0, me, sibling, src=x_ref).start()
        for j, chip in enumerate(chips):
            copy(1 + j, me, (*chip, c), src=x_ref).start()

    def forward():
        for j, chip in enumerate(chips):
            copy(1 + j, (*chip, c), me).wait_recv()
            copy(4 + j, (*chip, c), sibling).start()

    def finish():
        copy(0, sibling, me).wait_recv()
        for j, chip in enumerate(chips):
            copy(4 + j, (*chip, 1 - c), me).wait_recv()
        for k in range(7):
            copy(k, me, sibling).wait_send()
        own().wait()

    return send, forward, finish


def _grad_matmul(name, a, b, ka_tile, nb_tile, piece_w=None, gated=None, gather_blk=None):
    s, ka = a.shape
    nb = b.shape[1]
    ts = min(GRAD_ROW_TILE, s)
    nt = s // ts
    nja, njb = ka // ka_tile, nb // nb_tile
    steps = nja * njb * nt
    n_in = 2 + (2 if gated else 0) + (1 if gather_blk is not None else 0)
    n_out = 1 + (1 if gated else 0) + (1 if gather_blk is not None else 0)
    assert not (gated and njb != 1) and not (piece_w and nja != 1)

    def body(*refs):
        ins, outs, scratch = refs[:n_in], refs[n_in:n_in + n_out], refs[n_in + n_out:]
        a_ref, b_ref, o_ref, acc_ref = ins[0], ins[1], outs[0], scratch[0]
        ins = ins[2:]
        step = (pl.program_id(0) * njb + pl.program_id(1)) * nt + pl.program_id(2)
        if gather_blk is not None:
            send, forward, finish = _gather8_plan(ins[-1], outs[-1], *scratch[1:])

            @pl.when(step == 0)
            def _():
                send()

            @pl.when(step == (3 * steps) // 4)
            def _():
                forward()

        prod = _dot_tn(a_ref[...].astype(MM_DTYPE), b_ref[...].astype(MM_DTYPE))

        @pl.when(pl.program_id(2) == 0)
        def _():
            acc_ref[...] = prod

        @pl.when(pl.program_id(2) > 0)
        def _():
            acc_ref[...] += prod

        @pl.when(pl.program_id(2) == nt - 1)
        def _():
            gm = acc_ref[...]
            if gated:
                gate_ref, w_ref, dg_ref = ins[0], ins[1], outs[1]

                @pl.when(step == nt - 1)
                def _():
                    dg_ref[...] = jnp.zeros(dg_ref.shape, F32)

                dg_ref[0:1, :] += _colsum(gm * w_ref[...].astype(F32))
                gm = gm * gate_ref[...]
            if piece_w:
                for q in range(nb_tile // piece_w):
                    o_ref[q] = gm[:, q * piece_w:(q + 1) * piece_w].astype(WIRE_DTYPE)
            else:
                o_ref[...] = gm.astype(WIRE_DTYPE)

        if gather_blk is not None:
            @pl.when(step == steps - 1)
            def _():
                finish()

    in_specs = [pl.BlockSpec((ts, ka_tile), lambda ja, jb, i: (i, ja)),
                pl.BlockSpec((ts, nb_tile), lambda ja, jb, i: (i, jb))]
    operands = [a, b]
    if piece_w:
        out_shape = [SDS((nb // piece_w, ka, piece_w), WIRE_DTYPE)]
        out_specs = [pl.BlockSpec((nb_tile // piece_w, ka, piece_w), lambda ja, jb, i: (jb, 0, 0))]
    else:
        out_shape = [SDS((ka, nb), WIRE_DTYPE)]
        out_specs = [pl.BlockSpec((ka_tile, nb_tile), lambda ja, jb, i: (ja, jb))]
    scratch = [pltpu.VMEM((ka_tile, nb_tile), F32)]
    if gated:
        in_specs += [pl.BlockSpec((1, nb_tile), lambda ja, jb, i: (0, jb)),
                     pl.BlockSpec((ka_tile, nb_tile), lambda ja, jb, i: (ja, jb))]
        operands += list(gated)
        out_shape.append(SDS((8, nb), F32))
        out_specs.append(pl.BlockSpec((8, nb_tile), lambda ja, jb, i: (0, jb)))
    if gather_blk is not None:
        in_specs.append(_any())
        operands.append(gather_blk)
        out_shape.append(SDS((8,) + gather_blk.shape, gather_blk.dtype))
        out_specs.append(_any())
        scratch += [pltpu.SemaphoreType.DMA((7,)), pltpu.SemaphoreType.DMA((7,)), pltpu.SemaphoreType.DMA]
    return pl.pallas_call(
        body, name=name, grid=(nja, njb, nt), in_specs=in_specs, out_specs=out_specs, out_shape=out_shape,
        scratch_shapes=scratch, compiler_params=_params("arbitrary", "arbitrary", "arbitrary"),
    )(*operands)


def _grad_w_in_scattered(h, dz, blk):
    s = h.shape[0]
    ts = min(GRAD_ROW_TILE, s)
    nt = s // ts
    parts = 2 * W_IN_SPLIT
    steps = parts * nt
    half_rows = D // 2
    part_rows = D // parts

    def body(a_ref, b_ref, blk_ref, arr_ref, all_ref, acc_ref, stage_ref, send_sems, recv_sems, g_send, g_recv, g_local):
        ja, i = pl.program_id(0), pl.program_id(1)
        step = ja * nt + i
        x, y, c = _place()
        me = 4 * x + 2 * y + c
        gsend, gforward, gfinish = _gather8_plan(blk_ref, all_ref, g_send, g_recv, g_local)

        @pl.when(step == 0)
        def _():
            gsend()

        @pl.when(step == (7 * steps) // 8)
        def _():
            gforward()

        prod = _dot_tn(a_ref[...], b_ref[...])

        @pl.when(i == 0)
        def _():
            acc_ref[...] = prod

        @pl.when(i > 0)
        def _():
            acc_ref[...] += prod

        def copies(part):
            half, r = divmod(part, W_IN_SPLIT)
            out = []
            for q in range(NQ):
                src, dst = stage_ref.at[part, q], arr_ref.at[me, pl.ds(r * part_rows, part_rows)]
                remote = pltpu.make_async_remote_copy(src_ref=src, dst_ref=dst, send_sem=send_sems.at[part, q],
                                                      recv_sem=recv_sems.at[me, r], device_id=(q // 2, q % 2, half),
                                                      device_id_type=MESH)
                local = pltpu.make_async_copy(src, dst, send_sems.at[part, q])
                out.append((remote, local, (2 * x + y == q) & (c == half)))
            return out

        for part in range(parts):
            @pl.when((ja == part) & (i == nt - 1))
            def _():
                gm = acc_ref[...]
                for q in range(NQ):
                    stage_ref[part, q] = gm[:, q * PW_IN:(q + 1) * PW_IN].astype(WIRE_DTYPE)
                for remote, local, is_self in copies(part):
                    @pl.when(is_self)
                    def _():
                        local.start()

                    @pl.when(jnp.logical_not(is_self))
                    def _():
                        remote.start()

        @pl.when(step == steps - 1)
        def _():
            gfinish()
            for d in range(8):
                for r in range(W_IN_SPLIT):
                    @pl.when(me != d)
                    def _():
                        dst = arr_ref.at[d, pl.ds(r * part_rows, part_rows)]
                        pltpu.make_async_remote_copy(src_ref=stage_ref.at[0, 0], dst_ref=dst, send_sem=send_sems.at[0, 0],
                                                     recv_sem=recv_sems.at[d, r], device_id=(x, y, c),
                                                     device_id_type=MESH).wait_recv()
            for part in range(parts):
                for remote, local, is_self in copies(part):
                    @pl.when(is_self)
                    def _():
                        local.wait()

                    @pl.when(jnp.logical_not(is_self))
                    def _():
                        remote.wait_send()

    return pl.pallas_call(
        body, name="grad_w_in", grid=(parts, nt),
        in_specs=[pl.BlockSpec((ts, part_rows), lambda ja, i: (i, ja)), pl.BlockSpec((ts, NQ * PW_IN), lambda ja, i: (i, 0)),
                  _any()],
        out_specs=[_any(), _any()],
        out_shape=[SDS((8, half_rows, PW_IN), WIRE_DTYPE), SDS((8,) + blk.shape, blk.dtype)],
        scratch_shapes=[pltpu.VMEM((part_rows, NQ * PW_IN), F32), pltpu.VMEM((parts, NQ, part_rows, PW_IN), WIRE_DTYPE),
                        pltpu.SemaphoreType.DMA((parts, NQ)), pltpu.SemaphoreType.DMA((8, W_IN_SPLIT)),
                        pltpu.SemaphoreType.DMA((7,)), pltpu.SemaphoreType.DMA((7,)), pltpu.SemaphoreType.DMA],
        compiler_params=_params("arbitrary", "arbitrary"),
    )(h, dz, blk)


COND_COLS = 512


def _cond_partial(c_all, w_a, w_f):
    na = w_a.shape[1]

    def body(c_ref, wa_ref, wf_ref, oa_ref, of_ref):
        c_t = c_ref[...]
        ca = (c_t * _sigmoid(c_t)).astype(MM_DTYPE)
        oa_ref[...] = _dot(ca, wa_ref[...].astype(MM_DTYPE))

        @pl.when(pl.program_id(0) == 0)
        def _():
            of_ref[...] = _dot(ca, wf_ref[...].astype(MM_DTYPE))

    keep = lambda shape: pl.BlockSpec(shape, lambda j: (0, 0))
    return pl.pallas_call(
        body, name="cond_partial", grid=(na // COND_COLS,),
        in_specs=[keep((8, D)), pl.BlockSpec((D, COND_COLS), lambda j: (0, j)), keep(w_f.shape)],
        out_specs=[pl.BlockSpec((8, COND_COLS), lambda j: (0, j)), keep((8, w_f.shape[1]))],
        out_shape=[SDS((8, na), F32), SDS((8, w_f.shape[1]), F32)],
        compiler_params=_params("arbitrary"),
    )(c_all, w_a, w_f)


def _cond_grad(c_all, dmod_a, dmod_f):
    na = dmod_a.shape[1]

    def body(c_ref, da_ref, df_ref, oa_ref, of_ref):
        c_t = c_ref[...]
        ca = jnp.concatenate([c_t * _sigmoid(c_t), jnp.zeros((8, D), F32)], axis=0).astype(MM_DTYPE)

        def outer(d_ref):
            dm = jnp.concatenate([d_ref[...], jnp.zeros(d_ref.shape, F32)], axis=0).astype(MM_DTYPE)
            return _dot_tn(ca, dm)

        oa_ref[...] = outer(da_ref)

        @pl.when(pl.program_id(0) == 0)
        def _():
            of_ref[...] = outer(df_ref)

    keep = lambda shape: pl.BlockSpec(shape, lambda j: (0, 0))
    return pl.pallas_call(
        body, name="cond_grad", grid=(na // COND_COLS,),
        in_specs=[keep((8, D)), pl.BlockSpec((8, COND_COLS), lambda j: (0, j)), keep(dmod_f.shape)],
        out_specs=[pl.BlockSpec((D, COND_COLS), lambda j: (0, j)), keep((D, dmod_f.shape[1]))],
        out_shape=[SDS((D, na), F32), SDS((D, dmod_f.shape[1]), F32)],
        compiler_params=_params("arbitrary"),
    )(c_all, dmod_a, dmod_f)


def _row_tile(rows, cap=256):
    if rows <= cap:
        return rows
    for t in range(cap, 7, -8):
        if rows % t == 0:
            return t
    return rows


def _ordered_sum(name, parts, into_half=None):
    n, rows, cols = parts.shape
    rt = _row_tile(rows)
    nb = rows // rt

    def body(*refs):
        p_ref, o_ref = refs[-2:]
        acc = p_ref[0].astype(F32)
        for k in range(1, n):
            acc = acc + p_ref[k].astype(F32)
        o_ref[...] = acc

    if into_half is None:
        return pl.pallas_call(
            body, name=name, grid=(nb,),
            in_specs=[pl.BlockSpec((n, rt, cols), lambda i: (0, i, 0))],
            out_specs=pl.BlockSpec((rt, cols), lambda i: (i, 0)), out_shape=SDS((rows, cols), F32),
            compiler_params=_params("parallel"),
        )(parts)
    grid_spec = pltpu.PrefetchScalarGridSpec(
        num_scalar_prefetch=1, grid=(nb,),
        in_specs=[pl.BlockSpec((n, rt, cols), lambda i, c_ref: (0, i, 0))],
        out_specs=pl.BlockSpec((rt, cols), lambda i, c_ref: (c_ref[0] * nb + i, 0)))
    return pl.pallas_call(
        body, name=name, grid_spec=grid_spec, out_shape=SDS((2 * rows, cols), F32),
        compiler_params=_params("parallel"),
    )(into_half.astype(jnp.int32).reshape(1), parts)


def _adamw_update(w_ref, g_ref, m_ref, v_ref, d_ref, nm_ref, nv_ref):
    c1 = 1.0 - ADAM_B1 ** ADAM_STEP
    c2 = 1.0 - ADAM_B2 ** ADAM_STEP
    g_t = g_ref[...]
    m_new = ADAM_B1 * m_ref[...] + (1.0 - ADAM_B1) * g_t
    v_new = ADAM_B2 * v_ref[...] + (1.0 - ADAM_B2) * (g_t * g_t)
    nm_ref[...] = m_new
    nv_ref[...] = v_new
    d_ref[...] = -ADAM_LR * ((m_new / c1) / (jnp.sqrt(v_new / c2) + ADAM_EPS) + ADAM_WD * w_ref[...])


def _adamw_many(name, ws, gs, ms, vs):
    n = len(ws)

    def body(*refs):
        ins, outs = refs[:4 * n], refs[4 * n:]
        for k in range(n):
            _adamw_update(ins[k], ins[n + k], ins[2 * n + k], ins[3 * n + k], *outs[3 * k:3 * k + 3])

    return pl.pallas_call(
        body, name=name, out_shape=[SDS(w.shape, F32) for w in ws for _ in range(3)],
        compiler_params=pltpu.CompilerParams(vmem_limit_bytes=VMEM_LIMIT),
    )(*ws, *gs, *ms, *vs)


def _adamw_tiled(name, ws, gs, ms, vs):
    n = len(ws)

    def body(*refs):
        ins, outs = refs[:4 * n], refs[4 * n:]
        for k in range(n):
            _adamw_update(ins[k], ins[n + k], ins[2 * n + k], ins[3 * n + k], *outs[3 * k:3 * k + 3])

    specs = [pl.BlockSpec((w.shape[0] // ADAMW_STEPS, w.shape[1]), lambda i: (i, 0)) for w in ws]
    return pl.pallas_call(
        body, name=name, grid=(ADAMW_STEPS,), in_specs=specs * 4, out_specs=[s for s in specs for _ in range(3)],
        out_shape=[SDS(w.shape, F32) for w in ws for _ in range(3)], compiler_params=_params("parallel"),
    )(*ws, *gs, *ms, *vs)


def _place():
    return lax.axis_index("x"), lax.axis_index("y"), lax.axis_index("c")


def _other_chips(x, y):
    return [(1 - x, y), (x, 1 - y), (1 - x, 1 - y)]


def _all_gather8(name, blk):
    m, n = blk.shape

    def body(x_ref, out_ref, send_sems, recv_sems, local_sem):
        x, y, c = _place()
        me, sibling = (x, y, c), (x, y, 1 - c)
        chips = _other_chips(x, y)

        def slot(px, py, pc):
            return out_ref.at[4 * px + 2 * py + pc]

        def copy(k, block, to, src=None):
            return pltpu.make_async_remote_copy(
                src_ref=slot(*block) if src is None else src, dst_ref=slot(*block),
                send_sem=send_sems.at[k], recv_sem=recv_sems.at[k], device_id=to, device_id_type=MESH)

        mine = pltpu.make_async_copy(x_ref, slot(*me), local_sem)
        mine.start()
        first = [copy(0, me, sibling, src=x_ref)]
        first += [copy(1 + j, me, (*chip, c), src=x_ref) for j, chip in enumerate(chips)]
        for cp in first:
            cp.start()
        passed = [copy(4 + j, (*chip, c), sibling) for j, chip in enumerate(chips)]
        for j, chip in enumerate(chips):
            copy(1 + j, (*chip, c), me).wait_recv()
            passed[j].start()
        copy(0, sibling, me).wait_recv()
        for j, chip in enumerate(chips):
            copy(4 + j, (*chip, 1 - c), me).wait_recv()
        for cp in first + passed:
            cp.wait_send()
        mine.wait()

    return pl.pallas_call(
        body, name=name, out_shape=SDS((8, m, n), blk.dtype),
        in_specs=[_whole_vmem()], out_specs=_whole_vmem(),
        scratch_shapes=[pltpu.SemaphoreType.DMA((7,)), pltpu.SemaphoreType.DMA((7,)), pltpu.SemaphoreType.DMA],
        compiler_params=pltpu.CompilerParams(vmem_limit_bytes=VMEM_LIMIT),
    )(blk)


def _any():
    return pl.BlockSpec(memory_space=pl.ANY)


def _gather_weights(shards, blk):
    n = len(shards)

    def body(*refs):
        ins, outs, sems = refs[:n + 1], refs[n + 1:2 * n + 2], refs[2 * n + 2:]
        send, forward, finish = _gather_plan(ins[:n], outs[:n], sems[0], sems[1])
        send8, forward8, finish8 = _gather8_plan(ins[n], outs[n], *sems[2:])
        send8()
        send()
        forward8()
        forward()
        finish8()
        finish()

    return pl.pallas_call(
        body, name="gather_weights",
        out_shape=[SDS((NQ,) + s.shape, s.dtype) for s in shards] + [SDS((8,) + blk.shape, blk.dtype)],
        in_specs=[_any()] * (n + 1), out_specs=[_any()] * (n + 1),
        scratch_shapes=[pltpu.SemaphoreType.DMA((n, 6)), pltpu.SemaphoreType.DMA((n, 6)),
                        pltpu.SemaphoreType.DMA((7,)), pltpu.SemaphoreType.DMA((7,)), pltpu.SemaphoreType.DMA],
    )(*shards, blk)


def _own_piece(gathered, shard):
    myq = 2 * lax.axis_index("x") + lax.axis_index("y")
    return lax.dynamic_update_slice(gathered, shard[None], (myq,) + (0,) * shard.ndim)


def _scatter_to_owners(partials):
    n = len(partials)

    def body(*refs):
        send, finish = _scatter_plan(refs[:n], refs[n:2 * n], *refs[2 * n:])
        send()
        finish()

    return pl.pallas_call(
        body, name="scatter_to_owners",
        out_shape=[_scattered_shape(p) for p in partials],
        in_specs=[_any()] * n, out_specs=[_any()] * n,
        scratch_shapes=[pltpu.SemaphoreType.DMA((n, 7)), pltpu.SemaphoreType.DMA((n, 7))],
    )(*partials)


def _owner_sums(tag, arrived, partials=None):
    x, y, c = _place()
    sums = []
    for w, arr in enumerate(arrived):
        if partials is not None:
            part = partials[w]
            h = part.shape[1] // 2
            own = lax.dynamic_slice(part, (2 * x + y, c * h, 0), (1, h, part.shape[2]))
            arr = lax.dynamic_update_slice(arr, own, (4 * x + 2 * y + c, 0, 0))
        sums.append(_ordered_sum(f"owner_sum_{tag}_{w}", arr, into_half=c))
    return sums


def _join_halves(bufs):
    n = len(bufs)

    def body(*refs):
        ins, outs = refs[:n], refs[n:2 * n]
        send_sems, recv_sems = refs[2 * n:]
        x, y, c = _place()
        cps = []
        for w in range(n):
            h = ins[w].shape[0] // 2
            mine = outs[w].at[pl.ds(c * h, h)]
            cp = pltpu.make_async_remote_copy(src_ref=mine, dst_ref=mine, send_sem=send_sems.at[w],
                                              recv_sem=recv_sems.at[w], device_id=(x, y, 1 - c), device_id_type=MESH)
            cp.start()
            cps.append(cp)
        for cp in cps:
            cp.wait()

    return pl.pallas_call(
        body, name="join_halves",
        out_shape=[SDS(b.shape, b.dtype) for b in bufs],
        in_specs=[_any()] * n, out_specs=[_any()] * n, input_output_aliases={w: w for w in range(n)},
        scratch_shapes=[pltpu.SemaphoreType.DMA((n,)), pltpu.SemaphoreType.DMA((n,))],
    )(*bufs)


def _pad_rows(a, rows):
    return jnp.pad(a, ((0, rows - a.shape[0]),) + ((0, 0),) * (a.ndim - 1))


def _pack_small(dmod, g1, g2, gf, b_in, ln_g, ln_b, conv_b, gn_g, gn_b, ga, gb, sb, cw32, sw, loss_row):
    v512 = jnp.concatenate([ln_g, ln_b, conv_b, gn_g, gn_b, ga, gb, jnp.zeros((1, DA), F32)], axis=1).reshape(4, D)
    rows = [dmod.reshape(8, D), g1, g2, gf, b_in.reshape(2, D), v512, sb.reshape(1, D), cw32.reshape(16, D),
            sw.reshape(CHUNK, D), loss_row]
    packed = jnp.concatenate(rows, axis=0)
    return _pad_rows(packed, PK_ROWS)


def _unpack_small(p):
    v512 = p[PK_V512:PK_V512 + 4].reshape(1, 8 * DA)
    pieces = [v512[:, k * DA:(k + 1) * DA] for k in range(7)]
    return dict(
        dmod=p[PK_DMOD:PK_DMOD + 8].reshape(1, 8 * D), norm1_g=p[PK_G1:PK_G1 + 1], norm2_g=p[PK_G2:PK_G2 + 1],
        norm_f_g=p[PK_GF:PK_GF + 1], b_in=p[PK_BIN:PK_BIN + 2].reshape(1, 2 * D),
        a_ln_g=pieces[0], a_ln_b=pieces[1], b_conv_b=pieces[2], b_gn_g=pieces[3], b_gn_b=pieces[4],
        out_norm_a_g=pieces[5], out_norm_b_g=pieces[6],
        a_spatial_b=p[PK_SB:PK_SB + 1].reshape(N_HEADS, CHUNK),
        b_conv_w=p[PK_CW:PK_CW + 16].reshape(HALO, DB),
        a_spatial_w=p[PK_SW:PK_SW + CHUNK].reshape(N_HEADS, CHUNK, CHUNK))


def kernel(x, c, ada_w, ada_b, norm1_g, w_in, b_in, a_ln_g, a_ln_b, a_spatial_w, a_spatial_b, b_conv_w, b_conv_b, b_gn_g, b_gn_b, out_norm_a_g, out_norm_b_g, w_out, norm2_g, w_ffn_in, w_ffn_out, ada_f_w, ada_f_b, norm_f_g, loss_target, m_ada_w, m_ada_b, m_norm1_g, m_w_in, m_b_in, m_a_ln_g, m_a_ln_b, m_a_spatial_w, m_a_spatial_b, m_b_conv_w, m_b_conv_b, m_b_gn_g, m_b_gn_b, m_out_norm_a_g, m_out_norm_b_g, m_w_out, m_norm2_g, m_w_ffn_in, m_w_ffn_out, m_ada_f_w, m_ada_f_b, m_norm_f_g, v_ada_w, v_ada_b, v_norm1_g, v_w_in, v_b_in, v_a_ln_g, v_a_ln_b, v_a_spatial_w, v_a_spatial_b, v_b_conv_w, v_b_conv_b, v_b_gn_g, v_b_gn_b, v_out_norm_a_g, v_out_norm_b_g, v_w_out, v_norm2_g, v_w_ffn_in, v_w_ffn_out, v_ada_f_w, v_ada_f_b, v_norm_f_g):
    mx, my, mc = _place()
    me = 4 * mx + 2 * my + mc
    myq = 2 * mx + my
    xs = x[0]
    target = loss_target[0]
    s = xs.shape[0]
    n_ada = ada_w.shape[2]

    cw_shard = _pad_rows(b_conv_w[0], HALO)
    mix_shards = [w_in[0].astype(MM_DTYPE), w_out[0].astype(MM_DTYPE)]
    ffn_shards = [w_ffn_in[0].astype(MM_DTYPE), w_ffn_out[0].astype(MM_DTYPE)]
    w_in4, w_out4, first = _gather_weights(mix_shards, jnp.concatenate([c.reshape(8, LANES), cw_shard], axis=0))
    w_in4, w_out4 = _own_piece(w_in4, mix_shards[0]), _own_piece(w_out4, mix_shards[1])
    w_out_f = w_out4.reshape(D, D)
    c_all = first[:, 0:8, :].reshape(8, D)
    conv_w = jnp.concatenate([first[4 * (q // 2) + 2 * (q % 2), 8:8 + HALO, :] for q in range(NQ)], axis=1)
    cond_part = jnp.concatenate(_cond_partial(c_all, ada_w[0], ada_f_w), axis=1)
    cond_all = _all_gather8("gather_cond", cond_part)
    cond_q = [cond_all[4 * (q // 2) + 2 * (q % 2)] for q in range(NQ)]
    mod_all = jnp.concatenate([cq[:, :n_ada] for cq in cond_q] + [cq[:, n_ada:] for cq in cond_q], axis=1)
    mod = lax.dynamic_slice_in_dim(mod_all, me, 1, axis=0)
    mods = jnp.concatenate([ada_b, ada_f_b.reshape(1, 2 * D)], axis=1)

    causal = jnp.tril(jnp.ones((CHUNK, CHUNK), dtype=bool))
    wm_f = jnp.where(causal[None], a_spatial_w[0], 0.0)
    wm = wm_f.astype(MM_DTYPE)
    wmt = jnp.swapaxes(wm_f, 1, 2).astype(MM_DTYPE)
    bst = jnp.repeat(a_spatial_b[0].T, HALF, axis=1)

    z, x1, yb1, y, w_ffn_in4, w_ffn_out4 = _mixer_fwd(
        xs, mod, mods, norm1_g, w_in4, b_in, a_ln_g, a_ln_b, wm, bst, conv_w, b_conv_b, b_gn_g, b_gn_b,
        out_norm_a_g, out_norm_b_g, w_out_f, ffn_shards)
    w_ffn_out_f = w_ffn_out4.reshape(DFF, D)
    g, up, a_act, h2, dx2, dx2b, acc_f = _ffn_fwd(x1, target, mod, mods, norm2_g, norm_f_g, w_ffn_in4, w_ffn_out_f)

    dff, dx1, acc_2 = _ffn_bwd(dx2, x1, g, up, mod, mods, norm2_g, w_ffn_in4, w_ffn_out_f)
    (gw_ffn_in4,) = _grad_matmul("grad_w_ffn_in", h2, dff, D, PW_FF, piece_w=PW_FF)
    modv = mod + mods
    gw_ffn_out, dgate2 = _grad_matmul("grad_w_ffn_out", a_act, dx2b, PW_FF, D, gated=(modv[:, 5 * D:6 * D], w_ffn_out_f))
    gw_out, dgate1 = _grad_matmul("grad_w_out", y, dx1, D, D, gated=(modv[:, 2 * D:3 * D], w_out_f))
    early_partials = [gw_ffn_in4, gw_ffn_out.reshape(NQ, DFF // NQ, D), gw_out.reshape(NQ, D // NQ, D)]
    gx, dz, h, acc_1, acc_bin, acc_5, acc_cw, acc_sw, acc_sb, *early_arrived = _mixer_bwd(
        dx1, xs, z, yb1, mod, mods, norm1_g, w_in4, a_ln_g, a_ln_b, wm, wmt, bst, conv_w, b_gn_g, b_gn_b,
        out_norm_a_g, out_norm_b_g, w_out_f, early_partials)

    dmod = jnp.concatenate([acc_1[0:1], acc_1[1:2], dgate1[0:1], acc_2[0:1], acc_2[1:2], dgate2[0:1],
                            acc_f[0:1], acc_f[1:2]], axis=1)
    sw_grad = jnp.where(causal[None], acc_sw, 0.0)
    sb_grad = acc_sb[:, ::HALF].T
    packed = _pack_small(dmod, acc_1[2:3], acc_2[2:3], acc_f[2:3], acc_bin[0:1], acc_5[2:3], acc_5[3:4], acc_5[6:7],
                         acc_5[4:5], acc_5[5:6], acc_5[0:1], acc_5[1:2], sb_grad, acc_cw, sw_grad, acc_f[4:5])
    late_arrived, gathered = _grad_w_in_scattered(h, dz, packed)
    g_w_in, g_w_ffn_in, g_w_ffn_out, g_w_out = _join_halves(
        _owner_sums("late", [late_arrived]) + _owner_sums("early", early_arrived))
    summed = _ordered_sum("small_grad_sum", gathered)
    loss = summed[PK_LOSS, 0]
    small = _unpack_small(summed)
    dmod_all = gathered[:, PK_DMOD:PK_DMOD + 8, :].reshape(8, 8 * D)
    g_ada_w, g_ada_f_w = _cond_grad(c_all, lax.dynamic_slice_in_dim(dmod_all, myq * n_ada, n_ada, axis=1),
                                    lax.dynamic_slice_in_dim(dmod_all, 6 * D + myq * PW_IN, PW_IN, axis=1))

    grads = dict(
        ada_w=g_ada_w, ada_b=small["dmod"][:, :6 * D], norm1_g=small["norm1_g"], w_in=g_w_in,
        b_in=small["b_in"], a_ln_g=small["a_ln_g"], a_ln_b=small["a_ln_b"], a_spatial_w=small["a_spatial_w"],
        a_spatial_b=small["a_spatial_b"],
        b_conv_w=lax.dynamic_slice_in_dim(small["b_conv_w"], myq * LANES, LANES, axis=1)[:CONV_W],
        b_conv_b=small["b_conv_b"], b_gn_g=small["b_gn_g"], b_gn_b=small["b_gn_b"],
        out_norm_a_g=small["out_norm_a_g"], out_norm_b_g=small["out_norm_b_g"], w_out=g_w_out,
        norm2_g=small["norm2_g"], w_ffn_in=g_w_ffn_in, w_ffn_out=g_w_ffn_out, ada_f_w=g_ada_f_w,
        ada_f_b=small["dmod"][:, 6 * D:], norm_f_g=small["norm_f_g"])

    weights = dict(ada_w=ada_w, ada_b=ada_b, norm1_g=norm1_g, w_in=w_in, b_in=b_in, a_ln_g=a_ln_g, a_ln_b=a_ln_b,
                   a_spatial_w=a_spatial_w, a_spatial_b=a_spatial_b, b_conv_w=b_conv_w, b_conv_b=b_conv_b, b_gn_g=b_gn_g,
                   b_gn_b=b_gn_b, out_norm_a_g=out_norm_a_g, out_norm_b_g=out_norm_b_g, w_out=w_out, norm2_g=norm2_g,
                   w_ffn_in=w_ffn_in, w_ffn_out=w_ffn_out, ada_f_w=ada_f_w, ada_f_b=ada_f_b, norm_f_g=norm_f_g)
    m_in = dict(ada_w=m_ada_w, ada_b=m_ada_b, norm1_g=m_norm1_g, w_in=m_w_in, b_in=m_b_in, a_ln_g=m_a_ln_g, a_ln_b=m_a_ln_b,
                a_spatial_w=m_a_spatial_w, a_spatial_b=m_a_spatial_b, b_conv_w=m_b_conv_w, b_conv_b=m_b_conv_b,
                b_gn_g=m_b_gn_g, b_gn_b=m_b_gn_b, out_norm_a_g=m_out_norm_a_g, out_norm_b_g=m_out_norm_b_g, w_out=m_w_out,
                norm2_g=m_norm2_g, w_ffn_in=m_w_ffn_in, w_ffn_out=m_w_ffn_out, ada_f_w=m_ada_f_w, ada_f_b=m_ada_f_b,
                norm_f_g=m_norm_f_g)
    v_in = dict(ada_w=v_ada_w, ada_b=v_ada_b, norm1_g=v_norm1_g, w_in=v_w_in, b_in=v_b_in, a_ln_g=v_a_ln_g, a_ln_b=v_a_ln_b,
                a_spatial_w=v_a_spatial_w, a_spatial_b=v_a_spatial_b, b_conv_w=v_b_conv_w, b_conv_b=v_b_conv_b,
                b_gn_g=v_b_gn_g, b_gn_b=v_b_gn_b, out_norm_a_g=v_out_norm_a_g, out_norm_b_g=v_out_norm_b_g, w_out=v_w_out,
                norm2_g=v_norm2_g, w_ffn_in=v_w_ffn_in, w_ffn_out=v_w_ffn_out, ada_f_w=v_ada_f_w, ada_f_b=v_ada_f_b,
                norm_f_g=v_norm_f_g)
    names = list(weights)
    big = ("ada_w", "w_in", "w_out", "w_ffn_in", "w_ffn_out", "ada_f_w")

    def flat2(a):
        return a.reshape(-1, a.shape[-1])

    delta, new_m, new_v = {}, {}, {}
    for nm in big:
        grads[nm] = grads[nm].reshape(weights[nm].shape)
    big_out = _adamw_tiled("adamw_large", *[[flat2(tree[nm]) for nm in big] for tree in (weights, grads, m_in, v_in)])
    for k, nm in enumerate(big):
        shape = weights[nm].shape
        delta[nm], new_m[nm], new_v[nm] = [o.reshape(shape) for o in big_out[3 * k:3 * k + 3]]

    small_names = [nm for nm in names if nm not in big]
    for nm in small_names:
        grads[nm] = grads[nm].reshape(weights[nm].shape)
    small_out = _adamw_many("adamw_small", *[[flat2(tree[nm]) for nm in small_names] for tree in (weights, grads, m_in, v_in)])
    for k, nm in enumerate(small_names):
        shape = weights[nm].shape
        delta[nm], new_m[nm], new_v[nm] = [o.reshape(shape) for o in small_out[3 * k:3 * k + 3]]

    grad_x = gx.reshape(x.shape)
    return (loss, grad_x, *[grads[nm] for nm in names], *[delta[nm] for nm in names],
            *[new_m[nm] for nm in names], *[new_v[nm] for nm in names])
```

```python
import functools
import math

import jax
import jax.numpy as jnp
from jax import lax
from jax.experimental import pallas as pl
from jax.experimental.pallas import tpu as pltpu

F32 = jnp.float32
MM_DTYPE = jnp.bfloat16
WIRE_DTYPE = jnp.bfloat16
SDS = jax.ShapeDtypeStruct
MESH = pl.DeviceIdType.MESH

D = 1024
DA = 512
DB = 512
NQ = 4
PW_IN = 512
DFF = 2816
PW_FF = 1408
CHUNK = 128
N_HEADS = 8
CONV_W = 31
HALO = 32
CONV_ROWS = 64
EPS = 1e-6
LANES = 128
HALF = 64

ROW_TILE = 256
FWD_ROW_TILE = 512
FFN_ROW_TILE = 256
GRAD_ROW_TILE = 2048
ADAMW_STEPS = 8
W_IN_SPLIT = 2
VMEM_LIMIT = 60 * 1024 * 1024

ADAM_LR = 0.001
ADAM_B1 = 0.9
ADAM_B2 = 0.999
ADAM_EPS = 1e-08
ADAM_WD = 0.01
ADAM_STEP = 10

PK_DMOD = 0
PK_G1 = 8
PK_G2 = 9
PK_GF = 10
PK_BIN = 11
PK_V512 = 13
PK_SB = 17
PK_CW = 18
PK_SW = 34
PK_LOSS = 162
PK_ROWS = 168


def _dot(a, b):
    return jnp.dot(a, b, preferred_element_type=F32)


def _dot_nt(a, b):
    return lax.dot_general(a, b, (((1,), (1,)), ((), ())), preferred_element_type=F32)


def _dot_tn(a, b):
    return lax.dot_general(a, b, (((0,), (0,)), ((), ())), preferred_element_type=F32)


def _rowsum(x):
    return jnp.sum(x, axis=-1, keepdims=True)


def _colsum(x):
    return jnp.sum(x, axis=0, keepdims=True)


def _group_sum(x):
    rows, width = x.shape
    lo_mask = lax.broadcasted_iota(jnp.int32, (rows, LANES), 1) < HALF
    outs = []
    for jb in range(width // LANES):
        xb = x[:, jb * LANES:(jb + 1) * LANES]
        lo = _rowsum(jnp.where(lo_mask, xb, 0.0))
        hi = _rowsum(jnp.where(lo_mask, 0.0, xb))
        outs.append(jnp.where(lo_mask, lo, hi))
    return jnp.concatenate(outs, axis=-1)


def _sigmoid(x):
    return 1.0 / (1.0 + jnp.exp(-x))


def _gelu_parts(u):
    cdf = 0.5 * (1.0 + lax.erf(u * (1.0 / math.sqrt(2.0))))
    pdf = jnp.exp(-0.5 * u * u) * (1.0 / math.sqrt(2.0 * math.pi))
    return u * cdf, cdf + u * pdf


def _whole_vmem():
    return pl.BlockSpec(memory_space=pltpu.VMEM)


def _params(*semantics):
    return pltpu.CompilerParams(dimension_semantics=semantics, vmem_limit_bytes=VMEM_LIMIT)


def _mod_rows(mod_ref, modb_ref, first, count):
    m = mod_ref[...] + modb_ref[...]
    return [m[:, (first + k) * D:(first + k + 1) * D] for k in range(count)]


def _mixer_recompute(z_parts, lng, lnb, wm_ref, bst_ref, mix_ref):
    u, v, val, gate = z_parts
    rows = u.shape[0]
    gu, dgu = _gelu_parts(u)
    gv, dgv = _gelu_parts(v)
    mu = _rowsum(gv) * (1.0 / DA)
    vc = gv - mu
    rs = lax.rsqrt(_rowsum(vc * vc) * (1.0 / DA) + EPS)
    vhat = vc * rs
    vl = vhat * lng + lnb
    vlb = vl.astype(MM_DTYPE)
    lo_mask = lax.broadcasted_iota(jnp.int32, (CHUNK, LANES), 1) < HALF
    for ck in range(rows // CHUNK):
        for jb in range(DA // LANES):
            blk = vlb[ck * CHUNK:(ck + 1) * CHUNK, jb * LANES:(jb + 1) * LANES]
            a = _dot(wm_ref[2 * jb], blk)
            b = _dot(wm_ref[2 * jb + 1], blk)
            mix_ref[ck * CHUNK:(ck + 1) * CHUNK, jb * LANES:(jb + 1) * LANES] = (
                jnp.where(lo_mask, a, b) + bst_ref[:, jb * LANES:(jb + 1) * LANES])
    mixed = mix_ref[...]
    sg = _sigmoid(gate)
    yb0 = val * sg
    return dict(gu=gu, dgu=dgu, dgv=dgv, rs=rs, vhat=vhat, vlb=vlb, mixed=mixed, sg=sg, yb0=yb0)


def _conv_branch_tail(yb1, gng, gnb):
    gm = _group_sum(yb1) * (1.0 / HALF)
    gc = yb1 - gm
    grs = lax.rsqrt(_group_sum(gc * gc) * (1.0 / HALF) + EPS)
    ghat = gc * grs
    yb2 = ghat * gng + gnb
    s2 = _sigmoid(yb2)
    return dict(grs=grs, ghat=ghat, yb2=yb2, s2=s2, y_b=yb2 * s2)


def _shifted_copies(e_ref, sh_ref):
    n = sh_ref.shape[1]
    for b in range(1, 8):
        sh_ref[b - 1] = e_ref[pl.ds(b, n), :]


def _window(e_ref, sh_ref, offset, r0, nrows, cols):
    a, b = divmod(offset, 8)
    if b == 0:
        return e_ref[pl.ds(r0 + 8 * a, nrows), cols]
    return sh_ref[b - 1, pl.ds(r0 + 8 * a, nrows), cols]


def _conv_taps(e_ref, sh_ref, cw_ref, out_ref, ts, first_offset, flip, bias_ref=None, other_ref=None, tap_acc_ref=None):
    groups = CONV_ROWS // 8
    for cb in range(DB // LANES):
        cols = slice(cb * LANES, (cb + 1) * LANES)
        tap_acc = [jnp.zeros((8, LANES), F32) for _ in range(CONV_W)]
        for rb in range(ts // CONV_ROWS):
            r0 = rb * CONV_ROWS
            acc = jnp.zeros((CONV_ROWS, LANES), F32)
            if bias_ref is not None:
                acc = acc + bias_ref[:, cols]
            if other_ref is not None:
                other = other_ref[r0:r0 + CONV_ROWS, cols]
            for j in range(CONV_W):
                k = CONV_W - 1 - j if flip else j
                win = _window(e_ref, sh_ref, first_offset + j, r0, CONV_ROWS, cols)
                acc = acc + win * cw_ref[k:k + 1, cols]
                if other_ref is not None:
                    tap_acc[k] = tap_acc[k] + jnp.sum((other * win).reshape(groups, 8, LANES), axis=0)
            out_ref[r0:r0 + CONV_ROWS, cols] = acc
        if other_ref is not None:
            for k in range(CONV_W):
                tap_acc_ref[k:k + 1, cols] += _colsum(tap_acc[k])


def _gather_plan(ins, outs, send_sems, recv_sems, local_sems=None):
    x, y, c = _place()
    sibling = (x, y, 1 - c)
    chips = _other_chips(x, y)
    myq = 2 * x + y

    def copy(w, k, q, hc, to, src=None):
        rows = ins[w].shape[0]
        dst = outs[w].at[q, pl.ds(hc * (rows // 2), rows // 2)]
        return pltpu.make_async_remote_copy(
            src_ref=dst if src is None else src, dst_ref=dst,
            send_sem=send_sems.at[w, k], recv_sem=recv_sems.at[w, k], device_id=to, device_id_type=MESH)

    def own(w):
        return pltpu.make_async_copy(ins[w], outs[w].at[myq], local_sems.at[w])

    def send():
        for w in range(len(ins)):
            rows = ins[w].shape[0]
            src = ins[w].at[pl.ds(c * (rows // 2), rows // 2)]
            for j, chip in enumerate(chips):
                copy(w, j, myq, c, (*chip, c), src=src).start()
            if local_sems is not None:
                own(w).start()

    def forward():
        for w in range(len(ins)):
            for j, (qx, qy) in enumerate(chips):
                copy(w, j, 2 * qx + qy, c, sibling).wait_recv()
                copy(w, 3 + j, 2 * qx + qy, c, sibling).start()

    def finish():
        for w in range(len(ins)):
            for j, (qx, qy) in enumerate(chips):
                copy(w, 3 + j, 2 * qx + qy, 1 - c, sibling).wait_recv()
        for w in range(len(ins)):
            for k, (qx, qy) in enumerate(chips + chips):
                copy(w, k, 2 * qx + qy, c, sibling).wait_send()
            if local_sems is not None:
                own(w).wait()

    return send, forward, finish


def _mixer_fwd(x, mod, mods, norm1_g, w_in4, b_in, ln_g, ln_b, wm, bst, conv_w, conv_b, gn_g, gn_b, ga, gb, w_out,
               ffn_shards):
    s = x.shape[0]
    ts = min(FWD_ROW_TILE, s)
    nt = s // ts
    n_sh = len(ffn_shards)

    def body(x_ref, mod_ref, modb_ref, g1_ref, w4_ref, bin_ref, lng_ref, lnb_ref, wm_ref, bst_ref, cw_ref, cb_ref,
             gng_ref, gnb_ref, ga_ref, gb_ref, wout_ref, *rest):
        shard_refs, rest = rest[:n_sh], rest[n_sh:]
        z_ref, x1_ref, yb1_ref, y_ref, h_ref = rest[:5]
        full_refs, rest = rest[5:5 + n_sh], rest[5 + n_sh:]
        e_ref, sh_ref, mix_ref, send_sems, recv_sems, local_sems = rest
        i = pl.program_id(0)
        send, forward, finish = _gather_plan(shard_refs, full_refs, send_sems, recv_sems, local_sems)

        @pl.when(i == 0)
        def _():
            send()
            e_ref[0:HALO, :] = jnp.zeros((HALO, DB), F32)

        @pl.when(i == (3 * nt) // 4)
        def _():
            forward()

        shift1, scale1, gate1 = _mod_rows(mod_ref, modb_ref, 0, 3)
        x_t = x_ref[...]
        r1 = lax.rsqrt(_rowsum(x_t * x_t) * (1.0 / D) + EPS)
        h = (x_t * r1 * g1_ref[...]) * (1.0 + scale1) + shift1
        hb = h.astype(MM_DTYPE)
        h_ref[...] = hb
        z_parts = []
        for q in range(NQ):
            zq = _dot(hb, w4_ref[q]) + bin_ref[:, q * PW_IN:(q + 1) * PW_IN]
            z_ref[:, q * PW_IN:(q + 1) * PW_IN] = zq
            z_parts.append(zq)
        r = _mixer_recompute(z_parts, lng_ref[...], lnb_ref[...], wm_ref, bst_ref, mix_ref)
        y_a = r["gu"] * r["mixed"]
        e_ref[HALO:HALO + ts, :] = r["yb0"]
        _shifted_copies(e_ref, sh_ref)
        _conv_taps(e_ref, sh_ref, cw_ref, yb1_ref, ts, HALO - (CONV_W - 1), False, bias_ref=cb_ref)
        e_ref[0:HALO, :] = e_ref[ts:ts + HALO, :]
        t = _conv_branch_tail(yb1_ref[...], gng_ref[...], gnb_ref[...])
        ra = lax.rsqrt(_rowsum(y_a * y_a) * (1.0 / DA) + EPS)
        rb = lax.rsqrt(_rowsum(t["y_b"] * t["y_b"]) * (1.0 / DB) + EPS)
        yan = (y_a * ra * ga_ref[...]).astype(MM_DTYPE)
        ybn = (t["y_b"] * rb * gb_ref[...]).astype(MM_DTYPE)
        y_ref[:, 0:DA] = yan
        y_ref[:, DA:D] = ybn
        o1 = _dot(yan, wout_ref[0:DA, :]) + _dot(ybn, wout_ref[DA:D, :])
        x1_ref[...] = x_t + gate1 * o1

        @pl.when(i == nt - 1)
        def _():
            finish()

    row = lambda w: pl.BlockSpec((ts, w), lambda i: (i, 0))
    full = lambda a: pl.BlockSpec(a.shape, lambda i: (0,) * a.ndim)
    return pl.pallas_call(
        body, name="mixer_fwd", grid=(nt,),
        in_specs=[row(D), full(mod), full(mods), full(norm1_g), _whole_vmem(), full(b_in), full(ln_g), full(ln_b),
                  _whole_vmem(), full(bst), full(conv_w), full(conv_b), full(gn_g), full(gn_b), full(ga), full(gb),
                  _whole_vmem()] + [_any()] * n_sh,
        out_specs=[row(4 * PW_IN), row(D), row(DB), row(D), row(D)] + [_any()] * n_sh,
        out_shape=[SDS((s, 4 * PW_IN), F32), SDS((s, D), F32), SDS((s, DB), F32), SDS((s, D), MM_DTYPE),
                   SDS((s, D), MM_DTYPE)] + [SDS((NQ,) + w.shape, w.dtype) for w in ffn_shards],
        scratch_shapes=[pltpu.VMEM((ts + HALO, DB), F32), pltpu.VMEM((7, ts + HALO - 8, DB), F32), pltpu.VMEM((ts, DA), F32),
                        pltpu.SemaphoreType.DMA((n_sh, 6)), pltpu.SemaphoreType.DMA((n_sh, 6)),
                        pltpu.SemaphoreType.DMA((n_sh,))],
        compiler_params=_params("arbitrary"),
    )(x, mod, mods, norm1_g, w_in4, b_in, ln_g, ln_b, wm, bst, conv_w, conv_b, gn_g, gn_b, ga, gb, w_out, *ffn_shards)


def _ffn_fwd(x1, target, mod, mods, norm2_g, norm_f_g, w_ffn_in4, w_ffn_out):
    s = x1.shape[0]
    sub_rows = min(FFN_ROW_TILE, s)
    ts = min(2 * sub_rows, s)
    nt = s // ts

    def body(x1_ref, tgt_ref, mod_ref, modb_ref, g2_ref, gf_ref, wf_ref, wo_ref,
             g_ref, up_ref, a_ref, h2_ref, dx2_ref, acc_ref):
        i = pl.program_id(0)

        @pl.when(i == 0)
        def _():
            acc_ref[...] = jnp.zeros(acc_ref.shape, F32)

        shift2, scale2, gate2, shift_f, scale_f = _mod_rows(mod_ref, modb_ref, 3, 5)
        for sub in range(ts // sub_rows):
            rows = slice(sub * sub_rows, (sub + 1) * sub_rows)
            x1_t = x1_ref[rows, :]
            r2 = lax.rsqrt(_rowsum(x1_t * x1_t) * (1.0 / D) + EPS)
            h2 = (x1_t * r2 * g2_ref[...]) * (1.0 + scale2) + shift2
            h2b = h2.astype(MM_DTYPE)
            h2_ref[rows, :] = h2b
            o2 = jnp.zeros((sub_rows, D), F32)
            for p in range(2):
                g = _dot(h2b, wf_ref[p])
                up = _dot(h2b, wf_ref[2 + p])
                g_ref[rows, p * PW_FF:(p + 1) * PW_FF] = g.astype(MM_DTYPE)
                up_ref[rows, p * PW_FF:(p + 1) * PW_FF] = up.astype(MM_DTYPE)
                a = (g * _sigmoid(g) * up).astype(MM_DTYPE)
                a_ref[rows, p * PW_FF:(p + 1) * PW_FF] = a
                o2 = o2 + _dot(a, wo_ref[p * PW_FF:(p + 1) * PW_FF, :])
            x2 = x1_t + gate2 * o2
            rf = lax.rsqrt(_rowsum(x2 * x2) * (1.0 / D) + EPS)
            gf = gf_ref[...]
            nf = x2 * rf * gf
            err = nf * (1.0 + scale_f) + shift_f - tgt_ref[rows, :]
            d_out = err * (1.0 / D)
            d_nf = d_out * (1.0 + scale_f)
            t = d_nf * gf
            dx2_ref[rows, :] = rf * t - x2 * (rf * rf * rf) * (_rowsum(t * x2) * (1.0 / D))
            acc_ref[0:1, :] += _colsum(d_out)
            acc_ref[1:2, :] += _colsum(d_out * nf)
            acc_ref[2:3, :] += _colsum(d_nf * x2 * rf)
            acc_ref[3:4, :] += _colsum(err * err)

        @pl.when(i == nt - 1)
        def _():
            acc_ref[4:5, :] = jnp.zeros((1, D), F32) + _rowsum(acc_ref[3:4, :]) * (0.5 / D)

    row = lambda w: pl.BlockSpec((ts, w), lambda i: (i, 0))
    full = lambda a: pl.BlockSpec(a.shape, lambda i: (0,) * a.ndim)
    return pl.pallas_call(
        body, name="ffn_fwd", grid=(nt,),
        in_specs=[row(D), row(D), full(mod), full(mods), full(norm2_g), full(norm_f_g), _whole_vmem(), _whole_vmem()],
        out_specs=[row(DFF), row(DFF), row(DFF), row(D), row(D), pl.BlockSpec((8, D), lambda i: (0, 0))],
        out_shape=[SDS((s, DFF), MM_DTYPE), SDS((s, DFF), MM_DTYPE), SDS((s, DFF), MM_DTYPE), SDS((s, D), MM_DTYPE),
                   SDS((s, D), F32), SDS((8, D), F32)],
        compiler_params=_params("arbitrary"),
    )(x1, target, mod, mods, norm2_g, norm_f_g, w_ffn_in4, w_ffn_out)


def _ffn_bwd(dx2, x1, g, up, mod, mods, norm2_g, w_ffn_in4, w_ffn_out):
    s = x1.shape[0]
    sub_rows = min(FFN_ROW_TILE, s)
    ts = min(2 * sub_rows, s)
    nt = s // ts

    def body(dx2_ref, x1_ref, g_ref, up_ref, mod_ref, modb_ref, g2_ref, wf_ref, wo_ref,
             dff_ref, dx1_ref, acc_ref):
        @pl.when(pl.program_id(0) == 0)
        def _():
            acc_ref[...] = jnp.zeros(acc_ref.shape, F32)

        shift2, scale2, gate2 = _mod_rows(mod_ref, modb_ref, 3, 3)
        for sub in range(ts // sub_rows):
            rows = slice(sub * sub_rows, (sub + 1) * sub_rows)
            dx2_t = dx2_ref[rows, :]
            do2 = (dx2_t * gate2).astype(MM_DTYPE)
            dh2 = jnp.zeros((sub_rows, D), F32)
            for p in range(2):
                da = _dot_nt(do2, wo_ref[p * PW_FF:(p + 1) * PW_FF, :])
                gp = g_ref[rows, p * PW_FF:(p + 1) * PW_FF].astype(F32)
                upp = up_ref[rows, p * PW_FF:(p + 1) * PW_FF].astype(F32)
                sg = _sigmoid(gp)
                silu = gp * sg
                dg = (da * upp * (sg * (1.0 + gp * (1.0 - sg)))).astype(MM_DTYPE)
                dup = (da * silu).astype(MM_DTYPE)
                dff_ref[rows, p * PW_FF:(p + 1) * PW_FF] = dg
                dff_ref[rows, DFF + p * PW_FF:DFF + (p + 1) * PW_FF] = dup
                dh2 = dh2 + _dot_nt(dg, wf_ref[p]) + _dot_nt(dup, wf_ref[2 + p])
            x1_t = x1_ref[rows, :]
            r2 = lax.rsqrt(_rowsum(x1_t * x1_t) * (1.0 / D) + EPS)
            g2 = g2_ref[...]
            xr = x1_t * r2
            dn2 = dh2 * (1.0 + scale2)
            t = dn2 * g2
            dx1_ref[rows, :] = dx2_t + r2 * t - x1_t * (r2 * r2 * r2) * (_rowsum(t * x1_t) * (1.0 / D))
            acc_ref[0:1, :] += _colsum(dh2)
            acc_ref[1:2, :] += _colsum(dh2 * (xr * g2))
            acc_ref[2:3, :] += _colsum(dn2 * xr)

    row = lambda w: pl.BlockSpec((ts, w), lambda i: (i, 0))
    full = lambda a: pl.BlockSpec(a.shape, lambda i: (0,) * a.ndim)
    return pl.pallas_call(
        body, name="ffn_bwd", grid=(nt,),
        in_specs=[row(D), row(D), row(DFF), row(DFF), full(mod), full(mods), full(norm2_g), _whole_vmem(), _whole_vmem()],
        out_specs=[row(2 * DFF), row(D), pl.BlockSpec((8, D), lambda i: (0, 0))],
        out_shape=[SDS((s, 2 * DFF), MM_DTYPE), SDS((s, D), F32), SDS((8, D), F32)],
        compiler_params=_params("arbitrary"),
    )(dx2, x1, g, up, mod, mods, norm2_g, w_ffn_in4, w_ffn_out)


def _scatter_plan(ins, outs, send_sems, recv_sems, local_sems=None):
    x, y, c = _place()
    me = 4 * x + 2 * y + c

    def copies():
        cps = []
        for w in range(len(ins)):
            h = ins[w].shape[1] // 2
            for k in range(1, 8):
                px, py, pc = (1 - x if k & 4 else x), (1 - y if k & 2 else y), (1 - c if k & 1 else c)
                cps.append(pltpu.make_async_remote_copy(
                    src_ref=ins[w].at[2 * px + py, pl.ds(pc * h, h)], dst_ref=outs[w].at[me],
                    send_sem=send_sems.at[w, k - 1], recv_sem=recv_sems.at[w, k - 1],
                    device_id=(px, py, pc), device_id_type=MESH))
        return cps

    def own():
        if local_sems is None:
            return []
        return [pltpu.make_async_copy(ins[w].at[2 * x + y, pl.ds(c * (ins[w].shape[1] // 2), ins[w].shape[1] // 2)],
                                      outs[w].at[me], local_sems.at[w]) for w in range(len(ins))]

    def send():
        for cp in copies() + own():
            cp.start()

    def finish():
        for cp in copies() + own():
            cp.wait()

    return send, finish


def _scattered_shape(partial):
    nq, rows, cols = partial.shape
    return SDS((8, rows // 2, cols), partial.dtype)


def _mixer_bwd(dx1, x, z, yb1, mod, mods, norm1_g, w_in4, ln_g, ln_b, wm, wmt, bst, conv_w, gn_g, gn_b, ga, gb, w_out,
               partials):
    s = x.shape[0]
    ts = min(ROW_TILE, s)
    nt = s // ts
    n_cs = len(partials)

    def body(dx1_ref, x_ref, z_ref, yb1_ref, mod_ref, modb_ref, g1_ref, w4_ref, lng_ref, lnb_ref, wm_ref, wmt_ref,
             bst_ref, cw_ref, gng_ref, gnb_ref, ga_ref, gb_ref, wout_ref, *rest):
        cs_refs, rest = rest[:n_cs], rest[n_cs:]
        gx_ref, dz_ref, a1_ref, a2_ref, a5_ref, acw_ref, asw_ref, asb_ref = rest[:8]
        arrived_refs, rest = rest[8:8 + n_cs], rest[8 + n_cs:]
        e_ref, sh_ref, mix_ref, dvl_ref, send_sems, recv_sems, local_sems = rest
        i = pl.program_id(0)
        send, finish = _scatter_plan(cs_refs, arrived_refs, send_sems, recv_sems, local_sems)

        @pl.when(i == 0)
        def _():
            send()
            e_ref[ts:ts + HALO, :] = jnp.zeros((HALO, DB), F32)
            for r in (a1_ref, a2_ref, a5_ref, acw_ref, asw_ref, asb_ref):
                r[...] = jnp.zeros(r.shape, F32)

        shift1, scale1, gate1 = _mod_rows(mod_ref, modb_ref, 0, 3)
        dx1_t = dx1_ref[...]
        do1 = (dx1_t * gate1).astype(MM_DTYPE)
        d_yan = _dot_nt(do1, wout_ref[0:DA, :])
        d_ybn = _dot_nt(do1, wout_ref[DA:D, :])

        z_parts = [z_ref[:, q * PW_IN:(q + 1) * PW_IN] for q in range(NQ)]
        u, v, val, gate = z_parts
        lng = lng_ref[...]
        r = _mixer_recompute(z_parts, lng, lnb_ref[...], wm_ref, bst_ref, mix_ref)
        gng = gng_ref[...]
        t = _conv_branch_tail(yb1_ref[...], gng, gnb_ref[...])
        y_a = r["gu"] * r["mixed"]
        y_b = t["y_b"]
        ga_v, gb_v = ga_ref[...], gb_ref[...]
        ra = lax.rsqrt(_rowsum(y_a * y_a) * (1.0 / DA) + EPS)
        rb = lax.rsqrt(_rowsum(y_b * y_b) * (1.0 / DB) + EPS)

        a5_ref[0:1, :] += _colsum(d_yan * y_a * ra)
        a5_ref[1:2, :] += _colsum(d_ybn * y_b * rb)
        ta = d_yan * ga_v
        d_ya = ra * ta - y_a * (ra * ra * ra) * (_rowsum(ta * y_a) * (1.0 / DA))
        tb = d_ybn * gb_v
        d_yb = rb * tb - y_b * (rb * rb * rb) * (_rowsum(tb * y_b) * (1.0 / DB))

        d_u = d_ya * r["mixed"] * r["dgu"]
        d_mixed = d_ya * r["gu"]
        dmb = d_mixed.astype(MM_DTYPE)
        lo_mask = lax.broadcasted_iota(jnp.int32, (CHUNK, LANES), 1) < HALF
        zero_blk = jnp.zeros((CHUNK, LANES), MM_DTYPE)
        sb_acc = jnp.zeros((CHUNK, DA), F32)
        for ck in range(ts // CHUNK):
            rows = slice(ck * CHUNK, (ck + 1) * CHUNK)
            sb_acc = sb_acc + d_mixed[rows, :]
            for jb in range(DA // LANES):
                cols = slice(jb * LANES, (jb + 1) * LANES)
                dm_blk = dmb[rows, cols]
                vl_blk = r["vlb"][rows, cols]
                da_ = _dot(wmt_ref[2 * jb], dm_blk)
                db_ = _dot(wmt_ref[2 * jb + 1], dm_blk)
                dvl_ref[rows, cols] = jnp.where(lo_mask, da_, db_)
                asw_ref[2 * jb] += _dot_nt(jnp.where(lo_mask, dm_blk, zero_blk), vl_blk)
                asw_ref[2 * jb + 1] += _dot_nt(jnp.where(lo_mask, zero_blk, dm_blk), vl_blk)
        asb_ref[...] += sb_acc
        d_vl = dvl_ref[...]
        a5_ref[2:3, :] += _colsum(d_vl * r["vhat"])
        a5_ref[3:4, :] += _colsum(d_vl)
        dvh = d_vl * lng
        d_gv = r["rs"] * (dvh - _rowsum(dvh) * (1.0 / DA) - r["vhat"] * (_rowsum(dvh * r["vhat"]) * (1.0 / DA)))
        d_v = d_gv * r["dgv"]

        yb2, s2 = t["yb2"], t["s2"]
        d_yb2 = d_yb * (s2 * (1.0 + yb2 * (1.0 - s2)))
        a5_ref[4:5, :] += _colsum(d_yb2 * t["ghat"])
        a5_ref[5:6, :] += _colsum(d_yb2)
        dgh = d_yb2 * gng
        d_yb1 = t["grs"] * (dgh - _group_sum(dgh) * (1.0 / HALF) - t["ghat"] * (_group_sum(dgh * t["ghat"]) * (1.0 / HALF)))
        a5_ref[6:7, :] += _colsum(d_yb1)
        e_ref[0:ts, :] = d_yb1
        _shifted_copies(e_ref, sh_ref)
        mix_ref[...] = r["yb0"]
        _conv_taps(e_ref, sh_ref, cw_ref, dvl_ref, ts, 0, True, other_ref=mix_ref, tap_acc_ref=acw_ref)
        d_yb0 = dvl_ref[...]
        e_ref[ts:ts + HALO, :] = e_ref[0:HALO, :]
        sg = r["sg"]
        d_val = d_yb0 * sg
        d_gate = d_yb0 * val * sg * (1.0 - sg)

        dh = jnp.zeros((ts, D), F32)
        for q, dzq in enumerate((d_u, d_v, d_val, d_gate)):
            a2_ref[0:1, q * PW_IN:(q + 1) * PW_IN] += _colsum(dzq)
            dzb = dzq.astype(MM_DTYPE)
            dz_ref[:, q * PW_IN:(q + 1) * PW_IN] = dzb
            dh = dh + _dot_nt(dzb, w4_ref[q])
        x_t = x_ref[...]
        r1 = lax.rsqrt(_rowsum(x_t * x_t) * (1.0 / D) + EPS)
        g1 = g1_ref[...]
        xr = x_t * r1
        n1 = xr * g1
        dn1 = dh * (1.0 + scale1)
        t1 = dn1 * g1
        gx_ref[...] = dx1_t + r1 * t1 - x_t * (r1 * r1 * r1) * (_rowsum(t1 * x_t) * (1.0 / D))
        a1_ref[0:1, :] += _colsum(dh)
        a1_ref[1:2, :] += _colsum(dh * n1)
        a1_ref[2:3, :] += _colsum(dn1 * xr)

        @pl.when(i == nt - 1)
        def _():
            asb_ref[...] = _group_sum(asb_ref[...])
            finish()

    row = lambda w: pl.BlockSpec((ts, w), lambda i: (nt - 1 - i, 0))
    full = lambda a: pl.BlockSpec(a.shape, lambda i: (0,) * a.ndim)
    keep = lambda shape: pl.BlockSpec(shape, lambda i: (0,) * len(shape))
    return pl.pallas_call(
        body, name="mixer_bwd", grid=(nt,),
        in_specs=[row(D), row(D), row(4 * PW_IN), row(DB), full(mod), full(mods), full(norm1_g), _whole_vmem(),
                  full(ln_g), full(ln_b), _whole_vmem(), _whole_vmem(), full(bst), full(conv_w), full(gn_g), full(gn_b),
                  full(ga), full(gb), _whole_vmem()] + [_any()] * n_cs,
        out_specs=[row(D), row(4 * PW_IN), keep((8, D)), keep((8, 4 * PW_IN)), keep((8, DA)),
                   keep((HALO, DB)), keep((N_HEADS, CHUNK, CHUNK)), keep((CHUNK, DA))] + [_any()] * n_cs,
        out_shape=[SDS((s, D), F32), SDS((s, 4 * PW_IN), MM_DTYPE), SDS((8, D), F32),
                   SDS((8, 4 * PW_IN), F32), SDS((8, DA), F32), SDS((HALO, DB), F32),
                   SDS((N_HEADS, CHUNK, CHUNK), F32), SDS((CHUNK, DA), F32)]
        + [_scattered_shape(p) for p in partials],
        scratch_shapes=[pltpu.VMEM((ts + HALO, DB), F32), pltpu.VMEM((7, ts + HALO - 8, DB), F32),
                        pltpu.VMEM((ts, DA), F32), pltpu.VMEM((ts, DA), F32),
                        pltpu.SemaphoreType.DMA((n_cs, 7)), pltpu.SemaphoreType.DMA((n_cs, 7)),
                        pltpu.SemaphoreType.DMA((n_cs,))],
        compiler_params=_params("arbitrary"),
    )(dx1, x, z, yb1, mod, mods, norm1_g, w_in4, ln_g, ln_b, wm, wmt, bst, conv_w, gn_g, gn_b, ga, gb, w_out, *partials)


def _gather8_plan(x_ref, out_ref, send_sems, recv_sems, local_sem):
    x, y, c = _place()
    me, sibling = (x, y, c), (x, y, 1 - c)
    chips = _other_chips(x, y)

    def copy(k, block, to, src=None):
        dst = out_ref.at[4 * block[0] + 2 * block[1] + block[2]]
        return pltpu.make_async_remote_copy(src_ref=dst if src is None else src, dst_ref=dst, send_sem=send_sems.at[k],
                                            recv_sem=recv_sems.at[k], device_id=to, device_id_type=MESH)

    def own():
        return pltpu.make_async_copy(x_ref, out_ref.at[4 * x + 2 * y + c], local_sem)

    def send():
        own().start()
        copy(0, me, sibling, src=x_ref).start()
        for j, chip in enumerate(chips):
            copy(1 + j, me, (*chip, c), src=x_ref).start()

    def forward():
        for j, chip in enumerate(chips):
            copy(1 + j, (*chip, c), me).wait_recv()
            copy(4 + j, (*chip, c), sibling).start()

    def finish():
        copy(0, sibling, me).wait_recv()
        for j, chip in enumerate(chips):
            copy(4 + j, (*chip, 1 - c), me).wait_recv()
        for k in range(7):
            copy(k, me, sibling).wait_send()
        own().wait()

    return send, forward, finish


def _grad_matmul(name, a, b, ka_tile, nb_tile, piece_w=None, gated=None, gather_blk=None):
    s, ka = a.shape
    nb = b.shape[1]
    ts = min(GRAD_ROW_TILE, s)
    nt = s // ts
    nja, njb = ka // ka_tile, nb // nb_tile
    steps = nja * njb * nt
    n_in = 2 + (2 if gated else 0) + (1 if gather_blk is not None else 0)
    n_out = 1 + (1 if gated else 0) + (1 if gather_blk is not None else 0)
    assert not (gated and njb != 1) and not (piece_w and nja != 1)

    def body(*refs):
        ins, outs, scratch = refs[:n_in], refs[n_in:n_in + n_out], refs[n_in + n_out:]
        a_ref, b_ref, o_ref, acc_ref = ins[0], ins[1], outs[0], scratch[0]
        ins = ins[2:]
        step = (pl.program_id(0) * njb + pl.program_id(1)) * nt + pl.program_id(2)
        if gather_blk is not None:
            send, forward, finish = _gather8_plan(ins[-1], outs[-1], *scratch[1:])

            @pl.when(step == 0)
            def _():
                send()

            @pl.when(step == (3 * steps) // 4)
            def _():
                forward()

        prod = _dot_tn(a_ref[...].astype(MM_DTYPE), b_ref[...].astype(MM_DTYPE))

        @pl.when(pl.program_id(2) == 0)
        def _():
            acc_ref[...] = prod

        @pl.when(pl.program_id(2) > 0)
        def _():
            acc_ref[...] += prod

        @pl.when(pl.program_id(2) == nt - 1)
        def _():
            gm = acc_ref[...]
            if gated:
                gate_ref, w_ref, dg_ref = ins[0], ins[1], outs[1]

                @pl.when(step == nt - 1)
                def _():
                    dg_ref[...] = jnp.zeros(dg_ref.shape, F32)

                dg_ref[0:1, :] += _colsum(gm * w_ref[...].astype(F32))
                gm = gm * gate_ref[...]
            if piece_w:
                for q in range(nb_tile // piece_w):
                    o_ref[q] = gm[:, q * piece_w:(q + 1) * piece_w].astype(WIRE_DTYPE)
            else:
                o_ref[...] = gm.astype(WIRE_DTYPE)

        if gather_blk is not None:
            @pl.when(step == steps - 1)
            def _():
                finish()

    in_specs = [pl.BlockSpec((ts, ka_tile), lambda ja, jb, i: (i, ja)),
                pl.BlockSpec((ts, nb_tile), lambda ja, jb, i: (i, jb))]
    operands = [a, b]
    if piece_w:
        out_shape = [SDS((nb // piece_w, ka, piece_w), WIRE_DTYPE)]
        out_specs = [pl.BlockSpec((nb_tile // piece_w, ka, piece_w), lambda ja, jb, i: (jb, 0, 0))]
    else:
        out_shape = [SDS((ka, nb), WIRE_DTYPE)]
        out_specs = [pl.BlockSpec((ka_tile, nb_tile), lambda ja, jb, i: (ja, jb))]
    scratch = [pltpu.VMEM((ka_tile, nb_tile), F32)]
    if gated:
        in_specs += [pl.BlockSpec((1, nb_tile), lambda ja, jb, i: (0, jb)),
                     pl.BlockSpec((ka_tile, nb_tile), lambda ja, jb, i: (ja, jb))]
        operands += list(gated)
        out_shape.append(SDS((8, nb), F32))
        out_specs.append(pl.BlockSpec((8, nb_tile), lambda ja, jb, i: (0, jb)))
    if gather_blk is not None:
        in_specs.append(_any())
        operands.append(gather_blk)
        out_shape.append(SDS((8,) + gather_blk.shape, gather_blk.dtype))
        out_specs.append(_any())
        scratch += [pltpu.SemaphoreType.DMA((7,)), pltpu.SemaphoreType.DMA((7,)), pltpu.SemaphoreType.DMA]
    return pl.pallas_call(
        body, name=name, grid=(nja, njb, nt), in_specs=in_specs, out_specs=out_specs, out_shape=out_shape,
        scratch_shapes=scratch, compiler_params=_params("arbitrary", "arbitrary", "arbitrary"),
    )(*operands)


def _grad_w_in_scattered(h, dz, blk):
    s = h.shape[0]
    ts = min(GRAD_ROW_TILE, s)
    nt = s // ts
    parts = 2 * W_IN_SPLIT
    steps = parts * nt
    half_rows = D // 2
    part_rows = D // parts

    def body(a_ref, b_ref, blk_ref, arr_ref, all_ref, acc_ref, stage_ref, send_sems, recv_sems, g_send, g_recv, g_local):
        ja, i = pl.program_id(0), pl.program_id(1)
        step = ja * nt + i
        x, y, c = _place()
        me = 4 * x + 2 * y + c
        gsend, gforward, gfinish = _gather8_plan(blk_ref, all_ref, g_send, g_recv, g_local)

        @pl.when(step == 0)
        def _():
            gsend()

        @pl.when(step == (7 * steps) // 8)
        def _():
            gforward()

        prod = _dot_tn(a_ref[...], b_ref[...])

        @pl.when(i == 0)
        def _():
            acc_ref[...] = prod

        @pl.when(i > 0)
        def _():
            acc_ref[...] += prod

        def copies(part):
            half, r = divmod(part, W_IN_SPLIT)
            out = []
            for q in range(NQ):
                src, dst = stage_ref.at[part, q], arr_ref.at[me, pl.ds(r * part_rows, part_rows)]
                remote = pltpu.make_async_remote_copy(src_ref=src, dst_ref=dst, send_sem=send_sems.at[part, q],
                                                      recv_sem=recv_sems.at[me, r], device_id=(q // 2, q % 2, half),
                                                      device_id_type=MESH)
                local = pltpu.make_async_copy(src, dst, send_sems.at[part, q])
                out.append((remote, local, (2 * x + y == q) & (c == half)))
            return out

        for part in range(parts):
            @pl.when((ja == part) & (i == nt - 1))
            def _():
                gm = acc_ref[...]
                for q in range(NQ):
                    stage_ref[part, q] = gm[:, q * PW_IN:(q + 1) * PW_IN].astype(WIRE_DTYPE)
                for remote, local, is_self in copies(part):
                    @pl.when(is_self)
                    def _():
                        local.start()

                    @pl.when(jnp.logical_not(is_self))
                    def _():
                        remote.start()

        @pl.when(step == steps - 1)
        def _():
            gfinish()
            for d in range(8):
                for r in range(W_IN_SPLIT):
                    @pl.when(me != d)
                    def _():
                        dst = arr_ref.at[d, pl.ds(r * part_rows, part_rows)]
                        pltpu.make_async_remote_copy(src_ref=stage_ref.at[0, 0], dst_ref=dst, send_sem=send_sems.at[0, 0],
                                                     recv_sem=recv_sems.at[d, r], device_id=(x, y, c),
                                                     device_id_type=MESH).wait_recv()
            for part in range(parts):
                for remote, local, is_self in copies(part):
                    @pl.when(is_self)
                    def _():
                        local.wait()

                    @pl.when(jnp.logical_not(is_self))
                    def _():
                        remote.wait_send()

    return pl.pallas_call(
        body, name="grad_w_in", grid=(parts, nt),
        in_specs=[pl.BlockSpec((ts, part_rows), lambda ja, i: (i, ja)), pl.BlockSpec((ts, NQ * PW_IN), lambda ja, i: (i, 0)),
                  _any()],
        out_specs=[_any(), _any()],
        out_shape=[SDS((8, half_rows, PW_IN), WIRE_DTYPE), SDS((8,) + blk.shape, blk.dtype)],
        scratch_shapes=[pltpu.VMEM((part_rows, NQ * PW_IN), F32), pltpu.VMEM((parts, NQ, part_rows, PW_IN), WIRE_DTYPE),
                        pltpu.SemaphoreType.DMA((parts, NQ)), pltpu.SemaphoreType.DMA((8, W_IN_SPLIT)),
                        pltpu.SemaphoreType.DMA((7,)), pltpu.SemaphoreType.DMA((7,)), pltpu.SemaphoreType.DMA],
        compiler_params=_params("arbitrary", "arbitrary"),
    )(h, dz, blk)


COND_COLS = 512


def _cond_partial(c_all, w_a, w_f):
    na = w_a.shape[1]

    def body(c_ref, wa_ref, wf_ref, oa_ref, of_ref):
        c_t = c_ref[...]
        ca = (c_t * _sigmoid(c_t)).astype(MM_DTYPE)
        oa_ref[...] = _dot(ca, wa_ref[...].astype(MM_DTYPE))

        @pl.when(pl.program_id(0) == 0)
        def _():
            of_ref[...] = _dot(ca, wf_ref[...].astype(MM_DTYPE))

    keep = lambda shape: pl.BlockSpec(shape, lambda j: (0, 0))
    return pl.pallas_call(
        body, name="cond_partial", grid=(na // COND_COLS,),
        in_specs=[keep((8, D)), pl.BlockSpec((D, COND_COLS), lambda j: (0, j)), keep(w_f.shape)],
        out_specs=[pl.BlockSpec((8, COND_COLS), lambda j: (0, j)), keep((8, w_f.shape[1]))],
        out_shape=[SDS((8, na), F32), SDS((8, w_f.shape[1]), F32)],
        compiler_params=_params("arbitrary"),
    )(c_all, w_a, w_f)


def _cond_grad(c_all, dmod_a, dmod_f):
    na = dmod_a.shape[1]

    def body(c_ref, da_ref, df_ref, oa_ref, of_ref):
        c_t = c_ref[...]
        ca = jnp.concatenate([c_t * _sigmoid(c_t), jnp.zeros((8, D), F32)], axis=0).astype(MM_DTYPE)

        def outer(d_ref):
            dm = jnp.concatenate([d_ref[...], jnp.zeros(d_ref.shape, F32)], axis=0).astype(MM_DTYPE)
            return _dot_tn(ca, dm)

        oa_ref[...] = outer(da_ref)

        @pl.when(pl.program_id(0) == 0)
        def _():
            of_ref[...] = outer(df_ref)

    keep = lambda shape: pl.BlockSpec(shape, lambda j: (0, 0))
    return pl.pallas_call(
        body, name="cond_grad", grid=(na // COND_COLS,),
        in_specs=[keep((8, D)), pl.BlockSpec((8, COND_COLS), lambda j: (0, j)), keep(dmod_f.shape)],
        out_specs=[pl.BlockSpec((D, COND_COLS), lambda j: (0, j)), keep((D, dmod_f.shape[1]))],
        out_shape=[SDS((D, na), F32), SDS((D, dmod_f.shape[1]), F32)],
        compiler_params=_params("arbitrary"),
    )(c_all, dmod_a, dmod_f)


def _row_tile(rows, cap=256):
    if rows <= cap:
        return rows
    for t in range(cap, 7, -8):
        if rows % t == 0:
            return t
    return rows


def _ordered_sum(name, parts, into_half=None):
    n, rows, cols = parts.shape
    rt = _row_tile(rows)
    nb = rows // rt

    def body(*refs):
        p_ref, o_ref = refs[-2:]
        acc = p_ref[0].astype(F32)
        for k in range(1, n):
            acc = acc + p_ref[k].astype(F32)
        o_ref[...] = acc

    if into_half is None:
        return pl.pallas_call(
            body, name=name, grid=(nb,),
            in_specs=[pl.BlockSpec((n, rt, cols), lambda i: (0, i, 0))],
            out_specs=pl.BlockSpec((rt, cols), lambda i: (i, 0)), out_shape=SDS((rows, cols), F32),
            compiler_params=_params("parallel"),
        )(parts)
    grid_spec = pltpu.PrefetchScalarGridSpec(
        num_scalar_prefetch=1, grid=(nb,),
        in_specs=[pl.BlockSpec((n, rt, cols), lambda i, c_ref: (0, i, 0))],
        out_specs=pl.BlockSpec((rt, cols), lambda i, c_ref: (c_ref[0] * nb + i, 0)))
    return pl.pallas_call(
        body, name=name, grid_spec=grid_spec, out_shape=SDS((2 * rows, cols), F32),
        compiler_params=_params("parallel"),
    )(into_half.astype(jnp.int32).reshape(1), parts)


def _adamw_update(w_ref, g_ref, m_ref, v_ref, d_ref, nm_ref, nv_ref):
    c1 = 1.0 - ADAM_B1 ** ADAM_STEP
    c2 = 1.0 - ADAM_B2 ** ADAM_STEP
    g_t = g_ref[...]
    m_new = ADAM_B1 * m_ref[...] + (1.0 - ADAM_B1) * g_t
    v_new = ADAM_B2 * v_ref[...] + (1.0 - ADAM_B2) * (g_t * g_t)
    nm_ref[...] = m_new
    nv_ref[...] = v_new
    d_ref[...] = -ADAM_LR * ((m_new / c1) / (jnp.sqrt(v_new / c2) + ADAM_EPS) + ADAM_WD * w_ref[...])


def _adamw_many(name, ws, gs, ms, vs):
    n = len(ws)

    def body(*refs):
        ins, outs = refs[:4 * n], refs[4 * n:]
        for k in range(n):
            _adamw_update(ins[k], ins[n + k], ins[2 * n + k], ins[3 * n + k], *outs[3 * k:3 * k + 3])

    return pl.pallas_call(
        body, name=name, out_shape=[SDS(w.shape, F32) for w in ws for _ in range(3)],
        compiler_params=pltpu.CompilerParams(vmem_limit_bytes=VMEM_LIMIT),
    )(*ws, *gs, *ms, *vs)


def _adamw_tiled(name, ws, gs, ms, vs):
    n = len(ws)

    def body(*refs):
        ins, outs = refs[:4 * n], refs[4 * n:]
        for k in range(n):
            _adamw_update(ins[k], ins[n + k], ins[2 * n + k], ins[3 * n + k], *outs[3 * k:3 * k + 3])

    specs = [pl.BlockSpec((w.shape[0] // ADAMW_STEPS, w.shape[1]), lambda i: (i, 0)) for w in ws]
    return pl.pallas_call(
        body, name=name, grid=(ADAMW_STEPS,), in_specs=specs * 4, out_specs=[s for s in specs for _ in range(3)],
        out_shape=[SDS(w.shape, F32) for w in ws for _ in range(3)], compiler_params=_params("parallel"),
    )(*ws, *gs, *ms, *vs)


def _place():
    return lax.axis_index("x"), lax.axis_index("y"), lax.axis_index("c")


def _other_chips(x, y):
    return [(1 - x, y), (x, 1 - y), (1 - x, 1 - y)]


def _all_gather8(name, blk):
    m, n = blk.shape

    def body(x_ref, out_ref, send_sems, recv_sems, local_sem):
        x, y, c = _place()
        me, sibling = (x, y, c), (x, y, 1 - c)
        chips = _other_chips(x, y)

        def slot(px, py, pc):
            return out_ref.at[4 * px + 2 * py + pc]

        def copy(k, block, to, src=None):
            return pltpu.make_async_remote_copy(
                src_ref=slot(*block) if src is None else src, dst_ref=slot(*block),
                send_sem=send_sems.at[k], recv_sem=recv_sems.at[k], device_id=to, device_id_type=MESH)

        mine = pltpu.make_async_copy(x_ref, slot(*me), local_sem)
        mine.start()
        first = [copy(0, me, sibling, src=x_ref)]
        first += [copy(1 + j, me, (*chip, c), src=x_ref) for j, chip in enumerate(chips)]
        for cp in first:
            cp.start()
        passed = [copy(4 + j, (*chip, c), sibling) for j, chip in enumerate(chips)]
        for j, chip in enumerate(chips):
            copy(1 + j, (*chip, c), me).wait_recv()
            passed[j].start()
        copy(0, sibling, me).wait_recv()
        for j, chip in enumerate(chips):
            copy(4 + j, (*chip, 1 - c), me).wait_recv()
        for cp in first + passed:
            cp.wait_send()
        mine.wait()

    return pl.pallas_call(
        body, name=name, out_shape=SDS((8, m, n), blk.dtype),
        in_specs=[_whole_vmem()], out_specs=_whole_vmem(),
        scratch_shapes=[pltpu.SemaphoreType.DMA((7,)), pltpu.SemaphoreType.DMA((7,)), pltpu.SemaphoreType.DMA],
        compiler_params=pltpu.CompilerParams(vmem_limit_bytes=VMEM_LIMIT),
    )(blk)


def _any():
    return pl.BlockSpec(memory_space=pl.ANY)


def _gather_weights(shards, blk):
    n = len(shards)

    def body(*refs):
        ins, outs, sems = refs[:n + 1], refs[n + 1:2 * n + 2], refs[2 * n + 2:]
        send, forward, finish = _gather_plan(ins[:n], outs[:n], sems[0], sems[1])
        send8, forward8, finish8 = _gather8_plan(ins[n], outs[n], *sems[2:])
        send8()
        send()
        forward8()
        forward()
        finish8()
        finish()

    return pl.pallas_call(
        body, name="gather_weights",
        out_shape=[SDS((NQ,) + s.shape, s.dtype) for s in shards] + [SDS((8,) + blk.shape, blk.dtype)],
        in_specs=[_any()] * (n + 1), out_specs=[_any()] * (n + 1),
        scratch_shapes=[pltpu.SemaphoreType.DMA((n, 6)), pltpu.SemaphoreType.DMA((n, 6)),
                        pltpu.SemaphoreType.DMA((7,)), pltpu.SemaphoreType.DMA((7,)), pltpu.SemaphoreType.DMA],
    )(*shards, blk)


def _own_piece(gathered, shard):
    myq = 2 * lax.axis_index("x") + lax.axis_index("y")
    return lax.dynamic_update_slice(gathered, shard[None], (myq,) + (0,) * shard.ndim)


def _scatter_to_owners(partials):
    n = len(partials)

    def body(*refs):
        send, finish = _scatter_plan(refs[:n], refs[n:2 * n], *refs[2 * n:])
        send()
        finish()

    return pl.pallas_call(
        body, name="scatter_to_owners",
        out_shape=[_scattered_shape(p) for p in partials],
        in_specs=[_any()] * n, out_specs=[_any()] * n,
        scratch_shapes=[pltpu.SemaphoreType.DMA((n, 7)), pltpu.SemaphoreType.DMA((n, 7))],
    )(*partials)


def _owner_sums(tag, arrived, partials=None):
    x, y, c = _place()
    sums = []
    for w, arr in enumerate(arrived):
        if partials is not None:
            part = partials[w]
            h = part.shape[1] // 2
            own = lax.dynamic_slice(part, (2 * x + y, c * h, 0), (1, h, part.shape[2]))
            arr = lax.dynamic_update_slice(arr, own, (4 * x + 2 * y + c, 0, 0))
        sums.append(_ordered_sum(f"owner_sum_{tag}_{w}", arr, into_half=c))
    return sums


def _join_halves(bufs):
    n = len(bufs)

    def body(*refs):
        ins, outs = refs[:n], refs[n:2 * n]
        send_sems, recv_sems = refs[2 * n:]
        x, y, c = _place()
        cps = []
        for w in range(n):
            h = ins[w].shape[0] // 2
            mine = outs[w].at[pl.ds(c * h, h)]
            cp = pltpu.make_async_remote_copy(src_ref=mine, dst_ref=mine, send_sem=send_sems.at[w],
                                              recv_sem=recv_sems.at[w], device_id=(x, y, 1 - c), device_id_type=MESH)
            cp.start()
            cps.append(cp)
        for cp in cps:
            cp.wait()

    return pl.pallas_call(
        body, name="join_halves",
        out_shape=[SDS(b.shape, b.dtype) for b in bufs],
        in_specs=[_any()] * n, out_specs=[_any()] * n, input_output_aliases={w: w for w in range(n)},
        scratch_shapes=[pltpu.SemaphoreType.DMA((n,)), pltpu.SemaphoreType.DMA((n,))],
    )(*bufs)


def _pad_rows(a, rows):
    return jnp.pad(a, ((0, rows - a.shape[0]),) + ((0, 0),) * (a.ndim - 1))


def _pack_small(dmod, g1, g2, gf, b_in, ln_g, ln_b, conv_b, gn_g, gn_b, ga, gb, sb, cw32, sw, loss_row):
    v512 = jnp.concatenate([ln_g, ln_b, conv_b, gn_g, gn_b, ga, gb, jnp.zeros((1, DA), F32)], axis=1).reshape(4, D)
    rows = [dmod.reshape(8, D), g1, g2, gf, b_in.reshape(2, D), v512, sb.reshape(1, D), cw32.reshape(16, D),
            sw.reshape(CHUNK, D), loss_row]
    packed = jnp.concatenate(rows, axis=0)
    return _pad_rows(packed, PK_ROWS)


def _unpack_small(p):
    v512 = p[PK_V512:PK_V512 + 4].reshape(1, 8 * DA)
    pieces = [v512[:, k * DA:(k + 1) * DA] for k in range(7)]
    return dict(
        dmod=p[PK_DMOD:PK_DMOD + 8].reshape(1, 8 * D), norm1_g=p[PK_G1:PK_G1 + 1], norm2_g=p[PK_G2:PK_G2 + 1],
        norm_f_g=p[PK_GF:PK_GF + 1], b_in=p[PK_BIN:PK_BIN + 2].reshape(1, 2 * D),
        a_ln_g=pieces[0], a_ln_b=pieces[1], b_conv_b=pieces[2], b_gn_g=pieces[3], b_gn_b=pieces[4],
        out_norm_a_g=pieces[5], out_norm_b_g=pieces[6],
        a_spatial_b=p[PK_SB:PK_SB + 1].reshape(N_HEADS, CHUNK),
        b_conv_w=p[PK_CW:PK_CW + 16].reshape(HALO, DB),
        a_spatial_w=p[PK_SW:PK_SW + CHUNK].reshape(N_HEADS, CHUNK, CHUNK))


def kernel(x, c, ada_w, ada_b, norm1_g, w_in, b_in, a_ln_g, a_ln_b, a_spatial_w, a_spatial_b, b_conv_w, b_conv_b, b_gn_g, b_gn_b, out_norm_a_g, out_norm_b_g, w_out, norm2_g, w_ffn_in, w_ffn_out, ada_f_w, ada_f_b, norm_f_g, loss_target, m_ada_w, m_ada_b, m_norm1_g, m_w_in, m_b_in, m_a_ln_g, m_a_ln_b, m_a_spatial_w, m_a_spatial_b, m_b_conv_w, m_b_conv_b, m_b_gn_g, m_b_gn_b, m_out_norm_a_g, m_out_norm_b_g, m_w_out, m_norm2_g, m_w_ffn_in, m_w_ffn_out, m_ada_f_w, m_ada_f_b, m_norm_f_g, v_ada_w, v_ada_b, v_norm1_g, v_w_in, v_b_in, v_a_ln_g, v_a_ln_b, v_a_spatial_w, v_a_spatial_b, v_b_conv_w, v_b_conv_b, v_b_gn_g, v_b_gn_b, v_out_norm_a_g, v_out_norm_b_g, v_w_out, v_norm2_g, v_w_ffn_in, v_w_ffn_out, v_ada_f_w, v_ada_f_b, v_norm_f_g):
    mx, my, mc = _place()
    me = 4 * mx + 2 * my + mc
    myq = 2 * mx + my
    xs = x[0]
    target = loss_target[0]
    s = xs.shape[0]
    n_ada = ada_w.shape[2]

    cw_shard = _pad_rows(b_conv_w[0], HALO)
    mix_shards = [w_in[0].astype(MM_DTYPE), w_out[0].astype(MM_DTYPE)]
    ffn_shards = [w_ffn_in[0].astype(MM_DTYPE), w_ffn_out[0].astype(MM_DTYPE)]
    w_in4, w_out4, first = _gather_weights(mix_shards, jnp.concatenate([c.reshape(8, LANES), cw_shard], axis=0))
    w_in4, w_out4 = _own_piece(w_in4, mix_shards[0]), _own_piece(w_out4, mix_shards[1])
    w_out_f = w_out4.reshape(D, D)
    c_all = first[:, 0:8, :].reshape(8, D)
    conv_w = jnp.concatenate([first[4 * (q // 2) + 2 * (q % 2), 8:8 + HALO, :] for q in range(NQ)], axis=1)
    cond_part = jnp.concatenate(_cond_partial(c_all, ada_w[0], ada_f_w), axis=1)
    cond_all = _all_gather8("gather_cond", cond_part)
    cond_q = [cond_all[4 * (q // 2) + 2 * (q % 2)] for q in range(NQ)]
    mod_all = jnp.concatenate([cq[:, :n_ada] for cq in cond_q] + [cq[:, n_ada:] for cq in cond_q], axis=1)
    mod = lax.dynamic_slice_in_dim(mod_all, me, 1, axis=0)
    mods = jnp.concatenate([ada_b, ada_f_b.reshape(1, 2 * D)], axis=1)

    causal = jnp.tril(jnp.ones((CHUNK, CHUNK), dtype=bool))
    wm_f = jnp.where(causal[None], a_spatial_w[0], 0.0)
    wm = wm_f.astype(MM_DTYPE)
    wmt = jnp.swapaxes(wm_f, 1, 2).astype(MM_DTYPE)
    bst = jnp.repeat(a_spatial_b[0].T, HALF, axis=1)

    z, x1, yb1, y, h, w_ffn_in4, w_ffn_out4 = _mixer_fwd(
        xs, mod, mods, norm1_g, w_in4, b_in, a_ln_g, a_ln_b, wm, bst, conv_w, b_conv_b, b_gn_g, b_gn_b,
        out_norm_a_g, out_norm_b_g, w_out_f, ffn_shards)
    w_ffn_out_f = w_ffn_out4.reshape(DFF, D)
    g, up, a_act, h2, dx2, acc_f = _ffn_fwd(x1, target, mod, mods, norm2_g, norm_f_g, w_ffn_in4, w_ffn_out_f)

    dff, dx1, acc_2 = _ffn_bwd(dx2, x1, g, up, mod, mods, norm2_g, w_ffn_in4, w_ffn_out_f)
    (gw_ffn_in4,) = _grad_matmul("grad_w_ffn_in", h2, dff, D, PW_FF, piece_w=PW_FF)
    modv = mod + mods
    gw_ffn_out, dgate2 = _grad_matmul("grad_w_ffn_out", a_act, dx2, PW_FF, D, gated=(modv[:, 5 * D:6 * D], w_ffn_out_f))
    gw_out, dgate1 = _grad_matmul("grad_w_out", y, dx1, D, D, gated=(modv[:, 2 * D:3 * D], w_out_f))
    early_partials = [gw_ffn_in4, gw_ffn_out.reshape(NQ, DFF // NQ, D), gw_out.reshape(NQ, D // NQ, D)]
    gx, dz, acc_1, acc_bin, acc_5, acc_cw, acc_sw, acc_sb, *early_arrived = _mixer_bwd(
        dx1, xs, z, yb1, mod, mods, norm1_g, w_in4, a_ln_g, a_ln_b, wm, wmt, bst, conv_w, b_gn_g, b_gn_b,
        out_norm_a_g, out_norm_b_g, w_out_f, early_partials)

    dmod = jnp.concatenate([acc_1[0:1], acc_1[1:2], dgate1[0:1], acc_2[0:1], acc_2[1:2], dgate2[0:1],
                            acc_f[0:1], acc_f[1:2]], axis=1)
    sw_grad = jnp.where(causal[None], acc_sw, 0.0)
    sb_grad = acc_sb[:, ::HALF].T
    packed = _pack_small(dmod, acc_1[2:3], acc_2[2:3], acc_f[2:3], acc_bin[0:1], acc_5[2:3], acc_5[3:4], acc_5[6:7],
                         acc_5[4:5], acc_5[5:6], acc_5[0:1], acc_5[1:2], sb_grad, acc_cw, sw_grad, acc_f[4:5])
    late_arrived, gathered = _grad_w_in_scattered(h, dz, packed)
    g_w_in, g_w_ffn_in, g_w_ffn_out, g_w_out = _join_halves(
        _owner_sums("late", [late_arrived]) + _owner_sums("early", early_arrived))
    summed = _ordered_sum("small_grad_sum", gathered)
    loss = summed[PK_LOSS, 0]
    small = _unpack_small(summed)
    dmod_all = gathered[:, PK_DMOD:PK_DMOD + 8, :].reshape(8, 8 * D)
    g_ada_w, g_ada_f_w = _cond_grad(c_all, lax.dynamic_slice_in_dim(dmod_all, myq * n_ada, n_ada, axis=1),
                                    lax.dynamic_slice_in_dim(dmod_all, 6 * D + myq * PW_IN, PW_IN, axis=1))

    grads = dict(
        ada_w=g_ada_w, ada_b=small["dmod"][:, :6 * D], norm1_g=small["norm1_g"], w_in=g_w_in,
        b_in=small["b_in"], a_ln_g=small["a_ln_g"], a_ln_b=small["a_ln_b"], a_spatial_w=small["a_spatial_w"],
        a_spatial_b=small["a_spatial_b"],
        b_conv_w=lax.dynamic_slice_in_dim(small["b_conv_w"], myq * LANES, LANES, axis=1)[:CONV_W],
        b_conv_b=small["b_conv_b"], b_gn_g=small["b_gn_g"], b_gn_b=small["b_gn_b"],
        out_norm_a_g=small["out_norm_a_g"], out_norm_b_g=small["out_norm_b_g"], w_out=g_w_out,
        norm2_g=small["norm2_g"], w_ffn_in=g_w_ffn_in, w_ffn_out=g_w_ffn_out, ada_f_w=g_ada_f_w,
        ada_f_b=small["dmod"][:, 6 * D:], norm_f_g=small["norm_f_g"])

    weights = dict(ada_w=ada_w, ada_b=ada_b, norm1_g=norm1_g, w_in=w_in, b_in=b_in, a_ln_g=a_ln_g, a_ln_b=a_ln_b,
                   a_spatial_w=a_spatial_w, a_spatial_b=a_spatial_b, b_conv_w=b_conv_w, b_conv_b=b_conv_b, b_gn_g=b_gn_g,
                   b_gn_b=b_gn_b, out_norm_a_g=out_norm_a_g, out_norm_b_g=out_norm_b_g, w_out=w_out, norm2_g=norm2_g,
                   w_ffn_in=w_ffn_in, w_ffn_out=w_ffn_out, ada_f_w=ada_f_w, ada_f_b=ada_f_b, norm_f_g=norm_f_g)
    m_in = dict(ada_w=m_ada_w, ada_b=m_ada_b, norm1_g=m_norm1_g, w_in=m_w_in, b_in=m_b_in, a_ln_g=m_a_ln_g, a_ln_b=m_a_ln_b,
                a_spatial_w=m_a_spatial_w, a_spatial_b=m_a_spatial_b, b_conv_w=m_b_conv_w, b_conv_b=m_b_conv_b,
                b_gn_g=m_b_gn_g, b_gn_b=m_b_gn_b, out_norm_a_g=m_out_norm_a_g, out_norm_b_g=m_out_norm_b_g, w_out=m_w_out,
                norm2_g=m_norm2_g, w_ffn_in=m_w_ffn_in, w_ffn_out=m_w_ffn_out, ada_f_w=m_ada_f_w, ada_f_b=m_ada_f_b,
                norm_f_g=m_norm_f_g)
    v_in = dict(ada_w=v_ada_w, ada_b=v_ada_b, norm1_g=v_norm1_g, w_in=v_w_in, b_in=v_b_in, a_ln_g=v_a_ln_g, a_ln_b=v_a_ln_b,
                a_spatial_w=v_a_spatial_w, a_spatial_b=v_a_spatial_b, b_conv_w=v_b_conv_w, b_conv_b=v_b_conv_b,
                b_gn_g=v_b_gn_g, b_gn_b=v_b_gn_b, out_norm_a_g=v_out_norm_a_g, out_norm_b_g=v_out_norm_b_g, w_out=v_w_out,
                norm2_g=v_norm2_g, w_ffn_in=v_w_ffn_in, w_ffn_out=v_w_ffn_out, ada_f_w=v_ada_f_w, ada_f_b=v_ada_f_b,
                norm_f_g=v_norm_f_g)
    names = list(weights)
    big = ("ada_w", "w_in", "w_out", "w_ffn_in", "w_ffn_out", "ada_f_w")

    def flat2(a):
        return a.reshape(-1, a.shape[-1])

    delta, new_m, new_v = {}, {}, {}
    for nm in big:
        grads[nm] = grads[nm].reshape(weights[nm].shape)
    big_out = _adamw_tiled("adamw_large", *[[flat2(tree[nm]) for nm in big] for tree in (weights, grads, m_in, v_in)])
    for k, nm in enumerate(big):
        shape = weights[nm].shape
        delta[nm], new_m[nm], new_v[nm] = [o.reshape(shape) for o in big_out[3 * k:3 * k + 3]]

    small_names = [nm for nm in names if nm not in big]
    for nm in small_names:
        grads[nm] = grads[nm].reshape(weights[nm].shape)
    small_out = _adamw_many("adamw_small", *[[flat2(tree[nm]) for nm in small_names] for tree in (weights, grads, m_in, v_in)])
    for k, nm in enumerate(small_names):
        shape = weights[nm].shape
        delta[nm], new_m[nm], new_v[nm] = [o.reshape(shape) for o in small_out[3 * k:3 * k + 3]]

    grad_x = gx.reshape(x.shape)
    return (loss, grad_x, *[grads[nm] for nm in names], *[delta[nm] for nm in names],
            *[new_m[nm] for nm in names], *[new_v[nm] for nm in names])
```
